```python
import jax, jax.numpy as jnp
from jax import lax
import numpy as np

D_MODEL = 1024
BATCH = 8
SEQ = 4096
DEPTH = 4

N_A = DEPTH // 2
N_B = DEPTH - N_A
HEAD_DIM = 64
MIX_WIDTH = D_MODEL
MEM_LEN = 256
MEM_HEADS = 4
MEM_WIDTH = MEM_HEADS * HEAD_DIM
MAIN_WIDTH = MIX_WIDTH - MEM_WIDTH
POOL_WINDOWS = (2, 4, 8, 16)
POOL_GROUPS = len(POOL_WINDOWS)
POOL_GROUP_DIM = MAIN_WIDTH // POOL_GROUPS
SWA_Q_HEADS = MAIN_WIDTH // HEAD_DIM
SWA_KV_HEADS = 4
SWA_GROUP = SWA_Q_HEADS // SWA_KV_HEADS
KV_WIDTH = 2 * SWA_KV_HEADS * HEAD_DIM
MEM_KV_WIDTH = 2 * MEM_WIDTH
WINDOW = 128
BLOCK = 128
D_FF = 4 * D_MODEL
EPS = 1e-6

kernel_name = "yoco_pool_swa_sink_hybrid"


def rmsnorm(x, g):
    xf = x.astype(jnp.float32)
    y = xf * lax.rsqrt(jnp.mean(xf * xf, axis=-1, keepdims=True) + EPS)
    return (y * g.astype(jnp.float32)).astype(x.dtype)


def alibi_slopes(n):
    return jnp.exp2(-8.0 * jnp.arange(1, n + 1, dtype=jnp.float32) / n)


def pool_mixer(u, pool_w, pool_scale):
    B, S, _ = u.shape
    uf = u.astype(jnp.float32).reshape(B, S, POOL_GROUPS, POOL_GROUP_DIM)
    csum = jnp.concatenate([jnp.zeros((B, 1, POOL_GROUPS, POOL_GROUP_DIM), jnp.float32),
                            jnp.cumsum(uf, axis=1)], axis=1)
    win = jnp.array(POOL_WINDOWS, jnp.int32)
    t = jnp.arange(S, dtype=jnp.int32)[:, None]
    lo = jnp.maximum(t + 1 - win[None, :], 0)
    cnt = jnp.minimum(t + 1, win[None, :]).astype(jnp.float32)
    window_sum = csum[:, 1:] - csum[:, lo, jnp.arange(POOL_GROUPS)[None, :]]
    d = (window_sum / cnt[None, :, :, None] - uf).astype(u.dtype)
    mixed = jnp.einsum('bsgc,gcd->bsgd', d, pool_w)
    return mixed.reshape(B, S, MAIN_WIDTH) * pool_scale


def swa_sink_attention(q, k, v, sinks):
    B, S = q.shape[0], q.shape[1]
    nb = S // BLOCK
    qb = q.reshape(B, nb, BLOCK, SWA_KV_HEADS, SWA_GROUP, HEAD_DIM)

    def with_prev(a):
        ab = a.reshape(B, nb, BLOCK, SWA_KV_HEADS, HEAD_DIM)
        prev = jnp.pad(ab[:, :-1], ((0, 0), (1, 0), (0, 0), (0, 0), (0, 0)))
        return jnp.concatenate([prev, ab], axis=2)

    kb, vb = with_prev(k), with_prev(v)
    s = jnp.einsum('bnqkgd,bnpkd->bnkgqp', qb, kb).astype(jnp.float32) * (HEAD_DIM ** -0.5)
    blk = jnp.arange(nb, dtype=jnp.int32)[:, None] * BLOCK
    qpos = blk + jnp.arange(BLOCK, dtype=jnp.int32)[None, :]
    kpos = blk - BLOCK + jnp.arange(2 * BLOCK, dtype=jnp.int32)[None, :]
    dist = qpos[:, :, None] - kpos[:, None, :]
    valid = (dist >= 0) & (dist < WINDOW) & (kpos[:, None, :] >= 0)
    slopes = alibi_slopes(SWA_Q_HEADS).reshape(SWA_KV_HEADS, SWA_GROUP)
    s = s - slopes[None, None, :, :, None, None] * dist.astype(jnp.float32)[None, :, None, None]
    s = jnp.where(valid[None, :, None, None], s, jnp.finfo(jnp.float32).min)
    sink = sinks.astype(jnp.float32).reshape(SWA_KV_HEADS, SWA_GROUP)[None, None, :, :, None, None]
    m = jnp.maximum(jnp.max(s, axis=-1, keepdims=True), sink)
    e = jnp.exp(s - m)
    p = e / (jnp.sum(e, axis=-1, keepdims=True) + jnp.exp(sink - m))
    o = jnp.einsum('bnkgqp,bnpkd->bnqkgd', p.astype(vb.dtype), vb)
    return o.reshape(B, S, SWA_Q_HEADS * HEAD_DIM)


def memory_attention(q, mk, mv):
    s = jnp.einsum('bshd,bmhd->bhsm', q, mk).astype(jnp.float32) * (HEAD_DIM ** -0.5)
    p = jax.nn.softmax(s, axis=-1)
    o = jnp.einsum('bhsm,bmhd->bshd', p.astype(mv.dtype), mv)
    return o.reshape(q.shape[0], q.shape[1], MEM_WIDTH)


def sq_relu_mlp(x, w_up, w_down):
    h = jax.nn.relu(x @ w_up)
    return (h * h) @ w_down


def _fwd_setup_inputs(seed: int = 0) -> dict:
    key = jax.random.key(seed)
    ks = jax.random.split(key, 20)
    f32 = jnp.float32

    def nrm(k, shape, scale):
        return jax.random.normal(k, shape, f32) * scale

    def gain(k, shape):
        return 1.0 + 0.02 * jax.random.normal(k, shape, f32)

    return {
        "x": nrm(ks[0], (BATCH, SEQ, D_MODEL), 1.0),
        "mem": nrm(ks[1], (BATCH, MEM_LEN, D_MODEL), 1.0),
        "norm_mix": gain(ks[2], (DEPTH, D_MODEL)),
        "w_in": nrm(ks[3], (DEPTH, D_MODEL, MIX_WIDTH), D_MODEL ** -0.5),
        "pool_w": nrm(ks[4], (N_A, POOL_GROUPS, POOL_GROUP_DIM, POOL_GROUP_DIM), POOL_GROUP_DIM ** -0.5),
        "pool_scale": gain(ks[5], (N_A, MAIN_WIDTH)),
        "kv_norm": gain(ks[6], (D_MODEL,)),
        "w_kv": nrm(ks[7], (D_MODEL, KV_WIDTH), D_MODEL ** -0.5),
        "k_norm": gain(ks[8], (HEAD_DIM,)),
        "q_norm": gain(ks[9], (N_B, HEAD_DIM)),
        "sinks": nrm(ks[10], (N_B, SWA_Q_HEADS), 0.5),
        "mem_norm": gain(ks[11], (DEPTH, D_MODEL)),
        "w_mem_kv": nrm(ks[12], (DEPTH, D_MODEL, MEM_KV_WIDTH), D_MODEL ** -0.5),
        "mem_q_norm": gain(ks[13], (DEPTH, HEAD_DIM)),
        "mem_k_norm": gain(ks[14], (DEPTH, HEAD_DIM)),
        "w_out": nrm(ks[15], (DEPTH, MIX_WIDTH, D_MODEL), MIX_WIDTH ** -0.5),
        "norm_mlp": gain(ks[16], (DEPTH, D_MODEL)),
        "w_up": nrm(ks[17], (DEPTH, D_MODEL, D_FF), D_MODEL ** -0.5),
        "w_down": nrm(ks[18], (DEPTH, D_FF, D_MODEL), D_FF ** -0.5),
    }


def _fwd_reference(x, mem, norm_mix, w_in, pool_w, pool_scale, kv_norm, w_kv, k_norm, q_norm, sinks,
              mem_norm, w_mem_kv, mem_q_norm, mem_k_norm, w_out, norm_mlp, w_up, w_down):
    B, S, _ = x.shape
    h = x
    k_shared = None
    v_shared = None
    for l in range(DEPTH):
        if l == N_A:
            kv = rmsnorm(h, kv_norm) @ w_kv
            k_shared = rmsnorm(kv[..., :KV_WIDTH // 2].reshape(B, S, SWA_KV_HEADS, HEAD_DIM), k_norm)
            v_shared = kv[..., KV_WIDTH // 2:].reshape(B, S, SWA_KV_HEADS, HEAD_DIM)

        proj = rmsnorm(h, norm_mix[l]) @ w_in[l]
        main, mq = proj[..., :MAIN_WIDTH], proj[..., MAIN_WIDTH:]
        if l < N_A:
            main_out = pool_mixer(main, pool_w[l], pool_scale[l])
        else:
            j = l - N_A
            q = rmsnorm(main.reshape(B, S, SWA_Q_HEADS, HEAD_DIM), q_norm[j])
            main_out = swa_sink_attention(q, k_shared, v_shared, sinks[j])

        mkv = rmsnorm(mem, mem_norm[l]) @ w_mem_kv[l]
        mk = rmsnorm(mkv[..., :MEM_WIDTH].reshape(B, MEM_LEN, MEM_HEADS, HEAD_DIM), mem_k_norm[l])
        mv = mkv[..., MEM_WIDTH:].reshape(B, MEM_LEN, MEM_HEADS, HEAD_DIM)
        mqh = rmsnorm(mq.reshape(B, S, MEM_HEADS, HEAD_DIM), mem_q_norm[l])
        mem_out = memory_attention(mqh, mk, mv)

        h = h + jnp.concatenate([main_out, mem_out], axis=-1) @ w_out[l]
        h = h + sq_relu_mlp(rmsnorm(h, norm_mlp[l]), w_up[l], w_down[l])
    return h


import jax as _jax
import jax.numpy as _jnp

TWIN_FORMAT = 'train_step'
FWD_PARAMS = ['x', 'mem', 'norm_mix', 'w_in', 'pool_w', 'pool_scale', 'kv_norm', 'w_kv', 'k_norm', 'q_norm', 'sinks', 'mem_norm', 'w_mem_kv', 'mem_q_norm', 'mem_k_norm', 'w_out', 'norm_mlp', 'w_up', 'w_down']
TWIN_WEIGHTS = ['norm_mix', 'w_in', 'pool_w', 'pool_scale', 'kv_norm', 'w_kv', 'k_norm', 'q_norm', 'sinks', 'mem_norm', 'w_mem_kv', 'mem_q_norm', 'mem_k_norm', 'w_out', 'norm_mlp', 'w_up', 'w_down']
TWIN_DIFF_INPUT = 'x'
TWIN_INPUTS = ['x', 'mem', 'norm_mix', 'w_in', 'pool_w', 'pool_scale', 'kv_norm', 'w_kv', 'k_norm', 'q_norm', 'sinks', 'mem_norm', 'w_mem_kv', 'mem_q_norm', 'mem_k_norm', 'w_out', 'norm_mlp', 'w_up', 'w_down', 'loss_target', 'm_norm_mix', 'm_w_in', 'm_pool_w', 'm_pool_scale', 'm_kv_norm', 'm_w_kv', 'm_k_norm', 'm_q_norm', 'm_sinks', 'm_mem_norm', 'm_w_mem_kv', 'm_mem_q_norm', 'm_mem_k_norm', 'm_w_out', 'm_norm_mlp', 'm_w_up', 'm_w_down', 'v_norm_mix', 'v_w_in', 'v_pool_w', 'v_pool_scale', 'v_kv_norm', 'v_w_kv', 'v_k_norm', 'v_q_norm', 'v_sinks', 'v_mem_norm', 'v_w_mem_kv', 'v_mem_q_norm', 'v_mem_k_norm', 'v_w_out', 'v_norm_mlp', 'v_w_up', 'v_w_down']
TWIN_OUTPUTS = ['loss', 'grad_x', 'grad_norm_mix', 'grad_w_in', 'grad_pool_w', 'grad_pool_scale', 'grad_kv_norm', 'grad_w_kv', 'grad_k_norm', 'grad_q_norm', 'grad_sinks', 'grad_mem_norm', 'grad_w_mem_kv', 'grad_mem_q_norm', 'grad_mem_k_norm', 'grad_w_out', 'grad_norm_mlp', 'grad_w_up', 'grad_w_down', 'delta_norm_mix', 'delta_w_in', 'delta_pool_w', 'delta_pool_scale', 'delta_kv_norm', 'delta_w_kv', 'delta_k_norm', 'delta_q_norm', 'delta_sinks', 'delta_mem_norm', 'delta_w_mem_kv', 'delta_mem_q_norm', 'delta_mem_k_norm', 'delta_w_out', 'delta_norm_mlp', 'delta_w_up', 'delta_w_down', 'new_m_norm_mix', 'new_m_w_in', 'new_m_pool_w', 'new_m_pool_scale', 'new_m_kv_norm', 'new_m_w_kv', 'new_m_k_norm', 'new_m_q_norm', 'new_m_sinks', 'new_m_mem_norm', 'new_m_w_mem_kv', 'new_m_mem_q_norm', 'new_m_mem_k_norm', 'new_m_w_out', 'new_m_norm_mlp', 'new_m_w_up', 'new_m_w_down', 'new_v_norm_mix', 'new_v_w_in', 'new_v_pool_w', 'new_v_pool_scale', 'new_v_kv_norm', 'new_v_w_kv', 'new_v_k_norm', 'new_v_q_norm', 'new_v_sinks', 'new_v_mem_norm', 'new_v_w_mem_kv', 'new_v_mem_q_norm', 'new_v_mem_k_norm', 'new_v_w_out', 'new_v_norm_mlp', 'new_v_w_up', 'new_v_w_down']
TWIN_LEAF_KINDS = {'loss': 'loss', 'grad_x': 'grad_x', 'grad_norm_mix': 'grad_w', 'grad_w_in': 'grad_w', 'grad_pool_w': 'grad_w', 'grad_pool_scale': 'grad_w', 'grad_kv_norm': 'grad_w', 'grad_w_kv': 'grad_w', 'grad_k_norm': 'grad_w', 'grad_q_norm': 'grad_w', 'grad_sinks': 'grad_w', 'grad_mem_norm': 'grad_w', 'grad_w_mem_kv': 'grad_w', 'grad_mem_q_norm': 'grad_w', 'grad_mem_k_norm': 'grad_w', 'grad_w_out': 'grad_w', 'grad_norm_mlp': 'grad_w', 'grad_w_up': 'grad_w', 'grad_w_down': 'grad_w', 'delta_norm_mix': 'delta_w', 'delta_w_in': 'delta_w', 'delta_pool_w': 'delta_w', 'delta_pool_scale': 'delta_w', 'delta_kv_norm': 'delta_w', 'delta_w_kv': 'delta_w', 'delta_k_norm': 'delta_w', 'delta_q_norm': 'delta_w', 'delta_sinks': 'delta_w', 'delta_mem_norm': 'delta_w', 'delta_w_mem_kv': 'delta_w', 'delta_mem_q_norm': 'delta_w', 'delta_mem_k_norm': 'delta_w', 'delta_w_out': 'delta_w', 'delta_norm_mlp': 'delta_w', 'delta_w_up': 'delta_w', 'delta_w_down': 'delta_w', 'new_m_norm_mix': 'new_m', 'new_m_w_in': 'new_m', 'new_m_pool_w': 'new_m', 'new_m_pool_scale': 'new_m', 'new_m_kv_norm': 'new_m', 'new_m_w_kv': 'new_m', 'new_m_k_norm': 'new_m', 'new_m_q_norm': 'new_m', 'new_m_sinks': 'new_m', 'new_m_mem_norm': 'new_m', 'new_m_w_mem_kv': 'new_m', 'new_m_mem_q_norm': 'new_m', 'new_m_mem_k_norm': 'new_m', 'new_m_w_out': 'new_m', 'new_m_norm_mlp': 'new_m', 'new_m_w_up': 'new_m', 'new_m_w_down': 'new_m', 'new_v_norm_mix': 'new_v', 'new_v_w_in': 'new_v', 'new_v_pool_w': 'new_v', 'new_v_pool_scale': 'new_v', 'new_v_kv_norm': 'new_v', 'new_v_w_kv': 'new_v', 'new_v_k_norm': 'new_v', 'new_v_q_norm': 'new_v', 'new_v_sinks': 'new_v', 'new_v_mem_norm': 'new_v', 'new_v_w_mem_kv': 'new_v', 'new_v_mem_q_norm': 'new_v', 'new_v_mem_k_norm': 'new_v', 'new_v_w_out': 'new_v', 'new_v_norm_mlp': 'new_v', 'new_v_w_up': 'new_v', 'new_v_w_down': 'new_v'}


def _forward(args):
    return _fwd_reference(*[args[k] for k in FWD_PARAMS])


def _output_shape():
    out = _jax.eval_shape(lambda: _forward(_fwd_setup_inputs(0)))
    return out.shape, out.dtype

N_MICROBATCH = 1
ADAM_LR = 0.001
ADAM_B1 = 0.9
ADAM_B2 = 0.999
ADAM_EPS = 1e-08
ADAM_WD = 0.01
ADAM_STEP = 10
PER_EXAMPLE_BATCH_AXIS = {'x': 0, 'mem': 0, 'loss_target': 0}
SHARED_INPUTS = []
_WEIGHT_DTYPES = {'norm_mix': _jnp.float32, 'w_in': _jnp.float32, 'pool_w': _jnp.float32, 'pool_scale': _jnp.float32, 'kv_norm': _jnp.float32, 'w_kv': _jnp.float32, 'k_norm': _jnp.float32, 'q_norm': _jnp.float32, 'sinks': _jnp.float32, 'mem_norm': _jnp.float32, 'w_mem_kv': _jnp.float32, 'mem_q_norm': _jnp.float32, 'mem_k_norm': _jnp.float32, 'w_out': _jnp.float32, 'norm_mlp': _jnp.float32, 'w_up': _jnp.float32, 'w_down': _jnp.float32}
MOMENT_SCALE = {'norm_mix': 1.187376e+01, 'w_in': 1.656926e+00, 'pool_w': 3.196657e+00, 'pool_scale': 2.331026e+01, 'kv_norm': 2.247844e+01, 'w_kv': 2.707780e+01, 'k_norm': 2.554763e+01, 'q_norm': 1.282140e+01, 'sinks': 4.508403e+01, 'mem_norm': 9.860889e-01, 'w_mem_kv': 1.361844e+00, 'mem_q_norm': 1.826928e+00, 'mem_k_norm': 1.813114e+00, 'w_out': 8.933544e+00, 'norm_mlp': 9.935022e+01, 'w_up': 7.396165e+00, 'w_down': 2.842323e+01}


def _to_microbatches(a, axis):
    t = _jnp.moveaxis(a, axis, 0)
    t = t.reshape((N_MICROBATCH, t.shape[0] // N_MICROBATCH) + t.shape[1:])
    return _jnp.moveaxis(t, 1, axis + 1)


def setup_inputs(seed: int = 0) -> dict:
    inp = _fwd_setup_inputs(seed)
    key = _jax.random.fold_in(_jax.random.key(seed), 7919)
    shape, _ = _output_shape()
    out = dict(inp)
    out["loss_target"] = _jax.random.normal(_jax.random.fold_in(key, 0), shape, _jnp.float32)
    for i, name in enumerate(TWIN_WEIGHTS):
        w = inp[name].astype(_jnp.float32)
        if MOMENT_SCALE is None:
            s = _jnp.sqrt(_jnp.mean(_jnp.square(w)) + 1e-30)
        else:
            s = MOMENT_SCALE[name]
        km, kv = _jax.random.split(_jax.random.fold_in(key, i + 1))
        out[name] = w
        out["m_" + name] = s * _jax.random.normal(km, w.shape, _jnp.float32)
        out["v_" + name] = (s * s) * _jax.random.uniform(kv, w.shape, _jnp.float32, 0.5, 1.5)
    if N_MICROBATCH > 1:
        for name, axis in PER_EXAMPLE_BATCH_AXIS.items():
            out[name] = _to_microbatches(out[name], axis)
    return {'x': out['x'], 'mem': out['mem'], 'norm_mix': out['norm_mix'], 'w_in': out['w_in'], 'pool_w': out['pool_w'], 'pool_scale': out['pool_scale'], 'kv_norm': out['kv_norm'], 'w_kv': out['w_kv'], 'k_norm': out['k_norm'], 'q_norm': out['q_norm'], 'sinks': out['sinks'], 'mem_norm': out['mem_norm'], 'w_mem_kv': out['w_mem_kv'], 'mem_q_norm': out['mem_q_norm'], 'mem_k_norm': out['mem_k_norm'], 'w_out': out['w_out'], 'norm_mlp': out['norm_mlp'], 'w_up': out['w_up'], 'w_down': out['w_down'], 'loss_target': out['loss_target'], 'm_norm_mix': out['m_norm_mix'], 'm_w_in': out['m_w_in'], 'm_pool_w': out['m_pool_w'], 'm_pool_scale': out['m_pool_scale'], 'm_kv_norm': out['m_kv_norm'], 'm_w_kv': out['m_w_kv'], 'm_k_norm': out['m_k_norm'], 'm_q_norm': out['m_q_norm'], 'm_sinks': out['m_sinks'], 'm_mem_norm': out['m_mem_norm'], 'm_w_mem_kv': out['m_w_mem_kv'], 'm_mem_q_norm': out['m_mem_q_norm'], 'm_mem_k_norm': out['m_mem_k_norm'], 'm_w_out': out['m_w_out'], 'm_norm_mlp': out['m_norm_mlp'], 'm_w_up': out['m_w_up'], 'm_w_down': out['m_w_down'], 'v_norm_mix': out['v_norm_mix'], 'v_w_in': out['v_w_in'], 'v_pool_w': out['v_pool_w'], 'v_pool_scale': out['v_pool_scale'], 'v_kv_norm': out['v_kv_norm'], 'v_w_kv': out['v_w_kv'], 'v_k_norm': out['v_k_norm'], 'v_q_norm': out['v_q_norm'], 'v_sinks': out['v_sinks'], 'v_mem_norm': out['v_mem_norm'], 'v_w_mem_kv': out['v_w_mem_kv'], 'v_mem_q_norm': out['v_mem_q_norm'], 'v_mem_k_norm': out['v_mem_k_norm'], 'v_w_out': out['v_w_out'], 'v_norm_mlp': out['v_norm_mlp'], 'v_w_up': out['v_w_up'], 'v_w_down': out['v_w_down']}


def _loss(weights, diff, rest, loss_target):
    with _jax.named_scope("forward"):
        args = {**rest, TWIN_DIFF_INPUT: diff, **{k: w.astype(_WEIGHT_DTYPES[k]) for k, w in weights.items()}}
        y = _forward(args)
    with _jax.named_scope("loss_head"):
        err = _jnp.square(y.astype(_jnp.float32) - loss_target)
        return 0.5 * _jnp.sum(_jnp.mean(err, axis=-1)) if err.ndim else 0.5 * err


def _adamw(w, g, m, v):
    m = ADAM_B1 * m + (1.0 - ADAM_B1) * g
    v = ADAM_B2 * v + (1.0 - ADAM_B2) * _jnp.square(g)
    m_hat = m / (1.0 - ADAM_B1 ** ADAM_STEP)
    v_hat = v / (1.0 - ADAM_B2 ** ADAM_STEP)
    delta = -ADAM_LR * (m_hat / (_jnp.sqrt(v_hat) + ADAM_EPS) + ADAM_WD * w)
    return delta, m, v


def reference(x, mem, norm_mix, w_in, pool_w, pool_scale, kv_norm, w_kv, k_norm, q_norm, sinks, mem_norm, w_mem_kv, mem_q_norm, mem_k_norm, w_out, norm_mlp, w_up, w_down, loss_target, m_norm_mix, m_w_in, m_pool_w, m_pool_scale, m_kv_norm, m_w_kv, m_k_norm, m_q_norm, m_sinks, m_mem_norm, m_w_mem_kv, m_mem_q_norm, m_mem_k_norm, m_w_out, m_norm_mlp, m_w_up, m_w_down, v_norm_mix, v_w_in, v_pool_w, v_pool_scale, v_kv_norm, v_w_kv, v_k_norm, v_q_norm, v_sinks, v_mem_norm, v_w_mem_kv, v_mem_q_norm, v_mem_k_norm, v_w_out, v_norm_mlp, v_w_up, v_w_down):
    given = dict(x=x, mem=mem, norm_mix=norm_mix, w_in=w_in, pool_w=pool_w, pool_scale=pool_scale, kv_norm=kv_norm, w_kv=w_kv, k_norm=k_norm, q_norm=q_norm, sinks=sinks, mem_norm=mem_norm, w_mem_kv=w_mem_kv, mem_q_norm=mem_q_norm, mem_k_norm=mem_k_norm, w_out=w_out, norm_mlp=norm_mlp, w_up=w_up, w_down=w_down, loss_target=loss_target, m_norm_mix=m_norm_mix, m_w_in=m_w_in, m_pool_w=m_pool_w, m_pool_scale=m_pool_scale, m_kv_norm=m_kv_norm, m_w_kv=m_w_kv, m_k_norm=m_k_norm, m_q_norm=m_q_norm, m_sinks=m_sinks, m_mem_norm=m_mem_norm, m_w_mem_kv=m_w_mem_kv, m_mem_q_norm=m_mem_q_norm, m_mem_k_norm=m_mem_k_norm, m_w_out=m_w_out, m_norm_mlp=m_norm_mlp, m_w_up=m_w_up, m_w_down=m_w_down, v_norm_mix=v_norm_mix, v_w_in=v_w_in, v_pool_w=v_pool_w, v_pool_scale=v_pool_scale, v_kv_norm=v_kv_norm, v_w_kv=v_w_kv, v_k_norm=v_k_norm, v_q_norm=v_q_norm, v_sinks=v_sinks, v_mem_norm=v_mem_norm, v_w_mem_kv=v_w_mem_kv, v_mem_q_norm=v_mem_q_norm, v_mem_k_norm=v_mem_k_norm, v_w_out=v_w_out, v_norm_mlp=v_norm_mlp, v_w_up=v_w_up, v_w_down=v_w_down)
    weights = {n: given[n] for n in TWIN_WEIGHTS}
    shared = {n: given[n] for n in SHARED_INPUTS}
    per_example = {n: given[n] for n in ['x', 'mem']}
    grad_fn = _jax.value_and_grad(_loss, argnums=(0, 1))

    def one_microbatch(ex, loss_target):
        ex = dict(ex)
        diff = ex.pop(TWIN_DIFF_INPUT)
        return grad_fn(weights, diff, {**shared, **ex}, loss_target)

    if N_MICROBATCH == 1:
        loss, (grad_w, grad_x) = one_microbatch(per_example, given["loss_target"])
    else:
        def body(carry, xs):
            loss_sum, grad_sum = carry
            l_k, (gw_k, gx_k) = one_microbatch(xs[0], xs[1])
            with _jax.named_scope("update"):
                return (loss_sum + l_k, _jax.tree.map(_jnp.add, grad_sum, gw_k)), gx_k

        init = (_jnp.zeros((), _jnp.float32), _jax.tree.map(_jnp.zeros_like, weights))
        (loss, grad_w), grad_x = _jax.lax.scan(body, init, (per_example, given["loss_target"]))
    with _jax.named_scope("update"):
        delta_w, new_m, new_v = {}, {}, {}
        for n in TWIN_WEIGHTS:
            delta_w[n], new_m[n], new_v[n] = _adamw(weights[n], grad_w[n], given["m_" + n], given["v_" + n])
    return (loss, grad_x, *[grad_w[n] for n in TWIN_WEIGHTS], *[delta_w[n] for n in TWIN_WEIGHTS],
            *[new_m[n] for n in TWIN_WEIGHTS], *[new_v[n] for n in TWIN_WEIGHTS])
```

```python
import functools

import jax
import jax.numpy as jnp
from jax import lax
from jax.experimental import pallas as pl
from jax.experimental.pallas import tpu as pltpu

F32, BF = jnp.float32, jnp.bfloat16
SDS = jax.ShapeDtypeStruct
MESH = pl.DeviceIdType.MESH
ANY = pl.BlockSpec(memory_space=pl.ANY)

EPS = 1e-6
HEAD = 64
KV_HEADS = 4
KVW = KV_HEADS * HEAD
WINDOW = 128
POOL_WINDOWS = (2, 4, 8, 16)
HALO = 16
QK_SCALE = HEAD ** -0.5
NEG = float(jnp.finfo(jnp.float32).min)
N_CHIPS = 4
LANE = 128

ADAM_LR, ADAM_B1, ADAM_B2, ADAM_EPS, ADAM_WD, ADAM_STEP = 0.001, 0.9, 0.999, 1e-08, 0.01, 10

VMEM_LIMIT_MB = 56


def _call(body, name, grid, in_specs, out_specs, out_shape, *, scratch=(), semantics=None, aliases=None,
          prefetch=0):
    params = pltpu.CompilerParams(dimension_semantics=semantics, vmem_limit_bytes=VMEM_LIMIT_MB << 20)
    if prefetch:
        spec = pltpu.PrefetchScalarGridSpec(num_scalar_prefetch=prefetch, grid=grid, in_specs=in_specs,
                                            out_specs=out_specs, scratch_shapes=list(scratch))
        return pl.pallas_call(body, name=name, grid_spec=spec, out_shape=out_shape,
                              input_output_aliases=aliases or {}, compiler_params=params)
    return pl.pallas_call(body, name=name, grid=grid, in_specs=in_specs, out_specs=out_specs, out_shape=out_shape,
                          scratch_shapes=list(scratch), input_output_aliases=aliases or {}, compiler_params=params)


def _tile(n, pref):
    return max(t for t in range(8, min(n, pref) + 1, 8) if n % t == 0)


def _dot(a, b):
    return jnp.dot(a, b, preferred_element_type=F32)


def _dot_nt(a, b):
    return lax.dot_general(a, b, (((1,), (1,)), ((), ())), preferred_element_type=F32)


def _dot_tn(a, b):
    return lax.dot_general(a, b, (((0,), (0,)), ((), ())), preferred_element_type=F32)


def _rms(x):
    r = lax.rsqrt(jnp.mean(x * x, axis=-1, keepdims=True) + EPS)
    return x * r, r


def _rms_bwd(dy, xh, r, g):
    dg = jnp.sum(dy * xh, axis=0, keepdims=True)
    dyg = dy * g
    dx = r * (dyg - xh * jnp.mean(dyg * xh, axis=-1, keepdims=True))
    return dx, dg


def _norm_mm(h, g, w, nj, tn, *, act, name, tm):
    w_arr, w_block, w_imap = w
    rows, d = h.shape

    def body(h_ref, g_ref, w_ref, y_ref, xn_ref):
        @pl.when(pl.program_id(1) == 0)
        def _():
            xh, _ = _rms(h_ref[...])
            xn_ref[...] = (xh * g_ref[...]).astype(BF)

        u = _dot(xn_ref[...], w_ref[...].reshape(d, tn))
        if act:
            a = jnp.maximum(u, 0.0)
            y_ref[...] = (a * a).astype(BF)
        else:
            y_ref[...] = u

    return _call(
        body, name, (rows // tm, nj),
        [pl.BlockSpec((tm, d), lambda i, j: (i, 0)), pl.BlockSpec((1, d), lambda i, j: (0, 0)),
         pl.BlockSpec(w_block, lambda i, j: w_imap(j))],
        [pl.BlockSpec((tm, tn), lambda i, j: (i, j)), pl.BlockSpec((tm, d), lambda i, j: (i, 0))],
        [SDS((rows, nj * tn), BF if act else F32), SDS((rows, d), BF)],
        semantics=("parallel", "arbitrary"))(h, g, w_arr)


def _mm_res(res, a, w, *, name, tm):
    w_arr, w_block, w_imap = w
    rows, k = a.shape
    n = res.shape[1]

    def body(res_ref, a_ref, w_ref, o_ref):
        o_ref[...] = res_ref[...] + _dot(a_ref[...], w_ref[...].reshape(k, n))

    return _call(
        body, name, (rows // tm,),
        [pl.BlockSpec((tm, n), lambda i: (i, 0)), pl.BlockSpec((tm, k), lambda i: (i, 0)),
         pl.BlockSpec(w_block, lambda i: w_imap(0))],
        pl.BlockSpec((tm, n), lambda i: (i, 0)), SDS((rows, n), F32), semantics=("parallel",))(res, a, w_arr)


def _mm_nt(dy, w, k, *, name, tm):
    w_arr, w_block, w_imap = w
    rows, n = dy.shape

    def body(dy_ref, w_ref, o_ref):
        o_ref[...] = _dot_nt(dy_ref[...], w_ref[...].reshape(k, n))

    return _call(
        body, name, (rows // tm,),
        [pl.BlockSpec((tm, n), lambda i: (i, 0)), pl.BlockSpec(w_block, lambda i: w_imap(0))],
        pl.BlockSpec((tm, k), lambda i: (i, 0)), SDS((rows, k), F32), semantics=("parallel",))(dy, w_arr)


def _mm_nt_relu2(dh, hh, w, nj, *, name, tm):
    w_arr, w_block, w_imap = w
    rows, d = dh.shape
    tk = hh.shape[1] // nj

    def body(dh_ref, hh_ref, w_ref, o_ref):
        dhh = _dot_nt(dh_ref[...], w_ref[...].reshape(tk, d))
        o_ref[...] = (dhh * (2.0 * jnp.sqrt(hh_ref[...].astype(F32)))).astype(BF)

    return _call(
        body, name, (rows // tm, nj),
        [pl.BlockSpec((tm, d), lambda i, j: (i, 0)), pl.BlockSpec((tm, tk), lambda i, j: (i, j)),
         pl.BlockSpec(w_block, lambda i, j: w_imap(j))],
        pl.BlockSpec((tm, tk), lambda i, j: (i, j)), SDS((rows, nj * tk), BF),
        semantics=("parallel", "parallel"))(dh, hh, w_arr)


def _mm_nt_normbwd(dy, w, nsplit, h, g, dres, *, name, tm):
    w_arr, w_block, w_imap = w
    rows, n = dy.shape
    d = h.shape[1]
    ns = n // nsplit

    def body(dy_ref, w_ref, h_ref, g_ref, dres_ref, o_ref, obf_ref, dg_ref):
        if nsplit == 1:
            dxn = _dot_nt(dy_ref[...].astype(BF), w_ref[...].reshape(d, n))
        else:
            dxn = _dot_nt(dy_ref[:, 0:ns].astype(BF), w_ref[0])
            for s in range(1, nsplit):
                dxn += _dot_nt(dy_ref[:, s * ns:(s + 1) * ns].astype(BF), w_ref[s])
        xh, r = _rms(h_ref[...])
        dx, dg = _rms_bwd(dxn, xh, r, g_ref[...])
        out = dres_ref[...] + dx
        o_ref[...] = out
        obf_ref[...] = out.astype(BF)

        @pl.when(pl.program_id(0) == 0)
        def _():
            dg_ref[...] = jnp.zeros_like(dg_ref)

        dg_ref[...] += dg

    row = lambda i: (i, 0)
    return _call(
        body, name, (rows // tm,),
        [pl.BlockSpec((tm, n), row), pl.BlockSpec(w_block, lambda i: w_imap(0)), pl.BlockSpec((tm, d), row),
         pl.BlockSpec((1, d), lambda i: (0, 0)), pl.BlockSpec((tm, d), row)],
        [pl.BlockSpec((tm, d), row), pl.BlockSpec((tm, d), row), pl.BlockSpec((1, d), lambda i: (0, 0))],
        [SDS((rows, d), F32), SDS((rows, d), BF), SDS((1, d), F32)],
        semantics=("arbitrary",))(dy, w_arr, h, g, dres)


def _mm_tn(x, dy, packed, out_imap, tk, tn, *, name):
    s_len, k = x.shape
    n = dy.shape[1]

    def body(x_ref, dy_ref, _, o_ref):
        o_ref[0] = _dot_tn(x_ref[...], dy_ref[...].astype(BF))

    return _call(
        body, name, (k // tk, n // tn),
        [pl.BlockSpec((s_len, tk), lambda i, j: (0, i)), pl.BlockSpec((s_len, tn), lambda i, j: (0, j)), ANY],
        pl.BlockSpec((1, tk, tn), out_imap), SDS(packed.shape, packed.dtype),
        semantics=("parallel", "parallel"), aliases={2: 0})(x, dy, packed)


def _loss_head(y, tgt, *, tm):
    rows, d = y.shape

    def body(y_ref, t_ref, dh_ref, dhbf_ref, loss_ref):
        err = y_ref[...] - t_ref[...]
        dh = err * (1.0 / d)
        dh_ref[...] = dh
        dhbf_ref[...] = dh.astype(BF)

        @pl.when(pl.program_id(0) == 0)
        def _():
            loss_ref[...] = jnp.zeros_like(loss_ref)

        loss_ref[...] += 0.5 * jnp.sum(jnp.mean(err * err, axis=-1, keepdims=True), axis=0, keepdims=True)

    row = lambda i: (i, 0)
    return _call(
        body, "loss_head", (rows // tm,), [pl.BlockSpec((tm, d), row), pl.BlockSpec((tm, d), row)],
        [pl.BlockSpec((tm, d), row), pl.BlockSpec((tm, d), row), pl.BlockSpec((1, 1), lambda i: (0, 0))],
        [SDS((rows, d), F32), SDS((rows, d), BF), SDS((1, 1), F32)], semantics=("arbitrary",))(y, tgt)


def _hs(h):
    return slice(HEAD * h, HEAD * (h + 1))


def _softmax_rows(s):
    e = jnp.exp(s - jnp.max(s, axis=-1, keepdims=True))
    return e / jnp.sum(e, axis=-1, keepdims=True)


def _mem_fwd(mq, mk, mv, gq):
    outs = []
    for h in range(KV_HEADS):
        xh, _ = _rms(mq[:, _hs(h)])
        qn = (xh * gq).astype(BF)
        p = _softmax_rows(_dot_nt(qn, mk[:, _hs(h)]) * QK_SCALE)
        outs.append(_dot(p.astype(BF), mv[:, _hs(h)]))
    return jnp.concatenate(outs, axis=-1)


def _mem_bwd(mq, do, mk, mv, gq):
    dqs, dks, dvs, dgq = [], [], [], 0.0
    for h in range(KV_HEADS):
        xh, r = _rms(mq[:, _hs(h)])
        qn = (xh * gq).astype(BF)
        p = _softmax_rows(_dot_nt(qn, mk[:, _hs(h)]) * QK_SCALE)
        doh = do[:, _hs(h)].astype(BF)
        dp = _dot_nt(doh, mv[:, _hs(h)])
        ds = (p * (dp - jnp.sum(p * dp, axis=-1, keepdims=True)) * QK_SCALE).astype(BF)
        dq, dg = _rms_bwd(_dot(ds, mk[:, _hs(h)]), xh, r, gq)
        dqs.append(dq)
        dgq = dgq + dg
        dks.append(_dot_tn(ds, qn))
        dvs.append(_dot_tn(p.astype(BF), doh))
    cat = lambda xs: jnp.concatenate(xs, axis=-1)
    return cat(dqs), cat(dks), cat(dvs), dgq


def _mem_kv(mkv, gk):
    ks = []
    for h in range(KV_HEADS):
        xh, _ = _rms(mkv[:, _hs(h)])
        ks.append(xh * gk)
    return jnp.concatenate(ks, axis=-1).astype(BF), mkv[:, KVW:].astype(BF)


def _mem_kv_bwd(mkv, dmk, dmv, gk):
    dxs, dgk = [], 0.0
    for h in range(KV_HEADS):
        xh, r = _rms(mkv[:, _hs(h)])
        dx, dg = _rms_bwd(dmk[:, _hs(h)], xh, r, gk)
        dxs.append(dx)
        dgk = dgk + dg
    return jnp.concatenate(dxs + [dmv], axis=-1), dgk


def _pool_select(col, gd, a2, a4, a8, a16):
    return jnp.where(col < gd, a2, jnp.where(col < 2 * gd, a4, jnp.where(col < 3 * gd, a8, a16)))


def _pool_count(t0, shape, gd):
    col = lax.broadcasted_iota(jnp.int32, shape, 1)
    t = t0 + lax.broadcasted_iota(jnp.int32, shape, 0)
    win = _pool_select(col, gd, *POOL_WINDOWS)
    return jnp.minimum(t + 1, win).astype(F32)


def _pool_diff(u, halo, t0, gd):
    c = jnp.concatenate([halo, u], axis=0)
    s2 = c + pltpu.roll(c, 1, 0)
    s4 = s2 + pltpu.roll(s2, 2, 0)
    s8 = s4 + pltpu.roll(s4, 4, 0)
    s16 = s8 + pltpu.roll(s8, 8, 0)
    col = lax.broadcasted_iota(jnp.int32, c.shape, 1)
    ws = _pool_select(col, gd, s2, s4, s8, s16)[HALO:]
    return ws / _pool_count(t0, u.shape, gd) - u


def _pool_diff_bwd(dd, dd_halo, t0, gd):
    t = dd.shape[0]
    z = jnp.concatenate([dd / _pool_count(t0, dd.shape, gd), dd_halo / _pool_count(t0 + t, dd_halo.shape, gd)], axis=0)
    n = z.shape[0]
    f2 = z + pltpu.roll(z, n - 1, 0)
    f4 = f2 + pltpu.roll(f2, n - 2, 0)
    f8 = f4 + pltpu.roll(f4, n - 4, 0)
    f16 = f8 + pltpu.roll(f8, n - 8, 0)
    col = lax.broadcasted_iota(jnp.int32, z.shape, 1)
    return _pool_select(col, gd, f2, f4, f8, f16)[:t] - dd


def _swa_bias(n):
    qi = lax.broadcasted_iota(jnp.int32, (WINDOW, 2 * WINDOW), 0)
    kj = lax.broadcasted_iota(jnp.int32, (WINDOW, 2 * WINDOW), 1)
    dist = qi + WINDOW - kj
    valid = (dist >= 0) & (dist < WINDOW) & ((kj >= WINDOW) | (n > 0))
    return dist.astype(F32), valid


def _slopes(qh):
    return [2.0 ** (-8.0 * (h + 1) / qh) for h in range(qh)]


def _swa_probs(qn, kk, dist, valid, slope, sink):
    s = _dot_nt(qn, kk) * QK_SCALE - slope * dist
    s = jnp.where(valid, s, NEG)
    m = jnp.maximum(jnp.max(s, axis=-1, keepdims=True), sink)
    e = jnp.exp(s - m)
    es = jnp.exp(sink - m)
    z = jnp.sum(e, axis=-1, keepdims=True) + es
    return e / z, es / z


def _swa_fwd(q, kk, vv, gq, sinks, n, qh):
    dist, valid = _swa_bias(n)
    slopes = _slopes(qh)
    grp = qh // KV_HEADS
    outs = []
    for h in range(qh):
        kh = h // grp
        xh, _ = _rms(q[:, _hs(h)])
        qn = (xh * gq).astype(BF)
        p, _ = _swa_probs(qn, kk[:, _hs(kh)], dist, valid, slopes[h], sinks[:, h:h + 1])
        outs.append(_dot(p.astype(BF), vv[:, _hs(kh)]))
    return jnp.concatenate(outs, axis=-1)


def _swa_bwd(q, do, kk, vv, gq, sinks, n, qh):
    dist, valid = _swa_bias(n)
    slopes = _slopes(qh)
    grp = qh // KV_HEADS
    lane = lax.broadcasted_iota(jnp.int32, (1, LANE), 1)
    dqs, dks, dvs, dgq, dsk = [], [], [], 0.0, jnp.zeros((1, LANE), F32)
    for kh in range(KV_HEADS):
        dk_h, dv_h = 0.0, 0.0
        for h in range(kh * grp, (kh + 1) * grp):
            xh, r = _rms(q[:, _hs(h)])
            qn = (xh * gq).astype(BF)
            p, ps = _swa_probs(qn, kk[:, _hs(kh)], dist, valid, slopes[h], sinks[:, h:h + 1])
            doh = do[:, _hs(h)].astype(BF)
            dp = _dot_nt(doh, vv[:, _hs(kh)])
            delta = jnp.sum(p * dp, axis=-1, keepdims=True)
            ds = (p * (dp - delta) * QK_SCALE).astype(BF)
            dsk = dsk + jnp.where(lane == h, -jnp.sum(ps * delta, axis=0, keepdims=True), 0.0)
            dq, dg = _rms_bwd(_dot(ds, kk[:, _hs(kh)]), xh, r, gq)
            dqs.append(dq)
            dgq = dgq + dg
            dk_h = dk_h + _dot_tn(ds, qn)
            dv_h = dv_h + _dot_tn(p.astype(BF), doh)
        dks.append(dk_h)
        dvs.append(dv_h)
    cat = lambda xs: jnp.concatenate(xs, axis=-1)
    return cat(dqs), cat(dks), cat(dvs), dgq, dsk


def _mixer_pool_fwd(proj, mkv, pbd, scale, gq, gk, *, name, tm):
    s_len, d = proj.shape
    main = d - KVW
    gd = main // len(POOL_WINDOWS)
    mlen = mkv.shape[0]
    hb = tm // HALO

    def body(u_ref, halo_ref, mq_ref, mkv_ref, pbd_ref, scale_ref, gq_ref, gk_ref, o_ref, mk_s, mv_s):
        i = pl.program_id(0)

        @pl.when(i == 0)
        def _():
            mk, mv = _mem_kv(mkv_ref[...], gk_ref[...])
            mk_s[...] = mk
            mv_s[...] = mv

        halo = jnp.where(i > 0, halo_ref[...], 0.0)
        dif = _pool_diff(u_ref[...], halo, i * tm, gd)
        mixed = _dot(dif.astype(BF), pbd_ref[...]) * scale_ref[...]
        mem = _mem_fwd(mq_ref[...], mk_s[...], mv_s[...], gq_ref[...])
        o_ref[...] = jnp.concatenate([mixed, mem], axis=-1).astype(BF)

    full = lambda shape: pl.BlockSpec(shape, lambda i: (0,) * len(shape))
    return _call(
        body, name, (s_len // tm,),
        [pl.BlockSpec((tm, main), lambda i: (i, 0)),
         pl.BlockSpec((HALO, main), lambda i: (jnp.maximum(i * hb - 1, 0), 0)),
         pl.BlockSpec((tm, KVW), lambda i: (i, main // KVW)),
         full((mlen, 2 * KVW)), full((main, main)), full((1, main)), full((1, HEAD)), full((1, HEAD))],
        pl.BlockSpec((tm, d), lambda i: (i, 0)), SDS((s_len, d), BF),
        scratch=[pltpu.VMEM((mlen, KVW), BF), pltpu.VMEM((mlen, KVW), BF)],
        semantics=("arbitrary",))(proj, proj, proj, mkv, pbd, scale, gq, gk)


def _mixer_pool_bwd(proj, dcat, mkv, pbd, scale, gq, gk, *, name, tm):
    s_len, d = proj.shape
    main = d - KVW
    gd = main // len(POOL_WINDOWS)
    mlen = mkv.shape[0]
    hb = tm // HALO
    nt = s_len // tm
    last_halo = s_len // HALO - 1

    def body(u_ref, halo_ref, mq_ref, do_ref, donext_ref, dom_ref, mkv_ref, pbd_ref, scale_ref, gq_ref, gk_ref,
             dproj_ref, dpbd_ref, dscale_ref, dmkv_ref, dgq_ref, dgk_ref, mk_s, mv_s, dmk_s, dmv_s):
        i = pl.program_id(0)

        @pl.when(i == 0)
        def _():
            mk, mv = _mem_kv(mkv_ref[...], gk_ref[...])
            mk_s[...] = mk
            mv_s[...] = mv
            dmk_s[...] = jnp.zeros_like(dmk_s)
            dmv_s[...] = jnp.zeros_like(dmv_s)
            dpbd_ref[...] = jnp.zeros_like(dpbd_ref)
            dscale_ref[...] = jnp.zeros_like(dscale_ref)
            dgq_ref[...] = jnp.zeros_like(dgq_ref)

        pbd = pbd_ref[...]
        scale = scale_ref[...]
        halo = jnp.where(i > 0, halo_ref[...], 0.0)
        dif = _pool_diff(u_ref[...], halo, i * tm, gd).astype(BF)
        do = do_ref[...]
        dscale_ref[...] += jnp.sum(do * _dot(dif, pbd), axis=0, keepdims=True)
        dmixed = (do * scale).astype(BF)
        dpbd_ref[...] += _dot_tn(dif, dmixed)
        dd = _dot_nt(dmixed, pbd)
        donext = jnp.where(i < nt - 1, donext_ref[...], 0.0)
        dd_halo = _dot_nt((donext * scale).astype(BF), pbd)
        du = _pool_diff_bwd(dd, dd_halo, i * tm, gd)

        dmq, dmk, dmv, dgq = _mem_bwd(mq_ref[...], dom_ref[...], mk_s[...], mv_s[...], gq_ref[...])
        dmk_s[...] += dmk
        dmv_s[...] += dmv
        dgq_ref[...] += dgq
        dproj_ref[...] = jnp.concatenate([du, dmq], axis=-1).astype(BF)

        @pl.when(i == nt - 1)
        def _():
            dmkv, dgk = _mem_kv_bwd(mkv_ref[...], dmk_s[...], dmv_s[...], gk_ref[...])
            dmkv_ref[...] = dmkv
            dgk_ref[...] = dgk

    full = lambda shape: pl.BlockSpec(shape, lambda i: (0,) * len(shape))
    return _call(
        body, name, (nt,),
        [pl.BlockSpec((tm, main), lambda i: (i, 0)),
         pl.BlockSpec((HALO, main), lambda i: (jnp.maximum(i * hb - 1, 0), 0)),
         pl.BlockSpec((tm, KVW), lambda i: (i, main // KVW)),
         pl.BlockSpec((tm, main), lambda i: (i, 0)),
         pl.BlockSpec((HALO, main), lambda i: (jnp.minimum((i + 1) * hb, last_halo), 0)),
         pl.BlockSpec((tm, KVW), lambda i: (i, main // KVW)),
         full((mlen, 2 * KVW)), full((main, main)), full((1, main)), full((1, HEAD)), full((1, HEAD))],
        [pl.BlockSpec((tm, d), lambda i: (i, 0)), full((main, main)), full((1, main)), full((mlen, 2 * KVW)),
         full((1, HEAD)), full((1, HEAD))],
        [SDS((s_len, d), BF), SDS((main, main), F32), SDS((1, main), F32), SDS((mlen, 2 * KVW), F32),
         SDS((1, HEAD), F32), SDS((1, HEAD), F32)],
        scratch=[pltpu.VMEM((mlen, KVW), BF), pltpu.VMEM((mlen, KVW), BF), pltpu.VMEM((mlen, KVW), F32),
                 pltpu.VMEM((mlen, KVW), F32)],
        semantics=("arbitrary",))(proj, proj, proj, dcat, dcat, dcat, mkv, pbd, scale, gq, gk)


def _mixer_swa_fwd(proj, kn, v, mkv, gqs, sinks, gq, gk, *, name):
    s_len, d = proj.shape
    main = d - KVW
    qh = main // HEAD
    mlen = mkv.shape[0]
    tm = WINDOW

    def body(q_ref, mq_ref, kp_ref, kc_ref, vp_ref, vc_ref, mkv_ref, gqs_ref, sinks_ref, gq_ref, gk_ref, o_ref,
             mk_s, mv_s):
        n = pl.program_id(0)

        @pl.when(n == 0)
        def _():
            mk, mv = _mem_kv(mkv_ref[...], gk_ref[...])
            mk_s[...] = mk
            mv_s[...] = mv

        kk = jnp.concatenate([kp_ref[...], kc_ref[...]], axis=0)
        vv = jnp.concatenate([vp_ref[...], vc_ref[...]], axis=0)
        att = _swa_fwd(q_ref[...], kk, vv, gqs_ref[...], sinks_ref[...], n, qh)
        mem = _mem_fwd(mq_ref[...], mk_s[...], mv_s[...], gq_ref[...])
        o_ref[...] = jnp.concatenate([att, mem], axis=-1).astype(BF)

    full = lambda shape: pl.BlockSpec(shape, lambda i: (0,) * len(shape))
    prev = lambda i: (jnp.maximum(i - 1, 0), 0)
    cur = lambda i: (i, 0)
    return _call(
        body, name, (s_len // tm,),
        [pl.BlockSpec((tm, main), cur), pl.BlockSpec((tm, KVW), lambda i: (i, main // KVW)),
         pl.BlockSpec((tm, KVW), prev), pl.BlockSpec((tm, KVW), cur),
         pl.BlockSpec((tm, KVW), prev), pl.BlockSpec((tm, KVW), cur),
         full((mlen, 2 * KVW)), full((1, HEAD)), full((1, LANE)), full((1, HEAD)), full((1, HEAD))],
        pl.BlockSpec((tm, d), cur), SDS((s_len, d), BF),
        scratch=[pltpu.VMEM((mlen, KVW), BF), pltpu.VMEM((mlen, KVW), BF)],
        semantics=("arbitrary",))(proj, proj, kn, kn, v, v, mkv, gqs, sinks, gq, gk)


def _mixer_swa_bwd(proj, dcat, kn, v, mkv, gqs, sinks, gq, gk, *, name):
    s_len, d = proj.shape
    main = d - KVW
    qh = main // HEAD
    mlen = mkv.shape[0]
    tm = WINDOW
    nt = s_len // tm

    def body(q_ref, mq_ref, do_ref, dom_ref, kp_ref, kc_ref, vp_ref, vc_ref, mkv_ref, gqs_ref, sinks_ref, gq_ref,
             gk_ref, dproj_ref, dk_ref, dv_ref, dmkv_ref, dgqs_ref, dsinks_ref, dgq_ref, dgk_ref,
             mk_s, mv_s, dmk_s, dmv_s):
        n = pl.program_id(0)

        @pl.when(n == 0)
        def _():
            mk, mv = _mem_kv(mkv_ref[...], gk_ref[...])
            mk_s[...] = mk
            mv_s[...] = mv
            dmk_s[...] = jnp.zeros_like(dmk_s)
            dmv_s[...] = jnp.zeros_like(dmv_s)
            dk_ref[...] = jnp.zeros_like(dk_ref)
            dv_ref[...] = jnp.zeros_like(dv_ref)
            dgqs_ref[...] = jnp.zeros_like(dgqs_ref)
            dsinks_ref[...] = jnp.zeros_like(dsinks_ref)
            dgq_ref[...] = jnp.zeros_like(dgq_ref)

        kk = jnp.concatenate([kp_ref[...], kc_ref[...]], axis=0)
        vv = jnp.concatenate([vp_ref[...], vc_ref[...]], axis=0)
        dq, dkk, dvv, dgqs, dsk = _swa_bwd(q_ref[...], do_ref[...], kk, vv, gqs_ref[...], sinks_ref[...], n, qh)
        prev = pl.ds(pl.multiple_of(jnp.maximum(n - 1, 0) * tm, tm), tm)
        own = pl.ds(pl.multiple_of(n * tm, tm), tm)
        dk_ref[prev, :] += dkk[:tm]
        dk_ref[own, :] += dkk[tm:]
        dv_ref[prev, :] += dvv[:tm]
        dv_ref[own, :] += dvv[tm:]
        dgqs_ref[...] += dgqs
        dsinks_ref[...] += dsk

        dmq, dmk, dmv, dgq = _mem_bwd(mq_ref[...], dom_ref[...], mk_s[...], mv_s[...], gq_ref[...])
        dmk_s[...] += dmk
        dmv_s[...] += dmv
        dgq_ref[...] += dgq
        dproj_ref[...] = jnp.concatenate([dq, dmq], axis=-1).astype(BF)

        @pl.when(n == nt - 1)
        def _():
            dmkv, dgk = _mem_kv_bwd(mkv_ref[...], dmk_s[...], dmv_s[...], gk_ref[...])
            dmkv_ref[...] = dmkv
            dgk_ref[...] = dgk

    full = lambda shape: pl.BlockSpec(shape, lambda i: (0,) * len(shape))
    prev_b = lambda i: (jnp.maximum(i - 1, 0), 0)
    cur = lambda i: (i, 0)
    memcol = lambda i: (i, main // KVW)
    return _call(
        body, name, (nt,),
        [pl.BlockSpec((tm, main), cur), pl.BlockSpec((tm, KVW), memcol),
         pl.BlockSpec((tm, main), cur), pl.BlockSpec((tm, KVW), memcol),
         pl.BlockSpec((tm, KVW), prev_b), pl.BlockSpec((tm, KVW), cur),
         pl.BlockSpec((tm, KVW), prev_b), pl.BlockSpec((tm, KVW), cur),
         full((mlen, 2 * KVW)), full((1, HEAD)), full((1, LANE)), full((1, HEAD)), full((1, HEAD))],
        [pl.BlockSpec((tm, d), cur), full((s_len, KVW)), full((s_len, KVW)), full((mlen, 2 * KVW)),
         full((1, HEAD)), full((1, LANE)), full((1, HEAD)), full((1, HEAD))],
        [SDS((s_len, d), BF), SDS((s_len, KVW), F32), SDS((s_len, KVW), F32), SDS((mlen, 2 * KVW), F32),
         SDS((1, HEAD), F32), SDS((1, LANE), F32), SDS((1, HEAD), F32), SDS((1, HEAD), F32)],
        scratch=[pltpu.VMEM((mlen, KVW), BF), pltpu.VMEM((mlen, KVW), BF), pltpu.VMEM((mlen, KVW), F32),
                 pltpu.VMEM((mlen, KVW), F32)],
        semantics=("arbitrary",))(proj, proj, dcat, dcat, kn, kn, v, v, mkv, gqs, sinks, gq, gk)


def _kv_prep(kv, gk, *, tm):
    s_len = kv.shape[0]

    def body(kv_ref, gk_ref, k_ref, v_ref):
        k, v = _mem_kv(kv_ref[...], gk_ref[...])
        k_ref[...] = k
        v_ref[...] = v

    row = lambda i: (i, 0)
    return _call(
        body, "kv_prep", (s_len // tm,),
        [pl.BlockSpec((tm, 2 * KVW), row), pl.BlockSpec((1, HEAD), lambda i: (0, 0))],
        [pl.BlockSpec((tm, KVW), row), pl.BlockSpec((tm, KVW), row)],
        [SDS((s_len, KVW), BF), SDS((s_len, KVW), BF)], semantics=("parallel",))(kv, gk)


def _kv_bwd(kv, dks, dvs, gk, *, tm):
    s_len = kv.shape[0]
    nl = len(dks)

    def body(*refs):
        kv_ref, gk_ref = refs[0], refs[1]
        dk_refs, dv_refs = refs[2:2 + nl], refs[2 + nl:2 + 2 * nl]
        dkv_ref, dgk_ref = refs[2 + 2 * nl], refs[3 + 2 * nl]
        dk, dv = dk_refs[0][...], dv_refs[0][...]
        for t in range(1, nl):
            dk = dk + dk_refs[t][...]
            dv = dv + dv_refs[t][...]
        dkv, dgk = _mem_kv_bwd(kv_ref[...], dk, dv, gk_ref[...])
        dkv_ref[...] = dkv.astype(BF)

        @pl.when(pl.program_id(0) == 0)
        def _():
            dgk_ref[...] = jnp.zeros_like(dgk_ref)

        dgk_ref[...] += dgk

    row = lambda i: (i, 0)
    one = pl.BlockSpec((1, HEAD), lambda i: (0, 0))
    return _call(
        body, "kv_bwd", (s_len // tm,),
        [pl.BlockSpec((tm, 2 * KVW), row), one] + [pl.BlockSpec((tm, KVW), row)] * (2 * nl),
        [pl.BlockSpec((tm, 2 * KVW), row), one],
        [SDS((s_len, 2 * KVW), BF), SDS((1, HEAD), F32)], semantics=("arbitrary",))(kv, gk, *dks, *dvs)


def _place():
    x, y, c = lax.axis_index("x"), lax.axis_index("y"), lax.axis_index("c")
    flips = [(1 - x, y), (x, 1 - y), (1 - x, 1 - y)]
    return x, y, c, flips


def _remote(src, dst, send_sem, recv_sem, to):
    return pltpu.make_async_remote_copy(src_ref=src, dst_ref=dst, send_sem=send_sem, recv_sem=recv_sem,
                                        device_id=to, device_id_type=MESH)


def _allgather_weights(p1, p2, ps):
    bufs = ((p1.shape[0] // 2, p1.shape[1]), (p2.shape[0] // 2, p2.shape[1]))

    def body(p1_ref, p2_ref, ps_ref, o1_ref, o2_ref, os_ref, send, recv, lsem):
        x, y, c, flips = _place()
        chip = 2 * x + y
        srcs, dsts = (p1_ref, p2_ref), (o1_ref, o2_ref)
        local = [pltpu.make_async_copy(p1_ref, o1_ref.at[chip], lsem.at[0]),
                 pltpu.make_async_copy(p2_ref, o2_ref.at[chip], lsem.at[1]),
                 pltpu.make_async_copy(ps_ref, os_ref.at[chip], lsem.at[2])]
        for cp in local:
            cp.start()
        started = []
        for j, (fx, fy) in enumerate(flips):
            for b, (half, _) in enumerate(bufs):
                mine = pl.ds(c * half, half)
                cp = _remote(srcs[b].at[mine, :], dsts[b].at[chip, mine, :], send.at[2 * j + b], recv.at[2 * j + b],
                             (fx, fy, c))
                cp.start()
                started.append(cp)
            cp = _remote(ps_ref, os_ref.at[chip], send.at[6 + j], recv.at[6 + j], (fx, fy, c))
            cp.start()
            started.append(cp)
        for j, (fx, fy) in enumerate(flips):
            fchip = 2 * fx + fy
            for b, (half, _) in enumerate(bufs):
                landed = dsts[b].at[fchip, pl.ds(c * half, half), :]
                _remote(landed, landed, send.at[2 * j + b], recv.at[2 * j + b], (fx, fy, c)).wait_recv()
                cp = _remote(landed, landed, send.at[9 + 2 * j + b], recv.at[9 + 2 * j + b], (x, y, 1 - c))
                cp.start()
                started.append(cp)
        for j, (fx, fy) in enumerate(flips):
            fchip = 2 * fx + fy
            _remote(ps_ref, os_ref.at[fchip], send.at[6 + j], recv.at[6 + j], (fx, fy, c)).wait_recv()
            for b, (half, _) in enumerate(bufs):
                other = dsts[b].at[fchip, pl.ds((1 - c) * half, half), :]
                _remote(other, other, send.at[9 + 2 * j + b], recv.at[9 + 2 * j + b], (x, y, 1 - c)).wait_recv()
        for cp in started:
            cp.wait_send()
        for cp in local:
            cp.wait()

    return pl.pallas_call(
        body, name="allgather_weights", in_specs=[ANY, ANY, ANY], out_specs=[ANY, ANY, ANY],
        out_shape=[SDS((N_CHIPS,) + p1.shape, p1.dtype), SDS((N_CHIPS,) + p2.shape, p2.dtype),
                   SDS((N_CHIPS,) + ps.shape, ps.dtype)],
        scratch_shapes=[pltpu.SemaphoreType.DMA((15,)), pltpu.SemaphoreType.DMA((15,)),
                        pltpu.SemaphoreType.DMA((3,))])(p1, p2, ps)


def _swap_sibling_halves(g1, g2):
    h1, h2 = g1.shape[1] // 2, g2.shape[1] // 2

    def body(g1_ref, g2_ref, r1_ref, r2_ref, send, recv):
        x, y, c, _ = _place()
        cps = [_remote(g1_ref.at[:, pl.ds((1 - c) * h1, h1), :], r1_ref, send.at[0], recv.at[0], (x, y, 1 - c)),
               _remote(g2_ref.at[:, pl.ds((1 - c) * h2, h2), :], r2_ref, send.at[1], recv.at[1], (x, y, 1 - c))]
        for cp in cps:
            cp.start()
        for cp in cps:
            cp.wait()

    return pl.pallas_call(
        body, name="swap_sibling_halves", in_specs=[ANY, ANY], out_specs=[ANY, ANY],
        out_shape=[SDS((N_CHIPS, h1, g1.shape[2]), g1.dtype), SDS((N_CHIPS, h2, g2.shape[2]), g2.dtype)],
        scratch_shapes=[pltpu.SemaphoreType.DMA((2,)), pltpu.SemaphoreType.DMA((2,))])(g1, g2)


def _sum_sibling(g, r, place, *, tm):
    n_sh, half, w = r.shape
    nt = half // tm

    def body(place_ref, g_ref, r_ref, pbf_ref, own_ref):
        s = pl.program_id(1)
        p = g_ref[0] + r_ref[0]
        pbf_ref[0] = p.astype(BF)

        @pl.when(s == place_ref[1])
        def _():
            own_ref[...] = p

    return _call(
        body, "sum_sibling", (nt, n_sh),
        [pl.BlockSpec((1, tm, w), lambda i, s, pr: (s, pr[0] * nt + i, 0)),
         pl.BlockSpec((1, tm, w), lambda i, s, pr: (s, i, 0))],
        [pl.BlockSpec((1, tm, w), lambda i, s, pr: (s, i, 0)), pl.BlockSpec((tm, w), lambda i, s, pr: (i, 0))],
        [SDS((n_sh, half, w), BF), SDS((half, w), F32)],
        semantics=("arbitrary", "arbitrary"), prefetch=1)(place, g, r)


def _exchange_chip_partials(p1, p2):
    def body(p1_ref, p2_ref, r1_ref, r2_ref, send, recv):
        _, _, c, flips = _place()
        cps = []
        for j, (fx, fy) in enumerate(flips):
            fchip = 2 * fx + fy
            cps.append(_remote(p1_ref.at[fchip], r1_ref.at[j], send.at[2 * j], recv.at[2 * j], (fx, fy, c)))
            cps.append(_remote(p2_ref.at[fchip], r2_ref.at[j], send.at[2 * j + 1], recv.at[2 * j + 1], (fx, fy, c)))
        for cp in cps:
            cp.start()
        for cp in cps:
            cp.wait()

    return pl.pallas_call(
        body, name="exchange_chip_partials", in_specs=[ANY, ANY], out_specs=[ANY, ANY],
        out_shape=[SDS((3,) + p1.shape[1:], p1.dtype), SDS((3,) + p2.shape[1:], p2.dtype)],
        scratch_shapes=[pltpu.SemaphoreType.DMA((6,)), pltpu.SemaphoreType.DMA((6,))])(p1, p2)


def _sum_chips(own, r, *, tm):
    half, w = own.shape

    def body(own_ref, r_ref, o_ref):
        o_ref[...] = ((own_ref[...] + r_ref[0].astype(F32)) + r_ref[1].astype(F32)) + r_ref[2].astype(F32)

    return _call(
        body, "sum_chips", (half // tm,),
        [pl.BlockSpec((tm, w), lambda i: (i, 0)), pl.BlockSpec((3, tm, w), lambda i: (0, i, 0))],
        pl.BlockSpec((tm, w), lambda i: (i, 0)), SDS((half, w), F32), semantics=("parallel",))(own, r)


def _share_with_sibling(r1, r2):
    def body(r1_ref, r2_ref, o1_ref, o2_ref, send, recv, lsem):
        x, y, c, _ = _place()
        local = [pltpu.make_async_copy(r1_ref, o1_ref.at[c], lsem.at[0]),
                 pltpu.make_async_copy(r2_ref, o2_ref.at[c], lsem.at[1])]
        cps = [_remote(r1_ref, o1_ref.at[c], send.at[0], recv.at[0], (x, y, 1 - c)),
               _remote(r2_ref, o2_ref.at[c], send.at[1], recv.at[1], (x, y, 1 - c))]
        for cp in local + cps:
            cp.start()
        cps[0].wait_send()
        cps[1].wait_send()
        _remote(r1_ref, o1_ref.at[1 - c], send.at[0], recv.at[0], (x, y, 1 - c)).wait_recv()
        _remote(r2_ref, o2_ref.at[1 - c], send.at[1], recv.at[1], (x, y, 1 - c)).wait_recv()
        for cp in local:
            cp.wait()

    return pl.pallas_call(
        body, name="share_with_sibling", in_specs=[ANY, ANY], out_specs=[ANY, ANY],
        out_shape=[SDS((2,) + r1.shape, r1.dtype), SDS((2,) + r2.shape, r2.dtype)],
        scratch_shapes=[pltpu.SemaphoreType.DMA((2,)), pltpu.SemaphoreType.DMA((2,)),
                        pltpu.SemaphoreType.DMA((2,))])(r1, r2)


def _allreduce_small(sg):
    rows, w = sg.shape

    def body(sg_ref, o_ref, slots, send, recv):
        x, y, c, _ = _place()
        me = 4 * x + 2 * y + c
        slots[me] = sg_ref[...]
        peers = []
        for k in range(1, 8):
            a, b, e = (k >> 2) & 1, (k >> 1) & 1, k & 1
            px = x + a - 2 * a * x
            py = y + b - 2 * b * y
            pc = c + e - 2 * e * c
            cp = _remote(sg_ref, slots.at[me], send.at[k - 1], recv.at[k - 1], (px, py, pc))
            cp.start()
            peers.append((cp, 4 * px + 2 * py + pc, (px, py, pc)))
        for k, (cp, pidx, pid) in enumerate(peers):
            _remote(sg_ref, slots.at[pidx], send.at[k], recv.at[k], pid).wait_recv()
        acc = slots[0]
        for dev in range(1, 8):
            acc = acc + slots[dev]
        o_ref[...] = acc
        for cp, _, _ in peers:
            cp.wait_send()

    vm = pl.BlockSpec(memory_space=pltpu.VMEM)
    return pl.pallas_call(
        body, name="allreduce_small", in_specs=[vm], out_specs=vm, out_shape=SDS((rows, w), F32),
        scratch_shapes=[pltpu.VMEM((8, rows, w), F32), pltpu.SemaphoreType.DMA((7,)),
                        pltpu.SemaphoreType.DMA((7,))],
        compiler_params=pltpu.CompilerParams(vmem_limit_bytes=VMEM_LIMIT_MB << 20))(sg)


def _adamw(g_arr, g_off, w, m, v, *, name, tm):
    rows, cols = w.shape
    assert g_off % tm == 0 and rows % tm == 0
    c1 = 1.0 - ADAM_B1 ** ADAM_STEP
    c2 = 1.0 - ADAM_B2 ** ADAM_STEP

    def body(g_ref, w_ref, m_ref, v_ref, go_ref, d_ref, mo_ref, vo_ref):
        g = g_ref[...]
        mn = ADAM_B1 * m_ref[...] + (1.0 - ADAM_B1) * g
        vn = ADAM_B2 * v_ref[...] + (1.0 - ADAM_B2) * (g * g)
        go_ref[...] = g
        mo_ref[...] = mn
        vo_ref[...] = vn
        d_ref[...] = -ADAM_LR * ((mn / c1) / (jnp.sqrt(vn / c2) + ADAM_EPS) + ADAM_WD * w_ref[...])

    blk = pl.BlockSpec((tm, cols), lambda i: (i, 0))
    return _call(
        body, name, (rows // tm,),
        [pl.BlockSpec((tm, cols), lambda i: (g_off // tm + i, 0)), blk, blk, blk], [blk] * 4,
        [SDS((rows, cols), F32)] * 4, semantics=("parallel",))(g_arr, w, m, v)


def _pack_small(parts, width):
    flat = jnp.concatenate([p.reshape(-1).astype(F32) for p in parts])
    rows = -(-flat.shape[0] // (8 * width)) * 8
    return jnp.pad(flat, (0, rows * width - flat.shape[0])).reshape(rows, width)


def _unpack_small(packed, shapes):
    flat = packed.reshape(-1)
    out, off = [], 0
    for shp in shapes:
        size = 1
        for n in shp:
            size *= n
        out.append(flat[off:off + size].reshape(shp))
        off += size
    return out


def _block_diag(pw):
    g, c, _ = pw.shape
    eye = jnp.eye(g, dtype=pw.dtype)
    return (eye[:, None, :, None] * pw[:, :, None, :]).reshape(g * c, g * c)


def _diag_blocks(full, g):
    c = full.shape[0] // g
    return jnp.stack([full[i * c:(i + 1) * c, i * c:(i + 1) * c] for i in range(g)])


def kernel(x, mem, norm_mix, w_in, pool_w, pool_scale, kv_norm, w_kv, k_norm, q_norm, sinks, mem_norm, w_mem_kv, mem_q_norm, mem_k_norm, w_out, norm_mlp, w_up, w_down, loss_target, m_norm_mix, m_w_in, m_pool_w, m_pool_scale, m_kv_norm, m_w_kv, m_k_norm, m_q_norm, m_sinks, m_mem_norm, m_w_mem_kv, m_mem_q_norm, m_mem_k_norm, m_w_out, m_norm_mlp, m_w_up, m_w_down, v_norm_mix, v_w_in, v_pool_w, v_pool_scale, v_kv_norm, v_w_kv, v_k_norm, v_q_norm, v_sinks, v_mem_norm, v_w_mem_kv, v_mem_q_norm, v_mem_k_norm, v_w_out, v_norm_mlp, v_w_up, v_w_down):
    s_len, d = x.shape[1], x.shape[2]
    n_layers, n_pool = norm_mix.shape[0], pool_w.shape[0]
    n_swa = n_layers - n_pool
    main = d - KVW
    qh = main // HEAD
    ff = w_down.shape[1] * N_CHIPS
    dq = d // N_CHIPS
    assert w_up.shape[2] == d and ff == N_CHIPS * d and w_kv.shape[1] == 2 * KVW
    tm = min(512, s_len)
    tm_mem = mem.shape[1]

    cx, cy, cc = lax.axis_index("x"), lax.axis_index("y"), lax.axis_index("c")
    chip = 2 * cx + cy

    off_down, off_up, off_in, off_out = 0, n_layers * d, 2 * n_layers * d, 2 * n_layers * d + n_layers * dq
    rows1 = off_out + n_layers * dq
    off_mkv, off_kv = 0, n_layers * dq
    rows2 = off_kv + dq

    p1 = jnp.concatenate([w_down.reshape(-1, d), w_up.reshape(-1, d), w_in.reshape(-1, d), w_out.reshape(-1, d)]
                         ).astype(BF)
    p2 = jnp.concatenate([w_mem_kv.reshape(-1, 2 * KVW), w_kv]).astype(BF)
    ps = jnp.pad(pool_scale, ((0, 8 - n_pool), (0, 2 * LANE - pool_scale.shape[1])))
    wg1, wg2, psg = _allgather_weights(p1, p2, ps)
    pool_scale_full = jnp.concatenate([psg[k, :n_pool, :pool_scale.shape[1]] for k in range(N_CHIPS)], axis=1)

    def w_rows(arr, off, nrows, width):
        assert off % nrows == 0
        return (arr, (N_CHIPS, nrows, width), lambda j: (0, off // nrows, 0))

    def w_cols(arr, off, nrows, width):
        assert off % nrows == 0
        return (arr, (1, nrows, width), lambda j: (j, off // nrows, 0))

    w_in_l = [w_rows(wg1, off_in + l * dq, dq, d) for l in range(n_layers)]
    w_out_l = [w_rows(wg1, off_out + l * dq, dq, d) for l in range(n_layers)]
    w_down_l = [w_rows(wg1, off_down + l * d, d, d) for l in range(n_layers)]
    w_down_cols_l = [w_cols(wg1, off_down + l * d, d, d) for l in range(n_layers)]
    w_up_l = [w_cols(wg1, off_up + l * d, d, d) for l in range(n_layers)]
    w_up_all_l = [(wg1, (N_CHIPS, d, d), (lambda l: lambda j: (0, (off_up + l * d) // d, 0))(l))
                  for l in range(n_layers)]
    w_mkv_l = [w_rows(wg2, off_mkv + l * dq, dq, 2 * KVW) for l in range(n_layers)]
    w_kv_g = w_rows(wg2, off_kv, dq, 2 * KVW)

    row = lambda a: a.reshape(1, -1)
    h = x.reshape(s_len, d)
    memx = mem.reshape(tm_mem, d)
    tgt = loss_target.reshape(s_len, d)
    pbd = [_block_diag(pool_w[l]).astype(BF) for l in range(n_pool)]
    sinks_pad = [jnp.pad(row(sinks[j]), ((0, 0), (0, LANE - qh))) for j in range(n_swa)]

    saved = []
    kv = hn_kv = kn = vsh = None
    for l in range(n_layers):
        if l == n_pool:
            kv, hn_kv = _norm_mm(h, row(kv_norm), w_kv_g, 1, 2 * KVW, act=False, name="kv_proj", tm=tm)
            kn, vsh = _kv_prep(kv, row(k_norm), tm=tm)
        h0 = h
        proj, xn = _norm_mm(h0, row(norm_mix[l]), w_in_l[l], 1, d, act=False, name=f"in_proj_{l}", tm=tm)
        mkv, memn = _norm_mm(memx, row(mem_norm[l]), w_mkv_l[l], 1, 2 * KVW, act=False, name=f"mem_kv_{l}", tm=tm_mem)
        if l < n_pool:
            cat = _mixer_pool_fwd(proj, mkv, pbd[l], row(pool_scale_full[l]), row(mem_q_norm[l]), row(mem_k_norm[l]),
                                  name=f"mixer_fwd_{l}", tm=tm)
        else:
            j = l - n_pool
            cat = _mixer_swa_fwd(proj, kn, vsh, mkv, row(q_norm[j]), sinks_pad[j], row(mem_q_norm[l]),
                                 row(mem_k_norm[l]), name=f"mixer_fwd_{l}")
        h1 = _mm_res(h0, cat, w_out_l[l], name=f"out_proj_{l}", tm=tm)
        hh, xm = _norm_mm(h1, row(norm_mlp[l]), w_up_l[l], N_CHIPS, d, act=True, name=f"mlp_up_{l}", tm=tm)
        h = _mm_res(h1, hh, w_down_l[l], name=f"mlp_down_{l}", tm=tm)
        saved.append((h0, proj, xn, mkv, memn, cat, h1, hh, xm))

    dh, dh_bf, loss_part = _loss_head(h, tgt, tm=tm)
    loss = lax.psum(loss_part[0, 0], ("x", "y", "c"))

    g1 = lax.empty((N_CHIPS, rows1, d), F32)
    g2 = lax.empty((N_CHIPS, rows2, 2 * KVW), F32)
    tk = min(512, d)
    zeros_mem = jnp.zeros((tm_mem, d), F32)

    def rows_map(off, nrows, tkk):
        per = nrows // tkk
        return lambda i, j: (i // per, off // tkk + i % per, 0)

    def cols_map(off, tkk):
        return lambda i, j: (j, off // tkk + i, 0)

    d_norm_mix, d_norm_mlp, d_mem_norm = [None] * n_layers, [None] * n_layers, [None] * n_layers
    d_mem_q, d_mem_k = [None] * n_layers, [None] * n_layers
    d_pool_w, d_pool_scale = [None] * n_pool, [None] * n_pool
    d_q_norm, d_sinks = [None] * n_swa, [None] * n_swa
    dks, dvs = [], []
    d_kv_norm = d_k_norm = None
    for l in reversed(range(n_layers)):
        h0, proj, xn, mkv, memn, cat, h1, hh, xm = saved[l]
        g1 = _mm_tn(hh, dh_bf, g1, rows_map(off_down + l * d, d, tk), tk, d, name=f"dw_down_{l}")
        du = _mm_nt_relu2(dh_bf, hh, w_down_cols_l[l], N_CHIPS, name=f"d_mlp_act_{l}", tm=tm)
        g1 = _mm_tn(xm, du, g1, cols_map(off_up + l * d, tk), tk, d, name=f"dw_up_{l}")
        dh1, dh1_bf, d_norm_mlp[l] = _mm_nt_normbwd(du, w_up_all_l[l], N_CHIPS, h1, row(norm_mlp[l]), dh,
                                                    name=f"d_mlp_in_{l}", tm=tm)
        tkq = min(tk, dq)
        g1 = _mm_tn(cat, dh1_bf, g1, rows_map(off_out + l * dq, dq, tkq), tkq, d, name=f"dw_out_{l}")
        dcat = _mm_nt(dh1_bf, w_out_l[l], d, name=f"d_cat_{l}", tm=tm)
        if l < n_pool:
            dproj, dpbd, dscale, dmkv, d_mem_q[l], d_mem_k[l] = _mixer_pool_bwd(
                proj, dcat, mkv, pbd[l], row(pool_scale_full[l]), row(mem_q_norm[l]), row(mem_k_norm[l]),
                name=f"mixer_bwd_{l}", tm=tm)
            d_pool_w[l] = _diag_blocks(dpbd, len(POOL_WINDOWS))
            d_pool_scale[l] = dscale
        else:
            j = l - n_pool
            dproj, dk, dv, dmkv, d_q_norm[j], dsk, d_mem_q[l], d_mem_k[l] = _mixer_swa_bwd(
                proj, dcat, kn, vsh, mkv, row(q_norm[j]), sinks_pad[j], row(mem_q_norm[l]), row(mem_k_norm[l]),
                name=f"mixer_bwd_{l}")
            d_sinks[j] = dsk[0, :qh]
            dks.append(dk)
            dvs.append(dv)
        g1 = _mm_tn(xn, dproj, g1, rows_map(off_in + l * dq, dq, tkq), tkq, d, name=f"dw_in_{l}")
        dh, dh_bf, d_norm_mix[l] = _mm_nt_normbwd(dproj, w_in_l[l], 1, h0, row(norm_mix[l]), dh1,
                                                  name=f"d_in_{l}", tm=tm)
        g2 = _mm_tn(memn, dmkv, g2, rows_map(off_mkv + l * dq, dq, tkq), tkq, 2 * KVW, name=f"dw_mem_kv_{l}")
        _, _, d_mem_norm[l] = _mm_nt_normbwd(dmkv, w_mkv_l[l], 1, memx, row(mem_norm[l]), zeros_mem,
                                             name=f"d_mem_norm_{l}", tm=tm_mem)
        if l == n_pool:
            dkv, d_k_norm = _kv_bwd(kv, dks, dvs, row(k_norm), tm=tm)
            g2 = _mm_tn(hn_kv, dkv, g2, rows_map(off_kv, dq, tkq), tkq, 2 * KVW, name="dw_kv")
            dh, dh_bf, d_kv_norm = _mm_nt_normbwd(dkv, w_kv_g, 1, h0, row(kv_norm), dh, name="d_kv_in", tm=tm)
    grad_x = dh.reshape(x.shape)

    place = jnp.stack([cc, chip]).astype(jnp.int32)
    r1, r2 = _swap_sibling_halves(g1, g2)
    pbf1, own1 = _sum_sibling(g1, r1, place, tm=_tile(rows1 // 2, 256))
    pbf2, own2 = _sum_sibling(g2, r2, place, tm=_tile(rows2 // 2, 256))
    x1, x2 = _exchange_chip_partials(pbf1, pbf2)
    red1 = _sum_chips(own1, x1, tm=_tile(rows1 // 2, 256))
    red2 = _sum_chips(own2, x2, tm=_tile(rows2 // 2, 256))
    full1, full2 = _share_with_sibling(red1, red2)
    full1 = full1.reshape(rows1, d)
    full2 = full2.reshape(rows2, 2 * KVW)

    big = {}
    for name, arr, off, w_, m_, v_ in (
            ("w_down", full1, off_down, w_down, m_w_down, v_w_down), ("w_up", full1, off_up, w_up, m_w_up, v_w_up),
            ("w_in", full1, off_in, w_in, m_w_in, v_w_in), ("w_out", full1, off_out, w_out, m_w_out, v_w_out),
            ("w_mem_kv", full2, off_mkv, w_mem_kv, m_w_mem_kv, v_w_mem_kv),
            ("w_kv", full2, off_kv, w_kv, m_w_kv, v_w_kv)):
        cols = arr.shape[1]
        res = _adamw(arr, off, w_.reshape(-1, cols), m_.reshape(-1, cols), v_.reshape(-1, cols),
                     name=f"adamw_{name}", tm=min(256, dq))
        big[name] = [r.reshape(w_.shape) for r in res]

    small_names = ["norm_mix", "pool_w", "pool_scale", "kv_norm", "k_norm", "q_norm", "sinks", "mem_norm",
                   "mem_q_norm", "mem_k_norm", "norm_mlp"]
    small_grads = {
        "norm_mix": jnp.concatenate(d_norm_mix), "pool_w": jnp.stack(d_pool_w),
        "pool_scale": jnp.concatenate(d_pool_scale), "kv_norm": d_kv_norm[0], "k_norm": d_k_norm[0],
        "q_norm": jnp.concatenate(d_q_norm), "sinks": jnp.stack(d_sinks), "mem_norm": jnp.concatenate(d_mem_norm),
        "mem_q_norm": jnp.concatenate(d_mem_q), "mem_k_norm": jnp.concatenate(d_mem_k),
        "norm_mlp": jnp.concatenate(d_norm_mlp)}
    width = d
    sg = _pack_small([small_grads[n] for n in small_names], width)
    sg = _allreduce_small(sg)
    reduced = dict(zip(small_names, _unpack_small(sg, [small_grads[n].shape for n in small_names])))
    psw = pool_scale.shape[1]
    reduced["pool_scale"] = lax.dynamic_slice_in_dim(reduced["pool_scale"], chip * psw, psw, axis=1)
    params = dict(norm_mix=(norm_mix, m_norm_mix, v_norm_mix), pool_w=(pool_w, m_pool_w, v_pool_w),
                  pool_scale=(pool_scale, m_pool_scale, v_pool_scale), kv_norm=(kv_norm, m_kv_norm, v_kv_norm),
                  k_norm=(k_norm, m_k_norm, v_k_norm), q_norm=(q_norm, m_q_norm, v_q_norm),
                  sinks=(sinks, m_sinks, v_sinks), mem_norm=(mem_norm, m_mem_norm, v_mem_norm),
                  mem_q_norm=(mem_q_norm, m_mem_q_norm, v_mem_q_norm),
                  mem_k_norm=(mem_k_norm, m_mem_k_norm, v_mem_k_norm), norm_mlp=(norm_mlp, m_norm_mlp, v_norm_mlp))
    shapes = [params[n][0].shape for n in small_names]
    packs = [_pack_small([reduced[n].reshape(params[n][0].shape) for n in small_names], width)]
    packs += [_pack_small([params[n][t] for n in small_names], width) for t in range(3)]
    res = _adamw(packs[0], 0, packs[1], packs[2], packs[3], name="adamw_small", tm=8)
    small = {n: [] for n in small_names}
    for r in res:
        for n, a in zip(small_names, _unpack_small(r, shapes)):
            small[n].append(a)

    order = ["norm_mix", "w_in", "pool_w", "pool_scale", "kv_norm", "w_kv", "k_norm", "q_norm", "sinks", "mem_norm",
             "w_mem_kv", "mem_q_norm", "mem_k_norm", "w_out", "norm_mlp", "w_up", "w_down"]
    out = {**big, **small}
    return (loss, grad_x, *[out[n][0] for n in order], *[out[n][1] for n in order],
            *[out[n][2] for n in order], *[out[n][3] for n in order])
```

```python
import functools

import jax
import jax.numpy as jnp
from jax import lax
from jax.experimental import pallas as pl
from jax.experimental.pallas import tpu as pltpu

F32, BF = jnp.float32, jnp.bfloat16
SDS = jax.ShapeDtypeStruct
MESH = pl.DeviceIdType.MESH
ANY = pl.BlockSpec(memory_space=pl.ANY)

EPS = 1e-6
HEAD = 64
KV_HEADS = 4
KVW = KV_HEADS * HEAD
WINDOW = 128
POOL_WINDOWS = (2, 4, 8, 16)
HALO = 16
QK_SCALE = HEAD ** -0.5
NEG = float(jnp.finfo(jnp.float32).min)
N_CHIPS = 4
LANE = 128

ADAM_LR, ADAM_B1, ADAM_B2, ADAM_EPS, ADAM_WD, ADAM_STEP = 0.001, 0.9, 0.999, 1e-08, 0.01, 10

VMEM_LIMIT_MB = 56


def _call(body, name, grid, in_specs, out_specs, out_shape, *, scratch=(), semantics=None, aliases=None,
          prefetch=0):
    params = pltpu.CompilerParams(dimension_semantics=semantics, vmem_limit_bytes=VMEM_LIMIT_MB << 20)
    if prefetch:
        spec = pltpu.PrefetchScalarGridSpec(num_scalar_prefetch=prefetch, grid=grid, in_specs=in_specs,
                                            out_specs=out_specs, scratch_shapes=list(scratch))
        return pl.pallas_call(body, name=name, grid_spec=spec, out_shape=out_shape,
                              input_output_aliases=aliases or {}, compiler_params=params)
    return pl.pallas_call(body, name=name, grid=grid, in_specs=in_specs, out_specs=out_specs, out_shape=out_shape,
                          scratch_shapes=list(scratch), input_output_aliases=aliases or {}, compiler_params=params)


def _tile(n, pref):
    return max(t for t in range(8, min(n, pref) + 1, 8) if n % t == 0)


def _dot(a, b):
    return jnp.dot(a, b, preferred_element_type=F32)


def _dot_nt(a, b):
    return lax.dot_general(a, b, (((1,), (1,)), ((), ())), preferred_element_type=F32)


def _dot_tn(a, b):
    return lax.dot_general(a, b, (((0,), (0,)), ((), ())), preferred_element_type=F32)


def _rms(x):
    r = lax.rsqrt(jnp.mean(x * x, axis=-1, keepdims=True) + EPS)
    return x * r, r


def _rms_bwd(dy, xh, r, g):
    dg = jnp.sum(dy * xh, axis=0, keepdims=True)
    dyg = dy * g
    dx = r * (dyg - xh * jnp.mean(dyg * xh, axis=-1, keepdims=True))
    return dx, dg


def _norm_mm(h, g, w, nj, tn, *, act, name, tm):
    w_arr, w_block, w_imap = w
    rows, d = h.shape

    def body(h_ref, g_ref, w_ref, y_ref, xn_ref):
        @pl.when(pl.program_id(1) == 0)
        def _():
            xh, _ = _rms(h_ref[...])
            xn_ref[...] = (xh * g_ref[...]).astype(BF)

        u = _dot(xn_ref[...], w_ref[...].reshape(d, tn))
        if act:
            a = jnp.maximum(u, 0.0)
            y_ref[...] = (a * a).astype(BF)
        else:
            y_ref[...] = u

    return _call(
        body, name, (rows // tm, nj),
        [pl.BlockSpec((tm, d), lambda i, j: (i, 0)), pl.BlockSpec((1, d), lambda i, j: (0, 0)),
         pl.BlockSpec(w_block, lambda i, j: w_imap(j))],
        [pl.BlockSpec((tm, tn), lambda i, j: (i, j)), pl.BlockSpec((tm, d), lambda i, j: (i, 0))],
        [SDS((rows, nj * tn), BF if act else F32), SDS((rows, d), BF)],
        semantics=("parallel", "arbitrary"))(h, g, w_arr)


def _mm_res(res, a, w, *, name, tm):
    w_arr, w_block, w_imap = w
    rows, k = a.shape
    n = res.shape[1]

    def body(res_ref, a_ref, w_ref, o_ref):
        o_ref[...] = res_ref[...] + _dot(a_ref[...], w_ref[...].reshape(k, n))

    return _call(
        body, name, (rows // tm,),
        [pl.BlockSpec((tm, n), lambda i: (i, 0)), pl.BlockSpec((tm, k), lambda i: (i, 0)),
         pl.BlockSpec(w_block, lambda i: w_imap(0))],
        pl.BlockSpec((tm, n), lambda i: (i, 0)), SDS((rows, n), F32), semantics=("parallel",))(res, a, w_arr)


def _mm_nt(dy, w, k, *, name, tm):
    w_arr, w_block, w_imap = w
    rows, n = dy.shape

    def body(dy_ref, w_ref, o_ref):
        o_ref[...] = _dot_nt(dy_ref[...], w_ref[...].reshape(k, n))

    return _call(
        body, name, (rows // tm,),
        [pl.BlockSpec((tm, n), lambda i: (i, 0)), pl.BlockSpec(w_block, lambda i: w_imap(0))],
        pl.BlockSpec((tm, k), lambda i: (i, 0)), SDS((rows, k), F32), semantics=("parallel",))(dy, w_arr)


def _mm_nt_relu2(dh, hh, w, nj, *, name, tm):
    w_arr, w_block, w_imap = w
    rows, d = dh.shape
    tk = hh.shape[1] // nj

    def body(dh_ref, hh_ref, w_ref, o_ref):
        dhh = _dot_nt(dh_ref[...], w_ref[...].reshape(tk, d))
        o_ref[...] = (dhh * (2.0 * jnp.sqrt(hh_ref[...].astype(F32)))).astype(BF)

    return _call(
        body, name, (rows // tm, nj),
        [pl.BlockSpec((tm, d), lambda i, j: (i, 0)), pl.BlockSpec((tm, tk), lambda i, j: (i, j)),
         pl.BlockSpec(w_block, lambda i, j: w_imap(j))],
        pl.BlockSpec((tm, tk), lambda i, j: (i, j)), SDS((rows, nj * tk), BF),
        semantics=("parallel", "parallel"))(dh, hh, w_arr)


def _mm_nt_normbwd(dy, w, nsplit, h, g, dres, *, name, tm):
    w_arr, w_block, w_imap = w
    rows, n = dy.shape
    d = h.shape[1]
    ns = n // nsplit

    def body(dy_ref, w_ref, h_ref, g_ref, dres_ref, o_ref, obf_ref, dg_ref):
        if nsplit == 1:
            dxn = _dot_nt(dy_ref[...].astype(BF), w_ref[...].reshape(d, n))
        else:
            dxn = _dot_nt(dy_ref[:, 0:ns].astype(BF), w_ref[0])
            for s in range(1, nsplit):
                dxn += _dot_nt(dy_ref[:, s * ns:(s + 1) * ns].astype(BF), w_ref[s])
        xh, r = _rms(h_ref[...])
        dx, dg = _rms_bwd(dxn, xh, r, g_ref[...])
        out = dres_ref[...] + dx
        o_ref[...] = out
        obf_ref[...] = out.astype(BF)

        @pl.when(pl.program_id(0) == 0)
        def _():
            dg_ref[...] = jnp.zeros_like(dg_ref)

        dg_ref[...] += dg

    row = lambda i: (i, 0)
    return _call(
        body, name, (rows // tm,),
        [pl.BlockSpec((tm, n), row), pl.BlockSpec(w_block, lambda i: w_imap(0)), pl.BlockSpec((tm, d), row),
         pl.BlockSpec((1, d), lambda i: (0, 0)), pl.BlockSpec((tm, d), row)],
        [pl.BlockSpec((tm, d), row), pl.BlockSpec((tm, d), row), pl.BlockSpec((1, d), lambda i: (0, 0))],
        [SDS((rows, d), F32), SDS((rows, d), BF), SDS((1, d), F32)],
        semantics=("arbitrary",))(dy, w_arr, h, g, dres)


def _mm_tn(x, dy, packed, out_imap, tk, tn, *, name):
    s_len, k = x.shape
    n = dy.shape[1]

    def body(x_ref, dy_ref, _, o_ref):
        o_ref[0] = _dot_tn(x_ref[...], dy_ref[...].astype(BF))

    return _call(
        body, name, (k // tk, n // tn),
        [pl.BlockSpec((s_len, tk), lambda i, j: (0, i)), pl.BlockSpec((s_len, tn), lambda i, j: (0, j)), ANY],
        pl.BlockSpec((1, tk, tn), out_imap), SDS(packed.shape, packed.dtype),
        semantics=("parallel", "parallel"), aliases={2: 0})(x, dy, packed)


def _loss_head(y, tgt, *, tm):
    rows, d = y.shape

    def body(y_ref, t_ref, dh_ref, dhbf_ref, loss_ref):
        err = y_ref[...] - t_ref[...]
        dh = err * (1.0 / d)
        dh_ref[...] = dh
        dhbf_ref[...] = dh.astype(BF)

        @pl.when(pl.program_id(0) == 0)
        def _():
            loss_ref[...] = jnp.zeros_like(loss_ref)

        loss_ref[...] += 0.5 * jnp.sum(jnp.mean(err * err, axis=-1, keepdims=True), axis=0, keepdims=True)

    row = lambda i: (i, 0)
    return _call(
        body, "loss_head", (rows // tm,), [pl.BlockSpec((tm, d), row), pl.BlockSpec((tm, d), row)],
        [pl.BlockSpec((tm, d), row), pl.BlockSpec((tm, d), row), pl.BlockSpec((1, 1), lambda i: (0, 0))],
        [SDS((rows, d), F32), SDS((rows, d), BF), SDS((1, 1), F32)], semantics=("arbitrary",))(y, tgt)


def _hs(h):
    return slice(HEAD * h, HEAD * (h + 1))


def _softmax_rows(s):
    e = jnp.exp(s - jnp.max(s, axis=-1, keepdims=True))
    return e / jnp.sum(e, axis=-1, keepdims=True)


def _mem_fwd(mq, mk, mv, gq):
    outs = []
    for h in range(KV_HEADS):
        xh, _ = _rms(mq[:, _hs(h)])
        qn = (xh * gq).astype(BF)
        p = _softmax_rows(_dot_nt(qn, mk[:, _hs(h)]) * QK_SCALE)
        outs.append(_dot(p.astype(BF), mv[:, _hs(h)]))
    return jnp.concatenate(outs, axis=-1)


def _mem_bwd(mq, do, mk, mv, gq):
    dqs, dks, dvs, dgq = [], [], [], 0.0
    for h in range(KV_HEADS):
        xh, r = _rms(mq[:, _hs(h)])
        qn = (xh * gq).astype(BF)
        p = _softmax_rows(_dot_nt(qn, mk[:, _hs(h)]) * QK_SCALE)
        doh = do[:, _hs(h)].astype(BF)
        dp = _dot_nt(doh, mv[:, _hs(h)])
        ds = (p * (dp - jnp.sum(p * dp, axis=-1, keepdims=True)) * QK_SCALE).astype(BF)
        dq, dg = _rms_bwd(_dot(ds, mk[:, _hs(h)]), xh, r, gq)
        dqs.append(dq)
        dgq = dgq + dg
        dks.append(_dot_tn(ds, qn))
        dvs.append(_dot_tn(p.astype(BF), doh))
    cat = lambda xs: jnp.concatenate(xs, axis=-1)
    return cat(dqs), cat(dks), cat(dvs), dgq


def _mem_kv(mkv, gk):
    ks = []
    for h in range(KV_HEADS):
        xh, _ = _rms(mkv[:, _hs(h)])
        ks.append(xh * gk)
    return jnp.concatenate(ks, axis=-1).astype(BF), mkv[:, KVW:].astype(BF)


def _mem_kv_bwd(mkv, dmk, dmv, gk):
    dxs, dgk = [], 0.0
    for h in range(KV_HEADS):
        xh, r = _rms(mkv[:, _hs(h)])
        dx, dg = _rms_bwd(dmk[:, _hs(h)], xh, r, gk)
        dxs.append(dx)
        dgk = dgk + dg
    return jnp.concatenate(dxs + [dmv], axis=-1), dgk


def _pool_select(col, gd, a2, a4, a8, a16):
    return jnp.where(col < gd, a2, jnp.where(col < 2 * gd, a4, jnp.where(col < 3 * gd, a8, a16)))


def _pool_count(t0, shape, gd):
    col = lax.broadcasted_iota(jnp.int32, shape, 1)
    t = t0 + lax.broadcasted_iota(jnp.int32, shape, 0)
    win = _pool_select(col, gd, *POOL_WINDOWS)
    return jnp.minimum(t + 1, win).astype(F32)


def _pool_diff(u, halo, t0, gd):
    c = jnp.concatenate([halo, u], axis=0)
    s2 = c + pltpu.roll(c, 1, 0)
    s4 = s2 + pltpu.roll(s2, 2, 0)
    s8 = s4 + pltpu.roll(s4, 4, 0)
    s16 = s8 + pltpu.roll(s8, 8, 0)
    col = lax.broadcasted_iota(jnp.int32, c.shape, 1)
    ws = _pool_select(col, gd, s2, s4, s8, s16)[HALO:]
    return ws / _pool_count(t0, u.shape, gd) - u


def _pool_diff_bwd(dd, dd_halo, t0, gd):
    t = dd.shape[0]
    z = jnp.concatenate([dd / _pool_count(t0, dd.shape, gd), dd_halo / _pool_count(t0 + t, dd_halo.shape, gd)], axis=0)
    n = z.shape[0]
    f2 = z + pltpu.roll(z, n - 1, 0)
    f4 = f2 + pltpu.roll(f2, n - 2, 0)
    f8 = f4 + pltpu.roll(f4, n - 4, 0)
    f16 = f8 + pltpu.roll(f8, n - 8, 0)
    col = lax.broadcasted_iota(jnp.int32, z.shape, 1)
    return _pool_select(col, gd, f2, f4, f8, f16)[:t] - dd


def _swa_bias(n):
    qi = lax.broadcasted_iota(jnp.int32, (WINDOW, 2 * WINDOW), 0)
    kj = lax.broadcasted_iota(jnp.int32, (WINDOW, 2 * WINDOW), 1)
    dist = qi + WINDOW - kj
    valid = (dist >= 0) & (dist < WINDOW) & ((kj >= WINDOW) | (n > 0))
    return dist.astype(F32), valid


def _slopes(qh):
    return [2.0 ** (-8.0 * (h + 1) / qh) for h in range(qh)]


def _swa_probs(qn, kk, dist, valid, slope, sink):
    s = _dot_nt(qn, kk) * QK_SCALE - slope * dist
    s = jnp.where(valid, s, NEG)
    m = jnp.maximum(jnp.max(s, axis=-1, keepdims=True), sink)
    e = jnp.exp(s - m)
    es = jnp.exp(sink - m)
    z = jnp.sum(e, axis=-1, keepdims=True) + es
    return e / z, es / z


def _swa_group(q, kh, grp, n, qh, sinks):
    heads = range(kh * grp, (kh + 1) * grp)
    dist, valid = _swa_bias(n)
    slopes = _slopes(qh)
    rows = lambda vals: jnp.concatenate([jnp.broadcast_to(v, (WINDOW, 1)) for v in vals], axis=0)
    qs = jnp.concatenate([q[:, _hs(h)] for h in heads], axis=0)
    slope = rows([jnp.full((1, 1), slopes[h], F32) for h in heads])
    sink = rows([sinks[:, h:h + 1] for h in heads])
    return qs, slope, sink, jnp.concatenate([dist] * grp, axis=0), jnp.concatenate([valid] * grp, axis=0)


def _swa_fwd(q, kk, vv, gq, sinks, n, qh):
    grp = qh // KV_HEADS
    outs = []
    for kh in range(KV_HEADS):
        qs, slope, sink, dist, valid = _swa_group(q, kh, grp, n, qh, sinks)
        xh, _ = _rms(qs)
        p, _ = _swa_probs((xh * gq).astype(BF), kk[:, _hs(kh)], dist, valid, slope, sink)
        o = _dot(p.astype(BF), vv[:, _hs(kh)])
        outs += [o[g * WINDOW:(g + 1) * WINDOW] for g in range(grp)]
    return jnp.concatenate(outs, axis=-1)


def _swa_bwd(q, do, kk, vv, gq, sinks, n, qh):
    grp = qh // KV_HEADS
    lane = lax.broadcasted_iota(jnp.int32, (1, LANE), 1)
    dqs, dks, dvs, dgq, dsk = [], [], [], 0.0, jnp.zeros((1, LANE), F32)
    for kh in range(KV_HEADS):
        qs, slope, sink, dist, valid = _swa_group(q, kh, grp, n, qh, sinks)
        xh, r = _rms(qs)
        qn = (xh * gq).astype(BF)
        p, ps = _swa_probs(qn, kk[:, _hs(kh)], dist, valid, slope, sink)
        dos = jnp.concatenate([do[:, _hs(h)] for h in range(kh * grp, (kh + 1) * grp)], axis=0).astype(BF)
        dp = _dot_nt(dos, vv[:, _hs(kh)])
        delta = jnp.sum(p * dp, axis=-1, keepdims=True)
        ds = (p * (dp - delta) * QK_SCALE).astype(BF)
        dsink = ps * delta
        for g in range(grp):
            part = -jnp.sum(dsink[g * WINDOW:(g + 1) * WINDOW], axis=0, keepdims=True)
            dsk = dsk + jnp.where(lane == kh * grp + g, part, 0.0)
        dq, dg = _rms_bwd(_dot(ds, kk[:, _hs(kh)]), xh, r, gq)
        dqs += [dq[g * WINDOW:(g + 1) * WINDOW] for g in range(grp)]
        dgq = dgq + dg
        dks.append(_dot_tn(ds, qn))
        dvs.append(_dot_tn(p.astype(BF), dos))
    cat = lambda xs: jnp.concatenate(xs, axis=-1)
    return cat(dqs), cat(dks), cat(dvs), dgq, dsk


def _mixer_pool_fwd(proj, mkv, pbd, scale, gq, gk, *, name, tm):
    s_len, d = proj.shape
    main = d - KVW
    gd = main // len(POOL_WINDOWS)
    mlen = mkv.shape[0]
    hb = tm // HALO

    def body(u_ref, halo_ref, mq_ref, mkv_ref, pbd_ref, scale_ref, gq_ref, gk_ref, o_ref, mk_s, mv_s):
        i = pl.program_id(0)

        @pl.when(i == 0)
        def _():
            mk, mv = _mem_kv(mkv_ref[...], gk_ref[...])
            mk_s[...] = mk
            mv_s[...] = mv

        halo = jnp.where(i > 0, halo_ref[...], 0.0)
        dif = _pool_diff(u_ref[...], halo, i * tm, gd)
        mixed = _dot(dif.astype(BF), pbd_ref[...]) * scale_ref[...]
        mem = _mem_fwd(mq_ref[...], mk_s[...], mv_s[...], gq_ref[...])
        o_ref[...] = jnp.concatenate([mixed, mem], axis=-1).astype(BF)

    full = lambda shape: pl.BlockSpec(shape, lambda i: (0,) * len(shape))
    return _call(
        body, name, (s_len // tm,),
        [pl.BlockSpec((tm, main), lambda i: (i, 0)),
         pl.BlockSpec((HALO, main), lambda i: (jnp.maximum(i * hb - 1, 0), 0)),
         pl.BlockSpec((tm, KVW), lambda i: (i, main // KVW)),
         full((mlen, 2 * KVW)), full((main, main)), full((1, main)), full((1, HEAD)), full((1, HEAD))],
        pl.BlockSpec((tm, d), lambda i: (i, 0)), SDS((s_len, d), BF),
        scratch=[pltpu.VMEM((mlen, KVW), BF), pltpu.VMEM((mlen, KVW), BF)],
        semantics=("arbitrary",))(proj, proj, proj, mkv, pbd, scale, gq, gk)


def _mixer_pool_bwd(proj, dcat, mkv, pbd, scale, gq, gk, *, name, tm):
    s_len, d = proj.shape
    main = d - KVW
    gd = main // len(POOL_WINDOWS)
    mlen = mkv.shape[0]
    hb = tm // HALO
    nt = s_len // tm
    last_halo = s_len // HALO - 1

    def body(u_ref, halo_ref, mq_ref, do_ref, donext_ref, dom_ref, mkv_ref, pbd_ref, scale_ref, gq_ref, gk_ref,
             dproj_ref, dpbd_ref, dscale_ref, dmkv_ref, dgq_ref, dgk_ref, mk_s, mv_s, dmk_s, dmv_s):
        i = pl.program_id(0)

        @pl.when(i == 0)
        def _():
            mk, mv = _mem_kv(mkv_ref[...], gk_ref[...])
            mk_s[...] = mk
            mv_s[...] = mv
            dmk_s[...] = jnp.zeros_like(dmk_s)
            dmv_s[...] = jnp.zeros_like(dmv_s)
            dpbd_ref[...] = jnp.zeros_like(dpbd_ref)
            dscale_ref[...] = jnp.zeros_like(dscale_ref)
            dgq_ref[...] = jnp.zeros_like(dgq_ref)

        pbd = pbd_ref[...]
        scale = scale_ref[...]
        halo = jnp.where(i > 0, halo_ref[...], 0.0)
        dif = _pool_diff(u_ref[...], halo, i * tm, gd).astype(BF)
        do = do_ref[...]
        dscale_ref[...] += jnp.sum(do * _dot(dif, pbd), axis=0, keepdims=True)
        dmixed = (do * scale).astype(BF)
        dpbd_ref[...] += _dot_tn(dif, dmixed)
        dd = _dot_nt(dmixed, pbd)
        donext = jnp.where(i < nt - 1, donext_ref[...], 0.0)
        dd_halo = _dot_nt((donext * scale).astype(BF), pbd)
        du = _pool_diff_bwd(dd, dd_halo, i * tm, gd)

        dmq, dmk, dmv, dgq = _mem_bwd(mq_ref[...], dom_ref[...], mk_s[...], mv_s[...], gq_ref[...])
        dmk_s[...] += dmk
        dmv_s[...] += dmv
        dgq_ref[...] += dgq
        dproj_ref[...] = jnp.concatenate([du, dmq], axis=-1).astype(BF)

        @pl.when(i == nt - 1)
        def _():
            dmkv, dgk = _mem_kv_bwd(mkv_ref[...], dmk_s[...], dmv_s[...], gk_ref[...])
            dmkv_ref[...] = dmkv
            dgk_ref[...] = dgk

    full = lambda shape: pl.BlockSpec(shape, lambda i: (0,) * len(shape))
    return _call(
        body, name, (nt,),
        [pl.BlockSpec((tm, main), lambda i: (i, 0)),
         pl.BlockSpec((HALO, main), lambda i: (jnp.maximum(i * hb - 1, 0), 0)),
         pl.BlockSpec((tm, KVW), lambda i: (i, main // KVW)),
         pl.BlockSpec((tm, main), lambda i: (i, 0)),
         pl.BlockSpec((HALO, main), lambda i: (jnp.minimum((i + 1) * hb, last_halo), 0)),
         pl.BlockSpec((tm, KVW), lambda i: (i, main // KVW)),
         full((mlen, 2 * KVW)), full((main, main)), full((1, main)), full((1, HEAD)), full((1, HEAD))],
        [pl.BlockSpec((tm, d), lambda i: (i, 0)), full((main, main)), full((1, main)), full((mlen, 2 * KVW)),
         full((1, HEAD)), full((1, HEAD))],
        [SDS((s_len, d), BF), SDS((main, main), F32), SDS((1, main), F32), SDS((mlen, 2 * KVW), F32),
         SDS((1, HEAD), F32), SDS((1, HEAD), F32)],
        scratch=[pltpu.VMEM((mlen, KVW), BF), pltpu.VMEM((mlen, KVW), BF), pltpu.VMEM((mlen, KVW), F32),
                 pltpu.VMEM((mlen, KVW), F32)],
        semantics=("arbitrary",))(proj, proj, proj, dcat, dcat, dcat, mkv, pbd, scale, gq, gk)


def _mixer_swa_fwd(proj, kn, v, mkv, gqs, sinks, gq, gk, *, name):
    s_len, d = proj.shape
    main = d - KVW
    qh = main // HEAD
    mlen = mkv.shape[0]
    tm = WINDOW

    def body(q_ref, mq_ref, kp_ref, kc_ref, vp_ref, vc_ref, mkv_ref, gqs_ref, sinks_ref, gq_ref, gk_ref, o_ref,
             mk_s, mv_s):
        n = pl.program_id(0)

        @pl.when(n == 0)
        def _():
            mk, mv = _mem_kv(mkv_ref[...], gk_ref[...])
            mk_s[...] = mk
            mv_s[...] = mv

        kk = jnp.concatenate([kp_ref[...], kc_ref[...]], axis=0)
        vv = jnp.concatenate([vp_ref[...], vc_ref[...]], axis=0)
        att = _swa_fwd(q_ref[...], kk, vv, gqs_ref[...], sinks_ref[...], n, qh)
        mem = _mem_fwd(mq_ref[...], mk_s[...], mv_s[...], gq_ref[...])
        o_ref[...] = jnp.concatenate([att, mem], axis=-1).astype(BF)

    full = lambda shape: pl.BlockSpec(shape, lambda i: (0,) * len(shape))
    prev = lambda i: (jnp.maximum(i - 1, 0), 0)
    cur = lambda i: (i, 0)
    return _call(
        body, name, (s_len // tm,),
        [pl.BlockSpec((tm, main), cur), pl.BlockSpec((tm, KVW), lambda i: (i, main // KVW)),
         pl.BlockSpec((tm, KVW), prev), pl.BlockSpec((tm, KVW), cur),
         pl.BlockSpec((tm, KVW), prev), pl.BlockSpec((tm, KVW), cur),
         full((mlen, 2 * KVW)), full((1, HEAD)), full((1, LANE)), full((1, HEAD)), full((1, HEAD))],
        pl.BlockSpec((tm, d), cur), SDS((s_len, d), BF),
        scratch=[pltpu.VMEM((mlen, KVW), BF), pltpu.VMEM((mlen, KVW), BF)],
        semantics=("arbitrary",))(proj, proj, kn, kn, v, v, mkv, gqs, sinks, gq, gk)


def _mixer_swa_bwd(proj, dcat, kn, v, mkv, gqs, sinks, gq, gk, *, name):
    s_len, d = proj.shape
    main = d - KVW
    qh = main // HEAD
    mlen = mkv.shape[0]
    tm = WINDOW
    nt = s_len // tm

    def body(q_ref, mq_ref, do_ref, dom_ref, kp_ref, kc_ref, vp_ref, vc_ref, mkv_ref, gqs_ref, sinks_ref, gq_ref,
             gk_ref, dproj_ref, dk_ref, dv_ref, dmkv_ref, dgqs_ref, dsinks_ref, dgq_ref, dgk_ref,
             mk_s, mv_s, dmk_s, dmv_s):
        n = pl.program_id(0)

        @pl.when(n == 0)
        def _():
            mk, mv = _mem_kv(mkv_ref[...], gk_ref[...])
            mk_s[...] = mk
            mv_s[...] = mv
            dmk_s[...] = jnp.zeros_like(dmk_s)
            dmv_s[...] = jnp.zeros_like(dmv_s)
            dk_ref[...] = jnp.zeros_like(dk_ref)
            dv_ref[...] = jnp.zeros_like(dv_ref)
            dgqs_ref[...] = jnp.zeros_like(dgqs_ref)
            dsinks_ref[...] = jnp.zeros_like(dsinks_ref)
            dgq_ref[...] = jnp.zeros_like(dgq_ref)

        kk = jnp.concatenate([kp_ref[...], kc_ref[...]], axis=0)
        vv = jnp.concatenate([vp_ref[...], vc_ref[...]], axis=0)
        dq, dkk, dvv, dgqs, dsk = _swa_bwd(q_ref[...], do_ref[...], kk, vv, gqs_ref[...], sinks_ref[...], n, qh)
        prev = pl.ds(pl.multiple_of(jnp.maximum(n - 1, 0) * tm, tm), tm)
        own = pl.ds(pl.multiple_of(n * tm, tm), tm)
        dk_ref[prev, :] += dkk[:tm]
        dk_ref[own, :] += dkk[tm:]
        dv_ref[prev, :] += dvv[:tm]
        dv_ref[own, :] += dvv[tm:]
        dgqs_ref[...] += dgqs
        dsinks_ref[...] += dsk

        dmq, dmk, dmv, dgq = _mem_bwd(mq_ref[...], dom_ref[...], mk_s[...], mv_s[...], gq_ref[...])
        dmk_s[...] += dmk
        dmv_s[...] += dmv
        dgq_ref[...] += dgq
        dproj_ref[...] = jnp.concatenate([dq, dmq], axis=-1).astype(BF)

        @pl.when(n == nt - 1)
        def _():
            dmkv, dgk = _mem_kv_bwd(mkv_ref[...], dmk_s[...], dmv_s[...], gk_ref[...])
            dmkv_ref[...] = dmkv
            dgk_ref[...] = dgk

    full = lambda shape: pl.BlockSpec(shape, lambda i: (0,) * len(shape))
    prev_b = lambda i: (jnp.maximum(i - 1, 0), 0)
    cur = lambda i: (i, 0)
    memcol = lambda i: (i, main // KVW)
    return _call(
        body, name, (nt,),
        [pl.BlockSpec((tm, main), cur), pl.BlockSpec((tm, KVW), memcol),
         pl.BlockSpec((tm, main), cur), pl.BlockSpec((tm, KVW), memcol),
         pl.BlockSpec((tm, KVW), prev_b), pl.BlockSpec((tm, KVW), cur),
         pl.BlockSpec((tm, KVW), prev_b), pl.BlockSpec((tm, KVW), cur),
         full((mlen, 2 * KVW)), full((1, HEAD)), full((1, LANE)), full((1, HEAD)), full((1, HEAD))],
        [pl.BlockSpec((tm, d), cur), full((s_len, KVW)), full((s_len, KVW)), full((mlen, 2 * KVW)),
         full((1, HEAD)), full((1, LANE)), full((1, HEAD)), full((1, HEAD))],
        [SDS((s_len, d), BF), SDS((s_len, KVW), F32), SDS((s_len, KVW), F32), SDS((mlen, 2 * KVW), F32),
         SDS((1, HEAD), F32), SDS((1, LANE), F32), SDS((1, HEAD), F32), SDS((1, HEAD), F32)],
        scratch=[pltpu.VMEM((mlen, KVW), BF), pltpu.VMEM((mlen, KVW), BF), pltpu.VMEM((mlen, KVW), F32),
                 pltpu.VMEM((mlen, KVW), F32)],
        semantics=("arbitrary",))(proj, proj, dcat, dcat, kn, kn, v, v, mkv, gqs, sinks, gq, gk)


def _kv_prep(kv, gk, *, tm):
    s_len = kv.shape[0]

    def body(kv_ref, gk_ref, k_ref, v_ref):
        k, v = _mem_kv(kv_ref[...], gk_ref[...])
        k_ref[...] = k
        v_ref[...] = v

    row = lambda i: (i, 0)
    return _call(
        body, "kv_prep", (s_len // tm,),
        [pl.BlockSpec((tm, 2 * KVW), row), pl.BlockSpec((1, HEAD), lambda i: (0, 0))],
        [pl.BlockSpec((tm, KVW), row), pl.BlockSpec((tm, KVW), row)],
        [SDS((s_len, KVW), BF), SDS((s_len, KVW), BF)], semantics=("parallel",))(kv, gk)


def _kv_bwd(kv, dks, dvs, gk, *, tm):
    s_len = kv.shape[0]
    nl = len(dks)

    def body(*refs):
        kv_ref, gk_ref = refs[0], refs[1]
        dk_refs, dv_refs = refs[2:2 + nl], refs[2 + nl:2 + 2 * nl]
        dkv_ref, dgk_ref = refs[2 + 2 * nl], refs[3 + 2 * nl]
        dk, dv = dk_refs[0][...], dv_refs[0][...]
        for t in range(1, nl):
            dk = dk + dk_refs[t][...]
            dv = dv + dv_refs[t][...]
        dkv, dgk = _mem_kv_bwd(kv_ref[...], dk, dv, gk_ref[...])
        dkv_ref[...] = dkv.astype(BF)

        @pl.when(pl.program_id(0) == 0)
        def _():
            dgk_ref[...] = jnp.zeros_like(dgk_ref)

        dgk_ref[...] += dgk

    row = lambda i: (i, 0)
    one = pl.BlockSpec((1, HEAD), lambda i: (0, 0))
    return _call(
        body, "kv_bwd", (s_len // tm,),
        [pl.BlockSpec((tm, 2 * KVW), row), one] + [pl.BlockSpec((tm, KVW), row)] * (2 * nl),
        [pl.BlockSpec((tm, 2 * KVW), row), one],
        [SDS((s_len, 2 * KVW), BF), SDS((1, HEAD), F32)], semantics=("arbitrary",))(kv, gk, *dks, *dvs)


def _place():
    x, y, c = lax.axis_index("x"), lax.axis_index("y"), lax.axis_index("c")
    flips = [(1 - x, y), (x, 1 - y), (1 - x, 1 - y)]
    return x, y, c, flips


def _remote(src, dst, send_sem, recv_sem, to):
    return pltpu.make_async_remote_copy(src_ref=src, dst_ref=dst, send_sem=send_sem, recv_sem=recv_sem,
                                        device_id=to, device_id_type=MESH)


def _allgather_weights(p1, p2, ps):
    bufs = ((p1.shape[0] // 2, p1.shape[1]), (p2.shape[0] // 2, p2.shape[1]))

    def body(p1_ref, p2_ref, ps_ref, o1_ref, o2_ref, os_ref, send, recv):
        x, y, c, flips = _place()
        chip = 2 * x + y
        srcs, dsts = (p1_ref, p2_ref), (o1_ref, o2_ref)
        own = [(p1_ref, o1_ref.at[chip], 15), (p2_ref, o2_ref.at[chip], 16), (ps_ref, os_ref.at[chip], 17)]
        started = []
        for src, dst, k in own:
            cp = _remote(src, dst, send.at[k], recv.at[k], (x, y, 1 - c))
            cp.start()
            started.append(cp)
        for j, (fx, fy) in enumerate(flips):
            for b, (half, _) in enumerate(bufs):
                mine = pl.ds(c * half, half)
                cp = _remote(srcs[b].at[mine, :], dsts[b].at[chip, mine, :], send.at[2 * j + b], recv.at[2 * j + b],
                             (fx, fy, c))
                cp.start()
                started.append(cp)
            cp = _remote(ps_ref, os_ref.at[chip], send.at[6 + j], recv.at[6 + j], (fx, fy, c))
            cp.start()
            started.append(cp)
        for j, (fx, fy) in enumerate(flips):
            fchip = 2 * fx + fy
            for b, (half, _) in enumerate(bufs):
                landed = dsts[b].at[fchip, pl.ds(c * half, half), :]
                _remote(landed, landed, send.at[2 * j + b], recv.at[2 * j + b], (fx, fy, c)).wait_recv()
                cp = _remote(landed, landed, send.at[9 + 2 * j + b], recv.at[9 + 2 * j + b], (x, y, 1 - c))
                cp.start()
                started.append(cp)
        for j, (fx, fy) in enumerate(flips):
            fchip = 2 * fx + fy
            _remote(ps_ref, os_ref.at[fchip], send.at[6 + j], recv.at[6 + j], (fx, fy, c)).wait_recv()
            for b, (half, _) in enumerate(bufs):
                other = dsts[b].at[fchip, pl.ds((1 - c) * half, half), :]
                _remote(other, other, send.at[9 + 2 * j + b], recv.at[9 + 2 * j + b], (x, y, 1 - c)).wait_recv()
        for src, dst, k in own:
            _remote(src, dst, send.at[k], recv.at[k], (x, y, 1 - c)).wait_recv()
        for cp in started:
            cp.wait_send()

    return pl.pallas_call(
        body, name="allgather_weights", in_specs=[ANY, ANY, ANY], out_specs=[ANY, ANY, ANY],
        out_shape=[SDS((N_CHIPS,) + p1.shape, p1.dtype), SDS((N_CHIPS,) + p2.shape, p2.dtype),
                   SDS((N_CHIPS,) + ps.shape, ps.dtype)],
        scratch_shapes=[pltpu.SemaphoreType.DMA((18,)), pltpu.SemaphoreType.DMA((18,))])(p1, p2, ps)


def _swap_sibling_halves(g1, g2):
    h1, h2 = g1.shape[1] // 2, g2.shape[1] // 2

    def body(g1_ref, g2_ref, r1_ref, r2_ref, send, recv):
        x, y, c, _ = _place()
        cps = [_remote(g1_ref.at[:, pl.ds((1 - c) * h1, h1), :], r1_ref, send.at[0], recv.at[0], (x, y, 1 - c)),
               _remote(g2_ref.at[:, pl.ds((1 - c) * h2, h2), :], r2_ref, send.at[1], recv.at[1], (x, y, 1 - c))]
        for cp in cps:
            cp.start()
        for cp in cps:
            cp.wait()

    return pl.pallas_call(
        body, name="swap_sibling_halves", in_specs=[ANY, ANY], out_specs=[ANY, ANY],
        out_shape=[SDS((N_CHIPS, h1, g1.shape[2]), g1.dtype), SDS((N_CHIPS, h2, g2.shape[2]), g2.dtype)],
        scratch_shapes=[pltpu.SemaphoreType.DMA((2,)), pltpu.SemaphoreType.DMA((2,))])(g1, g2)


def _sum_sibling(g, r, place, *, tm):
    n_sh, half, w = r.shape
    nt = half // tm

    def body(place_ref, g_ref, r_ref, pbf_ref, own_ref):
        s = pl.program_id(1)
        p = g_ref[0] + r_ref[0]
        pbf_ref[0] = p.astype(BF)

        @pl.when(s == place_ref[1])
        def _():
            own_ref[...] = p

    return _call(
        body, "sum_sibling", (nt, n_sh),
        [pl.BlockSpec((1, tm, w), lambda i, s, pr: (s, pr[0] * nt + i, 0)),
         pl.BlockSpec((1, tm, w), lambda i, s, pr: (s, i, 0))],
        [pl.BlockSpec((1, tm, w), lambda i, s, pr: (s, i, 0)), pl.BlockSpec((tm, w), lambda i, s, pr: (i, 0))],
        [SDS((n_sh, half, w), BF), SDS((half, w), F32)],
        semantics=("arbitrary", "arbitrary"), prefetch=1)(place, g, r)


def _exchange_chip_partials(p1, p2):
    def body(p1_ref, p2_ref, r1_ref, r2_ref, send, recv):
        _, _, c, flips = _place()
        cps = []
        for j, (fx, fy) in enumerate(flips):
            fchip = 2 * fx + fy
            cps.append(_remote(p1_ref.at[fchip], r1_ref.at[j], send.at[2 * j], recv.at[2 * j], (fx, fy, c)))
            cps.append(_remote(p2_ref.at[fchip], r2_ref.at[j], send.at[2 * j + 1], recv.at[2 * j + 1], (fx, fy, c)))
        for cp in cps:
            cp.start()
        for cp in cps:
            cp.wait()

    return pl.pallas_call(
        body, name="exchange_chip_partials", in_specs=[ANY, ANY], out_specs=[ANY, ANY],
        out_shape=[SDS((3,) + p1.shape[1:], p1.dtype), SDS((3,) + p2.shape[1:], p2.dtype)],
        scratch_shapes=[pltpu.SemaphoreType.DMA((6,)), pltpu.SemaphoreType.DMA((6,))])(p1, p2)


def _sum_chips(own, r, place, *, tm):
    half, w = own.shape

    def body(place_ref, own_ref, r_ref, o_ref):
        o_ref[0] = ((own_ref[...] + r_ref[0].astype(F32)) + r_ref[1].astype(F32)) + r_ref[2].astype(F32)

    return _call(
        body, "sum_chips", (half // tm,),
        [pl.BlockSpec((tm, w), lambda i, pr: (i, 0)), pl.BlockSpec((3, tm, w), lambda i, pr: (0, i, 0))],
        pl.BlockSpec((1, tm, w), lambda i, pr: (pr[0], i, 0)), SDS((2, half, w), F32),
        semantics=("parallel",), prefetch=1)(place, own, r)


def _share_with_sibling(f1, f2):
    def body(_, __, o1_ref, o2_ref, send, recv):
        x, y, c, _ = _place()
        cps = [_remote(o1_ref.at[c], o1_ref.at[c], send.at[0], recv.at[0], (x, y, 1 - c)),
               _remote(o2_ref.at[c], o2_ref.at[c], send.at[1], recv.at[1], (x, y, 1 - c))]
        for cp in cps:
            cp.start()
        for cp in cps:
            cp.wait_send()
        _remote(o1_ref.at[1 - c], o1_ref.at[1 - c], send.at[0], recv.at[0], (x, y, 1 - c)).wait_recv()
        _remote(o2_ref.at[1 - c], o2_ref.at[1 - c], send.at[1], recv.at[1], (x, y, 1 - c)).wait_recv()

    return pl.pallas_call(
        body, name="share_with_sibling", in_specs=[ANY, ANY], out_specs=[ANY, ANY],
        out_shape=[SDS(f1.shape, f1.dtype), SDS(f2.shape, f2.dtype)], input_output_aliases={0: 0, 1: 1},
        scratch_shapes=[pltpu.SemaphoreType.DMA((2,)), pltpu.SemaphoreType.DMA((2,))])(f1, f2)


def _allreduce_small(sg):
    rows, w = sg.shape

    def body(sg_ref, o_ref, slots, send, recv):
        x, y, c, _ = _place()
        me = 4 * x + 2 * y + c
        slots[me] = sg_ref[...]
        peers = []
        for k in range(1, 8):
            a, b, e = (k >> 2) & 1, (k >> 1) & 1, k & 1
            px = x + a - 2 * a * x
            py = y + b - 2 * b * y
            pc = c + e - 2 * e * c
            cp = _remote(sg_ref, slots.at[me], send.at[k - 1], recv.at[k - 1], (px, py, pc))
            cp.start()
            peers.append((cp, 4 * px + 2 * py + pc, (px, py, pc)))
        for k, (cp, pidx, pid) in enumerate(peers):
            _remote(sg_ref, slots.at[pidx], send.at[k], recv.at[k], pid).wait_recv()
        acc = slots[0]
        for dev in range(1, 8):
            acc = acc + slots[dev]
        o_ref[...] = acc
        for cp, _, _ in peers:
            cp.wait_send()

    vm = pl.BlockSpec(memory_space=pltpu.VMEM)
    return pl.pallas_call(
        body, name="allreduce_small", in_specs=[vm], out_specs=vm, out_shape=SDS((rows, w), F32),
        scratch_shapes=[pltpu.VMEM((8, rows, w), F32), pltpu.SemaphoreType.DMA((7,)),
                        pltpu.SemaphoreType.DMA((7,))],
        compiler_params=pltpu.CompilerParams(vmem_limit_bytes=VMEM_LIMIT_MB << 20))(sg)


def _adamw(g_arr, g_off, w, m, v, *, name, tm):
    rows, cols = w.shape
    assert g_off % tm == 0 and rows % tm == 0
    c1 = 1.0 - ADAM_B1 ** ADAM_STEP
    c2 = 1.0 - ADAM_B2 ** ADAM_STEP

    def body(g_ref, w_ref, m_ref, v_ref, go_ref, d_ref, mo_ref, vo_ref):
        g = g_ref[...]
        mn = ADAM_B1 * m_ref[...] + (1.0 - ADAM_B1) * g
        vn = ADAM_B2 * v_ref[...] + (1.0 - ADAM_B2) * (g * g)
        go_ref[...] = g
        mo_ref[...] = mn
        vo_ref[...] = vn
        d_ref[...] = -ADAM_LR * ((mn / c1) / (jnp.sqrt(vn / c2) + ADAM_EPS) + ADAM_WD * w_ref[...])

    blk = pl.BlockSpec((tm, cols), lambda i: (i, 0))
    return _call(
        body, name, (rows // tm,),
        [pl.BlockSpec((tm, cols), lambda i: (g_off // tm + i, 0)), blk, blk, blk], [blk] * 4,
        [SDS((rows, cols), F32)] * 4, semantics=("parallel",))(g_arr, w, m, v)


def _pack_small(parts, width):
    flat = jnp.concatenate([p.reshape(-1).astype(F32) for p in parts])
    rows = -(-flat.shape[0] // (8 * width)) * 8
    return jnp.pad(flat, (0, rows * width - flat.shape[0])).reshape(rows, width)


def _unpack_small(packed, shapes):
    flat = packed.reshape(-1)
    out, off = [], 0
    for shp in shapes:
        size = 1
        for n in shp:
            size *= n
        out.append(flat[off:off + size].reshape(shp))
        off += size
    return out


def _block_diag(pw):
    g, c, _ = pw.shape
    eye = jnp.eye(g, dtype=pw.dtype)
    return (eye[:, None, :, None] * pw[:, :, None, :]).reshape(g * c, g * c)


def _diag_blocks(full, g):
    c = full.shape[0] // g
    return jnp.stack([full[i * c:(i + 1) * c, i * c:(i + 1) * c] for i in range(g)])


def kernel(x, mem, norm_mix, w_in, pool_w, pool_scale, kv_norm, w_kv, k_norm, q_norm, sinks, mem_norm, w_mem_kv, mem_q_norm, mem_k_norm, w_out, norm_mlp, w_up, w_down, loss_target, m_norm_mix, m_w_in, m_pool_w, m_pool_scale, m_kv_norm, m_w_kv, m_k_norm, m_q_norm, m_sinks, m_mem_norm, m_w_mem_kv, m_mem_q_norm, m_mem_k_norm, m_w_out, m_norm_mlp, m_w_up, m_w_down, v_norm_mix, v_w_in, v_pool_w, v_pool_scale, v_kv_norm, v_w_kv, v_k_norm, v_q_norm, v_sinks, v_mem_norm, v_w_mem_kv, v_mem_q_norm, v_mem_k_norm, v_w_out, v_norm_mlp, v_w_up, v_w_down):
    s_len, d = x.shape[1], x.shape[2]
    n_layers, n_pool = norm_mix.shape[0], pool_w.shape[0]
    n_swa = n_layers - n_pool
    main = d - KVW
    qh = main // HEAD
    ff = w_down.shape[1] * N_CHIPS
    dq = d // N_CHIPS
    assert w_up.shape[2] == d and ff == N_CHIPS * d and w_kv.shape[1] == 2 * KVW
    tm = min(512, s_len)
    tm_mem = mem.shape[1]

    cx, cy, cc = lax.axis_index("x"), lax.axis_index("y"), lax.axis_index("c")
    chip = 2 * cx + cy

    off_down, off_up, off_in, off_out = 0, n_layers * d, 2 * n_layers * d, 2 * n_layers * d + n_layers * dq
    rows1 = off_out + n_layers * dq
    off_mkv, off_kv = 0, n_layers * dq
    rows2 = off_kv + dq

    p1 = jnp.concatenate([w_down.reshape(-1, d), w_up.reshape(-1, d), w_in.reshape(-1, d), w_out.reshape(-1, d)]
                         ).astype(BF)
    p2 = jnp.concatenate([w_mem_kv.reshape(-1, 2 * KVW), w_kv]).astype(BF)
    ps = jnp.pad(pool_scale, ((0, 8 - n_pool), (0, 2 * LANE - pool_scale.shape[1])))
    wg1, wg2, psg = _allgather_weights(p1, p2, ps)
    pool_scale_full = jnp.concatenate([psg[k, :n_pool, :pool_scale.shape[1]] for k in range(N_CHIPS)], axis=1)

    def w_rows(arr, off, nrows, width):
        assert off % nrows == 0
        return (arr, (N_CHIPS, nrows, width), lambda j: (0, off // nrows, 0))

    def w_cols(arr, off, nrows, width):
        assert off % nrows == 0
        return (arr, (1, nrows, width), lambda j: (j, off // nrows, 0))

    w_in_l = [w_rows(wg1, off_in + l * dq, dq, d) for l in range(n_layers)]
    w_out_l = [w_rows(wg1, off_out + l * dq, dq, d) for l in range(n_layers)]
    w_down_l = [w_rows(wg1, off_down + l * d, d, d) for l in range(n_layers)]
    w_down_cols_l = [w_cols(wg1, off_down + l * d, d, d) for l in range(n_layers)]
    w_up_l = [w_cols(wg1, off_up + l * d, d, d) for l in range(n_layers)]
    w_up_all_l = [(wg1, (N_CHIPS, d, d), (lambda l: lambda j: (0, (off_up + l * d) // d, 0))(l))
                  for l in range(n_layers)]
    w_mkv_l = [w_rows(wg2, off_mkv + l * dq, dq, 2 * KVW) for l in range(n_layers)]
    w_kv_g = w_rows(wg2, off_kv, dq, 2 * KVW)

    row = lambda a: a.reshape(1, -1)
    h = x.reshape(s_len, d)
    memx = mem.reshape(tm_mem, d)
    tgt = loss_target.reshape(s_len, d)
    pbd = [_block_diag(pool_w[l]).astype(BF) for l in range(n_pool)]
    sinks_pad = [jnp.pad(row(sinks[j]), ((0, 0), (0, LANE - qh))) for j in range(n_swa)]

    saved = []
    kv = hn_kv = kn = vsh = None
    for l in range(n_layers):
        if l == n_pool:
            kv, hn_kv = _norm_mm(h, row(kv_norm), w_kv_g, 1, 2 * KVW, act=False, name="kv_proj", tm=tm)
            kn, vsh = _kv_prep(kv, row(k_norm), tm=tm)
        h0 = h
        proj, xn = _norm_mm(h0, row(norm_mix[l]), w_in_l[l], 1, d, act=False, name=f"in_proj_{l}", tm=tm)
        mkv, memn = _norm_mm(memx, row(mem_norm[l]), w_mkv_l[l], 1, 2 * KVW, act=False, name=f"mem_kv_{l}", tm=tm_mem)
        if l < n_pool:
            cat = _mixer_pool_fwd(proj, mkv, pbd[l], row(pool_scale_full[l]), row(mem_q_norm[l]), row(mem_k_norm[l]),
                                  name=f"mixer_fwd_{l}", tm=tm)
        else:
            j = l - n_pool
            cat = _mixer_swa_fwd(proj, kn, vsh, mkv, row(q_norm[j]), sinks_pad[j], row(mem_q_norm[l]),
                                 row(mem_k_norm[l]), name=f"mixer_fwd_{l}")
        h1 = _mm_res(h0, cat, w_out_l[l], name=f"out_proj_{l}", tm=tm)
        hh, xm = _norm_mm(h1, row(norm_mlp[l]), w_up_l[l], N_CHIPS, d, act=True, name=f"mlp_up_{l}", tm=tm)
        h = _mm_res(h1, hh, w_down_l[l], name=f"mlp_down_{l}", tm=tm)
        saved.append((h0, proj, xn, mkv, memn, cat, h1, hh, xm))

    dh, dh_bf, loss_part = _loss_head(h, tgt, tm=tm)
    loss = lax.psum(loss_part[0, 0], ("x", "y", "c"))

    g1 = lax.empty((N_CHIPS, rows1, d), F32)
    g2 = lax.empty((N_CHIPS, rows2, 2 * KVW), F32)
    tk = min(512, d)
    zeros_mem = jnp.zeros((tm_mem, d), F32)

    def rows_map(off, nrows, tkk):
        per = nrows // tkk
        return lambda i, j: (i // per, off // tkk + i % per, 0)

    def cols_map(off, tkk):
        return lambda i, j: (j, off // tkk + i, 0)

    d_norm_mix, d_norm_mlp, d_mem_norm = [None] * n_layers, [None] * n_layers, [None] * n_layers
    d_mem_q, d_mem_k = [None] * n_layers, [None] * n_layers
    d_pool_w, d_pool_scale = [None] * n_pool, [None] * n_pool
    d_q_norm, d_sinks = [None] * n_swa, [None] * n_swa
    dks, dvs = [], []
    d_kv_norm = d_k_norm = None
    for l in reversed(range(n_layers)):
        h0, proj, xn, mkv, memn, cat, h1, hh, xm = saved[l]
        g1 = _mm_tn(hh, dh_bf, g1, rows_map(off_down + l * d, d, tk), tk, d, name=f"dw_down_{l}")
        du = _mm_nt_relu2(dh_bf, hh, w_down_cols_l[l], N_CHIPS, name=f"d_mlp_act_{l}", tm=tm)
        g1 = _mm_tn(xm, du, g1, cols_map(off_up + l * d, tk), tk, d, name=f"dw_up_{l}")
        dh1, dh1_bf, d_norm_mlp[l] = _mm_nt_normbwd(du, w_up_all_l[l], N_CHIPS, h1, row(norm_mlp[l]), dh,
                                                    name=f"d_mlp_in_{l}", tm=tm)
        tkq = min(tk, dq)
        g1 = _mm_tn(cat, dh1_bf, g1, rows_map(off_out + l * dq, dq, tkq), tkq, d, name=f"dw_out_{l}")
        dcat = _mm_nt(dh1_bf, w_out_l[l], d, name=f"d_cat_{l}", tm=tm)
        if l < n_pool:
            dproj, dpbd, dscale, dmkv, d_mem_q[l], d_mem_k[l] = _mixer_pool_bwd(
                proj, dcat, mkv, pbd[l], row(pool_scale_full[l]), row(mem_q_norm[l]), row(mem_k_norm[l]),
                name=f"mixer_bwd_{l}", tm=tm)
            d_pool_w[l] = _diag_blocks(dpbd, len(POOL_WINDOWS))
            d_pool_scale[l] = dscale
        else:
            j = l - n_pool
            dproj, dk, dv, dmkv, d_q_norm[j], dsk, d_mem_q[l], d_mem_k[l] = _mixer_swa_bwd(
                proj, dcat, kn, vsh, mkv, row(q_norm[j]), sinks_pad[j], row(mem_q_norm[l]), row(mem_k_norm[l]),
                name=f"mixer_bwd_{l}")
            d_sinks[j] = dsk[0, :qh]
            dks.append(dk)
            dvs.append(dv)
        g1 = _mm_tn(xn, dproj, g1, rows_map(off_in + l * dq, dq, tkq), tkq, d, name=f"dw_in_{l}")
        dh, dh_bf, d_norm_mix[l] = _mm_nt_normbwd(dproj, w_in_l[l], 1, h0, row(norm_mix[l]), dh1,
                                                  name=f"d_in_{l}", tm=tm)
        g2 = _mm_tn(memn, dmkv, g2, rows_map(off_mkv + l * dq, dq, tkq), tkq, 2 * KVW, name=f"dw_mem_kv_{l}")
        _, _, d_mem_norm[l] = _mm_nt_normbwd(dmkv, w_mkv_l[l], 1, memx, row(mem_norm[l]), zeros_mem,
                                             name=f"d_mem_norm_{l}", tm=tm_mem)
        if l == n_pool:
            dkv, d_k_norm = _kv_bwd(kv, dks, dvs, row(k_norm), tm=tm)
            g2 = _mm_tn(hn_kv, dkv, g2, rows_map(off_kv, dq, tkq), tkq, 2 * KVW, name="dw_kv")
            dh, dh_bf, d_kv_norm = _mm_nt_normbwd(dkv, w_kv_g, 1, h0, row(kv_norm), dh, name="d_kv_in", tm=tm)
    grad_x = dh.reshape(x.shape)

    place = jnp.stack([cc, chip]).astype(jnp.int32)
    r1, r2 = _swap_sibling_halves(g1, g2)
    pbf1, own1 = _sum_sibling(g1, r1, place, tm=_tile(rows1 // 2, 256))
    pbf2, own2 = _sum_sibling(g2, r2, place, tm=_tile(rows2 // 2, 256))
    x1, x2 = _exchange_chip_partials(pbf1, pbf2)
    red1 = _sum_chips(own1, x1, place, tm=_tile(rows1 // 2, 256))
    red2 = _sum_chips(own2, x2, place, tm=_tile(rows2 // 2, 256))
    full1, full2 = _share_with_sibling(red1, red2)
    full1 = full1.reshape(rows1, d)
    full2 = full2.reshape(rows2, 2 * KVW)

    big = {}
    for name, arr, off, w_, m_, v_ in (
            ("w_down", full1, off_down, w_down, m_w_down, v_w_down), ("w_up", full1, off_up, w_up, m_w_up, v_w_up),
            ("w_in", full1, off_in, w_in, m_w_in, v_w_in), ("w_out", full1, off_out, w_out, m_w_out, v_w_out),
            ("w_mem_kv", full2, off_mkv, w_mem_kv, m_w_mem_kv, v_w_mem_kv),
            ("w_kv", full2, off_kv, w_kv, m_w_kv, v_w_kv)):
        cols = arr.shape[1]
        res = _adamw(arr, off, w_.reshape(-1, cols), m_.reshape(-1, cols), v_.reshape(-1, cols),
                     name=f"adamw_{name}", tm=min(256, dq))
        big[name] = [r.reshape(w_.shape) for r in res]

    small_names = ["norm_mix", "pool_w", "pool_scale", "kv_norm", "k_norm", "q_norm", "sinks", "mem_norm",
                   "mem_q_norm", "mem_k_norm", "norm_mlp"]
    small_grads = {
        "norm_mix": jnp.concatenate(d_norm_mix), "pool_w": jnp.stack(d_pool_w),
        "pool_scale": jnp.concatenate(d_pool_scale), "kv_norm": d_kv_norm[0], "k_norm": d_k_norm[0],
        "q_norm": jnp.concatenate(d_q_norm), "sinks": jnp.stack(d_sinks), "mem_norm": jnp.concatenate(d_mem_norm),
        "mem_q_norm": jnp.concatenate(d_mem_q), "mem_k_norm": jnp.concatenate(d_mem_k),
        "norm_mlp": jnp.concatenate(d_norm_mlp)}
    width = d
    sg = _pack_small([small_grads[n] for n in small_names], width)
    sg = _allreduce_small(sg)
    reduced = dict(zip(small_names, _unpack_small(sg, [small_grads[n].shape for n in small_names])))
    psw = pool_scale.shape[1]
    reduced["pool_scale"] = lax.dynamic_slice_in_dim(reduced["pool_scale"], chip * psw, psw, axis=1)
    params = dict(norm_mix=(norm_mix, m_norm_mix, v_norm_mix), pool_w=(pool_w, m_pool_w, v_pool_w),
                  pool_scale=(pool_scale, m_pool_scale, v_pool_scale), kv_norm=(kv_norm, m_kv_norm, v_kv_norm),
                  k_norm=(k_norm, m_k_norm, v_k_norm), q_norm=(q_norm, m_q_norm, v_q_norm),
                  sinks=(sinks, m_sinks, v_sinks), mem_norm=(mem_norm, m_mem_norm, v_mem_norm),
                  mem_q_norm=(mem_q_norm, m_mem_q_norm, v_mem_q_norm),
                  mem_k_norm=(mem_k_norm, m_mem_k_norm, v_mem_k_norm), norm_mlp=(norm_mlp, m_norm_mlp, v_norm_mlp))
    shapes = [params[n][0].shape for n in small_names]
    packs = [_pack_small([reduced[n].reshape(params[n][0].shape) for n in small_names], width)]
    packs += [_pack_small([params[n][t] for n in small_names], width) for t in range(3)]
    res = _adamw(packs[0], 0, packs[1], packs[2], packs[3], name="adamw_small", tm=8)
    small = {n: [] for n in small_names}
    for r in res:
        for n, a in zip(small_names, _unpack_small(r, shapes)):
            small[n].append(a)

    order = ["norm_mix", "w_in", "pool_w", "pool_scale", "kv_norm", "w_kv", "k_norm", "q_norm", "sinks", "mem_norm",
             "w_mem_kv", "mem_q_norm", "mem_k_norm", "w_out", "norm_mlp", "w_up", "w_down"]
    out = {**big, **small}
    return (loss, grad_x, *[out[n][0] for n in order], *[out[n][1] for n in order],
            *[out[n][2] for n in order], *[out[n][3] for n in order])
```

```python
import functools

import jax
import jax.numpy as jnp
from jax import lax
from jax.experimental import pallas as pl
from jax.experimental.pallas import tpu as pltpu

F32, BF = jnp.float32, jnp.bfloat16
SDS = jax.ShapeDtypeStruct
MESH = pl.DeviceIdType.MESH
ANY = pl.BlockSpec(memory_space=pl.ANY)
HBM = pl.BlockSpec(memory_space=pltpu.HBM)
SEM = pl.BlockSpec(memory_space=pltpu.SEMAPHORE)
VMEM_WHOLE = pl.BlockSpec(memory_space=pltpu.VMEM)
SIDE_EFFECT = pltpu.SideEffectType.DATAFLOW_SIDE_EFFECTING


def _in_hbm(a):
    return pltpu.with_memory_space_constraint(a, pltpu.HBM)


EPS = 1e-6
HEAD = 64
KV_HEADS = 4
KVW = KV_HEADS * HEAD
WINDOW = 128
POOL_WINDOWS = (2, 4, 8, 16)
HALO = 16
QK_SCALE = HEAD ** -0.5
NEG = float(jnp.finfo(jnp.float32).min)
N_CHIPS = 4
LANE = 128

ADAM_LR, ADAM_B1, ADAM_B2, ADAM_EPS, ADAM_WD, ADAM_STEP = 0.001, 0.9, 0.999, 1e-08, 0.01, 10

VMEM_LIMIT_MB = 56


def _call(body, name, grid, in_specs, out_specs, out_shape, *, scratch=(), semantics=None, aliases=None,
          prefetch=0):
    params = pltpu.CompilerParams(dimension_semantics=semantics, vmem_limit_bytes=VMEM_LIMIT_MB << 20)
    if prefetch:
        spec = pltpu.PrefetchScalarGridSpec(num_scalar_prefetch=prefetch, grid=grid, in_specs=in_specs,
                                            out_specs=out_specs, scratch_shapes=list(scratch))
        return pl.pallas_call(body, name=name, grid_spec=spec, out_shape=out_shape,
                              input_output_aliases=aliases or {}, compiler_params=params)
    return pl.pallas_call(body, name=name, grid=grid, in_specs=in_specs, out_specs=out_specs, out_shape=out_shape,
                          scratch_shapes=list(scratch), input_output_aliases=aliases or {}, compiler_params=params)


def _tile(n, pref):
    return max(t for t in range(8, min(n, pref) + 1, 8) if n % t == 0)


def _dot(a, b):
    return jnp.dot(a, b, preferred_element_type=F32)


def _dot_nt(a, b):
    return lax.dot_general(a, b, (((1,), (1,)), ((), ())), preferred_element_type=F32)


def _dot_tn(a, b):
    return lax.dot_general(a, b, (((0,), (0,)), ((), ())), preferred_element_type=F32)


def _rms(x):
    r = lax.rsqrt(jnp.mean(x * x, axis=-1, keepdims=True) + EPS)
    return x * r, r


def _rms_bwd(dy, xh, r, g):
    dg = jnp.sum(dy * xh, axis=0, keepdims=True)
    dyg = dy * g
    dx = r * (dyg - xh * jnp.mean(dyg * xh, axis=-1, keepdims=True))
    return dx, dg


def _norm_mm(h, g, w, nj, tn, *, act, name, tm):
    w_arr, w_block, w_imap = w
    rows, d = h.shape

    def body(h_ref, g_ref, w_ref, y_ref, xn_ref):
        @pl.when(pl.program_id(1) == 0)
        def _():
            xh, _ = _rms(h_ref[...])
            xn_ref[...] = (xh * g_ref[...]).astype(BF)

        u = _dot(xn_ref[...], w_ref[...].reshape(d, tn))
        if act:
            a = jnp.maximum(u, 0.0)
            y_ref[...] = (a * a).astype(BF)
        else:
            y_ref[...] = u

    return _call(
        body, name, (rows // tm, nj),
        [pl.BlockSpec((tm, d), lambda i, j: (i, 0)), pl.BlockSpec((1, d), lambda i, j: (0, 0)),
         pl.BlockSpec(w_block, lambda i, j: w_imap(j))],
        [pl.BlockSpec((tm, tn), lambda i, j: (i, j)), pl.BlockSpec((tm, d), lambda i, j: (i, 0))],
        [SDS((rows, nj * tn), BF if act else F32), SDS((rows, d), BF)],
        semantics=("parallel", "arbitrary"))(h, g, w_arr)


def _mm_res(res, a, w, *, name, tm):
    w_arr, w_block, w_imap = w
    rows, k = a.shape
    n = res.shape[1]

    def body(res_ref, a_ref, w_ref, o_ref):
        o_ref[...] = res_ref[...] + _dot(a_ref[...], w_ref[...].reshape(k, n))

    return _call(
        body, name, (rows // tm,),
        [pl.BlockSpec((tm, n), lambda i: (i, 0)), pl.BlockSpec((tm, k), lambda i: (i, 0)),
         pl.BlockSpec(w_block, lambda i: w_imap(0))],
        pl.BlockSpec((tm, n), lambda i: (i, 0)), SDS((rows, n), F32), semantics=("parallel",))(res, a, w_arr)


def _mm_nt(dy, w, k, *, name, tm):
    w_arr, w_block, w_imap = w
    rows, n = dy.shape

    def body(dy_ref, w_ref, o_ref):
        o_ref[...] = _dot_nt(dy_ref[...], w_ref[...].reshape(k, n))

    return _call(
        body, name, (rows // tm,),
        [pl.BlockSpec((tm, n), lambda i: (i, 0)), pl.BlockSpec(w_block, lambda i: w_imap(0))],
        pl.BlockSpec((tm, k), lambda i: (i, 0)), SDS((rows, k), F32), semantics=("parallel",))(dy, w_arr)


def _mm_nt_relu2(dh, hh, w, nj, *, name, tm):
    w_arr, w_block, w_imap = w
    rows, d = dh.shape
    tk = hh.shape[1] // nj

    def body(dh_ref, hh_ref, w_ref, o_ref):
        dhh = _dot_nt(dh_ref[...], w_ref[...].reshape(tk, d))
        o_ref[...] = (dhh * (2.0 * jnp.sqrt(hh_ref[...].astype(F32)))).astype(BF)

    return _call(
        body, name, (rows // tm, nj),
        [pl.BlockSpec((tm, d), lambda i, j: (i, 0)), pl.BlockSpec((tm, tk), lambda i, j: (i, j)),
         pl.BlockSpec(w_block, lambda i, j: w_imap(j))],
        pl.BlockSpec((tm, tk), lambda i, j: (i, j)), SDS((rows, nj * tk), BF),
        semantics=("parallel", "parallel"))(dh, hh, w_arr)


def _mm_nt_normbwd(dy, w, nsplit, h, g, dres, *, name, tm):
    w_arr, w_block, w_imap = w
    rows, n = dy.shape
    d = h.shape[1]
    ns = n // nsplit

    def body(dy_ref, w_ref, h_ref, g_ref, dres_ref, o_ref, obf_ref, dg_ref):
        if nsplit == 1:
            dxn = _dot_nt(dy_ref[...].astype(BF), w_ref[...].reshape(d, n))
        else:
            dxn = _dot_nt(dy_ref[:, 0:ns].astype(BF), w_ref[0])
            for s in range(1, nsplit):
                dxn += _dot_nt(dy_ref[:, s * ns:(s + 1) * ns].astype(BF), w_ref[s])
        xh, r = _rms(h_ref[...])
        dx, dg = _rms_bwd(dxn, xh, r, g_ref[...])
        out = dres_ref[...] + dx
        o_ref[...] = out
        obf_ref[...] = out.astype(BF)

        @pl.when(pl.program_id(0) == 0)
        def _():
            dg_ref[...] = jnp.zeros_like(dg_ref)

        dg_ref[...] += dg

    row = lambda i: (i, 0)
    return _call(
        body, name, (rows // tm,),
        [pl.BlockSpec((tm, n), row), pl.BlockSpec(w_block, lambda i: w_imap(0)), pl.BlockSpec((tm, d), row),
         pl.BlockSpec((1, d), lambda i: (0, 0)), pl.BlockSpec((tm, d), row)],
        [pl.BlockSpec((tm, d), row), pl.BlockSpec((tm, d), row), pl.BlockSpec((1, d), lambda i: (0, 0))],
        [SDS((rows, d), F32), SDS((rows, d), BF), SDS((1, d), F32)],
        semantics=("arbitrary",))(dy, w_arr, h, g, dres)


def _mm_tn(x, dy, packed, out_imap, tk, tn, *, name):
    s_len, k = x.shape
    n = dy.shape[1]

    def body(x_ref, dy_ref, _, o_ref):
        o_ref[0] = _dot_tn(x_ref[...], dy_ref[...].astype(BF))

    return _call(
        body, name, (k // tk, n // tn),
        [pl.BlockSpec((s_len, tk), lambda i, j: (0, i)), pl.BlockSpec((s_len, tn), lambda i, j: (0, j)), ANY],
        pl.BlockSpec((1, tk, tn), out_imap), SDS(packed.shape, packed.dtype),
        semantics=("parallel", "parallel"), aliases={2: 0})(x, dy, packed)


def _loss_head(y, tgt, *, tm):
    rows, d = y.shape

    def body(y_ref, t_ref, dh_ref, dhbf_ref, loss_ref):
        err = y_ref[...] - t_ref[...]
        dh = err * (1.0 / d)
        dh_ref[...] = dh
        dhbf_ref[...] = dh.astype(BF)

        @pl.when(pl.program_id(0) == 0)
        def _():
            loss_ref[...] = jnp.zeros_like(loss_ref)

        loss_ref[...] += 0.5 * jnp.sum(jnp.mean(err * err, axis=-1, keepdims=True), axis=0, keepdims=True)

    row = lambda i: (i, 0)
    return _call(
        body, "loss_head", (rows // tm,), [pl.BlockSpec((tm, d), row), pl.BlockSpec((tm, d), row)],
        [pl.BlockSpec((tm, d), row), pl.BlockSpec((tm, d), row), pl.BlockSpec((1, 1), lambda i: (0, 0))],
        [SDS((rows, d), F32), SDS((rows, d), BF), SDS((1, 1), F32)], semantics=("arbitrary",))(y, tgt)


def _hs(h):
    return slice(HEAD * h, HEAD * (h + 1))


def _softmax_rows(s):
    e = jnp.exp(s - jnp.max(s, axis=-1, keepdims=True))
    return e / jnp.sum(e, axis=-1, keepdims=True)


def _mem_fwd(mq, mk, mv, gq):
    outs = []
    for h in range(KV_HEADS):
        xh, _ = _rms(mq[:, _hs(h)])
        qn = (xh * gq).astype(BF)
        p = _softmax_rows(_dot_nt(qn, mk[:, _hs(h)]) * QK_SCALE)
        outs.append(_dot(p.astype(BF), mv[:, _hs(h)]))
    return jnp.concatenate(outs, axis=-1)


def _mem_bwd(mq, do, mk, mv, gq):
    dqs, dks, dvs, dgq = [], [], [], 0.0
    for h in range(KV_HEADS):
        xh, r = _rms(mq[:, _hs(h)])
        qn = (xh * gq).astype(BF)
        p = _softmax_rows(_dot_nt(qn, mk[:, _hs(h)]) * QK_SCALE)
        doh = do[:, _hs(h)].astype(BF)
        dp = _dot_nt(doh, mv[:, _hs(h)])
        ds = (p * (dp - jnp.sum(p * dp, axis=-1, keepdims=True)) * QK_SCALE).astype(BF)
        dq, dg = _rms_bwd(_dot(ds, mk[:, _hs(h)]), xh, r, gq)
        dqs.append(dq)
        dgq = dgq + dg
        dks.append(_dot_tn(ds, qn))
        dvs.append(_dot_tn(p.astype(BF), doh))
    cat = lambda xs: jnp.concatenate(xs, axis=-1)
    return cat(dqs), cat(dks), cat(dvs), dgq


def _mem_kv(mkv, gk):
    ks = []
    for h in range(KV_HEADS):
        xh, _ = _rms(mkv[:, _hs(h)])
        ks.append(xh * gk)
    return jnp.concatenate(ks, axis=-1).astype(BF), mkv[:, KVW:].astype(BF)


def _mem_kv_bwd(mkv, dmk, dmv, gk):
    dxs, dgk = [], 0.0
    for h in range(KV_HEADS):
        xh, r = _rms(mkv[:, _hs(h)])
        dx, dg = _rms_bwd(dmk[:, _hs(h)], xh, r, gk)
        dxs.append(dx)
        dgk = dgk + dg
    return jnp.concatenate(dxs + [dmv], axis=-1), dgk


def _pool_select(col, gd, a2, a4, a8, a16):
    return jnp.where(col < gd, a2, jnp.where(col < 2 * gd, a4, jnp.where(col < 3 * gd, a8, a16)))


def _pool_count(t0, shape, gd):
    col = lax.broadcasted_iota(jnp.int32, shape, 1)
    t = t0 + lax.broadcasted_iota(jnp.int32, shape, 0)
    win = _pool_select(col, gd, *POOL_WINDOWS)
    return jnp.minimum(t + 1, win).astype(F32)


def _pool_diff(u, halo, t0, gd):
    c = jnp.concatenate([halo, u], axis=0)
    s2 = c + pltpu.roll(c, 1, 0)
    s4 = s2 + pltpu.roll(s2, 2, 0)
    s8 = s4 + pltpu.roll(s4, 4, 0)
    s16 = s8 + pltpu.roll(s8, 8, 0)
    col = lax.broadcasted_iota(jnp.int32, c.shape, 1)
    ws = _pool_select(col, gd, s2, s4, s8, s16)[HALO:]
    return ws / _pool_count(t0, u.shape, gd) - u


def _pool_diff_bwd(dd, dd_halo, t0, gd):
    t = dd.shape[0]
    z = jnp.concatenate([dd / _pool_count(t0, dd.shape, gd), dd_halo / _pool_count(t0 + t, dd_halo.shape, gd)], axis=0)
    n = z.shape[0]
    f2 = z + pltpu.roll(z, n - 1, 0)
    f4 = f2 + pltpu.roll(f2, n - 2, 0)
    f8 = f4 + pltpu.roll(f4, n - 4, 0)
    f16 = f8 + pltpu.roll(f8, n - 8, 0)
    col = lax.broadcasted_iota(jnp.int32, z.shape, 1)
    return _pool_select(col, gd, f2, f4, f8, f16)[:t] - dd


def _swa_bias(n):
    qi = lax.broadcasted_iota(jnp.int32, (WINDOW, 2 * WINDOW), 0)
    kj = lax.broadcasted_iota(jnp.int32, (WINDOW, 2 * WINDOW), 1)
    dist = qi + WINDOW - kj
    valid = (dist >= 0) & (dist < WINDOW) & ((kj >= WINDOW) | (n > 0))
    return dist.astype(F32), valid


def _slopes(qh):
    return [2.0 ** (-8.0 * (h + 1) / qh) for h in range(qh)]


def _swa_probs(qn, kk, dist, valid, slope, sink):
    s = _dot_nt(qn, kk) * QK_SCALE - slope * dist
    s = jnp.where(valid, s, NEG)
    m = jnp.maximum(jnp.max(s, axis=-1, keepdims=True), sink)
    e = jnp.exp(s - m)
    es = jnp.exp(sink - m)
    z = jnp.sum(e, axis=-1, keepdims=True) + es
    return e / z, es / z


def _swa_group(q, kh, grp, n, qh, sinks):
    heads = range(kh * grp, (kh + 1) * grp)
    dist, valid = _swa_bias(n)
    slopes = _slopes(qh)
    rows = lambda vals: jnp.concatenate([jnp.broadcast_to(v, (WINDOW, 1)) for v in vals], axis=0)
    qs = jnp.concatenate([q[:, _hs(h)] for h in heads], axis=0)
    slope = rows([jnp.full((1, 1), slopes[h], F32) for h in heads])
    sink = rows([sinks[:, h:h + 1] for h in heads])
    return qs, slope, sink, jnp.concatenate([dist] * grp, axis=0), jnp.concatenate([valid] * grp, axis=0)


def _swa_fwd(q, kk, vv, gq, sinks, n, qh):
    grp = qh // KV_HEADS
    outs = []
    for kh in range(KV_HEADS):
        qs, slope, sink, dist, valid = _swa_group(q, kh, grp, n, qh, sinks)
        xh, _ = _rms(qs)
        p, _ = _swa_probs((xh * gq).astype(BF), kk[:, _hs(kh)], dist, valid, slope, sink)
        o = _dot(p.astype(BF), vv[:, _hs(kh)])
        outs += [o[g * WINDOW:(g + 1) * WINDOW] for g in range(grp)]
    return jnp.concatenate(outs, axis=-1)


def _swa_bwd(q, do, kk, vv, gq, sinks, n, qh):
    grp = qh // KV_HEADS
    lane = lax.broadcasted_iota(jnp.int32, (1, LANE), 1)
    dqs, dks, dvs, dgq, dsk = [], [], [], 0.0, jnp.zeros((1, LANE), F32)
    for kh in range(KV_HEADS):
        qs, slope, sink, dist, valid = _swa_group(q, kh, grp, n, qh, sinks)
        xh, r = _rms(qs)
        qn = (xh * gq).astype(BF)
        p, ps = _swa_probs(qn, kk[:, _hs(kh)], dist, valid, slope, sink)
        dos = jnp.concatenate([do[:, _hs(h)] for h in range(kh * grp, (kh + 1) * grp)], axis=0).astype(BF)
        dp = _dot_nt(dos, vv[:, _hs(kh)])
        delta = jnp.sum(p * dp, axis=-1, keepdims=True)
        ds = (p * (dp - delta) * QK_SCALE).astype(BF)
        dsink = ps * delta
        for g in range(grp):
            part = -jnp.sum(dsink[g * WINDOW:(g + 1) * WINDOW], axis=0, keepdims=True)
            dsk = dsk + jnp.where(lane == kh * grp + g, part, 0.0)
        dq, dg = _rms_bwd(_dot(ds, kk[:, _hs(kh)]), xh, r, gq)
        dqs += [dq[g * WINDOW:(g + 1) * WINDOW] for g in range(grp)]
        dgq = dgq + dg
        dks.append(_dot_tn(ds, qn))
        dvs.append(_dot_tn(p.astype(BF), dos))
    cat = lambda xs: jnp.concatenate(xs, axis=-1)
    return cat(dqs), cat(dks), cat(dvs), dgq, dsk


def _mixer_pool_fwd(proj, mkv, pbd, scale, gq, gk, *, name, tm):
    s_len, d = proj.shape
    main = d - KVW
    gd = main // len(POOL_WINDOWS)
    mlen = mkv.shape[0]
    hb = tm // HALO

    def body(u_ref, halo_ref, mq_ref, mkv_ref, pbd_ref, scale_ref, gq_ref, gk_ref, o_ref, mk_s, mv_s):
        i = pl.program_id(0)

        @pl.when(i == 0)
        def _():
            mk, mv = _mem_kv(mkv_ref[...], gk_ref[...])
            mk_s[...] = mk
            mv_s[...] = mv

        halo = jnp.where(i > 0, halo_ref[...], 0.0)
        dif = _pool_diff(u_ref[...], halo, i * tm, gd)
        mixed = _dot(dif.astype(BF), pbd_ref[...]) * scale_ref[...]
        mem = _mem_fwd(mq_ref[...], mk_s[...], mv_s[...], gq_ref[...])
        o_ref[...] = jnp.concatenate([mixed, mem], axis=-1).astype(BF)

    full = lambda shape: pl.BlockSpec(shape, lambda i: (0,) * len(shape))
    return _call(
        body, name, (s_len // tm,),
        [pl.BlockSpec((tm, main), lambda i: (i, 0)),
         pl.BlockSpec((HALO, main), lambda i: (jnp.maximum(i * hb - 1, 0), 0)),
         pl.BlockSpec((tm, KVW), lambda i: (i, main // KVW)),
         full((mlen, 2 * KVW)), full((main, main)), full((1, main)), full((1, HEAD)), full((1, HEAD))],
        pl.BlockSpec((tm, d), lambda i: (i, 0)), SDS((s_len, d), BF),
        scratch=[pltpu.VMEM((mlen, KVW), BF), pltpu.VMEM((mlen, KVW), BF)],
        semantics=("arbitrary",))(proj, proj, proj, mkv, pbd, scale, gq, gk)


def _mixer_pool_bwd(proj, dcat, mkv, pbd, scale, gq, gk, *, name, tm):
    s_len, d = proj.shape
    main = d - KVW
    gd = main // len(POOL_WINDOWS)
    mlen = mkv.shape[0]
    hb = tm // HALO
    nt = s_len // tm
    last_halo = s_len // HALO - 1

    def body(u_ref, halo_ref, mq_ref, do_ref, donext_ref, dom_ref, mkv_ref, pbd_ref, scale_ref, gq_ref, gk_ref,
             dproj_ref, dpbd_ref, dscale_ref, dmkv_ref, dgq_ref, dgk_ref, mk_s, mv_s, dmk_s, dmv_s):
        i = pl.program_id(0)

        @pl.when(i == 0)
        def _():
            mk, mv = _mem_kv(mkv_ref[...], gk_ref[...])
            mk_s[...] = mk
            mv_s[...] = mv
            dmk_s[...] = jnp.zeros_like(dmk_s)
            dmv_s[...] = jnp.zeros_like(dmv_s)
            dpbd_ref[...] = jnp.zeros_like(dpbd_ref)
            dscale_ref[...] = jnp.zeros_like(dscale_ref)
            dgq_ref[...] = jnp.zeros_like(dgq_ref)

        pbd = pbd_ref[...]
        scale = scale_ref[...]
        halo = jnp.where(i > 0, halo_ref[...], 0.0)
        dif = _pool_diff(u_ref[...], halo, i * tm, gd).astype(BF)
        do = do_ref[...]
        dscale_ref[...] += jnp.sum(do * _dot(dif, pbd), axis=0, keepdims=True)
        dmixed = (do * scale).astype(BF)
        dpbd_ref[...] += _dot_tn(dif, dmixed)
        dd = _dot_nt(dmixed, pbd)
        donext = jnp.where(i < nt - 1, donext_ref[...], 0.0)
        dd_halo = _dot_nt((donext * scale).astype(BF), pbd)
        du = _pool_diff_bwd(dd, dd_halo, i * tm, gd)

        dmq, dmk, dmv, dgq = _mem_bwd(mq_ref[...], dom_ref[...], mk_s[...], mv_s[...], gq_ref[...])
        dmk_s[...] += dmk
        dmv_s[...] += dmv
        dgq_ref[...] += dgq
        dproj_ref[...] = jnp.concatenate([du, dmq], axis=-1).astype(BF)

        @pl.when(i == nt - 1)
        def _():
            dmkv, dgk = _mem_kv_bwd(mkv_ref[...], dmk_s[...], dmv_s[...], gk_ref[...])
            dmkv_ref[...] = dmkv
            dgk_ref[...] = dgk

    full = lambda shape: pl.BlockSpec(shape, lambda i: (0,) * len(shape))
    return _call(
        body, name, (nt,),
        [pl.BlockSpec((tm, main), lambda i: (i, 0)),
         pl.BlockSpec((HALO, main), lambda i: (jnp.maximum(i * hb - 1, 0), 0)),
         pl.BlockSpec((tm, KVW), lambda i: (i, main // KVW)),
         pl.BlockSpec((tm, main), lambda i: (i, 0)),
         pl.BlockSpec((HALO, main), lambda i: (jnp.minimum((i + 1) * hb, last_halo), 0)),
         pl.BlockSpec((tm, KVW), lambda i: (i, main // KVW)),
         full((mlen, 2 * KVW)), full((main, main)), full((1, main)), full((1, HEAD)), full((1, HEAD))],
        [pl.BlockSpec((tm, d), lambda i: (i, 0)), full((main, main)), full((1, main)), full((mlen, 2 * KVW)),
         full((1, HEAD)), full((1, HEAD))],
        [SDS((s_len, d), BF), SDS((main, main), F32), SDS((1, main), F32), SDS((mlen, 2 * KVW), F32),
         SDS((1, HEAD), F32), SDS((1, HEAD), F32)],
        scratch=[pltpu.VMEM((mlen, KVW), BF), pltpu.VMEM((mlen, KVW), BF), pltpu.VMEM((mlen, KVW), F32),
                 pltpu.VMEM((mlen, KVW), F32)],
        semantics=("arbitrary",))(proj, proj, proj, dcat, dcat, dcat, mkv, pbd, scale, gq, gk)


def _mixer_swa_fwd(proj, kn, v, mkv, gqs, sinks, gq, gk, *, name):
    s_len, d = proj.shape
    main = d - KVW
    qh = main // HEAD
    mlen = mkv.shape[0]
    tm = WINDOW

    def body(q_ref, mq_ref, kp_ref, kc_ref, vp_ref, vc_ref, mkv_ref, gqs_ref, sinks_ref, gq_ref, gk_ref, o_ref,
             mk_s, mv_s):
        n = pl.program_id(0)

        @pl.when(n == 0)
        def _():
            mk, mv = _mem_kv(mkv_ref[...], gk_ref[...])
            mk_s[...] = mk
            mv_s[...] = mv

        kk = jnp.concatenate([kp_ref[...], kc_ref[...]], axis=0)
        vv = jnp.concatenate([vp_ref[...], vc_ref[...]], axis=0)
        att = _swa_fwd(q_ref[...], kk, vv, gqs_ref[...], sinks_ref[...], n, qh)
        mem = _mem_fwd(mq_ref[...], mk_s[...], mv_s[...], gq_ref[...])
        o_ref[...] = jnp.concatenate([att, mem], axis=-1).astype(BF)

    full = lambda shape: pl.BlockSpec(shape, lambda i: (0,) * len(shape))
    prev = lambda i: (jnp.maximum(i - 1, 0), 0)
    cur = lambda i: (i, 0)
    return _call(
        body, name, (s_len // tm,),
        [pl.BlockSpec((tm, main), cur), pl.BlockSpec((tm, KVW), lambda i: (i, main // KVW)),
         pl.BlockSpec((tm, KVW), prev), pl.BlockSpec((tm, KVW), cur),
         pl.BlockSpec((tm, KVW), prev), pl.BlockSpec((tm, KVW), cur),
         full((mlen, 2 * KVW)), full((1, HEAD)), full((1, LANE)), full((1, HEAD)), full((1, HEAD))],
        pl.BlockSpec((tm, d), cur), SDS((s_len, d), BF),
        scratch=[pltpu.VMEM((mlen, KVW), BF), pltpu.VMEM((mlen, KVW), BF)],
        semantics=("arbitrary",))(proj, proj, kn, kn, v, v, mkv, gqs, sinks, gq, gk)


def _mixer_swa_bwd(proj, dcat, kn, v, mkv, gqs, sinks, gq, gk, *, name):
    s_len, d = proj.shape
    main = d - KVW
    qh = main // HEAD
    mlen = mkv.shape[0]
    tm = WINDOW
    nt = s_len // tm

    def body(q_ref, mq_ref, do_ref, dom_ref, kp_ref, kc_ref, vp_ref, vc_ref, mkv_ref, gqs_ref, sinks_ref, gq_ref,
             gk_ref, dproj_ref, dk_ref, dv_ref, dmkv_ref, dgqs_ref, dsinks_ref, dgq_ref, dgk_ref,
             mk_s, mv_s, dmk_s, dmv_s):
        n = pl.program_id(0)

        @pl.when(n == 0)
        def _():
            mk, mv = _mem_kv(mkv_ref[...], gk_ref[...])
            mk_s[...] = mk
            mv_s[...] = mv
            dmk_s[...] = jnp.zeros_like(dmk_s)
            dmv_s[...] = jnp.zeros_like(dmv_s)
            dk_ref[...] = jnp.zeros_like(dk_ref)
            dv_ref[...] = jnp.zeros_like(dv_ref)
            dgqs_ref[...] = jnp.zeros_like(dgqs_ref)
            dsinks_ref[...] = jnp.zeros_like(dsinks_ref)
            dgq_ref[...] = jnp.zeros_like(dgq_ref)

        kk = jnp.concatenate([kp_ref[...], kc_ref[...]], axis=0)
        vv = jnp.concatenate([vp_ref[...], vc_ref[...]], axis=0)
        dq, dkk, dvv, dgqs, dsk = _swa_bwd(q_ref[...], do_ref[...], kk, vv, gqs_ref[...], sinks_ref[...], n, qh)
        prev = pl.ds(pl.multiple_of(jnp.maximum(n - 1, 0) * tm, tm), tm)
        own = pl.ds(pl.multiple_of(n * tm, tm), tm)
        dk_ref[prev, :] += dkk[:tm]
        dk_ref[own, :] += dkk[tm:]
        dv_ref[prev, :] += dvv[:tm]
        dv_ref[own, :] += dvv[tm:]
        dgqs_ref[...] += dgqs
        dsinks_ref[...] += dsk

        dmq, dmk, dmv, dgq = _mem_bwd(mq_ref[...], dom_ref[...], mk_s[...], mv_s[...], gq_ref[...])
        dmk_s[...] += dmk
        dmv_s[...] += dmv
        dgq_ref[...] += dgq
        dproj_ref[...] = jnp.concatenate([dq, dmq], axis=-1).astype(BF)

        @pl.when(n == nt - 1)
        def _():
            dmkv, dgk = _mem_kv_bwd(mkv_ref[...], dmk_s[...], dmv_s[...], gk_ref[...])
            dmkv_ref[...] = dmkv
            dgk_ref[...] = dgk

    full = lambda shape: pl.BlockSpec(shape, lambda i: (0,) * len(shape))
    prev_b = lambda i: (jnp.maximum(i - 1, 0), 0)
    cur = lambda i: (i, 0)
    memcol = lambda i: (i, main // KVW)
    return _call(
        body, name, (nt,),
        [pl.BlockSpec((tm, main), cur), pl.BlockSpec((tm, KVW), memcol),
         pl.BlockSpec((tm, main), cur), pl.BlockSpec((tm, KVW), memcol),
         pl.BlockSpec((tm, KVW), prev_b), pl.BlockSpec((tm, KVW), cur),
         pl.BlockSpec((tm, KVW), prev_b), pl.BlockSpec((tm, KVW), cur),
         full((mlen, 2 * KVW)), full((1, HEAD)), full((1, LANE)), full((1, HEAD)), full((1, HEAD))],
        [pl.BlockSpec((tm, d), cur), full((s_len, KVW)), full((s_len, KVW)), full((mlen, 2 * KVW)),
         full((1, HEAD)), full((1, LANE)), full((1, HEAD)), full((1, HEAD))],
        [SDS((s_len, d), BF), SDS((s_len, KVW), F32), SDS((s_len, KVW), F32), SDS((mlen, 2 * KVW), F32),
         SDS((1, HEAD), F32), SDS((1, LANE), F32), SDS((1, HEAD), F32), SDS((1, HEAD), F32)],
        scratch=[pltpu.VMEM((mlen, KVW), BF), pltpu.VMEM((mlen, KVW), BF), pltpu.VMEM((mlen, KVW), F32),
                 pltpu.VMEM((mlen, KVW), F32)],
        semantics=("arbitrary",))(proj, proj, dcat, dcat, kn, kn, v, v, mkv, gqs, sinks, gq, gk)


def _kv_prep(kv, gk, *, tm):
    s_len = kv.shape[0]

    def body(kv_ref, gk_ref, k_ref, v_ref):
        k, v = _mem_kv(kv_ref[...], gk_ref[...])
        k_ref[...] = k
        v_ref[...] = v

    row = lambda i: (i, 0)
    return _call(
        body, "kv_prep", (s_len // tm,),
        [pl.BlockSpec((tm, 2 * KVW), row), pl.BlockSpec((1, HEAD), lambda i: (0, 0))],
        [pl.BlockSpec((tm, KVW), row), pl.BlockSpec((tm, KVW), row)],
        [SDS((s_len, KVW), BF), SDS((s_len, KVW), BF)], semantics=("parallel",))(kv, gk)


def _kv_bwd(kv, dks, dvs, gk, *, tm):
    s_len = kv.shape[0]
    nl = len(dks)

    def body(*refs):
        kv_ref, gk_ref = refs[0], refs[1]
        dk_refs, dv_refs = refs[2:2 + nl], refs[2 + nl:2 + 2 * nl]
        dkv_ref, dgk_ref = refs[2 + 2 * nl], refs[3 + 2 * nl]
        dk, dv = dk_refs[0][...], dv_refs[0][...]
        for t in range(1, nl):
            dk = dk + dk_refs[t][...]
            dv = dv + dv_refs[t][...]
        dkv, dgk = _mem_kv_bwd(kv_ref[...], dk, dv, gk_ref[...])
        dkv_ref[...] = dkv.astype(BF)

        @pl.when(pl.program_id(0) == 0)
        def _():
            dgk_ref[...] = jnp.zeros_like(dgk_ref)

        dgk_ref[...] += dgk

    row = lambda i: (i, 0)
    one = pl.BlockSpec((1, HEAD), lambda i: (0, 0))
    return _call(
        body, "kv_bwd", (s_len // tm,),
        [pl.BlockSpec((tm, 2 * KVW), row), one] + [pl.BlockSpec((tm, KVW), row)] * (2 * nl),
        [pl.BlockSpec((tm, 2 * KVW), row), one],
        [SDS((s_len, 2 * KVW), BF), SDS((1, HEAD), F32)], semantics=("arbitrary",))(kv, gk, *dks, *dvs)


def _place():
    x, y, c = lax.axis_index("x"), lax.axis_index("y"), lax.axis_index("c")
    flips = [(1 - x, y), (x, 1 - y), (1 - x, 1 - y)]
    return x, y, c, flips


def _remote(src, dst, send_sem, recv_sem, to):
    return pltpu.make_async_remote_copy(src_ref=src, dst_ref=dst, send_sem=send_sem, recv_sem=recv_sem,
                                        device_id=to, device_id_type=MESH)


def _ag_copies(p_refs, wg_refs, send, recv):
    x, y, c, flips = _place()
    chip = 2 * x + y
    cps = []
    for j, (fx, fy) in enumerate(flips):
        for b in range(2):
            half = p_refs[b].shape[0] // 2
            mine = pl.ds(c * half, half)
            cps.append(_remote(p_refs[b].at[mine, :], wg_refs[b].at[chip, mine, :], send.at[2 * j + b],
                               recv.at[2 * j + b], (fx, fy, c)))
    return cps


def _ag_start(p1, p2, wg1, wg2, after, *, name):
    def body(p1_ref, p2_ref, wg1_ref, wg2_ref, after_ref, send, recv, p1_o, p2_o, wg1_o, wg2_o, tok):
        for cp in _ag_copies((p1_ref, p2_ref), (wg1_ref, wg2_ref), send, recv):
            cp.start()
        tok[...] = jnp.zeros_like(tok)

    return pl.pallas_call(
        body, name=name,
        out_shape=(pltpu.SemaphoreType.DMA((6,)), pltpu.SemaphoreType.DMA((6,)), pltpu.HBM(p1.shape, p1.dtype),
                   pltpu.HBM(p2.shape, p2.dtype), pltpu.HBM(wg1.shape, wg1.dtype), pltpu.HBM(wg2.shape, wg2.dtype),
                   SDS((8, LANE), F32)),
        in_specs=(HBM, HBM, HBM, HBM, ANY), out_specs=(SEM, SEM, HBM, HBM, HBM, HBM, VMEM_WHOLE),
        input_output_aliases={0: 2, 1: 3, 2: 4, 3: 5},
        compiler_params=pltpu.CompilerParams(has_side_effects=SIDE_EFFECT))(
            _in_hbm(p1), _in_hbm(p2), _in_hbm(wg1), _in_hbm(wg2), after)


def _ag_wait(send, recv, p1, p2, wg1, wg2, after, *, name):
    def body(p1_ref, p2_ref, wg1_ref, wg2_ref, send_ref, recv_ref, after_ref, p1_o, p2_o, wg1_o, wg2_o):
        for cp in _ag_copies((p1_ref, p2_ref), (wg1_ref, wg2_ref), send_ref, recv_ref):
            cp.wait_send()
            cp.wait_recv()

    out = pl.pallas_call(
        body, name=name,
        out_shape=(pltpu.HBM(p1.shape, p1.dtype), pltpu.HBM(p2.shape, p2.dtype), pltpu.HBM(wg1.shape, wg1.dtype),
                   pltpu.HBM(wg2.shape, wg2.dtype)),
        in_specs=(HBM, HBM, HBM, HBM, SEM, SEM, ANY), out_specs=(HBM, HBM, HBM, HBM),
        input_output_aliases={0: 0, 1: 1, 2: 2, 3: 3},
        compiler_params=pltpu.CompilerParams(has_side_effects=SIDE_EFFECT))(p1, p2, wg1, wg2, send, recv, after)
    return out[2], out[3]


def _ag_forward(p1, p2, wg1, wg2, *, name):
    def body(p1_ref, p2_ref, _, __, wg1_ref, wg2_ref, send, recv):
        x, y, c, flips = _place()
        chip = 2 * x + y
        sib = (x, y, 1 - c)
        p_refs, wg_refs = (p1_ref, p2_ref), (wg1_ref, wg2_ref)
        sends, arrivals = [], []
        for b in range(2):
            half = p_refs[b].shape[0] // 2
            own = wg_refs[b].at[chip]
            sends.append(_remote(p_refs[b], own, send.at[b], recv.at[b], sib))
            arrivals.append(_remote(p_refs[b], own, send.at[b], recv.at[b], sib))
            for j, (fx, fy) in enumerate(flips):
                k = 2 + 3 * b + j
                landed = wg_refs[b].at[2 * fx + fy, pl.ds(c * half, half), :]
                other = wg_refs[b].at[2 * fx + fy, pl.ds((1 - c) * half, half), :]
                sends.append(_remote(landed, landed, send.at[k], recv.at[k], sib))
                arrivals.append(_remote(other, other, send.at[k], recv.at[k], sib))
        for cp in sends:
            cp.start()
        for cp in arrivals:
            cp.wait_recv()
        for cp in sends:
            cp.wait_send()

    return pl.pallas_call(
        body, name=name, in_specs=[ANY, ANY, ANY, ANY], out_specs=[ANY, ANY],
        out_shape=[SDS(wg1.shape, wg1.dtype), SDS(wg2.shape, wg2.dtype)], input_output_aliases={2: 0, 3: 1},
        scratch_shapes=[pltpu.SemaphoreType.DMA((8,)), pltpu.SemaphoreType.DMA((8,))])(p1, p2, wg1, wg2)


def _gather_small(ps):
    def body(ps_ref, o_ref, send, recv):
        x, y, c, flips = _place()
        chip = 2 * x + y
        o_ref[chip] = ps_ref[...]
        cps = [_remote(ps_ref, o_ref.at[chip], send.at[j], recv.at[j], (fx, fy, c))
               for j, (fx, fy) in enumerate(flips)]
        for cp in cps:
            cp.start()
        for j, (fx, fy) in enumerate(flips):
            _remote(ps_ref, o_ref.at[2 * fx + fy], send.at[j], recv.at[j], (fx, fy, c)).wait_recv()
        for cp in cps:
            cp.wait_send()

    return pl.pallas_call(
        body, name="gather_small", in_specs=[VMEM_WHOLE], out_specs=VMEM_WHOLE,
        out_shape=SDS((N_CHIPS,) + ps.shape, ps.dtype),
        scratch_shapes=[pltpu.SemaphoreType.DMA((3,)), pltpu.SemaphoreType.DMA((3,))])(ps)


def _swap_sibling_halves(g1, g2, *, name):
    h1, h2 = g1.shape[1] // 2, g2.shape[1] // 2

    def body(g1_ref, g2_ref, r1_ref, r2_ref, send, recv):
        x, y, c, _ = _place()
        cps = [_remote(g1_ref.at[:, pl.ds((1 - c) * h1, h1), :], r1_ref, send.at[0], recv.at[0], (x, y, 1 - c)),
               _remote(g2_ref.at[:, pl.ds((1 - c) * h2, h2), :], r2_ref, send.at[1], recv.at[1], (x, y, 1 - c))]
        for cp in cps:
            cp.start()
        for cp in cps:
            cp.wait()

    return pl.pallas_call(
        body, name=name, in_specs=[ANY, ANY], out_specs=[ANY, ANY],
        out_shape=[SDS((N_CHIPS, h1, g1.shape[2]), g1.dtype), SDS((N_CHIPS, h2, g2.shape[2]), g2.dtype)],
        scratch_shapes=[pltpu.SemaphoreType.DMA((2,)), pltpu.SemaphoreType.DMA((2,))])(g1, g2)


def _sum_sibling(g, r, place, *, tm, name):
    n_sh, half, w = r.shape
    nt = half // tm

    def body(place_ref, g_ref, r_ref, pbf_ref, own_ref):
        s = pl.program_id(1)
        p = g_ref[0] + r_ref[0]
        pbf_ref[0] = p.astype(BF)

        @pl.when(s == place_ref[1])
        def _():
            own_ref[...] = p

    return _call(
        body, name, (nt, n_sh),
        [pl.BlockSpec((1, tm, w), lambda i, s, pr: (s, pr[0] * nt + i, 0)),
         pl.BlockSpec((1, tm, w), lambda i, s, pr: (s, i, 0))],
        [pl.BlockSpec((1, tm, w), lambda i, s, pr: (s, i, 0)), pl.BlockSpec((tm, w), lambda i, s, pr: (i, 0))],
        [SDS((n_sh, half, w), BF), SDS((half, w), F32)],
        semantics=("arbitrary", "arbitrary"), prefetch=1)(place, g, r)


def _rs_copies(p_refs, land_refs, send, recv):
    _, _, c, flips = _place()
    cps = []
    for j, (fx, fy) in enumerate(flips):
        for b in range(2):
            cps.append(_remote(p_refs[b].at[2 * fx + fy], land_refs[b].at[j], send.at[2 * j + b], recv.at[2 * j + b],
                               (fx, fy, c)))
    return cps


def _rs_start(pb1, pb2, land1, land2, fresh, *, name):
    def body(pb1_ref, pb2_ref, land1_ref, land2_ref, send, recv, *_):
        for cp in _rs_copies((pb1_ref, pb2_ref), (land1_ref, land2_ref), send, recv):
            cp.start()

    extra_shape = () if fresh is None else (pltpu.HBM(fresh, F32),)
    extra_spec = () if fresh is None else (HBM,)
    out = pl.pallas_call(
        body, name=name,
        out_shape=(pltpu.SemaphoreType.DMA((6,)), pltpu.SemaphoreType.DMA((6,)), pltpu.HBM(pb1.shape, pb1.dtype),
                   pltpu.HBM(pb2.shape, pb2.dtype), pltpu.HBM(land1.shape, land1.dtype),
                   pltpu.HBM(land2.shape, land2.dtype)) + extra_shape,
        in_specs=(HBM, HBM, HBM, HBM), out_specs=(SEM, SEM, HBM, HBM, HBM, HBM) + extra_spec,
        input_output_aliases={0: 2, 1: 3, 2: 4, 3: 5},
        compiler_params=pltpu.CompilerParams(has_side_effects=SIDE_EFFECT))(
            _in_hbm(pb1), _in_hbm(pb2), _in_hbm(land1), _in_hbm(land2))
    return tuple(out) if fresh is not None else tuple(out) + (None,)


def _rs_wait(send, recv, pb1, pb2, land1, land2, after, *, name):
    def body(pb1_ref, pb2_ref, land1_ref, land2_ref, send_ref, recv_ref, after_ref, *_):
        for cp in _rs_copies((pb1_ref, pb2_ref), (land1_ref, land2_ref), send_ref, recv_ref):
            cp.wait_send()
            cp.wait_recv()

    out = pl.pallas_call(
        body, name=name,
        out_shape=(pltpu.HBM(pb1.shape, pb1.dtype), pltpu.HBM(pb2.shape, pb2.dtype),
                   pltpu.HBM(land1.shape, land1.dtype), pltpu.HBM(land2.shape, land2.dtype)),
        in_specs=(HBM, HBM, HBM, HBM, SEM, SEM, ANY), out_specs=(HBM, HBM, HBM, HBM),
        input_output_aliases={0: 0, 1: 1, 2: 2, 3: 3},
        compiler_params=pltpu.CompilerParams(has_side_effects=SIDE_EFFECT))(pb1, pb2, land1, land2, send, recv, after)
    return out[2], out[3]


def _sum_chips(own, r, full, layer, place, *, tm, name):
    half, w = own.shape

    def body(place_ref, own_ref, r_ref, _, o_ref):
        o_ref[0, 0] = ((own_ref[...] + r_ref[0].astype(F32)) + r_ref[1].astype(F32)) + r_ref[2].astype(F32)

    return _call(
        body, name, (half // tm,),
        [pl.BlockSpec((tm, w), lambda i, pr: (i, 0)), pl.BlockSpec((3, tm, w), lambda i, pr: (0, i, 0)), ANY],
        pl.BlockSpec((1, 1, tm, w), lambda i, pr: (layer, pr[0], i, 0)), SDS(full.shape, F32),
        semantics=("parallel",), prefetch=1, aliases={3: 0})(place, own, r, full)


def _share_with_sibling(f1, f2):
    def body(_, __, o1_ref, o2_ref, send, recv):
        x, y, c, _ = _place()
        mine, other = pl.ds(c, 1), pl.ds(1 - c, 1)
        cps = [_remote(o1_ref.at[:, mine], o1_ref.at[:, mine], send.at[0], recv.at[0], (x, y, 1 - c)),
               _remote(o2_ref.at[:, mine], o2_ref.at[:, mine], send.at[1], recv.at[1], (x, y, 1 - c))]
        for cp in cps:
            cp.start()
        for cp in cps:
            cp.wait_send()
        _remote(o1_ref.at[:, other], o1_ref.at[:, other], send.at[0], recv.at[0], (x, y, 1 - c)).wait_recv()
        _remote(o2_ref.at[:, other], o2_ref.at[:, other], send.at[1], recv.at[1], (x, y, 1 - c)).wait_recv()

    return pl.pallas_call(
        body, name="share_with_sibling", in_specs=[ANY, ANY], out_specs=[ANY, ANY],
        out_shape=[SDS(f1.shape, f1.dtype), SDS(f2.shape, f2.dtype)], input_output_aliases={0: 0, 1: 1},
        scratch_shapes=[pltpu.SemaphoreType.DMA((2,)), pltpu.SemaphoreType.DMA((2,))])(f1, f2)


def _allreduce_small(sg):
    rows, w = sg.shape

    def body(sg_ref, o_ref, slots, send, recv):
        x, y, c, _ = _place()
        me = 4 * x + 2 * y + c
        slots[me] = sg_ref[...]
        peers = []
        for k in range(1, 8):
            a, b, e = (k >> 2) & 1, (k >> 1) & 1, k & 1
            px = x + a - 2 * a * x
            py = y + b - 2 * b * y
            pc = c + e - 2 * e * c
            cp = _remote(sg_ref, slots.at[me], send.at[k - 1], recv.at[k - 1], (px, py, pc))
            cp.start()
            peers.append((cp, 4 * px + 2 * py + pc, (px, py, pc)))
        for k, (cp, pidx, pid) in enumerate(peers):
            _remote(sg_ref, slots.at[pidx], send.at[k], recv.at[k], pid).wait_recv()
        acc = slots[0]
        for dev in range(1, 8):
            acc = acc + slots[dev]
        o_ref[...] = acc
        for cp, _, _ in peers:
            cp.wait_send()

    vm = pl.BlockSpec(memory_space=pltpu.VMEM)
    return pl.pallas_call(
        body, name="allreduce_small", in_specs=[vm], out_specs=vm, out_shape=SDS((rows, w), F32),
        scratch_shapes=[pltpu.VMEM((8, rows, w), F32), pltpu.SemaphoreType.DMA((7,)),
                        pltpu.SemaphoreType.DMA((7,))],
        compiler_params=pltpu.CompilerParams(vmem_limit_bytes=VMEM_LIMIT_MB << 20))(sg)


def _adamw(g_arr, layer0, g_off, per_layer, w, m, v, *, name, tm):
    rows, cols = w.shape
    assert g_off % tm == 0 and per_layer % tm == 0 and rows % per_layer == 0
    npl = per_layer // tm
    c1 = 1.0 - ADAM_B1 ** ADAM_STEP
    c2 = 1.0 - ADAM_B2 ** ADAM_STEP

    def body(g_ref, w_ref, m_ref, v_ref, go_ref, d_ref, mo_ref, vo_ref):
        g = g_ref[0]
        mn = ADAM_B1 * m_ref[...] + (1.0 - ADAM_B1) * g
        vn = ADAM_B2 * v_ref[...] + (1.0 - ADAM_B2) * (g * g)
        go_ref[...] = g
        mo_ref[...] = mn
        vo_ref[...] = vn
        d_ref[...] = -ADAM_LR * ((mn / c1) / (jnp.sqrt(vn / c2) + ADAM_EPS) + ADAM_WD * w_ref[...])

    blk = pl.BlockSpec((tm, cols), lambda i: (i, 0))
    return _call(
        body, name, (rows // tm,),
        [pl.BlockSpec((1, tm, cols), lambda i: (layer0 + i // npl, g_off // tm + i % npl, 0)), blk, blk, blk],
        [blk] * 4,
        [SDS((rows, cols), F32)] * 4, semantics=("parallel",))(g_arr, w, m, v)


def _pack_small(parts, width):
    flat = jnp.concatenate([p.reshape(-1).astype(F32) for p in parts])
    rows = -(-flat.shape[0] // (8 * width)) * 8
    return jnp.pad(flat, (0, rows * width - flat.shape[0])).reshape(rows, width)


def _unpack_small(packed, shapes):
    flat = packed.reshape(-1)
    out, off = [], 0
    for shp in shapes:
        size = 1
        for n in shp:
            size *= n
        out.append(flat[off:off + size].reshape(shp))
        off += size
    return out


def _block_diag(pw):
    g, c, _ = pw.shape
    eye = jnp.eye(g, dtype=pw.dtype)
    return (eye[:, None, :, None] * pw[:, :, None, :]).reshape(g * c, g * c)


def _diag_blocks(full, g):
    c = full.shape[0] // g
    return jnp.stack([full[i * c:(i + 1) * c, i * c:(i + 1) * c] for i in range(g)])


def kernel(x, mem, norm_mix, w_in, pool_w, pool_scale, kv_norm, w_kv, k_norm, q_norm, sinks, mem_norm, w_mem_kv, mem_q_norm, mem_k_norm, w_out, norm_mlp, w_up, w_down, loss_target, m_norm_mix, m_w_in, m_pool_w, m_pool_scale, m_kv_norm, m_w_kv, m_k_norm, m_q_norm, m_sinks, m_mem_norm, m_w_mem_kv, m_mem_q_norm, m_mem_k_norm, m_w_out, m_norm_mlp, m_w_up, m_w_down, v_norm_mix, v_w_in, v_pool_w, v_pool_scale, v_kv_norm, v_w_kv, v_k_norm, v_q_norm, v_sinks, v_mem_norm, v_w_mem_kv, v_mem_q_norm, v_mem_k_norm, v_w_out, v_norm_mlp, v_w_up, v_w_down):
    s_len, d = x.shape[1], x.shape[2]
    n_layers, n_pool = norm_mix.shape[0], pool_w.shape[0]
    n_swa = n_layers - n_pool
    main = d - KVW
    qh = main // HEAD
    ff = w_down.shape[1] * N_CHIPS
    dq = d // N_CHIPS
    assert w_up.shape[2] == d and ff == N_CHIPS * d and w_kv.shape[1] == 2 * KVW
    tm = min(512, s_len)
    tm_mem = mem.shape[1]

    cx, cy, cc = lax.axis_index("x"), lax.axis_index("y"), lax.axis_index("c")
    chip = 2 * cx + cy

    off_down, off_up, off_in, off_out = 0, d, 2 * d, 2 * d + dq
    rows1 = off_out + dq
    off_mkv, off_kv = 0, dq
    rows2 = 2 * dq

    ps = jnp.pad(pool_scale, ((0, 8 - n_pool), (0, 2 * LANE - pool_scale.shape[1])))
    psg = _gather_small(ps)
    pool_scale_full = jnp.concatenate([psg[k, :n_pool, :pool_scale.shape[1]] for k in range(N_CHIPS)], axis=1)

    def packed_weights(l):
        p1 = jnp.concatenate([w_down[l], w_up[l], w_in[l], w_out[l]]).astype(BF)
        p2 = jnp.concatenate([w_mem_kv[l], w_kv] if l == n_pool else [w_mem_kv[l]]).astype(BF)
        return p1, p2

    def gather_start(l, after):
        p1, p2 = packed_weights(l)
        return _ag_start(p1, p2, lax.empty((N_CHIPS,) + p1.shape, BF), lax.empty((N_CHIPS,) + p2.shape, BF), after,
                         name=f"gather_start_{l}")

    def gather_finish(started, after, l):
        send, recv, p1, p2, wg1, wg2, _ = started
        wg1, wg2 = _ag_wait(send, recv, p1, p2, wg1, wg2, after, name=f"gather_wait_{l}")
        return _ag_forward(p1, p2, wg1, wg2, name=f"gather_forward_{l}")

    def w_rows(arr, off, nrows, width):
        assert off % nrows == 0
        return (arr, (N_CHIPS, nrows, width), lambda j: (0, off // nrows, 0))

    def w_cols(arr, off, nrows, width):
        assert off % nrows == 0
        return (arr, (1, nrows, width), lambda j: (j, off // nrows, 0))

    row = lambda a: a.reshape(1, -1)
    h = x.reshape(s_len, d)
    memx = mem.reshape(tm_mem, d)
    tgt = loss_target.reshape(s_len, d)
    pbd = [_block_diag(pool_w[l]).astype(BF) for l in range(n_pool)]
    sinks_pad = [jnp.pad(row(sinks[j]), ((0, 0), (0, LANE - qh))) for j in range(n_swa)]

    w_in_l, w_out_l, w_down_l, w_down_cols_l, w_up_l, w_up_all_l, w_mkv_l = [], [], [], [], [], [], []
    w_kv_g = None
    started = {0: gather_start(0, ps)}
    if n_layers > 1:
        started[1] = gather_start(1, started[0][6])
    saved = []
    kv = hn_kv = kn = vsh = None
    for l in range(n_layers):
        wg1, wg2 = gather_finish(started.pop(l), h if l else started[1][6] if n_layers > 1 else ps, l)
        w_in_l.append(w_rows(wg1, off_in, dq, d))
        w_out_l.append(w_rows(wg1, off_out, dq, d))
        w_down_l.append(w_rows(wg1, off_down, d, d))
        w_down_cols_l.append(w_cols(wg1, off_down, d, d))
        w_up_l.append(w_cols(wg1, off_up, d, d))
        w_up_all_l.append((wg1, (N_CHIPS, d, d), lambda j: (0, off_up // d, 0)))
        w_mkv_l.append(w_rows(wg2, off_mkv, dq, 2 * KVW))
        g_mix = row(norm_mix[l])
        if l + 2 < n_layers:
            started[l + 2] = gather_start(l + 2, wg1)
            g_mix = g_mix + started[l + 2][6][0, 0]
        if l == n_pool:
            w_kv_g = w_rows(wg2, off_kv, dq, 2 * KVW)
            kv, hn_kv = _norm_mm(h, row(kv_norm), w_kv_g, 1, 2 * KVW, act=False, name="kv_proj", tm=tm)
            kn, vsh = _kv_prep(kv, row(k_norm), tm=tm)
        h0 = h
        proj, xn = _norm_mm(h0, g_mix, w_in_l[l], 1, d, act=False, name=f"in_proj_{l}", tm=tm)
        mkv, memn = _norm_mm(memx, row(mem_norm[l]), w_mkv_l[l], 1, 2 * KVW, act=False, name=f"mem_kv_{l}", tm=tm_mem)
        if l < n_pool:
            cat = _mixer_pool_fwd(proj, mkv, pbd[l], row(pool_scale_full[l]), row(mem_q_norm[l]), row(mem_k_norm[l]),
                                  name=f"mixer_fwd_{l}", tm=tm)
        else:
            j = l - n_pool
            cat = _mixer_swa_fwd(proj, kn, vsh, mkv, row(q_norm[j]), sinks_pad[j], row(mem_q_norm[l]),
                                 row(mem_k_norm[l]), name=f"mixer_fwd_{l}")
        h1 = _mm_res(h0, cat, w_out_l[l], name=f"out_proj_{l}", tm=tm)
        hh, xm = _norm_mm(h1, row(norm_mlp[l]), w_up_l[l], N_CHIPS, d, act=True, name=f"mlp_up_{l}", tm=tm)
        h = _mm_res(h1, hh, w_down_l[l], name=f"mlp_down_{l}", tm=tm)
        saved.append((h0, proj, xn, mkv, memn, cat, h1, hh, xm))

    dh, dh_bf, loss_part = _loss_head(h, tgt, tm=tm)
    loss = lax.psum(loss_part[0, 0], ("x", "y", "c"))

    place = jnp.stack([cc, chip]).astype(jnp.int32)
    half1, half2 = rows1 // 2, rows2 // 2
    g1 = lax.empty((N_CHIPS, rows1, d), F32)
    pending = {}
    tk = min(512, d)
    zeros_mem = jnp.zeros((tm_mem, d), F32)

    def rows_map(off, nrows, tkk):
        per = nrows // tkk
        return lambda i, j: (i // per, off // tkk + i % per, 0)

    def cols_map(off, tkk):
        return lambda i, j: (j, off // tkk + i, 0)

    d_norm_mix, d_norm_mlp, d_mem_norm = [None] * n_layers, [None] * n_layers, [None] * n_layers
    d_mem_q, d_mem_k = [None] * n_layers, [None] * n_layers
    d_pool_w, d_pool_scale = [None] * n_pool, [None] * n_pool
    d_q_norm, d_sinks = [None] * n_swa, [None] * n_swa
    dks, dvs = [], []
    d_kv_norm = d_k_norm = None
    for l in reversed(range(n_layers)):
        h0, proj, xn, mkv, memn, cat, h1, hh, xm = saved[l]
        g2 = jnp.zeros((N_CHIPS, rows2, 2 * KVW), F32)
        g1 = _mm_tn(hh, dh_bf, g1, rows_map(off_down, d, tk), tk, d, name=f"dw_down_{l}")
        du = _mm_nt_relu2(dh_bf, hh, w_down_cols_l[l], N_CHIPS, name=f"d_mlp_act_{l}", tm=tm)
        g1 = _mm_tn(xm, du, g1, cols_map(off_up, tk), tk, d, name=f"dw_up_{l}")
        dh1, dh1_bf, d_norm_mlp[l] = _mm_nt_normbwd(du, w_up_all_l[l], N_CHIPS, h1, row(norm_mlp[l]), dh,
                                                    name=f"d_mlp_in_{l}", tm=tm)
        tkq = min(tk, dq)
        g1 = _mm_tn(cat, dh1_bf, g1, rows_map(off_out, dq, tkq), tkq, d, name=f"dw_out_{l}")
        dcat = _mm_nt(dh1_bf, w_out_l[l], d, name=f"d_cat_{l}", tm=tm)
        if l < n_pool:
            dproj, dpbd, dscale, dmkv, d_mem_q[l], d_mem_k[l] = _mixer_pool_bwd(
                proj, dcat, mkv, pbd[l], row(pool_scale_full[l]), row(mem_q_norm[l]), row(mem_k_norm[l]),
                name=f"mixer_bwd_{l}", tm=tm)
            d_pool_w[l] = _diag_blocks(dpbd, len(POOL_WINDOWS))
            d_pool_scale[l] = dscale
        else:
            j = l - n_pool
            dproj, dk, dv, dmkv, d_q_norm[j], dsk, d_mem_q[l], d_mem_k[l] = _mixer_swa_bwd(
                proj, dcat, kn, vsh, mkv, row(q_norm[j]), sinks_pad[j], row(mem_q_norm[l]), row(mem_k_norm[l]),
                name=f"mixer_bwd_{l}")
            d_sinks[j] = dsk[0, :qh]
            dks.append(dk)
            dvs.append(dv)
        g1 = _mm_tn(xn, dproj, g1, rows_map(off_in, dq, tkq), tkq, d, name=f"dw_in_{l}")
        dh, dh_bf, d_norm_mix[l] = _mm_nt_normbwd(dproj, w_in_l[l], 1, h0, row(norm_mix[l]), dh1,
                                                  name=f"d_in_{l}", tm=tm)
        g2 = _mm_tn(memn, dmkv, g2, rows_map(off_mkv, dq, tkq), tkq, 2 * KVW, name=f"dw_mem_kv_{l}")
        _, _, d_mem_norm[l] = _mm_nt_normbwd(dmkv, w_mkv_l[l], 1, memx, row(mem_norm[l]), zeros_mem,
                                             name=f"d_mem_norm_{l}", tm=tm_mem)
        if l == n_pool:
            dkv, d_k_norm = _kv_bwd(kv, dks, dvs, row(k_norm), tm=tm)
            g2 = _mm_tn(hn_kv, dkv, g2, rows_map(off_kv, dq, tkq), tkq, 2 * KVW, name="dw_kv")
            dh, dh_bf, d_kv_norm = _mm_nt_normbwd(dkv, w_kv_g, 1, h0, row(kv_norm), dh, name="d_kv_in", tm=tm)
        r1, r2 = _swap_sibling_halves(g1, g2, name=f"swap_sibling_halves_{l}")
        pb1, own1 = _sum_sibling(g1, r1, place, tm=_tile(half1, 256), name=f"sum_sibling_a_{l}")
        pb2, own2 = _sum_sibling(g2, r2, place, tm=_tile(half2, 256), name=f"sum_sibling_b_{l}")
        send, recv, pb1, pb2, land1, land2, g1 = _rs_start(
            pb1, pb2, lax.empty((3, half1, d), BF), lax.empty((3, half2, 2 * KVW), BF),
            (N_CHIPS, rows1, d) if l > 0 else None, name=f"reduce_start_{l}")
        pending[l] = (send, recv, pb1, pb2, land1, land2, own1, own2)
    grad_x = dh.reshape(x.shape)

    full1 = lax.empty((n_layers, 2, half1, d), F32)
    full2 = lax.empty((n_layers, 2, half2, 2 * KVW), F32)
    for l in reversed(range(n_layers)):
        send, recv, pb1, pb2, land1, land2, own1, own2 = pending[l]
        x1, x2 = _rs_wait(send, recv, pb1, pb2, land1, land2, dh, name=f"reduce_wait_{l}")
        full1 = _sum_chips(own1, x1, full1, l, place, tm=_tile(half1, 256), name=f"sum_chips_a_{l}")
        full2 = _sum_chips(own2, x2, full2, l, place, tm=_tile(half2, 256), name=f"sum_chips_b_{l}")
    full1, full2 = _share_with_sibling(full1, full2)
    full1 = full1.reshape(n_layers, rows1, d)
    full2 = full2.reshape(n_layers, rows2, 2 * KVW)

    big = {}
    for name, arr, layer0, off, per, w_, m_, v_ in (
            ("w_down", full1, 0, off_down, d, w_down, m_w_down, v_w_down),
            ("w_up", full1, 0, off_up, d, w_up, m_w_up, v_w_up),
            ("w_in", full1, 0, off_in, dq, w_in, m_w_in, v_w_in),
            ("w_out", full1, 0, off_out, dq, w_out, m_w_out, v_w_out),
            ("w_mem_kv", full2, 0, off_mkv, dq, w_mem_kv, m_w_mem_kv, v_w_mem_kv),
            ("w_kv", full2, n_pool, off_kv, dq, w_kv, m_w_kv, v_w_kv)):
        cols = arr.shape[2]
        res = _adamw(arr, layer0, off, per, w_.reshape(-1, cols), m_.reshape(-1, cols), v_.reshape(-1, cols),
                     name=f"adamw_{name}", tm=min(256, dq))
        big[name] = [r.reshape(w_.shape) for r in res]

    small_names = ["norm_mix", "pool_w", "pool_scale", "kv_norm", "k_norm", "q_norm", "sinks", "mem_norm",
                   "mem_q_norm", "mem_k_norm", "norm_mlp"]
    small_grads = {
        "norm_mix": jnp.concatenate(d_norm_mix), "pool_w": jnp.stack(d_pool_w),
        "pool_scale": jnp.concatenate(d_pool_scale), "kv_norm": d_kv_norm[0], "k_norm": d_k_norm[0],
        "q_norm": jnp.concatenate(d_q_norm), "sinks": jnp.stack(d_sinks), "mem_norm": jnp.concatenate(d_mem_norm),
        "mem_q_norm": jnp.concatenate(d_mem_q), "mem_k_norm": jnp.concatenate(d_mem_k),
        "norm_mlp": jnp.concatenate(d_norm_mlp)}
    width = d
    sg = _pack_small([small_grads[n] for n in small_names], width)
    sg = _allreduce_small(sg)
    reduced = dict(zip(small_names, _unpack_small(sg, [small_grads[n].shape for n in small_names])))
    psw = pool_scale.shape[1]
    reduced["pool_scale"] = lax.dynamic_slice_in_dim(reduced["pool_scale"], chip * psw, psw, axis=1)
    params = dict(norm_mix=(norm_mix, m_norm_mix, v_norm_mix), pool_w=(pool_w, m_pool_w, v_pool_w),
                  pool_scale=(pool_scale, m_pool_scale, v_pool_scale), kv_norm=(kv_norm, m_kv_norm, v_kv_norm),
                  k_norm=(k_norm, m_k_norm, v_k_norm), q_norm=(q_norm, m_q_norm, v_q_norm),
                  sinks=(sinks, m_sinks, v_sinks), mem_norm=(mem_norm, m_mem_norm, v_mem_norm),
                  mem_q_norm=(mem_q_norm, m_mem_q_norm, v_mem_q_norm),
                  mem_k_norm=(mem_k_norm, m_mem_k_norm, v_mem_k_norm), norm_mlp=(norm_mlp, m_norm_mlp, v_norm_mlp))
    shapes = [params[n][0].shape for n in small_names]
    packs = [_pack_small([reduced[n].reshape(params[n][0].shape) for n in small_names], width)]
    packs += [_pack_small([params[n][t] for n in small_names], width) for t in range(3)]
    res = _adamw(packs[0][None], 0, 0, packs[0].shape[0], packs[1], packs[2], packs[3], name="adamw_small", tm=8)
    small = {n: [] for n in small_names}
    for r in res:
        for n, a in zip(small_names, _unpack_small(r, shapes)):
            small[n].append(a)

    order = ["norm_mix", "w_in", "pool_w", "pool_scale", "kv_norm", "w_kv", "k_norm", "q_norm", "sinks", "mem_norm",
             "w_mem_kv", "mem_q_norm", "mem_k_norm", "w_out", "norm_mlp", "w_up", "w_down"]
    out = {**big, **small}
    return (loss, grad_x, *[out[n][0] for n in order], *[out[n][1] for n in order],
            *[out[n][2] for n in order], *[out[n][3] for n in order])
```

```python
import functools

import jax
import jax.numpy as jnp
from jax import lax
from jax.experimental import pallas as pl
from jax.experimental.pallas import tpu as pltpu

F32, BF = jnp.float32, jnp.bfloat16
SDS = jax.ShapeDtypeStruct
MESH = pl.DeviceIdType.MESH
ANY = pl.BlockSpec(memory_space=pl.ANY)
HBM = pl.BlockSpec(memory_space=pltpu.HBM)
SEM = pl.BlockSpec(memory_space=pltpu.SEMAPHORE)
VMEM_WHOLE = pl.BlockSpec(memory_space=pltpu.VMEM)
SIDE_EFFECT = pltpu.SideEffectType.DATAFLOW_SIDE_EFFECTING


def _in_hbm(a):
    return pltpu.with_memory_space_constraint(a, pltpu.HBM)


EPS = 1e-6
HEAD = 64
KV_HEADS = 4
KVW = KV_HEADS * HEAD
WINDOW = 128
POOL_WINDOWS = (2, 4, 8, 16)
HALO = 16
QK_SCALE = HEAD ** -0.5
NEG = float(jnp.finfo(jnp.float32).min)
N_CHIPS = 4
LANE = 128

ADAM_LR, ADAM_B1, ADAM_B2, ADAM_EPS, ADAM_WD, ADAM_STEP = 0.001, 0.9, 0.999, 1e-08, 0.01, 10

VMEM_LIMIT_MB = 56


def _call(body, name, grid, in_specs, out_specs, out_shape, *, scratch=(), semantics=None, aliases=None,
          prefetch=0):
    params = pltpu.CompilerParams(dimension_semantics=semantics, vmem_limit_bytes=VMEM_LIMIT_MB << 20)
    if prefetch:
        spec = pltpu.PrefetchScalarGridSpec(num_scalar_prefetch=prefetch, grid=grid, in_specs=in_specs,
                                            out_specs=out_specs, scratch_shapes=list(scratch))
        return pl.pallas_call(body, name=name, grid_spec=spec, out_shape=out_shape,
                              input_output_aliases=aliases or {}, compiler_params=params)
    return pl.pallas_call(body, name=name, grid=grid, in_specs=in_specs, out_specs=out_specs, out_shape=out_shape,
                          scratch_shapes=list(scratch), input_output_aliases=aliases or {}, compiler_params=params)


def _tile(n, pref):
    return max(t for t in range(8, min(n, pref) + 1, 8) if n % t == 0)


def _dot(a, b):
    return jnp.dot(a, b, preferred_element_type=F32)


def _dot_nt(a, b):
    return lax.dot_general(a, b, (((1,), (1,)), ((), ())), preferred_element_type=F32)


def _dot_tn(a, b):
    return lax.dot_general(a, b, (((0,), (0,)), ((), ())), preferred_element_type=F32)


def _rms(x):
    r = lax.rsqrt(jnp.mean(x * x, axis=-1, keepdims=True) + EPS)
    return x * r, r


def _rms_bwd(dy, xh, r, g):
    dg = jnp.sum(dy * xh, axis=0, keepdims=True)
    dyg = dy * g
    dx = r * (dyg - xh * jnp.mean(dyg * xh, axis=-1, keepdims=True))
    return dx, dg


def _norm_mm(h, g, w, nj, tn, *, act, name, tm):
    w_arr, w_block, w_imap = w
    rows, d = h.shape

    def body(h_ref, g_ref, w_ref, y_ref, xn_ref):
        @pl.when(pl.program_id(1) == 0)
        def _():
            xh, _ = _rms(h_ref[...])
            xn_ref[...] = (xh * g_ref[...]).astype(BF)

        u = _dot(xn_ref[...], w_ref[...].reshape(d, tn))
        if act:
            a = jnp.maximum(u, 0.0)
            y_ref[...] = (a * a).astype(BF)
        else:
            y_ref[...] = u

    return _call(
        body, name, (rows // tm, nj),
        [pl.BlockSpec((tm, d), lambda i, j: (i, 0)), pl.BlockSpec((1, d), lambda i, j: (0, 0)),
         pl.BlockSpec(w_block, lambda i, j: w_imap(j))],
        [pl.BlockSpec((tm, tn), lambda i, j: (i, j)), pl.BlockSpec((tm, d), lambda i, j: (i, 0))],
        [SDS((rows, nj * tn), BF if act else F32), SDS((rows, d), BF)],
        semantics=("parallel", "arbitrary"))(h, g, w_arr)


def _mm_res(res, a, w, *, name, tm, after=None):
    w_arr, w_block, w_imap = w
    rows, k = a.shape
    n = res.shape[1]

    def body(res_ref, a_ref, w_ref, *rest):
        rest[-1][...] = res_ref[...] + _dot(a_ref[...], w_ref[...].reshape(k, n))

    extra = [] if after is None else [after]
    return _call(
        body, name, (rows // tm,),
        [pl.BlockSpec((tm, n), lambda i: (i, 0)), pl.BlockSpec((tm, k), lambda i: (i, 0)),
         pl.BlockSpec(w_block, lambda i: w_imap(0))] + [ANY] * len(extra),
        pl.BlockSpec((tm, n), lambda i: (i, 0)), SDS((rows, n), F32), semantics=("parallel",))(res, a, w_arr, *extra)


def _mm_nt(dy, w, k, *, name, tm):
    w_arr, w_block, w_imap = w
    rows, n = dy.shape

    def body(dy_ref, w_ref, o_ref):
        o_ref[...] = _dot_nt(dy_ref[...], w_ref[...].reshape(k, n))

    return _call(
        body, name, (rows // tm,),
        [pl.BlockSpec((tm, n), lambda i: (i, 0)), pl.BlockSpec(w_block, lambda i: w_imap(0))],
        pl.BlockSpec((tm, k), lambda i: (i, 0)), SDS((rows, k), F32), semantics=("parallel",))(dy, w_arr)


def _mm_nt_relu2(dh, hh, w, nj, *, name, tm):
    w_arr, w_block, w_imap = w
    rows, d = dh.shape
    tk = hh.shape[1] // nj

    def body(dh_ref, hh_ref, w_ref, o_ref):
        dhh = _dot_nt(dh_ref[...], w_ref[...].reshape(tk, d))
        o_ref[...] = (dhh * (2.0 * jnp.sqrt(hh_ref[...].astype(F32)))).astype(BF)

    return _call(
        body, name, (rows // tm, nj),
        [pl.BlockSpec((tm, d), lambda i, j: (i, 0)), pl.BlockSpec((tm, tk), lambda i, j: (i, j)),
         pl.BlockSpec(w_block, lambda i, j: w_imap(j))],
        pl.BlockSpec((tm, tk), lambda i, j: (i, j)), SDS((rows, nj * tk), BF),
        semantics=("parallel", "parallel"))(dh, hh, w_arr)


def _mm_nt_normbwd(dy, w, nsplit, h, g, dres, *, name, tm):
    w_arr, w_block, w_imap = w
    rows, n = dy.shape
    d = h.shape[1]
    ns = n // nsplit

    def body(dy_ref, w_ref, h_ref, g_ref, dres_ref, o_ref, obf_ref, dg_ref):
        if nsplit == 1:
            dxn = _dot_nt(dy_ref[...].astype(BF), w_ref[...].reshape(d, n))
        else:
            dxn = _dot_nt(dy_ref[:, 0:ns].astype(BF), w_ref[0])
            for s in range(1, nsplit):
                dxn += _dot_nt(dy_ref[:, s * ns:(s + 1) * ns].astype(BF), w_ref[s])
        xh, r = _rms(h_ref[...])
        dx, dg = _rms_bwd(dxn, xh, r, g_ref[...])
        out = dres_ref[...] + dx
        o_ref[...] = out
        obf_ref[...] = out.astype(BF)

        @pl.when(pl.program_id(0) == 0)
        def _():
            dg_ref[...] = jnp.zeros_like(dg_ref)

        dg_ref[...] += dg

    row = lambda i: (i, 0)
    return _call(
        body, name, (rows // tm,),
        [pl.BlockSpec((tm, n), row), pl.BlockSpec(w_block, lambda i: w_imap(0)), pl.BlockSpec((tm, d), row),
         pl.BlockSpec((1, d), lambda i: (0, 0)), pl.BlockSpec((tm, d), row)],
        [pl.BlockSpec((tm, d), row), pl.BlockSpec((tm, d), row), pl.BlockSpec((1, d), lambda i: (0, 0))],
        [SDS((rows, d), F32), SDS((rows, d), BF), SDS((1, d), F32)],
        semantics=("arbitrary",))(dy, w_arr, h, g, dres)


def _mm_tn(x, dy, packed, out_imap, tk, tn, *, name):
    s_len, k = x.shape
    n = dy.shape[1]

    def body(x_ref, dy_ref, _, o_ref):
        o_ref[0] = _dot_tn(x_ref[...], dy_ref[...].astype(BF))

    return _call(
        body, name, (k // tk, n // tn),
        [pl.BlockSpec((s_len, tk), lambda i, j: (0, i)), pl.BlockSpec((s_len, tn), lambda i, j: (0, j)), ANY],
        pl.BlockSpec((1, tk, tn), out_imap), SDS(packed.shape, packed.dtype),
        semantics=("parallel", "parallel"), aliases={2: 0})(x, dy, packed)


def _loss_head(y, tgt, *, tm):
    rows, d = y.shape

    def body(y_ref, t_ref, dh_ref, dhbf_ref, loss_ref):
        err = y_ref[...] - t_ref[...]
        dh = err * (1.0 / d)
        dh_ref[...] = dh
        dhbf_ref[...] = dh.astype(BF)

        @pl.when(pl.program_id(0) == 0)
        def _():
            loss_ref[...] = jnp.zeros_like(loss_ref)

        loss_ref[...] += 0.5 * jnp.sum(jnp.mean(err * err, axis=-1, keepdims=True), axis=0, keepdims=True)

    row = lambda i: (i, 0)
    return _call(
        body, "loss_head", (rows // tm,), [pl.BlockSpec((tm, d), row), pl.BlockSpec((tm, d), row)],
        [pl.BlockSpec((tm, d), row), pl.BlockSpec((tm, d), row), pl.BlockSpec((1, 1), lambda i: (0, 0))],
        [SDS((rows, d), F32), SDS((rows, d), BF), SDS((1, 1), F32)], semantics=("arbitrary",))(y, tgt)


def _hs(h):
    return slice(HEAD * h, HEAD * (h + 1))


def _softmax_rows(s):
    e = jnp.exp(s - jnp.max(s, axis=-1, keepdims=True))
    return e * (1.0 / jnp.sum(e, axis=-1, keepdims=True))


def _scaled_bf16(qn):
    return (qn * QK_SCALE).astype(BF)


def _mem_fwd(mq, mk, mv, gq):
    outs = []
    for h in range(KV_HEADS):
        xh, _ = _rms(mq[:, _hs(h)])
        p = _softmax_rows(_dot_nt(_scaled_bf16(xh * gq), mk[:, _hs(h)]))
        outs.append(_dot(p.astype(BF), mv[:, _hs(h)]))
    return jnp.concatenate(outs, axis=-1)


def _mem_bwd(mq, do, mk, mv, gq):
    dqs, dks, dvs, dgq = [], [], [], 0.0
    for h in range(KV_HEADS):
        xh, r = _rms(mq[:, _hs(h)])
        qn = _scaled_bf16(xh * gq)
        p = _softmax_rows(_dot_nt(qn, mk[:, _hs(h)]))
        doh = do[:, _hs(h)].astype(BF)
        dp = _dot_nt(doh, mv[:, _hs(h)])
        ds = (p * (dp - jnp.sum(p * dp, axis=-1, keepdims=True))).astype(BF)
        dq, dg = _rms_bwd(_dot(ds, mk[:, _hs(h)]) * QK_SCALE, xh, r, gq)
        dqs.append(dq)
        dgq = dgq + dg
        dks.append(_dot_tn(ds, qn))
        dvs.append(_dot_tn(p.astype(BF), doh))
    cat = lambda xs: jnp.concatenate(xs, axis=-1)
    return cat(dqs), cat(dks), cat(dvs), dgq


def _mem_kv(mkv, gk):
    ks = []
    for h in range(KV_HEADS):
        xh, _ = _rms(mkv[:, _hs(h)])
        ks.append(xh * gk)
    return jnp.concatenate(ks, axis=-1).astype(BF), mkv[:, KVW:].astype(BF)


def _mem_kv_bwd(mkv, dmk, dmv, gk):
    dxs, dgk = [], 0.0
    for h in range(KV_HEADS):
        xh, r = _rms(mkv[:, _hs(h)])
        dx, dg = _rms_bwd(dmk[:, _hs(h)], xh, r, gk)
        dxs.append(dx)
        dgk = dgk + dg
    return jnp.concatenate(dxs + [dmv], axis=-1), dgk


def _pool_select(col, gd, a2, a4, a8, a16):
    return jnp.where(col < gd, a2, jnp.where(col < 2 * gd, a4, jnp.where(col < 3 * gd, a8, a16)))


def _pool_count(t0, shape, gd):
    col = lax.broadcasted_iota(jnp.int32, shape, 1)
    t = t0 + lax.broadcasted_iota(jnp.int32, shape, 0)
    win = _pool_select(col, gd, *POOL_WINDOWS)
    return jnp.minimum(t + 1, win).astype(F32)


def _pool_diff(u, halo, t0, gd):
    c = jnp.concatenate([halo, u], axis=0)
    s2 = c + pltpu.roll(c, 1, 0)
    s4 = s2 + pltpu.roll(s2, 2, 0)
    s8 = s4 + pltpu.roll(s4, 4, 0)
    s16 = s8 + pltpu.roll(s8, 8, 0)
    col = lax.broadcasted_iota(jnp.int32, c.shape, 1)
    ws = _pool_select(col, gd, s2, s4, s8, s16)[HALO:]
    return ws / _pool_count(t0, u.shape, gd) - u


def _pool_diff_bwd(dd, dd_halo, t0, gd):
    t = dd.shape[0]
    z = jnp.concatenate([dd / _pool_count(t0, dd.shape, gd), dd_halo / _pool_count(t0 + t, dd_halo.shape, gd)], axis=0)
    n = z.shape[0]
    f2 = z + pltpu.roll(z, n - 1, 0)
    f4 = f2 + pltpu.roll(f2, n - 2, 0)
    f8 = f4 + pltpu.roll(f4, n - 4, 0)
    f16 = f8 + pltpu.roll(f8, n - 8, 0)
    col = lax.broadcasted_iota(jnp.int32, z.shape, 1)
    return _pool_select(col, gd, f2, f4, f8, f16)[:t] - dd


def _swa_bias(n):
    qi = lax.broadcasted_iota(jnp.int32, (WINDOW, 2 * WINDOW), 0)
    kj = lax.broadcasted_iota(jnp.int32, (WINDOW, 2 * WINDOW), 1)
    dist = qi + WINDOW - kj
    valid = (dist >= 0) & (dist < WINDOW) & ((kj >= WINDOW) | (n > 0))
    return dist.astype(F32), valid


def _slopes(qh):
    return [2.0 ** (-8.0 * (h + 1) / qh) for h in range(qh)]


def _swa_probs(qn, kk, dist, valid, slope, sink):
    s = _dot_nt(qn, kk) - slope * dist
    s = jnp.where(valid, s, NEG)
    m = jnp.maximum(jnp.max(s, axis=-1, keepdims=True), sink)
    e = jnp.exp(s - m)
    es = jnp.exp(sink - m)
    z = jnp.sum(e, axis=-1, keepdims=True) + es
    inv = 1.0 / z
    return e * inv, es * inv


def _swa_group(q, kh, grp, n, qh, sinks):
    heads = range(kh * grp, (kh + 1) * grp)
    dist, valid = _swa_bias(n)
    slopes = _slopes(qh)
    rows = lambda vals: jnp.concatenate([jnp.broadcast_to(v, (WINDOW, 1)) for v in vals], axis=0)
    qs = jnp.concatenate([q[:, _hs(h)] for h in heads], axis=0)
    slope = rows([jnp.full((1, 1), slopes[h], F32) for h in heads])
    sink = rows([sinks[:, h:h + 1] for h in heads])
    return qs, slope, sink, jnp.concatenate([dist] * grp, axis=0), jnp.concatenate([valid] * grp, axis=0)


def _swa_fwd(q, kk, vv, gq, sinks, n, qh):
    grp = qh // KV_HEADS
    outs = []
    for kh in range(KV_HEADS):
        qs, slope, sink, dist, valid = _swa_group(q, kh, grp, n, qh, sinks)
        xh, _ = _rms(qs)
        p, _ = _swa_probs(_scaled_bf16(xh * gq), kk[:, _hs(kh)], dist, valid, slope, sink)
        o = _dot(p.astype(BF), vv[:, _hs(kh)])
        outs += [o[g * WINDOW:(g + 1) * WINDOW] for g in range(grp)]
    return jnp.concatenate(outs, axis=-1)


def _swa_bwd(q, do, kk, vv, gq, sinks, n, qh):
    grp = qh // KV_HEADS
    lane = lax.broadcasted_iota(jnp.int32, (1, LANE), 1)
    dqs, dks, dvs, dgq, dsk = [], [], [], 0.0, jnp.zeros((1, LANE), F32)
    for kh in range(KV_HEADS):
        qs, slope, sink, dist, valid = _swa_group(q, kh, grp, n, qh, sinks)
        xh, r = _rms(qs)
        qn = _scaled_bf16(xh * gq)
        p, ps = _swa_probs(qn, kk[:, _hs(kh)], dist, valid, slope, sink)
        dos = jnp.concatenate([do[:, _hs(h)] for h in range(kh * grp, (kh + 1) * grp)], axis=0).astype(BF)
        dp = _dot_nt(dos, vv[:, _hs(kh)])
        delta = jnp.sum(p * dp, axis=-1, keepdims=True)
        ds = (p * (dp - delta)).astype(BF)
        dsink = ps * delta
        for g in range(grp):
            part = -jnp.sum(dsink[g * WINDOW:(g + 1) * WINDOW], axis=0, keepdims=True)
            dsk = dsk + jnp.where(lane == kh * grp + g, part, 0.0)
        dq, dg = _rms_bwd(_dot(ds, kk[:, _hs(kh)]) * QK_SCALE, xh, r, gq)
        dqs += [dq[g * WINDOW:(g + 1) * WINDOW] for g in range(grp)]
        dgq = dgq + dg
        dks.append(_dot_tn(ds, qn))
        dvs.append(_dot_tn(p.astype(BF), dos))
    cat = lambda xs: jnp.concatenate(xs, axis=-1)
    return cat(dqs), cat(dks), cat(dvs), dgq, dsk


def _mixer_pool_fwd(proj, mkv, pbd, scale, gq, gk, *, name, tm):
    s_len, d = proj.shape
    main = d - KVW
    gd = main // len(POOL_WINDOWS)
    mlen = mkv.shape[0]
    hb = tm // HALO

    def body(u_ref, halo_ref, mq_ref, mkv_ref, pbd_ref, scale_ref, gq_ref, gk_ref, o_ref, mk_s, mv_s):
        i = pl.program_id(0)

        @pl.when(i == 0)
        def _():
            mk, mv = _mem_kv(mkv_ref[...], gk_ref[...])
            mk_s[...] = mk
            mv_s[...] = mv

        halo = jnp.where(i > 0, halo_ref[...], 0.0)
        dif = _pool_diff(u_ref[...], halo, i * tm, gd)
        mixed = _dot(dif.astype(BF), pbd_ref[...]) * scale_ref[...]
        mem = _mem_fwd(mq_ref[...], mk_s[...], mv_s[...], gq_ref[...])
        o_ref[...] = jnp.concatenate([mixed, mem], axis=-1).astype(BF)

    full = lambda shape: pl.BlockSpec(shape, lambda i: (0,) * len(shape))
    return _call(
        body, name, (s_len // tm,),
        [pl.BlockSpec((tm, main), lambda i: (i, 0)),
         pl.BlockSpec((HALO, main), lambda i: (jnp.maximum(i * hb - 1, 0), 0)),
         pl.BlockSpec((tm, KVW), lambda i: (i, main // KVW)),
         full((mlen, 2 * KVW)), full((main, main)), full((1, main)), full((1, HEAD)), full((1, HEAD))],
        pl.BlockSpec((tm, d), lambda i: (i, 0)), SDS((s_len, d), BF),
        scratch=[pltpu.VMEM((mlen, KVW), BF), pltpu.VMEM((mlen, KVW), BF)],
        semantics=("arbitrary",))(proj, proj, proj, mkv, pbd, scale, gq, gk)


def _mixer_pool_bwd(proj, dcat, mkv, pbd, scale, gq, gk, *, name, tm):
    s_len, d = proj.shape
    main = d - KVW
    gd = main // len(POOL_WINDOWS)
    mlen = mkv.shape[0]
    hb = tm // HALO
    nt = s_len // tm
    last_halo = s_len // HALO - 1

    def body(u_ref, halo_ref, mq_ref, do_ref, donext_ref, dom_ref, mkv_ref, pbd_ref, scale_ref, gq_ref, gk_ref,
             dproj_ref, dpbd_ref, dscale_ref, dmkv_ref, dgq_ref, dgk_ref, mk_s, mv_s, dmk_s, dmv_s):
        i = pl.program_id(0)

        @pl.when(i == 0)
        def _():
            mk, mv = _mem_kv(mkv_ref[...], gk_ref[...])
            mk_s[...] = mk
            mv_s[...] = mv
            dmk_s[...] = jnp.zeros_like(dmk_s)
            dmv_s[...] = jnp.zeros_like(dmv_s)
            dpbd_ref[...] = jnp.zeros_like(dpbd_ref)
            dscale_ref[...] = jnp.zeros_like(dscale_ref)
            dgq_ref[...] = jnp.zeros_like(dgq_ref)

        pbd = pbd_ref[...]
        scale = scale_ref[...]
        halo = jnp.where(i > 0, halo_ref[...], 0.0)
        dif = _pool_diff(u_ref[...], halo, i * tm, gd).astype(BF)
        do = do_ref[...]
        dscale_ref[...] += jnp.sum(do * _dot(dif, pbd), axis=0, keepdims=True)
        dmixed = (do * scale).astype(BF)
        dpbd_ref[...] += _dot_tn(dif, dmixed)
        dd = _dot_nt(dmixed, pbd)
        donext = jnp.where(i < nt - 1, donext_ref[...], 0.0)
        dd_halo = _dot_nt((donext * scale).astype(BF), pbd)
        du = _pool_diff_bwd(dd, dd_halo, i * tm, gd)

        dmq, dmk, dmv, dgq = _mem_bwd(mq_ref[...], dom_ref[...], mk_s[...], mv_s[...], gq_ref[...])
        dmk_s[...] += dmk
        dmv_s[...] += dmv
        dgq_ref[...] += dgq
        dproj_ref[...] = jnp.concatenate([du, dmq], axis=-1).astype(BF)

        @pl.when(i == nt - 1)
        def _():
            dmkv, dgk = _mem_kv_bwd(mkv_ref[...], dmk_s[...], dmv_s[...], gk_ref[...])
            dmkv_ref[...] = dmkv
            dgk_ref[...] = dgk

    full = lambda shape: pl.BlockSpec(shape, lambda i: (0,) * len(shape))
    return _call(
        body, name, (nt,),
        [pl.BlockSpec((tm, main), lambda i: (i, 0)),
         pl.BlockSpec((HALO, main), lambda i: (jnp.maximum(i * hb - 1, 0), 0)),
         pl.BlockSpec((tm, KVW), lambda i: (i, main // KVW)),
         pl.BlockSpec((tm, main), lambda i: (i, 0)),
         pl.BlockSpec((HALO, main), lambda i: (jnp.minimum((i + 1) * hb, last_halo), 0)),
         pl.BlockSpec((tm, KVW), lambda i: (i, main // KVW)),
         full((mlen, 2 * KVW)), full((main, main)), full((1, main)), full((1, HEAD)), full((1, HEAD))],
        [pl.BlockSpec((tm, d), lambda i: (i, 0)), full((main, main)), full((1, main)), full((mlen, 2 * KVW)),
         full((1, HEAD)), full((1, HEAD))],
        [SDS((s_len, d), BF), SDS((main, main), F32), SDS((1, main), F32), SDS((mlen, 2 * KVW), F32),
         SDS((1, HEAD), F32), SDS((1, HEAD), F32)],
        scratch=[pltpu.VMEM((mlen, KVW), BF), pltpu.VMEM((mlen, KVW), BF), pltpu.VMEM((mlen, KVW), F32),
                 pltpu.VMEM((mlen, KVW), F32)],
        semantics=("arbitrary",))(proj, proj, proj, dcat, dcat, dcat, mkv, pbd, scale, gq, gk)


def _mixer_swa_fwd(proj, kn, v, mkv, gqs, sinks, gq, gk, *, name):
    s_len, d = proj.shape
    main = d - KVW
    qh = main // HEAD
    mlen = mkv.shape[0]
    tm = WINDOW

    def body(q_ref, mq_ref, kp_ref, kc_ref, vp_ref, vc_ref, mkv_ref, gqs_ref, sinks_ref, gq_ref, gk_ref, o_ref,
             mk_s, mv_s):
        n = pl.program_id(0)

        @pl.when(n == 0)
        def _():
            mk, mv = _mem_kv(mkv_ref[...], gk_ref[...])
            mk_s[...] = mk
            mv_s[...] = mv

        kk = jnp.concatenate([kp_ref[...], kc_ref[...]], axis=0)
        vv = jnp.concatenate([vp_ref[...], vc_ref[...]], axis=0)
        att = _swa_fwd(q_ref[...], kk, vv, gqs_ref[...], sinks_ref[...], n, qh)
        mem = _mem_fwd(mq_ref[...], mk_s[...], mv_s[...], gq_ref[...])
        o_ref[...] = jnp.concatenate([att, mem], axis=-1).astype(BF)

    full = lambda shape: pl.BlockSpec(shape, lambda i: (0,) * len(shape))
    prev = lambda i: (jnp.maximum(i - 1, 0), 0)
    cur = lambda i: (i, 0)
    return _call(
        body, name, (s_len // tm,),
        [pl.BlockSpec((tm, main), cur), pl.BlockSpec((tm, KVW), lambda i: (i, main // KVW)),
         pl.BlockSpec((tm, KVW), prev), pl.BlockSpec((tm, KVW), cur),
         pl.BlockSpec((tm, KVW), prev), pl.BlockSpec((tm, KVW), cur),
         full((mlen, 2 * KVW)), full((1, HEAD)), full((1, LANE)), full((1, HEAD)), full((1, HEAD))],
        pl.BlockSpec((tm, d), cur), SDS((s_len, d), BF),
        scratch=[pltpu.VMEM((mlen, KVW), BF), pltpu.VMEM((mlen, KVW), BF)],
        semantics=("arbitrary",))(proj, proj, kn, kn, v, v, mkv, gqs, sinks, gq, gk)


def _mixer_swa_bwd(proj, dcat, kn, v, mkv, gqs, sinks, gq, gk, *, name):
    s_len, d = proj.shape
    main = d - KVW
    qh = main // HEAD
    mlen = mkv.shape[0]
    tm = WINDOW
    nt = s_len // tm

    def body(q_ref, mq_ref, do_ref, dom_ref, kp_ref, kc_ref, vp_ref, vc_ref, mkv_ref, gqs_ref, sinks_ref, gq_ref,
             gk_ref, dproj_ref, dk_ref, dv_ref, dmkv_ref, dgqs_ref, dsinks_ref, dgq_ref, dgk_ref,
             mk_s, mv_s, dmk_s, dmv_s):
        n = pl.program_id(0)

        @pl.when(n == 0)
        def _():
            mk, mv = _mem_kv(mkv_ref[...], gk_ref[...])
            mk_s[...] = mk
            mv_s[...] = mv
            dmk_s[...] = jnp.zeros_like(dmk_s)
            dmv_s[...] = jnp.zeros_like(dmv_s)
            dk_ref[...] = jnp.zeros_like(dk_ref)
            dv_ref[...] = jnp.zeros_like(dv_ref)
            dgqs_ref[...] = jnp.zeros_like(dgqs_ref)
            dsinks_ref[...] = jnp.zeros_like(dsinks_ref)
            dgq_ref[...] = jnp.zeros_like(dgq_ref)

        kk = jnp.concatenate([kp_ref[...], kc_ref[...]], axis=0)
        vv = jnp.concatenate([vp_ref[...], vc_ref[...]], axis=0)
        dq, dkk, dvv, dgqs, dsk = _swa_bwd(q_ref[...], do_ref[...], kk, vv, gqs_ref[...], sinks_ref[...], n, qh)
        prev = pl.ds(pl.multiple_of(jnp.maximum(n - 1, 0) * tm, tm), tm)
        own = pl.ds(pl.multiple_of(n * tm, tm), tm)
        dk_ref[prev, :] += dkk[:tm]
        dk_ref[own, :] += dkk[tm:]
        dv_ref[prev, :] += dvv[:tm]
        dv_ref[own, :] += dvv[tm:]
        dgqs_ref[...] += dgqs
        dsinks_ref[...] += dsk

        dmq, dmk, dmv, dgq = _mem_bwd(mq_ref[...], dom_ref[...], mk_s[...], mv_s[...], gq_ref[...])
        dmk_s[...] += dmk
        dmv_s[...] += dmv
        dgq_ref[...] += dgq
        dproj_ref[...] = jnp.concatenate([dq, dmq], axis=-1).astype(BF)

        @pl.when(n == nt - 1)
        def _():
            dmkv, dgk = _mem_kv_bwd(mkv_ref[...], dmk_s[...], dmv_s[...], gk_ref[...])
            dmkv_ref[...] = dmkv
            dgk_ref[...] = dgk

    full = lambda shape: pl.BlockSpec(shape, lambda i: (0,) * len(shape))
    prev_b = lambda i: (jnp.maximum(i - 1, 0), 0)
    cur = lambda i: (i, 0)
    memcol = lambda i: (i, main // KVW)
    return _call(
        body, name, (nt,),
        [pl.BlockSpec((tm, main), cur), pl.BlockSpec((tm, KVW), memcol),
         pl.BlockSpec((tm, main), cur), pl.BlockSpec((tm, KVW), memcol),
         pl.BlockSpec((tm, KVW), prev_b), pl.BlockSpec((tm, KVW), cur),
         pl.BlockSpec((tm, KVW), prev_b), pl.BlockSpec((tm, KVW), cur),
         full((mlen, 2 * KVW)), full((1, HEAD)), full((1, LANE)), full((1, HEAD)), full((1, HEAD))],
        [pl.BlockSpec((tm, d), cur), full((s_len, KVW)), full((s_len, KVW)), full((mlen, 2 * KVW)),
         full((1, HEAD)), full((1, LANE)), full((1, HEAD)), full((1, HEAD))],
        [SDS((s_len, d), BF), SDS((s_len, KVW), F32), SDS((s_len, KVW), F32), SDS((mlen, 2 * KVW), F32),
         SDS((1, HEAD), F32), SDS((1, LANE), F32), SDS((1, HEAD), F32), SDS((1, HEAD), F32)],
        scratch=[pltpu.VMEM((mlen, KVW), BF), pltpu.VMEM((mlen, KVW), BF), pltpu.VMEM((mlen, KVW), F32),
                 pltpu.VMEM((mlen, KVW), F32)],
        semantics=("arbitrary",))(proj, proj, dcat, dcat, kn, kn, v, v, mkv, gqs, sinks, gq, gk)


def _kv_prep(kv, gk, *, tm):
    s_len = kv.shape[0]

    def body(kv_ref, gk_ref, k_ref, v_ref):
        k, v = _mem_kv(kv_ref[...], gk_ref[...])
        k_ref[...] = k
        v_ref[...] = v

    row = lambda i: (i, 0)
    return _call(
        body, "kv_prep", (s_len // tm,),
        [pl.BlockSpec((tm, 2 * KVW), row), pl.BlockSpec((1, HEAD), lambda i: (0, 0))],
        [pl.BlockSpec((tm, KVW), row), pl.BlockSpec((tm, KVW), row)],
        [SDS((s_len, KVW), BF), SDS((s_len, KVW), BF)], semantics=("parallel",))(kv, gk)


def _kv_bwd(kv, dks, dvs, gk, *, tm):
    s_len = kv.shape[0]
    nl = len(dks)

    def body(*refs):
        kv_ref, gk_ref = refs[0], refs[1]
        dk_refs, dv_refs = refs[2:2 + nl], refs[2 + nl:2 + 2 * nl]
        dkv_ref, dgk_ref = refs[2 + 2 * nl], refs[3 + 2 * nl]
        dk, dv = dk_refs[0][...], dv_refs[0][...]
        for t in range(1, nl):
            dk = dk + dk_refs[t][...]
            dv = dv + dv_refs[t][...]
        dkv, dgk = _mem_kv_bwd(kv_ref[...], dk, dv, gk_ref[...])
        dkv_ref[...] = dkv.astype(BF)

        @pl.when(pl.program_id(0) == 0)
        def _():
            dgk_ref[...] = jnp.zeros_like(dgk_ref)

        dgk_ref[...] += dgk

    row = lambda i: (i, 0)
    one = pl.BlockSpec((1, HEAD), lambda i: (0, 0))
    return _call(
        body, "kv_bwd", (s_len // tm,),
        [pl.BlockSpec((tm, 2 * KVW), row), one] + [pl.BlockSpec((tm, KVW), row)] * (2 * nl),
        [pl.BlockSpec((tm, 2 * KVW), row), one],
        [SDS((s_len, 2 * KVW), BF), SDS((1, HEAD), F32)], semantics=("arbitrary",))(kv, gk, *dks, *dvs)


def _place():
    x, y, c = lax.axis_index("x"), lax.axis_index("y"), lax.axis_index("c")
    flips = [(1 - x, y), (x, 1 - y), (1 - x, 1 - y)]
    return x, y, c, flips


def _remote(src, dst, send_sem, recv_sem, to):
    return pltpu.make_async_remote_copy(src_ref=src, dst_ref=dst, send_sem=send_sem, recv_sem=recv_sem,
                                        device_id=to, device_id_type=MESH)


def _gather_copies(p_refs, wg_refs, send, recv):
    x, y, c, flips = _place()
    chip = 2 * x + y
    cps = []
    for j, (fx, fy) in enumerate(flips):
        for b in range(2):
            half = p_refs[b].shape[0] // 2
            mine = pl.ds(c * half, half)
            cps.append(_remote(p_refs[b].at[mine, :], wg_refs[b].at[chip, mine, :], send.at[2 * j + b],
                               recv.at[2 * j + b], (fx, fy, c)))
    return cps, cps


def _forward_copies(p_refs, wg_refs, send, recv):
    x, y, c, flips = _place()
    chip = 2 * x + y
    sib = (x, y, 1 - c)
    sends, arrivals = [], []
    for b in range(2):
        half = p_refs[b].shape[0] // 2
        own = _remote(p_refs[b], wg_refs[b].at[chip], send.at[b], recv.at[b], sib)
        sends.append(own)
        arrivals.append(own)
        for j, (fx, fy) in enumerate(flips):
            k = 2 + 3 * b + j
            landed = wg_refs[b].at[2 * fx + fy, pl.ds(c * half, half), :]
            other = wg_refs[b].at[2 * fx + fy, pl.ds((1 - c) * half, half), :]
            sends.append(_remote(landed, landed, send.at[k], recv.at[k], sib))
            arrivals.append(_remote(other, other, send.at[k], recv.at[k], sib))
    return sends, arrivals


def _swap_copies(g_refs, r_refs, send, recv):
    x, y, c, _ = _place()
    cps = []
    for b in range(2):
        half = g_refs[b].shape[1] // 2
        cps.append(_remote(g_refs[b].at[:, pl.ds((1 - c) * half, half), :], r_refs[b], send.at[b], recv.at[b],
                           (x, y, 1 - c)))
    return cps, cps


def _split_start(make_copies, n_sems, bufs, after, fresh, *, name):
    def body(a1, a2, b1, b2, after_ref, send, recv, *outs):
        for cp in make_copies((a1, a2), (b1, b2), send, recv)[0]:
            cp.start()
        outs[4][...] = jnp.zeros_like(outs[4])

    extra_shape = () if fresh is None else (pltpu.HBM(fresh, F32),)
    extra_spec = () if fresh is None else (HBM,)
    return pl.pallas_call(
        body, name=name,
        out_shape=(pltpu.SemaphoreType.DMA((n_sems,)), pltpu.SemaphoreType.DMA((n_sems,)))
        + tuple(pltpu.HBM(b.shape, b.dtype) for b in bufs) + (SDS((8, LANE), F32),) + extra_shape,
        in_specs=(HBM, HBM, HBM, HBM, ANY), out_specs=(SEM, SEM, HBM, HBM, HBM, HBM, VMEM_WHOLE) + extra_spec,
        input_output_aliases={0: 2, 1: 3, 2: 4, 3: 5},
        compiler_params=pltpu.CompilerParams(has_side_effects=SIDE_EFFECT))(*[_in_hbm(b) for b in bufs], after)


def _split_wait(make_copies, started, after, *, name):
    send, recv, bufs = started[0], started[1], started[2:6]

    def body(a1, a2, b1, b2, send_ref, recv_ref, after_ref, *outs):
        sends, arrivals = make_copies((a1, a2), (b1, b2), send_ref, recv_ref)
        for cp in arrivals:
            cp.wait_recv()
        for cp in sends:
            cp.wait_send()

    return pl.pallas_call(
        body, name=name, out_shape=tuple(pltpu.HBM(b.shape, b.dtype) for b in bufs),
        in_specs=(HBM, HBM, HBM, HBM, SEM, SEM, ANY), out_specs=(HBM, HBM, HBM, HBM),
        input_output_aliases={0: 0, 1: 1, 2: 2, 3: 3},
        compiler_params=pltpu.CompilerParams(has_side_effects=SIDE_EFFECT))(*bufs, send, recv, after)


def _gather_small(ps):
    def body(ps_ref, o_ref, send, recv):
        x, y, c, flips = _place()
        chip = 2 * x + y
        o_ref[chip] = ps_ref[...]
        cps = [_remote(ps_ref, o_ref.at[chip], send.at[j], recv.at[j], (fx, fy, c))
               for j, (fx, fy) in enumerate(flips)]
        for cp in cps:
            cp.start()
        for j, (fx, fy) in enumerate(flips):
            _remote(ps_ref, o_ref.at[2 * fx + fy], send.at[j], recv.at[j], (fx, fy, c)).wait_recv()
        for cp in cps:
            cp.wait_send()

    return pl.pallas_call(
        body, name="gather_small", in_specs=[VMEM_WHOLE], out_specs=VMEM_WHOLE,
        out_shape=SDS((N_CHIPS,) + ps.shape, ps.dtype),
        scratch_shapes=[pltpu.SemaphoreType.DMA((3,)), pltpu.SemaphoreType.DMA((3,))])(ps)


def _sum_sibling(g, r, place, *, tm, name):
    n_sh, half, w = r.shape
    nt = half // tm

    def body(place_ref, g_ref, r_ref, pbf_ref, own_ref):
        s = pl.program_id(1)
        p = g_ref[0] + r_ref[0]
        pbf_ref[0] = p.astype(BF)

        @pl.when(s == place_ref[1])
        def _():
            own_ref[...] = p

    return _call(
        body, name, (nt, n_sh),
        [pl.BlockSpec((1, tm, w), lambda i, s, pr: (s, pr[0] * nt + i, 0)),
         pl.BlockSpec((1, tm, w), lambda i, s, pr: (s, i, 0))],
        [pl.BlockSpec((1, tm, w), lambda i, s, pr: (s, i, 0)), pl.BlockSpec((tm, w), lambda i, s, pr: (i, 0))],
        [SDS((n_sh, half, w), BF), SDS((half, w), F32)],
        semantics=("arbitrary", "arbitrary"), prefetch=1)(place, g, r)


def _rs_copies(p_refs, land_refs, send, recv):
    _, _, c, flips = _place()
    cps = []
    for j, (fx, fy) in enumerate(flips):
        for b in range(2):
            cps.append(_remote(p_refs[b].at[2 * fx + fy], land_refs[b].at[j], send.at[2 * j + b], recv.at[2 * j + b],
                               (fx, fy, c)))
    return cps, cps


def _sum_chips(own, r, full, layer, place, *, tm, name):
    half, w = own.shape

    def body(place_ref, own_ref, r_ref, _, o_ref):
        o_ref[0, 0] = ((own_ref[...] + r_ref[0].astype(F32)) + r_ref[1].astype(F32)) + r_ref[2].astype(F32)

    return _call(
        body, name, (half // tm,),
        [pl.BlockSpec((tm, w), lambda i, pr: (i, 0)), pl.BlockSpec((3, tm, w), lambda i, pr: (0, i, 0)), ANY],
        pl.BlockSpec((1, 1, tm, w), lambda i, pr: (layer, pr[0], i, 0)), SDS(full.shape, F32),
        semantics=("parallel",), prefetch=1, aliases={3: 0})(place, own, r, full)


def _share_with_sibling(f1, f2):
    def body(_, __, o1_ref, o2_ref, send, recv):
        x, y, c, _ = _place()
        mine, other = pl.ds(c, 1), pl.ds(1 - c, 1)
        cps = [_remote(o1_ref.at[:, mine], o1_ref.at[:, mine], send.at[0], recv.at[0], (x, y, 1 - c)),
               _remote(o2_ref.at[:, mine], o2_ref.at[:, mine], send.at[1], recv.at[1], (x, y, 1 - c))]
        for cp in cps:
            cp.start()
        for cp in cps:
            cp.wait_send()
        _remote(o1_ref.at[:, other], o1_ref.at[:, other], send.at[0], recv.at[0], (x, y, 1 - c)).wait_recv()
        _remote(o2_ref.at[:, other], o2_ref.at[:, other], send.at[1], recv.at[1], (x, y, 1 - c)).wait_recv()

    return pl.pallas_call(
        body, name="share_with_sibling", in_specs=[ANY, ANY], out_specs=[ANY, ANY],
        out_shape=[SDS(f1.shape, f1.dtype), SDS(f2.shape, f2.dtype)], input_output_aliases={0: 0, 1: 1},
        scratch_shapes=[pltpu.SemaphoreType.DMA((2,)), pltpu.SemaphoreType.DMA((2,))])(f1, f2)


def _allreduce_small(sg):
    rows, w = sg.shape

    def body(sg_ref, o_ref, slots, send, recv):
        x, y, c, _ = _place()
        me = 4 * x + 2 * y + c
        slots[me] = sg_ref[...]
        peers = []
        for k in range(1, 8):
            a, b, e = (k >> 2) & 1, (k >> 1) & 1, k & 1
            px = x + a - 2 * a * x
            py = y + b - 2 * b * y
            pc = c + e - 2 * e * c
            cp = _remote(sg_ref, slots.at[me], send.at[k - 1], recv.at[k - 1], (px, py, pc))
            cp.start()
            peers.append((cp, 4 * px + 2 * py + pc, (px, py, pc)))
        for k, (cp, pidx, pid) in enumerate(peers):
            _remote(sg_ref, slots.at[pidx], send.at[k], recv.at[k], pid).wait_recv()
        acc = slots[0]
        for dev in range(1, 8):
            acc = acc + slots[dev]
        o_ref[...] = acc
        for cp, _, _ in peers:
            cp.wait_send()

    vm = pl.BlockSpec(memory_space=pltpu.VMEM)
    return pl.pallas_call(
        body, name="allreduce_small", in_specs=[vm], out_specs=vm, out_shape=SDS((rows, w), F32),
        scratch_shapes=[pltpu.VMEM((8, rows, w), F32), pltpu.SemaphoreType.DMA((7,)),
                        pltpu.SemaphoreType.DMA((7,))],
        compiler_params=pltpu.CompilerParams(vmem_limit_bytes=VMEM_LIMIT_MB << 20))(sg)


def _adamw(g_arr, layer0, g_off, per_layer, w, m, v, *, name, tm):
    rows, cols = w.shape
    assert g_off % tm == 0 and per_layer % tm == 0 and rows % per_layer == 0
    npl = per_layer // tm
    c1 = 1.0 - ADAM_B1 ** ADAM_STEP
    c2 = 1.0 - ADAM_B2 ** ADAM_STEP

    def body(g_ref, w_ref, m_ref, v_ref, go_ref, d_ref, mo_ref, vo_ref):
        g = g_ref[0]
        mn = ADAM_B1 * m_ref[...] + (1.0 - ADAM_B1) * g
        vn = ADAM_B2 * v_ref[...] + (1.0 - ADAM_B2) * (g * g)
        go_ref[...] = g
        mo_ref[...] = mn
        vo_ref[...] = vn
        d_ref[...] = -ADAM_LR * ((mn / c1) / (jnp.sqrt(vn / c2) + ADAM_EPS) + ADAM_WD * w_ref[...])

    blk = pl.BlockSpec((tm, cols), lambda i: (i, 0))
    return _call(
        body, name, (rows // tm,),
        [pl.BlockSpec((1, tm, cols), lambda i: (layer0 + i // npl, g_off // tm + i % npl, 0)), blk, blk, blk],
        [blk] * 4,
        [SDS((rows, cols), F32)] * 4, semantics=("parallel",))(g_arr, w, m, v)


def _pack_small(parts, width):
    flat = jnp.concatenate([p.reshape(-1).astype(F32) for p in parts])
    rows = -(-flat.shape[0] // (8 * width)) * 8
    return jnp.pad(flat, (0, rows * width - flat.shape[0])).reshape(rows, width)


def _unpack_small(packed, shapes):
    flat = packed.reshape(-1)
    out, off = [], 0
    for shp in shapes:
        size = 1
        for n in shp:
            size *= n
        out.append(flat[off:off + size].reshape(shp))
        off += size
    return out


def _block_diag(pw):
    g, c, _ = pw.shape
    eye = jnp.eye(g, dtype=pw.dtype)
    return (eye[:, None, :, None] * pw[:, :, None, :]).reshape(g * c, g * c)


def _diag_blocks(full, g):
    c = full.shape[0] // g
    return jnp.stack([full[i * c:(i + 1) * c, i * c:(i + 1) * c] for i in range(g)])


def kernel(x, mem, norm_mix, w_in, pool_w, pool_scale, kv_norm, w_kv, k_norm, q_norm, sinks, mem_norm, w_mem_kv, mem_q_norm, mem_k_norm, w_out, norm_mlp, w_up, w_down, loss_target, m_norm_mix, m_w_in, m_pool_w, m_pool_scale, m_kv_norm, m_w_kv, m_k_norm, m_q_norm, m_sinks, m_mem_norm, m_w_mem_kv, m_mem_q_norm, m_mem_k_norm, m_w_out, m_norm_mlp, m_w_up, m_w_down, v_norm_mix, v_w_in, v_pool_w, v_pool_scale, v_kv_norm, v_w_kv, v_k_norm, v_q_norm, v_sinks, v_mem_norm, v_w_mem_kv, v_mem_q_norm, v_mem_k_norm, v_w_out, v_norm_mlp, v_w_up, v_w_down):
    s_len, d = x.shape[1], x.shape[2]
    n_layers, n_pool = norm_mix.shape[0], pool_w.shape[0]
    n_swa = n_layers - n_pool
    main = d - KVW
    qh = main // HEAD
    ff = w_down.shape[1] * N_CHIPS
    dq = d // N_CHIPS
    assert w_up.shape[2] == d and ff == N_CHIPS * d and w_kv.shape[1] == 2 * KVW
    tm = min(512, s_len)
    tm_mem = mem.shape[1]

    cx, cy, cc = lax.axis_index("x"), lax.axis_index("y"), lax.axis_index("c")
    chip = 2 * cx + cy

    off_down, off_up, off_in, off_out = 0, d, 2 * d, 2 * d + dq
    rows1 = off_out + dq
    off_mkv, off_kv = 0, dq
    rows2 = 2 * dq

    ps = jnp.pad(pool_scale, ((0, 8 - n_pool), (0, 2 * LANE - pool_scale.shape[1])))
    psg = _gather_small(ps)
    pool_scale_full = jnp.concatenate([psg[k, :n_pool, :pool_scale.shape[1]] for k in range(N_CHIPS)], axis=1)

    def packed_weights(l):
        p1 = jnp.concatenate([w_down[l], w_up[l], w_in[l], w_out[l]]).astype(BF)
        p2 = jnp.concatenate([w_mem_kv[l], w_kv] if l == n_pool else [w_mem_kv[l]]).astype(BF)
        return p1, p2

    def gather_start(l, after):
        p1, p2 = packed_weights(l)
        bufs = (p1, p2, lax.empty((N_CHIPS,) + p1.shape, BF), lax.empty((N_CHIPS,) + p2.shape, BF))
        return _split_start(_gather_copies, 6, bufs, after, None, name=f"gather_start_{l}")

    def gather_land(l, started, after):
        bufs = _split_wait(_gather_copies, started, after, name=f"gather_wait_{l}")
        return _split_start(_forward_copies, 8, bufs, bufs[2], None, name=f"forward_start_{l}")

    def gather_finish(l, forwarding, after):
        bufs = _split_wait(_forward_copies, forwarding, after, name=f"forward_wait_{l}")
        return bufs[2], bufs[3]

    def w_rows(arr, off, nrows, width):
        assert off % nrows == 0
        return (arr, (N_CHIPS, nrows, width), lambda j: (0, off // nrows, 0))

    def w_cols(arr, off, nrows, width):
        assert off % nrows == 0
        return (arr, (1, nrows, width), lambda j: (j, off // nrows, 0))

    row = lambda a: a.reshape(1, -1)
    h = x.reshape(s_len, d)
    memx = mem.reshape(tm_mem, d)
    tgt = loss_target.reshape(s_len, d)
    pbd = [_block_diag(pool_w[l]).astype(BF) for l in range(n_pool)]
    sinks_pad = [jnp.pad(row(sinks[j]), ((0, 0), (0, LANE - qh))) for j in range(n_swa)]

    w_in_l, w_out_l, w_down_l, w_down_cols_l, w_up_l, w_up_all_l, w_mkv_l = [], [], [], [], [], [], []
    w_kv_g = None
    forwarding = gather_land(0, gather_start(0, psg), psg)
    travelling = gather_start(1, forwarding[6]) if n_layers > 1 else None
    saved = []
    kv = hn_kv = kn = vsh = None
    for l in range(n_layers):
        wg1, wg2 = gather_finish(l, forwarding, h if l else forwarding[6])
        w_in_l.append(w_rows(wg1, off_in, dq, d))
        w_out_l.append(w_rows(wg1, off_out, dq, d))
        w_down_l.append(w_rows(wg1, off_down, d, d))
        w_down_cols_l.append(w_cols(wg1, off_down, d, d))
        w_up_l.append(w_cols(wg1, off_up, d, d))
        w_up_all_l.append((wg1, (N_CHIPS, d, d), lambda j: (0, off_up // d, 0)))
        w_mkv_l.append(w_rows(wg2, off_mkv, dq, 2 * KVW))
        g_mix = row(norm_mix[l])
        if travelling is not None:
            g_mix = g_mix + travelling[6][0, 0]
        if l == n_pool:
            w_kv_g = w_rows(wg2, off_kv, dq, 2 * KVW)
            kv, hn_kv = _norm_mm(h, row(kv_norm), w_kv_g, 1, 2 * KVW, act=False, name="kv_proj", tm=tm)
            kn, vsh = _kv_prep(kv, row(k_norm), tm=tm)
        h0 = h
        proj, xn = _norm_mm(h0, g_mix, w_in_l[l], 1, d, act=False, name=f"in_proj_{l}", tm=tm)
        mkv, memn = _norm_mm(memx, row(mem_norm[l]), w_mkv_l[l], 1, 2 * KVW, act=False, name=f"mem_kv_{l}", tm=tm_mem)
        if l < n_pool:
            cat = _mixer_pool_fwd(proj, mkv, pbd[l], row(pool_scale_full[l]), row(mem_q_norm[l]), row(mem_k_norm[l]),
                                  name=f"mixer_fwd_{l}", tm=tm)
        else:
            j = l - n_pool
            cat = _mixer_swa_fwd(proj, kn, vsh, mkv, row(q_norm[j]), sinks_pad[j], row(mem_q_norm[l]),
                                 row(mem_k_norm[l]), name=f"mixer_fwd_{l}")
        h1 = _mm_res(h0, cat, w_out_l[l], name=f"out_proj_{l}", tm=tm)
        hh, xm = _norm_mm(h1, row(norm_mlp[l]), w_up_l[l], N_CHIPS, d, act=True, name=f"mlp_up_{l}", tm=tm)
        after = None
        if travelling is not None:
            forwarding = gather_land(l + 1, travelling, hh)
            travelling = gather_start(l + 2, forwarding[6]) if l + 2 < n_layers else None
            after = forwarding[6]
        h = _mm_res(h1, hh, w_down_l[l], name=f"mlp_down_{l}", tm=tm, after=after)
        saved.append((h0, proj, xn, mkv, memn, cat, h1, hh, xm))

    dh, dh_bf, loss_part = _loss_head(h, tgt, tm=tm)
    loss = lax.psum(loss_part[0, 0], ("x", "y", "c"))

    place = jnp.stack([cc, chip]).astype(jnp.int32)
    half1, half2 = rows1 // 2, rows2 // 2
    g1 = lax.empty((N_CHIPS, rows1, d), F32)
    pending = {}
    swapping = None
    tk = min(512, d)

    def reduce_begin(l, swapped, after):
        g1_l, g2_l, r1, r2 = _split_wait(_swap_copies, swapped, after, name=f"swap_wait_{l}")
        pb1, own1 = _sum_sibling(g1_l, r1, place, tm=_tile(half1, 256), name=f"sum_sibling_a_{l}")
        pb2, own2 = _sum_sibling(g2_l, r2, place, tm=_tile(half2, 256), name=f"sum_sibling_b_{l}")
        bufs = (pb1, pb2, lax.empty((3, half1, d), BF), lax.empty((3, half2, 2 * KVW), BF))
        return _split_start(_rs_copies, 6, bufs, pb1, None, name=f"reduce_start_{l}"), own1, own2
    zeros_mem = jnp.zeros((tm_mem, d), F32)

    def rows_map(off, nrows, tkk):
        per = nrows // tkk
        return lambda i, j: (i // per, off // tkk + i % per, 0)

    def cols_map(off, tkk):
        return lambda i, j: (j, off // tkk + i, 0)

    d_norm_mix, d_norm_mlp, d_mem_norm = [None] * n_layers, [None] * n_layers, [None] * n_layers
    d_mem_q, d_mem_k = [None] * n_layers, [None] * n_layers
    d_pool_w, d_pool_scale = [None] * n_pool, [None] * n_pool
    d_q_norm, d_sinks = [None] * n_swa, [None] * n_swa
    dks, dvs = [], []
    d_kv_norm = d_k_norm = None
    for l in reversed(range(n_layers)):
        h0, proj, xn, mkv, memn, cat, h1, hh, xm = saved[l]
        g2 = jnp.zeros((N_CHIPS, rows2, 2 * KVW), F32)
        g1 = _mm_tn(hh, dh_bf, g1, rows_map(off_down, d, tk), tk, d, name=f"dw_down_{l}")
        du = _mm_nt_relu2(dh_bf, hh, w_down_cols_l[l], N_CHIPS, name=f"d_mlp_act_{l}", tm=tm)
        g1 = _mm_tn(xm, du, g1, cols_map(off_up, tk), tk, d, name=f"dw_up_{l}")
        g_mlp = row(norm_mlp[l])
        if swapping is not None:
            pending[swapping[0]] = reduce_begin(*swapping, after=g1)
            g_mlp = g_mlp + pending[swapping[0]][0][6][0, 0]
        dh1, dh1_bf, d_norm_mlp[l] = _mm_nt_normbwd(du, w_up_all_l[l], N_CHIPS, h1, g_mlp, dh,
                                                    name=f"d_mlp_in_{l}", tm=tm)
        tkq = min(tk, dq)
        g1 = _mm_tn(cat, dh1_bf, g1, rows_map(off_out, dq, tkq), tkq, d, name=f"dw_out_{l}")
        dcat = _mm_nt(dh1_bf, w_out_l[l], d, name=f"d_cat_{l}", tm=tm)
        if l < n_pool:
            dproj, dpbd, dscale, dmkv, d_mem_q[l], d_mem_k[l] = _mixer_pool_bwd(
                proj, dcat, mkv, pbd[l], row(pool_scale_full[l]), row(mem_q_norm[l]), row(mem_k_norm[l]),
                name=f"mixer_bwd_{l}", tm=tm)
            d_pool_w[l] = _diag_blocks(dpbd, len(POOL_WINDOWS))
            d_pool_scale[l] = dscale
        else:
            j = l - n_pool
            dproj, dk, dv, dmkv, d_q_norm[j], dsk, d_mem_q[l], d_mem_k[l] = _mixer_swa_bwd(
                proj, dcat, kn, vsh, mkv, row(q_norm[j]), sinks_pad[j], row(mem_q_norm[l]), row(mem_k_norm[l]),
                name=f"mixer_bwd_{l}")
            d_sinks[j] = dsk[0, :qh]
            dks.append(dk)
            dvs.append(dv)
        g1 = _mm_tn(xn, dproj, g1, rows_map(off_in, dq, tkq), tkq, d, name=f"dw_in_{l}")
        dh, dh_bf, d_norm_mix[l] = _mm_nt_normbwd(dproj, w_in_l[l], 1, h0, row(norm_mix[l]), dh1,
                                                  name=f"d_in_{l}", tm=tm)
        g2 = _mm_tn(memn, dmkv, g2, rows_map(off_mkv, dq, tkq), tkq, 2 * KVW, name=f"dw_mem_kv_{l}")
        _, _, d_mem_norm[l] = _mm_nt_normbwd(dmkv, w_mkv_l[l], 1, memx, row(mem_norm[l]), zeros_mem,
                                             name=f"d_mem_norm_{l}", tm=tm_mem)
        if l == n_pool:
            dkv, d_k_norm = _kv_bwd(kv, dks, dvs, row(k_norm), tm=tm)
            g2 = _mm_tn(hn_kv, dkv, g2, rows_map(off_kv, dq, tkq), tkq, 2 * KVW, name="dw_kv")
            dh, dh_bf, d_kv_norm = _mm_nt_normbwd(dkv, w_kv_g, 1, h0, row(kv_norm), dh, name="d_kv_in", tm=tm)
        bufs = (g1, g2, lax.empty((N_CHIPS, half1, d), F32), lax.empty((N_CHIPS, half2, 2 * KVW), F32))
        swapping = (l, _split_start(_swap_copies, 2, bufs, g1, (N_CHIPS, rows1, d) if l > 0 else None,
                                    name=f"swap_start_{l}"))
        g1 = swapping[1][7] if l > 0 else None
    pending[0] = reduce_begin(*swapping, after=dh)
    grad_x = dh.reshape(x.shape)

    small_names = ["norm_mix", "pool_w", "pool_scale", "kv_norm", "k_norm", "q_norm", "sinks", "mem_norm",
                   "mem_q_norm", "mem_k_norm", "norm_mlp"]
    small_grads = {
        "norm_mix": jnp.concatenate(d_norm_mix), "pool_w": jnp.stack(d_pool_w),
        "pool_scale": jnp.concatenate(d_pool_scale), "kv_norm": d_kv_norm[0], "k_norm": d_k_norm[0],
        "q_norm": jnp.concatenate(d_q_norm), "sinks": jnp.stack(d_sinks), "mem_norm": jnp.concatenate(d_mem_norm),
        "mem_q_norm": jnp.concatenate(d_mem_q), "mem_k_norm": jnp.concatenate(d_mem_k),
        "norm_mlp": jnp.concatenate(d_norm_mlp)}
    width = d
    sg = _pack_small([small_grads[n] for n in small_names], width)
    sg = _allreduce_small(sg)
    reduced = dict(zip(small_names, _unpack_small(sg, [small_grads[n].shape for n in small_names])))
    psw = pool_scale.shape[1]
    reduced["pool_scale"] = lax.dynamic_slice_in_dim(reduced["pool_scale"], chip * psw, psw, axis=1)
    params = dict(norm_mix=(norm_mix, m_norm_mix, v_norm_mix), pool_w=(pool_w, m_pool_w, v_pool_w),
                  pool_scale=(pool_scale, m_pool_scale, v_pool_scale), kv_norm=(kv_norm, m_kv_norm, v_kv_norm),
                  k_norm=(k_norm, m_k_norm, v_k_norm), q_norm=(q_norm, m_q_norm, v_q_norm),
                  sinks=(sinks, m_sinks, v_sinks), mem_norm=(mem_norm, m_mem_norm, v_mem_norm),
                  mem_q_norm=(mem_q_norm, m_mem_q_norm, v_mem_q_norm),
                  mem_k_norm=(mem_k_norm, m_mem_k_norm, v_mem_k_norm), norm_mlp=(norm_mlp, m_norm_mlp, v_norm_mlp))
    shapes = [params[n][0].shape for n in small_names]
    packs = [_pack_small([reduced[n].reshape(params[n][0].shape) for n in small_names], width)]
    packs += [_pack_small([params[n][t] for n in small_names], width) for t in range(3)]
    res = _adamw(packs[0][None], 0, 0, packs[0].shape[0], packs[1], packs[2], packs[3], name="adamw_small", tm=8)
    small = {n: [] for n in small_names}
    for r in res:
        for n, a in zip(small_names, _unpack_small(r, shapes)):
            small[n].append(a)

    full1 = lax.empty((n_layers, 2, half1, d), F32)
    full2 = lax.empty((n_layers, 2, half2, 2 * KVW), F32)
    for l in reversed(range(n_layers)):
        exchange, own1, own2 = pending[l]
        _, _, x1, x2 = _split_wait(_rs_copies, exchange, res[0], name=f"reduce_wait_{l}")
        full1 = _sum_chips(own1, x1, full1, l, place, tm=_tile(half1, 256), name=f"sum_chips_a_{l}")
        full2 = _sum_chips(own2, x2, full2, l, place, tm=_tile(half2, 256), name=f"sum_chips_b_{l}")
    full1, full2 = _share_with_sibling(full1, full2)
    full1 = full1.reshape(n_layers, rows1, d)
    full2 = full2.reshape(n_layers, rows2, 2 * KVW)

    big = {}
    for name, arr, layer0, off, per, w_, m_, v_ in (
            ("w_down", full1, 0, off_down, d, w_down, m_w_down, v_w_down),
            ("w_up", full1, 0, off_up, d, w_up, m_w_up, v_w_up),
            ("w_in", full1, 0, off_in, dq, w_in, m_w_in, v_w_in),
            ("w_out", full1, 0, off_out, dq, w_out, m_w_out, v_w_out),
            ("w_mem_kv", full2, 0, off_mkv, dq, w_mem_kv, m_w_mem_kv, v_w_mem_kv),
            ("w_kv", full2, n_pool, off_kv, dq, w_kv, m_w_kv, v_w_kv)):
        cols = arr.shape[2]
        upd = _adamw(arr, layer0, off, per, w_.reshape(-1, cols), m_.reshape(-1, cols), v_.reshape(-1, cols),
                     name=f"adamw_{name}", tm=min(256, dq))
        big[name] = [r.reshape(w_.shape) for r in upd]

    order = ["norm_mix", "w_in", "pool_w", "pool_scale", "kv_norm", "w_kv", "k_norm", "q_norm", "sinks", "mem_norm",
             "w_mem_kv", "mem_q_norm", "mem_k_norm", "w_out", "norm_mlp", "w_up", "w_down"]
    out = {**big, **small}
    return (loss, grad_x, *[out[n][0] for n in order], *[out[n][1] for n in order],
            *[out[n][2] for n in order], *[out[n][3] for n in order])
```

```python
import functools

import jax
import jax.numpy as jnp
from jax import lax
from jax.experimental import pallas as pl
from jax.experimental.pallas import tpu as pltpu

F32, BF = jnp.float32, jnp.bfloat16
SDS = jax.ShapeDtypeStruct
MESH = pl.DeviceIdType.MESH
ANY = pl.BlockSpec(memory_space=pl.ANY)
HBM = pl.BlockSpec(memory_space=pltpu.HBM)
SEM = pl.BlockSpec(memory_space=pltpu.SEMAPHORE)
VMEM_WHOLE = pl.BlockSpec(memory_space=pltpu.VMEM)
SIDE_EFFECT = pltpu.SideEffectType.DATAFLOW_SIDE_EFFECTING


def _in_hbm(a):
    return pltpu.with_memory_space_constraint(a, pltpu.HBM)


EPS = 1e-6
HEAD = 64
KV_HEADS = 4
KVW = KV_HEADS * HEAD
WINDOW = 128
POOL_WINDOWS = (2, 4, 8, 16)
HALO = 16
QK_SCALE = HEAD ** -0.5
NEG = float(jnp.finfo(jnp.float32).min)
N_CHIPS = 4
LANE = 128

ADAM_LR, ADAM_B1, ADAM_B2, ADAM_EPS, ADAM_WD, ADAM_STEP = 0.001, 0.9, 0.999, 1e-08, 0.01, 10

VMEM_LIMIT_MB = 56


def _call(body, name, grid, in_specs, out_specs, out_shape, *, scratch=(), semantics=None, aliases=None,
          prefetch=0):
    params = pltpu.CompilerParams(dimension_semantics=semantics, vmem_limit_bytes=VMEM_LIMIT_MB << 20)
    if prefetch:
        spec = pltpu.PrefetchScalarGridSpec(num_scalar_prefetch=prefetch, grid=grid, in_specs=in_specs,
                                            out_specs=out_specs, scratch_shapes=list(scratch))
        return pl.pallas_call(body, name=name, grid_spec=spec, out_shape=out_shape,
                              input_output_aliases=aliases or {}, compiler_params=params)
    return pl.pallas_call(body, name=name, grid=grid, in_specs=in_specs, out_specs=out_specs, out_shape=out_shape,
                          scratch_shapes=list(scratch), input_output_aliases=aliases or {}, compiler_params=params)


def _tile(n, pref):
    return max(t for t in range(8, min(n, pref) + 1, 8) if n % t == 0)


def _dot(a, b):
    return jnp.dot(a, b, preferred_element_type=F32)


def _dot_nt(a, b):
    return lax.dot_general(a, b, (((1,), (1,)), ((), ())), preferred_element_type=F32)


def _dot_tn(a, b):
    return lax.dot_general(a, b, (((0,), (0,)), ((), ())), preferred_element_type=F32)


def _rms(x):
    r = lax.rsqrt(jnp.mean(x * x, axis=-1, keepdims=True) + EPS)
    return x * r, r


def _rms_bwd(dy, xh, r, g):
    dg = jnp.sum(dy * xh, axis=0, keepdims=True)
    dyg = dy * g
    dx = r * (dyg - xh * jnp.mean(dyg * xh, axis=-1, keepdims=True))
    return dx, dg


def _norm_mm(h, g, w, nj, tn, *, act, name, tm):
    w_arr, w_block, w_imap = w
    rows, d = h.shape

    def body(h_ref, g_ref, w_ref, y_ref, xn_ref):
        @pl.when(pl.program_id(1) == 0)
        def _():
            xh, _ = _rms(h_ref[...])
            xn_ref[...] = (xh * g_ref[...]).astype(BF)

        w_j = w_ref[pl.program_id(1)] if (nj > 1 and w_block[0] == nj) else w_ref[...].reshape(d, tn)
        u = _dot(xn_ref[...], w_j)
        if act:
            a = jnp.maximum(u, 0.0)
            y_ref[...] = (a * a).astype(BF)
        else:
            y_ref[...] = u

    return _call(
        body, name, (rows // tm, nj),
        [pl.BlockSpec((tm, d), lambda i, j: (i, 0)), pl.BlockSpec((1, d), lambda i, j: (0, 0)),
         pl.BlockSpec(w_block, lambda i, j: w_imap(j))],
        [pl.BlockSpec((tm, tn), lambda i, j: (i, j)), pl.BlockSpec((tm, d), lambda i, j: (i, 0))],
        [SDS((rows, nj * tn), BF if act else F32), SDS((rows, d), BF)],
        semantics=("parallel", "arbitrary"))(h, g, w_arr)


def _mm_res(res, a, w, *, name, tm, after=None):
    w_arr, w_block, w_imap = w
    rows, k = a.shape
    n = res.shape[1]

    def body(res_ref, a_ref, w_ref, *rest):
        rest[-1][...] = res_ref[...] + _dot(a_ref[...], w_ref[...].reshape(k, n))

    extra = [] if after is None else [after]
    return _call(
        body, name, (rows // tm,),
        [pl.BlockSpec((tm, n), lambda i: (i, 0)), pl.BlockSpec((tm, k), lambda i: (i, 0)),
         pl.BlockSpec(w_block, lambda i: w_imap(0))] + [ANY] * len(extra),
        pl.BlockSpec((tm, n), lambda i: (i, 0)), SDS((rows, n), F32), semantics=("parallel",))(res, a, w_arr, *extra)


def _mm_nt(dy, w, k, *, name, tm):
    w_arr, w_block, w_imap = w
    rows, n = dy.shape

    def body(dy_ref, w_ref, o_ref):
        o_ref[...] = _dot_nt(dy_ref[...], w_ref[...].reshape(k, n))

    return _call(
        body, name, (rows // tm,),
        [pl.BlockSpec((tm, n), lambda i: (i, 0)), pl.BlockSpec(w_block, lambda i: w_imap(0))],
        pl.BlockSpec((tm, k), lambda i: (i, 0)), SDS((rows, k), F32), semantics=("parallel",))(dy, w_arr)


def _mm_nt_relu2(dh, hh, w, nj, *, name, tm):
    w_arr, w_block, w_imap = w
    rows, d = dh.shape
    tk = hh.shape[1] // nj

    def body(dh_ref, hh_ref, w_ref, o_ref):
        w_j = w_ref[pl.program_id(1)] if w_block[0] == nj else w_ref[...].reshape(tk, d)
        dhh = _dot_nt(dh_ref[...], w_j)
        o_ref[...] = (dhh * (2.0 * jnp.sqrt(hh_ref[...].astype(F32)))).astype(BF)

    return _call(
        body, name, (rows // tm, nj),
        [pl.BlockSpec((tm, d), lambda i, j: (i, 0)), pl.BlockSpec((tm, tk), lambda i, j: (i, j)),
         pl.BlockSpec(w_block, lambda i, j: w_imap(j))],
        pl.BlockSpec((tm, tk), lambda i, j: (i, j)), SDS((rows, nj * tk), BF),
        semantics=("parallel", "parallel"))(dh, hh, w_arr)


def _mm_nt_normbwd(dy, w, nsplit, h, g, dres, *, name, tm):
    w_arr, w_block, w_imap = w
    rows, n = dy.shape
    d = h.shape[1]
    ns = n // nsplit

    def body(dy_ref, w_ref, h_ref, g_ref, dres_ref, o_ref, obf_ref, dg_ref):
        if nsplit == 1:
            dxn = _dot_nt(dy_ref[...].astype(BF), w_ref[...].reshape(d, n))
        else:
            dxn = _dot_nt(dy_ref[:, 0:ns].astype(BF), w_ref[0])
            for s in range(1, nsplit):
                dxn += _dot_nt(dy_ref[:, s * ns:(s + 1) * ns].astype(BF), w_ref[s])
        xh, r = _rms(h_ref[...])
        dx, dg = _rms_bwd(dxn, xh, r, g_ref[...])
        out = dres_ref[...] + dx
        o_ref[...] = out
        obf_ref[...] = out.astype(BF)

        @pl.when(pl.program_id(0) == 0)
        def _():
            dg_ref[...] = jnp.zeros_like(dg_ref)

        dg_ref[...] += dg

    row = lambda i: (i, 0)
    return _call(
        body, name, (rows // tm,),
        [pl.BlockSpec((tm, n), row), pl.BlockSpec(w_block, lambda i: w_imap(0)), pl.BlockSpec((tm, d), row),
         pl.BlockSpec((1, d), lambda i: (0, 0)), pl.BlockSpec((tm, d), row)],
        [pl.BlockSpec((tm, d), row), pl.BlockSpec((tm, d), row), pl.BlockSpec((1, d), lambda i: (0, 0))],
        [SDS((rows, d), F32), SDS((rows, d), BF), SDS((1, d), F32)],
        semantics=("arbitrary",))(dy, w_arr, h, g, dres)


def _mm_tn(x, dy, packed, out_imap, tk, tn, *, name):
    s_len, k = x.shape
    n = dy.shape[1]

    def body(x_ref, dy_ref, _, o_ref):
        o_ref[0] = _dot_tn(x_ref[...], dy_ref[...].astype(BF))

    return _call(
        body, name, (k // tk, n // tn),
        [pl.BlockSpec((s_len, tk), lambda i, j: (0, i)), pl.BlockSpec((s_len, tn), lambda i, j: (0, j)), ANY],
        pl.BlockSpec((1, tk, tn), out_imap), SDS(packed.shape, packed.dtype),
        semantics=("parallel", "parallel"), aliases={2: 0})(x, dy, packed)


def _loss_head(y, tgt, *, tm):
    rows, d = y.shape

    def body(y_ref, t_ref, dh_ref, dhbf_ref, loss_ref):
        err = y_ref[...] - t_ref[...]
        dh = err * (1.0 / d)
        dh_ref[...] = dh
        dhbf_ref[...] = dh.astype(BF)

        @pl.when(pl.program_id(0) == 0)
        def _():
            loss_ref[...] = jnp.zeros_like(loss_ref)

        loss_ref[...] += 0.5 * jnp.sum(jnp.mean(err * err, axis=-1, keepdims=True), axis=0, keepdims=True)

    row = lambda i: (i, 0)
    return _call(
        body, "loss_head", (rows // tm,), [pl.BlockSpec((tm, d), row), pl.BlockSpec((tm, d), row)],
        [pl.BlockSpec((tm, d), row), pl.BlockSpec((tm, d), row), pl.BlockSpec((1, 1), lambda i: (0, 0))],
        [SDS((rows, d), F32), SDS((rows, d), BF), SDS((1, 1), F32)], semantics=("arbitrary",))(y, tgt)


def _hs(h):
    return slice(HEAD * h, HEAD * (h + 1))


def _softmax_rows(s):
    e = jnp.exp(s - jnp.max(s, axis=-1, keepdims=True))
    return e * (1.0 / jnp.sum(e, axis=-1, keepdims=True))


def _scaled_bf16(qn):
    return (qn * QK_SCALE).astype(BF)


def _mem_fwd(mq, mk, mv, gq):
    outs = []
    for h in range(KV_HEADS):
        xh, _ = _rms(mq[:, _hs(h)])
        p = _softmax_rows(_dot_nt(_scaled_bf16(xh * gq), mk[:, _hs(h)]))
        outs.append(_dot(p.astype(BF), mv[:, _hs(h)]))
    return jnp.concatenate(outs, axis=-1)


def _mem_bwd(mq, do, mk, mv, gq):
    dqs, dks, dvs, dgq = [], [], [], 0.0
    for h in range(KV_HEADS):
        xh, r = _rms(mq[:, _hs(h)])
        qn = _scaled_bf16(xh * gq)
        p = _softmax_rows(_dot_nt(qn, mk[:, _hs(h)]))
        doh = do[:, _hs(h)].astype(BF)
        dp = _dot_nt(doh, mv[:, _hs(h)])
        ds = (p * (dp - jnp.sum(p * dp, axis=-1, keepdims=True))).astype(BF)
        dq, dg = _rms_bwd(_dot(ds, mk[:, _hs(h)]) * QK_SCALE, xh, r, gq)
        dqs.append(dq)
        dgq = dgq + dg
        dks.append(_dot_tn(ds, qn))
        dvs.append(_dot_tn(p.astype(BF), doh))
    cat = lambda xs: jnp.concatenate(xs, axis=-1)
    return cat(dqs), cat(dks), cat(dvs), dgq


def _mem_kv(mkv, gk):
    ks = []
    for h in range(KV_HEADS):
        xh, _ = _rms(mkv[:, _hs(h)])
        ks.append(xh * gk)
    return jnp.concatenate(ks, axis=-1).astype(BF), mkv[:, KVW:].astype(BF)


def _mem_kv_bwd(mkv, dmk, dmv, gk):
    dxs, dgk = [], 0.0
    for h in range(KV_HEADS):
        xh, r = _rms(mkv[:, _hs(h)])
        dx, dg = _rms_bwd(dmk[:, _hs(h)], xh, r, gk)
        dxs.append(dx)
        dgk = dgk + dg
    return jnp.concatenate(dxs + [dmv], axis=-1), dgk


def _pool_select(col, gd, a2, a4, a8, a16):
    return jnp.where(col < gd, a2, jnp.where(col < 2 * gd, a4, jnp.where(col < 3 * gd, a8, a16)))


def _pool_count(t0, shape, gd):
    col = lax.broadcasted_iota(jnp.int32, shape, 1)
    t = t0 + lax.broadcasted_iota(jnp.int32, shape, 0)
    win = _pool_select(col, gd, *POOL_WINDOWS)
    return jnp.minimum(t + 1, win).astype(F32)


def _pool_diff(u, halo, t0, gd):
    c = jnp.concatenate([halo, u], axis=0)
    s2 = c + pltpu.roll(c, 1, 0)
    s4 = s2 + pltpu.roll(s2, 2, 0)
    s8 = s4 + pltpu.roll(s4, 4, 0)
    s16 = s8 + pltpu.roll(s8, 8, 0)
    col = lax.broadcasted_iota(jnp.int32, c.shape, 1)
    ws = _pool_select(col, gd, s2, s4, s8, s16)[HALO:]
    return ws / _pool_count(t0, u.shape, gd) - u


def _pool_diff_bwd(dd, dd_halo, t0, gd):
    t = dd.shape[0]
    z = jnp.concatenate([dd / _pool_count(t0, dd.shape, gd), dd_halo / _pool_count(t0 + t, dd_halo.shape, gd)], axis=0)
    n = z.shape[0]
    f2 = z + pltpu.roll(z, n - 1, 0)
    f4 = f2 + pltpu.roll(f2, n - 2, 0)
    f8 = f4 + pltpu.roll(f4, n - 4, 0)
    f16 = f8 + pltpu.roll(f8, n - 8, 0)
    col = lax.broadcasted_iota(jnp.int32, z.shape, 1)
    return _pool_select(col, gd, f2, f4, f8, f16)[:t] - dd


def _swa_bias(n):
    qi = lax.broadcasted_iota(jnp.int32, (WINDOW, 2 * WINDOW), 0)
    kj = lax.broadcasted_iota(jnp.int32, (WINDOW, 2 * WINDOW), 1)
    dist = qi + WINDOW - kj
    valid = (dist >= 0) & (dist < WINDOW) & ((kj >= WINDOW) | (n > 0))
    return dist.astype(F32), valid


def _slopes(qh):
    return [2.0 ** (-8.0 * (h + 1) / qh) for h in range(qh)]


def _swa_probs(qn, kk, dist, valid, slope, sink):
    s = _dot_nt(qn, kk) - slope * dist
    s = jnp.where(valid, s, NEG)
    m = jnp.maximum(jnp.max(s, axis=-1, keepdims=True), sink)
    e = jnp.exp(s - m)
    es = jnp.exp(sink - m)
    z = jnp.sum(e, axis=-1, keepdims=True) + es
    inv = 1.0 / z
    return e * inv, es * inv


def _swa_group(q, kh, grp, n, qh, sinks):
    heads = range(kh * grp, (kh + 1) * grp)
    dist, valid = _swa_bias(n)
    slopes = _slopes(qh)
    rows = lambda vals: jnp.concatenate([jnp.broadcast_to(v, (WINDOW, 1)) for v in vals], axis=0)
    qs = jnp.concatenate([q[:, _hs(h)] for h in heads], axis=0)
    slope = rows([jnp.full((1, 1), slopes[h], F32) for h in heads])
    sink = rows([sinks[:, h:h + 1] for h in heads])
    return qs, slope, sink, jnp.concatenate([dist] * grp, axis=0), jnp.concatenate([valid] * grp, axis=0)


def _swa_fwd(q, kk, vv, gq, sinks, n, qh):
    grp = qh // KV_HEADS
    outs = []
    for kh in range(KV_HEADS):
        qs, slope, sink, dist, valid = _swa_group(q, kh, grp, n, qh, sinks)
        xh, _ = _rms(qs)
        p, _ = _swa_probs(_scaled_bf16(xh * gq), kk[:, _hs(kh)], dist, valid, slope, sink)
        o = _dot(p.astype(BF), vv[:, _hs(kh)])
        outs += [o[g * WINDOW:(g + 1) * WINDOW] for g in range(grp)]
    return jnp.concatenate(outs, axis=-1)


def _swa_bwd(q, do, kk, vv, gq, sinks, n, qh):
    grp = qh // KV_HEADS
    lane = lax.broadcasted_iota(jnp.int32, (1, LANE), 1)
    dqs, dks, dvs, dgq, dsk = [], [], [], 0.0, jnp.zeros((1, LANE), F32)
    for kh in range(KV_HEADS):
        qs, slope, sink, dist, valid = _swa_group(q, kh, grp, n, qh, sinks)
        xh, r = _rms(qs)
        qn = _scaled_bf16(xh * gq)
        p, ps = _swa_probs(qn, kk[:, _hs(kh)], dist, valid, slope, sink)
        dos = jnp.concatenate([do[:, _hs(h)] for h in range(kh * grp, (kh + 1) * grp)], axis=0).astype(BF)
        dp = _dot_nt(dos, vv[:, _hs(kh)])
        delta = jnp.sum(p * dp, axis=-1, keepdims=True)
        ds = (p * (dp - delta)).astype(BF)
        dsink = ps * delta
        for g in range(grp):
            part = -jnp.sum(dsink[g * WINDOW:(g + 1) * WINDOW], axis=0, keepdims=True)
            dsk = dsk + jnp.where(lane == kh * grp + g, part, 0.0)
        dq, dg = _rms_bwd(_dot(ds, kk[:, _hs(kh)]) * QK_SCALE, xh, r, gq)
        dqs += [dq[g * WINDOW:(g + 1) * WINDOW] for g in range(grp)]
        dgq = dgq + dg
        dks.append(_dot_tn(ds, qn))
        dvs.append(_dot_tn(p.astype(BF), dos))
    cat = lambda xs: jnp.concatenate(xs, axis=-1)
    return cat(dqs), cat(dks), cat(dvs), dgq, dsk


def _mixer_pool_fwd(proj, mkv, pbd, scale, gq, gk, *, name, tm):
    s_len, d = proj.shape
    main = d - KVW
    gd = main // len(POOL_WINDOWS)
    mlen = mkv.shape[0]
    hb = tm // HALO

    def body(u_ref, halo_ref, mq_ref, mkv_ref, pbd_ref, scale_ref, gq_ref, gk_ref, o_ref, mk_s, mv_s):
        i = pl.program_id(0)

        @pl.when(i == 0)
        def _():
            mk, mv = _mem_kv(mkv_ref[...], gk_ref[...])
            mk_s[...] = mk
            mv_s[...] = mv

        halo = jnp.where(i > 0, halo_ref[...], 0.0)
        dif = _pool_diff(u_ref[...], halo, i * tm, gd)
        mixed = _dot(dif.astype(BF), pbd_ref[...]) * scale_ref[...]
        mem = _mem_fwd(mq_ref[...], mk_s[...], mv_s[...], gq_ref[...])
        o_ref[...] = jnp.concatenate([mixed, mem], axis=-1).astype(BF)

    full = lambda shape: pl.BlockSpec(shape, lambda i: (0,) * len(shape))
    return _call(
        body, name, (s_len // tm,),
        [pl.BlockSpec((tm, main), lambda i: (i, 0)),
         pl.BlockSpec((HALO, main), lambda i: (jnp.maximum(i * hb - 1, 0), 0)),
         pl.BlockSpec((tm, KVW), lambda i: (i, main // KVW)),
         full((mlen, 2 * KVW)), full((main, main)), full((1, main)), full((1, HEAD)), full((1, HEAD))],
        pl.BlockSpec((tm, d), lambda i: (i, 0)), SDS((s_len, d), BF),
        scratch=[pltpu.VMEM((mlen, KVW), BF), pltpu.VMEM((mlen, KVW), BF)],
        semantics=("arbitrary",))(proj, proj, proj, mkv, pbd, scale, gq, gk)


def _mixer_pool_bwd(proj, dcat, mkv, pbd, scale, gq, gk, *, name, tm):
    s_len, d = proj.shape
    main = d - KVW
    gd = main // len(POOL_WINDOWS)
    mlen = mkv.shape[0]
    hb = tm // HALO
    nt = s_len // tm
    last_halo = s_len // HALO - 1

    def body(u_ref, halo_ref, mq_ref, do_ref, donext_ref, dom_ref, mkv_ref, pbd_ref, scale_ref, gq_ref, gk_ref,
             dproj_ref, dpbd_ref, dscale_ref, dmkv_ref, dgq_ref, dgk_ref, mk_s, mv_s, dmk_s, dmv_s):
        i = pl.program_id(0)

        @pl.when(i == 0)
        def _():
            mk, mv = _mem_kv(mkv_ref[...], gk_ref[...])
            mk_s[...] = mk
            mv_s[...] = mv
            dmk_s[...] = jnp.zeros_like(dmk_s)
            dmv_s[...] = jnp.zeros_like(dmv_s)
            dpbd_ref[...] = jnp.zeros_like(dpbd_ref)
            dscale_ref[...] = jnp.zeros_like(dscale_ref)
            dgq_ref[...] = jnp.zeros_like(dgq_ref)

        pbd = pbd_ref[...]
        scale = scale_ref[...]
        halo = jnp.where(i > 0, halo_ref[...], 0.0)
        dif = _pool_diff(u_ref[...], halo, i * tm, gd).astype(BF)
        do = do_ref[...]
        dscale_ref[...] += jnp.sum(do * _dot(dif, pbd), axis=0, keepdims=True)
        dmixed = (do * scale).astype(BF)
        dpbd_ref[...] += _dot_tn(dif, dmixed)
        dd = _dot_nt(dmixed, pbd)
        donext = jnp.where(i < nt - 1, donext_ref[...], 0.0)
        dd_halo = _dot_nt((donext * scale).astype(BF), pbd)
        du = _pool_diff_bwd(dd, dd_halo, i * tm, gd)

        dmq, dmk, dmv, dgq = _mem_bwd(mq_ref[...], dom_ref[...], mk_s[...], mv_s[...], gq_ref[...])
        dmk_s[...] += dmk
        dmv_s[...] += dmv
        dgq_ref[...] += dgq
        dproj_ref[...] = jnp.concatenate([du, dmq], axis=-1).astype(BF)

        @pl.when(i == nt - 1)
        def _():
            dmkv, dgk = _mem_kv_bwd(mkv_ref[...], dmk_s[...], dmv_s[...], gk_ref[...])
            dmkv_ref[...] = dmkv
            dgk_ref[...] = dgk

    full = lambda shape: pl.BlockSpec(shape, lambda i: (0,) * len(shape))
    return _call(
        body, name, (nt,),
        [pl.BlockSpec((tm, main), lambda i: (i, 0)),
         pl.BlockSpec((HALO, main), lambda i: (jnp.maximum(i * hb - 1, 0), 0)),
         pl.BlockSpec((tm, KVW), lambda i: (i, main // KVW)),
         pl.BlockSpec((tm, main), lambda i: (i, 0)),
         pl.BlockSpec((HALO, main), lambda i: (jnp.minimum((i + 1) * hb, last_halo), 0)),
         pl.BlockSpec((tm, KVW), lambda i: (i, main // KVW)),
         full((mlen, 2 * KVW)), full((main, main)), full((1, main)), full((1, HEAD)), full((1, HEAD))],
        [pl.BlockSpec((tm, d), lambda i: (i, 0)), full((main, main)), full((1, main)), full((mlen, 2 * KVW)),
         full((1, HEAD)), full((1, HEAD))],
        [SDS((s_len, d), BF), SDS((main, main), F32), SDS((1, main), F32), SDS((mlen, 2 * KVW), F32),
         SDS((1, HEAD), F32), SDS((1, HEAD), F32)],
        scratch=[pltpu.VMEM((mlen, KVW), BF), pltpu.VMEM((mlen, KVW), BF), pltpu.VMEM((mlen, KVW), F32),
                 pltpu.VMEM((mlen, KVW), F32)],
        semantics=("arbitrary",))(proj, proj, proj, dcat, dcat, dcat, mkv, pbd, scale, gq, gk)


def _mixer_swa_fwd(proj, kn, v, mkv, gqs, sinks, gq, gk, *, name):
    s_len, d = proj.shape
    main = d - KVW
    qh = main // HEAD
    mlen = mkv.shape[0]
    tm = WINDOW

    def body(q_ref, mq_ref, kp_ref, kc_ref, vp_ref, vc_ref, mkv_ref, gqs_ref, sinks_ref, gq_ref, gk_ref, o_ref,
             mk_s, mv_s):
        n = pl.program_id(0)

        @pl.when(n == 0)
        def _():
            mk, mv = _mem_kv(mkv_ref[...], gk_ref[...])
            mk_s[...] = mk
            mv_s[...] = mv

        kk = jnp.concatenate([kp_ref[...], kc_ref[...]], axis=0)
        vv = jnp.concatenate([vp_ref[...], vc_ref[...]], axis=0)
        att = _swa_fwd(q_ref[...], kk, vv, gqs_ref[...], sinks_ref[...], n, qh)
        mem = _mem_fwd(mq_ref[...], mk_s[...], mv_s[...], gq_ref[...])
        o_ref[...] = jnp.concatenate([att, mem], axis=-1).astype(BF)

    full = lambda shape: pl.BlockSpec(shape, lambda i: (0,) * len(shape))
    prev = lambda i: (jnp.maximum(i - 1, 0), 0)
    cur = lambda i: (i, 0)
    return _call(
        body, name, (s_len // tm,),
        [pl.BlockSpec((tm, main), cur), pl.BlockSpec((tm, KVW), lambda i: (i, main // KVW)),
         pl.BlockSpec((tm, KVW), prev), pl.BlockSpec((tm, KVW), cur),
         pl.BlockSpec((tm, KVW), prev), pl.BlockSpec((tm, KVW), cur),
         full((mlen, 2 * KVW)), full((1, HEAD)), full((1, LANE)), full((1, HEAD)), full((1, HEAD))],
        pl.BlockSpec((tm, d), cur), SDS((s_len, d), BF),
        scratch=[pltpu.VMEM((mlen, KVW), BF), pltpu.VMEM((mlen, KVW), BF)],
        semantics=("arbitrary",))(proj, proj, kn, kn, v, v, mkv, gqs, sinks, gq, gk)


def _mixer_swa_bwd(proj, dcat, kn, v, mkv, gqs, sinks, gq, gk, *, name):
    s_len, d = proj.shape
    main = d - KVW
    qh = main // HEAD
    mlen = mkv.shape[0]
    tm = WINDOW
    nt = s_len // tm

    def body(q_ref, mq_ref, do_ref, dom_ref, kp_ref, kc_ref, vp_ref, vc_ref, mkv_ref, gqs_ref, sinks_ref, gq_ref,
             gk_ref, dproj_ref, dk_ref, dv_ref, dmkv_ref, dgqs_ref, dsinks_ref, dgq_ref, dgk_ref,
             mk_s, mv_s, dmk_s, dmv_s):
        n = pl.program_id(0)

        @pl.when(n == 0)
        def _():
            mk, mv = _mem_kv(mkv_ref[...], gk_ref[...])
            mk_s[...] = mk
            mv_s[...] = mv
            dmk_s[...] = jnp.zeros_like(dmk_s)
            dmv_s[...] = jnp.zeros_like(dmv_s)
            dk_ref[...] = jnp.zeros_like(dk_ref)
            dv_ref[...] = jnp.zeros_like(dv_ref)
            dgqs_ref[...] = jnp.zeros_like(dgqs_ref)
            dsinks_ref[...] = jnp.zeros_like(dsinks_ref)
            dgq_ref[...] = jnp.zeros_like(dgq_ref)

        kk = jnp.concatenate([kp_ref[...], kc_ref[...]], axis=0)
        vv = jnp.concatenate([vp_ref[...], vc_ref[...]], axis=0)
        dq, dkk, dvv, dgqs, dsk = _swa_bwd(q_ref[...], do_ref[...], kk, vv, gqs_ref[...], sinks_ref[...], n, qh)
        prev = pl.ds(pl.multiple_of(jnp.maximum(n - 1, 0) * tm, tm), tm)
        own = pl.ds(pl.multiple_of(n * tm, tm), tm)
        dk_ref[prev, :] += dkk[:tm]
        dk_ref[own, :] += dkk[tm:]
        dv_ref[prev, :] += dvv[:tm]
        dv_ref[own, :] += dvv[tm:]
        dgqs_ref[...] += dgqs
        dsinks_ref[...] += dsk

        dmq, dmk, dmv, dgq = _mem_bwd(mq_ref[...], dom_ref[...], mk_s[...], mv_s[...], gq_ref[...])
        dmk_s[...] += dmk
        dmv_s[...] += dmv
        dgq_ref[...] += dgq
        dproj_ref[...] = jnp.concatenate([dq, dmq], axis=-1).astype(BF)

        @pl.when(n == nt - 1)
        def _():
            dmkv, dgk = _mem_kv_bwd(mkv_ref[...], dmk_s[...], dmv_s[...], gk_ref[...])
            dmkv_ref[...] = dmkv
            dgk_ref[...] = dgk

    full = lambda shape: pl.BlockSpec(shape, lambda i: (0,) * len(shape))
    prev_b = lambda i: (jnp.maximum(i - 1, 0), 0)
    cur = lambda i: (i, 0)
    memcol = lambda i: (i, main // KVW)
    return _call(
        body, name, (nt,),
        [pl.BlockSpec((tm, main), cur), pl.BlockSpec((tm, KVW), memcol),
         pl.BlockSpec((tm, main), cur), pl.BlockSpec((tm, KVW), memcol),
         pl.BlockSpec((tm, KVW), prev_b), pl.BlockSpec((tm, KVW), cur),
         pl.BlockSpec((tm, KVW), prev_b), pl.BlockSpec((tm, KVW), cur),
         full((mlen, 2 * KVW)), full((1, HEAD)), full((1, LANE)), full((1, HEAD)), full((1, HEAD))],
        [pl.BlockSpec((tm, d), cur), full((s_len, KVW)), full((s_len, KVW)), full((mlen, 2 * KVW)),
         full((1, HEAD)), full((1, LANE)), full((1, HEAD)), full((1, HEAD))],
        [SDS((s_len, d), BF), SDS((s_len, KVW), F32), SDS((s_len, KVW), F32), SDS((mlen, 2 * KVW), F32),
         SDS((1, HEAD), F32), SDS((1, LANE), F32), SDS((1, HEAD), F32), SDS((1, HEAD), F32)],
        scratch=[pltpu.VMEM((mlen, KVW), BF), pltpu.VMEM((mlen, KVW), BF), pltpu.VMEM((mlen, KVW), F32),
                 pltpu.VMEM((mlen, KVW), F32)],
        semantics=("arbitrary",))(proj, proj, dcat, dcat, kn, kn, v, v, mkv, gqs, sinks, gq, gk)


def _kv_prep(kv, gk, *, tm):
    s_len = kv.shape[0]

    def body(kv_ref, gk_ref, k_ref, v_ref):
        k, v = _mem_kv(kv_ref[...], gk_ref[...])
        k_ref[...] = k
        v_ref[...] = v

    row = lambda i: (i, 0)
    return _call(
        body, "kv_prep", (s_len // tm,),
        [pl.BlockSpec((tm, 2 * KVW), row), pl.BlockSpec((1, HEAD), lambda i: (0, 0))],
        [pl.BlockSpec((tm, KVW), row), pl.BlockSpec((tm, KVW), row)],
        [SDS((s_len, KVW), BF), SDS((s_len, KVW), BF)], semantics=("parallel",))(kv, gk)


def _kv_bwd(kv, dks, dvs, gk, *, tm):
    s_len = kv.shape[0]
    nl = len(dks)

    def body(*refs):
        kv_ref, gk_ref = refs[0], refs[1]
        dk_refs, dv_refs = refs[2:2 + nl], refs[2 + nl:2 + 2 * nl]
        dkv_ref, dgk_ref = refs[2 + 2 * nl], refs[3 + 2 * nl]
        dk, dv = dk_refs[0][...], dv_refs[0][...]
        for t in range(1, nl):
            dk = dk + dk_refs[t][...]
            dv = dv + dv_refs[t][...]
        dkv, dgk = _mem_kv_bwd(kv_ref[...], dk, dv, gk_ref[...])
        dkv_ref[...] = dkv.astype(BF)

        @pl.when(pl.program_id(0) == 0)
        def _():
            dgk_ref[...] = jnp.zeros_like(dgk_ref)

        dgk_ref[...] += dgk

    row = lambda i: (i, 0)
    one = pl.BlockSpec((1, HEAD), lambda i: (0, 0))
    return _call(
        body, "kv_bwd", (s_len // tm,),
        [pl.BlockSpec((tm, 2 * KVW), row), one] + [pl.BlockSpec((tm, KVW), row)] * (2 * nl),
        [pl.BlockSpec((tm, 2 * KVW), row), one],
        [SDS((s_len, 2 * KVW), BF), SDS((1, HEAD), F32)], semantics=("arbitrary",))(kv, gk, *dks, *dvs)


def _place():
    x, y, c = lax.axis_index("x"), lax.axis_index("y"), lax.axis_index("c")
    flips = [(1 - x, y), (x, 1 - y), (1 - x, 1 - y)]
    return x, y, c, flips


def _remote(src, dst, send_sem, recv_sem, to):
    return pltpu.make_async_remote_copy(src_ref=src, dst_ref=dst, send_sem=send_sem, recv_sem=recv_sem,
                                        device_id=to, device_id_type=MESH)


def _gather_copies(p_refs, wg_refs, send, recv):
    x, y, c, flips = _place()
    chip = 2 * x + y
    cps = []
    for j, (fx, fy) in enumerate(flips):
        for b in range(2):
            half = p_refs[b].shape[0] // 2
            mine = pl.ds(c * half, half)
            cps.append(_remote(p_refs[b].at[mine, :], wg_refs[b].at[chip, mine, :], send.at[2 * j + b],
                               recv.at[2 * j + b], (fx, fy, c)))
    return cps, cps


def _forward_copies(p_refs, wg_refs, send, recv):
    x, y, c, flips = _place()
    chip = 2 * x + y
    sib = (x, y, 1 - c)
    sends, arrivals = [], []
    for b in range(2):
        half = p_refs[b].shape[0] // 2
        own = _remote(p_refs[b], wg_refs[b].at[chip], send.at[b], recv.at[b], sib)
        sends.append(own)
        arrivals.append(own)
        for j, (fx, fy) in enumerate(flips):
            k = 2 + 3 * b + j
            landed = wg_refs[b].at[2 * fx + fy, pl.ds(c * half, half), :]
            other = wg_refs[b].at[2 * fx + fy, pl.ds((1 - c) * half, half), :]
            sends.append(_remote(landed, landed, send.at[k], recv.at[k], sib))
            arrivals.append(_remote(other, other, send.at[k], recv.at[k], sib))
    return sends, arrivals


def _swap_copies(g_refs, r_refs, send, recv):
    x, y, c, _ = _place()
    cps = []
    for b in range(2):
        half = g_refs[b].shape[1] // 2
        cps.append(_remote(g_refs[b].at[:, pl.ds((1 - c) * half, half), :], r_refs[b], send.at[b], recv.at[b],
                           (x, y, 1 - c)))
    return cps, cps


def _split_start(make_copies, n_sems, bufs, after, fresh, *, name):
    def body(a1, a2, b1, b2, after_ref, send, recv, *outs):
        for cp in make_copies((a1, a2), (b1, b2), send, recv)[0]:
            cp.start()
        outs[4][...] = jnp.zeros_like(outs[4])

    extra_shape = () if fresh is None else (pltpu.HBM(fresh, F32),)
    extra_spec = () if fresh is None else (HBM,)
    return pl.pallas_call(
        body, name=name,
        out_shape=(pltpu.SemaphoreType.DMA((n_sems,)), pltpu.SemaphoreType.DMA((n_sems,)))
        + tuple(pltpu.HBM(b.shape, b.dtype) for b in bufs) + (SDS((8, LANE), F32),) + extra_shape,
        in_specs=(HBM, HBM, HBM, HBM, ANY), out_specs=(SEM, SEM, HBM, HBM, HBM, HBM, VMEM_WHOLE) + extra_spec,
        input_output_aliases={0: 2, 1: 3, 2: 4, 3: 5},
        compiler_params=pltpu.CompilerParams(has_side_effects=SIDE_EFFECT))(*[_in_hbm(b) for b in bufs], after)


def _split_wait(make_copies, started, after, *, name):
    send, recv, bufs = started[0], started[1], started[2:6]

    def body(a1, a2, b1, b2, send_ref, recv_ref, after_ref, *outs):
        sends, arrivals = make_copies((a1, a2), (b1, b2), send_ref, recv_ref)
        for cp in arrivals:
            cp.wait_recv()
        for cp in sends:
            cp.wait_send()

    return pl.pallas_call(
        body, name=name, out_shape=tuple(pltpu.HBM(b.shape, b.dtype) for b in bufs),
        in_specs=(HBM, HBM, HBM, HBM, SEM, SEM, ANY), out_specs=(HBM, HBM, HBM, HBM),
        input_output_aliases={0: 0, 1: 1, 2: 2, 3: 3},
        compiler_params=pltpu.CompilerParams(has_side_effects=SIDE_EFFECT))(*bufs, send, recv, after)


def _gather_small(ps):
    def body(ps_ref, o_ref, send, recv):
        x, y, c, flips = _place()
        chip = 2 * x + y
        o_ref[chip] = ps_ref[...]
        cps = [_remote(ps_ref, o_ref.at[chip], send.at[j], recv.at[j], (fx, fy, c))
               for j, (fx, fy) in enumerate(flips)]
        for cp in cps:
            cp.start()
        for j, (fx, fy) in enumerate(flips):
            _remote(ps_ref, o_ref.at[2 * fx + fy], send.at[j], recv.at[j], (fx, fy, c)).wait_recv()
        for cp in cps:
            cp.wait_send()

    return pl.pallas_call(
        body, name="gather_small", in_specs=[VMEM_WHOLE], out_specs=VMEM_WHOLE,
        out_shape=SDS((N_CHIPS,) + ps.shape, ps.dtype),
        scratch_shapes=[pltpu.SemaphoreType.DMA((3,)), pltpu.SemaphoreType.DMA((3,))])(ps)


def _sum_sibling(g, r, place, *, tm, name):
    n_sh, half, w = r.shape
    nt = half // tm

    def body(place_ref, g_ref, r_ref, pbf_ref, own_ref):
        s = pl.program_id(1)
        p = g_ref[0] + r_ref[0]
        pbf_ref[0] = p.astype(BF)

        @pl.when(s == place_ref[1])
        def _():
            own_ref[...] = p

    return _call(
        body, name, (nt, n_sh),
        [pl.BlockSpec((1, tm, w), lambda i, s, pr: (s, pr[0] * nt + i, 0)),
         pl.BlockSpec((1, tm, w), lambda i, s, pr: (s, i, 0))],
        [pl.BlockSpec((1, tm, w), lambda i, s, pr: (s, i, 0)), pl.BlockSpec((tm, w), lambda i, s, pr: (i, 0))],
        [SDS((n_sh, half, w), BF), SDS((half, w), F32)],
        semantics=("arbitrary", "arbitrary"), prefetch=1)(place, g, r)


def _rs_copies(p_refs, land_refs, send, recv):
    _, _, c, flips = _place()
    cps = []
    for j, (fx, fy) in enumerate(flips):
        for b in range(2):
            cps.append(_remote(p_refs[b].at[2 * fx + fy], land_refs[b].at[j], send.at[2 * j + b], recv.at[2 * j + b],
                               (fx, fy, c)))
    return cps, cps


def _sum_chips(own, r, full, layer, place, *, tm, name):
    half, w = own.shape

    def body(place_ref, own_ref, r_ref, _, o_ref):
        o_ref[0, 0] = ((own_ref[...] + r_ref[0].astype(F32)) + r_ref[1].astype(F32)) + r_ref[2].astype(F32)

    return _call(
        body, name, (half // tm,),
        [pl.BlockSpec((tm, w), lambda i, pr: (i, 0)), pl.BlockSpec((3, tm, w), lambda i, pr: (0, i, 0)), ANY],
        pl.BlockSpec((1, 1, tm, w), lambda i, pr: (layer, pr[0], i, 0)), SDS(full.shape, F32),
        semantics=("parallel",), prefetch=1, aliases={3: 0})(place, own, r, full)


def _share_with_sibling(f1, f2):
    def body(_, __, o1_ref, o2_ref, send, recv):
        x, y, c, _ = _place()
        mine, other = pl.ds(c, 1), pl.ds(1 - c, 1)
        cps = [_remote(o1_ref.at[:, mine], o1_ref.at[:, mine], send.at[0], recv.at[0], (x, y, 1 - c)),
               _remote(o2_ref.at[:, mine], o2_ref.at[:, mine], send.at[1], recv.at[1], (x, y, 1 - c))]
        for cp in cps:
            cp.start()
        for cp in cps:
            cp.wait_send()
        _remote(o1_ref.at[:, other], o1_ref.at[:, other], send.at[0], recv.at[0], (x, y, 1 - c)).wait_recv()
        _remote(o2_ref.at[:, other], o2_ref.at[:, other], send.at[1], recv.at[1], (x, y, 1 - c)).wait_recv()

    return pl.pallas_call(
        body, name="share_with_sibling", in_specs=[ANY, ANY], out_specs=[ANY, ANY],
        out_shape=[SDS(f1.shape, f1.dtype), SDS(f2.shape, f2.dtype)], input_output_aliases={0: 0, 1: 1},
        scratch_shapes=[pltpu.SemaphoreType.DMA((2,)), pltpu.SemaphoreType.DMA((2,))])(f1, f2)


def _allreduce_small(sg):
    rows, w = sg.shape

    def body(sg_ref, o_ref, slots, send, recv):
        x, y, c, _ = _place()
        me = 4 * x + 2 * y + c
        slots[me] = sg_ref[...]
        peers = []
        for k in range(1, 8):
            a, b, e = (k >> 2) & 1, (k >> 1) & 1, k & 1
            px = x + a - 2 * a * x
            py = y + b - 2 * b * y
            pc = c + e - 2 * e * c
            cp = _remote(sg_ref, slots.at[me], send.at[k - 1], recv.at[k - 1], (px, py, pc))
            cp.start()
            peers.append((cp, 4 * px + 2 * py + pc, (px, py, pc)))
        for k, (cp, pidx, pid) in enumerate(peers):
            _remote(sg_ref, slots.at[pidx], send.at[k], recv.at[k], pid).wait_recv()
        acc = slots[0]
        for dev in range(1, 8):
            acc = acc + slots[dev]
        o_ref[...] = acc
        for cp, _, _ in peers:
            cp.wait_send()

    vm = pl.BlockSpec(memory_space=pltpu.VMEM)
    return pl.pallas_call(
        body, name="allreduce_small", in_specs=[vm], out_specs=vm, out_shape=SDS((rows, w), F32),
        scratch_shapes=[pltpu.VMEM((8, rows, w), F32), pltpu.SemaphoreType.DMA((7,)),
                        pltpu.SemaphoreType.DMA((7,))],
        compiler_params=pltpu.CompilerParams(vmem_limit_bytes=VMEM_LIMIT_MB << 20))(sg)


def _adamw(g_arr, layer0, g_off, per_layer, w, m, v, *, name, tm):
    rows, cols = w.shape
    assert g_off % tm == 0 and per_layer % tm == 0 and rows % per_layer == 0
    npl = per_layer // tm
    c1 = 1.0 - ADAM_B1 ** ADAM_STEP
    c2 = 1.0 - ADAM_B2 ** ADAM_STEP

    def body(g_ref, w_ref, m_ref, v_ref, go_ref, d_ref, mo_ref, vo_ref):
        g = g_ref[0]
        mn = ADAM_B1 * m_ref[...] + (1.0 - ADAM_B1) * g
        vn = ADAM_B2 * v_ref[...] + (1.0 - ADAM_B2) * (g * g)
        go_ref[...] = g
        mo_ref[...] = mn
        vo_ref[...] = vn
        d_ref[...] = -ADAM_LR * ((mn / c1) / (jnp.sqrt(vn / c2) + ADAM_EPS) + ADAM_WD * w_ref[...])

    blk = pl.BlockSpec((tm, cols), lambda i: (i, 0))
    return _call(
        body, name, (rows // tm,),
        [pl.BlockSpec((1, tm, cols), lambda i: (layer0 + i // npl, g_off // tm + i % npl, 0)), blk, blk, blk],
        [blk] * 4,
        [SDS((rows, cols), F32)] * 4, semantics=("parallel",))(g_arr, w, m, v)


def _pack_small(parts, width):
    flat = jnp.concatenate([p.reshape(-1).astype(F32) for p in parts])
    rows = -(-flat.shape[0] // (8 * width)) * 8
    return jnp.pad(flat, (0, rows * width - flat.shape[0])).reshape(rows, width)


def _unpack_small(packed, shapes):
    flat = packed.reshape(-1)
    out, off = [], 0
    for shp in shapes:
        size = 1
        for n in shp:
            size *= n
        out.append(flat[off:off + size].reshape(shp))
        off += size
    return out


def _block_diag(pw):
    g, c, _ = pw.shape
    eye = jnp.eye(g, dtype=pw.dtype)
    return (eye[:, None, :, None] * pw[:, :, None, :]).reshape(g * c, g * c)


def _diag_blocks(full, g):
    c = full.shape[0] // g
    return jnp.stack([full[i * c:(i + 1) * c, i * c:(i + 1) * c] for i in range(g)])


def kernel(x, mem, norm_mix, w_in, pool_w, pool_scale, kv_norm, w_kv, k_norm, q_norm, sinks, mem_norm, w_mem_kv, mem_q_norm, mem_k_norm, w_out, norm_mlp, w_up, w_down, loss_target, m_norm_mix, m_w_in, m_pool_w, m_pool_scale, m_kv_norm, m_w_kv, m_k_norm, m_q_norm, m_sinks, m_mem_norm, m_w_mem_kv, m_mem_q_norm, m_mem_k_norm, m_w_out, m_norm_mlp, m_w_up, m_w_down, v_norm_mix, v_w_in, v_pool_w, v_pool_scale, v_kv_norm, v_w_kv, v_k_norm, v_q_norm, v_sinks, v_mem_norm, v_w_mem_kv, v_mem_q_norm, v_mem_k_norm, v_w_out, v_norm_mlp, v_w_up, v_w_down):
    s_len, d = x.shape[1], x.shape[2]
    n_layers, n_pool = norm_mix.shape[0], pool_w.shape[0]
    n_swa = n_layers - n_pool
    main = d - KVW
    qh = main // HEAD
    ff = w_down.shape[1] * N_CHIPS
    dq = d // N_CHIPS
    assert w_up.shape[2] == d and ff == N_CHIPS * d and w_kv.shape[1] == 2 * KVW
    tm = min(512, s_len)
    tm_mem = mem.shape[1]

    cx, cy, cc = lax.axis_index("x"), lax.axis_index("y"), lax.axis_index("c")
    chip = 2 * cx + cy
    place = jnp.stack([cc, chip]).astype(jnp.int32)

    off_down, off_up, off_in, off_out = 0, d, 2 * d, 2 * d + dq
    rows1 = off_out + dq
    off_mkv, off_kv = 0, dq
    rows2 = 2 * dq

    ps = jnp.pad(pool_scale, ((0, 8 - n_pool), (0, 2 * LANE - pool_scale.shape[1])))
    psg = _gather_small(ps)
    pool_scale_full = jnp.concatenate([psg[k, :n_pool, :pool_scale.shape[1]] for k in range(N_CHIPS)], axis=1)

    def packed_weights(l):
        p1 = jnp.concatenate([w_down[l], w_up[l], w_in[l], w_out[l]]).astype(BF)
        p2 = jnp.concatenate([w_mem_kv[l], w_kv] if l == n_pool else [w_mem_kv[l]]).astype(BF)
        return p1, p2

    def gather_start(l, after):
        p1, p2 = packed_weights(l)
        bufs = (p1, p2, lax.empty((N_CHIPS,) + p1.shape, BF), lax.empty((N_CHIPS,) + p2.shape, BF))
        return _split_start(_gather_copies, 6, bufs, after, None, name=f"gather_start_{l}")

    def gather_land(l, started, after):
        bufs = _split_wait(_gather_copies, started, after, name=f"gather_wait_{l}")
        return _split_start(_forward_copies, 8, bufs, place, None, name=f"forward_start_{l}")

    def gather_finish(l, forwarding, after):
        bufs = _split_wait(_forward_copies, forwarding, after, name=f"forward_wait_{l}")
        return bufs[2], bufs[3]

    def w_rows(arr, off, nrows, width):
        assert off % nrows == 0
        return (arr, (N_CHIPS, nrows, width), lambda j: (0, off // nrows, 0))

    row = lambda a: a.reshape(1, -1)
    h = x.reshape(s_len, d)
    memx = mem.reshape(tm_mem, d)
    tgt = loss_target.reshape(s_len, d)
    pbd = [_block_diag(pool_w[l]).astype(BF) for l in range(n_pool)]
    sinks_pad = [jnp.pad(row(sinks[j]), ((0, 0), (0, LANE - qh))) for j in range(n_swa)]

    w_in_l, w_out_l, w_down_l, w_up_all_l, w_mkv_l = [], [], [], [], []
    w_kv_g = None
    forwarding = gather_land(0, gather_start(0, psg), psg)
    travelling = gather_start(1, forwarding[6]) if n_layers > 1 else None
    saved = []
    kv = hn_kv = kn = vsh = None
    for l in range(n_layers):
        wg1, wg2 = gather_finish(l, forwarding, h if l else forwarding[6])
        w_in_l.append(w_rows(wg1, off_in, dq, d))
        w_out_l.append(w_rows(wg1, off_out, dq, d))
        w_down_l.append(w_rows(wg1, off_down, d, d))
        w_up_all_l.append((wg1, (N_CHIPS, d, d), lambda j: (0, off_up // d, 0)))
        w_mkv_l.append(w_rows(wg2, off_mkv, dq, 2 * KVW))
        g_mix = row(norm_mix[l])
        if travelling is not None:
            g_mix = g_mix + travelling[6][0, 0]
        if l == n_pool:
            w_kv_g = w_rows(wg2, off_kv, dq, 2 * KVW)
            kv, hn_kv = _norm_mm(h, row(kv_norm), w_kv_g, 1, 2 * KVW, act=False, name="kv_proj", tm=tm)
            kn, vsh = _kv_prep(kv, row(k_norm), tm=tm)
        h0 = h
        proj, xn = _norm_mm(h0, g_mix, w_in_l[l], 1, d, act=False, name=f"in_proj_{l}", tm=tm)
        mkv, memn = _norm_mm(memx, row(mem_norm[l]), w_mkv_l[l], 1, 2 * KVW, act=False, name=f"mem_kv_{l}", tm=tm_mem)
        if l < n_pool:
            cat = _mixer_pool_fwd(proj, mkv, pbd[l], row(pool_scale_full[l]), row(mem_q_norm[l]), row(mem_k_norm[l]),
                                  name=f"mixer_fwd_{l}", tm=tm)
        else:
            j = l - n_pool
            cat = _mixer_swa_fwd(proj, kn, vsh, mkv, row(q_norm[j]), sinks_pad[j], row(mem_q_norm[l]),
                                 row(mem_k_norm[l]), name=f"mixer_fwd_{l}")
        h1 = _mm_res(h0, cat, w_out_l[l], name=f"out_proj_{l}", tm=tm)
        hh, xm = _norm_mm(h1, row(norm_mlp[l]), w_up_all_l[l], N_CHIPS, d, act=True, name=f"mlp_up_{l}", tm=tm)
        after = None
        if travelling is not None:
            forwarding = gather_land(l + 1, travelling, hh)
            travelling = gather_start(l + 2, forwarding[6]) if l + 2 < n_layers else None
            after = forwarding[6]
        h = _mm_res(h1, hh, w_down_l[l], name=f"mlp_down_{l}", tm=tm, after=after)
        saved.append((h0, proj, xn, mkv, memn, cat, h1, hh, xm))

    dh, dh_bf, loss_part = _loss_head(h, tgt, tm=tm)
    loss = lax.psum(loss_part[0, 0], ("x", "y", "c"))

    half1, half2 = rows1 // 2, rows2 // 2
    g1 = lax.empty((N_CHIPS, rows1, d), F32)
    pending = {}
    swapping = None
    tk = min(512, d)

    def reduce_begin(l, swapped, after):
        g1_l, g2_l, r1, r2 = _split_wait(_swap_copies, swapped, after, name=f"swap_wait_{l}")
        pb1, own1 = _sum_sibling(g1_l, r1, place, tm=_tile(half1, 256), name=f"sum_sibling_a_{l}")
        pb2, own2 = _sum_sibling(g2_l, r2, place, tm=_tile(half2, 256), name=f"sum_sibling_b_{l}")
        bufs = (pb1, pb2, lax.empty((3, half1, d), BF), lax.empty((3, half2, 2 * KVW), BF))
        return _split_start(_rs_copies, 6, bufs, place, None, name=f"reduce_start_{l}"), own1, own2
    zeros_mem = jnp.zeros((tm_mem, d), F32)

    def rows_map(off, nrows, tkk):
        per = nrows // tkk
        return lambda i, j: (i // per, off // tkk + i % per, 0)

    def cols_map(off, tkk):
        return lambda i, j: (j, off // tkk + i, 0)

    d_norm_mix, d_norm_mlp, d_mem_norm = [None] * n_layers, [None] * n_layers, [None] * n_layers
    d_mem_q, d_mem_k = [None] * n_layers, [None] * n_layers
    d_pool_w, d_pool_scale = [None] * n_pool, [None] * n_pool
    d_q_norm, d_sinks = [None] * n_swa, [None] * n_swa
    dks, dvs = [], []
    d_kv_norm = d_k_norm = None
    for l in reversed(range(n_layers)):
        h0, proj, xn, mkv, memn, cat, h1, hh, xm = saved[l]
        g2 = jnp.zeros((N_CHIPS, rows2, 2 * KVW), F32)
        g1 = _mm_tn(hh, dh_bf, g1, rows_map(off_down, d, tk), tk, d, name=f"dw_down_{l}")
        du = _mm_nt_relu2(dh_bf, hh, w_down_l[l], N_CHIPS, name=f"d_mlp_act_{l}", tm=tm)
        g1 = _mm_tn(xm, du, g1, cols_map(off_up, tk), tk, d, name=f"dw_up_{l}")
        g_mlp = row(norm_mlp[l])
        if swapping is not None:
            pending[swapping[0]] = reduce_begin(*swapping, after=g1)
            g_mlp = g_mlp + pending[swapping[0]][0][6][0, 0]
        dh1, dh1_bf, d_norm_mlp[l] = _mm_nt_normbwd(du, w_up_all_l[l], N_CHIPS, h1, g_mlp, dh,
                                                    name=f"d_mlp_in_{l}", tm=tm)
        tkq = min(tk, dq)
        g1 = _mm_tn(cat, dh1_bf, g1, rows_map(off_out, dq, tkq), tkq, d, name=f"dw_out_{l}")
        dcat = _mm_nt(dh1_bf, w_out_l[l], d, name=f"d_cat_{l}", tm=tm)
        if l < n_pool:
            dproj, dpbd, dscale, dmkv, d_mem_q[l], d_mem_k[l] = _mixer_pool_bwd(
                proj, dcat, mkv, pbd[l], row(pool_scale_full[l]), row(mem_q_norm[l]), row(mem_k_norm[l]),
                name=f"mixer_bwd_{l}", tm=tm)
            d_pool_w[l] = _diag_blocks(dpbd, len(POOL_WINDOWS))
            d_pool_scale[l] = dscale
        else:
            j = l - n_pool
            dproj, dk, dv, dmkv, d_q_norm[j], dsk, d_mem_q[l], d_mem_k[l] = _mixer_swa_bwd(
                proj, dcat, kn, vsh, mkv, row(q_norm[j]), sinks_pad[j], row(mem_q_norm[l]), row(mem_k_norm[l]),
                name=f"mixer_bwd_{l}")
            d_sinks[j] = dsk[0, :qh]
            dks.append(dk)
            dvs.append(dv)
        g1 = _mm_tn(xn, dproj, g1, rows_map(off_in, dq, tkq), tkq, d, name=f"dw_in_{l}")
        dh, dh_bf, d_norm_mix[l] = _mm_nt_normbwd(dproj, w_in_l[l], 1, h0, row(norm_mix[l]), dh1,
                                                  name=f"d_in_{l}", tm=tm)
        g2 = _mm_tn(memn, dmkv, g2, rows_map(off_mkv, dq, tkq), tkq, 2 * KVW, name=f"dw_mem_kv_{l}")
        _, _, d_mem_norm[l] = _mm_nt_normbwd(dmkv, w_mkv_l[l], 1, memx, row(mem_norm[l]), zeros_mem,
                                             name=f"d_mem_norm_{l}", tm=tm_mem)
        if l == n_pool:
            dkv, d_k_norm = _kv_bwd(kv, dks, dvs, row(k_norm), tm=tm)
            g2 = _mm_tn(hn_kv, dkv, g2, rows_map(off_kv, dq, tkq), tkq, 2 * KVW, name="dw_kv")
            dh, dh_bf, d_kv_norm = _mm_nt_normbwd(dkv, w_kv_g, 1, h0, row(kv_norm), dh, name="d_kv_in", tm=tm)
        bufs = (g1, g2, lax.empty((N_CHIPS, half1, d), F32), lax.empty((N_CHIPS, half2, 2 * KVW), F32))
        swapping = (l, _split_start(_swap_copies, 2, bufs, place, (N_CHIPS, rows1, d) if l > 0 else None,
                                    name=f"swap_start_{l}"))
        g1 = swapping[1][7] if l > 0 else None
    pending[0] = reduce_begin(*swapping, after=dh)
    grad_x = dh.reshape(x.shape)

    small_names = ["norm_mix", "pool_w", "pool_scale", "kv_norm", "k_norm", "q_norm", "sinks", "mem_norm",
                   "mem_q_norm", "mem_k_norm", "norm_mlp"]
    small_grads = {
        "norm_mix": jnp.concatenate(d_norm_mix), "pool_w": jnp.stack(d_pool_w),
        "pool_scale": jnp.concatenate(d_pool_scale), "kv_norm": d_kv_norm[0], "k_norm": d_k_norm[0],
        "q_norm": jnp.concatenate(d_q_norm), "sinks": jnp.stack(d_sinks), "mem_norm": jnp.concatenate(d_mem_norm),
        "mem_q_norm": jnp.concatenate(d_mem_q), "mem_k_norm": jnp.concatenate(d_mem_k),
        "norm_mlp": jnp.concatenate(d_norm_mlp)}
    width = d
    sg = _pack_small([small_grads[n] for n in small_names], width)
    sg = sg + pending[0][0][6][0, 0]
    sg = _allreduce_small(sg)
    reduced = dict(zip(small_names, _unpack_small(sg, [small_grads[n].shape for n in small_names])))
    psw = pool_scale.shape[1]
    reduced["pool_scale"] = lax.dynamic_slice_in_dim(reduced["pool_scale"], chip * psw, psw, axis=1)
    params = dict(norm_mix=(norm_mix, m_norm_mix, v_norm_mix), pool_w=(pool_w, m_pool_w, v_pool_w),
                  pool_scale=(pool_scale, m_pool_scale, v_pool_scale), kv_norm=(kv_norm, m_kv_norm, v_kv_norm),
                  k_norm=(k_norm, m_k_norm, v_k_norm), q_norm=(q_norm, m_q_norm, v_q_norm),
                  sinks=(sinks, m_sinks, v_sinks), mem_norm=(mem_norm, m_mem_norm, v_mem_norm),
                  mem_q_norm=(mem_q_norm, m_mem_q_norm, v_mem_q_norm),
                  mem_k_norm=(mem_k_norm, m_mem_k_norm, v_mem_k_norm), norm_mlp=(norm_mlp, m_norm_mlp, v_norm_mlp))
    shapes = [params[n][0].shape for n in small_names]
    packs = [_pack_small([reduced[n].reshape(params[n][0].shape) for n in small_names], width)]
    packs += [_pack_small([params[n][t] for n in small_names], width) for t in range(3)]
    res = _adamw(packs[0][None], 0, 0, packs[0].shape[0], packs[1], packs[2], packs[3], name="adamw_small", tm=8)
    small = {n: [] for n in small_names}
    for r in res:
        for n, a in zip(small_names, _unpack_small(r, shapes)):
            small[n].append(a)

    full1 = lax.empty((n_layers, 2, half1, d), F32)
    full2 = lax.empty((n_layers, 2, half2, 2 * KVW), F32)
    for l in reversed(range(n_layers)):
        exchange, own1, own2 = pending[l]
        _, _, x1, x2 = _split_wait(_rs_copies, exchange, res[0], name=f"reduce_wait_{l}")
        full1 = _sum_chips(own1, x1, full1, l, place, tm=_tile(half1, 256), name=f"sum_chips_a_{l}")
        full2 = _sum_chips(own2, x2, full2, l, place, tm=_tile(half2, 256), name=f"sum_chips_b_{l}")
    full1, full2 = _share_with_sibling(full1, full2)
    full1 = full1.reshape(n_layers, rows1, d)
    full2 = full2.reshape(n_layers, rows2, 2 * KVW)

    big = {}
    for name, arr, layer0, off, per, w_, m_, v_ in (
            ("w_down", full1, 0, off_down, d, w_down, m_w_down, v_w_down),
            ("w_up", full1, 0, off_up, d, w_up, m_w_up, v_w_up),
            ("w_in", full1, 0, off_in, dq, w_in, m_w_in, v_w_in),
            ("w_out", full1, 0, off_out, dq, w_out, m_w_out, v_w_out),
            ("w_mem_kv", full2, 0, off_mkv, dq, w_mem_kv, m_w_mem_kv, v_w_mem_kv),
            ("w_kv", full2, n_pool, off_kv, dq, w_kv, m_w_kv, v_w_kv)):
        cols = arr.shape[2]
        upd = _adamw(arr, layer0, off, per, w_.reshape(-1, cols), m_.reshape(-1, cols), v_.reshape(-1, cols),
                     name=f"adamw_{name}", tm=min(256, dq))
        big[name] = [r.reshape(w_.shape) for r in upd]

    order = ["norm_mix", "w_in", "pool_w", "pool_scale", "kv_norm", "w_kv", "k_norm", "q_norm", "sinks", "mem_norm",
             "w_mem_kv", "mem_q_norm", "mem_k_norm", "w_out", "norm_mlp", "w_up", "w_down"]
    out = {**big, **small}
    return (loss, grad_x, *[out[n][0] for n in order], *[out[n][1] for n in order],
            *[out[n][2] for n in order], *[out[n][3] for n in order])
```

```python
import functools

import jax
import jax.numpy as jnp
from jax import lax
from jax.experimental import pallas as pl
from jax.experimental.pallas import tpu as pltpu

F32, BF = jnp.float32, jnp.bfloat16
SDS = jax.ShapeDtypeStruct
MESH = pl.DeviceIdType.MESH
ANY = pl.BlockSpec(memory_space=pl.ANY)
HBM = pl.BlockSpec(memory_space=pltpu.HBM)
SEM = pl.BlockSpec(memory_space=pltpu.SEMAPHORE)
VMEM_WHOLE = pl.BlockSpec(memory_space=pltpu.VMEM)
SIDE_EFFECT = pltpu.SideEffectType.DATAFLOW_SIDE_EFFECTING


def _in_hbm(a):
    return pltpu.with_memory_space_constraint(a, pltpu.HBM)


EPS = 1e-6
HEAD = 64
KV_HEADS = 4
KVW = KV_HEADS * HEAD
WINDOW = 128
POOL_WINDOWS = (2, 4, 8, 16)
HALO = 16
QK_SCALE = HEAD ** -0.5
NEG = float(jnp.finfo(jnp.float32).min)
N_CHIPS = 4
LANE = 128

ADAM_LR, ADAM_B1, ADAM_B2, ADAM_EPS, ADAM_WD, ADAM_STEP = 0.001, 0.9, 0.999, 1e-08, 0.01, 10

VMEM_LIMIT_MB = 56


def _call(body, name, grid, in_specs, out_specs, out_shape, *, scratch=(), semantics=None, aliases=None,
          prefetch=0):
    params = pltpu.CompilerParams(dimension_semantics=semantics, vmem_limit_bytes=VMEM_LIMIT_MB << 20)
    if prefetch:
        spec = pltpu.PrefetchScalarGridSpec(num_scalar_prefetch=prefetch, grid=grid, in_specs=in_specs,
                                            out_specs=out_specs, scratch_shapes=list(scratch))
        return pl.pallas_call(body, name=name, grid_spec=spec, out_shape=out_shape,
                              input_output_aliases=aliases or {}, compiler_params=params)
    return pl.pallas_call(body, name=name, grid=grid, in_specs=in_specs, out_specs=out_specs, out_shape=out_shape,
                          scratch_shapes=list(scratch), input_output_aliases=aliases or {}, compiler_params=params)


def _tile(n, pref):
    return max(t for t in range(8, min(n, pref) + 1, 8) if n % t == 0)


def _dot(a, b):
    return jnp.dot(a, b, preferred_element_type=F32)


def _dot_nt(a, b):
    return lax.dot_general(a, b, (((1,), (1,)), ((), ())), preferred_element_type=F32)


def _dot_tn(a, b):
    return lax.dot_general(a, b, (((0,), (0,)), ((), ())), preferred_element_type=F32)


def _rms(x):
    r = lax.rsqrt(jnp.mean(x * x, axis=-1, keepdims=True) + EPS)
    return x * r, r


def _rms_bwd(dy, xh, r, g):
    dg = jnp.sum(dy * xh, axis=0, keepdims=True)
    dyg = dy * g
    dx = r * (dyg - xh * jnp.mean(dyg * xh, axis=-1, keepdims=True))
    return dx, dg


def _norm_mm(h, g, w, nj, tn, *, act, name, tm):
    w_arr, w_block, w_imap = w
    rows, d = h.shape

    def body(h_ref, g_ref, w_ref, y_ref, xn_ref):
        @pl.when(pl.program_id(1) == 0)
        def _():
            xh, _ = _rms(h_ref[...])
            xn_ref[...] = (xh * g_ref[...]).astype(BF)

        w_j = w_ref[pl.program_id(1)] if (nj > 1 and w_block[0] == nj) else w_ref[...].reshape(d, tn)
        u = _dot(xn_ref[...], w_j)
        if act:
            a = jnp.maximum(u, 0.0)
            y_ref[...] = (a * a).astype(BF)
        else:
            y_ref[...] = u

    return _call(
        body, name, (rows // tm, nj),
        [pl.BlockSpec((tm, d), lambda i, j: (i, 0)), pl.BlockSpec((1, d), lambda i, j: (0, 0)),
         pl.BlockSpec(w_block, lambda i, j: w_imap(j))],
        [pl.BlockSpec((tm, tn), lambda i, j: (i, j)), pl.BlockSpec((tm, d), lambda i, j: (i, 0))],
        [SDS((rows, nj * tn), BF if act else F32), SDS((rows, d), BF)],
        semantics=("parallel", "arbitrary"))(h, g, w_arr)


def _mm_res(res, a, w, *, name, tm, after=None):
    w_arr, w_block, w_imap = w
    rows, k = a.shape
    n = res.shape[1]

    def body(res_ref, a_ref, w_ref, *rest):
        rest[-1][...] = res_ref[...] + _dot(a_ref[...], w_ref[...].reshape(k, n))

    extra = [] if after is None else [after]
    return _call(
        body, name, (rows // tm,),
        [pl.BlockSpec((tm, n), lambda i: (i, 0)), pl.BlockSpec((tm, k), lambda i: (i, 0)),
         pl.BlockSpec(w_block, lambda i: w_imap(0))] + [ANY] * len(extra),
        pl.BlockSpec((tm, n), lambda i: (i, 0)), SDS((rows, n), F32), semantics=("parallel",))(res, a, w_arr, *extra)


def _mm_nt(dy, w, k, *, name, tm):
    w_arr, w_block, w_imap = w
    rows, n = dy.shape

    def body(dy_ref, w_ref, o_ref):
        o_ref[...] = _dot_nt(dy_ref[...], w_ref[...].reshape(k, n))

    return _call(
        body, name, (rows // tm,),
        [pl.BlockSpec((tm, n), lambda i: (i, 0)), pl.BlockSpec(w_block, lambda i: w_imap(0))],
        pl.BlockSpec((tm, k), lambda i: (i, 0)), SDS((rows, k), F32), semantics=("parallel",))(dy, w_arr)


def _mm_nt_relu2(dh, hh, w, nj, *, name, tm):
    w_arr, w_block, w_imap = w
    rows, d = dh.shape
    tk = hh.shape[1] // nj

    def body(dh_ref, hh_ref, w_ref, o_ref):
        w_j = w_ref[pl.program_id(1)] if w_block[0] == nj else w_ref[...].reshape(tk, d)
        dhh = _dot_nt(dh_ref[...], w_j)
        o_ref[...] = (dhh * (2.0 * jnp.sqrt(hh_ref[...].astype(F32)))).astype(BF)

    return _call(
        body, name, (rows // tm, nj),
        [pl.BlockSpec((tm, d), lambda i, j: (i, 0)), pl.BlockSpec((tm, tk), lambda i, j: (i, j)),
         pl.BlockSpec(w_block, lambda i, j: w_imap(j))],
        pl.BlockSpec((tm, tk), lambda i, j: (i, j)), SDS((rows, nj * tk), BF),
        semantics=("parallel", "parallel"))(dh, hh, w_arr)


def _mm_nt_normbwd(dy, w, nsplit, h, g, dres, *, name, tm):
    w_arr, w_block, w_imap = w
    rows, n = dy.shape
    d = h.shape[1]
    ns = n // nsplit

    def body(dy_ref, w_ref, h_ref, g_ref, dres_ref, o_ref, obf_ref, dg_ref):
        if nsplit == 1:
            dxn = _dot_nt(dy_ref[...].astype(BF), w_ref[...].reshape(d, n))
        else:
            dxn = _dot_nt(dy_ref[:, 0:ns].astype(BF), w_ref[0])
            for s in range(1, nsplit):
                dxn += _dot_nt(dy_ref[:, s * ns:(s + 1) * ns].astype(BF), w_ref[s])
        xh, r = _rms(h_ref[...])
        dx, dg = _rms_bwd(dxn, xh, r, g_ref[...])
        out = dres_ref[...] + dx
        o_ref[...] = out
        obf_ref[...] = out.astype(BF)

        @pl.when(pl.program_id(0) == 0)
        def _():
            dg_ref[...] = jnp.zeros_like(dg_ref)

        dg_ref[...] += dg

    row = lambda i: (i, 0)
    return _call(
        body, name, (rows // tm,),
        [pl.BlockSpec((tm, n), row), pl.BlockSpec(w_block, lambda i: w_imap(0)), pl.BlockSpec((tm, d), row),
         pl.BlockSpec((1, d), lambda i: (0, 0)), pl.BlockSpec((tm, d), row)],
        [pl.BlockSpec((tm, d), row), pl.BlockSpec((tm, d), row), pl.BlockSpec((1, d), lambda i: (0, 0))],
        [SDS((rows, d), F32), SDS((rows, d), BF), SDS((1, d), F32)],
        semantics=("arbitrary",))(dy, w_arr, h, g, dres)


def _mm_tn(x, dy, packed, out_imap, tk, tn, *, name):
    s_len, k = x.shape
    n = dy.shape[1]

    def body(x_ref, dy_ref, _, o_ref):
        o_ref[0] = _dot_tn(x_ref[...], dy_ref[...].astype(BF))

    return _call(
        body, name, (k // tk, n // tn),
        [pl.BlockSpec((s_len, tk), lambda i, j: (0, i)), pl.BlockSpec((s_len, tn), lambda i, j: (0, j)), ANY],
        pl.BlockSpec((1, tk, tn), out_imap), SDS(packed.shape, packed.dtype),
        semantics=("parallel", "parallel"), aliases={2: 0})(x, dy, packed)


def _loss_head(y, tgt, *, tm):
    rows, d = y.shape

    def body(y_ref, t_ref, dh_ref, dhbf_ref, loss_ref):
        err = y_ref[...] - t_ref[...]
        dh = err * (1.0 / d)
        dh_ref[...] = dh
        dhbf_ref[...] = dh.astype(BF)

        @pl.when(pl.program_id(0) == 0)
        def _():
            loss_ref[...] = jnp.zeros_like(loss_ref)

        loss_ref[...] += 0.5 * jnp.sum(jnp.mean(err * err, axis=-1, keepdims=True), axis=0, keepdims=True)

    row = lambda i: (i, 0)
    return _call(
        body, "loss_head", (rows // tm,), [pl.BlockSpec((tm, d), row), pl.BlockSpec((tm, d), row)],
        [pl.BlockSpec((tm, d), row), pl.BlockSpec((tm, d), row), pl.BlockSpec((1, 1), lambda i: (0, 0))],
        [SDS((rows, d), F32), SDS((rows, d), BF), SDS((1, 1), F32)], semantics=("arbitrary",))(y, tgt)


def _hs(h):
    return slice(HEAD * h, HEAD * (h + 1))


def _softmax_rows(s):
    e = jnp.exp(s - jnp.max(s, axis=-1, keepdims=True))
    return e * (1.0 / jnp.sum(e, axis=-1, keepdims=True))


def _scaled_bf16(qn):
    return (qn * QK_SCALE).astype(BF)


def _mem_fwd(mq, mk, mv, gq):
    outs = []
    for h in range(KV_HEADS):
        xh, _ = _rms(mq[:, _hs(h)])
        p = _softmax_rows(_dot_nt(_scaled_bf16(xh * gq), mk[:, _hs(h)]))
        outs.append(_dot(p.astype(BF), mv[:, _hs(h)]))
    return jnp.concatenate(outs, axis=-1)


def _mem_bwd(mq, do, mk, mv, gq):
    dqs, dks, dvs, dgq = [], [], [], 0.0
    for h in range(KV_HEADS):
        xh, r = _rms(mq[:, _hs(h)])
        qn = _scaled_bf16(xh * gq)
        p = _softmax_rows(_dot_nt(qn, mk[:, _hs(h)]))
        doh = do[:, _hs(h)].astype(BF)
        dp = _dot_nt(doh, mv[:, _hs(h)])
        ds = (p * (dp - jnp.sum(p * dp, axis=-1, keepdims=True))).astype(BF)
        dq, dg = _rms_bwd(_dot(ds, mk[:, _hs(h)]) * QK_SCALE, xh, r, gq)
        dqs.append(dq)
        dgq = dgq + dg
        dks.append(_dot_tn(ds, qn))
        dvs.append(_dot_tn(p.astype(BF), doh))
    cat = lambda xs: jnp.concatenate(xs, axis=-1)
    return cat(dqs), cat(dks), cat(dvs), dgq


def _mem_kv(mkv, gk):
    ks = []
    for h in range(KV_HEADS):
        xh, _ = _rms(mkv[:, _hs(h)])
        ks.append(xh * gk)
    return jnp.concatenate(ks, axis=-1).astype(BF), mkv[:, KVW:].astype(BF)


def _mem_kv_bwd(mkv, dmk, dmv, gk):
    dxs, dgk = [], 0.0
    for h in range(KV_HEADS):
        xh, r = _rms(mkv[:, _hs(h)])
        dx, dg = _rms_bwd(dmk[:, _hs(h)], xh, r, gk)
        dxs.append(dx)
        dgk = dgk + dg
    return jnp.concatenate(dxs + [dmv], axis=-1), dgk


def _pool_select(col, gd, a2, a4, a8, a16):
    return jnp.where(col < gd, a2, jnp.where(col < 2 * gd, a4, jnp.where(col < 3 * gd, a8, a16)))


def _pool_count(t0, shape, gd):
    col = lax.broadcasted_iota(jnp.int32, shape, 1)
    t = t0 + lax.broadcasted_iota(jnp.int32, shape, 0)
    win = _pool_select(col, gd, *POOL_WINDOWS)
    return jnp.minimum(t + 1, win).astype(F32)


def _pool_diff(u, halo, t0, gd):
    c = jnp.concatenate([halo, u], axis=0)
    s2 = c + pltpu.roll(c, 1, 0)
    s4 = s2 + pltpu.roll(s2, 2, 0)
    s8 = s4 + pltpu.roll(s4, 4, 0)
    s16 = s8 + pltpu.roll(s8, 8, 0)
    col = lax.broadcasted_iota(jnp.int32, c.shape, 1)
    ws = _pool_select(col, gd, s2, s4, s8, s16)[HALO:]
    return ws / _pool_count(t0, u.shape, gd) - u


def _pool_diff_bwd(dd, dd_halo, t0, gd):
    t = dd.shape[0]
    z = jnp.concatenate([dd / _pool_count(t0, dd.shape, gd), dd_halo / _pool_count(t0 + t, dd_halo.shape, gd)], axis=0)
    n = z.shape[0]
    f2 = z + pltpu.roll(z, n - 1, 0)
    f4 = f2 + pltpu.roll(f2, n - 2, 0)
    f8 = f4 + pltpu.roll(f4, n - 4, 0)
    f16 = f8 + pltpu.roll(f8, n - 8, 0)
    col = lax.broadcasted_iota(jnp.int32, z.shape, 1)
    return _pool_select(col, gd, f2, f4, f8, f16)[:t] - dd


def _swa_bias(n):
    qi = lax.broadcasted_iota(jnp.int32, (WINDOW, 2 * WINDOW), 0)
    kj = lax.broadcasted_iota(jnp.int32, (WINDOW, 2 * WINDOW), 1)
    dist = qi + WINDOW - kj
    valid = (dist >= 0) & (dist < WINDOW) & ((kj >= WINDOW) | (n > 0))
    return dist.astype(F32), valid


def _slopes(qh):
    return [2.0 ** (-8.0 * (h + 1) / qh) for h in range(qh)]


def _swa_probs(qn, kk, dist, valid, slope, sink):
    s = _dot_nt(qn, kk) - slope * dist
    s = jnp.where(valid, s, NEG)
    m = jnp.maximum(jnp.max(s, axis=-1, keepdims=True), sink)
    e = jnp.exp(s - m)
    es = jnp.exp(sink - m)
    z = jnp.sum(e, axis=-1, keepdims=True) + es
    inv = 1.0 / z
    return e * inv, es * inv


def _swa_group(q, kh, grp, n, qh, sinks):
    heads = range(kh * grp, (kh + 1) * grp)
    dist, valid = _swa_bias(n)
    slopes = _slopes(qh)
    rows = lambda vals: jnp.concatenate([jnp.broadcast_to(v, (WINDOW, 1)) for v in vals], axis=0)
    qs = jnp.concatenate([q[:, _hs(h)] for h in heads], axis=0)
    slope = rows([jnp.full((1, 1), slopes[h], F32) for h in heads])
    sink = rows([sinks[:, h:h + 1] for h in heads])
    return qs, slope, sink, jnp.concatenate([dist] * grp, axis=0), jnp.concatenate([valid] * grp, axis=0)


def _swa_fwd(q, kk, vv, gq, sinks, n, qh):
    grp = qh // KV_HEADS
    lane = lax.broadcasted_iota(jnp.int32, (WINDOW, LANE), 1)
    outs, probs, sink_probs = [], [], jnp.zeros((WINDOW, LANE), F32)
    for kh in range(KV_HEADS):
        qs, slope, sink, dist, valid = _swa_group(q, kh, grp, n, qh, sinks)
        xh, _ = _rms(qs)
        p, ps = _swa_probs(_scaled_bf16(xh * gq), kk[:, _hs(kh)], dist, valid, slope, sink)
        p = p.astype(BF)
        probs.append(p)
        o = _dot(p, vv[:, _hs(kh)])
        for g in range(grp):
            outs.append(o[g * WINDOW:(g + 1) * WINDOW])
            sink_probs = jnp.where(lane == kh * grp + g, ps[g * WINDOW:(g + 1) * WINDOW], sink_probs)
    return jnp.concatenate(outs, axis=-1), probs, sink_probs


def _swa_bwd(q, do, kk, vv, gq, probs, sink_probs, qh):
    grp = qh // KV_HEADS
    lane = lax.broadcasted_iota(jnp.int32, (1, LANE), 1)
    dqs, dks, dvs, dgq, dsk = [], [], [], 0.0, jnp.zeros((1, LANE), F32)
    for kh in range(KV_HEADS):
        heads = range(kh * grp, (kh + 1) * grp)
        xh, r = _rms(jnp.concatenate([q[:, _hs(h)] for h in heads], axis=0))
        qn = _scaled_bf16(xh * gq)
        p_bf = probs[kh]
        p = p_bf.astype(F32)
        ps = jnp.concatenate([sink_probs[:, h:h + 1] for h in heads], axis=0)
        dos = jnp.concatenate([do[:, _hs(h)] for h in heads], axis=0).astype(BF)
        dp = _dot_nt(dos, vv[:, _hs(kh)])
        delta = jnp.sum(p * dp, axis=-1, keepdims=True)
        ds = (p * (dp - delta)).astype(BF)
        dsink = ps * delta
        for g in range(grp):
            part = -jnp.sum(dsink[g * WINDOW:(g + 1) * WINDOW], axis=0, keepdims=True)
            dsk = dsk + jnp.where(lane == kh * grp + g, part, 0.0)
        dq, dg = _rms_bwd(_dot(ds, kk[:, _hs(kh)]) * QK_SCALE, xh, r, gq)
        dqs += [dq[g * WINDOW:(g + 1) * WINDOW] for g in range(grp)]
        dgq = dgq + dg
        dks.append(_dot_tn(ds, qn))
        dvs.append(_dot_tn(p_bf, dos))
    cat = lambda xs: jnp.concatenate(xs, axis=-1)
    return cat(dqs), cat(dks), cat(dvs), dgq, dsk


def _mixer_pool_fwd(proj, mkv, pbd, scale, gq, gk, *, name, tm):
    s_len, d = proj.shape
    main = d - KVW
    gd = main // len(POOL_WINDOWS)
    mlen = mkv.shape[0]
    hb = tm // HALO

    def body(u_ref, halo_ref, mq_ref, mkv_ref, pbd_ref, scale_ref, gq_ref, gk_ref, o_ref, mk_s, mv_s):
        i = pl.program_id(0)

        @pl.when(i == 0)
        def _():
            mk, mv = _mem_kv(mkv_ref[...], gk_ref[...])
            mk_s[...] = mk
            mv_s[...] = mv

        halo = jnp.where(i > 0, halo_ref[...], 0.0)
        dif = _pool_diff(u_ref[...], halo, i * tm, gd)
        mixed = _dot(dif.astype(BF), pbd_ref[...]) * scale_ref[...]
        mem = _mem_fwd(mq_ref[...], mk_s[...], mv_s[...], gq_ref[...])
        o_ref[...] = jnp.concatenate([mixed, mem], axis=-1).astype(BF)

    full = lambda shape: pl.BlockSpec(shape, lambda i: (0,) * len(shape))
    return _call(
        body, name, (s_len // tm,),
        [pl.BlockSpec((tm, main), lambda i: (i, 0)),
         pl.BlockSpec((HALO, main), lambda i: (jnp.maximum(i * hb - 1, 0), 0)),
         pl.BlockSpec((tm, KVW), lambda i: (i, main // KVW)),
         full((mlen, 2 * KVW)), full((main, main)), full((1, main)), full((1, HEAD)), full((1, HEAD))],
        pl.BlockSpec((tm, d), lambda i: (i, 0)), SDS((s_len, d), BF),
        scratch=[pltpu.VMEM((mlen, KVW), BF), pltpu.VMEM((mlen, KVW), BF)],
        semantics=("arbitrary",))(proj, proj, proj, mkv, pbd, scale, gq, gk)


def _mixer_pool_bwd(proj, dcat, mkv, pbd, scale, gq, gk, *, name, tm):
    s_len, d = proj.shape
    main = d - KVW
    gd = main // len(POOL_WINDOWS)
    mlen = mkv.shape[0]
    hb = tm // HALO
    nt = s_len // tm
    last_halo = s_len // HALO - 1

    def body(u_ref, halo_ref, mq_ref, do_ref, donext_ref, dom_ref, mkv_ref, pbd_ref, scale_ref, gq_ref, gk_ref,
             dproj_ref, dpbd_ref, dscale_ref, dmkv_ref, dgq_ref, dgk_ref, mk_s, mv_s, dmk_s, dmv_s):
        i = pl.program_id(0)

        @pl.when(i == 0)
        def _():
            mk, mv = _mem_kv(mkv_ref[...], gk_ref[...])
            mk_s[...] = mk
            mv_s[...] = mv
            dmk_s[...] = jnp.zeros_like(dmk_s)
            dmv_s[...] = jnp.zeros_like(dmv_s)
            dpbd_ref[...] = jnp.zeros_like(dpbd_ref)
            dscale_ref[...] = jnp.zeros_like(dscale_ref)
            dgq_ref[...] = jnp.zeros_like(dgq_ref)

        pbd = pbd_ref[...]
        scale = scale_ref[...]
        halo = jnp.where(i > 0, halo_ref[...], 0.0)
        dif = _pool_diff(u_ref[...], halo, i * tm, gd).astype(BF)
        do = do_ref[...]
        dscale_ref[...] += jnp.sum(do * _dot(dif, pbd), axis=0, keepdims=True)
        dmixed = (do * scale).astype(BF)
        dpbd_ref[...] += _dot_tn(dif, dmixed)
        dd = _dot_nt(dmixed, pbd)
        donext = jnp.where(i < nt - 1, donext_ref[...], 0.0)
        dd_halo = _dot_nt((donext * scale).astype(BF), pbd)
        du = _pool_diff_bwd(dd, dd_halo, i * tm, gd)

        dmq, dmk, dmv, dgq = _mem_bwd(mq_ref[...], dom_ref[...], mk_s[...], mv_s[...], gq_ref[...])
        dmk_s[...] += dmk
        dmv_s[...] += dmv
        dgq_ref[...] += dgq
        dproj_ref[...] = jnp.concatenate([du, dmq], axis=-1).astype(BF)

        @pl.when(i == nt - 1)
        def _():
            dmkv, dgk = _mem_kv_bwd(mkv_ref[...], dmk_s[...], dmv_s[...], gk_ref[...])
            dmkv_ref[...] = dmkv
            dgk_ref[...] = dgk

    full = lambda shape: pl.BlockSpec(shape, lambda i: (0,) * len(shape))
    return _call(
        body, name, (nt,),
        [pl.BlockSpec((tm, main), lambda i: (i, 0)),
         pl.BlockSpec((HALO, main), lambda i: (jnp.maximum(i * hb - 1, 0), 0)),
         pl.BlockSpec((tm, KVW), lambda i: (i, main // KVW)),
         pl.BlockSpec((tm, main), lambda i: (i, 0)),
         pl.BlockSpec((HALO, main), lambda i: (jnp.minimum((i + 1) * hb, last_halo), 0)),
         pl.BlockSpec((tm, KVW), lambda i: (i, main // KVW)),
         full((mlen, 2 * KVW)), full((main, main)), full((1, main)), full((1, HEAD)), full((1, HEAD))],
        [pl.BlockSpec((tm, d), lambda i: (i, 0)), full((main, main)), full((1, main)), full((mlen, 2 * KVW)),
         full((1, HEAD)), full((1, HEAD))],
        [SDS((s_len, d), BF), SDS((main, main), F32), SDS((1, main), F32), SDS((mlen, 2 * KVW), F32),
         SDS((1, HEAD), F32), SDS((1, HEAD), F32)],
        scratch=[pltpu.VMEM((mlen, KVW), BF), pltpu.VMEM((mlen, KVW), BF), pltpu.VMEM((mlen, KVW), F32),
                 pltpu.VMEM((mlen, KVW), F32)],
        semantics=("arbitrary",))(proj, proj, proj, dcat, dcat, dcat, mkv, pbd, scale, gq, gk)


def _mixer_swa_fwd(proj, kn, v, mkv, gqs, sinks, gq, gk, *, name):
    s_len, d = proj.shape
    main = d - KVW
    qh = main // HEAD
    mlen = mkv.shape[0]
    tm = WINDOW
    prow = qh // KV_HEADS * tm

    def body(q_ref, mq_ref, kp_ref, kc_ref, vp_ref, vc_ref, mkv_ref, gqs_ref, sinks_ref, gq_ref, gk_ref, o_ref,
             p_ref, ps_ref, mk_s, mv_s):
        n = pl.program_id(0)

        @pl.when(n == 0)
        def _():
            mk, mv = _mem_kv(mkv_ref[...], gk_ref[...])
            mk_s[...] = mk
            mv_s[...] = mv

        kk = jnp.concatenate([kp_ref[...], kc_ref[...]], axis=0)
        vv = jnp.concatenate([vp_ref[...], vc_ref[...]], axis=0)
        att, probs, sink_probs = _swa_fwd(q_ref[...], kk, vv, gqs_ref[...], sinks_ref[...], n, qh)
        for kh in range(KV_HEADS):
            p_ref[0, kh] = probs[kh]
        ps_ref[...] = sink_probs
        mem = _mem_fwd(mq_ref[...], mk_s[...], mv_s[...], gq_ref[...])
        o_ref[...] = jnp.concatenate([att, mem], axis=-1).astype(BF)

    full = lambda shape: pl.BlockSpec(shape, lambda i: (0,) * len(shape))
    prev = lambda i: (jnp.maximum(i - 1, 0), 0)
    cur = lambda i: (i, 0)
    return _call(
        body, name, (s_len // tm,),
        [pl.BlockSpec((tm, main), cur), pl.BlockSpec((tm, KVW), lambda i: (i, main // KVW)),
         pl.BlockSpec((tm, KVW), prev), pl.BlockSpec((tm, KVW), cur),
         pl.BlockSpec((tm, KVW), prev), pl.BlockSpec((tm, KVW), cur),
         full((mlen, 2 * KVW)), full((1, HEAD)), full((1, LANE)), full((1, HEAD)), full((1, HEAD))],
        [pl.BlockSpec((tm, d), cur), pl.BlockSpec((1, KV_HEADS, prow, 2 * tm), lambda i: (i, 0, 0, 0)),
         pl.BlockSpec((tm, LANE), cur)],
        [SDS((s_len, d), BF), SDS((s_len // tm, KV_HEADS, prow, 2 * tm), BF), SDS((s_len, LANE), F32)],
        scratch=[pltpu.VMEM((mlen, KVW), BF), pltpu.VMEM((mlen, KVW), BF)],
        semantics=("arbitrary",))(proj, proj, kn, kn, v, v, mkv, gqs, sinks, gq, gk)


def _mixer_swa_bwd(proj, dcat, kn, v, mkv, gqs, probs, sink_probs, gq, gk, *, name):
    s_len, d = proj.shape
    main = d - KVW
    qh = main // HEAD
    mlen = mkv.shape[0]
    tm = WINDOW
    nt = s_len // tm
    prow = qh // KV_HEADS * tm

    def body(q_ref, mq_ref, do_ref, dom_ref, kp_ref, kc_ref, vp_ref, vc_ref, mkv_ref, gqs_ref, p_ref, ps_ref,
             gq_ref, gk_ref, dproj_ref, dk_ref, dv_ref, dmkv_ref, dgqs_ref, dsinks_ref, dgq_ref, dgk_ref,
             mk_s, mv_s, dmk_s, dmv_s):
        n = pl.program_id(0)

        @pl.when(n == 0)
        def _():
            mk, mv = _mem_kv(mkv_ref[...], gk_ref[...])
            mk_s[...] = mk
            mv_s[...] = mv
            dmk_s[...] = jnp.zeros_like(dmk_s)
            dmv_s[...] = jnp.zeros_like(dmv_s)
            dk_ref[...] = jnp.zeros_like(dk_ref)
            dv_ref[...] = jnp.zeros_like(dv_ref)
            dgqs_ref[...] = jnp.zeros_like(dgqs_ref)
            dsinks_ref[...] = jnp.zeros_like(dsinks_ref)
            dgq_ref[...] = jnp.zeros_like(dgq_ref)

        kk = jnp.concatenate([kp_ref[...], kc_ref[...]], axis=0)
        vv = jnp.concatenate([vp_ref[...], vc_ref[...]], axis=0)
        dq, dkk, dvv, dgqs, dsk = _swa_bwd(q_ref[...], do_ref[...], kk, vv, gqs_ref[...],
                                           [p_ref[0, kh] for kh in range(KV_HEADS)], ps_ref[...], qh)
        prev = pl.ds(pl.multiple_of(jnp.maximum(n - 1, 0) * tm, tm), tm)
        own = pl.ds(pl.multiple_of(n * tm, tm), tm)
        dk_ref[prev, :] += dkk[:tm]
        dk_ref[own, :] += dkk[tm:]
        dv_ref[prev, :] += dvv[:tm]
        dv_ref[own, :] += dvv[tm:]
        dgqs_ref[...] += dgqs
        dsinks_ref[...] += dsk

        dmq, dmk, dmv, dgq = _mem_bwd(mq_ref[...], dom_ref[...], mk_s[...], mv_s[...], gq_ref[...])
        dmk_s[...] += dmk
        dmv_s[...] += dmv
        dgq_ref[...] += dgq
        dproj_ref[...] = jnp.concatenate([dq, dmq], axis=-1).astype(BF)

        @pl.when(n == nt - 1)
        def _():
            dmkv, dgk = _mem_kv_bwd(mkv_ref[...], dmk_s[...], dmv_s[...], gk_ref[...])
            dmkv_ref[...] = dmkv
            dgk_ref[...] = dgk

    full = lambda shape: pl.BlockSpec(shape, lambda i: (0,) * len(shape))
    prev_b = lambda i: (jnp.maximum(i - 1, 0), 0)
    cur = lambda i: (i, 0)
    memcol = lambda i: (i, main // KVW)
    return _call(
        body, name, (nt,),
        [pl.BlockSpec((tm, main), cur), pl.BlockSpec((tm, KVW), memcol),
         pl.BlockSpec((tm, main), cur), pl.BlockSpec((tm, KVW), memcol),
         pl.BlockSpec((tm, KVW), prev_b), pl.BlockSpec((tm, KVW), cur),
         pl.BlockSpec((tm, KVW), prev_b), pl.BlockSpec((tm, KVW), cur),
         full((mlen, 2 * KVW)), full((1, HEAD)), pl.BlockSpec((1, KV_HEADS, prow, 2 * tm), lambda i: (i, 0, 0, 0)),
         pl.BlockSpec((tm, LANE), cur), full((1, HEAD)), full((1, HEAD))],
        [pl.BlockSpec((tm, d), cur), full((s_len, KVW)), full((s_len, KVW)), full((mlen, 2 * KVW)),
         full((1, HEAD)), full((1, LANE)), full((1, HEAD)), full((1, HEAD))],
        [SDS((s_len, d), BF), SDS((s_len, KVW), F32), SDS((s_len, KVW), F32), SDS((mlen, 2 * KVW), F32),
         SDS((1, HEAD), F32), SDS((1, LANE), F32), SDS((1, HEAD), F32), SDS((1, HEAD), F32)],
        scratch=[pltpu.VMEM((mlen, KVW), BF), pltpu.VMEM((mlen, KVW), BF), pltpu.VMEM((mlen, KVW), F32),
                 pltpu.VMEM((mlen, KVW), F32)],
        semantics=("arbitrary",))(proj, proj, dcat, dcat, kn, kn, v, v, mkv, gqs, probs, sink_probs, gq, gk)


def _kv_prep(kv, gk, *, tm):
    s_len = kv.shape[0]

    def body(kv_ref, gk_ref, k_ref, v_ref):
        k, v = _mem_kv(kv_ref[...], gk_ref[...])
        k_ref[...] = k
        v_ref[...] = v

    row = lambda i: (i, 0)
    return _call(
        body, "kv_prep", (s_len // tm,),
        [pl.BlockSpec((tm, 2 * KVW), row), pl.BlockSpec((1, HEAD), lambda i: (0, 0))],
        [pl.BlockSpec((tm, KVW), row), pl.BlockSpec((tm, KVW), row)],
        [SDS((s_len, KVW), BF), SDS((s_len, KVW), BF)], semantics=("parallel",))(kv, gk)


def _kv_bwd(kv, dks, dvs, gk, *, tm):
    s_len = kv.shape[0]
    nl = len(dks)

    def body(*refs):
        kv_ref, gk_ref = refs[0], refs[1]
        dk_refs, dv_refs = refs[2:2 + nl], refs[2 + nl:2 + 2 * nl]
        dkv_ref, dgk_ref = refs[2 + 2 * nl], refs[3 + 2 * nl]
        dk, dv = dk_refs[0][...], dv_refs[0][...]
        for t in range(1, nl):
            dk = dk + dk_refs[t][...]
            dv = dv + dv_refs[t][...]
        dkv, dgk = _mem_kv_bwd(kv_ref[...], dk, dv, gk_ref[...])
        dkv_ref[...] = dkv.astype(BF)

        @pl.when(pl.program_id(0) == 0)
        def _():
            dgk_ref[...] = jnp.zeros_like(dgk_ref)

        dgk_ref[...] += dgk

    row = lambda i: (i, 0)
    one = pl.BlockSpec((1, HEAD), lambda i: (0, 0))
    return _call(
        body, "kv_bwd", (s_len // tm,),
        [pl.BlockSpec((tm, 2 * KVW), row), one] + [pl.BlockSpec((tm, KVW), row)] * (2 * nl),
        [pl.BlockSpec((tm, 2 * KVW), row), one],
        [SDS((s_len, 2 * KVW), BF), SDS((1, HEAD), F32)], semantics=("arbitrary",))(kv, gk, *dks, *dvs)


def _place():
    x, y, c = lax.axis_index("x"), lax.axis_index("y"), lax.axis_index("c")
    flips = [(1 - x, y), (x, 1 - y), (1 - x, 1 - y)]
    return x, y, c, flips


def _remote(src, dst, send_sem, recv_sem, to):
    return pltpu.make_async_remote_copy(src_ref=src, dst_ref=dst, send_sem=send_sem, recv_sem=recv_sem,
                                        device_id=to, device_id_type=MESH)


def _gather_copies(p_refs, wg_refs, send, recv):
    x, y, c, flips = _place()
    chip = 2 * x + y
    cps = []
    for j, (fx, fy) in enumerate(flips):
        for b in range(2):
            half = p_refs[b].shape[0] // 2
            mine = pl.ds(c * half, half)
            cps.append(_remote(p_refs[b].at[mine, :], wg_refs[b].at[chip, mine, :], send.at[2 * j + b],
                               recv.at[2 * j + b], (fx, fy, c)))
    return cps, cps


def _forward_copies(p_refs, wg_refs, send, recv):
    x, y, c, flips = _place()
    chip = 2 * x + y
    sib = (x, y, 1 - c)
    sends, arrivals = [], []
    for b in range(2):
        half = p_refs[b].shape[0] // 2
        own = _remote(p_refs[b], wg_refs[b].at[chip], send.at[b], recv.at[b], sib)
        sends.append(own)
        arrivals.append(own)
        for j, (fx, fy) in enumerate(flips):
            k = 2 + 3 * b + j
            landed = wg_refs[b].at[2 * fx + fy, pl.ds(c * half, half), :]
            other = wg_refs[b].at[2 * fx + fy, pl.ds((1 - c) * half, half), :]
            sends.append(_remote(landed, landed, send.at[k], recv.at[k], sib))
            arrivals.append(_remote(other, other, send.at[k], recv.at[k], sib))
    return sends, arrivals


def _swap_copies(g_refs, r_refs, send, recv):
    x, y, c, _ = _place()
    cps = []
    for b in range(2):
        half = g_refs[b].shape[1] // 2
        cps.append(_remote(g_refs[b].at[:, pl.ds((1 - c) * half, half), :], r_refs[b], send.at[b], recv.at[b],
                           (x, y, 1 - c)))
    return cps, cps


def _split_start(make_copies, n_sems, bufs, after, fresh, *, name):
    def body(a1, a2, b1, b2, after_ref, send, recv, *outs):
        for cp in make_copies((a1, a2), (b1, b2), send, recv)[0]:
            cp.start()
        outs[4][...] = jnp.zeros_like(outs[4])

    extra_shape = () if fresh is None else (pltpu.HBM(fresh, F32),)
    extra_spec = () if fresh is None else (HBM,)
    return pl.pallas_call(
        body, name=name,
        out_shape=(pltpu.SemaphoreType.DMA((n_sems,)), pltpu.SemaphoreType.DMA((n_sems,)))
        + tuple(pltpu.HBM(b.shape, b.dtype) for b in bufs) + (SDS((8, LANE), F32),) + extra_shape,
        in_specs=(HBM, HBM, HBM, HBM, ANY), out_specs=(SEM, SEM, HBM, HBM, HBM, HBM, VMEM_WHOLE) + extra_spec,
        input_output_aliases={0: 2, 1: 3, 2: 4, 3: 5},
        compiler_params=pltpu.CompilerParams(has_side_effects=SIDE_EFFECT))(*[_in_hbm(b) for b in bufs], after)


def _split_wait(make_copies, started, after, *, name):
    send, recv, bufs = started[0], started[1], started[2:6]

    def body(a1, a2, b1, b2, send_ref, recv_ref, after_ref, *outs):
        sends, arrivals = make_copies((a1, a2), (b1, b2), send_ref, recv_ref)
        for cp in arrivals:
            cp.wait_recv()
        for cp in sends:
            cp.wait_send()

    return pl.pallas_call(
        body, name=name, out_shape=tuple(pltpu.HBM(b.shape, b.dtype) for b in bufs),
        in_specs=(HBM, HBM, HBM, HBM, SEM, SEM, ANY), out_specs=(HBM, HBM, HBM, HBM),
        input_output_aliases={0: 0, 1: 1, 2: 2, 3: 3},
        compiler_params=pltpu.CompilerParams(has_side_effects=SIDE_EFFECT))(*bufs, send, recv, after)


def _gather_small(ps):
    def body(ps_ref, o_ref, send, recv):
        x, y, c, flips = _place()
        chip = 2 * x + y
        o_ref[chip] = ps_ref[...]
        cps = [_remote(ps_ref, o_ref.at[chip], send.at[j], recv.at[j], (fx, fy, c))
               for j, (fx, fy) in enumerate(flips)]
        for cp in cps:
            cp.start()
        for j, (fx, fy) in enumerate(flips):
            _remote(ps_ref, o_ref.at[2 * fx + fy], send.at[j], recv.at[j], (fx, fy, c)).wait_recv()
        for cp in cps:
            cp.wait_send()

    return pl.pallas_call(
        body, name="gather_small", in_specs=[VMEM_WHOLE], out_specs=VMEM_WHOLE,
        out_shape=SDS((N_CHIPS,) + ps.shape, ps.dtype),
        scratch_shapes=[pltpu.SemaphoreType.DMA((3,)), pltpu.SemaphoreType.DMA((3,))])(ps)


def _sum_sibling(g, r, place, *, tm, name):
    n_sh, half, w = r.shape
    nt = half // tm

    def body(place_ref, g_ref, r_ref, pbf_ref, own_ref):
        s = pl.program_id(1)
        p = g_ref[0] + r_ref[0]
        pbf_ref[0] = p.astype(BF)

        @pl.when(s == place_ref[1])
        def _():
            own_ref[...] = p

    return _call(
        body, name, (nt, n_sh),
        [pl.BlockSpec((1, tm, w), lambda i, s, pr: (s, pr[0] * nt + i, 0)),
         pl.BlockSpec((1, tm, w), lambda i, s, pr: (s, i, 0))],
        [pl.BlockSpec((1, tm, w), lambda i, s, pr: (s, i, 0)), pl.BlockSpec((tm, w), lambda i, s, pr: (i, 0))],
        [SDS((n_sh, half, w), BF), SDS((half, w), F32)],
        semantics=("arbitrary", "arbitrary"), prefetch=1)(place, g, r)


def _rs_copies(p_refs, land_refs, send, recv):
    _, _, c, flips = _place()
    cps = []
    for j, (fx, fy) in enumerate(flips):
        for b in range(2):
            cps.append(_remote(p_refs[b].at[2 * fx + fy], land_refs[b].at[j], send.at[2 * j + b], recv.at[2 * j + b],
                               (fx, fy, c)))
    return cps, cps


def _sum_chips(own, r, full, layer, place, *, tm, name):
    half, w = own.shape

    def body(place_ref, own_ref, r_ref, _, o_ref):
        o_ref[0, 0] = ((own_ref[...] + r_ref[0].astype(F32)) + r_ref[1].astype(F32)) + r_ref[2].astype(F32)

    return _call(
        body, name, (half // tm,),
        [pl.BlockSpec((tm, w), lambda i, pr: (i, 0)), pl.BlockSpec((3, tm, w), lambda i, pr: (0, i, 0)), ANY],
        pl.BlockSpec((1, 1, tm, w), lambda i, pr: (layer, pr[0], i, 0)), SDS(full.shape, F32),
        semantics=("parallel",), prefetch=1, aliases={3: 0})(place, own, r, full)


def _share_with_sibling(f1, f2):
    def body(_, __, o1_ref, o2_ref, send, recv):
        x, y, c, _ = _place()
        mine, other = pl.ds(c, 1), pl.ds(1 - c, 1)
        cps = [_remote(o1_ref.at[:, mine], o1_ref.at[:, mine], send.at[0], recv.at[0], (x, y, 1 - c)),
               _remote(o2_ref.at[:, mine], o2_ref.at[:, mine], send.at[1], recv.at[1], (x, y, 1 - c))]
        for cp in cps:
            cp.start()
        for cp in cps:
            cp.wait_send()
        _remote(o1_ref.at[:, other], o1_ref.at[:, other], send.at[0], recv.at[0], (x, y, 1 - c)).wait_recv()
        _remote(o2_ref.at[:, other], o2_ref.at[:, other], send.at[1], recv.at[1], (x, y, 1 - c)).wait_recv()

    return pl.pallas_call(
        body, name="share_with_sibling", in_specs=[ANY, ANY], out_specs=[ANY, ANY],
        out_shape=[SDS(f1.shape, f1.dtype), SDS(f2.shape, f2.dtype)], input_output_aliases={0: 0, 1: 1},
        scratch_shapes=[pltpu.SemaphoreType.DMA((2,)), pltpu.SemaphoreType.DMA((2,))])(f1, f2)


def _allreduce_small(sg):
    rows, w = sg.shape

    def body(sg_ref, o_ref, slots, send, recv):
        x, y, c, _ = _place()
        me = 4 * x + 2 * y + c
        slots[me] = sg_ref[...]
        peers = []
        for k in range(1, 8):
            a, b, e = (k >> 2) & 1, (k >> 1) & 1, k & 1
            px = x + a - 2 * a * x
            py = y + b - 2 * b * y
            pc = c + e - 2 * e * c
            cp = _remote(sg_ref, slots.at[me], send.at[k - 1], recv.at[k - 1], (px, py, pc))
            cp.start()
            peers.append((cp, 4 * px + 2 * py + pc, (px, py, pc)))
        for k, (cp, pidx, pid) in enumerate(peers):
            _remote(sg_ref, slots.at[pidx], send.at[k], recv.at[k], pid).wait_recv()
        acc = slots[0]
        for dev in range(1, 8):
            acc = acc + slots[dev]
        o_ref[...] = acc
        for cp, _, _ in peers:
            cp.wait_send()

    vm = pl.BlockSpec(memory_space=pltpu.VMEM)
    return pl.pallas_call(
        body, name="allreduce_small", in_specs=[vm], out_specs=vm, out_shape=SDS((rows, w), F32),
        scratch_shapes=[pltpu.VMEM((8, rows, w), F32), pltpu.SemaphoreType.DMA((7,)),
                        pltpu.SemaphoreType.DMA((7,))],
        compiler_params=pltpu.CompilerParams(vmem_limit_bytes=VMEM_LIMIT_MB << 20))(sg)


def _adamw(g_arr, layer0, g_off, per_layer, w, m, v, *, name, tm):
    rows, cols = w.shape
    assert g_off % tm == 0 and per_layer % tm == 0 and rows % per_layer == 0
    npl = per_layer // tm
    c1 = 1.0 - ADAM_B1 ** ADAM_STEP
    c2 = 1.0 - ADAM_B2 ** ADAM_STEP

    def body(g_ref, w_ref, m_ref, v_ref, go_ref, d_ref, mo_ref, vo_ref):
        g = g_ref[0]
        mn = ADAM_B1 * m_ref[...] + (1.0 - ADAM_B1) * g
        vn = ADAM_B2 * v_ref[...] + (1.0 - ADAM_B2) * (g * g)
        go_ref[...] = g
        mo_ref[...] = mn
        vo_ref[...] = vn
        d_ref[...] = -ADAM_LR * ((mn / c1) / (jnp.sqrt(vn / c2) + ADAM_EPS) + ADAM_WD * w_ref[...])

    blk = pl.BlockSpec((tm, cols), lambda i: (i, 0))
    return _call(
        body, name, (rows // tm,),
        [pl.BlockSpec((1, tm, cols), lambda i: (layer0 + i // npl, g_off // tm + i % npl, 0)), blk, blk, blk],
        [blk] * 4,
        [SDS((rows, cols), F32)] * 4, semantics=("parallel",))(g_arr, w, m, v)


def _pack_small(parts, width):
    flat = jnp.concatenate([p.reshape(-1).astype(F32) for p in parts])
    rows = -(-flat.shape[0] // (8 * width)) * 8
    return jnp.pad(flat, (0, rows * width - flat.shape[0])).reshape(rows, width)


def _unpack_small(packed, shapes):
    flat = packed.reshape(-1)
    out, off = [], 0
    for shp in shapes:
        size = 1
        for n in shp:
            size *= n
        out.append(flat[off:off + size].reshape(shp))
        off += size
    return out


def _block_diag(pw):
    g, c, _ = pw.shape
    eye = jnp.eye(g, dtype=pw.dtype)
    return (eye[:, None, :, None] * pw[:, :, None, :]).reshape(g * c, g * c)


def _diag_blocks(full, g):
    c = full.shape[0] // g
    return jnp.stack([full[i * c:(i + 1) * c, i * c:(i + 1) * c] for i in range(g)])


def kernel(x, mem, norm_mix, w_in, pool_w, pool_scale, kv_norm, w_kv, k_norm, q_norm, sinks, mem_norm, w_mem_kv, mem_q_norm, mem_k_norm, w_out, norm_mlp, w_up, w_down, loss_target, m_norm_mix, m_w_in, m_pool_w, m_pool_scale, m_kv_norm, m_w_kv, m_k_norm, m_q_norm, m_sinks, m_mem_norm, m_w_mem_kv, m_mem_q_norm, m_mem_k_norm, m_w_out, m_norm_mlp, m_w_up, m_w_down, v_norm_mix, v_w_in, v_pool_w, v_pool_scale, v_kv_norm, v_w_kv, v_k_norm, v_q_norm, v_sinks, v_mem_norm, v_w_mem_kv, v_mem_q_norm, v_mem_k_norm, v_w_out, v_norm_mlp, v_w_up, v_w_down):
    s_len, d = x.shape[1], x.shape[2]
    n_layers, n_pool = norm_mix.shape[0], pool_w.shape[0]
    n_swa = n_layers - n_pool
    main = d - KVW
    qh = main // HEAD
    ff = w_down.shape[1] * N_CHIPS
    dq = d // N_CHIPS
    assert w_up.shape[2] == d and ff == N_CHIPS * d and w_kv.shape[1] == 2 * KVW
    tm = min(512, s_len)
    tm_mem = mem.shape[1]

    cx, cy, cc = lax.axis_index("x"), lax.axis_index("y"), lax.axis_index("c")
    chip = 2 * cx + cy
    place = jnp.stack([cc, chip]).astype(jnp.int32)

    off_down, off_up, off_in, off_out = 0, d, 2 * d, 2 * d + dq
    rows1 = off_out + dq
    off_mkv, off_kv = 0, dq
    rows2 = 2 * dq

    ps = jnp.pad(pool_scale, ((0, 8 - n_pool), (0, 2 * LANE - pool_scale.shape[1])))
    psg = _gather_small(ps)
    pool_scale_full = jnp.concatenate([psg[k, :n_pool, :pool_scale.shape[1]] for k in range(N_CHIPS)], axis=1)

    def packed_weights(l):
        p1 = jnp.concatenate([w_down[l], w_up[l], w_in[l], w_out[l]]).astype(BF)
        p2 = jnp.concatenate([w_mem_kv[l], w_kv] if l == n_pool else [w_mem_kv[l]]).astype(BF)
        return p1, p2

    def gather_start(l, after):
        p1, p2 = packed_weights(l)
        bufs = (p1, p2, lax.empty((N_CHIPS,) + p1.shape, BF), lax.empty((N_CHIPS,) + p2.shape, BF))
        return _split_start(_gather_copies, 6, bufs, after, None, name=f"gather_start_{l}")

    def gather_land(l, started, after):
        bufs = _split_wait(_gather_copies, started, after, name=f"gather_wait_{l}")
        return _split_start(_forward_copies, 8, bufs, place, None, name=f"forward_start_{l}")

    def gather_finish(l, forwarding, after):
        bufs = _split_wait(_forward_copies, forwarding, after, name=f"forward_wait_{l}")
        return bufs[2], bufs[3]

    def w_rows(arr, off, nrows, width):
        assert off % nrows == 0
        return (arr, (N_CHIPS, nrows, width), lambda j: (0, off // nrows, 0))

    row = lambda a: a.reshape(1, -1)
    h = x.reshape(s_len, d)
    memx = mem.reshape(tm_mem, d)
    tgt = loss_target.reshape(s_len, d)
    pbd = [_block_diag(pool_w[l]).astype(BF) for l in range(n_pool)]
    sinks_pad = [jnp.pad(row(sinks[j]), ((0, 0), (0, LANE - qh))) for j in range(n_swa)]

    w_in_l, w_out_l, w_down_l, w_up_all_l, w_mkv_l = [], [], [], [], []
    w_kv_g = None
    forwarding = gather_land(0, gather_start(0, psg), psg)
    travelling = gather_start(1, forwarding[6]) if n_layers > 1 else None
    saved, probs, sink_probs = [], {}, {}
    kv = hn_kv = kn = vsh = None
    for l in range(n_layers):
        wg1, wg2 = gather_finish(l, forwarding, h if l else forwarding[6])
        w_in_l.append(w_rows(wg1, off_in, dq, d))
        w_out_l.append(w_rows(wg1, off_out, dq, d))
        w_down_l.append(w_rows(wg1, off_down, d, d))
        w_up_all_l.append((wg1, (N_CHIPS, d, d), lambda j: (0, off_up // d, 0)))
        w_mkv_l.append(w_rows(wg2, off_mkv, dq, 2 * KVW))
        g_mix = row(norm_mix[l])
        if travelling is not None:
            g_mix = g_mix + travelling[6][0, 0]
        if l == n_pool:
            w_kv_g = w_rows(wg2, off_kv, dq, 2 * KVW)
            kv, hn_kv = _norm_mm(h, row(kv_norm), w_kv_g, 1, 2 * KVW, act=False, name="kv_proj", tm=tm)
            kn, vsh = _kv_prep(kv, row(k_norm), tm=tm)
        h0 = h
        proj, xn = _norm_mm(h0, g_mix, w_in_l[l], 1, d, act=False, name=f"in_proj_{l}", tm=tm)
        mkv, memn = _norm_mm(memx, row(mem_norm[l]), w_mkv_l[l], 1, 2 * KVW, act=False, name=f"mem_kv_{l}", tm=tm_mem)
        if l < n_pool:
            cat = _mixer_pool_fwd(proj, mkv, pbd[l], row(pool_scale_full[l]), row(mem_q_norm[l]), row(mem_k_norm[l]),
                                  name=f"mixer_fwd_{l}", tm=tm)
        else:
            j = l - n_pool
            cat, probs[l], sink_probs[l] = _mixer_swa_fwd(
                proj, kn, vsh, mkv, row(q_norm[j]), sinks_pad[j], row(mem_q_norm[l]), row(mem_k_norm[l]),
                name=f"mixer_fwd_{l}")
        h1 = _mm_res(h0, cat, w_out_l[l], name=f"out_proj_{l}", tm=tm)
        hh, xm = _norm_mm(h1, row(norm_mlp[l]), w_up_all_l[l], N_CHIPS, d, act=True, name=f"mlp_up_{l}", tm=tm)
        after = None
        if travelling is not None:
            forwarding = gather_land(l + 1, travelling, hh)
            travelling = gather_start(l + 2, forwarding[6]) if l + 2 < n_layers else None
            after = forwarding[6]
        h = _mm_res(h1, hh, w_down_l[l], name=f"mlp_down_{l}", tm=tm, after=after)
        saved.append((h0, proj, xn, mkv, memn, cat, h1, hh, xm))

    dh, dh_bf, loss_part = _loss_head(h, tgt, tm=tm)
    loss = lax.psum(loss_part[0, 0], ("x", "y", "c"))

    half1, half2 = rows1 // 2, rows2 // 2
    g1 = lax.empty((N_CHIPS, rows1, d), F32)
    pending = {}
    swapping = None
    tk = min(512, d)

    def reduce_begin(l, swapped, after):
        g1_l, g2_l, r1, r2 = _split_wait(_swap_copies, swapped, after, name=f"swap_wait_{l}")
        pb1, own1 = _sum_sibling(g1_l, r1, place, tm=_tile(half1, 256), name=f"sum_sibling_a_{l}")
        pb2, own2 = _sum_sibling(g2_l, r2, place, tm=_tile(half2, 256), name=f"sum_sibling_b_{l}")
        bufs = (pb1, pb2, lax.empty((3, half1, d), BF), lax.empty((3, half2, 2 * KVW), BF))
        return _split_start(_rs_copies, 6, bufs, place, None, name=f"reduce_start_{l}"), own1, own2
    zeros_mem = jnp.zeros((tm_mem, d), F32)

    def rows_map(off, nrows, tkk):
        per = nrows // tkk
        return lambda i, j: (i // per, off // tkk + i % per, 0)

    def cols_map(off, tkk):
        return lambda i, j: (j, off // tkk + i, 0)

    d_norm_mix, d_norm_mlp, d_mem_norm = [None] * n_layers, [None] * n_layers, [None] * n_layers
    d_mem_q, d_mem_k = [None] * n_layers, [None] * n_layers
    d_pool_w, d_pool_scale = [None] * n_pool, [None] * n_pool
    d_q_norm, d_sinks = [None] * n_swa, [None] * n_swa
    dks, dvs = [], []
    d_kv_norm = d_k_norm = None
    for l in reversed(range(n_layers)):
        h0, proj, xn, mkv, memn, cat, h1, hh, xm = saved[l]
        g2 = jnp.zeros((N_CHIPS, rows2, 2 * KVW), F32)
        g1 = _mm_tn(hh, dh_bf, g1, rows_map(off_down, d, tk), tk, d, name=f"dw_down_{l}")
        du = _mm_nt_relu2(dh_bf, hh, w_down_l[l], N_CHIPS, name=f"d_mlp_act_{l}", tm=tm)
        g1 = _mm_tn(xm, du, g1, cols_map(off_up, tk), tk, d, name=f"dw_up_{l}")
        g_mlp = row(norm_mlp[l])
        if swapping is not None:
            pending[swapping[0]] = reduce_begin(*swapping, after=g1)
            g_mlp = g_mlp + pending[swapping[0]][0][6][0, 0]
        dh1, dh1_bf, d_norm_mlp[l] = _mm_nt_normbwd(du, w_up_all_l[l], N_CHIPS, h1, g_mlp, dh,
                                                    name=f"d_mlp_in_{l}", tm=tm)
        tkq = min(tk, dq)
        g1 = _mm_tn(cat, dh1_bf, g1, rows_map(off_out, dq, tkq), tkq, d, name=f"dw_out_{l}")
        dcat = _mm_nt(dh1_bf, w_out_l[l], d, name=f"d_cat_{l}", tm=tm)
        if l < n_pool:
            dproj, dpbd, dscale, dmkv, d_mem_q[l], d_mem_k[l] = _mixer_pool_bwd(
                proj, dcat, mkv, pbd[l], row(pool_scale_full[l]), row(mem_q_norm[l]), row(mem_k_norm[l]),
                name=f"mixer_bwd_{l}", tm=tm)
            d_pool_w[l] = _diag_blocks(dpbd, len(POOL_WINDOWS))
            d_pool_scale[l] = dscale
        else:
            j = l - n_pool
            dproj, dk, dv, dmkv, d_q_norm[j], dsk, d_mem_q[l], d_mem_k[l] = _mixer_swa_bwd(
                proj, dcat, kn, vsh, mkv, row(q_norm[j]), probs[l], sink_probs[l], row(mem_q_norm[l]),
                row(mem_k_norm[l]), name=f"mixer_bwd_{l}")
            d_sinks[j] = dsk[0, :qh]
            dks.append(dk)
            dvs.append(dv)
        g1 = _mm_tn(xn, dproj, g1, rows_map(off_in, dq, tkq), tkq, d, name=f"dw_in_{l}")
        dh, dh_bf, d_norm_mix[l] = _mm_nt_normbwd(dproj, w_in_l[l], 1, h0, row(norm_mix[l]), dh1,
                                                  name=f"d_in_{l}", tm=tm)
        g2 = _mm_tn(memn, dmkv, g2, rows_map(off_mkv, dq, tkq), tkq, 2 * KVW, name=f"dw_mem_kv_{l}")
        _, _, d_mem_norm[l] = _mm_nt_normbwd(dmkv, w_mkv_l[l], 1, memx, row(mem_norm[l]), zeros_mem,
                                             name=f"d_mem_norm_{l}", tm=tm_mem)
        if l == n_pool:
            dkv, d_k_norm = _kv_bwd(kv, dks, dvs, row(k_norm), tm=tm)
            g2 = _mm_tn(hn_kv, dkv, g2, rows_map(off_kv, dq, tkq), tkq, 2 * KVW, name="dw_kv")
            dh, dh_bf, d_kv_norm = _mm_nt_normbwd(dkv, w_kv_g, 1, h0, row(kv_norm), dh, name="d_kv_in", tm=tm)
        bufs = (g1, g2, lax.empty((N_CHIPS, half1, d), F32), lax.empty((N_CHIPS, half2, 2 * KVW), F32))
        swapping = (l, _split_start(_swap_copies, 2, bufs, place, (N_CHIPS, rows1, d) if l > 0 else None,
                                    name=f"swap_start_{l}"))
        g1 = swapping[1][7] if l > 0 else None
    pending[0] = reduce_begin(*swapping, after=dh)
    grad_x = dh.reshape(x.shape)

    small_names = ["norm_mix", "pool_w", "pool_scale", "kv_norm", "k_norm", "q_norm", "sinks", "mem_norm",
                   "mem_q_norm", "mem_k_norm", "norm_mlp"]
    small_grads = {
        "norm_mix": jnp.concatenate(d_norm_mix), "pool_w": jnp.stack(d_pool_w),
        "pool_scale": jnp.concatenate(d_pool_scale), "kv_norm": d_kv_norm[0], "k_norm": d_k_norm[0],
        "q_norm": jnp.concatenate(d_q_norm), "sinks": jnp.stack(d_sinks), "mem_norm": jnp.concatenate(d_mem_norm),
        "mem_q_norm": jnp.concatenate(d_mem_q), "mem_k_norm": jnp.concatenate(d_mem_k),
        "norm_mlp": jnp.concatenate(d_norm_mlp)}
    width = d
    sg = _pack_small([small_grads[n] for n in small_names], width)
    sg = sg + pending[0][0][6][0, 0]
    sg = _allreduce_small(sg)
    reduced = dict(zip(small_names, _unpack_small(sg, [small_grads[n].shape for n in small_names])))
    psw = pool_scale.shape[1]
    reduced["pool_scale"] = lax.dynamic_slice_in_dim(reduced["pool_scale"], chip * psw, psw, axis=1)
    params = dict(norm_mix=(norm_mix, m_norm_mix, v_norm_mix), pool_w=(pool_w, m_pool_w, v_pool_w),
                  pool_scale=(pool_scale, m_pool_scale, v_pool_scale), kv_norm=(kv_norm, m_kv_norm, v_kv_norm),
                  k_norm=(k_norm, m_k_norm, v_k_norm), q_norm=(q_norm, m_q_norm, v_q_norm),
                  sinks=(sinks, m_sinks, v_sinks), mem_norm=(mem_norm, m_mem_norm, v_mem_norm),
                  mem_q_norm=(mem_q_norm, m_mem_q_norm, v_mem_q_norm),
                  mem_k_norm=(mem_k_norm, m_mem_k_norm, v_mem_k_norm), norm_mlp=(norm_mlp, m_norm_mlp, v_norm_mlp))
    shapes = [params[n][0].shape for n in small_names]
    packs = [_pack_small([reduced[n].reshape(params[n][0].shape) for n in small_names], width)]
    packs += [_pack_small([params[n][t] for n in small_names], width) for t in range(3)]
    res = _adamw(packs[0][None], 0, 0, packs[0].shape[0], packs[1], packs[2], packs[3], name="adamw_small", tm=8)
    small = {n: [] for n in small_names}
    for r in res:
        for n, a in zip(small_names, _unpack_small(r, shapes)):
            small[n].append(a)

    full1 = lax.empty((n_layers, 2, half1, d), F32)
    full2 = lax.empty((n_layers, 2, half2, 2 * KVW), F32)
    for l in reversed(range(n_layers)):
        exchange, own1, own2 = pending[l]
        _, _, x1, x2 = _split_wait(_rs_copies, exchange, res[0], name=f"reduce_wait_{l}")
        full1 = _sum_chips(own1, x1, full1, l, place, tm=_tile(half1, 256), name=f"sum_chips_a_{l}")
        full2 = _sum_chips(own2, x2, full2, l, place, tm=_tile(half2, 256), name=f"sum_chips_b_{l}")
    full1, full2 = _share_with_sibling(full1, full2)
    full1 = full1.reshape(n_layers, rows1, d)
    full2 = full2.reshape(n_layers, rows2, 2 * KVW)

    big = {}
    for name, arr, layer0, off, per, w_, m_, v_ in (
            ("w_down", full1, 0, off_down, d, w_down, m_w_down, v_w_down),
            ("w_up", full1, 0, off_up, d, w_up, m_w_up, v_w_up),
            ("w_in", full1, 0, off_in, dq, w_in, m_w_in, v_w_in),
            ("w_out", full1, 0, off_out, dq, w_out, m_w_out, v_w_out),
            ("w_mem_kv", full2, 0, off_mkv, dq, w_mem_kv, m_w_mem_kv, v_w_mem_kv),
            ("w_kv", full2, n_pool, off_kv, dq, w_kv, m_w_kv, v_w_kv)):
        cols = arr.shape[2]
        upd = _adamw(arr, layer0, off, per, w_.reshape(-1, cols), m_.reshape(-1, cols), v_.reshape(-1, cols),
                     name=f"adamw_{name}", tm=min(256, dq))
        big[name] = [r.reshape(w_.shape) for r in upd]

    order = ["norm_mix", "w_in", "pool_w", "pool_scale", "kv_norm", "w_kv", "k_norm", "q_norm", "sinks", "mem_norm",
             "w_mem_kv", "mem_q_norm", "mem_k_norm", "w_out", "norm_mlp", "w_up", "w_down"]
    out = {**big, **small}
    return (loss, grad_x, *[out[n][0] for n in order], *[out[n][1] for n in order],
            *[out[n][2] for n in order], *[out[n][3] for n in order])
```

```python
import functools

import jax
import jax.numpy as jnp
from jax import lax
from jax.experimental import pallas as pl
from jax.experimental.pallas import tpu as pltpu

F32, BF = jnp.float32, jnp.bfloat16
SDS = jax.ShapeDtypeStruct
MESH = pl.DeviceIdType.MESH
ANY = pl.BlockSpec(memory_space=pl.ANY)
HBM = pl.BlockSpec(memory_space=pltpu.HBM)
SEM = pl.BlockSpec(memory_space=pltpu.SEMAPHORE)
VMEM_WHOLE = pl.BlockSpec(memory_space=pltpu.VMEM)
SIDE_EFFECT = pltpu.SideEffectType.DATAFLOW_SIDE_EFFECTING


def _in_hbm(a):
    return pltpu.with_memory_space_constraint(a, pltpu.HBM)


EPS = 1e-6
HEAD = 64
KV_HEADS = 4
KVW = KV_HEADS * HEAD
WINDOW = 128
POOL_WINDOWS = (2, 4, 8, 16)
HALO = 16
QK_SCALE = HEAD ** -0.5
NEG = float(jnp.finfo(jnp.float32).min)
N_CHIPS = 4
LANE = 128

ADAM_LR, ADAM_B1, ADAM_B2, ADAM_EPS, ADAM_WD, ADAM_STEP = 0.001, 0.9, 0.999, 1e-08, 0.01, 10

VMEM_LIMIT_MB = 56


def _call(body, name, grid, in_specs, out_specs, out_shape, *, scratch=(), semantics=None, aliases=None,
          prefetch=0):
    params = pltpu.CompilerParams(dimension_semantics=semantics, vmem_limit_bytes=VMEM_LIMIT_MB << 20)
    if prefetch:
        spec = pltpu.PrefetchScalarGridSpec(num_scalar_prefetch=prefetch, grid=grid, in_specs=in_specs,
                                            out_specs=out_specs, scratch_shapes=list(scratch))
        return pl.pallas_call(body, name=name, grid_spec=spec, out_shape=out_shape,
                              input_output_aliases=aliases or {}, compiler_params=params)
    return pl.pallas_call(body, name=name, grid=grid, in_specs=in_specs, out_specs=out_specs, out_shape=out_shape,
                          scratch_shapes=list(scratch), input_output_aliases=aliases or {}, compiler_params=params)


def _tile(n, pref):
    return max(t for t in range(8, min(n, pref) + 1, 8) if n % t == 0)


def _dot(a, b):
    return jnp.dot(a, b, preferred_element_type=F32)


def _dot_nt(a, b):
    return lax.dot_general(a, b, (((1,), (1,)), ((), ())), preferred_element_type=F32)


def _dot_tn(a, b):
    return lax.dot_general(a, b, (((0,), (0,)), ((), ())), preferred_element_type=F32)


def _rms(x):
    r = lax.rsqrt(jnp.mean(x * x, axis=-1, keepdims=True) + EPS)
    return x * r, r


def _rms_bwd(dy, xh, r, g):
    dg = jnp.sum(dy * xh, axis=0, keepdims=True)
    dyg = dy * g
    dx = r * (dyg - xh * jnp.mean(dyg * xh, axis=-1, keepdims=True))
    return dx, dg


def _norm_mm(h, g, w, nj, tn, *, act, name, tm):
    w_arr, w_block, w_imap = w
    rows, d = h.shape

    def body(h_ref, g_ref, w_ref, y_ref, xn_ref):
        @pl.when(pl.program_id(1) == 0)
        def _():
            xh, _ = _rms(h_ref[...])
            xn_ref[...] = (xh * g_ref[...]).astype(BF)

        w_j = w_ref[pl.program_id(1)] if (nj > 1 and w_block[0] == nj) else w_ref[...].reshape(d, tn)
        u = _dot(xn_ref[...], w_j)
        if act:
            a = jnp.maximum(u, 0.0)
            y_ref[...] = (a * a).astype(BF)
        else:
            y_ref[...] = u

    return _call(
        body, name, (rows // tm, nj),
        [pl.BlockSpec((tm, d), lambda i, j: (i, 0)), pl.BlockSpec((1, d), lambda i, j: (0, 0)),
         pl.BlockSpec(w_block, lambda i, j: w_imap(j))],
        [pl.BlockSpec((tm, tn), lambda i, j: (i, j)), pl.BlockSpec((tm, d), lambda i, j: (i, 0))],
        [SDS((rows, nj * tn), BF if act else F32), SDS((rows, d), BF)],
        semantics=("parallel", "arbitrary"))(h, g, w_arr)


def _mm_res(res, a, w, *, name, tm, after=None):
    w_arr, w_block, w_imap = w
    rows, k = a.shape
    n = res.shape[1]

    def body(res_ref, a_ref, w_ref, *rest):
        rest[-1][...] = res_ref[...] + _dot(a_ref[...], w_ref[...].reshape(k, n))

    extra = [] if after is None else [after]
    return _call(
        body, name, (rows // tm,),
        [pl.BlockSpec((tm, n), lambda i: (i, 0)), pl.BlockSpec((tm, k), lambda i: (i, 0)),
         pl.BlockSpec(w_block, lambda i: w_imap(0))] + [ANY] * len(extra),
        pl.BlockSpec((tm, n), lambda i: (i, 0)), SDS((rows, n), F32), semantics=("parallel",))(res, a, w_arr, *extra)


def _mm_nt(dy, w, k, *, name, tm):
    w_arr, w_block, w_imap = w
    rows, n = dy.shape

    def body(dy_ref, w_ref, o_ref):
        o_ref[...] = _dot_nt(dy_ref[...], w_ref[...].reshape(k, n))

    return _call(
        body, name, (rows // tm,),
        [pl.BlockSpec((tm, n), lambda i: (i, 0)), pl.BlockSpec(w_block, lambda i: w_imap(0))],
        pl.BlockSpec((tm, k), lambda i: (i, 0)), SDS((rows, k), F32), semantics=("parallel",))(dy, w_arr)


def _mm_nt_relu2(dh, hh, w, nj, *, name, tm):
    w_arr, w_block, w_imap = w
    rows, d = dh.shape
    tk = hh.shape[1] // nj

    def body(dh_ref, hh_ref, w_ref, o_ref):
        w_j = w_ref[pl.program_id(1)] if w_block[0] == nj else w_ref[...].reshape(tk, d)
        dhh = _dot_nt(dh_ref[...], w_j)
        o_ref[...] = (dhh * (2.0 * jnp.sqrt(hh_ref[...].astype(F32)))).astype(BF)

    return _call(
        body, name, (rows // tm, nj),
        [pl.BlockSpec((tm, d), lambda i, j: (i, 0)), pl.BlockSpec((tm, tk), lambda i, j: (i, j)),
         pl.BlockSpec(w_block, lambda i, j: w_imap(j))],
        pl.BlockSpec((tm, tk), lambda i, j: (i, j)), SDS((rows, nj * tk), BF),
        semantics=("parallel", "parallel"))(dh, hh, w_arr)


def _mm_nt_normbwd(dy, w, nsplit, h, g, dres, *, name, tm):
    w_arr, w_block, w_imap = w
    rows, n = dy.shape
    d = h.shape[1]
    ns = n // nsplit

    def body(dy_ref, w_ref, h_ref, g_ref, dres_ref, o_ref, obf_ref, dg_ref):
        if nsplit == 1:
            dxn = _dot_nt(dy_ref[...].astype(BF), w_ref[...].reshape(d, n))
        else:
            dxn = _dot_nt(dy_ref[:, 0:ns].astype(BF), w_ref[0])
            for s in range(1, nsplit):
                dxn += _dot_nt(dy_ref[:, s * ns:(s + 1) * ns].astype(BF), w_ref[s])
        xh, r = _rms(h_ref[...])
        dx, dg = _rms_bwd(dxn, xh, r, g_ref[...])
        out = dres_ref[...] + dx
        o_ref[...] = out
        obf_ref[...] = out.astype(BF)

        @pl.when(pl.program_id(0) == 0)
        def _():
            dg_ref[...] = jnp.zeros_like(dg_ref)

        dg_ref[...] += dg

    row = lambda i: (i, 0)
    return _call(
        body, name, (rows // tm,),
        [pl.BlockSpec((tm, n), row), pl.BlockSpec(w_block, lambda i: w_imap(0)), pl.BlockSpec((tm, d), row),
         pl.BlockSpec((1, d), lambda i: (0, 0)), pl.BlockSpec((tm, d), row)],
        [pl.BlockSpec((tm, d), row), pl.BlockSpec((tm, d), row), pl.BlockSpec((1, d), lambda i: (0, 0))],
        [SDS((rows, d), F32), SDS((rows, d), BF), SDS((1, d), F32)],
        semantics=("arbitrary",))(dy, w_arr, h, g, dres)


def _mm_tn(x, dy, packed, out_imap, tk, tn, *, name):
    s_len, k = x.shape
    n = dy.shape[1]

    def body(x_ref, dy_ref, _, o_ref):
        o_ref[0] = _dot_tn(x_ref[...], dy_ref[...].astype(BF))

    return _call(
        body, name, (k // tk, n // tn),
        [pl.BlockSpec((s_len, tk), lambda i, j: (0, i)), pl.BlockSpec((s_len, tn), lambda i, j: (0, j)), ANY],
        pl.BlockSpec((1, tk, tn), out_imap), SDS(packed.shape, packed.dtype),
        semantics=("parallel", "parallel"), aliases={2: 0})(x, dy, packed)


def _loss_head(y, tgt, *, tm):
    rows, d = y.shape

    def body(y_ref, t_ref, dh_ref, dhbf_ref, loss_ref):
        err = y_ref[...] - t_ref[...]
        dh = err * (1.0 / d)
        dh_ref[...] = dh
        dhbf_ref[...] = dh.astype(BF)

        @pl.when(pl.program_id(0) == 0)
        def _():
            loss_ref[...] = jnp.zeros_like(loss_ref)

        loss_ref[...] += 0.5 * jnp.sum(jnp.mean(err * err, axis=-1, keepdims=True), axis=0, keepdims=True)

    row = lambda i: (i, 0)
    return _call(
        body, "loss_head", (rows // tm,), [pl.BlockSpec((tm, d), row), pl.BlockSpec((tm, d), row)],
        [pl.BlockSpec((tm, d), row), pl.BlockSpec((tm, d), row), pl.BlockSpec((1, 1), lambda i: (0, 0))],
        [SDS((rows, d), F32), SDS((rows, d), BF), SDS((1, 1), F32)], semantics=("arbitrary",))(y, tgt)


def _hs(h):
    return slice(HEAD * h, HEAD * (h + 1))


def _softmax_rows(s):
    e = jnp.exp(s - jnp.max(s, axis=-1, keepdims=True))
    return e * (1.0 / jnp.sum(e, axis=-1, keepdims=True))


def _scaled_bf16(qn):
    return (qn * QK_SCALE).astype(BF)


def _mem_fwd(mq, mk, mv, gq):
    outs, probs = [], []
    for h in range(KV_HEADS):
        xh, _ = _rms(mq[:, _hs(h)])
        p = _softmax_rows(_dot_nt(_scaled_bf16(xh * gq), mk[:, _hs(h)])).astype(BF)
        probs.append(p)
        outs.append(_dot(p, mv[:, _hs(h)]))
    return jnp.concatenate(outs, axis=-1), jnp.concatenate(probs, axis=-1)


def _mem_bwd(mq, do, mk, mv, gq, probs):
    dqs, dks, dvs, dgq = [], [], [], 0.0
    mlen = mk.shape[0]
    for h in range(KV_HEADS):
        xh, r = _rms(mq[:, _hs(h)])
        qn = _scaled_bf16(xh * gq)
        p_bf = probs[:, h * mlen:(h + 1) * mlen]
        p = p_bf.astype(F32)
        doh = do[:, _hs(h)].astype(BF)
        dp = _dot_nt(doh, mv[:, _hs(h)])
        ds = (p * (dp - jnp.sum(p * dp, axis=-1, keepdims=True))).astype(BF)
        dq, dg = _rms_bwd(_dot(ds, mk[:, _hs(h)]) * QK_SCALE, xh, r, gq)
        dqs.append(dq)
        dgq = dgq + dg
        dks.append(_dot_tn(ds, qn))
        dvs.append(_dot_tn(p_bf, doh))
    cat = lambda xs: jnp.concatenate(xs, axis=-1)
    return cat(dqs), cat(dks), cat(dvs), dgq


def _mem_kv(mkv, gk):
    ks = []
    for h in range(KV_HEADS):
        xh, _ = _rms(mkv[:, _hs(h)])
        ks.append(xh * gk)
    return jnp.concatenate(ks, axis=-1).astype(BF), mkv[:, KVW:].astype(BF)


def _mem_kv_bwd(mkv, dmk, dmv, gk):
    dxs, dgk = [], 0.0
    for h in range(KV_HEADS):
        xh, r = _rms(mkv[:, _hs(h)])
        dx, dg = _rms_bwd(dmk[:, _hs(h)], xh, r, gk)
        dxs.append(dx)
        dgk = dgk + dg
    return jnp.concatenate(dxs + [dmv], axis=-1), dgk


def _pool_select(col, gd, a2, a4, a8, a16):
    return jnp.where(col < gd, a2, jnp.where(col < 2 * gd, a4, jnp.where(col < 3 * gd, a8, a16)))


def _pool_count(t0, shape, gd):
    col = lax.broadcasted_iota(jnp.int32, shape, 1)
    t = t0 + lax.broadcasted_iota(jnp.int32, shape, 0)
    win = _pool_select(col, gd, *POOL_WINDOWS)
    return jnp.minimum(t + 1, win).astype(F32)


def _pool_diff(u, halo, t0, gd):
    c = jnp.concatenate([halo, u], axis=0)
    s2 = c + pltpu.roll(c, 1, 0)
    s4 = s2 + pltpu.roll(s2, 2, 0)
    s8 = s4 + pltpu.roll(s4, 4, 0)
    s16 = s8 + pltpu.roll(s8, 8, 0)
    col = lax.broadcasted_iota(jnp.int32, c.shape, 1)
    ws = _pool_select(col, gd, s2, s4, s8, s16)[HALO:]
    return ws / _pool_count(t0, u.shape, gd) - u


def _pool_diff_bwd(dd, dd_halo, t0, gd):
    t = dd.shape[0]
    z = jnp.concatenate([dd / _pool_count(t0, dd.shape, gd), dd_halo / _pool_count(t0 + t, dd_halo.shape, gd)], axis=0)
    n = z.shape[0]
    f2 = z + pltpu.roll(z, n - 1, 0)
    f4 = f2 + pltpu.roll(f2, n - 2, 0)
    f8 = f4 + pltpu.roll(f4, n - 4, 0)
    f16 = f8 + pltpu.roll(f8, n - 8, 0)
    col = lax.broadcasted_iota(jnp.int32, z.shape, 1)
    return _pool_select(col, gd, f2, f4, f8, f16)[:t] - dd


def _swa_bias(n):
    qi = lax.broadcasted_iota(jnp.int32, (WINDOW, 2 * WINDOW), 0)
    kj = lax.broadcasted_iota(jnp.int32, (WINDOW, 2 * WINDOW), 1)
    dist = qi + WINDOW - kj
    valid = (dist >= 0) & (dist < WINDOW) & ((kj >= WINDOW) | (n > 0))
    return dist.astype(F32), valid


def _slopes(qh):
    return [2.0 ** (-8.0 * (h + 1) / qh) for h in range(qh)]


def _swa_probs(qn, kk, dist, valid, slope, sink):
    s = _dot_nt(qn, kk) - slope * dist
    s = jnp.where(valid, s, NEG)
    m = jnp.maximum(jnp.max(s, axis=-1, keepdims=True), sink)
    e = jnp.exp(s - m)
    es = jnp.exp(sink - m)
    z = jnp.sum(e, axis=-1, keepdims=True) + es
    inv = 1.0 / z
    return e * inv, es * inv


def _swa_group(q, kh, grp, n, qh, sinks):
    heads = range(kh * grp, (kh + 1) * grp)
    dist, valid = _swa_bias(n)
    slopes = _slopes(qh)
    rows = lambda vals: jnp.concatenate([jnp.broadcast_to(v, (WINDOW, 1)) for v in vals], axis=0)
    qs = jnp.concatenate([q[:, _hs(h)] for h in heads], axis=0)
    slope = rows([jnp.full((1, 1), slopes[h], F32) for h in heads])
    sink = rows([sinks[:, h:h + 1] for h in heads])
    return qs, slope, sink, jnp.concatenate([dist] * grp, axis=0), jnp.concatenate([valid] * grp, axis=0)


def _swa_fwd(q, kk, vv, gq, sinks, n, qh):
    grp = qh // KV_HEADS
    lane = lax.broadcasted_iota(jnp.int32, (WINDOW, LANE), 1)
    outs, probs, sink_probs = [], [], jnp.zeros((WINDOW, LANE), F32)
    for kh in range(KV_HEADS):
        qs, slope, sink, dist, valid = _swa_group(q, kh, grp, n, qh, sinks)
        xh, _ = _rms(qs)
        p, ps = _swa_probs(_scaled_bf16(xh * gq), kk[:, _hs(kh)], dist, valid, slope, sink)
        p = p.astype(BF)
        probs.append(p)
        o = _dot(p, vv[:, _hs(kh)])
        for g in range(grp):
            outs.append(o[g * WINDOW:(g + 1) * WINDOW])
            sink_probs = jnp.where(lane == kh * grp + g, ps[g * WINDOW:(g + 1) * WINDOW], sink_probs)
    return jnp.concatenate(outs, axis=-1), probs, sink_probs


def _swa_bwd(q, do, kk, vv, gq, probs, sink_probs, qh):
    grp = qh // KV_HEADS
    lane = lax.broadcasted_iota(jnp.int32, (1, LANE), 1)
    dqs, dks, dvs, dgq, dsk = [], [], [], 0.0, jnp.zeros((1, LANE), F32)
    for kh in range(KV_HEADS):
        heads = range(kh * grp, (kh + 1) * grp)
        xh, r = _rms(jnp.concatenate([q[:, _hs(h)] for h in heads], axis=0))
        qn = _scaled_bf16(xh * gq)
        p_bf = probs[kh]
        p = p_bf.astype(F32)
        ps = jnp.concatenate([sink_probs[:, h:h + 1] for h in heads], axis=0)
        dos = jnp.concatenate([do[:, _hs(h)] for h in heads], axis=0).astype(BF)
        dp = _dot_nt(dos, vv[:, _hs(kh)])
        delta = jnp.sum(p * dp, axis=-1, keepdims=True)
        ds = (p * (dp - delta)).astype(BF)
        dsink = ps * delta
        for g in range(grp):
            part = -jnp.sum(dsink[g * WINDOW:(g + 1) * WINDOW], axis=0, keepdims=True)
            dsk = dsk + jnp.where(lane == kh * grp + g, part, 0.0)
        dq, dg = _rms_bwd(_dot(ds, kk[:, _hs(kh)]) * QK_SCALE, xh, r, gq)
        dqs += [dq[g * WINDOW:(g + 1) * WINDOW] for g in range(grp)]
        dgq = dgq + dg
        dks.append(_dot_tn(ds, qn))
        dvs.append(_dot_tn(p_bf, dos))
    cat = lambda xs: jnp.concatenate(xs, axis=-1)
    return cat(dqs), cat(dks), cat(dvs), dgq, dsk


def _mixer_pool_fwd(proj, mkv, pbd, scale, gq, gk, *, name, tm):
    s_len, d = proj.shape
    main = d - KVW
    gd = main // len(POOL_WINDOWS)
    mlen = mkv.shape[0]
    hb = tm // HALO

    def body(u_ref, halo_ref, mq_ref, mkv_ref, pbd_ref, scale_ref, gq_ref, gk_ref, o_ref, mp_ref, mk_s, mv_s):
        i = pl.program_id(0)

        @pl.when(i == 0)
        def _():
            mk, mv = _mem_kv(mkv_ref[...], gk_ref[...])
            mk_s[...] = mk
            mv_s[...] = mv

        halo = jnp.where(i > 0, halo_ref[...], 0.0)
        dif = _pool_diff(u_ref[...], halo, i * tm, gd)
        mixed = _dot(dif.astype(BF), pbd_ref[...]) * scale_ref[...]
        mem, mp_ref[...] = _mem_fwd(mq_ref[...], mk_s[...], mv_s[...], gq_ref[...])
        o_ref[...] = jnp.concatenate([mixed, mem], axis=-1).astype(BF)

    full = lambda shape: pl.BlockSpec(shape, lambda i: (0,) * len(shape))
    return _call(
        body, name, (s_len // tm,),
        [pl.BlockSpec((tm, main), lambda i: (i, 0)),
         pl.BlockSpec((HALO, main), lambda i: (jnp.maximum(i * hb - 1, 0), 0)),
         pl.BlockSpec((tm, KVW), lambda i: (i, main // KVW)),
         full((mlen, 2 * KVW)), full((main, main)), full((1, main)), full((1, HEAD)), full((1, HEAD))],
        [pl.BlockSpec((tm, d), lambda i: (i, 0)), pl.BlockSpec((tm, KV_HEADS * mlen), lambda i: (i, 0))],
        [SDS((s_len, d), BF), SDS((s_len, KV_HEADS * mlen), BF)],
        scratch=[pltpu.VMEM((mlen, KVW), BF), pltpu.VMEM((mlen, KVW), BF)],
        semantics=("arbitrary",))(proj, proj, proj, mkv, pbd, scale, gq, gk)


def _mixer_pool_bwd(proj, dcat, mkv, pbd, scale, gq, gk, mem_probs, *, name, tm):
    s_len, d = proj.shape
    main = d - KVW
    gd = main // len(POOL_WINDOWS)
    mlen = mkv.shape[0]
    hb = tm // HALO
    nt = s_len // tm
    last_halo = s_len // HALO - 1

    def body(u_ref, halo_ref, mq_ref, do_ref, donext_ref, dom_ref, mkv_ref, pbd_ref, scale_ref, gq_ref, gk_ref, mp_ref,
             dproj_ref, dpbd_ref, dscale_ref, dmkv_ref, dgq_ref, dgk_ref, mk_s, mv_s, dmk_s, dmv_s):
        i = pl.program_id(0)

        @pl.when(i == 0)
        def _():
            mk, mv = _mem_kv(mkv_ref[...], gk_ref[...])
            mk_s[...] = mk
            mv_s[...] = mv
            dmk_s[...] = jnp.zeros_like(dmk_s)
            dmv_s[...] = jnp.zeros_like(dmv_s)
            dpbd_ref[...] = jnp.zeros_like(dpbd_ref)
            dscale_ref[...] = jnp.zeros_like(dscale_ref)
            dgq_ref[...] = jnp.zeros_like(dgq_ref)

        pbd = pbd_ref[...]
        scale = scale_ref[...]
        halo = jnp.where(i > 0, halo_ref[...], 0.0)
        dif = _pool_diff(u_ref[...], halo, i * tm, gd).astype(BF)
        do = do_ref[...]
        dscale_ref[...] += jnp.sum(do * _dot(dif, pbd), axis=0, keepdims=True)
        dmixed = (do * scale).astype(BF)
        dpbd_ref[...] += _dot_tn(dif, dmixed)
        dd = _dot_nt(dmixed, pbd)
        donext = jnp.where(i < nt - 1, donext_ref[...], 0.0)
        dd_halo = _dot_nt((donext * scale).astype(BF), pbd)
        du = _pool_diff_bwd(dd, dd_halo, i * tm, gd)

        dmq, dmk, dmv, dgq = _mem_bwd(mq_ref[...], dom_ref[...], mk_s[...], mv_s[...], gq_ref[...], mp_ref[...])
        dmk_s[...] += dmk
        dmv_s[...] += dmv
        dgq_ref[...] += dgq
        dproj_ref[...] = jnp.concatenate([du, dmq], axis=-1).astype(BF)

        @pl.when(i == nt - 1)
        def _():
            dmkv, dgk = _mem_kv_bwd(mkv_ref[...], dmk_s[...], dmv_s[...], gk_ref[...])
            dmkv_ref[...] = dmkv
            dgk_ref[...] = dgk

    full = lambda shape: pl.BlockSpec(shape, lambda i: (0,) * len(shape))
    return _call(
        body, name, (nt,),
        [pl.BlockSpec((tm, main), lambda i: (i, 0)),
         pl.BlockSpec((HALO, main), lambda i: (jnp.maximum(i * hb - 1, 0), 0)),
         pl.BlockSpec((tm, KVW), lambda i: (i, main // KVW)),
         pl.BlockSpec((tm, main), lambda i: (i, 0)),
         pl.BlockSpec((HALO, main), lambda i: (jnp.minimum((i + 1) * hb, last_halo), 0)),
         pl.BlockSpec((tm, KVW), lambda i: (i, main // KVW)),
         full((mlen, 2 * KVW)), full((main, main)), full((1, main)), full((1, HEAD)), full((1, HEAD)),
         pl.BlockSpec((tm, KV_HEADS * mlen), lambda i: (i, 0))],
        [pl.BlockSpec((tm, d), lambda i: (i, 0)), full((main, main)), full((1, main)), full((mlen, 2 * KVW)),
         full((1, HEAD)), full((1, HEAD))],
        [SDS((s_len, d), BF), SDS((main, main), F32), SDS((1, main), F32), SDS((mlen, 2 * KVW), F32),
         SDS((1, HEAD), F32), SDS((1, HEAD), F32)],
        scratch=[pltpu.VMEM((mlen, KVW), BF), pltpu.VMEM((mlen, KVW), BF), pltpu.VMEM((mlen, KVW), F32),
                 pltpu.VMEM((mlen, KVW), F32)],
        semantics=("arbitrary",))(proj, proj, proj, dcat, dcat, dcat, mkv, pbd, scale, gq, gk, mem_probs)


def _mixer_swa_fwd(proj, kn, v, mkv, gqs, sinks, gq, gk, *, name):
    s_len, d = proj.shape
    main = d - KVW
    qh = main // HEAD
    mlen = mkv.shape[0]
    tm = WINDOW
    prow = qh // KV_HEADS * tm

    def body(q_ref, mq_ref, kp_ref, kc_ref, vp_ref, vc_ref, mkv_ref, gqs_ref, sinks_ref, gq_ref, gk_ref, o_ref,
             p_ref, ps_ref, mp_ref, mk_s, mv_s):
        n = pl.program_id(0)

        @pl.when(n == 0)
        def _():
            mk, mv = _mem_kv(mkv_ref[...], gk_ref[...])
            mk_s[...] = mk
            mv_s[...] = mv

        kk = jnp.concatenate([kp_ref[...], kc_ref[...]], axis=0)
        vv = jnp.concatenate([vp_ref[...], vc_ref[...]], axis=0)
        att, probs, sink_probs = _swa_fwd(q_ref[...], kk, vv, gqs_ref[...], sinks_ref[...], n, qh)
        for kh in range(KV_HEADS):
            p_ref[0, kh] = probs[kh]
        ps_ref[...] = sink_probs
        mem, mp_ref[...] = _mem_fwd(mq_ref[...], mk_s[...], mv_s[...], gq_ref[...])
        o_ref[...] = jnp.concatenate([att, mem], axis=-1).astype(BF)

    full = lambda shape: pl.BlockSpec(shape, lambda i: (0,) * len(shape))
    prev = lambda i: (jnp.maximum(i - 1, 0), 0)
    cur = lambda i: (i, 0)
    return _call(
        body, name, (s_len // tm,),
        [pl.BlockSpec((tm, main), cur), pl.BlockSpec((tm, KVW), lambda i: (i, main // KVW)),
         pl.BlockSpec((tm, KVW), prev), pl.BlockSpec((tm, KVW), cur),
         pl.BlockSpec((tm, KVW), prev), pl.BlockSpec((tm, KVW), cur),
         full((mlen, 2 * KVW)), full((1, HEAD)), full((1, LANE)), full((1, HEAD)), full((1, HEAD))],
        [pl.BlockSpec((tm, d), cur), pl.BlockSpec((1, KV_HEADS, prow, 2 * tm), lambda i: (i, 0, 0, 0)),
         pl.BlockSpec((tm, LANE), cur), pl.BlockSpec((tm, KV_HEADS * mlen), cur)],
        [SDS((s_len, d), BF), SDS((s_len // tm, KV_HEADS, prow, 2 * tm), BF), SDS((s_len, LANE), F32),
         SDS((s_len, KV_HEADS * mlen), BF)],
        scratch=[pltpu.VMEM((mlen, KVW), BF), pltpu.VMEM((mlen, KVW), BF)],
        semantics=("arbitrary",))(proj, proj, kn, kn, v, v, mkv, gqs, sinks, gq, gk)


def _mixer_swa_bwd(proj, dcat, kn, v, mkv, gqs, probs, sink_probs, gq, gk, mem_probs, *, name):
    s_len, d = proj.shape
    main = d - KVW
    qh = main // HEAD
    mlen = mkv.shape[0]
    tm = WINDOW
    nt = s_len // tm
    prow = qh // KV_HEADS * tm

    def body(q_ref, mq_ref, do_ref, dom_ref, kp_ref, kc_ref, vp_ref, vc_ref, mkv_ref, gqs_ref, p_ref, ps_ref,
             gq_ref, gk_ref, mp_ref, dproj_ref, dk_ref, dv_ref, dmkv_ref, dgqs_ref, dsinks_ref, dgq_ref, dgk_ref,
             mk_s, mv_s, dmk_s, dmv_s):
        n = pl.program_id(0)

        @pl.when(n == 0)
        def _():
            mk, mv = _mem_kv(mkv_ref[...], gk_ref[...])
            mk_s[...] = mk
            mv_s[...] = mv
            dmk_s[...] = jnp.zeros_like(dmk_s)
            dmv_s[...] = jnp.zeros_like(dmv_s)
            dk_ref[...] = jnp.zeros_like(dk_ref)
            dv_ref[...] = jnp.zeros_like(dv_ref)
            dgqs_ref[...] = jnp.zeros_like(dgqs_ref)
            dsinks_ref[...] = jnp.zeros_like(dsinks_ref)
            dgq_ref[...] = jnp.zeros_like(dgq_ref)

        kk = jnp.concatenate([kp_ref[...], kc_ref[...]], axis=0)
        vv = jnp.concatenate([vp_ref[...], vc_ref[...]], axis=0)
        dq, dkk, dvv, dgqs, dsk = _swa_bwd(q_ref[...], do_ref[...], kk, vv, gqs_ref[...],
                                           [p_ref[0, kh] for kh in range(KV_HEADS)], ps_ref[...], qh)
        prev = pl.ds(pl.multiple_of(jnp.maximum(n - 1, 0) * tm, tm), tm)
        own = pl.ds(pl.multiple_of(n * tm, tm), tm)
        dk_ref[prev, :] += dkk[:tm]
        dk_ref[own, :] += dkk[tm:]
        dv_ref[prev, :] += dvv[:tm]
        dv_ref[own, :] += dvv[tm:]
        dgqs_ref[...] += dgqs
        dsinks_ref[...] += dsk

        dmq, dmk, dmv, dgq = _mem_bwd(mq_ref[...], dom_ref[...], mk_s[...], mv_s[...], gq_ref[...], mp_ref[...])
        dmk_s[...] += dmk
        dmv_s[...] += dmv
        dgq_ref[...] += dgq
        dproj_ref[...] = jnp.concatenate([dq, dmq], axis=-1).astype(BF)

        @pl.when(n == nt - 1)
        def _():
            dmkv, dgk = _mem_kv_bwd(mkv_ref[...], dmk_s[...], dmv_s[...], gk_ref[...])
            dmkv_ref[...] = dmkv
            dgk_ref[...] = dgk

    full = lambda shape: pl.BlockSpec(shape, lambda i: (0,) * len(shape))
    prev_b = lambda i: (jnp.maximum(i - 1, 0), 0)
    cur = lambda i: (i, 0)
    memcol = lambda i: (i, main // KVW)
    return _call(
        body, name, (nt,),
        [pl.BlockSpec((tm, main), cur), pl.BlockSpec((tm, KVW), memcol),
         pl.BlockSpec((tm, main), cur), pl.BlockSpec((tm, KVW), memcol),
         pl.BlockSpec((tm, KVW), prev_b), pl.BlockSpec((tm, KVW), cur),
         pl.BlockSpec((tm, KVW), prev_b), pl.BlockSpec((tm, KVW), cur),
         full((mlen, 2 * KVW)), full((1, HEAD)), pl.BlockSpec((1, KV_HEADS, prow, 2 * tm), lambda i: (i, 0, 0, 0)),
         pl.BlockSpec((tm, LANE), cur), full((1, HEAD)), full((1, HEAD)),
         pl.BlockSpec((tm, KV_HEADS * mlen), cur)],
        [pl.BlockSpec((tm, d), cur), full((s_len, KVW)), full((s_len, KVW)), full((mlen, 2 * KVW)),
         full((1, HEAD)), full((1, LANE)), full((1, HEAD)), full((1, HEAD))],
        [SDS((s_len, d), BF), SDS((s_len, KVW), F32), SDS((s_len, KVW), F32), SDS((mlen, 2 * KVW), F32),
         SDS((1, HEAD), F32), SDS((1, LANE), F32), SDS((1, HEAD), F32), SDS((1, HEAD), F32)],
        scratch=[pltpu.VMEM((mlen, KVW), BF), pltpu.VMEM((mlen, KVW), BF), pltpu.VMEM((mlen, KVW), F32),
                 pltpu.VMEM((mlen, KVW), F32)],
        semantics=("arbitrary",))(proj, proj, dcat, dcat, kn, kn, v, v, mkv, gqs, probs, sink_probs, gq, gk, mem_probs)


def _kv_prep(kv, gk, *, tm):
    s_len = kv.shape[0]

    def body(kv_ref, gk_ref, k_ref, v_ref):
        k, v = _mem_kv(kv_ref[...], gk_ref[...])
        k_ref[...] = k
        v_ref[...] = v

    row = lambda i: (i, 0)
    return _call(
        body, "kv_prep", (s_len // tm,),
        [pl.BlockSpec((tm, 2 * KVW), row), pl.BlockSpec((1, HEAD), lambda i: (0, 0))],
        [pl.BlockSpec((tm, KVW), row), pl.BlockSpec((tm, KVW), row)],
        [SDS((s_len, KVW), BF), SDS((s_len, KVW), BF)], semantics=("parallel",))(kv, gk)


def _kv_bwd(kv, dks, dvs, gk, *, tm):
    s_len = kv.shape[0]
    nl = len(dks)

    def body(*refs):
        kv_ref, gk_ref = refs[0], refs[1]
        dk_refs, dv_refs = refs[2:2 + nl], refs[2 + nl:2 + 2 * nl]
        dkv_ref, dgk_ref = refs[2 + 2 * nl], refs[3 + 2 * nl]
        dk, dv = dk_refs[0][...], dv_refs[0][...]
        for t in range(1, nl):
            dk = dk + dk_refs[t][...]
            dv = dv + dv_refs[t][...]
        dkv, dgk = _mem_kv_bwd(kv_ref[...], dk, dv, gk_ref[...])
        dkv_ref[...] = dkv.astype(BF)

        @pl.when(pl.program_id(0) == 0)
        def _():
            dgk_ref[...] = jnp.zeros_like(dgk_ref)

        dgk_ref[...] += dgk

    row = lambda i: (i, 0)
    one = pl.BlockSpec((1, HEAD), lambda i: (0, 0))
    return _call(
        body, "kv_bwd", (s_len // tm,),
        [pl.BlockSpec((tm, 2 * KVW), row), one] + [pl.BlockSpec((tm, KVW), row)] * (2 * nl),
        [pl.BlockSpec((tm, 2 * KVW), row), one],
        [SDS((s_len, 2 * KVW), BF), SDS((1, HEAD), F32)], semantics=("arbitrary",))(kv, gk, *dks, *dvs)


def _place():
    x, y, c = lax.axis_index("x"), lax.axis_index("y"), lax.axis_index("c")
    flips = [(1 - x, y), (x, 1 - y), (1 - x, 1 - y)]
    return x, y, c, flips


def _remote(src, dst, send_sem, recv_sem, to):
    return pltpu.make_async_remote_copy(src_ref=src, dst_ref=dst, send_sem=send_sem, recv_sem=recv_sem,
                                        device_id=to, device_id_type=MESH)


def _gather_copies(p_refs, wg_refs, send, recv):
    x, y, c, flips = _place()
    chip = 2 * x + y
    cps = []
    for j, (fx, fy) in enumerate(flips):
        for b in range(2):
            half = p_refs[b].shape[0] // 2
            mine = pl.ds(c * half, half)
            cps.append(_remote(p_refs[b].at[mine, :], wg_refs[b].at[chip, mine, :], send.at[2 * j + b],
                               recv.at[2 * j + b], (fx, fy, c)))
    return cps, cps


def _forward_copies(p_refs, wg_refs, send, recv):
    x, y, c, flips = _place()
    chip = 2 * x + y
    sib = (x, y, 1 - c)
    sends, arrivals = [], []
    for b in range(2):
        half = p_refs[b].shape[0] // 2
        own = _remote(p_refs[b], wg_refs[b].at[chip], send.at[b], recv.at[b], sib)
        sends.append(own)
        arrivals.append(own)
        for j, (fx, fy) in enumerate(flips):
            k = 2 + 3 * b + j
            landed = wg_refs[b].at[2 * fx + fy, pl.ds(c * half, half), :]
            other = wg_refs[b].at[2 * fx + fy, pl.ds((1 - c) * half, half), :]
            sends.append(_remote(landed, landed, send.at[k], recv.at[k], sib))
            arrivals.append(_remote(other, other, send.at[k], recv.at[k], sib))
    return sends, arrivals


def _swap_copies(g_refs, r_refs, send, recv):
    x, y, c, _ = _place()
    cps = []
    for b in range(2):
        half = g_refs[b].shape[1] // 2
        cps.append(_remote(g_refs[b].at[:, pl.ds((1 - c) * half, half), :], r_refs[b], send.at[b], recv.at[b],
                           (x, y, 1 - c)))
    return cps, cps


def _split_start(make_copies, n_sems, bufs, after, fresh, *, name):
    def body(a1, a2, b1, b2, after_ref, send, recv, *outs):
        for cp in make_copies((a1, a2), (b1, b2), send, recv)[0]:
            cp.start()
        outs[4][...] = jnp.zeros_like(outs[4])

    extra_shape = () if fresh is None else (pltpu.HBM(fresh, F32),)
    extra_spec = () if fresh is None else (HBM,)
    return pl.pallas_call(
        body, name=name,
        out_shape=(pltpu.SemaphoreType.DMA((n_sems,)), pltpu.SemaphoreType.DMA((n_sems,)))
        + tuple(pltpu.HBM(b.shape, b.dtype) for b in bufs) + (SDS((8, LANE), F32),) + extra_shape,
        in_specs=(HBM, HBM, HBM, HBM, ANY), out_specs=(SEM, SEM, HBM, HBM, HBM, HBM, VMEM_WHOLE) + extra_spec,
        input_output_aliases={0: 2, 1: 3, 2: 4, 3: 5},
        compiler_params=pltpu.CompilerParams(has_side_effects=SIDE_EFFECT))(*[_in_hbm(b) for b in bufs], after)


def _split_wait(make_copies, started, after, *, name):
    send, recv, bufs = started[0], started[1], started[2:6]

    def body(a1, a2, b1, b2, send_ref, recv_ref, after_ref, *outs):
        sends, arrivals = make_copies((a1, a2), (b1, b2), send_ref, recv_ref)
        for cp in arrivals:
            cp.wait_recv()
        for cp in sends:
            cp.wait_send()

    return pl.pallas_call(
        body, name=name, out_shape=tuple(pltpu.HBM(b.shape, b.dtype) for b in bufs),
        in_specs=(HBM, HBM, HBM, HBM, SEM, SEM, ANY), out_specs=(HBM, HBM, HBM, HBM),
        input_output_aliases={0: 0, 1: 1, 2: 2, 3: 3},
        compiler_params=pltpu.CompilerParams(has_side_effects=SIDE_EFFECT))(*bufs, send, recv, after)


def _gather_small(ps):
    def body(ps_ref, o_ref, send, recv):
        x, y, c, flips = _place()
        chip = 2 * x + y
        o_ref[chip] = ps_ref[...]
        cps = [_remote(ps_ref, o_ref.at[chip], send.at[j], recv.at[j], (fx, fy, c))
               for j, (fx, fy) in enumerate(flips)]
        for cp in cps:
            cp.start()
        for j, (fx, fy) in enumerate(flips):
            _remote(ps_ref, o_ref.at[2 * fx + fy], send.at[j], recv.at[j], (fx, fy, c)).wait_recv()
        for cp in cps:
            cp.wait_send()

    return pl.pallas_call(
        body, name="gather_small", in_specs=[VMEM_WHOLE], out_specs=VMEM_WHOLE,
        out_shape=SDS((N_CHIPS,) + ps.shape, ps.dtype),
        scratch_shapes=[pltpu.SemaphoreType.DMA((3,)), pltpu.SemaphoreType.DMA((3,))])(ps)


def _sum_sibling(g, r, place, *, tm, name):
    n_sh, half, w = r.shape
    nt = half // tm

    def body(place_ref, g_ref, r_ref, pbf_ref, own_ref):
        s = pl.program_id(1)
        p = g_ref[0] + r_ref[0]
        pbf_ref[0] = p.astype(BF)

        @pl.when(s == place_ref[1])
        def _():
            own_ref[...] = p

    return _call(
        body, name, (nt, n_sh),
        [pl.BlockSpec((1, tm, w), lambda i, s, pr: (s, pr[0] * nt + i, 0)),
         pl.BlockSpec((1, tm, w), lambda i, s, pr: (s, i, 0))],
        [pl.BlockSpec((1, tm, w), lambda i, s, pr: (s, i, 0)), pl.BlockSpec((tm, w), lambda i, s, pr: (i, 0))],
        [SDS((n_sh, half, w), BF), SDS((half, w), F32)],
        semantics=("arbitrary", "arbitrary"), prefetch=1)(place, g, r)


def _rs_copies(p_refs, land_refs, send, recv):
    _, _, c, flips = _place()
    cps = []
    for j, (fx, fy) in enumerate(flips):
        for b in range(2):
            cps.append(_remote(p_refs[b].at[2 * fx + fy], land_refs[b].at[j], send.at[2 * j + b], recv.at[2 * j + b],
                               (fx, fy, c)))
    return cps, cps


def _sum_chips(own, r, full, layer, place, *, tm, name):
    half, w = own.shape

    def body(place_ref, own_ref, r_ref, _, o_ref):
        o_ref[0, 0] = ((own_ref[...] + r_ref[0].astype(F32)) + r_ref[1].astype(F32)) + r_ref[2].astype(F32)

    return _call(
        body, name, (half // tm,),
        [pl.BlockSpec((tm, w), lambda i, pr: (i, 0)), pl.BlockSpec((3, tm, w), lambda i, pr: (0, i, 0)), ANY],
        pl.BlockSpec((1, 1, tm, w), lambda i, pr: (layer, pr[0], i, 0)), SDS(full.shape, F32),
        semantics=("parallel",), prefetch=1, aliases={3: 0})(place, own, r, full)


def _share_with_sibling(f1, f2):
    def body(_, __, o1_ref, o2_ref, send, recv):
        x, y, c, _ = _place()
        mine, other = pl.ds(c, 1), pl.ds(1 - c, 1)
        cps = [_remote(o1_ref.at[:, mine], o1_ref.at[:, mine], send.at[0], recv.at[0], (x, y, 1 - c)),
               _remote(o2_ref.at[:, mine], o2_ref.at[:, mine], send.at[1], recv.at[1], (x, y, 1 - c))]
        for cp in cps:
            cp.start()
        for cp in cps:
            cp.wait_send()
        _remote(o1_ref.at[:, other], o1_ref.at[:, other], send.at[0], recv.at[0], (x, y, 1 - c)).wait_recv()
        _remote(o2_ref.at[:, other], o2_ref.at[:, other], send.at[1], recv.at[1], (x, y, 1 - c)).wait_recv()

    return pl.pallas_call(
        body, name="share_with_sibling", in_specs=[ANY, ANY], out_specs=[ANY, ANY],
        out_shape=[SDS(f1.shape, f1.dtype), SDS(f2.shape, f2.dtype)], input_output_aliases={0: 0, 1: 1},
        scratch_shapes=[pltpu.SemaphoreType.DMA((2,)), pltpu.SemaphoreType.DMA((2,))])(f1, f2)


def _allreduce_small(sg):
    rows, w = sg.shape
    half = rows // 2
    assert half % 8 == 0

    def body(sg_ref, o_ref, sib_buf, part, slots, send, recv):
        x, y, c, flips = _place()
        chip = 2 * x + y
        sib = (x, y, 1 - c)
        mine = pl.ds(pl.multiple_of(c * half, 8), half)
        other = pl.ds(pl.multiple_of((1 - c) * half, 8), half)
        to_sib = _remote(sg_ref.at[other, :], sib_buf, send.at[0], recv.at[0], sib)
        to_sib.start()
        to_sib.wait_recv()
        part[...] = sg_ref[mine, :] + sib_buf[...]
        slots[chip] = part[...]
        to_chips = [_remote(part, slots.at[chip], send.at[1 + j], recv.at[1 + j], (fx, fy, c))
                    for j, (fx, fy) in enumerate(flips)]
        for cp in to_chips:
            cp.start()
        for j, (fx, fy) in enumerate(flips):
            _remote(part, slots.at[2 * fx + fy], send.at[1 + j], recv.at[1 + j], (fx, fy, c)).wait_recv()
        o_ref[mine, :] = ((slots[0] + slots[1]) + slots[2]) + slots[3]
        back = _remote(o_ref.at[mine, :], o_ref.at[mine, :], send.at[4], recv.at[4], sib)
        back.start()
        _remote(o_ref.at[other, :], o_ref.at[other, :], send.at[4], recv.at[4], sib).wait_recv()
        for cp in [to_sib, back] + to_chips:
            cp.wait_send()

    return pl.pallas_call(
        body, name="allreduce_small", in_specs=[VMEM_WHOLE], out_specs=VMEM_WHOLE, out_shape=SDS((rows, w), F32),
        scratch_shapes=[pltpu.VMEM((half, w), F32), pltpu.VMEM((half, w), F32), pltpu.VMEM((N_CHIPS, half, w), F32),
                        pltpu.SemaphoreType.DMA((5,)), pltpu.SemaphoreType.DMA((5,))])(sg)


def _adamw(g_arr, layer0, g_off, per_layer, w, m, v, *, name, tm):
    rows, cols = w.shape
    assert g_off % tm == 0 and per_layer % tm == 0 and rows % per_layer == 0
    npl = per_layer // tm
    c1 = 1.0 - ADAM_B1 ** ADAM_STEP
    c2 = 1.0 - ADAM_B2 ** ADAM_STEP

    def body(g_ref, w_ref, m_ref, v_ref, go_ref, d_ref, mo_ref, vo_ref):
        g = g_ref[0]
        mn = ADAM_B1 * m_ref[...] + (1.0 - ADAM_B1) * g
        vn = ADAM_B2 * v_ref[...] + (1.0 - ADAM_B2) * (g * g)
        go_ref[...] = g
        mo_ref[...] = mn
        vo_ref[...] = vn
        d_ref[...] = -ADAM_LR * ((mn / c1) / (jnp.sqrt(vn / c2) + ADAM_EPS) + ADAM_WD * w_ref[...])

    blk = pl.BlockSpec((tm, cols), lambda i: (i, 0))
    return _call(
        body, name, (rows // tm,),
        [pl.BlockSpec((1, tm, cols), lambda i: (layer0 + i // npl, g_off // tm + i % npl, 0)), blk, blk, blk],
        [blk] * 4,
        [SDS((rows, cols), F32)] * 4, semantics=("parallel",))(g_arr, w, m, v)


def _pack_small(parts, width):
    flat = jnp.concatenate([p.reshape(-1).astype(F32) for p in parts])
    rows = -(-flat.shape[0] // (16 * width)) * 16
    return jnp.pad(flat, (0, rows * width - flat.shape[0])).reshape(rows, width)


def _unpack_small(packed, shapes):
    flat = packed.reshape(-1)
    out, off = [], 0
    for shp in shapes:
        size = 1
        for n in shp:
            size *= n
        out.append(flat[off:off + size].reshape(shp))
        off += size
    return out


def _block_diag(pw):
    g, c, _ = pw.shape
    eye = jnp.eye(g, dtype=pw.dtype)
    return (eye[:, None, :, None] * pw[:, :, None, :]).reshape(g * c, g * c)


def _diag_blocks(full, g):
    c = full.shape[0] // g
    return jnp.stack([full[i * c:(i + 1) * c, i * c:(i + 1) * c] for i in range(g)])


def kernel(x, mem, norm_mix, w_in, pool_w, pool_scale, kv_norm, w_kv, k_norm, q_norm, sinks, mem_norm, w_mem_kv, mem_q_norm, mem_k_norm, w_out, norm_mlp, w_up, w_down, loss_target, m_norm_mix, m_w_in, m_pool_w, m_pool_scale, m_kv_norm, m_w_kv, m_k_norm, m_q_norm, m_sinks, m_mem_norm, m_w_mem_kv, m_mem_q_norm, m_mem_k_norm, m_w_out, m_norm_mlp, m_w_up, m_w_down, v_norm_mix, v_w_in, v_pool_w, v_pool_scale, v_kv_norm, v_w_kv, v_k_norm, v_q_norm, v_sinks, v_mem_norm, v_w_mem_kv, v_mem_q_norm, v_mem_k_norm, v_w_out, v_norm_mlp, v_w_up, v_w_down):
    s_len, d = x.shape[1], x.shape[2]
    n_layers, n_pool = norm_mix.shape[0], pool_w.shape[0]
    n_swa = n_layers - n_pool
    main = d - KVW
    qh = main // HEAD
    ff = w_down.shape[1] * N_CHIPS
    dq = d // N_CHIPS
    assert w_up.shape[2] == d and ff == N_CHIPS * d and w_kv.shape[1] == 2 * KVW
    tm = min(512, s_len)
    tm_mem = mem.shape[1]

    cx, cy, cc = lax.axis_index("x"), lax.axis_index("y"), lax.axis_index("c")
    chip = 2 * cx + cy
    place = jnp.stack([cc, chip]).astype(jnp.int32)

    off_down, off_up, off_in, off_out = 0, d, 2 * d, 2 * d + dq
    rows1 = off_out + dq
    off_mkv, off_kv = 0, dq
    rows2 = 2 * dq

    ps = jnp.pad(pool_scale, ((0, 8 - n_pool), (0, 2 * LANE - pool_scale.shape[1])))
    psg = _gather_small(ps)
    pool_scale_full = jnp.concatenate([psg[k, :n_pool, :pool_scale.shape[1]] for k in range(N_CHIPS)], axis=1)

    def packed_weights(l):
        p1 = jnp.concatenate([w_down[l], w_up[l], w_in[l], w_out[l]]).astype(BF)
        p2 = jnp.concatenate([w_mem_kv[l], w_kv] if l == n_pool else [w_mem_kv[l]]).astype(BF)
        return p1, p2

    def gather_start(l, after):
        p1, p2 = packed_weights(l)
        bufs = (p1, p2, lax.empty((N_CHIPS,) + p1.shape, BF), lax.empty((N_CHIPS,) + p2.shape, BF))
        return _split_start(_gather_copies, 6, bufs, after, None, name=f"gather_start_{l}")

    def gather_land(l, started, after):
        bufs = _split_wait(_gather_copies, started, after, name=f"gather_wait_{l}")
        return _split_start(_forward_copies, 8, bufs, place, None, name=f"forward_start_{l}")

    def gather_finish(l, forwarding, after):
        bufs = _split_wait(_forward_copies, forwarding, after, name=f"forward_wait_{l}")
        return bufs[2], bufs[3]

    def w_rows(arr, off, nrows, width):
        assert off % nrows == 0
        return (arr, (N_CHIPS, nrows, width), lambda j: (0, off // nrows, 0))

    row = lambda a: a.reshape(1, -1)
    h = x.reshape(s_len, d)
    memx = mem.reshape(tm_mem, d)
    tgt = loss_target.reshape(s_len, d)
    pbd = [_block_diag(pool_w[l]).astype(BF) for l in range(n_pool)]
    sinks_pad = [jnp.pad(row(sinks[j]), ((0, 0), (0, LANE - qh))) for j in range(n_swa)]

    w_in_l, w_out_l, w_down_l, w_up_all_l, w_mkv_l = [], [], [], [], []
    w_kv_g = None
    forwarding = gather_land(0, gather_start(0, psg), psg)
    travelling = gather_start(1, forwarding[6]) if n_layers > 1 else None
    saved, probs, sink_probs, mem_probs = [], {}, {}, {}
    kv = hn_kv = kn = vsh = None
    for l in range(n_layers):
        wg1, wg2 = gather_finish(l, forwarding, h if l else forwarding[6])
        w_in_l.append(w_rows(wg1, off_in, dq, d))
        w_out_l.append(w_rows(wg1, off_out, dq, d))
        w_down_l.append(w_rows(wg1, off_down, d, d))
        w_up_all_l.append((wg1, (N_CHIPS, d, d), lambda j: (0, off_up // d, 0)))
        w_mkv_l.append(w_rows(wg2, off_mkv, dq, 2 * KVW))
        g_mix = row(norm_mix[l])
        if travelling is not None:
            g_mix = g_mix + travelling[6][0, 0]
        if l == n_pool:
            w_kv_g = w_rows(wg2, off_kv, dq, 2 * KVW)
            kv, hn_kv = _norm_mm(h, row(kv_norm), w_kv_g, 1, 2 * KVW, act=False, name="kv_proj", tm=tm)
            kn, vsh = _kv_prep(kv, row(k_norm), tm=tm)
        h0 = h
        proj, xn = _norm_mm(h0, g_mix, w_in_l[l], 1, d, act=False, name=f"in_proj_{l}", tm=tm)
        mkv, memn = _norm_mm(memx, row(mem_norm[l]), w_mkv_l[l], 1, 2 * KVW, act=False, name=f"mem_kv_{l}", tm=tm_mem)
        if l < n_pool:
            cat, mem_probs[l] = _mixer_pool_fwd(proj, mkv, pbd[l], row(pool_scale_full[l]), row(mem_q_norm[l]),
                                                row(mem_k_norm[l]), name=f"mixer_fwd_{l}", tm=tm)
        else:
            j = l - n_pool
            cat, probs[l], sink_probs[l], mem_probs[l] = _mixer_swa_fwd(
                proj, kn, vsh, mkv, row(q_norm[j]), sinks_pad[j], row(mem_q_norm[l]), row(mem_k_norm[l]),
                name=f"mixer_fwd_{l}")
        h1 = _mm_res(h0, cat, w_out_l[l], name=f"out_proj_{l}", tm=tm)
        hh, xm = _norm_mm(h1, row(norm_mlp[l]), w_up_all_l[l], N_CHIPS, d, act=True, name=f"mlp_up_{l}", tm=tm)
        after = None
        if travelling is not None:
            forwarding = gather_land(l + 1, travelling, hh)
            travelling = gather_start(l + 2, forwarding[6]) if l + 2 < n_layers else None
            after = forwarding[6]
        h = _mm_res(h1, hh, w_down_l[l], name=f"mlp_down_{l}", tm=tm, after=after)
        saved.append((h0, proj, xn, mkv, memn, cat, h1, hh, xm))

    dh, dh_bf, loss_part = _loss_head(h, tgt, tm=tm)
    loss = lax.psum(loss_part[0, 0], ("x", "y", "c"))

    half1, half2 = rows1 // 2, rows2 // 2
    g1 = lax.empty((N_CHIPS, rows1, d), F32)
    pending = {}
    swapping = None
    tk = min(512, d)

    def reduce_begin(l, swapped, after):
        g1_l, g2_l, r1, r2 = _split_wait(_swap_copies, swapped, after, name=f"swap_wait_{l}")
        pb1, own1 = _sum_sibling(g1_l, r1, place, tm=_tile(half1, 256), name=f"sum_sibling_a_{l}")
        pb2, own2 = _sum_sibling(g2_l, r2, place, tm=_tile(half2, 256), name=f"sum_sibling_b_{l}")
        bufs = (pb1, pb2, lax.empty((3, half1, d), BF), lax.empty((3, half2, 2 * KVW), BF))
        return _split_start(_rs_copies, 6, bufs, place, None, name=f"reduce_start_{l}"), own1, own2
    zeros_mem = jnp.zeros((tm_mem, d), F32)

    def rows_map(off, nrows, tkk):
        per = nrows // tkk
        return lambda i, j: (i // per, off // tkk + i % per, 0)

    def cols_map(off, tkk):
        return lambda i, j: (j, off // tkk + i, 0)

    d_norm_mix, d_norm_mlp, d_mem_norm = [None] * n_layers, [None] * n_layers, [None] * n_layers
    d_mem_q, d_mem_k = [None] * n_layers, [None] * n_layers
    d_pool_w, d_pool_scale = [None] * n_pool, [None] * n_pool
    d_q_norm, d_sinks = [None] * n_swa, [None] * n_swa
    dks, dvs = [], []
    d_kv_norm = d_k_norm = None
    for l in reversed(range(n_layers)):
        h0, proj, xn, mkv, memn, cat, h1, hh, xm = saved[l]
        g2 = jnp.zeros((N_CHIPS, rows2, 2 * KVW), F32)
        g1 = _mm_tn(hh, dh_bf, g1, rows_map(off_down, d, tk), tk, d, name=f"dw_down_{l}")
        du = _mm_nt_relu2(dh_bf, hh, w_down_l[l], N_CHIPS, name=f"d_mlp_act_{l}", tm=tm)
        g1 = _mm_tn(xm, du, g1, cols_map(off_up, tk), tk, d, name=f"dw_up_{l}")
        g_mlp = row(norm_mlp[l])
        if swapping is not None:
            pending[swapping[0]] = reduce_begin(*swapping, after=g1)
            g_mlp = g_mlp + pending[swapping[0]][0][6][0, 0]
        dh1, dh1_bf, d_norm_mlp[l] = _mm_nt_normbwd(du, w_up_all_l[l], N_CHIPS, h1, g_mlp, dh,
                                                    name=f"d_mlp_in_{l}", tm=tm)
        tkq = min(tk, dq)
        g1 = _mm_tn(cat, dh1_bf, g1, rows_map(off_out, dq, tkq), tkq, d, name=f"dw_out_{l}")
        dcat = _mm_nt(dh1_bf, w_out_l[l], d, name=f"d_cat_{l}", tm=tm)
        if l < n_pool:
            dproj, dpbd, dscale, dmkv, d_mem_q[l], d_mem_k[l] = _mixer_pool_bwd(
                proj, dcat, mkv, pbd[l], row(pool_scale_full[l]), row(mem_q_norm[l]), row(mem_k_norm[l]),
                mem_probs[l], name=f"mixer_bwd_{l}", tm=tm)
            d_pool_w[l] = _diag_blocks(dpbd, len(POOL_WINDOWS))
            d_pool_scale[l] = dscale
        else:
            j = l - n_pool
            dproj, dk, dv, dmkv, d_q_norm[j], dsk, d_mem_q[l], d_mem_k[l] = _mixer_swa_bwd(
                proj, dcat, kn, vsh, mkv, row(q_norm[j]), probs[l], sink_probs[l], row(mem_q_norm[l]),
                row(mem_k_norm[l]), mem_probs[l], name=f"mixer_bwd_{l}")
            d_sinks[j] = dsk[0, :qh]
            dks.append(dk)
            dvs.append(dv)
        g1 = _mm_tn(xn, dproj, g1, rows_map(off_in, dq, tkq), tkq, d, name=f"dw_in_{l}")
        dh, dh_bf, d_norm_mix[l] = _mm_nt_normbwd(dproj, w_in_l[l], 1, h0, row(norm_mix[l]), dh1,
                                                  name=f"d_in_{l}", tm=tm)
        g2 = _mm_tn(memn, dmkv, g2, rows_map(off_mkv, dq, tkq), tkq, 2 * KVW, name=f"dw_mem_kv_{l}")
        _, _, d_mem_norm[l] = _mm_nt_normbwd(dmkv, w_mkv_l[l], 1, memx, row(mem_norm[l]), zeros_mem,
                                             name=f"d_mem_norm_{l}", tm=tm_mem)
        if l == n_pool:
            dkv, d_k_norm = _kv_bwd(kv, dks, dvs, row(k_norm), tm=tm)
            g2 = _mm_tn(hn_kv, dkv, g2, rows_map(off_kv, dq, tkq), tkq, 2 * KVW, name="dw_kv")
            dh, dh_bf, d_kv_norm = _mm_nt_normbwd(dkv, w_kv_g, 1, h0, row(kv_norm), dh, name="d_kv_in", tm=tm)
        bufs = (g1, g2, lax.empty((N_CHIPS, half1, d), F32), lax.empty((N_CHIPS, half2, 2 * KVW), F32))
        swapping = (l, _split_start(_swap_copies, 2, bufs, place, (N_CHIPS, rows1, d) if l > 0 else None,
                                    name=f"swap_start_{l}"))
        g1 = swapping[1][7] if l > 0 else None
    pending[0] = reduce_begin(*swapping, after=dh)
    grad_x = dh.reshape(x.shape)

    small_names = ["norm_mix", "pool_w", "pool_scale", "kv_norm", "k_norm", "q_norm", "sinks", "mem_norm",
                   "mem_q_norm", "mem_k_norm", "norm_mlp"]
    small_grads = {
        "norm_mix": jnp.concatenate(d_norm_mix), "pool_w": jnp.stack(d_pool_w),
        "pool_scale": jnp.concatenate(d_pool_scale), "kv_norm": d_kv_norm[0], "k_norm": d_k_norm[0],
        "q_norm": jnp.concatenate(d_q_norm), "sinks": jnp.stack(d_sinks), "mem_norm": jnp.concatenate(d_mem_norm),
        "mem_q_norm": jnp.concatenate(d_mem_q), "mem_k_norm": jnp.concatenate(d_mem_k),
        "norm_mlp": jnp.concatenate(d_norm_mlp)}
    width = d
    sg = _pack_small([small_grads[n] for n in small_names], width)
    sg = sg + pending[0][0][6][0, 0]
    sg = _allreduce_small(sg)
    reduced = dict(zip(small_names, _unpack_small(sg, [small_grads[n].shape for n in small_names])))
    psw = pool_scale.shape[1]
    reduced["pool_scale"] = lax.dynamic_slice_in_dim(reduced["pool_scale"], chip * psw, psw, axis=1)
    params = dict(norm_mix=(norm_mix, m_norm_mix, v_norm_mix), pool_w=(pool_w, m_pool_w, v_pool_w),
                  pool_scale=(pool_scale, m_pool_scale, v_pool_scale), kv_norm=(kv_norm, m_kv_norm, v_kv_norm),
                  k_norm=(k_norm, m_k_norm, v_k_norm), q_norm=(q_norm, m_q_norm, v_q_norm),
                  sinks=(sinks, m_sinks, v_sinks), mem_norm=(mem_norm, m_mem_norm, v_mem_norm),
                  mem_q_norm=(mem_q_norm, m_mem_q_norm, v_mem_q_norm),
                  mem_k_norm=(mem_k_norm, m_mem_k_norm, v_mem_k_norm), norm_mlp=(norm_mlp, m_norm_mlp, v_norm_mlp))
    shapes = [params[n][0].shape for n in small_names]
    packs = [_pack_small([reduced[n].reshape(params[n][0].shape) for n in small_names], width)]
    packs += [_pack_small([params[n][t] for n in small_names], width) for t in range(3)]
    res = _adamw(packs[0][None], 0, 0, packs[0].shape[0], packs[1], packs[2], packs[3], name="adamw_small", tm=8)
    small = {n: [] for n in small_names}
    for r in res:
        for n, a in zip(small_names, _unpack_small(r, shapes)):
            small[n].append(a)

    full1 = lax.empty((n_layers, 2, half1, d), F32)
    full2 = lax.empty((n_layers, 2, half2, 2 * KVW), F32)
    for l in reversed(range(n_layers)):
        exchange, own1, own2 = pending[l]
        _, _, x1, x2 = _split_wait(_rs_copies, exchange, res[0], name=f"reduce_wait_{l}")
        full1 = _sum_chips(own1, x1, full1, l, place, tm=_tile(half1, 256), name=f"sum_chips_a_{l}")
        full2 = _sum_chips(own2, x2, full2, l, place, tm=_tile(half2, 256), name=f"sum_chips_b_{l}")
    full1, full2 = _share_with_sibling(full1, full2)
    full1 = full1.reshape(n_layers, rows1, d)
    full2 = full2.reshape(n_layers, rows2, 2 * KVW)

    big = {}
    for name, arr, layer0, off, per, w_, m_, v_ in (
            ("w_down", full1, 0, off_down, d, w_down, m_w_down, v_w_down),
            ("w_up", full1, 0, off_up, d, w_up, m_w_up, v_w_up),
            ("w_in", full1, 0, off_in, dq, w_in, m_w_in, v_w_in),
            ("w_out", full1, 0, off_out, dq, w_out, m_w_out, v_w_out),
            ("w_mem_kv", full2, 0, off_mkv, dq, w_mem_kv, m_w_mem_kv, v_w_mem_kv),
            ("w_kv", full2, n_pool, off_kv, dq, w_kv, m_w_kv, v_w_kv)):
        cols = arr.shape[2]
        upd = _adamw(arr, layer0, off, per, w_.reshape(-1, cols), m_.reshape(-1, cols), v_.reshape(-1, cols),
                     name=f"adamw_{name}", tm=min(256, dq))
        big[name] = [r.reshape(w_.shape) for r in upd]

    order = ["norm_mix", "w_in", "pool_w", "pool_scale", "kv_norm", "w_kv", "k_norm", "q_norm", "sinks", "mem_norm",
             "w_mem_kv", "mem_q_norm", "mem_k_norm", "w_out", "norm_mlp", "w_up", "w_down"]
    out = {**big, **small}
    return (loss, grad_x, *[out[n][0] for n in order], *[out[n][1] for n in order],
            *[out[n][2] for n in order], *[out[n][3] for n in order])
```

```python
import functools

import jax
import jax.numpy as jnp
from jax import lax
from jax.experimental import pallas as pl
from jax.experimental.pallas import tpu as pltpu

F32, BF = jnp.float32, jnp.bfloat16
SDS = jax.ShapeDtypeStruct
MESH = pl.DeviceIdType.MESH
ANY = pl.BlockSpec(memory_space=pl.ANY)
HBM = pl.BlockSpec(memory_space=pltpu.HBM)
SEM = pl.BlockSpec(memory_space=pltpu.SEMAPHORE)
VMEM_WHOLE = pl.BlockSpec(memory_space=pltpu.VMEM)
SIDE_EFFECT = pltpu.SideEffectType.DATAFLOW_SIDE_EFFECTING


def _in_hbm(a):
    return pltpu.with_memory_space_constraint(a, pltpu.HBM)


EPS = 1e-6
HEAD = 64
KV_HEADS = 4
KVW = KV_HEADS * HEAD
WINDOW = 128
POOL_WINDOWS = (2, 4, 8, 16)
HALO = 16
QK_SCALE = HEAD ** -0.5
NEG = float(jnp.finfo(jnp.float32).min)
N_CHIPS = 4
LANE = 128

ADAM_LR, ADAM_B1, ADAM_B2, ADAM_EPS, ADAM_WD, ADAM_STEP = 0.001, 0.9, 0.999, 1e-08, 0.01, 10

VMEM_LIMIT_MB = 56


def _call(body, name, grid, in_specs, out_specs, out_shape, *, scratch=(), semantics=None, aliases=None,
          prefetch=0):
    params = pltpu.CompilerParams(dimension_semantics=semantics, vmem_limit_bytes=VMEM_LIMIT_MB << 20)
    if prefetch:
        spec = pltpu.PrefetchScalarGridSpec(num_scalar_prefetch=prefetch, grid=grid, in_specs=in_specs,
                                            out_specs=out_specs, scratch_shapes=list(scratch))
        return pl.pallas_call(body, name=name, grid_spec=spec, out_shape=out_shape,
                              input_output_aliases=aliases or {}, compiler_params=params)
    return pl.pallas_call(body, name=name, grid=grid, in_specs=in_specs, out_specs=out_specs, out_shape=out_shape,
                          scratch_shapes=list(scratch), input_output_aliases=aliases or {}, compiler_params=params)


def _tile(n, pref):
    return max(t for t in range(8, min(n, pref) + 1, 8) if n % t == 0)


def _dot(a, b):
    return jnp.dot(a, b, preferred_element_type=F32)


def _dot_nt(a, b):
    return lax.dot_general(a, b, (((1,), (1,)), ((), ())), preferred_element_type=F32)


def _dot_tn(a, b):
    return lax.dot_general(a, b, (((0,), (0,)), ((), ())), preferred_element_type=F32)


def _rms(x):
    r = lax.rsqrt(jnp.mean(x * x, axis=-1, keepdims=True) + EPS)
    return x * r, r


def _rms_bwd(dy, xh, r, g):
    dg = jnp.sum(dy * xh, axis=0, keepdims=True)
    dyg = dy * g
    dx = r * (dyg - xh * jnp.mean(dyg * xh, axis=-1, keepdims=True))
    return dx, dg


def _norm_mm(h, g, w, nj, tn, *, act, name, tm):
    w_arr, w_block, w_imap = w
    rows, d = h.shape

    def body(h_ref, g_ref, w_ref, y_ref, xn_ref):
        @pl.when(pl.program_id(1) == 0)
        def _():
            xh, _ = _rms(h_ref[...])
            xn_ref[...] = (xh * g_ref[...]).astype(BF)

        w_j = w_ref[pl.program_id(1)] if (nj > 1 and w_block[0] == nj) else w_ref[...].reshape(d, tn)
        u = _dot(xn_ref[...], w_j)
        if act:
            a = jnp.maximum(u, 0.0)
            y_ref[...] = (a * a).astype(BF)
        else:
            y_ref[...] = u

    return _call(
        body, name, (rows // tm, nj),
        [pl.BlockSpec((tm, d), lambda i, j: (i, 0)), pl.BlockSpec((1, d), lambda i, j: (0, 0)),
         pl.BlockSpec(w_block, lambda i, j: w_imap(j))],
        [pl.BlockSpec((tm, tn), lambda i, j: (i, j)), pl.BlockSpec((tm, d), lambda i, j: (i, 0))],
        [SDS((rows, nj * tn), BF if act else F32), SDS((rows, d), BF)],
        semantics=("parallel", "arbitrary"))(h, g, w_arr)


def _mm_res(res, a, w, *, name, tm, after=None):
    w_arr, w_block, w_imap = w
    rows, k = a.shape
    n = res.shape[1]

    def body(res_ref, a_ref, w_ref, *rest):
        rest[-1][...] = res_ref[...] + _dot(a_ref[...], w_ref[...].reshape(k, n))

    extra = [] if after is None else [after]
    return _call(
        body, name, (rows // tm,),
        [pl.BlockSpec((tm, n), lambda i: (i, 0)), pl.BlockSpec((tm, k), lambda i: (i, 0)),
         pl.BlockSpec(w_block, lambda i: w_imap(0))] + [ANY] * len(extra),
        pl.BlockSpec((tm, n), lambda i: (i, 0)), SDS((rows, n), F32), semantics=("parallel",))(res, a, w_arr, *extra)


def _mm_nt(dy, w, k, *, name, tm):
    w_arr, w_block, w_imap = w
    rows, n = dy.shape

    def body(dy_ref, w_ref, o_ref):
        o_ref[...] = _dot_nt(dy_ref[...], w_ref[...].reshape(k, n))

    return _call(
        body, name, (rows // tm,),
        [pl.BlockSpec((tm, n), lambda i: (i, 0)), pl.BlockSpec(w_block, lambda i: w_imap(0))],
        pl.BlockSpec((tm, k), lambda i: (i, 0)), SDS((rows, k), F32), semantics=("parallel",))(dy, w_arr)


def _mm_nt_relu2(dh, hh, w, nj, *, name, tm):
    w_arr, w_block, w_imap = w
    rows, d = dh.shape
    tk = hh.shape[1] // nj

    def body(dh_ref, hh_ref, w_ref, o_ref):
        w_j = w_ref[pl.program_id(1)] if w_block[0] == nj else w_ref[...].reshape(tk, d)
        dhh = _dot_nt(dh_ref[...], w_j)
        o_ref[...] = (dhh * (2.0 * jnp.sqrt(hh_ref[...].astype(F32)))).astype(BF)

    return _call(
        body, name, (rows // tm, nj),
        [pl.BlockSpec((tm, d), lambda i, j: (i, 0)), pl.BlockSpec((tm, tk), lambda i, j: (i, j)),
         pl.BlockSpec(w_block, lambda i, j: w_imap(j))],
        pl.BlockSpec((tm, tk), lambda i, j: (i, j)), SDS((rows, nj * tk), BF),
        semantics=("parallel", "parallel"))(dh, hh, w_arr)


def _mm_nt_normbwd(dy, w, nsplit, h, g, dres, *, name, tm):
    w_arr, w_block, w_imap = w
    rows, n = dy.shape
    d = h.shape[1]
    ns = n // nsplit

    def body(dy_ref, w_ref, h_ref, g_ref, dres_ref, o_ref, obf_ref, dg_ref):
        if nsplit == 1:
            dxn = _dot_nt(dy_ref[...].astype(BF), w_ref[...].reshape(d, n))
        else:
            dxn = _dot_nt(dy_ref[:, 0:ns].astype(BF), w_ref[0])
            for s in range(1, nsplit):
                dxn += _dot_nt(dy_ref[:, s * ns:(s + 1) * ns].astype(BF), w_ref[s])
        xh, r = _rms(h_ref[...])
        dx, dg = _rms_bwd(dxn, xh, r, g_ref[...])
        out = dres_ref[...] + dx
        o_ref[...] = out
        obf_ref[...] = out.astype(BF)

        @pl.when(pl.program_id(0) == 0)
        def _():
            dg_ref[...] = jnp.zeros_like(dg_ref)

        dg_ref[...] += dg

    row = lambda i: (i, 0)
    return _call(
        body, name, (rows // tm,),
        [pl.BlockSpec((tm, n), row), pl.BlockSpec(w_block, lambda i: w_imap(0)), pl.BlockSpec((tm, d), row),
         pl.BlockSpec((1, d), lambda i: (0, 0)), pl.BlockSpec((tm, d), row)],
        [pl.BlockSpec((tm, d), row), pl.BlockSpec((tm, d), row), pl.BlockSpec((1, d), lambda i: (0, 0))],
        [SDS((rows, d), F32), SDS((rows, d), BF), SDS((1, d), F32)],
        semantics=("arbitrary",))(dy, w_arr, h, g, dres)


def _mm_tn(x, dy, packed, out_imap, tk, tn, *, name):
    s_len, k = x.shape
    n = dy.shape[1]

    def body(x_ref, dy_ref, _, o_ref):
        o_ref[0] = _dot_tn(x_ref[...], dy_ref[...].astype(BF))

    return _call(
        body, name, (k // tk, n // tn),
        [pl.BlockSpec((s_len, tk), lambda i, j: (0, i)), pl.BlockSpec((s_len, tn), lambda i, j: (0, j)), ANY],
        pl.BlockSpec((1, tk, tn), out_imap), SDS(packed.shape, packed.dtype),
        semantics=("parallel", "parallel"), aliases={2: 0})(x, dy, packed)


def _loss_head(y, tgt, *, tm):
    rows, d = y.shape

    def body(y_ref, t_ref, dh_ref, dhbf_ref, loss_ref):
        err = y_ref[...] - t_ref[...]
        dh = err * (1.0 / d)
        dh_ref[...] = dh
        dhbf_ref[...] = dh.astype(BF)

        @pl.when(pl.program_id(0) == 0)
        def _():
            loss_ref[...] = jnp.zeros_like(loss_ref)

        loss_ref[...] += 0.5 * jnp.sum(jnp.mean(err * err, axis=-1, keepdims=True), axis=0, keepdims=True)

    row = lambda i: (i, 0)
    return _call(
        body, "loss_head", (rows // tm,), [pl.BlockSpec((tm, d), row), pl.BlockSpec((tm, d), row)],
        [pl.BlockSpec((tm, d), row), pl.BlockSpec((tm, d), row), pl.BlockSpec((1, 1), lambda i: (0, 0))],
        [SDS((rows, d), F32), SDS((rows, d), BF), SDS((1, 1), F32)], semantics=("arbitrary",))(y, tgt)


def _hs(h):
    return slice(HEAD * h, HEAD * (h + 1))


def _dot_hi(a, b):
    return jnp.dot(a, b, preferred_element_type=F32, precision=lax.Precision.HIGHEST)


def _head_mats(width):
    shift = HEAD.bit_length() - 1
    c = jnp.right_shift(lax.broadcasted_iota(jnp.int32, (width, LANE), 0), shift)
    h = lax.broadcasted_iota(jnp.int32, (width, LANE), 1)
    ht = lax.broadcasted_iota(jnp.int32, (LANE, width), 0)
    ct = jnp.right_shift(lax.broadcasted_iota(jnp.int32, (LANE, width), 1), shift)
    return jnp.where(c == h, 1.0 / HEAD, 0.0), jnp.where(ht == ct, 1.0, 0.0)


def _head_mean(v, mats):
    return _dot_hi(_dot_hi(v, mats[0]), mats[1])


def _rms_heads(x, mats):
    rf = _dot_hi(lax.rsqrt(_dot_hi(x * x, mats[0]) + EPS), mats[1])
    return x * rf, rf


def _tile_heads(g, width):
    return jnp.concatenate([g] * (width // HEAD), axis=-1)


def _rms_heads_bwd(dy, xh, rf, g, mats):
    width = dy.shape[1]
    col = jnp.sum(dy * xh, axis=0, keepdims=True)
    dg = col[:, _hs(0)]
    for h in range(1, width // HEAD):
        dg = dg + col[:, _hs(h)]
    dyg = dy * _tile_heads(g, width)
    return rf * (dyg - xh * _head_mean(dyg * xh, mats)), dg


def _softmax_rows(s):
    e = jnp.exp(s - jnp.max(s, axis=-1, keepdims=True))
    return e * (1.0 / jnp.sum(e, axis=-1, keepdims=True))


def _scaled_bf16(qn):
    return (qn * QK_SCALE).astype(BF)


def _mem_fwd(mq, mk, mv, gq):
    outs, probs = [], []
    xh, _ = _rms_heads(mq, _head_mats(KVW))
    qn = _scaled_bf16(xh * _tile_heads(gq, KVW))
    for h in range(KV_HEADS):
        p = _softmax_rows(_dot_nt(qn[:, _hs(h)], mk[:, _hs(h)])).astype(BF)
        probs.append(p)
        outs.append(_dot(p, mv[:, _hs(h)]))
    return jnp.concatenate(outs, axis=-1), jnp.concatenate(probs, axis=-1)


def _mem_bwd(mq, do, mk, mv, gq, probs):
    dqs, dks, dvs = [], [], []
    mlen = mk.shape[0]
    mats = _head_mats(KVW)
    xh, rf = _rms_heads(mq, mats)
    qn = _scaled_bf16(xh * _tile_heads(gq, KVW))
    for h in range(KV_HEADS):
        p_bf = probs[:, h * mlen:(h + 1) * mlen]
        p = p_bf.astype(F32)
        doh = do[:, _hs(h)].astype(BF)
        dp = _dot_nt(doh, mv[:, _hs(h)])
        ds = (p * (dp - jnp.sum(p * dp, axis=-1, keepdims=True))).astype(BF)
        dqs.append(_dot(ds, mk[:, _hs(h)]))
        dks.append(_dot_tn(ds, qn[:, _hs(h)]))
        dvs.append(_dot_tn(p_bf, doh))
    cat = lambda xs: jnp.concatenate(xs, axis=-1)
    dq, dgq = _rms_heads_bwd(cat(dqs) * QK_SCALE, xh, rf, gq, mats)
    return dq, cat(dks), cat(dvs), dgq


def _mem_kv(mkv, gk):
    xh, _ = _rms_heads(mkv[:, :KVW], _head_mats(KVW))
    return (xh * _tile_heads(gk, KVW)).astype(BF), mkv[:, KVW:].astype(BF)


def _mem_kv_bwd(mkv, dmk, dmv, gk):
    mats = _head_mats(KVW)
    xh, rf = _rms_heads(mkv[:, :KVW], mats)
    dx, dgk = _rms_heads_bwd(dmk, xh, rf, gk, mats)
    return jnp.concatenate([dx, dmv], axis=-1), dgk


def _pool_select(col, gd, a2, a4, a8, a16):
    return jnp.where(col < gd, a2, jnp.where(col < 2 * gd, a4, jnp.where(col < 3 * gd, a8, a16)))


def _pool_count(t0, shape, gd):
    col = lax.broadcasted_iota(jnp.int32, shape, 1)
    t = t0 + lax.broadcasted_iota(jnp.int32, shape, 0)
    win = _pool_select(col, gd, *POOL_WINDOWS)
    return jnp.minimum(t + 1, win).astype(F32)


def _pool_diff(u, halo, t0, gd):
    c = jnp.concatenate([halo, u], axis=0)
    s2 = c + pltpu.roll(c, 1, 0)
    s4 = s2 + pltpu.roll(s2, 2, 0)
    s8 = s4 + pltpu.roll(s4, 4, 0)
    s16 = s8 + pltpu.roll(s8, 8, 0)
    col = lax.broadcasted_iota(jnp.int32, c.shape, 1)
    ws = _pool_select(col, gd, s2, s4, s8, s16)[HALO:]
    return ws / _pool_count(t0, u.shape, gd) - u


def _pool_diff_bwd(dd, dd_halo, t0, gd):
    t = dd.shape[0]
    z = jnp.concatenate([dd / _pool_count(t0, dd.shape, gd), dd_halo / _pool_count(t0 + t, dd_halo.shape, gd)], axis=0)
    n = z.shape[0]
    f2 = z + pltpu.roll(z, n - 1, 0)
    f4 = f2 + pltpu.roll(f2, n - 2, 0)
    f8 = f4 + pltpu.roll(f4, n - 4, 0)
    f16 = f8 + pltpu.roll(f8, n - 8, 0)
    col = lax.broadcasted_iota(jnp.int32, z.shape, 1)
    return _pool_select(col, gd, f2, f4, f8, f16)[:t] - dd


def _swa_bias(n):
    qi = lax.broadcasted_iota(jnp.int32, (WINDOW, 2 * WINDOW), 0)
    kj = lax.broadcasted_iota(jnp.int32, (WINDOW, 2 * WINDOW), 1)
    dist = qi + WINDOW - kj
    valid = (dist >= 0) & (dist < WINDOW) & ((kj >= WINDOW) | (n > 0))
    return dist.astype(F32), valid


def _slopes(qh):
    return [2.0 ** (-8.0 * (h + 1) / qh) for h in range(qh)]


def _swa_probs(qn, kk, dist, valid, slope, sink):
    s = _dot_nt(qn, kk) - slope * dist
    s = jnp.where(valid, s, NEG)
    m = jnp.maximum(jnp.max(s, axis=-1, keepdims=True), sink)
    e = jnp.exp(s - m)
    es = jnp.exp(sink - m)
    z = jnp.sum(e, axis=-1, keepdims=True) + es
    inv = 1.0 / z
    return e * inv, es * inv


def _stack_heads(a, kh, grp):
    return jnp.concatenate([a[:, _hs(h)] for h in range(kh * grp, (kh + 1) * grp)], axis=0)


def _swa_group(kh, grp, n, qh, sinks):
    heads = range(kh * grp, (kh + 1) * grp)
    dist, valid = _swa_bias(n)
    slopes = _slopes(qh)
    rows = lambda vals: jnp.concatenate([jnp.broadcast_to(v, (WINDOW, 1)) for v in vals], axis=0)
    slope = rows([jnp.full((1, 1), slopes[h], F32) for h in heads])
    sink = rows([sinks[:, h:h + 1] for h in heads])
    return slope, sink, jnp.concatenate([dist] * grp, axis=0), jnp.concatenate([valid] * grp, axis=0)


def _swa_fwd(q, kk, vv, gq, sinks, n, qh):
    grp = qh // KV_HEADS
    lane = lax.broadcasted_iota(jnp.int32, (WINDOW, LANE), 1)
    outs, probs, sink_probs = [], [], jnp.zeros((WINDOW, LANE), F32)
    xh, _ = _rms_heads(q, _head_mats(q.shape[1]))
    qn = _scaled_bf16(xh * _tile_heads(gq, q.shape[1]))
    for kh in range(KV_HEADS):
        slope, sink, dist, valid = _swa_group(kh, grp, n, qh, sinks)
        p, ps = _swa_probs(_stack_heads(qn, kh, grp), kk[:, _hs(kh)], dist, valid, slope, sink)
        p = p.astype(BF)
        probs.append(p)
        o = _dot(p, vv[:, _hs(kh)])
        for g in range(grp):
            outs.append(o[g * WINDOW:(g + 1) * WINDOW])
            sink_probs = jnp.where(lane == kh * grp + g, ps[g * WINDOW:(g + 1) * WINDOW], sink_probs)
    return jnp.concatenate(outs, axis=-1), probs, sink_probs


def _swa_bwd(q, do, kk, vv, gq, probs, sink_probs, qh):
    grp = qh // KV_HEADS
    lane = lax.broadcasted_iota(jnp.int32, (1, LANE), 1)
    dqs, dks, dvs, dsk = [], [], [], jnp.zeros((1, LANE), F32)
    mats = _head_mats(q.shape[1])
    xh, rf = _rms_heads(q, mats)
    qn = _scaled_bf16(xh * _tile_heads(gq, q.shape[1]))
    do = do.astype(BF)
    for kh in range(KV_HEADS):
        heads = range(kh * grp, (kh + 1) * grp)
        p_bf = probs[kh]
        p = p_bf.astype(F32)
        ps = jnp.concatenate([sink_probs[:, h:h + 1] for h in heads], axis=0)
        dos = _stack_heads(do, kh, grp)
        dp = _dot_nt(dos, vv[:, _hs(kh)])
        delta = jnp.sum(p * dp, axis=-1, keepdims=True)
        ds = (p * (dp - delta)).astype(BF)
        dsink = ps * delta
        for g in range(grp):
            part = -jnp.sum(dsink[g * WINDOW:(g + 1) * WINDOW], axis=0, keepdims=True)
            dsk = dsk + jnp.where(lane == kh * grp + g, part, 0.0)
        dq = _dot(ds, kk[:, _hs(kh)])
        dqs += [dq[g * WINDOW:(g + 1) * WINDOW] for g in range(grp)]
        dks.append(_dot_tn(ds, _stack_heads(qn, kh, grp)))
        dvs.append(_dot_tn(p_bf, dos))
    cat = lambda xs: jnp.concatenate(xs, axis=-1)
    dq, dgq = _rms_heads_bwd(cat(dqs) * QK_SCALE, xh, rf, gq, mats)
    return dq, cat(dks), cat(dvs), dgq, dsk


def _mixer_pool_fwd(proj, mkv, pbd, scale, gq, gk, *, name, tm):
    s_len, d = proj.shape
    main = d - KVW
    gd = main // len(POOL_WINDOWS)
    mlen = mkv.shape[0]
    hb = tm // HALO

    def body(u_ref, halo_ref, mq_ref, mkv_ref, pbd_ref, scale_ref, gq_ref, gk_ref, o_ref, mp_ref, mk_s, mv_s):
        i = pl.program_id(0)

        @pl.when(i == 0)
        def _():
            mk, mv = _mem_kv(mkv_ref[...], gk_ref[...])
            mk_s[...] = mk
            mv_s[...] = mv

        halo = jnp.where(i > 0, halo_ref[...], 0.0)
        dif = _pool_diff(u_ref[...], halo, i * tm, gd)
        mixed = _dot(dif.astype(BF), pbd_ref[...]) * scale_ref[...]
        mem, mp_ref[...] = _mem_fwd(mq_ref[...], mk_s[...], mv_s[...], gq_ref[...])
        o_ref[...] = jnp.concatenate([mixed, mem], axis=-1).astype(BF)

    full = lambda shape: pl.BlockSpec(shape, lambda i: (0,) * len(shape))
    return _call(
        body, name, (s_len // tm,),
        [pl.BlockSpec((tm, main), lambda i: (i, 0)),
         pl.BlockSpec((HALO, main), lambda i: (jnp.maximum(i * hb - 1, 0), 0)),
         pl.BlockSpec((tm, KVW), lambda i: (i, main // KVW)),
         full((mlen, 2 * KVW)), full((main, main)), full((1, main)), full((1, HEAD)), full((1, HEAD))],
        [pl.BlockSpec((tm, d), lambda i: (i, 0)), pl.BlockSpec((tm, KV_HEADS * mlen), lambda i: (i, 0))],
        [SDS((s_len, d), BF), SDS((s_len, KV_HEADS * mlen), BF)],
        scratch=[pltpu.VMEM((mlen, KVW), BF), pltpu.VMEM((mlen, KVW), BF)],
        semantics=("arbitrary",))(proj, proj, proj, mkv, pbd, scale, gq, gk)


def _mixer_pool_bwd(proj, dcat, mkv, pbd, scale, gq, gk, mem_probs, *, name, tm):
    s_len, d = proj.shape
    main = d - KVW
    gd = main // len(POOL_WINDOWS)
    mlen = mkv.shape[0]
    hb = tm // HALO
    nt = s_len // tm
    last_halo = s_len // HALO - 1

    def body(u_ref, halo_ref, mq_ref, do_ref, donext_ref, dom_ref, mkv_ref, pbd_ref, scale_ref, gq_ref, gk_ref, mp_ref,
             dproj_ref, dpbd_ref, dscale_ref, dmkv_ref, dgq_ref, dgk_ref, mk_s, mv_s, dmk_s, dmv_s):
        i = pl.program_id(0)

        @pl.when(i == 0)
        def _():
            mk, mv = _mem_kv(mkv_ref[...], gk_ref[...])
            mk_s[...] = mk
            mv_s[...] = mv
            dmk_s[...] = jnp.zeros_like(dmk_s)
            dmv_s[...] = jnp.zeros_like(dmv_s)
            dpbd_ref[...] = jnp.zeros_like(dpbd_ref)
            dscale_ref[...] = jnp.zeros_like(dscale_ref)
            dgq_ref[...] = jnp.zeros_like(dgq_ref)

        pbd = pbd_ref[...]
        scale = scale_ref[...]
        halo = jnp.where(i > 0, halo_ref[...], 0.0)
        dif = _pool_diff(u_ref[...], halo, i * tm, gd).astype(BF)
        do = do_ref[...]
        dscale_ref[...] += jnp.sum(do * _dot(dif, pbd), axis=0, keepdims=True)
        dmixed = (do * scale).astype(BF)
        dpbd_ref[...] += _dot_tn(dif, dmixed)
        dd = _dot_nt(dmixed, pbd)
        donext = jnp.where(i < nt - 1, donext_ref[...], 0.0)
        dd_halo = _dot_nt((donext * scale).astype(BF), pbd)
        du = _pool_diff_bwd(dd, dd_halo, i * tm, gd)

        dmq, dmk, dmv, dgq = _mem_bwd(mq_ref[...], dom_ref[...], mk_s[...], mv_s[...], gq_ref[...], mp_ref[...])
        dmk_s[...] += dmk
        dmv_s[...] += dmv
        dgq_ref[...] += dgq
        dproj_ref[...] = jnp.concatenate([du, dmq], axis=-1).astype(BF)

        @pl.when(i == nt - 1)
        def _():
            dmkv, dgk = _mem_kv_bwd(mkv_ref[...], dmk_s[...], dmv_s[...], gk_ref[...])
            dmkv_ref[...] = dmkv
            dgk_ref[...] = dgk

    full = lambda shape: pl.BlockSpec(shape, lambda i: (0,) * len(shape))
    return _call(
        body, name, (nt,),
        [pl.BlockSpec((tm, main), lambda i: (i, 0)),
         pl.BlockSpec((HALO, main), lambda i: (jnp.maximum(i * hb - 1, 0), 0)),
         pl.BlockSpec((tm, KVW), lambda i: (i, main // KVW)),
         pl.BlockSpec((tm, main), lambda i: (i, 0)),
         pl.BlockSpec((HALO, main), lambda i: (jnp.minimum((i + 1) * hb, last_halo), 0)),
         pl.BlockSpec((tm, KVW), lambda i: (i, main // KVW)),
         full((mlen, 2 * KVW)), full((main, main)), full((1, main)), full((1, HEAD)), full((1, HEAD)),
         pl.BlockSpec((tm, KV_HEADS * mlen), lambda i: (i, 0))],
        [pl.BlockSpec((tm, d), lambda i: (i, 0)), full((main, main)), full((1, main)), full((mlen, 2 * KVW)),
         full((1, HEAD)), full((1, HEAD))],
        [SDS((s_len, d), BF), SDS((main, main), F32), SDS((1, main), F32), SDS((mlen, 2 * KVW), F32),
         SDS((1, HEAD), F32), SDS((1, HEAD), F32)],
        scratch=[pltpu.VMEM((mlen, KVW), BF), pltpu.VMEM((mlen, KVW), BF), pltpu.VMEM((mlen, KVW), F32),
                 pltpu.VMEM((mlen, KVW), F32)],
        semantics=("arbitrary",))(proj, proj, proj, dcat, dcat, dcat, mkv, pbd, scale, gq, gk, mem_probs)


def _mixer_swa_fwd(proj, kn, v, mkv, gqs, sinks, gq, gk, *, name):
    s_len, d = proj.shape
    main = d - KVW
    qh = main // HEAD
    mlen = mkv.shape[0]
    tm = WINDOW
    prow = qh // KV_HEADS * tm

    def body(q_ref, mq_ref, kp_ref, kc_ref, vp_ref, vc_ref, mkv_ref, gqs_ref, sinks_ref, gq_ref, gk_ref, o_ref,
             p_ref, ps_ref, mp_ref, mk_s, mv_s):
        n = pl.program_id(0)

        @pl.when(n == 0)
        def _():
            mk, mv = _mem_kv(mkv_ref[...], gk_ref[...])
            mk_s[...] = mk
            mv_s[...] = mv

        kk = jnp.concatenate([kp_ref[...], kc_ref[...]], axis=0)
        vv = jnp.concatenate([vp_ref[...], vc_ref[...]], axis=0)
        att, probs, sink_probs = _swa_fwd(q_ref[...], kk, vv, gqs_ref[...], sinks_ref[...], n, qh)
        for kh in range(KV_HEADS):
            p_ref[0, kh] = probs[kh]
        ps_ref[...] = sink_probs
        mem, mp_ref[...] = _mem_fwd(mq_ref[...], mk_s[...], mv_s[...], gq_ref[...])
        o_ref[...] = jnp.concatenate([att, mem], axis=-1).astype(BF)

    full = lambda shape: pl.BlockSpec(shape, lambda i: (0,) * len(shape))
    prev = lambda i: (jnp.maximum(i - 1, 0), 0)
    cur = lambda i: (i, 0)
    return _call(
        body, name, (s_len // tm,),
        [pl.BlockSpec((tm, main), cur), pl.BlockSpec((tm, KVW), lambda i: (i, main // KVW)),
         pl.BlockSpec((tm, KVW), prev), pl.BlockSpec((tm, KVW), cur),
         pl.BlockSpec((tm, KVW), prev), pl.BlockSpec((tm, KVW), cur),
         full((mlen, 2 * KVW)), full((1, HEAD)), full((1, LANE)), full((1, HEAD)), full((1, HEAD))],
        [pl.BlockSpec((tm, d), cur), pl.BlockSpec((1, KV_HEADS, prow, 2 * tm), lambda i: (i, 0, 0, 0)),
         pl.BlockSpec((tm, LANE), cur), pl.BlockSpec((tm, KV_HEADS * mlen), cur)],
        [SDS((s_len, d), BF), SDS((s_len // tm, KV_HEADS, prow, 2 * tm), BF), SDS((s_len, LANE), F32),
         SDS((s_len, KV_HEADS * mlen), BF)],
        scratch=[pltpu.VMEM((mlen, KVW), BF), pltpu.VMEM((mlen, KVW), BF)],
        semantics=("arbitrary",))(proj, proj, kn, kn, v, v, mkv, gqs, sinks, gq, gk)


def _mixer_swa_bwd(proj, dcat, kn, v, mkv, gqs, probs, sink_probs, gq, gk, mem_probs, *, name):
    s_len, d = proj.shape
    main = d - KVW
    qh = main // HEAD
    mlen = mkv.shape[0]
    tm = WINDOW
    nt = s_len // tm
    prow = qh // KV_HEADS * tm

    def body(q_ref, mq_ref, do_ref, dom_ref, kp_ref, kc_ref, vp_ref, vc_ref, mkv_ref, gqs_ref, p_ref, ps_ref,
             gq_ref, gk_ref, mp_ref, dproj_ref, dk_ref, dv_ref, dmkv_ref, dgqs_ref, dsinks_ref, dgq_ref, dgk_ref,
             mk_s, mv_s, dmk_s, dmv_s):
        n = pl.program_id(0)

        @pl.when(n == 0)
        def _():
            mk, mv = _mem_kv(mkv_ref[...], gk_ref[...])
            mk_s[...] = mk
            mv_s[...] = mv
            dmk_s[...] = jnp.zeros_like(dmk_s)
            dmv_s[...] = jnp.zeros_like(dmv_s)
            dk_ref[...] = jnp.zeros_like(dk_ref)
            dv_ref[...] = jnp.zeros_like(dv_ref)
            dgqs_ref[...] = jnp.zeros_like(dgqs_ref)
            dsinks_ref[...] = jnp.zeros_like(dsinks_ref)
            dgq_ref[...] = jnp.zeros_like(dgq_ref)

        kk = jnp.concatenate([kp_ref[...], kc_ref[...]], axis=0)
        vv = jnp.concatenate([vp_ref[...], vc_ref[...]], axis=0)
        dq, dkk, dvv, dgqs, dsk = _swa_bwd(q_ref[...], do_ref[...], kk, vv, gqs_ref[...],
                                           [p_ref[0, kh] for kh in range(KV_HEADS)], ps_ref[...], qh)
        prev = pl.ds(pl.multiple_of(jnp.maximum(n - 1, 0) * tm, tm), tm)
        own = pl.ds(pl.multiple_of(n * tm, tm), tm)
        dk_ref[prev, :] += dkk[:tm]
        dk_ref[own, :] += dkk[tm:]
        dv_ref[prev, :] += dvv[:tm]
        dv_ref[own, :] += dvv[tm:]
        dgqs_ref[...] += dgqs
        dsinks_ref[...] += dsk

        dmq, dmk, dmv, dgq = _mem_bwd(mq_ref[...], dom_ref[...], mk_s[...], mv_s[...], gq_ref[...], mp_ref[...])
        dmk_s[...] += dmk
        dmv_s[...] += dmv
        dgq_ref[...] += dgq
        dproj_ref[...] = jnp.concatenate([dq, dmq], axis=-1).astype(BF)

        @pl.when(n == nt - 1)
        def _():
            dmkv, dgk = _mem_kv_bwd(mkv_ref[...], dmk_s[...], dmv_s[...], gk_ref[...])
            dmkv_ref[...] = dmkv
            dgk_ref[...] = dgk

    full = lambda shape: pl.BlockSpec(shape, lambda i: (0,) * len(shape))
    prev_b = lambda i: (jnp.maximum(i - 1, 0), 0)
    cur = lambda i: (i, 0)
    memcol = lambda i: (i, main // KVW)
    return _call(
        body, name, (nt,),
        [pl.BlockSpec((tm, main), cur), pl.BlockSpec((tm, KVW), memcol),
         pl.BlockSpec((tm, main), cur), pl.BlockSpec((tm, KVW), memcol),
         pl.BlockSpec((tm, KVW), prev_b), pl.BlockSpec((tm, KVW), cur),
         pl.BlockSpec((tm, KVW), prev_b), pl.BlockSpec((tm, KVW), cur),
         full((mlen, 2 * KVW)), full((1, HEAD)), pl.BlockSpec((1, KV_HEADS, prow, 2 * tm), lambda i: (i, 0, 0, 0)),
         pl.BlockSpec((tm, LANE), cur), full((1, HEAD)), full((1, HEAD)),
         pl.BlockSpec((tm, KV_HEADS * mlen), cur)],
        [pl.BlockSpec((tm, d), cur), full((s_len, KVW)), full((s_len, KVW)), full((mlen, 2 * KVW)),
         full((1, HEAD)), full((1, LANE)), full((1, HEAD)), full((1, HEAD))],
        [SDS((s_len, d), BF), SDS((s_len, KVW), F32), SDS((s_len, KVW), F32), SDS((mlen, 2 * KVW), F32),
         SDS((1, HEAD), F32), SDS((1, LANE), F32), SDS((1, HEAD), F32), SDS((1, HEAD), F32)],
        scratch=[pltpu.VMEM((mlen, KVW), BF), pltpu.VMEM((mlen, KVW), BF), pltpu.VMEM((mlen, KVW), F32),
                 pltpu.VMEM((mlen, KVW), F32)],
        semantics=("arbitrary",))(proj, proj, dcat, dcat, kn, kn, v, v, mkv, gqs, probs, sink_probs, gq, gk, mem_probs)


def _kv_prep(kv, gk, *, tm):
    s_len = kv.shape[0]

    def body(kv_ref, gk_ref, k_ref, v_ref):
        k, v = _mem_kv(kv_ref[...], gk_ref[...])
        k_ref[...] = k
        v_ref[...] = v

    row = lambda i: (i, 0)
    return _call(
        body, "kv_prep", (s_len // tm,),
        [pl.BlockSpec((tm, 2 * KVW), row), pl.BlockSpec((1, HEAD), lambda i: (0, 0))],
        [pl.BlockSpec((tm, KVW), row), pl.BlockSpec((tm, KVW), row)],
        [SDS((s_len, KVW), BF), SDS((s_len, KVW), BF)], semantics=("parallel",))(kv, gk)


def _kv_bwd(kv, dks, dvs, gk, *, tm):
    s_len = kv.shape[0]
    nl = len(dks)

    def body(*refs):
        kv_ref, gk_ref = refs[0], refs[1]
        dk_refs, dv_refs = refs[2:2 + nl], refs[2 + nl:2 + 2 * nl]
        dkv_ref, dgk_ref = refs[2 + 2 * nl], refs[3 + 2 * nl]
        dk, dv = dk_refs[0][...], dv_refs[0][...]
        for t in range(1, nl):
            dk = dk + dk_refs[t][...]
            dv = dv + dv_refs[t][...]
        dkv, dgk = _mem_kv_bwd(kv_ref[...], dk, dv, gk_ref[...])
        dkv_ref[...] = dkv.astype(BF)

        @pl.when(pl.program_id(0) == 0)
        def _():
            dgk_ref[...] = jnp.zeros_like(dgk_ref)

        dgk_ref[...] += dgk

    row = lambda i: (i, 0)
    one = pl.BlockSpec((1, HEAD), lambda i: (0, 0))
    return _call(
        body, "kv_bwd", (s_len // tm,),
        [pl.BlockSpec((tm, 2 * KVW), row), one] + [pl.BlockSpec((tm, KVW), row)] * (2 * nl),
        [pl.BlockSpec((tm, 2 * KVW), row), one],
        [SDS((s_len, 2 * KVW), BF), SDS((1, HEAD), F32)], semantics=("arbitrary",))(kv, gk, *dks, *dvs)


def _place():
    x, y, c = lax.axis_index("x"), lax.axis_index("y"), lax.axis_index("c")
    flips = [(1 - x, y), (x, 1 - y), (1 - x, 1 - y)]
    return x, y, c, flips


def _remote(src, dst, send_sem, recv_sem, to):
    return pltpu.make_async_remote_copy(src_ref=src, dst_ref=dst, send_sem=send_sem, recv_sem=recv_sem,
                                        device_id=to, device_id_type=MESH)


def _gather_copies(p_refs, wg_refs, send, recv):
    x, y, c, flips = _place()
    chip = 2 * x + y
    cps = []
    for j, (fx, fy) in enumerate(flips):
        for b in range(2):
            half = p_refs[b].shape[0] // 2
            mine = pl.ds(c * half, half)
            cps.append(_remote(p_refs[b].at[mine, :], wg_refs[b].at[chip, mine, :], send.at[2 * j + b],
                               recv.at[2 * j + b], (fx, fy, c)))
    return cps, cps


def _forward_copies(p_refs, wg_refs, send, recv):
    x, y, c, flips = _place()
    chip = 2 * x + y
    sib = (x, y, 1 - c)
    sends, arrivals = [], []
    for b in range(2):
        half = p_refs[b].shape[0] // 2
        own = _remote(p_refs[b], wg_refs[b].at[chip], send.at[b], recv.at[b], sib)
        sends.append(own)
        arrivals.append(own)
        for j, (fx, fy) in enumerate(flips):
            k = 2 + 3 * b + j
            landed = wg_refs[b].at[2 * fx + fy, pl.ds(c * half, half), :]
            other = wg_refs[b].at[2 * fx + fy, pl.ds((1 - c) * half, half), :]
            sends.append(_remote(landed, landed, send.at[k], recv.at[k], sib))
            arrivals.append(_remote(other, other, send.at[k], recv.at[k], sib))
    return sends, arrivals


def _swap_copies(g_refs, r_refs, send, recv):
    x, y, c, _ = _place()
    cps = []
    for b in range(2):
        half = g_refs[b].shape[1] // 2
        cps.append(_remote(g_refs[b].at[:, pl.ds((1 - c) * half, half), :], r_refs[b], send.at[b], recv.at[b],
                           (x, y, 1 - c)))
    return cps, cps


def _split_start(make_copies, n_sems, bufs, after, fresh, *, name):
    def body(a1, a2, b1, b2, after_ref, send, recv, *outs):
        for cp in make_copies((a1, a2), (b1, b2), send, recv)[0]:
            cp.start()
        outs[4][...] = jnp.zeros_like(outs[4])

    extra_shape = () if fresh is None else (pltpu.HBM(fresh, F32),)
    extra_spec = () if fresh is None else (HBM,)
    return pl.pallas_call(
        body, name=name,
        out_shape=(pltpu.SemaphoreType.DMA((n_sems,)), pltpu.SemaphoreType.DMA((n_sems,)))
        + tuple(pltpu.HBM(b.shape, b.dtype) for b in bufs) + (SDS((8, LANE), F32),) + extra_shape,
        in_specs=(HBM, HBM, HBM, HBM, ANY), out_specs=(SEM, SEM, HBM, HBM, HBM, HBM, VMEM_WHOLE) + extra_spec,
        input_output_aliases={0: 2, 1: 3, 2: 4, 3: 5},
        compiler_params=pltpu.CompilerParams(has_side_effects=SIDE_EFFECT))(*[_in_hbm(b) for b in bufs], after)


def _split_wait(make_copies, started, after, *, name):
    send, recv, bufs = started[0], started[1], started[2:6]

    def body(a1, a2, b1, b2, send_ref, recv_ref, after_ref, *outs):
        sends, arrivals = make_copies((a1, a2), (b1, b2), send_ref, recv_ref)
        for cp in arrivals:
            cp.wait_recv()
        for cp in sends:
            cp.wait_send()

    return pl.pallas_call(
        body, name=name, out_shape=tuple(pltpu.HBM(b.shape, b.dtype) for b in bufs),
        in_specs=(HBM, HBM, HBM, HBM, SEM, SEM, ANY), out_specs=(HBM, HBM, HBM, HBM),
        input_output_aliases={0: 0, 1: 1, 2: 2, 3: 3},
        compiler_params=pltpu.CompilerParams(has_side_effects=SIDE_EFFECT))(*bufs, send, recv, after)


def _gather_small(ps):
    def body(ps_ref, o_ref, send, recv):
        x, y, c, flips = _place()
        chip = 2 * x + y
        o_ref[chip] = ps_ref[...]
        cps = [_remote(ps_ref, o_ref.at[chip], send.at[j], recv.at[j], (fx, fy, c))
               for j, (fx, fy) in enumerate(flips)]
        for cp in cps:
            cp.start()
        for j, (fx, fy) in enumerate(flips):
            _remote(ps_ref, o_ref.at[2 * fx + fy], send.at[j], recv.at[j], (fx, fy, c)).wait_recv()
        for cp in cps:
            cp.wait_send()

    return pl.pallas_call(
        body, name="gather_small", in_specs=[VMEM_WHOLE], out_specs=VMEM_WHOLE,
        out_shape=SDS((N_CHIPS,) + ps.shape, ps.dtype),
        scratch_shapes=[pltpu.SemaphoreType.DMA((3,)), pltpu.SemaphoreType.DMA((3,))])(ps)


def _sum_sibling(g, r, place, *, tm, name):
    n_sh, half, w = r.shape
    nt = half // tm

    def body(place_ref, g_ref, r_ref, pbf_ref, own_ref):
        s = pl.program_id(1)
        p = g_ref[0] + r_ref[0]
        pbf_ref[0] = p.astype(BF)

        @pl.when(s == place_ref[1])
        def _():
            own_ref[...] = p

    return _call(
        body, name, (nt, n_sh),
        [pl.BlockSpec((1, tm, w), lambda i, s, pr: (s, pr[0] * nt + i, 0)),
         pl.BlockSpec((1, tm, w), lambda i, s, pr: (s, i, 0))],
        [pl.BlockSpec((1, tm, w), lambda i, s, pr: (s, i, 0)), pl.BlockSpec((tm, w), lambda i, s, pr: (i, 0))],
        [SDS((n_sh, half, w), BF), SDS((half, w), F32)],
        semantics=("arbitrary", "arbitrary"), prefetch=1)(place, g, r)


def _rs_copies(p_refs, land_refs, send, recv):
    _, _, c, flips = _place()
    cps = []
    for j, (fx, fy) in enumerate(flips):
        for b in range(2):
            cps.append(_remote(p_refs[b].at[2 * fx + fy], land_refs[b].at[j], send.at[2 * j + b], recv.at[2 * j + b],
                               (fx, fy, c)))
    return cps, cps


def _sum_chips(own, r, full, layer, place, *, tm, name):
    half, w = own.shape

    def body(place_ref, own_ref, r_ref, _, o_ref):
        o_ref[0, 0] = ((own_ref[...] + r_ref[0].astype(F32)) + r_ref[1].astype(F32)) + r_ref[2].astype(F32)

    return _call(
        body, name, (half // tm,),
        [pl.BlockSpec((tm, w), lambda i, pr: (i, 0)), pl.BlockSpec((3, tm, w), lambda i, pr: (0, i, 0)), ANY],
        pl.BlockSpec((1, 1, tm, w), lambda i, pr: (layer, pr[0], i, 0)), SDS(full.shape, F32),
        semantics=("parallel",), prefetch=1, aliases={3: 0})(place, own, r, full)


def _share_with_sibling(f1, f2):
    def body(_, __, o1_ref, o2_ref, send, recv):
        x, y, c, _ = _place()
        mine, other = pl.ds(c, 1), pl.ds(1 - c, 1)
        cps = [_remote(o1_ref.at[:, mine], o1_ref.at[:, mine], send.at[0], recv.at[0], (x, y, 1 - c)),
               _remote(o2_ref.at[:, mine], o2_ref.at[:, mine], send.at[1], recv.at[1], (x, y, 1 - c))]
        for cp in cps:
            cp.start()
        for cp in cps:
            cp.wait_send()
        _remote(o1_ref.at[:, other], o1_ref.at[:, other], send.at[0], recv.at[0], (x, y, 1 - c)).wait_recv()
        _remote(o2_ref.at[:, other], o2_ref.at[:, other], send.at[1], recv.at[1], (x, y, 1 - c)).wait_recv()

    return pl.pallas_call(
        body, name="share_with_sibling", in_specs=[ANY, ANY], out_specs=[ANY, ANY],
        out_shape=[SDS(f1.shape, f1.dtype), SDS(f2.shape, f2.dtype)], input_output_aliases={0: 0, 1: 1},
        scratch_shapes=[pltpu.SemaphoreType.DMA((2,)), pltpu.SemaphoreType.DMA((2,))])(f1, f2)


def _allreduce_small(sg):
    rows, w = sg.shape
    half = rows // 2
    assert half % 8 == 0

    def body(sg_ref, o_ref, sib_buf, part, slots, send, recv):
        x, y, c, flips = _place()
        chip = 2 * x + y
        sib = (x, y, 1 - c)
        mine = pl.ds(pl.multiple_of(c * half, 8), half)
        other = pl.ds(pl.multiple_of((1 - c) * half, 8), half)
        to_sib = _remote(sg_ref.at[other, :], sib_buf, send.at[0], recv.at[0], sib)
        to_sib.start()
        to_sib.wait_recv()
        part[...] = sg_ref[mine, :] + sib_buf[...]
        slots[chip] = part[...]
        to_chips = [_remote(part, slots.at[chip], send.at[1 + j], recv.at[1 + j], (fx, fy, c))
                    for j, (fx, fy) in enumerate(flips)]
        for cp in to_chips:
            cp.start()
        for j, (fx, fy) in enumerate(flips):
            _remote(part, slots.at[2 * fx + fy], send.at[1 + j], recv.at[1 + j], (fx, fy, c)).wait_recv()
        o_ref[mine, :] = ((slots[0] + slots[1]) + slots[2]) + slots[3]
        back = _remote(o_ref.at[mine, :], o_ref.at[mine, :], send.at[4], recv.at[4], sib)
        back.start()
        _remote(o_ref.at[other, :], o_ref.at[other, :], send.at[4], recv.at[4], sib).wait_recv()
        for cp in [to_sib, back] + to_chips:
            cp.wait_send()

    return pl.pallas_call(
        body, name="allreduce_small", in_specs=[VMEM_WHOLE], out_specs=VMEM_WHOLE, out_shape=SDS((rows, w), F32),
        scratch_shapes=[pltpu.VMEM((half, w), F32), pltpu.VMEM((half, w), F32), pltpu.VMEM((N_CHIPS, half, w), F32),
                        pltpu.SemaphoreType.DMA((5,)), pltpu.SemaphoreType.DMA((5,))])(sg)


def _adamw(g_arr, layer0, g_off, per_layer, w, m, v, *, name, tm):
    rows, cols = w.shape
    assert g_off % tm == 0 and per_layer % tm == 0 and rows % per_layer == 0
    npl = per_layer // tm
    c1 = 1.0 - ADAM_B1 ** ADAM_STEP
    c2 = 1.0 - ADAM_B2 ** ADAM_STEP

    def body(g_ref, w_ref, m_ref, v_ref, go_ref, d_ref, mo_ref, vo_ref):
        g = g_ref[0]
        mn = ADAM_B1 * m_ref[...] + (1.0 - ADAM_B1) * g
        vn = ADAM_B2 * v_ref[...] + (1.0 - ADAM_B2) * (g * g)
        go_ref[...] = g
        mo_ref[...] = mn
        vo_ref[...] = vn
        d_ref[...] = -ADAM_LR * ((mn / c1) / (jnp.sqrt(vn / c2) + ADAM_EPS) + ADAM_WD * w_ref[...])

    blk = pl.BlockSpec((tm, cols), lambda i: (i, 0))
    return _call(
        body, name, (rows // tm,),
        [pl.BlockSpec((1, tm, cols), lambda i: (layer0 + i // npl, g_off // tm + i % npl, 0)), blk, blk, blk],
        [blk] * 4,
        [SDS((rows, cols), F32)] * 4, semantics=("parallel",))(g_arr, w, m, v)


def _pack_small(parts, width):
    flat = jnp.concatenate([p.reshape(-1).astype(F32) for p in parts])
    rows = -(-flat.shape[0] // (16 * width)) * 16
    return jnp.pad(flat, (0, rows * width - flat.shape[0])).reshape(rows, width)


def _unpack_small(packed, shapes):
    flat = packed.reshape(-1)
    out, off = [], 0
    for shp in shapes:
        size = 1
        for n in shp:
            size *= n
        out.append(flat[off:off + size].reshape(shp))
        off += size
    return out


def _block_diag(pw):
    g, c, _ = pw.shape
    eye = jnp.eye(g, dtype=pw.dtype)
    return (eye[:, None, :, None] * pw[:, :, None, :]).reshape(g * c, g * c)


def _diag_blocks(full, g):
    c = full.shape[0] // g
    return jnp.stack([full[i * c:(i + 1) * c, i * c:(i + 1) * c] for i in range(g)])


def kernel(x, mem, norm_mix, w_in, pool_w, pool_scale, kv_norm, w_kv, k_norm, q_norm, sinks, mem_norm, w_mem_kv, mem_q_norm, mem_k_norm, w_out, norm_mlp, w_up, w_down, loss_target, m_norm_mix, m_w_in, m_pool_w, m_pool_scale, m_kv_norm, m_w_kv, m_k_norm, m_q_norm, m_sinks, m_mem_norm, m_w_mem_kv, m_mem_q_norm, m_mem_k_norm, m_w_out, m_norm_mlp, m_w_up, m_w_down, v_norm_mix, v_w_in, v_pool_w, v_pool_scale, v_kv_norm, v_w_kv, v_k_norm, v_q_norm, v_sinks, v_mem_norm, v_w_mem_kv, v_mem_q_norm, v_mem_k_norm, v_w_out, v_norm_mlp, v_w_up, v_w_down):
    s_len, d = x.shape[1], x.shape[2]
    n_layers, n_pool = norm_mix.shape[0], pool_w.shape[0]
    n_swa = n_layers - n_pool
    main = d - KVW
    qh = main // HEAD
    ff = w_down.shape[1] * N_CHIPS
    dq = d // N_CHIPS
    assert w_up.shape[2] == d and ff == N_CHIPS * d and w_kv.shape[1] == 2 * KVW
    tm = min(512, s_len)
    tm_mem = mem.shape[1]

    cx, cy, cc = lax.axis_index("x"), lax.axis_index("y"), lax.axis_index("c")
    chip = 2 * cx + cy
    place = jnp.stack([cc, chip]).astype(jnp.int32)

    off_down, off_up, off_in, off_out = 0, d, 2 * d, 2 * d + dq
    rows1 = off_out + dq
    off_mkv, off_kv = 0, dq
    rows2 = 2 * dq

    ps = jnp.pad(pool_scale, ((0, 8 - n_pool), (0, 2 * LANE - pool_scale.shape[1])))
    psg = _gather_small(ps)
    pool_scale_full = jnp.concatenate([psg[k, :n_pool, :pool_scale.shape[1]] for k in range(N_CHIPS)], axis=1)

    def packed_weights(l):
        p1 = jnp.concatenate([w_down[l], w_up[l], w_in[l], w_out[l]]).astype(BF)
        p2 = jnp.concatenate([w_mem_kv[l], w_kv] if l == n_pool else [w_mem_kv[l]]).astype(BF)
        return p1, p2

    def gather_start(l, after):
        p1, p2 = packed_weights(l)
        bufs = (p1, p2, lax.empty((N_CHIPS,) + p1.shape, BF), lax.empty((N_CHIPS,) + p2.shape, BF))
        return _split_start(_gather_copies, 6, bufs, after, None, name=f"gather_start_{l}")

    def gather_land(l, started, after):
        bufs = _split_wait(_gather_copies, started, after, name=f"gather_wait_{l}")
        return _split_start(_forward_copies, 8, bufs, place, None, name=f"forward_start_{l}")

    def gather_finish(l, forwarding, after):
        bufs = _split_wait(_forward_copies, forwarding, after, name=f"forward_wait_{l}")
        return bufs[2], bufs[3]

    def w_rows(arr, off, nrows, width):
        assert off % nrows == 0
        return (arr, (N_CHIPS, nrows, width), lambda j: (0, off // nrows, 0))

    row = lambda a: a.reshape(1, -1)
    h = x.reshape(s_len, d)
    memx = mem.reshape(tm_mem, d)
    tgt = loss_target.reshape(s_len, d)
    pbd = [_block_diag(pool_w[l]).astype(BF) for l in range(n_pool)]
    sinks_pad = [jnp.pad(row(sinks[j]), ((0, 0), (0, LANE - qh))) for j in range(n_swa)]

    w_in_l, w_out_l, w_down_l, w_up_all_l, w_mkv_l = [], [], [], [], []
    w_kv_g = None
    forwarding = gather_land(0, gather_start(0, psg), psg)
    travelling = gather_start(1, forwarding[6]) if n_layers > 1 else None
    saved, probs, sink_probs, mem_probs = [], {}, {}, {}
    kv = hn_kv = kn = vsh = None
    for l in range(n_layers):
        wg1, wg2 = gather_finish(l, forwarding, h if l else forwarding[6])
        w_in_l.append(w_rows(wg1, off_in, dq, d))
        w_out_l.append(w_rows(wg1, off_out, dq, d))
        w_down_l.append(w_rows(wg1, off_down, d, d))
        w_up_all_l.append((wg1, (N_CHIPS, d, d), lambda j: (0, off_up // d, 0)))
        w_mkv_l.append(w_rows(wg2, off_mkv, dq, 2 * KVW))
        g_mix = row(norm_mix[l])
        if travelling is not None:
            g_mix = g_mix + travelling[6][0, 0]
        if l == n_pool:
            w_kv_g = w_rows(wg2, off_kv, dq, 2 * KVW)
            kv, hn_kv = _norm_mm(h, row(kv_norm), w_kv_g, 1, 2 * KVW, act=False, name="kv_proj", tm=tm)
            kn, vsh = _kv_prep(kv, row(k_norm), tm=tm)
        h0 = h
        proj, xn = _norm_mm(h0, g_mix, w_in_l[l], 1, d, act=False, name=f"in_proj_{l}", tm=tm)
        mkv, memn = _norm_mm(memx, row(mem_norm[l]), w_mkv_l[l], 1, 2 * KVW, act=False, name=f"mem_kv_{l}", tm=tm_mem)
        if l < n_pool:
            cat, mem_probs[l] = _mixer_pool_fwd(proj, mkv, pbd[l], row(pool_scale_full[l]), row(mem_q_norm[l]),
                                                row(mem_k_norm[l]), name=f"mixer_fwd_{l}", tm=tm)
        else:
            j = l - n_pool
            cat, probs[l], sink_probs[l], mem_probs[l] = _mixer_swa_fwd(
                proj, kn, vsh, mkv, row(q_norm[j]), sinks_pad[j], row(mem_q_norm[l]), row(mem_k_norm[l]),
                name=f"mixer_fwd_{l}")
        h1 = _mm_res(h0, cat, w_out_l[l], name=f"out_proj_{l}", tm=tm)
        hh, xm = _norm_mm(h1, row(norm_mlp[l]), w_up_all_l[l], N_CHIPS, d, act=True, name=f"mlp_up_{l}", tm=tm)
        after = None
        if travelling is not None:
            forwarding = gather_land(l + 1, travelling, hh)
            travelling = gather_start(l + 2, forwarding[6]) if l + 2 < n_layers else None
            after = forwarding[6]
        h = _mm_res(h1, hh, w_down_l[l], name=f"mlp_down_{l}", tm=tm, after=after)
        saved.append((h0, proj, xn, mkv, memn, cat, h1, hh, xm))

    dh, dh_bf, loss_part = _loss_head(h, tgt, tm=tm)
    loss = lax.psum(loss_part[0, 0], ("x", "y", "c"))

    half1, half2 = rows1 // 2, rows2 // 2
    g1 = lax.empty((N_CHIPS, rows1, d), F32)
    pending = {}
    swapping = None
    tk = min(512, d)

    def reduce_begin(l, swapped, after):
        g1_l, g2_l, r1, r2 = _split_wait(_swap_copies, swapped, after, name=f"swap_wait_{l}")
        pb1, own1 = _sum_sibling(g1_l, r1, place, tm=_tile(half1, 256), name=f"sum_sibling_a_{l}")
        pb2, own2 = _sum_sibling(g2_l, r2, place, tm=_tile(half2, 256), name=f"sum_sibling_b_{l}")
        bufs = (pb1, pb2, lax.empty((3, half1, d), BF), lax.empty((3, half2, 2 * KVW), BF))
        return _split_start(_rs_copies, 6, bufs, place, None, name=f"reduce_start_{l}"), own1, own2
    zeros_mem = jnp.zeros((tm_mem, d), F32)

    def rows_map(off, nrows, tkk):
        per = nrows // tkk
        return lambda i, j: (i // per, off // tkk + i % per, 0)

    def cols_map(off, tkk):
        return lambda i, j: (j, off // tkk + i, 0)

    d_norm_mix, d_norm_mlp, d_mem_norm = [None] * n_layers, [None] * n_layers, [None] * n_layers
    d_mem_q, d_mem_k = [None] * n_layers, [None] * n_layers
    d_pool_w, d_pool_scale = [None] * n_pool, [None] * n_pool
    d_q_norm, d_sinks = [None] * n_swa, [None] * n_swa
    dks, dvs = [], []
    d_kv_norm = d_k_norm = None
    for l in reversed(range(n_layers)):
        h0, proj, xn, mkv, memn, cat, h1, hh, xm = saved[l]
        g2 = jnp.zeros((N_CHIPS, rows2, 2 * KVW), F32)
        g1 = _mm_tn(hh, dh_bf, g1, rows_map(off_down, d, tk), tk, d, name=f"dw_down_{l}")
        du = _mm_nt_relu2(dh_bf, hh, w_down_l[l], N_CHIPS, name=f"d_mlp_act_{l}", tm=tm)
        g1 = _mm_tn(xm, du, g1, cols_map(off_up, tk), tk, d, name=f"dw_up_{l}")
        g_mlp = row(norm_mlp[l])
        if swapping is not None:
            pending[swapping[0]] = reduce_begin(*swapping, after=g1)
            g_mlp = g_mlp + pending[swapping[0]][0][6][0, 0]
        dh1, dh1_bf, d_norm_mlp[l] = _mm_nt_normbwd(du, w_up_all_l[l], N_CHIPS, h1, g_mlp, dh,
                                                    name=f"d_mlp_in_{l}", tm=tm)
        tkq = min(tk, dq)
        g1 = _mm_tn(cat, dh1_bf, g1, rows_map(off_out, dq, tkq), tkq, d, name=f"dw_out_{l}")
        dcat = _mm_nt(dh1_bf, w_out_l[l], d, name=f"d_cat_{l}", tm=tm)
        if l < n_pool:
            dproj, dpbd, dscale, dmkv, d_mem_q[l], d_mem_k[l] = _mixer_pool_bwd(
                proj, dcat, mkv, pbd[l], row(pool_scale_full[l]), row(mem_q_norm[l]), row(mem_k_norm[l]),
                mem_probs[l], name=f"mixer_bwd_{l}", tm=tm)
            d_pool_w[l] = _diag_blocks(dpbd, len(POOL_WINDOWS))
            d_pool_scale[l] = dscale
        else:
            j = l - n_pool
            dproj, dk, dv, dmkv, d_q_norm[j], dsk, d_mem_q[l], d_mem_k[l] = _mixer_swa_bwd(
                proj, dcat, kn, vsh, mkv, row(q_norm[j]), probs[l], sink_probs[l], row(mem_q_norm[l]),
                row(mem_k_norm[l]), mem_probs[l], name=f"mixer_bwd_{l}")
            d_sinks[j] = dsk[0, :qh]
            dks.append(dk)
            dvs.append(dv)
        g1 = _mm_tn(xn, dproj, g1, rows_map(off_in, dq, tkq), tkq, d, name=f"dw_in_{l}")
        dh, dh_bf, d_norm_mix[l] = _mm_nt_normbwd(dproj, w_in_l[l], 1, h0, row(norm_mix[l]), dh1,
                                                  name=f"d_in_{l}", tm=tm)
        g2 = _mm_tn(memn, dmkv, g2, rows_map(off_mkv, dq, tkq), tkq, 2 * KVW, name=f"dw_mem_kv_{l}")
        _, _, d_mem_norm[l] = _mm_nt_normbwd(dmkv, w_mkv_l[l], 1, memx, row(mem_norm[l]), zeros_mem,
                                             name=f"d_mem_norm_{l}", tm=tm_mem)
        if l == n_pool:
            dkv, d_k_norm = _kv_bwd(kv, dks, dvs, row(k_norm), tm=tm)
            g2 = _mm_tn(hn_kv, dkv, g2, rows_map(off_kv, dq, tkq), tkq, 2 * KVW, name="dw_kv")
            dh, dh_bf, d_kv_norm = _mm_nt_normbwd(dkv, w_kv_g, 1, h0, row(kv_norm), dh, name="d_kv_in", tm=tm)
        bufs = (g1, g2, lax.empty((N_CHIPS, half1, d), F32), lax.empty((N_CHIPS, half2, 2 * KVW), F32))
        swapping = (l, _split_start(_swap_copies, 2, bufs, place, (N_CHIPS, rows1, d) if l > 0 else None,
                                    name=f"swap_start_{l}"))
        g1 = swapping[1][7] if l > 0 else None
    grad_x = dh.reshape(x.shape)

    small_names = ["norm_mix", "pool_w", "pool_scale", "kv_norm", "k_norm", "q_norm", "sinks", "mem_norm",
                   "mem_q_norm", "mem_k_norm", "norm_mlp"]
    small_grads = {
        "norm_mix": jnp.concatenate(d_norm_mix), "pool_w": jnp.stack(d_pool_w),
        "pool_scale": jnp.concatenate(d_pool_scale), "kv_norm": d_kv_norm[0], "k_norm": d_k_norm[0],
        "q_norm": jnp.concatenate(d_q_norm), "sinks": jnp.stack(d_sinks), "mem_norm": jnp.concatenate(d_mem_norm),
        "mem_q_norm": jnp.concatenate(d_mem_q), "mem_k_norm": jnp.concatenate(d_mem_k),
        "norm_mlp": jnp.concatenate(d_norm_mlp)}
    width = d
    sg = _pack_small([small_grads[n] for n in small_names], width)
    sg = sg + swapping[1][6][0, 0]
    sg = _allreduce_small(sg)
    pending[0] = reduce_begin(*swapping, after=sg)
    reduced = dict(zip(small_names, _unpack_small(sg, [small_grads[n].shape for n in small_names])))
    psw = pool_scale.shape[1]
    reduced["pool_scale"] = lax.dynamic_slice_in_dim(reduced["pool_scale"], chip * psw, psw, axis=1)
    params = dict(norm_mix=(norm_mix, m_norm_mix, v_norm_mix), pool_w=(pool_w, m_pool_w, v_pool_w),
                  pool_scale=(pool_scale, m_pool_scale, v_pool_scale), kv_norm=(kv_norm, m_kv_norm, v_kv_norm),
                  k_norm=(k_norm, m_k_norm, v_k_norm), q_norm=(q_norm, m_q_norm, v_q_norm),
                  sinks=(sinks, m_sinks, v_sinks), mem_norm=(mem_norm, m_mem_norm, v_mem_norm),
                  mem_q_norm=(mem_q_norm, m_mem_q_norm, v_mem_q_norm),
                  mem_k_norm=(mem_k_norm, m_mem_k_norm, v_mem_k_norm), norm_mlp=(norm_mlp, m_norm_mlp, v_norm_mlp))
    shapes = [params[n][0].shape for n in small_names]
    packs = [_pack_small([reduced[n].reshape(params[n][0].shape) for n in small_names], width)
             + pending[0][0][6][0, 0]]
    packs += [_pack_small([params[n][t] for n in small_names], width) for t in range(3)]
    res = _adamw(packs[0][None], 0, 0, packs[0].shape[0], packs[1], packs[2], packs[3], name="adamw_small", tm=8)
    small = {n: [] for n in small_names}
    for r in res:
        for n, a in zip(small_names, _unpack_small(r, shapes)):
            small[n].append(a)

    full1 = lax.empty((n_layers, 2, half1, d), F32)
    full2 = lax.empty((n_layers, 2, half2, 2 * KVW), F32)
    for l in reversed(range(n_layers)):
        exchange, own1, own2 = pending[l]
        _, _, x1, x2 = _split_wait(_rs_copies, exchange, res[0], name=f"reduce_wait_{l}")
        full1 = _sum_chips(own1, x1, full1, l, place, tm=_tile(half1, 256), name=f"sum_chips_a_{l}")
        full2 = _sum_chips(own2, x2, full2, l, place, tm=_tile(half2, 256), name=f"sum_chips_b_{l}")
    full1, full2 = _share_with_sibling(full1, full2)
    full1 = full1.reshape(n_layers, rows1, d)
    full2 = full2.reshape(n_layers, rows2, 2 * KVW)

    big = {}
    for name, arr, layer0, off, per, w_, m_, v_ in (
            ("w_down", full1, 0, off_down, d, w_down, m_w_down, v_w_down),
            ("w_up", full1, 0, off_up, d, w_up, m_w_up, v_w_up),
            ("w_in", full1, 0, off_in, dq, w_in, m_w_in, v_w_in),
            ("w_out", full1, 0, off_out, dq, w_out, m_w_out, v_w_out),
            ("w_mem_kv", full2, 0, off_mkv, dq, w_mem_kv, m_w_mem_kv, v_w_mem_kv),
            ("w_kv", full2, n_pool, off_kv, dq, w_kv, m_w_kv, v_w_kv)):
        cols = arr.shape[2]
        upd = _adamw(arr, layer0, off, per, w_.reshape(-1, cols), m_.reshape(-1, cols), v_.reshape(-1, cols),
                     name=f"adamw_{name}", tm=min(256, dq))
        big[name] = [r.reshape(w_.shape) for r in upd]

    order = ["norm_mix", "w_in", "pool_w", "pool_scale", "kv_norm", "w_kv", "k_norm", "q_norm", "sinks", "mem_norm",
             "w_mem_kv", "mem_q_norm", "mem_k_norm", "w_out", "norm_mlp", "w_up", "w_down"]
    out = {**big, **small}
    return (loss, grad_x, *[out[n][0] for n in order], *[out[n][1] for n in order],
            *[out[n][2] for n in order], *[out[n][3] for n in order])
```

```python
import functools

import jax
import jax.numpy as jnp
from jax import lax
from jax.experimental import pallas as pl
from jax.experimental.pallas import tpu as pltpu

F32, BF = jnp.float32, jnp.bfloat16
SDS = jax.ShapeDtypeStruct
MESH = pl.DeviceIdType.MESH
ANY = pl.BlockSpec(memory_space=pl.ANY)
HBM = pl.BlockSpec(memory_space=pltpu.HBM)
SEM = pl.BlockSpec(memory_space=pltpu.SEMAPHORE)
VMEM_WHOLE = pl.BlockSpec(memory_space=pltpu.VMEM)
SIDE_EFFECT = pltpu.SideEffectType.DATAFLOW_SIDE_EFFECTING


def _in_hbm(a):
    return pltpu.with_memory_space_constraint(a, pltpu.HBM)


EPS = 1e-6
HEAD = 64
KV_HEADS = 4
KVW = KV_HEADS * HEAD
WINDOW = 128
POOL_WINDOWS = (2, 4, 8, 16)
HALO = 16
QK_SCALE = HEAD ** -0.5
NEG = float(jnp.finfo(jnp.float32).min)
N_CHIPS = 4
LANE = 128

ADAM_LR, ADAM_B1, ADAM_B2, ADAM_EPS, ADAM_WD, ADAM_STEP = 0.001, 0.9, 0.999, 1e-08, 0.01, 10

VMEM_LIMIT_MB = 56


def _call(body, name, grid, in_specs, out_specs, out_shape, *, scratch=(), semantics=None, aliases=None,
          prefetch=0):
    params = pltpu.CompilerParams(dimension_semantics=semantics, vmem_limit_bytes=VMEM_LIMIT_MB << 20)
    if prefetch:
        spec = pltpu.PrefetchScalarGridSpec(num_scalar_prefetch=prefetch, grid=grid, in_specs=in_specs,
                                            out_specs=out_specs, scratch_shapes=list(scratch))
        return pl.pallas_call(body, name=name, grid_spec=spec, out_shape=out_shape,
                              input_output_aliases=aliases or {}, compiler_params=params)
    return pl.pallas_call(body, name=name, grid=grid, in_specs=in_specs, out_specs=out_specs, out_shape=out_shape,
                          scratch_shapes=list(scratch), input_output_aliases=aliases or {}, compiler_params=params)


def _tile(n, pref):
    return max(t for t in range(8, min(n, pref) + 1, 8) if n % t == 0)


def _dot(a, b):
    return jnp.dot(a, b, preferred_element_type=F32)


def _dot_nt(a, b):
    return lax.dot_general(a, b, (((1,), (1,)), ((), ())), preferred_element_type=F32)


def _dot_tn(a, b):
    return lax.dot_general(a, b, (((0,), (0,)), ((), ())), preferred_element_type=F32)


def _rms(x):
    r = lax.rsqrt(jnp.mean(x * x, axis=-1, keepdims=True) + EPS)
    return x * r, r


def _rms_bwd(dy, xh, r, g):
    dg = jnp.sum(dy * xh, axis=0, keepdims=True)
    dyg = dy * g
    dx = r * (dyg - xh * jnp.mean(dyg * xh, axis=-1, keepdims=True))
    return dx, dg


def _norm_mm(h, g, w, nj, tn, *, act, name, tm):
    w_arr, w_block, w_imap = w
    rows, d = h.shape

    def body(h_ref, g_ref, w_ref, y_ref, xn_ref):
        xh, _ = _rms(h_ref[...])
        xn = (xh * g_ref[...]).astype(BF)
        xn_ref[...] = xn
        for j in range(nj):
            u = _dot(xn, w_ref[j] if nj > 1 else w_ref[...].reshape(d, tn))
            if act:
                a = jnp.maximum(u, 0.0)
                y_ref[:, j * tn:(j + 1) * tn] = (a * a).astype(BF)
            else:
                y_ref[:, j * tn:(j + 1) * tn] = u

    assert nj == 1 or w_block[0] == nj
    return _call(
        body, name, (rows // tm,),
        [pl.BlockSpec((tm, d), lambda i: (i, 0)), pl.BlockSpec((1, d), lambda i: (0, 0)),
         pl.BlockSpec(w_block, lambda i: w_imap(0))],
        [pl.BlockSpec((tm, nj * tn), lambda i: (i, 0)), pl.BlockSpec((tm, d), lambda i: (i, 0))],
        [SDS((rows, nj * tn), BF if act else F32), SDS((rows, d), BF)],
        semantics=("parallel",))(h, g, w_arr)


def _mm_res(res, a, w, *, name, tm, after=None):
    w_arr, w_block, w_imap = w
    rows, k = a.shape
    n = res.shape[1]

    def body(res_ref, a_ref, w_ref, *rest):
        rest[-1][...] = res_ref[...] + _dot(a_ref[...], w_ref[...].reshape(k, n))

    extra = [] if after is None else [after]
    return _call(
        body, name, (rows // tm,),
        [pl.BlockSpec((tm, n), lambda i: (i, 0)), pl.BlockSpec((tm, k), lambda i: (i, 0)),
         pl.BlockSpec(w_block, lambda i: w_imap(0))] + [ANY] * len(extra),
        pl.BlockSpec((tm, n), lambda i: (i, 0)), SDS((rows, n), F32), semantics=("parallel",))(res, a, w_arr, *extra)


def _mm_nt(dy, w, k, *, name, tm):
    w_arr, w_block, w_imap = w
    rows, n = dy.shape

    def body(dy_ref, w_ref, o_ref):
        o_ref[...] = _dot_nt(dy_ref[...], w_ref[...].reshape(k, n))

    return _call(
        body, name, (rows // tm,),
        [pl.BlockSpec((tm, n), lambda i: (i, 0)), pl.BlockSpec(w_block, lambda i: w_imap(0))],
        pl.BlockSpec((tm, k), lambda i: (i, 0)), SDS((rows, k), F32), semantics=("parallel",))(dy, w_arr)


def _mm_nt_relu2(dh, hh, w, nj, *, name, tm):
    w_arr, w_block, w_imap = w
    rows, d = dh.shape
    tk = hh.shape[1] // nj

    def body(dh_ref, hh_ref, w_ref, o_ref):
        dh_t = dh_ref[...]
        for j in range(nj):
            cols = slice(j * tk, (j + 1) * tk)
            dhh = _dot_nt(dh_t, w_ref[j])
            o_ref[:, cols] = (dhh * (2.0 * jnp.sqrt(hh_ref[:, cols].astype(F32)))).astype(BF)

    assert w_block[0] == nj
    return _call(
        body, name, (rows // tm,),
        [pl.BlockSpec((tm, d), lambda i: (i, 0)), pl.BlockSpec((tm, nj * tk), lambda i: (i, 0)),
         pl.BlockSpec(w_block, lambda i: w_imap(0))],
        pl.BlockSpec((tm, nj * tk), lambda i: (i, 0)), SDS((rows, nj * tk), BF),
        semantics=("parallel",))(dh, hh, w_arr)


def _mm_nt_normbwd(dy, w, nsplit, h, g, dres, *, name, tm):
    w_arr, w_block, w_imap = w
    rows, n = dy.shape
    d = h.shape[1]
    ns = n // nsplit

    def body(dy_ref, w_ref, h_ref, g_ref, dres_ref, o_ref, obf_ref, dg_ref):
        if nsplit == 1:
            dxn = _dot_nt(dy_ref[...].astype(BF), w_ref[...].reshape(d, n))
        else:
            dxn = _dot_nt(dy_ref[:, 0:ns].astype(BF), w_ref[0])
            for s in range(1, nsplit):
                dxn += _dot_nt(dy_ref[:, s * ns:(s + 1) * ns].astype(BF), w_ref[s])
        xh, r = _rms(h_ref[...])
        dx, dg = _rms_bwd(dxn, xh, r, g_ref[...])
        out = dres_ref[...] + dx
        o_ref[...] = out
        obf_ref[...] = out.astype(BF)

        @pl.when(pl.program_id(0) == 0)
        def _():
            dg_ref[...] = jnp.zeros_like(dg_ref)

        dg_ref[...] += dg

    row = lambda i: (i, 0)
    return _call(
        body, name, (rows // tm,),
        [pl.BlockSpec((tm, n), row), pl.BlockSpec(w_block, lambda i: w_imap(0)), pl.BlockSpec((tm, d), row),
         pl.BlockSpec((1, d), lambda i: (0, 0)), pl.BlockSpec((tm, d), row)],
        [pl.BlockSpec((tm, d), row), pl.BlockSpec((tm, d), row), pl.BlockSpec((1, d), lambda i: (0, 0))],
        [SDS((rows, d), F32), SDS((rows, d), BF), SDS((1, d), F32)],
        semantics=("arbitrary",))(dy, w_arr, h, g, dres)


def _mm_tn(x, dy, packed, out_imap, tk, tn, *, name):
    s_len, k = x.shape
    n = dy.shape[1]

    def body(x_ref, dy_ref, _, o_ref):
        o_ref[0] = _dot_tn(x_ref[...], dy_ref[...].astype(BF))

    return _call(
        body, name, (k // tk, n // tn),
        [pl.BlockSpec((s_len, tk), lambda i, j: (0, i)), pl.BlockSpec((s_len, tn), lambda i, j: (0, j)), ANY],
        pl.BlockSpec((1, tk, tn), out_imap), SDS(packed.shape, packed.dtype),
        semantics=("parallel", "parallel"), aliases={2: 0})(x, dy, packed)


def _loss_head(y, tgt, *, tm):
    rows, d = y.shape

    def body(y_ref, t_ref, dh_ref, dhbf_ref, loss_ref):
        err = y_ref[...] - t_ref[...]
        dh = err * (1.0 / d)
        dh_ref[...] = dh
        dhbf_ref[...] = dh.astype(BF)

        @pl.when(pl.program_id(0) == 0)
        def _():
            loss_ref[...] = jnp.zeros_like(loss_ref)

        loss_ref[...] += 0.5 * jnp.sum(jnp.mean(err * err, axis=-1, keepdims=True), axis=0, keepdims=True)

    row = lambda i: (i, 0)
    return _call(
        body, "loss_head", (rows // tm,), [pl.BlockSpec((tm, d), row), pl.BlockSpec((tm, d), row)],
        [pl.BlockSpec((tm, d), row), pl.BlockSpec((tm, d), row), pl.BlockSpec((1, 1), lambda i: (0, 0))],
        [SDS((rows, d), F32), SDS((rows, d), BF), SDS((1, 1), F32)], semantics=("arbitrary",))(y, tgt)


def _hs(h):
    return slice(HEAD * h, HEAD * (h + 1))


def _softmax_rows(s):
    e = jnp.exp(s - jnp.max(s, axis=-1, keepdims=True))
    return e * (1.0 / jnp.sum(e, axis=-1, keepdims=True))


def _scaled_bf16(qn):
    return (qn * QK_SCALE).astype(BF)


def _mem_fwd(mq, mk, mv, gq):
    outs, probs = [], []
    for h in range(KV_HEADS):
        xh, _ = _rms(mq[:, _hs(h)])
        p = _softmax_rows(_dot_nt(_scaled_bf16(xh * gq), mk[:, _hs(h)])).astype(BF)
        probs.append(p)
        outs.append(_dot(p, mv[:, _hs(h)]))
    return jnp.concatenate(outs, axis=-1), jnp.concatenate(probs, axis=-1)


def _mem_bwd(mq, do, mk, mv, gq, probs):
    dqs, dks, dvs, dgq = [], [], [], 0.0
    mlen = mk.shape[0]
    for h in range(KV_HEADS):
        xh, r = _rms(mq[:, _hs(h)])
        qn = _scaled_bf16(xh * gq)
        p_bf = probs[:, h * mlen:(h + 1) * mlen]
        p = p_bf.astype(F32)
        doh = do[:, _hs(h)].astype(BF)
        dp = _dot_nt(doh, mv[:, _hs(h)])
        ds = (p * (dp - jnp.sum(p * dp, axis=-1, keepdims=True))).astype(BF)
        dq, dg = _rms_bwd(_dot(ds, mk[:, _hs(h)]) * QK_SCALE, xh, r, gq)
        dqs.append(dq)
        dgq = dgq + dg
        dks.append(_dot_tn(ds, qn))
        dvs.append(_dot_tn(p_bf, doh))
    cat = lambda xs: jnp.concatenate(xs, axis=-1)
    return cat(dqs), cat(dks), cat(dvs), dgq


def _mem_kv(mkv, gk):
    ks = []
    for h in range(KV_HEADS):
        xh, _ = _rms(mkv[:, _hs(h)])
        ks.append(xh * gk)
    return jnp.concatenate(ks, axis=-1).astype(BF), mkv[:, KVW:].astype(BF)


def _mem_kv_bwd(mkv, dmk, dmv, gk):
    dxs, dgk = [], 0.0
    for h in range(KV_HEADS):
        xh, r = _rms(mkv[:, _hs(h)])
        dx, dg = _rms_bwd(dmk[:, _hs(h)], xh, r, gk)
        dxs.append(dx)
        dgk = dgk + dg
    return jnp.concatenate(dxs + [dmv], axis=-1), dgk


def _pool_select(col, gd, a2, a4, a8, a16):
    return jnp.where(col < gd, a2, jnp.where(col < 2 * gd, a4, jnp.where(col < 3 * gd, a8, a16)))


def _pool_count(t0, shape, gd):
    col = lax.broadcasted_iota(jnp.int32, shape, 1)
    t = t0 + lax.broadcasted_iota(jnp.int32, shape, 0)
    win = _pool_select(col, gd, *POOL_WINDOWS)
    return jnp.minimum(t + 1, win).astype(F32)


def _pool_diff(u, halo, t0, gd):
    c = jnp.concatenate([halo, u], axis=0)
    s2 = c + pltpu.roll(c, 1, 0)
    s4 = s2 + pltpu.roll(s2, 2, 0)
    s8 = s4 + pltpu.roll(s4, 4, 0)
    s16 = s8 + pltpu.roll(s8, 8, 0)
    col = lax.broadcasted_iota(jnp.int32, c.shape, 1)
    ws = _pool_select(col, gd, s2, s4, s8, s16)[HALO:]
    return ws / _pool_count(t0, u.shape, gd) - u


def _pool_diff_bwd(dd, dd_halo, t0, gd):
    t = dd.shape[0]
    z = jnp.concatenate([dd / _pool_count(t0, dd.shape, gd), dd_halo / _pool_count(t0 + t, dd_halo.shape, gd)], axis=0)
    n = z.shape[0]
    f2 = z + pltpu.roll(z, n - 1, 0)
    f4 = f2 + pltpu.roll(f2, n - 2, 0)
    f8 = f4 + pltpu.roll(f4, n - 4, 0)
    f16 = f8 + pltpu.roll(f8, n - 8, 0)
    col = lax.broadcasted_iota(jnp.int32, z.shape, 1)
    return _pool_select(col, gd, f2, f4, f8, f16)[:t] - dd


def _swa_bias(n):
    qi = lax.broadcasted_iota(jnp.int32, (WINDOW, 2 * WINDOW), 0)
    kj = lax.broadcasted_iota(jnp.int32, (WINDOW, 2 * WINDOW), 1)
    dist = qi + WINDOW - kj
    valid = (dist >= 0) & (dist < WINDOW) & ((kj >= WINDOW) | (n > 0))
    return dist.astype(F32), valid


def _slopes(qh):
    return [2.0 ** (-8.0 * (h + 1) / qh) for h in range(qh)]


def _swa_probs(qn, kk, dist, valid, slope, sink):
    s = _dot_nt(qn, kk) - slope * dist
    s = jnp.where(valid, s, NEG)
    m = jnp.maximum(jnp.max(s, axis=-1, keepdims=True), sink)
    e = jnp.exp(s - m)
    es = jnp.exp(sink - m)
    z = jnp.sum(e, axis=-1, keepdims=True) + es
    inv = 1.0 / z
    return e * inv, es * inv


def _stack_heads(a, kh, grp):
    return jnp.concatenate([a[:, _hs(h)] for h in range(kh * grp, (kh + 1) * grp)], axis=0)


def _swa_group(q, kh, grp, n, qh, sinks):
    heads = range(kh * grp, (kh + 1) * grp)
    dist, valid = _swa_bias(n)
    slopes = _slopes(qh)
    rows = lambda vals: jnp.concatenate([jnp.broadcast_to(v, (WINDOW, 1)) for v in vals], axis=0)
    slope = rows([jnp.full((1, 1), slopes[h], F32) for h in heads])
    sink = rows([sinks[:, h:h + 1] for h in heads])
    return (_stack_heads(q, kh, grp), slope, sink, jnp.concatenate([dist] * grp, axis=0),
            jnp.concatenate([valid] * grp, axis=0))


def _swa_fwd(q, kk, vv, gq, sinks, n, qh):
    grp = qh // KV_HEADS
    lane = lax.broadcasted_iota(jnp.int32, (WINDOW, LANE), 1)
    outs, probs, sink_probs = [], [], jnp.zeros((WINDOW, LANE), F32)
    for kh in range(KV_HEADS):
        qs, slope, sink, dist, valid = _swa_group(q, kh, grp, n, qh, sinks)
        xh, _ = _rms(qs)
        p, ps = _swa_probs(_scaled_bf16(xh * gq), kk[:, _hs(kh)], dist, valid, slope, sink)
        p = p.astype(BF)
        probs.append(p)
        o = _dot(p, vv[:, _hs(kh)])
        for g in range(grp):
            outs.append(o[g * WINDOW:(g + 1) * WINDOW])
            sink_probs = jnp.where(lane == kh * grp + g, ps[g * WINDOW:(g + 1) * WINDOW], sink_probs)
    return jnp.concatenate(outs, axis=-1), probs, sink_probs


def _swa_bwd(q, do, kk, vv, gq, probs, sink_probs, qh):
    grp = qh // KV_HEADS
    lane = lax.broadcasted_iota(jnp.int32, (1, LANE), 1)
    dqs, dks, dvs, dgq, dsk = [], [], [], 0.0, jnp.zeros((1, LANE), F32)
    do = do.astype(BF)
    for kh in range(KV_HEADS):
        heads = range(kh * grp, (kh + 1) * grp)
        xh, r = _rms(_stack_heads(q, kh, grp))
        qn = _scaled_bf16(xh * gq)
        p_bf = probs[kh]
        p = p_bf.astype(F32)
        ps = jnp.concatenate([sink_probs[:, h:h + 1] for h in heads], axis=0)
        dos = _stack_heads(do, kh, grp)
        dp = _dot_nt(dos, vv[:, _hs(kh)])
        delta = jnp.sum(p * dp, axis=-1, keepdims=True)
        ds = (p * (dp - delta)).astype(BF)
        dsink = ps * delta
        for g in range(grp):
            part = -jnp.sum(dsink[g * WINDOW:(g + 1) * WINDOW], axis=0, keepdims=True)
            dsk = dsk + jnp.where(lane == kh * grp + g, part, 0.0)
        dq, dg = _rms_bwd(_dot(ds, kk[:, _hs(kh)]) * QK_SCALE, xh, r, gq)
        dqs += [dq[g * WINDOW:(g + 1) * WINDOW] for g in range(grp)]
        dgq = dgq + dg
        dks.append(_dot_tn(ds, qn))
        dvs.append(_dot_tn(p_bf, dos))
    cat = lambda xs: jnp.concatenate(xs, axis=-1)
    return cat(dqs), cat(dks), cat(dvs), dgq, dsk


def _mixer_pool_fwd(proj, mkv, pbd, scale, gq, gk, *, name, tm):
    s_len, d = proj.shape
    main = d - KVW
    gd = main // len(POOL_WINDOWS)
    mlen = mkv.shape[0]
    hb = tm // HALO

    def body(u_ref, halo_ref, mq_ref, mkv_ref, pbd_ref, scale_ref, gq_ref, gk_ref, o_ref, mp_ref, mk_s, mv_s):
        i = pl.program_id(0)

        @pl.when(i == 0)
        def _():
            mk, mv = _mem_kv(mkv_ref[...], gk_ref[...])
            mk_s[...] = mk
            mv_s[...] = mv

        halo = jnp.where(i > 0, halo_ref[...], 0.0)
        dif = _pool_diff(u_ref[...], halo, i * tm, gd)
        mixed = _dot(dif.astype(BF), pbd_ref[...]) * scale_ref[...]
        mem, mp_ref[...] = _mem_fwd(mq_ref[...], mk_s[...], mv_s[...], gq_ref[...])
        o_ref[...] = jnp.concatenate([mixed, mem], axis=-1).astype(BF)

    full = lambda shape: pl.BlockSpec(shape, lambda i: (0,) * len(shape))
    return _call(
        body, name, (s_len // tm,),
        [pl.BlockSpec((tm, main), lambda i: (i, 0)),
         pl.BlockSpec((HALO, main), lambda i: (jnp.maximum(i * hb - 1, 0), 0)),
         pl.BlockSpec((tm, KVW), lambda i: (i, main // KVW)),
         full((mlen, 2 * KVW)), full((main, main)), full((1, main)), full((1, HEAD)), full((1, HEAD))],
        [pl.BlockSpec((tm, d), lambda i: (i, 0)), pl.BlockSpec((tm, KV_HEADS * mlen), lambda i: (i, 0))],
        [SDS((s_len, d), BF), SDS((s_len, KV_HEADS * mlen), BF)],
        scratch=[pltpu.VMEM((mlen, KVW), BF), pltpu.VMEM((mlen, KVW), BF)],
        semantics=("arbitrary",))(proj, proj, proj, mkv, pbd, scale, gq, gk)


def _mixer_pool_bwd(proj, dcat, mkv, pbd, scale, gq, gk, mem_probs, *, name, tm):
    s_len, d = proj.shape
    main = d - KVW
    gd = main // len(POOL_WINDOWS)
    mlen = mkv.shape[0]
    hb = tm // HALO
    nt = s_len // tm
    last_halo = s_len // HALO - 1

    def body(u_ref, halo_ref, mq_ref, do_ref, donext_ref, dom_ref, mkv_ref, pbd_ref, scale_ref, gq_ref, gk_ref, mp_ref,
             dproj_ref, dpbd_ref, dscale_ref, dmkv_ref, dgq_ref, dgk_ref, mk_s, mv_s, dmk_s, dmv_s):
        i = pl.program_id(0)

        @pl.when(i == 0)
        def _():
            mk, mv = _mem_kv(mkv_ref[...], gk_ref[...])
            mk_s[...] = mk
            mv_s[...] = mv
            dmk_s[...] = jnp.zeros_like(dmk_s)
            dmv_s[...] = jnp.zeros_like(dmv_s)
            dpbd_ref[...] = jnp.zeros_like(dpbd_ref)
            dscale_ref[...] = jnp.zeros_like(dscale_ref)
            dgq_ref[...] = jnp.zeros_like(dgq_ref)

        pbd = pbd_ref[...]
        scale = scale_ref[...]
        halo = jnp.where(i > 0, halo_ref[...], 0.0)
        dif = _pool_diff(u_ref[...], halo, i * tm, gd).astype(BF)
        do = do_ref[...]
        dscale_ref[...] += jnp.sum(do * _dot(dif, pbd), axis=0, keepdims=True)
        dmixed = (do * scale).astype(BF)
        dpbd_ref[...] += _dot_tn(dif, dmixed)
        dd = _dot_nt(dmixed, pbd)
        donext = jnp.where(i < nt - 1, donext_ref[...], 0.0)
        dd_halo = _dot_nt((donext * scale).astype(BF), pbd)
        du = _pool_diff_bwd(dd, dd_halo, i * tm, gd)

        dmq, dmk, dmv, dgq = _mem_bwd(mq_ref[...], dom_ref[...], mk_s[...], mv_s[...], gq_ref[...], mp_ref[...])
        dmk_s[...] += dmk
        dmv_s[...] += dmv
        dgq_ref[...] += dgq
        dproj_ref[...] = jnp.concatenate([du, dmq], axis=-1).astype(BF)

        @pl.when(i == nt - 1)
        def _():
            dmkv, dgk = _mem_kv_bwd(mkv_ref[...], dmk_s[...], dmv_s[...], gk_ref[...])
            dmkv_ref[...] = dmkv
            dgk_ref[...] = dgk

    full = lambda shape: pl.BlockSpec(shape, lambda i: (0,) * len(shape))
    return _call(
        body, name, (nt,),
        [pl.BlockSpec((tm, main), lambda i: (i, 0)),
         pl.BlockSpec((HALO, main), lambda i: (jnp.maximum(i * hb - 1, 0), 0)),
         pl.BlockSpec((tm, KVW), lambda i: (i, main // KVW)),
         pl.BlockSpec((tm, main), lambda i: (i, 0)),
         pl.BlockSpec((HALO, main), lambda i: (jnp.minimum((i + 1) * hb, last_halo), 0)),
         pl.BlockSpec((tm, KVW), lambda i: (i, main // KVW)),
         full((mlen, 2 * KVW)), full((main, main)), full((1, main)), full((1, HEAD)), full((1, HEAD)),
         pl.BlockSpec((tm, KV_HEADS * mlen), lambda i: (i, 0))],
        [pl.BlockSpec((tm, d), lambda i: (i, 0)), full((main, main)), full((1, main)), full((mlen, 2 * KVW)),
         full((1, HEAD)), full((1, HEAD))],
        [SDS((s_len, d), BF), SDS((main, main), F32), SDS((1, main), F32), SDS((mlen, 2 * KVW), F32),
         SDS((1, HEAD), F32), SDS((1, HEAD), F32)],
        scratch=[pltpu.VMEM((mlen, KVW), BF), pltpu.VMEM((mlen, KVW), BF), pltpu.VMEM((mlen, KVW), F32),
                 pltpu.VMEM((mlen, KVW), F32)],
        semantics=("arbitrary",))(proj, proj, proj, dcat, dcat, dcat, mkv, pbd, scale, gq, gk, mem_probs)


def _mixer_swa_fwd(proj, kn, v, mkv, gqs, sinks, gq, gk, *, name):
    s_len, d = proj.shape
    main = d - KVW
    qh = main // HEAD
    mlen = mkv.shape[0]
    tm = WINDOW
    prow = qh // KV_HEADS * tm

    def body(q_ref, mq_ref, kp_ref, kc_ref, vp_ref, vc_ref, mkv_ref, gqs_ref, sinks_ref, gq_ref, gk_ref, o_ref,
             p_ref, ps_ref, mp_ref, mk_s, mv_s):
        n = pl.program_id(0)

        @pl.when(n == 0)
        def _():
            mk, mv = _mem_kv(mkv_ref[...], gk_ref[...])
            mk_s[...] = mk
            mv_s[...] = mv

        kk = jnp.concatenate([kp_ref[...], kc_ref[...]], axis=0)
        vv = jnp.concatenate([vp_ref[...], vc_ref[...]], axis=0)
        att, probs, sink_probs = _swa_fwd(q_ref[...], kk, vv, gqs_ref[...], sinks_ref[...], n, qh)
        for kh in range(KV_HEADS):
            p_ref[0, kh] = probs[kh]
        ps_ref[...] = sink_probs
        mem, mp_ref[...] = _mem_fwd(mq_ref[...], mk_s[...], mv_s[...], gq_ref[...])
        o_ref[...] = jnp.concatenate([att, mem], axis=-1).astype(BF)

    full = lambda shape: pl.BlockSpec(shape, lambda i: (0,) * len(shape))
    prev = lambda i: (jnp.maximum(i - 1, 0), 0)
    cur = lambda i: (i, 0)
    return _call(
        body, name, (s_len // tm,),
        [pl.BlockSpec((tm, main), cur), pl.BlockSpec((tm, KVW), lambda i: (i, main // KVW)),
         pl.BlockSpec((tm, KVW), prev), pl.BlockSpec((tm, KVW), cur),
         pl.BlockSpec((tm, KVW), prev), pl.BlockSpec((tm, KVW), cur),
         full((mlen, 2 * KVW)), full((1, HEAD)), full((1, LANE)), full((1, HEAD)), full((1, HEAD))],
        [pl.BlockSpec((tm, d), cur), pl.BlockSpec((1, KV_HEADS, prow, 2 * tm), lambda i: (i, 0, 0, 0)),
         pl.BlockSpec((tm, LANE), cur), pl.BlockSpec((tm, KV_HEADS * mlen), cur)],
        [SDS((s_len, d), BF), SDS((s_len // tm, KV_HEADS, prow, 2 * tm), BF), SDS((s_len, LANE), F32),
         SDS((s_len, KV_HEADS * mlen), BF)],
        scratch=[pltpu.VMEM((mlen, KVW), BF), pltpu.VMEM((mlen, KVW), BF)],
        semantics=("arbitrary",))(proj, proj, kn, kn, v, v, mkv, gqs, sinks, gq, gk)


def _mixer_swa_bwd(proj, dcat, kn, v, mkv, gqs, probs, sink_probs, gq, gk, mem_probs, *, name):
    s_len, d = proj.shape
    main = d - KVW
    qh = main // HEAD
    mlen = mkv.shape[0]
    tm = WINDOW
    nt = s_len // tm
    prow = qh // KV_HEADS * tm

    def body(q_ref, mq_ref, do_ref, dom_ref, kp_ref, kc_ref, vp_ref, vc_ref, mkv_ref, gqs_ref, p_ref, ps_ref,
             gq_ref, gk_ref, mp_ref, dproj_ref, dk_ref, dv_ref, dmkv_ref, dgqs_ref, dsinks_ref, dgq_ref, dgk_ref,
             mk_s, mv_s, dmk_s, dmv_s):
        n = pl.program_id(0)

        @pl.when(n == 0)
        def _():
            mk, mv = _mem_kv(mkv_ref[...], gk_ref[...])
            mk_s[...] = mk
            mv_s[...] = mv
            dmk_s[...] = jnp.zeros_like(dmk_s)
            dmv_s[...] = jnp.zeros_like(dmv_s)
            dk_ref[...] = jnp.zeros_like(dk_ref)
            dv_ref[...] = jnp.zeros_like(dv_ref)
            dgqs_ref[...] = jnp.zeros_like(dgqs_ref)
            dsinks_ref[...] = jnp.zeros_like(dsinks_ref)
            dgq_ref[...] = jnp.zeros_like(dgq_ref)

        kk = jnp.concatenate([kp_ref[...], kc_ref[...]], axis=0)
        vv = jnp.concatenate([vp_ref[...], vc_ref[...]], axis=0)
        dq, dkk, dvv, dgqs, dsk = _swa_bwd(q_ref[...], do_ref[...], kk, vv, gqs_ref[...],
                                           [p_ref[0, kh] for kh in range(KV_HEADS)], ps_ref[...], qh)
        prev = pl.ds(pl.multiple_of(jnp.maximum(n - 1, 0) * tm, tm), tm)
        own = pl.ds(pl.multiple_of(n * tm, tm), tm)
        dk_ref[prev, :] += dkk[:tm]
        dk_ref[own, :] += dkk[tm:]
        dv_ref[prev, :] += dvv[:tm]
        dv_ref[own, :] += dvv[tm:]
        dgqs_ref[...] += dgqs
        dsinks_ref[...] += dsk

        dmq, dmk, dmv, dgq = _mem_bwd(mq_ref[...], dom_ref[...], mk_s[...], mv_s[...], gq_ref[...], mp_ref[...])
        dmk_s[...] += dmk
        dmv_s[...] += dmv
        dgq_ref[...] += dgq
        dproj_ref[...] = jnp.concatenate([dq, dmq], axis=-1).astype(BF)

        @pl.when(n == nt - 1)
        def _():
            dmkv, dgk = _mem_kv_bwd(mkv_ref[...], dmk_s[...], dmv_s[...], gk_ref[...])
            dmkv_ref[...] = dmkv
            dgk_ref[...] = dgk

    full = lambda shape: pl.BlockSpec(shape, lambda i: (0,) * len(shape))
    prev_b = lambda i: (jnp.maximum(i - 1, 0), 0)
    cur = lambda i: (i, 0)
    memcol = lambda i: (i, main // KVW)
    return _call(
        body, name, (nt,),
        [pl.BlockSpec((tm, main), cur), pl.BlockSpec((tm, KVW), memcol),
         pl.BlockSpec((tm, main), cur), pl.BlockSpec((tm, KVW), memcol),
         pl.BlockSpec((tm, KVW), prev_b), pl.BlockSpec((tm, KVW), cur),
         pl.BlockSpec((tm, KVW), prev_b), pl.BlockSpec((tm, KVW), cur),
         full((mlen, 2 * KVW)), full((1, HEAD)), pl.BlockSpec((1, KV_HEADS, prow, 2 * tm), lambda i: (i, 0, 0, 0)),
         pl.BlockSpec((tm, LANE), cur), full((1, HEAD)), full((1, HEAD)),
         pl.BlockSpec((tm, KV_HEADS * mlen), cur)],
        [pl.BlockSpec((tm, d), cur), full((s_len, KVW)), full((s_len, KVW)), full((mlen, 2 * KVW)),
         full((1, HEAD)), full((1, LANE)), full((1, HEAD)), full((1, HEAD))],
        [SDS((s_len, d), BF), SDS((s_len, KVW), F32), SDS((s_len, KVW), F32), SDS((mlen, 2 * KVW), F32),
         SDS((1, HEAD), F32), SDS((1, LANE), F32), SDS((1, HEAD), F32), SDS((1, HEAD), F32)],
        scratch=[pltpu.VMEM((mlen, KVW), BF), pltpu.VMEM((mlen, KVW), BF), pltpu.VMEM((mlen, KVW), F32),
                 pltpu.VMEM((mlen, KVW), F32)],
        semantics=("arbitrary",))(proj, proj, dcat, dcat, kn, kn, v, v, mkv, gqs, probs, sink_probs, gq, gk, mem_probs)


def _kv_prep(kv, gk, *, tm):
    s_len = kv.shape[0]

    def body(kv_ref, gk_ref, k_ref, v_ref):
        k, v = _mem_kv(kv_ref[...], gk_ref[...])
        k_ref[...] = k
        v_ref[...] = v

    row = lambda i: (i, 0)
    return _call(
        body, "kv_prep", (s_len // tm,),
        [pl.BlockSpec((tm, 2 * KVW), row), pl.BlockSpec((1, HEAD), lambda i: (0, 0))],
        [pl.BlockSpec((tm, KVW), row), pl.BlockSpec((tm, KVW), row)],
        [SDS((s_len, KVW), BF), SDS((s_len, KVW), BF)], semantics=("parallel",))(kv, gk)


def _kv_bwd(kv, dks, dvs, gk, *, tm):
    s_len = kv.shape[0]
    nl = len(dks)

    def body(*refs):
        kv_ref, gk_ref = refs[0], refs[1]
        dk_refs, dv_refs = refs[2:2 + nl], refs[2 + nl:2 + 2 * nl]
        dkv_ref, dgk_ref = refs[2 + 2 * nl], refs[3 + 2 * nl]
        dk, dv = dk_refs[0][...], dv_refs[0][...]
        for t in range(1, nl):
            dk = dk + dk_refs[t][...]
            dv = dv + dv_refs[t][...]
        dkv, dgk = _mem_kv_bwd(kv_ref[...], dk, dv, gk_ref[...])
        dkv_ref[...] = dkv.astype(BF)

        @pl.when(pl.program_id(0) == 0)
        def _():
            dgk_ref[...] = jnp.zeros_like(dgk_ref)

        dgk_ref[...] += dgk

    row = lambda i: (i, 0)
    one = pl.BlockSpec((1, HEAD), lambda i: (0, 0))
    return _call(
        body, "kv_bwd", (s_len // tm,),
        [pl.BlockSpec((tm, 2 * KVW), row), one] + [pl.BlockSpec((tm, KVW), row)] * (2 * nl),
        [pl.BlockSpec((tm, 2 * KVW), row), one],
        [SDS((s_len, 2 * KVW), BF), SDS((1, HEAD), F32)], semantics=("arbitrary",))(kv, gk, *dks, *dvs)


def _place():
    x, y, c = lax.axis_index("x"), lax.axis_index("y"), lax.axis_index("c")
    flips = [(1 - x, y), (x, 1 - y), (1 - x, 1 - y)]
    return x, y, c, flips


def _remote(src, dst, send_sem, recv_sem, to):
    return pltpu.make_async_remote_copy(src_ref=src, dst_ref=dst, send_sem=send_sem, recv_sem=recv_sem,
                                        device_id=to, device_id_type=MESH)


def _gather_copies(p_refs, wg_refs, send, recv):
    x, y, c, flips = _place()
    chip = 2 * x + y
    cps = []
    for j, (fx, fy) in enumerate(flips):
        for b in range(2):
            half = p_refs[b].shape[0] // 2
            mine = pl.ds(c * half, half)
            cps.append(_remote(p_refs[b].at[mine, :], wg_refs[b].at[chip, mine, :], send.at[2 * j + b],
                               recv.at[2 * j + b], (fx, fy, c)))
    return cps, cps


def _forward_copies(p_refs, wg_refs, send, recv):
    x, y, c, flips = _place()
    chip = 2 * x + y
    sib = (x, y, 1 - c)
    sends, arrivals = [], []
    for b in range(2):
        half = p_refs[b].shape[0] // 2
        own = _remote(p_refs[b], wg_refs[b].at[chip], send.at[b], recv.at[b], sib)
        sends.append(own)
        arrivals.append(own)
        for j, (fx, fy) in enumerate(flips):
            k = 2 + 3 * b + j
            landed = wg_refs[b].at[2 * fx + fy, pl.ds(c * half, half), :]
            other = wg_refs[b].at[2 * fx + fy, pl.ds((1 - c) * half, half), :]
            sends.append(_remote(landed, landed, send.at[k], recv.at[k], sib))
            arrivals.append(_remote(other, other, send.at[k], recv.at[k], sib))
    return sends, arrivals


def _swap_copies(g_refs, r_refs, send, recv):
    x, y, c, _ = _place()
    cps = []
    for b in range(2):
        half = g_refs[b].shape[1] // 2
        cps.append(_remote(g_refs[b].at[:, pl.ds((1 - c) * half, half), :], r_refs[b], send.at[b], recv.at[b],
                           (x, y, 1 - c)))
    return cps, cps


def _split_start(make_copies, n_sems, bufs, after, fresh, *, name):
    def body(a1, a2, b1, b2, after_ref, send, recv, *outs):
        for cp in make_copies((a1, a2), (b1, b2), send, recv)[0]:
            cp.start()
        outs[4][...] = jnp.zeros_like(outs[4])

    extra_shape = () if fresh is None else (pltpu.HBM(fresh, F32),)
    extra_spec = () if fresh is None else (HBM,)
    return pl.pallas_call(
        body, name=name,
        out_shape=(pltpu.SemaphoreType.DMA((n_sems,)), pltpu.SemaphoreType.DMA((n_sems,)))
        + tuple(pltpu.HBM(b.shape, b.dtype) for b in bufs) + (SDS((8, LANE), F32),) + extra_shape,
        in_specs=(HBM, HBM, HBM, HBM, ANY), out_specs=(SEM, SEM, HBM, HBM, HBM, HBM, VMEM_WHOLE) + extra_spec,
        input_output_aliases={0: 2, 1: 3, 2: 4, 3: 5},
        compiler_params=pltpu.CompilerParams(has_side_effects=SIDE_EFFECT))(*[_in_hbm(b) for b in bufs], after)


def _split_wait(make_copies, started, after, *, name):
    send, recv, bufs = started[0], started[1], started[2:6]

    def body(a1, a2, b1, b2, send_ref, recv_ref, after_ref, *outs):
        sends, arrivals = make_copies((a1, a2), (b1, b2), send_ref, recv_ref)
        for cp in arrivals:
            cp.wait_recv()
        for cp in sends:
            cp.wait_send()

    return pl.pallas_call(
        body, name=name, out_shape=tuple(pltpu.HBM(b.shape, b.dtype) for b in bufs),
        in_specs=(HBM, HBM, HBM, HBM, SEM, SEM, ANY), out_specs=(HBM, HBM, HBM, HBM),
        input_output_aliases={0: 0, 1: 1, 2: 2, 3: 3},
        compiler_params=pltpu.CompilerParams(has_side_effects=SIDE_EFFECT))(*bufs, send, recv, after)


def _gather_small(ps):
    def body(ps_ref, o_ref, send, recv):
        x, y, c, flips = _place()
        chip = 2 * x + y
        o_ref[chip] = ps_ref[...]
        cps = [_remote(ps_ref, o_ref.at[chip], send.at[j], recv.at[j], (fx, fy, c))
               for j, (fx, fy) in enumerate(flips)]
        for cp in cps:
            cp.start()
        for j, (fx, fy) in enumerate(flips):
            _remote(ps_ref, o_ref.at[2 * fx + fy], send.at[j], recv.at[j], (fx, fy, c)).wait_recv()
        for cp in cps:
            cp.wait_send()

    return pl.pallas_call(
        body, name="gather_small", in_specs=[VMEM_WHOLE], out_specs=VMEM_WHOLE,
        out_shape=SDS((N_CHIPS,) + ps.shape, ps.dtype),
        scratch_shapes=[pltpu.SemaphoreType.DMA((3,)), pltpu.SemaphoreType.DMA((3,))])(ps)


def _sum_sibling(g, r, place, *, tm, name):
    n_sh, half, w = r.shape
    nt = half // tm

    def body(place_ref, g_ref, r_ref, pbf_ref, own_ref):
        s = pl.program_id(1)
        p = g_ref[0] + r_ref[0]
        pbf_ref[0] = p.astype(BF)

        @pl.when(s == place_ref[1])
        def _():
            own_ref[...] = p

    return _call(
        body, name, (nt, n_sh),
        [pl.BlockSpec((1, tm, w), lambda i, s, pr: (s, pr[0] * nt + i, 0)),
         pl.BlockSpec((1, tm, w), lambda i, s, pr: (s, i, 0))],
        [pl.BlockSpec((1, tm, w), lambda i, s, pr: (s, i, 0)), pl.BlockSpec((tm, w), lambda i, s, pr: (i, 0))],
        [SDS((n_sh, half, w), BF), SDS((half, w), F32)],
        semantics=("arbitrary", "arbitrary"), prefetch=1)(place, g, r)


def _rs_copies(p_refs, land_refs, send, recv):
    _, _, c, flips = _place()
    cps = []
    for j, (fx, fy) in enumerate(flips):
        for b in range(2):
            cps.append(_remote(p_refs[b].at[2 * fx + fy], land_refs[b].at[j], send.at[2 * j + b], recv.at[2 * j + b],
                               (fx, fy, c)))
    return cps, cps


def _sum_chips(own, r, full, layer, place, *, tm, name):
    half, w = own.shape

    def body(place_ref, own_ref, r_ref, _, o_ref):
        o_ref[0, 0] = ((own_ref[...] + r_ref[0].astype(F32)) + r_ref[1].astype(F32)) + r_ref[2].astype(F32)

    return _call(
        body, name, (half // tm,),
        [pl.BlockSpec((tm, w), lambda i, pr: (i, 0)), pl.BlockSpec((3, tm, w), lambda i, pr: (0, i, 0)), ANY],
        pl.BlockSpec((1, 1, tm, w), lambda i, pr: (layer, pr[0], i, 0)), SDS(full.shape, F32),
        semantics=("parallel",), prefetch=1, aliases={3: 0})(place, own, r, full)


def _share_with_sibling(f1, f2):
    def body(_, __, o1_ref, o2_ref, send, recv):
        x, y, c, _ = _place()
        mine, other = pl.ds(c, 1), pl.ds(1 - c, 1)
        cps = [_remote(o1_ref.at[:, mine], o1_ref.at[:, mine], send.at[0], recv.at[0], (x, y, 1 - c)),
               _remote(o2_ref.at[:, mine], o2_ref.at[:, mine], send.at[1], recv.at[1], (x, y, 1 - c))]
        for cp in cps:
            cp.start()
        for cp in cps:
            cp.wait_send()
        _remote(o1_ref.at[:, other], o1_ref.at[:, other], send.at[0], recv.at[0], (x, y, 1 - c)).wait_recv()
        _remote(o2_ref.at[:, other], o2_ref.at[:, other], send.at[1], recv.at[1], (x, y, 1 - c)).wait_recv()

    return pl.pallas_call(
        body, name="share_with_sibling", in_specs=[ANY, ANY], out_specs=[ANY, ANY],
        out_shape=[SDS(f1.shape, f1.dtype), SDS(f2.shape, f2.dtype)], input_output_aliases={0: 0, 1: 1},
        scratch_shapes=[pltpu.SemaphoreType.DMA((2,)), pltpu.SemaphoreType.DMA((2,))])(f1, f2)


def _allreduce_small(sg):
    rows, w = sg.shape
    half = rows // 2
    assert half % 8 == 0

    def body(sg_ref, o_ref, sib_buf, part, slots, send, recv):
        x, y, c, flips = _place()
        chip = 2 * x + y
        sib = (x, y, 1 - c)
        mine = pl.ds(pl.multiple_of(c * half, 8), half)
        other = pl.ds(pl.multiple_of((1 - c) * half, 8), half)
        to_sib = _remote(sg_ref.at[other, :], sib_buf, send.at[0], recv.at[0], sib)
        to_sib.start()
        to_sib.wait_recv()
        part[...] = sg_ref[mine, :] + sib_buf[...]
        slots[chip] = part[...]
        to_chips = [_remote(part, slots.at[chip], send.at[1 + j], recv.at[1 + j], (fx, fy, c))
                    for j, (fx, fy) in enumerate(flips)]
        for cp in to_chips:
            cp.start()
        for j, (fx, fy) in enumerate(flips):
            _remote(part, slots.at[2 * fx + fy], send.at[1 + j], recv.at[1 + j], (fx, fy, c)).wait_recv()
        o_ref[mine, :] = ((slots[0] + slots[1]) + slots[2]) + slots[3]
        back = _remote(o_ref.at[mine, :], o_ref.at[mine, :], send.at[4], recv.at[4], sib)
        back.start()
        _remote(o_ref.at[other, :], o_ref.at[other, :], send.at[4], recv.at[4], sib).wait_recv()
        for cp in [to_sib, back] + to_chips:
            cp.wait_send()

    return pl.pallas_call(
        body, name="allreduce_small", in_specs=[VMEM_WHOLE], out_specs=VMEM_WHOLE, out_shape=SDS((rows, w), F32),
        scratch_shapes=[pltpu.VMEM((half, w), F32), pltpu.VMEM((half, w), F32), pltpu.VMEM((N_CHIPS, half, w), F32),
                        pltpu.SemaphoreType.DMA((5,)), pltpu.SemaphoreType.DMA((5,))])(sg)


def _adamw(g_arr, layer0, g_off, per_layer, w, m, v, *, name, tm):
    rows, cols = w.shape
    assert g_off % tm == 0 and per_layer % tm == 0 and rows % per_layer == 0
    npl = per_layer // tm
    c1 = 1.0 - ADAM_B1 ** ADAM_STEP
    c2 = 1.0 - ADAM_B2 ** ADAM_STEP

    def body(g_ref, w_ref, m_ref, v_ref, go_ref, d_ref, mo_ref, vo_ref):
        g = g_ref[0]
        mn = ADAM_B1 * m_ref[...] + (1.0 - ADAM_B1) * g
        vn = ADAM_B2 * v_ref[...] + (1.0 - ADAM_B2) * (g * g)
        go_ref[...] = g
        mo_ref[...] = mn
        vo_ref[...] = vn
        d_ref[...] = -ADAM_LR * ((mn / c1) / (jnp.sqrt(vn / c2) + ADAM_EPS) + ADAM_WD * w_ref[...])

    blk = pl.BlockSpec((tm, cols), lambda i: (i, 0))
    return _call(
        body, name, (rows // tm,),
        [pl.BlockSpec((1, tm, cols), lambda i: (layer0 + i // npl, g_off // tm + i % npl, 0)), blk, blk, blk],
        [blk] * 4,
        [SDS((rows, cols), F32)] * 4, semantics=("parallel",))(g_arr, w, m, v)


def _pack_small(parts, width):
    flat = jnp.concatenate([p.reshape(-1).astype(F32) for p in parts])
    rows = -(-flat.shape[0] // (16 * width)) * 16
    return jnp.pad(flat, (0, rows * width - flat.shape[0])).reshape(rows, width)


def _unpack_small(packed, shapes):
    flat = packed.reshape(-1)
    out, off = [], 0
    for shp in shapes:
        size = 1
        for n in shp:
            size *= n
        out.append(flat[off:off + size].reshape(shp))
        off += size
    return out


def _block_diag(pw):
    g, c, _ = pw.shape
    eye = jnp.eye(g, dtype=pw.dtype)
    return (eye[:, None, :, None] * pw[:, :, None, :]).reshape(g * c, g * c)


def _diag_blocks(full, g):
    c = full.shape[0] // g
    return jnp.stack([full[i * c:(i + 1) * c, i * c:(i + 1) * c] for i in range(g)])


def kernel(x, mem, norm_mix, w_in, pool_w, pool_scale, kv_norm, w_kv, k_norm, q_norm, sinks, mem_norm, w_mem_kv, mem_q_norm, mem_k_norm, w_out, norm_mlp, w_up, w_down, loss_target, m_norm_mix, m_w_in, m_pool_w, m_pool_scale, m_kv_norm, m_w_kv, m_k_norm, m_q_norm, m_sinks, m_mem_norm, m_w_mem_kv, m_mem_q_norm, m_mem_k_norm, m_w_out, m_norm_mlp, m_w_up, m_w_down, v_norm_mix, v_w_in, v_pool_w, v_pool_scale, v_kv_norm, v_w_kv, v_k_norm, v_q_norm, v_sinks, v_mem_norm, v_w_mem_kv, v_mem_q_norm, v_mem_k_norm, v_w_out, v_norm_mlp, v_w_up, v_w_down):
    s_len, d = x.shape[1], x.shape[2]
    n_layers, n_pool = norm_mix.shape[0], pool_w.shape[0]
    n_swa = n_layers - n_pool
    main = d - KVW
    qh = main // HEAD
    ff = w_down.shape[1] * N_CHIPS
    dq = d // N_CHIPS
    assert w_up.shape[2] == d and ff == N_CHIPS * d and w_kv.shape[1] == 2 * KVW
    tm = min(512, s_len)
    tm_mem = mem.shape[1]

    cx, cy, cc = lax.axis_index("x"), lax.axis_index("y"), lax.axis_index("c")
    chip = 2 * cx + cy
    place = jnp.stack([cc, chip]).astype(jnp.int32)

    off_down, off_up, off_in, off_out = 0, d, 2 * d, 2 * d + dq
    rows1 = off_out + dq
    off_mkv, off_kv = 0, dq
    rows2 = 2 * dq

    ps = jnp.pad(pool_scale, ((0, 8 - n_pool), (0, 2 * LANE - pool_scale.shape[1])))
    psg = _gather_small(ps)
    pool_scale_full = jnp.concatenate([psg[k, :n_pool, :pool_scale.shape[1]] for k in range(N_CHIPS)], axis=1)

    def packed_weights(l):
        p1 = jnp.concatenate([w_down[l], w_up[l], w_in[l], w_out[l]]).astype(BF)
        p2 = jnp.concatenate([w_mem_kv[l], w_kv] if l == n_pool else [w_mem_kv[l]]).astype(BF)
        return p1, p2

    def gather_start(l, after):
        p1, p2 = packed_weights(l)
        bufs = (p1, p2, lax.empty((N_CHIPS,) + p1.shape, BF), lax.empty((N_CHIPS,) + p2.shape, BF))
        return _split_start(_gather_copies, 6, bufs, after, None, name=f"gather_start_{l}")

    def gather_land(l, started, after):
        bufs = _split_wait(_gather_copies, started, after, name=f"gather_wait_{l}")
        return _split_start(_forward_copies, 8, bufs, place, None, name=f"forward_start_{l}")

    def gather_finish(l, forwarding, after):
        bufs = _split_wait(_forward_copies, forwarding, after, name=f"forward_wait_{l}")
        return bufs[2], bufs[3]

    def w_rows(arr, off, nrows, width):
        assert off % nrows == 0
        return (arr, (N_CHIPS, nrows, width), lambda j: (0, off // nrows, 0))

    row = lambda a: a.reshape(1, -1)
    h = x.reshape(s_len, d)
    memx = mem.reshape(tm_mem, d)
    tgt = loss_target.reshape(s_len, d)
    pbd = [_block_diag(pool_w[l]).astype(BF) for l in range(n_pool)]
    sinks_pad = [jnp.pad(row(sinks[j]), ((0, 0), (0, LANE - qh))) for j in range(n_swa)]

    w_in_l, w_out_l, w_down_l, w_up_all_l, w_mkv_l = [], [], [], [], []
    w_kv_g = None
    forwarding = gather_land(0, gather_start(0, psg), psg)
    travelling = gather_start(1, forwarding[6]) if n_layers > 1 else None
    saved, probs, sink_probs, mem_probs = [], {}, {}, {}
    kv = hn_kv = kn = vsh = None
    for l in range(n_layers):
        wg1, wg2 = gather_finish(l, forwarding, h if l else forwarding[6])
        w_in_l.append(w_rows(wg1, off_in, dq, d))
        w_out_l.append(w_rows(wg1, off_out, dq, d))
        w_down_l.append(w_rows(wg1, off_down, d, d))
        w_up_all_l.append((wg1, (N_CHIPS, d, d), lambda j: (0, off_up // d, 0)))
        w_mkv_l.append(w_rows(wg2, off_mkv, dq, 2 * KVW))
        g_mix = row(norm_mix[l])
        if travelling is not None:
            g_mix = g_mix + travelling[6][0, 0]
        if l == n_pool:
            w_kv_g = w_rows(wg2, off_kv, dq, 2 * KVW)
            kv, hn_kv = _norm_mm(h, row(kv_norm), w_kv_g, 1, 2 * KVW, act=False, name="kv_proj", tm=tm)
            kn, vsh = _kv_prep(kv, row(k_norm), tm=tm)
        h0 = h
        proj, xn = _norm_mm(h0, g_mix, w_in_l[l], 1, d, act=False, name=f"in_proj_{l}", tm=tm)
        mkv, memn = _norm_mm(memx, row(mem_norm[l]), w_mkv_l[l], 1, 2 * KVW, act=False, name=f"mem_kv_{l}", tm=tm_mem)
        if l < n_pool:
            cat, mem_probs[l] = _mixer_pool_fwd(proj, mkv, pbd[l], row(pool_scale_full[l]), row(mem_q_norm[l]),
                                                row(mem_k_norm[l]), name=f"mixer_fwd_{l}", tm=tm)
        else:
            j = l - n_pool
            cat, probs[l], sink_probs[l], mem_probs[l] = _mixer_swa_fwd(
                proj, kn, vsh, mkv, row(q_norm[j]), sinks_pad[j], row(mem_q_norm[l]), row(mem_k_norm[l]),
                name=f"mixer_fwd_{l}")
        h1 = _mm_res(h0, cat, w_out_l[l], name=f"out_proj_{l}", tm=tm)
        hh, xm = _norm_mm(h1, row(norm_mlp[l]), w_up_all_l[l], N_CHIPS, d, act=True, name=f"mlp_up_{l}", tm=tm)
        after = None
        if travelling is not None:
            forwarding = gather_land(l + 1, travelling, hh)
            travelling = gather_start(l + 2, forwarding[6]) if l + 2 < n_layers else None
            after = forwarding[6]
        h = _mm_res(h1, hh, w_down_l[l], name=f"mlp_down_{l}", tm=tm, after=after)
        saved.append((h0, proj, xn, mkv, memn, cat, h1, hh, xm))

    dh, dh_bf, loss_part = _loss_head(h, tgt, tm=tm)
    loss = lax.psum(loss_part[0, 0], ("x", "y", "c"))

    half1, half2 = rows1 // 2, rows2 // 2
    g1 = lax.empty((N_CHIPS, rows1, d), F32)
    pending = {}
    swapping = None
    tk = min(512, d)

    def reduce_begin(l, swapped, after):
        g1_l, g2_l, r1, r2 = _split_wait(_swap_copies, swapped, after, name=f"swap_wait_{l}")
        pb1, own1 = _sum_sibling(g1_l, r1, place, tm=_tile(half1, 256), name=f"sum_sibling_a_{l}")
        pb2, own2 = _sum_sibling(g2_l, r2, place, tm=_tile(half2, 256), name=f"sum_sibling_b_{l}")
        bufs = (pb1, pb2, lax.empty((3, half1, d), BF), lax.empty((3, half2, 2 * KVW), BF))
        return _split_start(_rs_copies, 6, bufs, place, None, name=f"reduce_start_{l}"), own1, own2
    zeros_mem = jnp.zeros((tm_mem, d), F32)

    def rows_map(off, nrows, tkk):
        per = nrows // tkk
        return lambda i, j: (i // per, off // tkk + i % per, 0)

    def cols_map(off, tkk):
        return lambda i, j: (j, off // tkk + i, 0)

    d_norm_mix, d_norm_mlp, d_mem_norm = [None] * n_layers, [None] * n_layers, [None] * n_layers
    d_mem_q, d_mem_k = [None] * n_layers, [None] * n_layers
    d_pool_w, d_pool_scale = [None] * n_pool, [None] * n_pool
    d_q_norm, d_sinks = [None] * n_swa, [None] * n_swa
    dks, dvs = [], []
    d_kv_norm = d_k_norm = None
    for l in reversed(range(n_layers)):
        h0, proj, xn, mkv, memn, cat, h1, hh, xm = saved[l]
        g2 = jnp.zeros((N_CHIPS, rows2, 2 * KVW), F32)
        g1 = _mm_tn(hh, dh_bf, g1, rows_map(off_down, d, tk), tk, d, name=f"dw_down_{l}")
        du = _mm_nt_relu2(dh_bf, hh, w_down_l[l], N_CHIPS, name=f"d_mlp_act_{l}", tm=tm)
        g1 = _mm_tn(xm, du, g1, cols_map(off_up, tk), tk, d, name=f"dw_up_{l}")
        g_mlp = row(norm_mlp[l])
        if swapping is not None:
            pending[swapping[0]] = reduce_begin(*swapping, after=g1)
            g_mlp = g_mlp + pending[swapping[0]][0][6][0, 0]
        dh1, dh1_bf, d_norm_mlp[l] = _mm_nt_normbwd(du, w_up_all_l[l], N_CHIPS, h1, g_mlp, dh,
                                                    name=f"d_mlp_in_{l}", tm=tm)
        tkq = min(tk, dq)
        g1 = _mm_tn(cat, dh1_bf, g1, rows_map(off_out, dq, tkq), tkq, d, name=f"dw_out_{l}")
        dcat = _mm_nt(dh1_bf, w_out_l[l], d, name=f"d_cat_{l}", tm=tm)
        if l < n_pool:
            dproj, dpbd, dscale, dmkv, d_mem_q[l], d_mem_k[l] = _mixer_pool_bwd(
                proj, dcat, mkv, pbd[l], row(pool_scale_full[l]), row(mem_q_norm[l]), row(mem_k_norm[l]),
                mem_probs[l], name=f"mixer_bwd_{l}", tm=tm)
            d_pool_w[l] = _diag_blocks(dpbd, len(POOL_WINDOWS))
            d_pool_scale[l] = dscale
        else:
            j = l - n_pool
            dproj, dk, dv, dmkv, d_q_norm[j], dsk, d_mem_q[l], d_mem_k[l] = _mixer_swa_bwd(
                proj, dcat, kn, vsh, mkv, row(q_norm[j]), probs[l], sink_probs[l], row(mem_q_norm[l]),
                row(mem_k_norm[l]), mem_probs[l], name=f"mixer_bwd_{l}")
            d_sinks[j] = dsk[0, :qh]
            dks.append(dk)
            dvs.append(dv)
        g1 = _mm_tn(xn, dproj, g1, rows_map(off_in, dq, tkq), tkq, d, name=f"dw_in_{l}")
        dh, dh_bf, d_norm_mix[l] = _mm_nt_normbwd(dproj, w_in_l[l], 1, h0, row(norm_mix[l]), dh1,
                                                  name=f"d_in_{l}", tm=tm)
        g2 = _mm_tn(memn, dmkv, g2, rows_map(off_mkv, dq, tkq), tkq, 2 * KVW, name=f"dw_mem_kv_{l}")
        _, _, d_mem_norm[l] = _mm_nt_normbwd(dmkv, w_mkv_l[l], 1, memx, row(mem_norm[l]), zeros_mem,
                                             name=f"d_mem_norm_{l}", tm=tm_mem)
        if l == n_pool:
            dkv, d_k_norm = _kv_bwd(kv, dks, dvs, row(k_norm), tm=tm)
            g2 = _mm_tn(hn_kv, dkv, g2, rows_map(off_kv, dq, tkq), tkq, 2 * KVW, name="dw_kv")
            dh, dh_bf, d_kv_norm = _mm_nt_normbwd(dkv, w_kv_g, 1, h0, row(kv_norm), dh, name="d_kv_in", tm=tm)
        bufs = (g1, g2, lax.empty((N_CHIPS, half1, d), F32), lax.empty((N_CHIPS, half2, 2 * KVW), F32))
        swapping = (l, _split_start(_swap_copies, 2, bufs, place, (N_CHIPS, rows1, d) if l > 0 else None,
                                    name=f"swap_start_{l}"))
        g1 = swapping[1][7] if l > 0 else None
    grad_x = dh.reshape(x.shape)

    small_names = ["norm_mix", "pool_w", "pool_scale", "kv_norm", "k_norm", "q_norm", "sinks", "mem_norm",
                   "mem_q_norm", "mem_k_norm", "norm_mlp"]
    small_grads = {
        "norm_mix": jnp.concatenate(d_norm_mix), "pool_w": jnp.stack(d_pool_w),
        "pool_scale": jnp.concatenate(d_pool_scale), "kv_norm": d_kv_norm[0], "k_norm": d_k_norm[0],
        "q_norm": jnp.concatenate(d_q_norm), "sinks": jnp.stack(d_sinks), "mem_norm": jnp.concatenate(d_mem_norm),
        "mem_q_norm": jnp.concatenate(d_mem_q), "mem_k_norm": jnp.concatenate(d_mem_k),
        "norm_mlp": jnp.concatenate(d_norm_mlp)}
    width = d
    sg = _pack_small([small_grads[n] for n in small_names], width)
    sg = sg + swapping[1][6][0, 0]
    sg = _allreduce_small(sg)
    pending[0] = reduce_begin(*swapping, after=sg)
    reduced = dict(zip(small_names, _unpack_small(sg, [small_grads[n].shape for n in small_names])))
    psw = pool_scale.shape[1]
    reduced["pool_scale"] = lax.dynamic_slice_in_dim(reduced["pool_scale"], chip * psw, psw, axis=1)
    params = dict(norm_mix=(norm_mix, m_norm_mix, v_norm_mix), pool_w=(pool_w, m_pool_w, v_pool_w),
                  pool_scale=(pool_scale, m_pool_scale, v_pool_scale), kv_norm=(kv_norm, m_kv_norm, v_kv_norm),
                  k_norm=(k_norm, m_k_norm, v_k_norm), q_norm=(q_norm, m_q_norm, v_q_norm),
                  sinks=(sinks, m_sinks, v_sinks), mem_norm=(mem_norm, m_mem_norm, v_mem_norm),
                  mem_q_norm=(mem_q_norm, m_mem_q_norm, v_mem_q_norm),
                  mem_k_norm=(mem_k_norm, m_mem_k_norm, v_mem_k_norm), norm_mlp=(norm_mlp, m_norm_mlp, v_norm_mlp))
    shapes = [params[n][0].shape for n in small_names]
    packs = [_pack_small([reduced[n].reshape(params[n][0].shape) for n in small_names], width)
             + pending[0][0][6][0, 0]]
    packs += [_pack_small([params[n][t] for n in small_names], width) for t in range(3)]
    res = _adamw(packs[0][None], 0, 0, packs[0].shape[0], packs[1], packs[2], packs[3], name="adamw_small", tm=8)
    small = {n: [] for n in small_names}
    for r in res:
        for n, a in zip(small_names, _unpack_small(r, shapes)):
            small[n].append(a)

    full1 = lax.empty((n_layers, 2, half1, d), F32)
    full2 = lax.empty((n_layers, 2, half2, 2 * KVW), F32)
    for l in reversed(range(n_layers)):
        exchange, own1, own2 = pending[l]
        _, _, x1, x2 = _split_wait(_rs_copies, exchange, res[0], name=f"reduce_wait_{l}")
        full1 = _sum_chips(own1, x1, full1, l, place, tm=_tile(half1, 256), name=f"sum_chips_a_{l}")
        full2 = _sum_chips(own2, x2, full2, l, place, tm=_tile(half2, 256), name=f"sum_chips_b_{l}")
    full1, full2 = _share_with_sibling(full1, full2)
    full1 = full1.reshape(n_layers, rows1, d)
    full2 = full2.reshape(n_layers, rows2, 2 * KVW)

    big = {}
    for name, arr, layer0, off, per, w_, m_, v_ in (
            ("w_down", full1, 0, off_down, d, w_down, m_w_down, v_w_down),
            ("w_up", full1, 0, off_up, d, w_up, m_w_up, v_w_up),
            ("w_in", full1, 0, off_in, dq, w_in, m_w_in, v_w_in),
            ("w_out", full1, 0, off_out, dq, w_out, m_w_out, v_w_out),
            ("w_mem_kv", full2, 0, off_mkv, dq, w_mem_kv, m_w_mem_kv, v_w_mem_kv),
            ("w_kv", full2, n_pool, off_kv, dq, w_kv, m_w_kv, v_w_kv)):
        cols = arr.shape[2]
        upd = _adamw(arr, layer0, off, per, w_.reshape(-1, cols), m_.reshape(-1, cols), v_.reshape(-1, cols),
                     name=f"adamw_{name}", tm=min(256, dq))
        big[name] = [r.reshape(w_.shape) for r in upd]

    order = ["norm_mix", "w_in", "pool_w", "pool_scale", "kv_norm", "w_kv", "k_norm", "q_norm", "sinks", "mem_norm",
             "w_mem_kv", "mem_q_norm", "mem_k_norm", "w_out", "norm_mlp", "w_up", "w_down"]
    out = {**big, **small}
    return (loss, grad_x, *[out[n][0] for n in order], *[out[n][1] for n in order],
            *[out[n][2] for n in order], *[out[n][3] for n in order])
```

```python
import functools

import jax
import jax.numpy as jnp
from jax import lax
from jax.experimental import pallas as pl
from jax.experimental.pallas import tpu as pltpu

F32, BF = jnp.float32, jnp.bfloat16
SDS = jax.ShapeDtypeStruct
MESH = pl.DeviceIdType.MESH
ANY = pl.BlockSpec(memory_space=pl.ANY)
HBM = pl.BlockSpec(memory_space=pltpu.HBM)
SEM = pl.BlockSpec(memory_space=pltpu.SEMAPHORE)
VMEM_WHOLE = pl.BlockSpec(memory_space=pltpu.VMEM)
SIDE_EFFECT = pltpu.SideEffectType.DATAFLOW_SIDE_EFFECTING


def _in_hbm(a):
    return pltpu.with_memory_space_constraint(a, pltpu.HBM)


EPS = 1e-6
HEAD = 64
KV_HEADS = 4
KVW = KV_HEADS * HEAD
WINDOW = 128
POOL_WINDOWS = (2, 4, 8, 16)
HALO = 16
QK_SCALE = HEAD ** -0.5
NEG = float(jnp.finfo(jnp.float32).min)
N_CHIPS = 4
LANE = 128

ADAM_LR, ADAM_B1, ADAM_B2, ADAM_EPS, ADAM_WD, ADAM_STEP = 0.001, 0.9, 0.999, 1e-08, 0.01, 10

VMEM_LIMIT_MB = 56


def _call(body, name, grid, in_specs, out_specs, out_shape, *, scratch=(), semantics=None, aliases=None,
          prefetch=0):
    params = pltpu.CompilerParams(dimension_semantics=semantics, vmem_limit_bytes=VMEM_LIMIT_MB << 20)
    if prefetch:
        spec = pltpu.PrefetchScalarGridSpec(num_scalar_prefetch=prefetch, grid=grid, in_specs=in_specs,
                                            out_specs=out_specs, scratch_shapes=list(scratch))
        return pl.pallas_call(body, name=name, grid_spec=spec, out_shape=out_shape,
                              input_output_aliases=aliases or {}, compiler_params=params)
    return pl.pallas_call(body, name=name, grid=grid, in_specs=in_specs, out_specs=out_specs, out_shape=out_shape,
                          scratch_shapes=list(scratch), input_output_aliases=aliases or {}, compiler_params=params)


def _tile(n, pref):
    return max(t for t in range(8, min(n, pref) + 1, 8) if n % t == 0)


def _dot(a, b):
    return jnp.dot(a, b, preferred_element_type=F32)


def _dot_nt(a, b):
    return lax.dot_general(a, b, (((1,), (1,)), ((), ())), preferred_element_type=F32)


def _dot_tn(a, b):
    return lax.dot_general(a, b, (((0,), (0,)), ((), ())), preferred_element_type=F32)


def _rms(x):
    r = lax.rsqrt(jnp.mean(x * x, axis=-1, keepdims=True) + EPS)
    return x * r, r


def _rms_bwd(dy, xh, r, g):
    dg = jnp.sum(dy * xh, axis=0, keepdims=True)
    dyg = dy * g
    dx = r * (dyg - xh * jnp.mean(dyg * xh, axis=-1, keepdims=True))
    return dx, dg


def _norm_mm(h, g, w, nj, tn, *, act, name, tm):
    w_arr, w_block, w_imap = w
    rows, d = h.shape

    def body(h_ref, g_ref, w_ref, y_ref, xn_ref):
        xh, _ = _rms(h_ref[...])
        xn = (xh * g_ref[...]).astype(BF)
        xn_ref[...] = xn
        for j in range(nj):
            u = _dot(xn, w_ref[j] if nj > 1 else w_ref[...].reshape(d, tn))
            if act:
                a = jnp.maximum(u, 0.0)
                y_ref[:, j * tn:(j + 1) * tn] = (a * a).astype(BF)
            else:
                y_ref[:, j * tn:(j + 1) * tn] = u

    assert nj == 1 or w_block[0] == nj
    return _call(
        body, name, (rows // tm,),
        [pl.BlockSpec((tm, d), lambda i: (i, 0)), pl.BlockSpec((1, d), lambda i: (0, 0)),
         pl.BlockSpec(w_block, lambda i: w_imap(0))],
        [pl.BlockSpec((tm, nj * tn), lambda i: (i, 0)), pl.BlockSpec((tm, d), lambda i: (i, 0))],
        [SDS((rows, nj * tn), BF if act else F32), SDS((rows, d), BF)],
        semantics=("parallel",))(h, g, w_arr)


def _mm_res(res, a, w, *, name, tm, after=None):
    w_arr, w_block, w_imap = w
    rows, k = a.shape
    n = res.shape[1]

    def body(res_ref, a_ref, w_ref, *rest):
        rest[-1][...] = res_ref[...] + _dot(a_ref[...], w_ref[...].reshape(k, n))

    extra = [] if after is None else [after]
    return _call(
        body, name, (rows // tm,),
        [pl.BlockSpec((tm, n), lambda i: (i, 0)), pl.BlockSpec((tm, k), lambda i: (i, 0)),
         pl.BlockSpec(w_block, lambda i: w_imap(0))] + [ANY] * len(extra),
        pl.BlockSpec((tm, n), lambda i: (i, 0)), SDS((rows, n), F32), semantics=("parallel",))(res, a, w_arr, *extra)


def _mm_nt(dy, w, k, *, name, tm):
    w_arr, w_block, w_imap = w
    rows, n = dy.shape

    def body(dy_ref, w_ref, o_ref):
        o_ref[...] = _dot_nt(dy_ref[...], w_ref[...].reshape(k, n))

    return _call(
        body, name, (rows // tm,),
        [pl.BlockSpec((tm, n), lambda i: (i, 0)), pl.BlockSpec(w_block, lambda i: w_imap(0))],
        pl.BlockSpec((tm, k), lambda i: (i, 0)), SDS((rows, k), F32), semantics=("parallel",))(dy, w_arr)


def _mm_nt_relu2(dh, hh, w, nj, *, name, tm):
    w_arr, w_block, w_imap = w
    rows, d = dh.shape
    tk = hh.shape[1] // nj

    def body(dh_ref, hh_ref, w_ref, o_ref):
        dh_t = dh_ref[...]
        for j in range(nj):
            cols = slice(j * tk, (j + 1) * tk)
            dhh = _dot_nt(dh_t, w_ref[j])
            o_ref[:, cols] = (dhh * (2.0 * jnp.sqrt(hh_ref[:, cols].astype(F32)))).astype(BF)

    assert w_block[0] == nj
    return _call(
        body, name, (rows // tm,),
        [pl.BlockSpec((tm, d), lambda i: (i, 0)), pl.BlockSpec((tm, nj * tk), lambda i: (i, 0)),
         pl.BlockSpec(w_block, lambda i: w_imap(0))],
        pl.BlockSpec((tm, nj * tk), lambda i: (i, 0)), SDS((rows, nj * tk), BF),
        semantics=("parallel",))(dh, hh, w_arr)


def _mm_nt_normbwd(dy, w, nsplit, h, g, dres, *, name, tm):
    w_arr, w_block, w_imap = w
    rows, n = dy.shape
    d = h.shape[1]
    ns = n // nsplit

    def body(dy_ref, w_ref, h_ref, g_ref, dres_ref, o_ref, obf_ref, dg_ref):
        if nsplit == 1:
            dxn = _dot_nt(dy_ref[...].astype(BF), w_ref[...].reshape(d, n))
        else:
            dxn = _dot_nt(dy_ref[:, 0:ns].astype(BF), w_ref[0])
            for s in range(1, nsplit):
                dxn += _dot_nt(dy_ref[:, s * ns:(s + 1) * ns].astype(BF), w_ref[s])
        xh, r = _rms(h_ref[...])
        dx, dg = _rms_bwd(dxn, xh, r, g_ref[...])
        out = dres_ref[...] + dx
        o_ref[...] = out
        obf_ref[...] = out.astype(BF)

        @pl.when(pl.program_id(0) == 0)
        def _():
            dg_ref[...] = jnp.zeros_like(dg_ref)

        dg_ref[...] += dg

    row = lambda i: (i, 0)
    return _call(
        body, name, (rows // tm,),
        [pl.BlockSpec((tm, n), row), pl.BlockSpec(w_block, lambda i: w_imap(0)), pl.BlockSpec((tm, d), row),
         pl.BlockSpec((1, d), lambda i: (0, 0)), pl.BlockSpec((tm, d), row)],
        [pl.BlockSpec((tm, d), row), pl.BlockSpec((tm, d), row), pl.BlockSpec((1, d), lambda i: (0, 0))],
        [SDS((rows, d), F32), SDS((rows, d), BF), SDS((1, d), F32)],
        semantics=("arbitrary",))(dy, w_arr, h, g, dres)


def _mm_tn(x, dy, packed, out_imap, tk, tn, *, name):
    s_len, k = x.shape
    n = dy.shape[1]

    def body(x_ref, dy_ref, _, o_ref):
        o_ref[0] = _dot_tn(x_ref[...], dy_ref[...].astype(BF))

    return _call(
        body, name, (k // tk, n // tn),
        [pl.BlockSpec((s_len, tk), lambda i, j: (0, i)), pl.BlockSpec((s_len, tn), lambda i, j: (0, j)), ANY],
        pl.BlockSpec((1, tk, tn), out_imap), SDS(packed.shape, packed.dtype),
        semantics=("parallel", "parallel"), aliases={2: 0})(x, dy, packed)


def _loss_head(y, tgt, *, tm):
    rows, d = y.shape

    def body(y_ref, t_ref, dh_ref, dhbf_ref, loss_ref):
        err = y_ref[...] - t_ref[...]
        dh = err * (1.0 / d)
        dh_ref[...] = dh
        dhbf_ref[...] = dh.astype(BF)

        @pl.when(pl.program_id(0) == 0)
        def _():
            loss_ref[...] = jnp.zeros_like(loss_ref)

        loss_ref[...] += 0.5 * jnp.sum(jnp.mean(err * err, axis=-1, keepdims=True), axis=0, keepdims=True)

    row = lambda i: (i, 0)
    return _call(
        body, "loss_head", (rows // tm,), [pl.BlockSpec((tm, d), row), pl.BlockSpec((tm, d), row)],
        [pl.BlockSpec((tm, d), row), pl.BlockSpec((tm, d), row), pl.BlockSpec((1, 1), lambda i: (0, 0))],
        [SDS((rows, d), F32), SDS((rows, d), BF), SDS((1, 1), F32)], semantics=("arbitrary",))(y, tgt)


def _hs(h):
    return slice(HEAD * h, HEAD * (h + 1))


def _softmax_rows(s):
    e = jnp.exp(s - jnp.max(s, axis=-1, keepdims=True))
    return e * (1.0 / jnp.sum(e, axis=-1, keepdims=True))


def _scaled_bf16(qn):
    return (qn * QK_SCALE).astype(BF)


def _mem_fwd(mq, mk, mv, gq):
    outs, probs = [], []
    for h in range(KV_HEADS):
        xh, _ = _rms(mq[:, _hs(h)])
        p = _softmax_rows(_dot_nt(_scaled_bf16(xh * gq), mk[:, _hs(h)])).astype(BF)
        probs.append(p)
        outs.append(_dot(p, mv[:, _hs(h)]))
    return jnp.concatenate(outs, axis=-1), jnp.concatenate(probs, axis=-1)


def _mem_bwd(mq, do, mk, mv, gq, probs):
    dqs, dks, dvs, dgq = [], [], [], 0.0
    mlen = mk.shape[0]
    for h in range(KV_HEADS):
        xh, r = _rms(mq[:, _hs(h)])
        qn = _scaled_bf16(xh * gq)
        p_bf = probs[:, h * mlen:(h + 1) * mlen]
        p = p_bf.astype(F32)
        doh = do[:, _hs(h)].astype(BF)
        dp = _dot_nt(doh, mv[:, _hs(h)])
        ds = (p * (dp - jnp.sum(p * dp, axis=-1, keepdims=True))).astype(BF)
        dq, dg = _rms_bwd(_dot(ds, mk[:, _hs(h)]) * QK_SCALE, xh, r, gq)
        dqs.append(dq)
        dgq = dgq + dg
        dks.append(_dot_tn(ds, qn))
        dvs.append(_dot_tn(p_bf, doh))
    cat = lambda xs: jnp.concatenate(xs, axis=-1)
    return cat(dqs), cat(dks), cat(dvs), dgq


def _mem_kv(mkv, gk):
    ks = []
    for h in range(KV_HEADS):
        xh, _ = _rms(mkv[:, _hs(h)])
        ks.append(xh * gk)
    return jnp.concatenate(ks, axis=-1).astype(BF), mkv[:, KVW:].astype(BF)


def _mem_kv_bwd(mkv, dmk, dmv, gk):
    dxs, dgk = [], 0.0
    for h in range(KV_HEADS):
        xh, r = _rms(mkv[:, _hs(h)])
        dx, dg = _rms_bwd(dmk[:, _hs(h)], xh, r, gk)
        dxs.append(dx)
        dgk = dgk + dg
    return jnp.concatenate(dxs + [dmv], axis=-1), dgk


def _pool_select(col, gd, a2, a4, a8, a16):
    return jnp.where(col < gd, a2, jnp.where(col < 2 * gd, a4, jnp.where(col < 3 * gd, a8, a16)))


def _pool_count(t0, shape, gd):
    col = lax.broadcasted_iota(jnp.int32, shape, 1)
    t = t0 + lax.broadcasted_iota(jnp.int32, shape, 0)
    win = _pool_select(col, gd, *POOL_WINDOWS)
    return jnp.minimum(t + 1, win).astype(F32)


def _pool_diff(u, halo, t0, gd):
    c = jnp.concatenate([halo, u], axis=0)
    s2 = c + pltpu.roll(c, 1, 0)
    s4 = s2 + pltpu.roll(s2, 2, 0)
    s8 = s4 + pltpu.roll(s4, 4, 0)
    s16 = s8 + pltpu.roll(s8, 8, 0)
    col = lax.broadcasted_iota(jnp.int32, c.shape, 1)
    ws = _pool_select(col, gd, s2, s4, s8, s16)[HALO:]
    return ws / _pool_count(t0, u.shape, gd) - u


def _pool_diff_bwd(dd, dd_halo, t0, gd):
    t = dd.shape[0]
    z = jnp.concatenate([dd / _pool_count(t0, dd.shape, gd), dd_halo / _pool_count(t0 + t, dd_halo.shape, gd)], axis=0)
    n = z.shape[0]
    f2 = z + pltpu.roll(z, n - 1, 0)
    f4 = f2 + pltpu.roll(f2, n - 2, 0)
    f8 = f4 + pltpu.roll(f4, n - 4, 0)
    f16 = f8 + pltpu.roll(f8, n - 8, 0)
    col = lax.broadcasted_iota(jnp.int32, z.shape, 1)
    return _pool_select(col, gd, f2, f4, f8, f16)[:t] - dd


def _swa_bias(n):
    qi = lax.broadcasted_iota(jnp.int32, (WINDOW, 2 * WINDOW), 0)
    kj = lax.broadcasted_iota(jnp.int32, (WINDOW, 2 * WINDOW), 1)
    dist = qi + WINDOW - kj
    valid = (dist >= 0) & (dist < WINDOW) & ((kj >= WINDOW) | (n > 0))
    return dist.astype(F32), valid


def _slopes(qh):
    return [2.0 ** (-8.0 * (h + 1) / qh) for h in range(qh)]


def _swa_probs(qn, kk, dist, valid, slope, sink):
    s = _dot_nt(qn, kk) - slope * dist
    s = jnp.where(valid, s, NEG)
    m = jnp.maximum(jnp.max(s, axis=-1, keepdims=True), sink)
    e = jnp.exp(s - m)
    es = jnp.exp(sink - m)
    z = jnp.sum(e, axis=-1, keepdims=True) + es
    inv = 1.0 / z
    return e * inv, es * inv


def _stack_heads(a, kh, grp):
    return jnp.concatenate([a[:, _hs(h)] for h in range(kh * grp, (kh + 1) * grp)], axis=0)


def _swa_group(q, kh, grp, n, qh, sinks):
    heads = range(kh * grp, (kh + 1) * grp)
    dist, valid = _swa_bias(n)
    slopes = _slopes(qh)
    rows = lambda vals: jnp.concatenate([jnp.broadcast_to(v, (WINDOW, 1)) for v in vals], axis=0)
    slope = rows([jnp.full((1, 1), slopes[h], F32) for h in heads])
    sink = rows([sinks[:, h:h + 1] for h in heads])
    return (_stack_heads(q, kh, grp), slope, sink, jnp.concatenate([dist] * grp, axis=0),
            jnp.concatenate([valid] * grp, axis=0))


def _swa_fwd(q, kk, vv, gq, sinks, n, qh):
    grp = qh // KV_HEADS
    lane = lax.broadcasted_iota(jnp.int32, (WINDOW, LANE), 1)
    outs, probs, sink_probs = [], [], jnp.zeros((WINDOW, LANE), F32)
    for kh in range(KV_HEADS):
        qs, slope, sink, dist, valid = _swa_group(q, kh, grp, n, qh, sinks)
        xh, _ = _rms(qs)
        p, ps = _swa_probs(_scaled_bf16(xh * gq), kk[:, _hs(kh)], dist, valid, slope, sink)
        p = p.astype(BF)
        probs.append(p)
        o = _dot(p, vv[:, _hs(kh)])
        for g in range(grp):
            outs.append(o[g * WINDOW:(g + 1) * WINDOW])
            sink_probs = jnp.where(lane == kh * grp + g, ps[g * WINDOW:(g + 1) * WINDOW], sink_probs)
    return jnp.concatenate(outs, axis=-1), probs, sink_probs


def _swa_bwd(q, do, kk, vv, gq, probs, sink_probs, qh):
    grp = qh // KV_HEADS
    lane = lax.broadcasted_iota(jnp.int32, (1, LANE), 1)
    dqs, dks, dvs, dgq, dsk = [], [], [], 0.0, jnp.zeros((1, LANE), F32)
    do = do.astype(BF)
    for kh in range(KV_HEADS):
        heads = range(kh * grp, (kh + 1) * grp)
        xh, r = _rms(_stack_heads(q, kh, grp))
        qn = _scaled_bf16(xh * gq)
        p_bf = probs[kh]
        p = p_bf.astype(F32)
        ps = jnp.concatenate([sink_probs[:, h:h + 1] for h in heads], axis=0)
        dos = _stack_heads(do, kh, grp)
        dp = _dot_nt(dos, vv[:, _hs(kh)])
        delta = jnp.sum(p * dp, axis=-1, keepdims=True)
        ds = (p * (dp - delta)).astype(BF)
        dsink = ps * delta
        for g in range(grp):
            part = -jnp.sum(dsink[g * WINDOW:(g + 1) * WINDOW], axis=0, keepdims=True)
            dsk = dsk + jnp.where(lane == kh * grp + g, part, 0.0)
        dq, dg = _rms_bwd(_dot(ds, kk[:, _hs(kh)]) * QK_SCALE, xh, r, gq)
        dqs += [dq[g * WINDOW:(g + 1) * WINDOW] for g in range(grp)]
        dgq = dgq + dg
        dks.append(_dot_tn(ds, qn))
        dvs.append(_dot_tn(p_bf, dos))
    cat = lambda xs: jnp.concatenate(xs, axis=-1)
    return cat(dqs), cat(dks), cat(dvs), dgq, dsk


def _mixer_pool_fwd(proj, mkv, pbd, scale, gq, gk, *, name, tm):
    s_len, d = proj.shape
    main = d - KVW
    gd = main // len(POOL_WINDOWS)
    mlen = mkv.shape[0]
    hb = tm // HALO

    def body(u_ref, halo_ref, mq_ref, mkv_ref, pbd_ref, scale_ref, gq_ref, gk_ref, o_ref, mp_ref, mk_s, mv_s):
        i = pl.program_id(0)

        @pl.when(i == 0)
        def _():
            mk, mv = _mem_kv(mkv_ref[...], gk_ref[...])
            mk_s[...] = mk
            mv_s[...] = mv

        halo = jnp.where(i > 0, halo_ref[...], 0.0)
        dif = _pool_diff(u_ref[...], halo, i * tm, gd)
        mixed = _dot(dif.astype(BF), pbd_ref[...]) * scale_ref[...]
        mem, mp_ref[...] = _mem_fwd(mq_ref[...], mk_s[...], mv_s[...], gq_ref[...])
        o_ref[...] = jnp.concatenate([mixed, mem], axis=-1).astype(BF)

    full = lambda shape: pl.BlockSpec(shape, lambda i: (0,) * len(shape))
    return _call(
        body, name, (s_len // tm,),
        [pl.BlockSpec((tm, main), lambda i: (i, 0)),
         pl.BlockSpec((HALO, main), lambda i: (jnp.maximum(i * hb - 1, 0), 0)),
         pl.BlockSpec((tm, KVW), lambda i: (i, main // KVW)),
         full((mlen, 2 * KVW)), full((main, main)), full((1, main)), full((1, HEAD)), full((1, HEAD))],
        [pl.BlockSpec((tm, d), lambda i: (i, 0)), pl.BlockSpec((tm, KV_HEADS * mlen), lambda i: (i, 0))],
        [SDS((s_len, d), BF), SDS((s_len, KV_HEADS * mlen), BF)],
        scratch=[pltpu.VMEM((mlen, KVW), BF), pltpu.VMEM((mlen, KVW), BF)],
        semantics=("arbitrary",))(proj, proj, proj, mkv, pbd, scale, gq, gk)


def _mixer_pool_bwd(proj, dcat, mkv, pbd, scale, gq, gk, mem_probs, *, name, tm):
    s_len, d = proj.shape
    main = d - KVW
    gd = main // len(POOL_WINDOWS)
    mlen = mkv.shape[0]
    hb = tm // HALO
    nt = s_len // tm
    last_halo = s_len // HALO - 1

    def body(u_ref, halo_ref, mq_ref, do_ref, donext_ref, dom_ref, mkv_ref, pbd_ref, scale_ref, gq_ref, gk_ref, mp_ref,
             dproj_ref, dpbd_ref, dscale_ref, dmkv_ref, dgq_ref, dgk_ref, mk_s, mv_s, dmk_s, dmv_s):
        i = pl.program_id(0)

        @pl.when(i == 0)
        def _():
            mk, mv = _mem_kv(mkv_ref[...], gk_ref[...])
            mk_s[...] = mk
            mv_s[...] = mv
            dmk_s[...] = jnp.zeros_like(dmk_s)
            dmv_s[...] = jnp.zeros_like(dmv_s)
            dpbd_ref[...] = jnp.zeros_like(dpbd_ref)
            dscale_ref[...] = jnp.zeros_like(dscale_ref)
            dgq_ref[...] = jnp.zeros_like(dgq_ref)

        pbd = pbd_ref[...]
        scale = scale_ref[...]
        halo = jnp.where(i > 0, halo_ref[...], 0.0)
        dif = _pool_diff(u_ref[...], halo, i * tm, gd).astype(BF)
        do = do_ref[...]
        dscale_ref[...] += jnp.sum(do * _dot(dif, pbd), axis=0, keepdims=True)
        dmixed = (do * scale).astype(BF)
        dpbd_ref[...] += _dot_tn(dif, dmixed)
        dd = _dot_nt(dmixed, pbd)
        donext = jnp.where(i < nt - 1, donext_ref[...], 0.0)
        dd_halo = _dot_nt((donext * scale).astype(BF), pbd)
        du = _pool_diff_bwd(dd, dd_halo, i * tm, gd)

        dmq, dmk, dmv, dgq = _mem_bwd(mq_ref[...], dom_ref[...], mk_s[...], mv_s[...], gq_ref[...], mp_ref[...])
        dmk_s[...] += dmk
        dmv_s[...] += dmv
        dgq_ref[...] += dgq
        dproj_ref[...] = jnp.concatenate([du, dmq], axis=-1).astype(BF)

        @pl.when(i == nt - 1)
        def _():
            dmkv, dgk = _mem_kv_bwd(mkv_ref[...], dmk_s[...], dmv_s[...], gk_ref[...])
            dmkv_ref[...] = dmkv
            dgk_ref[...] = dgk

    full = lambda shape: pl.BlockSpec(shape, lambda i: (0,) * len(shape))
    return _call(
        body, name, (nt,),
        [pl.BlockSpec((tm, main), lambda i: (i, 0)),
         pl.BlockSpec((HALO, main), lambda i: (jnp.maximum(i * hb - 1, 0), 0)),
         pl.BlockSpec((tm, KVW), lambda i: (i, main // KVW)),
         pl.BlockSpec((tm, main), lambda i: (i, 0)),
         pl.BlockSpec((HALO, main), lambda i: (jnp.minimum((i + 1) * hb, last_halo), 0)),
         pl.BlockSpec((tm, KVW), lambda i: (i, main // KVW)),
         full((mlen, 2 * KVW)), full((main, main)), full((1, main)), full((1, HEAD)), full((1, HEAD)),
         pl.BlockSpec((tm, KV_HEADS * mlen), lambda i: (i, 0))],
        [pl.BlockSpec((tm, d), lambda i: (i, 0)), full((main, main)), full((1, main)), full((mlen, 2 * KVW)),
         full((1, HEAD)), full((1, HEAD))],
        [SDS((s_len, d), BF), SDS((main, main), F32), SDS((1, main), F32), SDS((mlen, 2 * KVW), F32),
         SDS((1, HEAD), F32), SDS((1, HEAD), F32)],
        scratch=[pltpu.VMEM((mlen, KVW), BF), pltpu.VMEM((mlen, KVW), BF), pltpu.VMEM((mlen, KVW), F32),
                 pltpu.VMEM((mlen, KVW), F32)],
        semantics=("arbitrary",))(proj, proj, proj, dcat, dcat, dcat, mkv, pbd, scale, gq, gk, mem_probs)


def _mem_attn_fwd(proj, mkv, gq, gk, cat, *, name, tm):
    s_len, d = proj.shape
    main = d - KVW
    mlen = mkv.shape[0]

    def body(mq_ref, mkv_ref, gq_ref, gk_ref, _, o_ref, mp_ref, mk_s, mv_s):
        @pl.when(pl.program_id(0) == 0)
        def _():
            mk, mv = _mem_kv(mkv_ref[...], gk_ref[...])
            mk_s[...] = mk
            mv_s[...] = mv

        mem, mp_ref[...] = _mem_fwd(mq_ref[...], mk_s[...], mv_s[...], gq_ref[...])
        o_ref[...] = mem.astype(BF)

    full = lambda shape: pl.BlockSpec(shape, lambda i: (0,) * len(shape))
    memcol = lambda i: (i, main // KVW)
    return _call(
        body, name, (s_len // tm,),
        [pl.BlockSpec((tm, KVW), memcol), full((mlen, 2 * KVW)), full((1, HEAD)), full((1, HEAD)), ANY],
        [pl.BlockSpec((tm, KVW), memcol), pl.BlockSpec((tm, KV_HEADS * mlen), lambda i: (i, 0))],
        [SDS((s_len, d), BF), SDS((s_len, KV_HEADS * mlen), BF)],
        scratch=[pltpu.VMEM((mlen, KVW), BF), pltpu.VMEM((mlen, KVW), BF)],
        semantics=("arbitrary",), aliases={4: 0})(proj, mkv, gq, gk, cat)


def _mem_attn_bwd(proj, dcat, mkv, gq, gk, mem_probs, dproj, *, name, tm):
    s_len, d = proj.shape
    main = d - KVW
    mlen = mkv.shape[0]
    nt = s_len // tm

    def body(mq_ref, dom_ref, mkv_ref, gq_ref, gk_ref, mp_ref, _, dproj_ref, dmkv_ref, dgq_ref, dgk_ref,
             mk_s, mv_s, dmk_s, dmv_s):
        i = pl.program_id(0)

        @pl.when(i == 0)
        def _():
            mk, mv = _mem_kv(mkv_ref[...], gk_ref[...])
            mk_s[...] = mk
            mv_s[...] = mv
            dmk_s[...] = jnp.zeros_like(dmk_s)
            dmv_s[...] = jnp.zeros_like(dmv_s)
            dgq_ref[...] = jnp.zeros_like(dgq_ref)

        dmq, dmk, dmv, dgq = _mem_bwd(mq_ref[...], dom_ref[...], mk_s[...], mv_s[...], gq_ref[...], mp_ref[...])
        dmk_s[...] += dmk
        dmv_s[...] += dmv
        dgq_ref[...] += dgq
        dproj_ref[...] = dmq.astype(BF)

        @pl.when(i == nt - 1)
        def _():
            dmkv, dgk = _mem_kv_bwd(mkv_ref[...], dmk_s[...], dmv_s[...], gk_ref[...])
            dmkv_ref[...] = dmkv
            dgk_ref[...] = dgk

    full = lambda shape: pl.BlockSpec(shape, lambda i: (0,) * len(shape))
    memcol = lambda i: (i, main // KVW)
    return _call(
        body, name, (nt,),
        [pl.BlockSpec((tm, KVW), memcol), pl.BlockSpec((tm, KVW), memcol), full((mlen, 2 * KVW)), full((1, HEAD)),
         full((1, HEAD)), pl.BlockSpec((tm, KV_HEADS * mlen), lambda i: (i, 0)), ANY],
        [pl.BlockSpec((tm, KVW), memcol), full((mlen, 2 * KVW)), full((1, HEAD)), full((1, HEAD))],
        [SDS((s_len, d), BF), SDS((mlen, 2 * KVW), F32), SDS((1, HEAD), F32), SDS((1, HEAD), F32)],
        scratch=[pltpu.VMEM((mlen, KVW), BF), pltpu.VMEM((mlen, KVW), BF), pltpu.VMEM((mlen, KVW), F32),
                 pltpu.VMEM((mlen, KVW), F32)],
        semantics=("arbitrary",), aliases={6: 0})(proj, dcat, mkv, gq, gk, mem_probs, dproj)


def _mixer_swa_fwd(proj, kn, v, gqs, sinks, *, name):
    s_len, d = proj.shape
    main = d - KVW
    qh = main // HEAD
    tm = WINDOW
    prow = qh // KV_HEADS * tm

    def body(q_ref, kp_ref, kc_ref, vp_ref, vc_ref, gqs_ref, sinks_ref, o_ref, p_ref, ps_ref):
        n = pl.program_id(0)
        kk = jnp.concatenate([kp_ref[...], kc_ref[...]], axis=0)
        vv = jnp.concatenate([vp_ref[...], vc_ref[...]], axis=0)
        att, probs, sink_probs = _swa_fwd(q_ref[...], kk, vv, gqs_ref[...], sinks_ref[...], n, qh)
        for kh in range(KV_HEADS):
            p_ref[0, kh] = probs[kh]
        ps_ref[...] = sink_probs
        o_ref[...] = att.astype(BF)

    full = lambda shape: pl.BlockSpec(shape, lambda i: (0,) * len(shape))
    prev = lambda i: (jnp.maximum(i - 1, 0), 0)
    cur = lambda i: (i, 0)
    return _call(
        body, name, (s_len // tm,),
        [pl.BlockSpec((tm, main), cur), pl.BlockSpec((tm, KVW), prev), pl.BlockSpec((tm, KVW), cur),
         pl.BlockSpec((tm, KVW), prev), pl.BlockSpec((tm, KVW), cur), full((1, HEAD)), full((1, LANE))],
        [pl.BlockSpec((tm, main), cur), pl.BlockSpec((1, KV_HEADS, prow, 2 * tm), lambda i: (i, 0, 0, 0)),
         pl.BlockSpec((tm, LANE), cur)],
        [SDS((s_len, d), BF), SDS((s_len // tm, KV_HEADS, prow, 2 * tm), BF), SDS((s_len, LANE), F32)],
        semantics=("parallel",))(proj, kn, kn, v, v, gqs, sinks)


def _mixer_swa_bwd(proj, dcat, kn, v, gqs, probs, sink_probs, *, name):
    s_len, d = proj.shape
    main = d - KVW
    qh = main // HEAD
    tm = WINDOW
    nt = s_len // tm
    prow = qh // KV_HEADS * tm

    def body(q_ref, do_ref, kp_ref, kc_ref, vp_ref, vc_ref, gqs_ref, p_ref, ps_ref,
             dproj_ref, dk_ref, dv_ref, dgqs_ref, dsinks_ref):
        n = pl.program_id(0)

        @pl.when(n == 0)
        def _():
            dk_ref[...] = jnp.zeros_like(dk_ref)
            dv_ref[...] = jnp.zeros_like(dv_ref)
            dgqs_ref[...] = jnp.zeros_like(dgqs_ref)
            dsinks_ref[...] = jnp.zeros_like(dsinks_ref)

        kk = jnp.concatenate([kp_ref[...], kc_ref[...]], axis=0)
        vv = jnp.concatenate([vp_ref[...], vc_ref[...]], axis=0)
        dq, dkk, dvv, dgqs, dsk = _swa_bwd(q_ref[...], do_ref[...], kk, vv, gqs_ref[...],
                                           [p_ref[0, kh] for kh in range(KV_HEADS)], ps_ref[...], qh)
        prev = pl.ds(pl.multiple_of(jnp.maximum(n - 1, 0) * tm, tm), tm)
        own = pl.ds(pl.multiple_of(n * tm, tm), tm)
        dk_ref[prev, :] += dkk[:tm]
        dk_ref[own, :] += dkk[tm:]
        dv_ref[prev, :] += dvv[:tm]
        dv_ref[own, :] += dvv[tm:]
        dgqs_ref[...] += dgqs
        dsinks_ref[...] += dsk
        dproj_ref[...] = dq.astype(BF)

    full = lambda shape: pl.BlockSpec(shape, lambda i: (0,) * len(shape))
    prev_b = lambda i: (jnp.maximum(i - 1, 0), 0)
    cur = lambda i: (i, 0)
    return _call(
        body, name, (nt,),
        [pl.BlockSpec((tm, main), cur), pl.BlockSpec((tm, main), cur),
         pl.BlockSpec((tm, KVW), prev_b), pl.BlockSpec((tm, KVW), cur),
         pl.BlockSpec((tm, KVW), prev_b), pl.BlockSpec((tm, KVW), cur),
         full((1, HEAD)), pl.BlockSpec((1, KV_HEADS, prow, 2 * tm), lambda i: (i, 0, 0, 0)),
         pl.BlockSpec((tm, LANE), cur)],
        [pl.BlockSpec((tm, main), cur), full((s_len, KVW)), full((s_len, KVW)), full((1, HEAD)), full((1, LANE))],
        [SDS((s_len, d), BF), SDS((s_len, KVW), F32), SDS((s_len, KVW), F32), SDS((1, HEAD), F32),
         SDS((1, LANE), F32)],
        semantics=("arbitrary",))(proj, dcat, kn, kn, v, v, gqs, probs, sink_probs)


def _kv_prep(kv, gk, *, tm):
    s_len = kv.shape[0]

    def body(kv_ref, gk_ref, k_ref, v_ref):
        k, v = _mem_kv(kv_ref[...], gk_ref[...])
        k_ref[...] = k
        v_ref[...] = v

    row = lambda i: (i, 0)
    return _call(
        body, "kv_prep", (s_len // tm,),
        [pl.BlockSpec((tm, 2 * KVW), row), pl.BlockSpec((1, HEAD), lambda i: (0, 0))],
        [pl.BlockSpec((tm, KVW), row), pl.BlockSpec((tm, KVW), row)],
        [SDS((s_len, KVW), BF), SDS((s_len, KVW), BF)], semantics=("parallel",))(kv, gk)


def _kv_bwd(kv, dks, dvs, gk, *, tm):
    s_len = kv.shape[0]
    nl = len(dks)

    def body(*refs):
        kv_ref, gk_ref = refs[0], refs[1]
        dk_refs, dv_refs = refs[2:2 + nl], refs[2 + nl:2 + 2 * nl]
        dkv_ref, dgk_ref = refs[2 + 2 * nl], refs[3 + 2 * nl]
        dk, dv = dk_refs[0][...], dv_refs[0][...]
        for t in range(1, nl):
            dk = dk + dk_refs[t][...]
            dv = dv + dv_refs[t][...]
        dkv, dgk = _mem_kv_bwd(kv_ref[...], dk, dv, gk_ref[...])
        dkv_ref[...] = dkv.astype(BF)

        @pl.when(pl.program_id(0) == 0)
        def _():
            dgk_ref[...] = jnp.zeros_like(dgk_ref)

        dgk_ref[...] += dgk

    row = lambda i: (i, 0)
    one = pl.BlockSpec((1, HEAD), lambda i: (0, 0))
    return _call(
        body, "kv_bwd", (s_len // tm,),
        [pl.BlockSpec((tm, 2 * KVW), row), one] + [pl.BlockSpec((tm, KVW), row)] * (2 * nl),
        [pl.BlockSpec((tm, 2 * KVW), row), one],
        [SDS((s_len, 2 * KVW), BF), SDS((1, HEAD), F32)], semantics=("arbitrary",))(kv, gk, *dks, *dvs)


def _place():
    x, y, c = lax.axis_index("x"), lax.axis_index("y"), lax.axis_index("c")
    flips = [(1 - x, y), (x, 1 - y), (1 - x, 1 - y)]
    return x, y, c, flips


def _remote(src, dst, send_sem, recv_sem, to):
    return pltpu.make_async_remote_copy(src_ref=src, dst_ref=dst, send_sem=send_sem, recv_sem=recv_sem,
                                        device_id=to, device_id_type=MESH)


def _gather_copies(p_refs, wg_refs, send, recv):
    x, y, c, flips = _place()
    chip = 2 * x + y
    cps = []
    for j, (fx, fy) in enumerate(flips):
        for b in range(2):
            half = p_refs[b].shape[0] // 2
            mine = pl.ds(c * half, half)
            cps.append(_remote(p_refs[b].at[mine, :], wg_refs[b].at[chip, mine, :], send.at[2 * j + b],
                               recv.at[2 * j + b], (fx, fy, c)))
    return cps, cps


def _forward_copies(p_refs, wg_refs, send, recv):
    x, y, c, flips = _place()
    chip = 2 * x + y
    sib = (x, y, 1 - c)
    sends, arrivals = [], []
    for b in range(2):
        half = p_refs[b].shape[0] // 2
        own = _remote(p_refs[b], wg_refs[b].at[chip], send.at[b], recv.at[b], sib)
        sends.append(own)
        arrivals.append(own)
        for j, (fx, fy) in enumerate(flips):
            k = 2 + 3 * b + j
            landed = wg_refs[b].at[2 * fx + fy, pl.ds(c * half, half), :]
            other = wg_refs[b].at[2 * fx + fy, pl.ds((1 - c) * half, half), :]
            sends.append(_remote(landed, landed, send.at[k], recv.at[k], sib))
            arrivals.append(_remote(other, other, send.at[k], recv.at[k], sib))
    return sends, arrivals


def _swap_copies(g_refs, r_refs, send, recv):
    x, y, c, _ = _place()
    cps = []
    for b in range(2):
        half = g_refs[b].shape[1] // 2
        cps.append(_remote(g_refs[b].at[:, pl.ds((1 - c) * half, half), :], r_refs[b], send.at[b], recv.at[b],
                           (x, y, 1 - c)))
    return cps, cps


def _split_start(make_copies, n_sems, bufs, after, fresh, *, name):
    def body(a1, a2, b1, b2, after_ref, send, recv, *outs):
        for cp in make_copies((a1, a2), (b1, b2), send, recv)[0]:
            cp.start()
        outs[4][...] = jnp.zeros_like(outs[4])

    extra_shape = () if fresh is None else (pltpu.HBM(fresh, F32),)
    extra_spec = () if fresh is None else (HBM,)
    return pl.pallas_call(
        body, name=name,
        out_shape=(pltpu.SemaphoreType.DMA((n_sems,)), pltpu.SemaphoreType.DMA((n_sems,)))
        + tuple(pltpu.HBM(b.shape, b.dtype) for b in bufs) + (SDS((8, LANE), F32),) + extra_shape,
        in_specs=(HBM, HBM, HBM, HBM, ANY), out_specs=(SEM, SEM, HBM, HBM, HBM, HBM, VMEM_WHOLE) + extra_spec,
        input_output_aliases={0: 2, 1: 3, 2: 4, 3: 5},
        compiler_params=pltpu.CompilerParams(has_side_effects=SIDE_EFFECT))(*[_in_hbm(b) for b in bufs], after)


def _split_wait(make_copies, started, after, *, name):
    send, recv, bufs = started[0], started[1], started[2:6]

    def body(a1, a2, b1, b2, send_ref, recv_ref, after_ref, *outs):
        sends, arrivals = make_copies((a1, a2), (b1, b2), send_ref, recv_ref)
        for cp in arrivals:
            cp.wait_recv()
        for cp in sends:
            cp.wait_send()

    return pl.pallas_call(
        body, name=name, out_shape=tuple(pltpu.HBM(b.shape, b.dtype) for b in bufs),
        in_specs=(HBM, HBM, HBM, HBM, SEM, SEM, ANY), out_specs=(HBM, HBM, HBM, HBM),
        input_output_aliases={0: 0, 1: 1, 2: 2, 3: 3},
        compiler_params=pltpu.CompilerParams(has_side_effects=SIDE_EFFECT))(*bufs, send, recv, after)


def _gather_small(ps):
    def body(ps_ref, o_ref, send, recv):
        x, y, c, flips = _place()
        chip = 2 * x + y
        o_ref[chip] = ps_ref[...]
        cps = [_remote(ps_ref, o_ref.at[chip], send.at[j], recv.at[j], (fx, fy, c))
               for j, (fx, fy) in enumerate(flips)]
        for cp in cps:
            cp.start()
        for j, (fx, fy) in enumerate(flips):
            _remote(ps_ref, o_ref.at[2 * fx + fy], send.at[j], recv.at[j], (fx, fy, c)).wait_recv()
        for cp in cps:
            cp.wait_send()

    return pl.pallas_call(
        body, name="gather_small", in_specs=[VMEM_WHOLE], out_specs=VMEM_WHOLE,
        out_shape=SDS((N_CHIPS,) + ps.shape, ps.dtype),
        scratch_shapes=[pltpu.SemaphoreType.DMA((3,)), pltpu.SemaphoreType.DMA((3,))])(ps)


def _sum_sibling(g, r, place, *, tm, name):
    n_sh, half, w = r.shape
    nt = half // tm

    def body(place_ref, g_ref, r_ref, pbf_ref, own_ref):
        s = pl.program_id(1)
        p = g_ref[0] + r_ref[0]
        pbf_ref[0] = p.astype(BF)

        @pl.when(s == place_ref[1])
        def _():
            own_ref[...] = p

    return _call(
        body, name, (nt, n_sh),
        [pl.BlockSpec((1, tm, w), lambda i, s, pr: (s, pr[0] * nt + i, 0)),
         pl.BlockSpec((1, tm, w), lambda i, s, pr: (s, i, 0))],
        [pl.BlockSpec((1, tm, w), lambda i, s, pr: (s, i, 0)), pl.BlockSpec((tm, w), lambda i, s, pr: (i, 0))],
        [SDS((n_sh, half, w), BF), SDS((half, w), F32)],
        semantics=("arbitrary", "arbitrary"), prefetch=1)(place, g, r)


def _rs_copies(p_refs, land_refs, send, recv):
    _, _, c, flips = _place()
    cps = []
    for j, (fx, fy) in enumerate(flips):
        for b in range(2):
            cps.append(_remote(p_refs[b].at[2 * fx + fy], land_refs[b].at[j], send.at[2 * j + b], recv.at[2 * j + b],
                               (fx, fy, c)))
    return cps, cps


def _sum_chips(own, r, full, layer, place, *, tm, name):
    half, w = own.shape

    def body(place_ref, own_ref, r_ref, _, o_ref):
        o_ref[0, 0] = ((own_ref[...] + r_ref[0].astype(F32)) + r_ref[1].astype(F32)) + r_ref[2].astype(F32)

    return _call(
        body, name, (half // tm,),
        [pl.BlockSpec((tm, w), lambda i, pr: (i, 0)), pl.BlockSpec((3, tm, w), lambda i, pr: (0, i, 0)), ANY],
        pl.BlockSpec((1, 1, tm, w), lambda i, pr: (layer, pr[0], i, 0)), SDS(full.shape, F32),
        semantics=("parallel",), prefetch=1, aliases={3: 0})(place, own, r, full)


def _share_with_sibling(f1, f2):
    def body(_, __, o1_ref, o2_ref, send, recv):
        x, y, c, _ = _place()
        mine, other = pl.ds(c, 1), pl.ds(1 - c, 1)
        cps = [_remote(o1_ref.at[:, mine], o1_ref.at[:, mine], send.at[0], recv.at[0], (x, y, 1 - c)),
               _remote(o2_ref.at[:, mine], o2_ref.at[:, mine], send.at[1], recv.at[1], (x, y, 1 - c))]
        for cp in cps:
            cp.start()
        for cp in cps:
            cp.wait_send()
        _remote(o1_ref.at[:, other], o1_ref.at[:, other], send.at[0], recv.at[0], (x, y, 1 - c)).wait_recv()
        _remote(o2_ref.at[:, other], o2_ref.at[:, other], send.at[1], recv.at[1], (x, y, 1 - c)).wait_recv()

    return pl.pallas_call(
        body, name="share_with_sibling", in_specs=[ANY, ANY], out_specs=[ANY, ANY],
        out_shape=[SDS(f1.shape, f1.dtype), SDS(f2.shape, f2.dtype)], input_output_aliases={0: 0, 1: 1},
        scratch_shapes=[pltpu.SemaphoreType.DMA((2,)), pltpu.SemaphoreType.DMA((2,))])(f1, f2)


def _allreduce_small(sg):
    rows, w = sg.shape
    half = rows // 2
    assert half % 8 == 0

    def body(sg_ref, o_ref, sib_buf, part, slots, send, recv):
        x, y, c, flips = _place()
        chip = 2 * x + y
        sib = (x, y, 1 - c)
        mine = pl.ds(pl.multiple_of(c * half, 8), half)
        other = pl.ds(pl.multiple_of((1 - c) * half, 8), half)
        to_sib = _remote(sg_ref.at[other, :], sib_buf, send.at[0], recv.at[0], sib)
        to_sib.start()
        to_sib.wait_recv()
        part[...] = sg_ref[mine, :] + sib_buf[...]
        slots[chip] = part[...]
        to_chips = [_remote(part, slots.at[chip], send.at[1 + j], recv.at[1 + j], (fx, fy, c))
                    for j, (fx, fy) in enumerate(flips)]
        for cp in to_chips:
            cp.start()
        for j, (fx, fy) in enumerate(flips):
            _remote(part, slots.at[2 * fx + fy], send.at[1 + j], recv.at[1 + j], (fx, fy, c)).wait_recv()
        o_ref[mine, :] = ((slots[0] + slots[1]) + slots[2]) + slots[3]
        back = _remote(o_ref.at[mine, :], o_ref.at[mine, :], send.at[4], recv.at[4], sib)
        back.start()
        _remote(o_ref.at[other, :], o_ref.at[other, :], send.at[4], recv.at[4], sib).wait_recv()
        for cp in [to_sib, back] + to_chips:
            cp.wait_send()

    return pl.pallas_call(
        body, name="allreduce_small", in_specs=[VMEM_WHOLE], out_specs=VMEM_WHOLE, out_shape=SDS((rows, w), F32),
        scratch_shapes=[pltpu.VMEM((half, w), F32), pltpu.VMEM((half, w), F32), pltpu.VMEM((N_CHIPS, half, w), F32),
                        pltpu.SemaphoreType.DMA((5,)), pltpu.SemaphoreType.DMA((5,))])(sg)


def _adamw(g_arr, layer0, g_off, per_layer, w, m, v, *, name, tm):
    rows, cols = w.shape
    assert g_off % tm == 0 and per_layer % tm == 0 and rows % per_layer == 0
    npl = per_layer // tm
    c1 = 1.0 - ADAM_B1 ** ADAM_STEP
    c2 = 1.0 - ADAM_B2 ** ADAM_STEP

    def body(g_ref, w_ref, m_ref, v_ref, go_ref, d_ref, mo_ref, vo_ref):
        g = g_ref[0]
        mn = ADAM_B1 * m_ref[...] + (1.0 - ADAM_B1) * g
        vn = ADAM_B2 * v_ref[...] + (1.0 - ADAM_B2) * (g * g)
        go_ref[...] = g
        mo_ref[...] = mn
        vo_ref[...] = vn
        d_ref[...] = -ADAM_LR * ((mn / c1) / (jnp.sqrt(vn / c2) + ADAM_EPS) + ADAM_WD * w_ref[...])

    blk = pl.BlockSpec((tm, cols), lambda i: (i, 0))
    return _call(
        body, name, (rows // tm,),
        [pl.BlockSpec((1, tm, cols), lambda i: (layer0 + i // npl, g_off // tm + i % npl, 0)), blk, blk, blk],
        [blk] * 4,
        [SDS((rows, cols), F32)] * 4, semantics=("parallel",))(g_arr, w, m, v)


def _pack_small(parts, width):
    flat = jnp.concatenate([p.reshape(-1).astype(F32) for p in parts])
    rows = -(-flat.shape[0] // (16 * width)) * 16
    return jnp.pad(flat, (0, rows * width - flat.shape[0])).reshape(rows, width)


def _unpack_small(packed, shapes):
    flat = packed.reshape(-1)
    out, off = [], 0
    for shp in shapes:
        size = 1
        for n in shp:
            size *= n
        out.append(flat[off:off + size].reshape(shp))
        off += size
    return out


def _block_diag(pw):
    g, c, _ = pw.shape
    eye = jnp.eye(g, dtype=pw.dtype)
    return (eye[:, None, :, None] * pw[:, :, None, :]).reshape(g * c, g * c)


def _diag_blocks(full, g):
    c = full.shape[0] // g
    return jnp.stack([full[i * c:(i + 1) * c, i * c:(i + 1) * c] for i in range(g)])


def kernel(x, mem, norm_mix, w_in, pool_w, pool_scale, kv_norm, w_kv, k_norm, q_norm, sinks, mem_norm, w_mem_kv, mem_q_norm, mem_k_norm, w_out, norm_mlp, w_up, w_down, loss_target, m_norm_mix, m_w_in, m_pool_w, m_pool_scale, m_kv_norm, m_w_kv, m_k_norm, m_q_norm, m_sinks, m_mem_norm, m_w_mem_kv, m_mem_q_norm, m_mem_k_norm, m_w_out, m_norm_mlp, m_w_up, m_w_down, v_norm_mix, v_w_in, v_pool_w, v_pool_scale, v_kv_norm, v_w_kv, v_k_norm, v_q_norm, v_sinks, v_mem_norm, v_w_mem_kv, v_mem_q_norm, v_mem_k_norm, v_w_out, v_norm_mlp, v_w_up, v_w_down):
    s_len, d = x.shape[1], x.shape[2]
    n_layers, n_pool = norm_mix.shape[0], pool_w.shape[0]
    n_swa = n_layers - n_pool
    main = d - KVW
    qh = main // HEAD
    ff = w_down.shape[1] * N_CHIPS
    dq = d // N_CHIPS
    assert w_up.shape[2] == d and ff == N_CHIPS * d and w_kv.shape[1] == 2 * KVW
    tm = min(512, s_len)
    tm_mem = mem.shape[1]

    cx, cy, cc = lax.axis_index("x"), lax.axis_index("y"), lax.axis_index("c")
    chip = 2 * cx + cy
    place = jnp.stack([cc, chip]).astype(jnp.int32)

    off_down, off_up, off_in, off_out = 0, d, 2 * d, 2 * d + dq
    rows1 = off_out + dq
    off_mkv, off_kv = 0, dq
    rows2 = 2 * dq

    ps = jnp.pad(pool_scale, ((0, 8 - n_pool), (0, 2 * LANE - pool_scale.shape[1])))
    psg = _gather_small(ps)
    pool_scale_full = jnp.concatenate([psg[k, :n_pool, :pool_scale.shape[1]] for k in range(N_CHIPS)], axis=1)

    def packed_weights(l):
        p1 = jnp.concatenate([w_down[l], w_up[l], w_in[l], w_out[l]]).astype(BF)
        p2 = jnp.concatenate([w_mem_kv[l], w_kv] if l == n_pool else [w_mem_kv[l]]).astype(BF)
        return p1, p2

    def gather_start(l, after):
        p1, p2 = packed_weights(l)
        bufs = (p1, p2, lax.empty((N_CHIPS,) + p1.shape, BF), lax.empty((N_CHIPS,) + p2.shape, BF))
        return _split_start(_gather_copies, 6, bufs, after, None, name=f"gather_start_{l}")

    def gather_land(l, started, after):
        bufs = _split_wait(_gather_copies, started, after, name=f"gather_wait_{l}")
        return _split_start(_forward_copies, 8, bufs, place, None, name=f"forward_start_{l}")

    def gather_finish(l, forwarding, after):
        bufs = _split_wait(_forward_copies, forwarding, after, name=f"forward_wait_{l}")
        return bufs[2], bufs[3]

    def w_rows(arr, off, nrows, width):
        assert off % nrows == 0
        return (arr, (N_CHIPS, nrows, width), lambda j: (0, off // nrows, 0))

    row = lambda a: a.reshape(1, -1)
    h = x.reshape(s_len, d)
    memx = mem.reshape(tm_mem, d)
    tgt = loss_target.reshape(s_len, d)
    pbd = [_block_diag(pool_w[l]).astype(BF) for l in range(n_pool)]
    sinks_pad = [jnp.pad(row(sinks[j]), ((0, 0), (0, LANE - qh))) for j in range(n_swa)]

    w_in_l, w_out_l, w_down_l, w_up_all_l, w_mkv_l = [], [], [], [], []
    w_kv_g = None
    forwarding = gather_land(0, gather_start(0, psg), psg)
    travelling = gather_start(1, forwarding[6]) if n_layers > 1 else None
    saved, probs, sink_probs, mem_probs = [], {}, {}, {}
    kv = hn_kv = kn = vsh = None
    for l in range(n_layers):
        wg1, wg2 = gather_finish(l, forwarding, h if l else forwarding[6])
        w_in_l.append(w_rows(wg1, off_in, dq, d))
        w_out_l.append(w_rows(wg1, off_out, dq, d))
        w_down_l.append(w_rows(wg1, off_down, d, d))
        w_up_all_l.append((wg1, (N_CHIPS, d, d), lambda j: (0, off_up // d, 0)))
        w_mkv_l.append(w_rows(wg2, off_mkv, dq, 2 * KVW))
        g_mix = row(norm_mix[l])
        if travelling is not None:
            g_mix = g_mix + travelling[6][0, 0]
        if l == n_pool:
            w_kv_g = w_rows(wg2, off_kv, dq, 2 * KVW)
            kv, hn_kv = _norm_mm(h, row(kv_norm), w_kv_g, 1, 2 * KVW, act=False, name="kv_proj", tm=tm)
            kn, vsh = _kv_prep(kv, row(k_norm), tm=tm)
        h0 = h
        proj, xn = _norm_mm(h0, g_mix, w_in_l[l], 1, d, act=False, name=f"in_proj_{l}", tm=tm)
        mkv, memn = _norm_mm(memx, row(mem_norm[l]), w_mkv_l[l], 1, 2 * KVW, act=False, name=f"mem_kv_{l}", tm=tm_mem)
        if l < n_pool:
            cat, mem_probs[l] = _mixer_pool_fwd(proj, mkv, pbd[l], row(pool_scale_full[l]), row(mem_q_norm[l]),
                                                row(mem_k_norm[l]), name=f"mixer_fwd_{l}", tm=tm)
        else:
            j = l - n_pool
            cat, probs[l], sink_probs[l] = _mixer_swa_fwd(proj, kn, vsh, row(q_norm[j]), sinks_pad[j],
                                                          name=f"mixer_fwd_{l}")
            cat, mem_probs[l] = _mem_attn_fwd(proj, mkv, row(mem_q_norm[l]), row(mem_k_norm[l]), cat,
                                              name=f"mem_attn_fwd_{l}", tm=tm)
        h1 = _mm_res(h0, cat, w_out_l[l], name=f"out_proj_{l}", tm=tm)
        hh, xm = _norm_mm(h1, row(norm_mlp[l]), w_up_all_l[l], N_CHIPS, d, act=True, name=f"mlp_up_{l}", tm=tm)
        after = None
        if travelling is not None:
            forwarding = gather_land(l + 1, travelling, hh)
            travelling = gather_start(l + 2, forwarding[6]) if l + 2 < n_layers else None
            after = forwarding[6]
        h = _mm_res(h1, hh, w_down_l[l], name=f"mlp_down_{l}", tm=tm, after=after)
        saved.append((h0, proj, xn, mkv, memn, cat, h1, hh, xm))

    dh, dh_bf, loss_part = _loss_head(h, tgt, tm=tm)
    loss = lax.psum(loss_part[0, 0], ("x", "y", "c"))

    half1, half2 = rows1 // 2, rows2 // 2
    g1 = lax.empty((N_CHIPS, rows1, d), F32)
    pending = {}
    swapping = None
    tk = min(512, d)

    def reduce_begin(l, swapped, after):
        g1_l, g2_l, r1, r2 = _split_wait(_swap_copies, swapped, after, name=f"swap_wait_{l}")
        pb1, own1 = _sum_sibling(g1_l, r1, place, tm=_tile(half1, 256), name=f"sum_sibling_a_{l}")
        pb2, own2 = _sum_sibling(g2_l, r2, place, tm=_tile(half2, 256), name=f"sum_sibling_b_{l}")
        bufs = (pb1, pb2, lax.empty((3, half1, d), BF), lax.empty((3, half2, 2 * KVW), BF))
        return _split_start(_rs_copies, 6, bufs, place, None, name=f"reduce_start_{l}"), own1, own2
    zeros_mem = jnp.zeros((tm_mem, d), F32)

    def rows_map(off, nrows, tkk):
        per = nrows // tkk
        return lambda i, j: (i // per, off // tkk + i % per, 0)

    def cols_map(off, tkk):
        return lambda i, j: (j, off // tkk + i, 0)

    d_norm_mix, d_norm_mlp, d_mem_norm = [None] * n_layers, [None] * n_layers, [None] * n_layers
    d_mem_q, d_mem_k = [None] * n_layers, [None] * n_layers
    d_pool_w, d_pool_scale = [None] * n_pool, [None] * n_pool
    d_q_norm, d_sinks = [None] * n_swa, [None] * n_swa
    dks, dvs = [], []
    d_kv_norm = d_k_norm = None
    for l in reversed(range(n_layers)):
        h0, proj, xn, mkv, memn, cat, h1, hh, xm = saved[l]
        g2 = jnp.zeros((N_CHIPS, rows2, 2 * KVW), F32)
        g1 = _mm_tn(hh, dh_bf, g1, rows_map(off_down, d, tk), tk, d, name=f"dw_down_{l}")
        du = _mm_nt_relu2(dh_bf, hh, w_down_l[l], N_CHIPS, name=f"d_mlp_act_{l}", tm=tm)
        g1 = _mm_tn(xm, du, g1, cols_map(off_up, tk), tk, d, name=f"dw_up_{l}")
        g_mlp = row(norm_mlp[l])
        if swapping is not None:
            pending[swapping[0]] = reduce_begin(*swapping, after=g1)
            g_mlp = g_mlp + pending[swapping[0]][0][6][0, 0]
        dh1, dh1_bf, d_norm_mlp[l] = _mm_nt_normbwd(du, w_up_all_l[l], N_CHIPS, h1, g_mlp, dh,
                                                    name=f"d_mlp_in_{l}", tm=tm)
        tkq = min(tk, dq)
        g1 = _mm_tn(cat, dh1_bf, g1, rows_map(off_out, dq, tkq), tkq, d, name=f"dw_out_{l}")
        dcat = _mm_nt(dh1_bf, w_out_l[l], d, name=f"d_cat_{l}", tm=tm)
        if l < n_pool:
            dproj, dpbd, dscale, dmkv, d_mem_q[l], d_mem_k[l] = _mixer_pool_bwd(
                proj, dcat, mkv, pbd[l], row(pool_scale_full[l]), row(mem_q_norm[l]), row(mem_k_norm[l]),
                mem_probs[l], name=f"mixer_bwd_{l}", tm=tm)
            d_pool_w[l] = _diag_blocks(dpbd, len(POOL_WINDOWS))
            d_pool_scale[l] = dscale
        else:
            j = l - n_pool
            dproj, dk, dv, d_q_norm[j], dsk = _mixer_swa_bwd(proj, dcat, kn, vsh, row(q_norm[j]), probs[l],
                                                             sink_probs[l], name=f"mixer_bwd_{l}")
            dproj, dmkv, d_mem_q[l], d_mem_k[l] = _mem_attn_bwd(
                proj, dcat, mkv, row(mem_q_norm[l]), row(mem_k_norm[l]), mem_probs[l], dproj,
                name=f"mem_attn_bwd_{l}", tm=tm)
            d_sinks[j] = dsk[0, :qh]
            dks.append(dk)
            dvs.append(dv)
        g1 = _mm_tn(xn, dproj, g1, rows_map(off_in, dq, tkq), tkq, d, name=f"dw_in_{l}")
        dh, dh_bf, d_norm_mix[l] = _mm_nt_normbwd(dproj, w_in_l[l], 1, h0, row(norm_mix[l]), dh1,
                                                  name=f"d_in_{l}", tm=tm)
        g2 = _mm_tn(memn, dmkv, g2, rows_map(off_mkv, dq, tkq), tkq, 2 * KVW, name=f"dw_mem_kv_{l}")
        _, _, d_mem_norm[l] = _mm_nt_normbwd(dmkv, w_mkv_l[l], 1, memx, row(mem_norm[l]), zeros_mem,
                                             name=f"d_mem_norm_{l}", tm=tm_mem)
        if l == n_pool:
            dkv, d_k_norm = _kv_bwd(kv, dks, dvs, row(k_norm), tm=tm)
            g2 = _mm_tn(hn_kv, dkv, g2, rows_map(off_kv, dq, tkq), tkq, 2 * KVW, name="dw_kv")
            dh, dh_bf, d_kv_norm = _mm_nt_normbwd(dkv, w_kv_g, 1, h0, row(kv_norm), dh, name="d_kv_in", tm=tm)
        bufs = (g1, g2, lax.empty((N_CHIPS, half1, d), F32), lax.empty((N_CHIPS, half2, 2 * KVW), F32))
        swapping = (l, _split_start(_swap_copies, 2, bufs, place, (N_CHIPS, rows1, d) if l > 0 else None,
                                    name=f"swap_start_{l}"))
        g1 = swapping[1][7] if l > 0 else None
    grad_x = dh.reshape(x.shape)

    small_names = ["norm_mix", "pool_w", "pool_scale", "kv_norm", "k_norm", "q_norm", "sinks", "mem_norm",
                   "mem_q_norm", "mem_k_norm", "norm_mlp"]
    small_grads = {
        "norm_mix": jnp.concatenate(d_norm_mix), "pool_w": jnp.stack(d_pool_w),
        "pool_scale": jnp.concatenate(d_pool_scale), "kv_norm": d_kv_norm[0], "k_norm": d_k_norm[0],
        "q_norm": jnp.concatenate(d_q_norm), "sinks": jnp.stack(d_sinks), "mem_norm": jnp.concatenate(d_mem_norm),
        "mem_q_norm": jnp.concatenate(d_mem_q), "mem_k_norm": jnp.concatenate(d_mem_k),
        "norm_mlp": jnp.concatenate(d_norm_mlp)}
    width = d
    sg = _pack_small([small_grads[n] for n in small_names], width)
    sg = sg + swapping[1][6][0, 0]
    sg = _allreduce_small(sg)
    pending[0] = reduce_begin(*swapping, after=sg)
    reduced = dict(zip(small_names, _unpack_small(sg, [small_grads[n].shape for n in small_names])))
    psw = pool_scale.shape[1]
    reduced["pool_scale"] = lax.dynamic_slice_in_dim(reduced["pool_scale"], chip * psw, psw, axis=1)
    params = dict(norm_mix=(norm_mix, m_norm_mix, v_norm_mix), pool_w=(pool_w, m_pool_w, v_pool_w),
                  pool_scale=(pool_scale, m_pool_scale, v_pool_scale), kv_norm=(kv_norm, m_kv_norm, v_kv_norm),
                  k_norm=(k_norm, m_k_norm, v_k_norm), q_norm=(q_norm, m_q_norm, v_q_norm),
                  sinks=(sinks, m_sinks, v_sinks), mem_norm=(mem_norm, m_mem_norm, v_mem_norm),
                  mem_q_norm=(mem_q_norm, m_mem_q_norm, v_mem_q_norm),
                  mem_k_norm=(mem_k_norm, m_mem_k_norm, v_mem_k_norm), norm_mlp=(norm_mlp, m_norm_mlp, v_norm_mlp))
    shapes = [params[n][0].shape for n in small_names]
    packs = [_pack_small([reduced[n].reshape(params[n][0].shape) for n in small_names], width)
             + pending[0][0][6][0, 0]]
    packs += [_pack_small([params[n][t] for n in small_names], width) for t in range(3)]
    res = _adamw(packs[0][None], 0, 0, packs[0].shape[0], packs[1], packs[2], packs[3], name="adamw_small", tm=8)
    small = {n: [] for n in small_names}
    for r in res:
        for n, a in zip(small_names, _unpack_small(r, shapes)):
            small[n].append(a)

    full1 = lax.empty((n_layers, 2, half1, d), F32)
    full2 = lax.empty((n_layers, 2, half2, 2 * KVW), F32)
    for l in reversed(range(n_layers)):
        exchange, own1, own2 = pending[l]
        _, _, x1, x2 = _split_wait(_rs_copies, exchange, res[0], name=f"reduce_wait_{l}")
        full1 = _sum_chips(own1, x1, full1, l, place, tm=_tile(half1, 256), name=f"sum_chips_a_{l}")
        full2 = _sum_chips(own2, x2, full2, l, place, tm=_tile(half2, 256), name=f"sum_chips_b_{l}")
    full1, full2 = _share_with_sibling(full1, full2)
    full1 = full1.reshape(n_layers, rows1, d)
    full2 = full2.reshape(n_layers, rows2, 2 * KVW)

    big = {}
    for name, arr, layer0, off, per, w_, m_, v_ in (
            ("w_down", full1, 0, off_down, d, w_down, m_w_down, v_w_down),
            ("w_up", full1, 0, off_up, d, w_up, m_w_up, v_w_up),
            ("w_in", full1, 0, off_in, dq, w_in, m_w_in, v_w_in),
            ("w_out", full1, 0, off_out, dq, w_out, m_w_out, v_w_out),
            ("w_mem_kv", full2, 0, off_mkv, dq, w_mem_kv, m_w_mem_kv, v_w_mem_kv),
            ("w_kv", full2, n_pool, off_kv, dq, w_kv, m_w_kv, v_w_kv)):
        cols = arr.shape[2]
        upd = _adamw(arr, layer0, off, per, w_.reshape(-1, cols), m_.reshape(-1, cols), v_.reshape(-1, cols),
                     name=f"adamw_{name}", tm=min(256, dq))
        big[name] = [r.reshape(w_.shape) for r in upd]

    order = ["norm_mix", "w_in", "pool_w", "pool_scale", "kv_norm", "w_kv", "k_norm", "q_norm", "sinks", "mem_norm",
             "w_mem_kv", "mem_q_norm", "mem_k_norm", "w_out", "norm_mlp", "w_up", "w_down"]
    out = {**big, **small}
    return (loss, grad_x, *[out[n][0] for n in order], *[out[n][1] for n in order],
            *[out[n][2] for n in order], *[out[n][3] for n in order])
```

```python
import functools

import jax
import jax.numpy as jnp
from jax import lax
from jax.experimental import pallas as pl
from jax.experimental.pallas import tpu as pltpu

F32, BF = jnp.float32, jnp.bfloat16
SDS = jax.ShapeDtypeStruct
MESH = pl.DeviceIdType.MESH
ANY = pl.BlockSpec(memory_space=pl.ANY)
HBM = pl.BlockSpec(memory_space=pltpu.HBM)
SEM = pl.BlockSpec(memory_space=pltpu.SEMAPHORE)
VMEM_WHOLE = pl.BlockSpec(memory_space=pltpu.VMEM)
SIDE_EFFECT = pltpu.SideEffectType.DATAFLOW_SIDE_EFFECTING


def _in_hbm(a):
    return pltpu.with_memory_space_constraint(a, pltpu.HBM)


EPS = 1e-6
HEAD = 64
KV_HEADS = 4
KVW = KV_HEADS * HEAD
WINDOW = 128
POOL_WINDOWS = (2, 4, 8, 16)
HALO = 16
QK_SCALE = HEAD ** -0.5
NEG = float(jnp.finfo(jnp.float32).min)
N_CHIPS = 4
LANE = 128

ADAM_LR, ADAM_B1, ADAM_B2, ADAM_EPS, ADAM_WD, ADAM_STEP = 0.001, 0.9, 0.999, 1e-08, 0.01, 10

VMEM_LIMIT_MB = 56


def _call(body, name, grid, in_specs, out_specs, out_shape, *, scratch=(), semantics=None, aliases=None,
          prefetch=0):
    params = pltpu.CompilerParams(dimension_semantics=semantics, vmem_limit_bytes=VMEM_LIMIT_MB << 20)
    if prefetch:
        spec = pltpu.PrefetchScalarGridSpec(num_scalar_prefetch=prefetch, grid=grid, in_specs=in_specs,
                                            out_specs=out_specs, scratch_shapes=list(scratch))
        return pl.pallas_call(body, name=name, grid_spec=spec, out_shape=out_shape,
                              input_output_aliases=aliases or {}, compiler_params=params)
    return pl.pallas_call(body, name=name, grid=grid, in_specs=in_specs, out_specs=out_specs, out_shape=out_shape,
                          scratch_shapes=list(scratch), input_output_aliases=aliases or {}, compiler_params=params)


def _tile(n, pref):
    return max(t for t in range(8, min(n, pref) + 1, 8) if n % t == 0)


def _dot(a, b):
    return jnp.dot(a, b, preferred_element_type=F32)


def _dot_nt(a, b):
    return lax.dot_general(a, b, (((1,), (1,)), ((), ())), preferred_element_type=F32)


def _dot_tn(a, b):
    return lax.dot_general(a, b, (((0,), (0,)), ((), ())), preferred_element_type=F32)


def _rms(x):
    r = lax.rsqrt(jnp.mean(x * x, axis=-1, keepdims=True) + EPS)
    return x * r, r


def _rms_bwd(dy, xh, r, g):
    dg = jnp.sum(dy * xh, axis=0, keepdims=True)
    dyg = dy * g
    dx = r * (dyg - xh * jnp.mean(dyg * xh, axis=-1, keepdims=True))
    return dx, dg


def _norm_mm(h, g, w, nj, tn, *, act, name, tm):
    w_arr, w_block, w_imap = w
    rows, d = h.shape

    def body(h_ref, g_ref, w_ref, y_ref, xn_ref):
        xh, _ = _rms(h_ref[...])
        xn = (xh * g_ref[...]).astype(BF)
        xn_ref[...] = xn
        for j in range(nj):
            u = _dot(xn, w_ref[j] if nj > 1 else w_ref[...].reshape(d, tn))
            if act:
                a = jnp.maximum(u, 0.0)
                y_ref[:, j * tn:(j + 1) * tn] = (a * a).astype(BF)
            else:
                y_ref[:, j * tn:(j + 1) * tn] = u

    assert nj == 1 or w_block[0] == nj
    return _call(
        body, name, (rows // tm,),
        [pl.BlockSpec((tm, d), lambda i: (i, 0)), pl.BlockSpec((1, d), lambda i: (0, 0)),
         pl.BlockSpec(w_block, lambda i: w_imap(0))],
        [pl.BlockSpec((tm, nj * tn), lambda i: (i, 0)), pl.BlockSpec((tm, d), lambda i: (i, 0))],
        [SDS((rows, nj * tn), BF if act else F32), SDS((rows, d), BF)],
        semantics=("parallel",))(h, g, w_arr)


def _mm_res(res, a, w, *, name, tm, after=None):
    w_arr, w_block, w_imap = w
    rows, k = a.shape
    n = res.shape[1]

    def body(res_ref, a_ref, w_ref, *rest):
        rest[-1][...] = res_ref[...] + _dot(a_ref[...], w_ref[...].reshape(k, n))

    extra = [] if after is None else [after]
    return _call(
        body, name, (rows // tm,),
        [pl.BlockSpec((tm, n), lambda i: (i, 0)), pl.BlockSpec((tm, k), lambda i: (i, 0)),
         pl.BlockSpec(w_block, lambda i: w_imap(0))] + [ANY] * len(extra),
        pl.BlockSpec((tm, n), lambda i: (i, 0)), SDS((rows, n), F32), semantics=("parallel",))(res, a, w_arr, *extra)


def _mm_nt(dy, w, k, *, name, tm):
    w_arr, w_block, w_imap = w
    rows, n = dy.shape

    def body(dy_ref, w_ref, o_ref):
        o_ref[...] = _dot_nt(dy_ref[...], w_ref[...].reshape(k, n))

    return _call(
        body, name, (rows // tm,),
        [pl.BlockSpec((tm, n), lambda i: (i, 0)), pl.BlockSpec(w_block, lambda i: w_imap(0))],
        pl.BlockSpec((tm, k), lambda i: (i, 0)), SDS((rows, k), F32), semantics=("parallel",))(dy, w_arr)


def _mm_nt_relu2(dh, hh, w, nj, *, name, tm):
    w_arr, w_block, w_imap = w
    rows, d = dh.shape
    tk = hh.shape[1] // nj

    def body(dh_ref, hh_ref, w_ref, o_ref):
        dh_t = dh_ref[...]
        for j in range(nj):
            cols = slice(j * tk, (j + 1) * tk)
            dhh = _dot_nt(dh_t, w_ref[j])
            o_ref[:, cols] = (dhh * (2.0 * jnp.sqrt(hh_ref[:, cols].astype(F32)))).astype(BF)

    assert w_block[0] == nj
    return _call(
        body, name, (rows // tm,),
        [pl.BlockSpec((tm, d), lambda i: (i, 0)), pl.BlockSpec((tm, nj * tk), lambda i: (i, 0)),
         pl.BlockSpec(w_block, lambda i: w_imap(0))],
        pl.BlockSpec((tm, nj * tk), lambda i: (i, 0)), SDS((rows, nj * tk), BF),
        semantics=("parallel",))(dh, hh, w_arr)


def _mm_nt_normbwd(dy, w, nsplit, h, g, dres, *, name, tm):
    w_arr, w_block, w_imap = w
    rows, n = dy.shape
    d = h.shape[1]
    ns = n // nsplit

    def body(dy_ref, w_ref, h_ref, g_ref, dres_ref, o_ref, obf_ref, dg_ref):
        if nsplit == 1:
            dxn = _dot_nt(dy_ref[...].astype(BF), w_ref[...].reshape(d, n))
        else:
            dxn = _dot_nt(dy_ref[:, 0:ns].astype(BF), w_ref[0])
            for s in range(1, nsplit):
                dxn += _dot_nt(dy_ref[:, s * ns:(s + 1) * ns].astype(BF), w_ref[s])
        xh, r = _rms(h_ref[...])
        dx, dg = _rms_bwd(dxn, xh, r, g_ref[...])
        out = dres_ref[...] + dx
        o_ref[...] = out
        obf_ref[...] = out.astype(BF)

        @pl.when(pl.program_id(0) == 0)
        def _():
            dg_ref[...] = jnp.zeros_like(dg_ref)

        dg_ref[...] += dg

    row = lambda i: (i, 0)
    return _call(
        body, name, (rows // tm,),
        [pl.BlockSpec((tm, n), row), pl.BlockSpec(w_block, lambda i: w_imap(0)), pl.BlockSpec((tm, d), row),
         pl.BlockSpec((1, d), lambda i: (0, 0)), pl.BlockSpec((tm, d), row)],
        [pl.BlockSpec((tm, d), row), pl.BlockSpec((tm, d), row), pl.BlockSpec((1, d), lambda i: (0, 0))],
        [SDS((rows, d), F32), SDS((rows, d), BF), SDS((1, d), F32)],
        semantics=("arbitrary",))(dy, w_arr, h, g, dres)


def _mm_tn(x, dy, packed, out_imap, tk, tn, *, name):
    s_len, k = x.shape
    n = dy.shape[1]

    def body(x_ref, dy_ref, _, o_ref):
        o_ref[0] = _dot_tn(x_ref[...], dy_ref[...].astype(BF))

    return _call(
        body, name, (k // tk, n // tn),
        [pl.BlockSpec((s_len, tk), lambda i, j: (0, i)), pl.BlockSpec((s_len, tn), lambda i, j: (0, j)), ANY],
        pl.BlockSpec((1, tk, tn), out_imap), SDS(packed.shape, packed.dtype),
        semantics=("parallel", "parallel"), aliases={2: 0})(x, dy, packed)


def _loss_head(y, tgt, *, tm):
    rows, d = y.shape

    def body(y_ref, t_ref, dh_ref, dhbf_ref, loss_ref):
        err = y_ref[...] - t_ref[...]
        dh = err * (1.0 / d)
        dh_ref[...] = dh
        dhbf_ref[...] = dh.astype(BF)

        @pl.when(pl.program_id(0) == 0)
        def _():
            loss_ref[...] = jnp.zeros_like(loss_ref)

        loss_ref[...] += 0.5 * jnp.sum(jnp.mean(err * err, axis=-1, keepdims=True), axis=0, keepdims=True)

    row = lambda i: (i, 0)
    return _call(
        body, "loss_head", (rows // tm,), [pl.BlockSpec((tm, d), row), pl.BlockSpec((tm, d), row)],
        [pl.BlockSpec((tm, d), row), pl.BlockSpec((tm, d), row), pl.BlockSpec((1, 1), lambda i: (0, 0))],
        [SDS((rows, d), F32), SDS((rows, d), BF), SDS((1, 1), F32)], semantics=("arbitrary",))(y, tgt)


def _hs(h):
    return slice(HEAD * h, HEAD * (h + 1))


def _softmax_rows(s):
    e = jnp.exp(s - jnp.max(s, axis=-1, keepdims=True))
    return e * (1.0 / jnp.sum(e, axis=-1, keepdims=True))


def _scaled_bf16(qn):
    return (qn * QK_SCALE).astype(BF)


def _mem_fwd(mq, mk, mv, gq):
    outs, probs = [], []
    for h in range(KV_HEADS):
        xh, _ = _rms(mq[:, _hs(h)])
        p = _softmax_rows(_dot_nt(_scaled_bf16(xh * gq), mk[:, _hs(h)])).astype(BF)
        probs.append(p)
        outs.append(_dot(p, mv[:, _hs(h)]))
    return jnp.concatenate(outs, axis=-1), jnp.concatenate(probs, axis=-1)


def _mem_bwd(mq, do, mk, mv, gq, probs):
    dqs, dks, dvs, dgq = [], [], [], 0.0
    mlen = mk.shape[0]
    for h in range(KV_HEADS):
        xh, r = _rms(mq[:, _hs(h)])
        qn = _scaled_bf16(xh * gq)
        p_bf = probs[:, h * mlen:(h + 1) * mlen]
        p = p_bf.astype(F32)
        doh = do[:, _hs(h)].astype(BF)
        dp = _dot_nt(doh, mv[:, _hs(h)])
        ds = (p * (dp - jnp.sum(p * dp, axis=-1, keepdims=True))).astype(BF)
        dq, dg = _rms_bwd(_dot(ds, mk[:, _hs(h)]) * QK_SCALE, xh, r, gq)
        dqs.append(dq)
        dgq = dgq + dg
        dks.append(_dot_tn(ds, qn))
        dvs.append(_dot_tn(p_bf, doh))
    cat = lambda xs: jnp.concatenate(xs, axis=-1)
    return cat(dqs), cat(dks), cat(dvs), dgq


def _mem_kv(mkv, gk):
    ks = []
    for h in range(KV_HEADS):
        xh, _ = _rms(mkv[:, _hs(h)])
        ks.append(xh * gk)
    return jnp.concatenate(ks, axis=-1).astype(BF), mkv[:, KVW:].astype(BF)


def _mem_kv_bwd(mkv, dmk, dmv, gk):
    dxs, dgk = [], 0.0
    for h in range(KV_HEADS):
        xh, r = _rms(mkv[:, _hs(h)])
        dx, dg = _rms_bwd(dmk[:, _hs(h)], xh, r, gk)
        dxs.append(dx)
        dgk = dgk + dg
    return jnp.concatenate(dxs + [dmv], axis=-1), dgk


def _pool_select(col, gd, a2, a4, a8, a16):
    return jnp.where(col < gd, a2, jnp.where(col < 2 * gd, a4, jnp.where(col < 3 * gd, a8, a16)))


def _pool_count(t0, shape, gd):
    col = lax.broadcasted_iota(jnp.int32, shape, 1)
    t = t0 + lax.broadcasted_iota(jnp.int32, shape, 0)
    win = _pool_select(col, gd, *POOL_WINDOWS)
    return jnp.minimum(t + 1, win).astype(F32)


def _pool_diff(u, halo, t0, gd):
    c = jnp.concatenate([halo, u], axis=0)
    s2 = c + pltpu.roll(c, 1, 0)
    s4 = s2 + pltpu.roll(s2, 2, 0)
    s8 = s4 + pltpu.roll(s4, 4, 0)
    s16 = s8 + pltpu.roll(s8, 8, 0)
    col = lax.broadcasted_iota(jnp.int32, c.shape, 1)
    ws = _pool_select(col, gd, s2, s4, s8, s16)[HALO:]
    return ws / _pool_count(t0, u.shape, gd) - u


def _pool_diff_bwd(dd, dd_halo, t0, gd):
    t = dd.shape[0]
    z = jnp.concatenate([dd / _pool_count(t0, dd.shape, gd), dd_halo / _pool_count(t0 + t, dd_halo.shape, gd)], axis=0)
    n = z.shape[0]
    f2 = z + pltpu.roll(z, n - 1, 0)
    f4 = f2 + pltpu.roll(f2, n - 2, 0)
    f8 = f4 + pltpu.roll(f4, n - 4, 0)
    f16 = f8 + pltpu.roll(f8, n - 8, 0)
    col = lax.broadcasted_iota(jnp.int32, z.shape, 1)
    return _pool_select(col, gd, f2, f4, f8, f16)[:t] - dd


def _swa_bias(n):
    qi = lax.broadcasted_iota(jnp.int32, (WINDOW, 2 * WINDOW), 0)
    kj = lax.broadcasted_iota(jnp.int32, (WINDOW, 2 * WINDOW), 1)
    dist = qi + WINDOW - kj
    valid = (dist >= 0) & (dist < WINDOW) & ((kj >= WINDOW) | (n > 0))
    return dist.astype(F32), valid


def _slopes(qh):
    return [2.0 ** (-8.0 * (h + 1) / qh) for h in range(qh)]


def _swa_probs(qn, kk, dist, valid, slope, sink):
    s = _dot_nt(qn, kk) - slope * dist
    s = jnp.where(valid, s, NEG)
    m = jnp.maximum(jnp.max(s, axis=-1, keepdims=True), sink)
    e = jnp.exp(s - m)
    es = jnp.exp(sink - m)
    z = jnp.sum(e, axis=-1, keepdims=True) + es
    inv = 1.0 / z
    return e * inv, es * inv


def _stack_heads(a, kh, grp):
    return jnp.concatenate([a[:, _hs(h)] for h in range(kh * grp, (kh + 1) * grp)], axis=0)


def _swa_group(q, kh, grp, n, qh, sinks):
    heads = range(kh * grp, (kh + 1) * grp)
    dist, valid = _swa_bias(n)
    slopes = _slopes(qh)
    rows = lambda vals: jnp.concatenate([jnp.broadcast_to(v, (WINDOW, 1)) for v in vals], axis=0)
    slope = rows([jnp.full((1, 1), slopes[h], F32) for h in heads])
    sink = rows([sinks[:, h:h + 1] for h in heads])
    return (_stack_heads(q, kh, grp), slope, sink, jnp.concatenate([dist] * grp, axis=0),
            jnp.concatenate([valid] * grp, axis=0))


def _swa_fwd(q, kk, vv, gq, sinks, n, qh):
    grp = qh // KV_HEADS
    lane = lax.broadcasted_iota(jnp.int32, (WINDOW, LANE), 1)
    outs, probs, sink_probs = [], [], jnp.zeros((WINDOW, LANE), F32)
    for kh in range(KV_HEADS):
        qs, slope, sink, dist, valid = _swa_group(q, kh, grp, n, qh, sinks)
        xh, _ = _rms(qs)
        p, ps = _swa_probs(_scaled_bf16(xh * gq), kk[:, _hs(kh)], dist, valid, slope, sink)
        p = p.astype(BF)
        probs.append(p)
        o = _dot(p, vv[:, _hs(kh)])
        for g in range(grp):
            outs.append(o[g * WINDOW:(g + 1) * WINDOW])
            sink_probs = jnp.where(lane == kh * grp + g, ps[g * WINDOW:(g + 1) * WINDOW], sink_probs)
    return jnp.concatenate(outs, axis=-1), probs, sink_probs


def _swa_bwd(q, do, kk, vv, gq, probs, sink_probs, qh):
    grp = qh // KV_HEADS
    lane = lax.broadcasted_iota(jnp.int32, (1, LANE), 1)
    dqs, dks, dvs, dgq, dsk = [], [], [], 0.0, jnp.zeros((1, LANE), F32)
    do = do.astype(BF)
    for kh in range(KV_HEADS):
        heads = range(kh * grp, (kh + 1) * grp)
        xh, r = _rms(_stack_heads(q, kh, grp))
        qn = _scaled_bf16(xh * gq)
        p_bf = probs[kh]
        p = p_bf.astype(F32)
        ps = jnp.concatenate([sink_probs[:, h:h + 1] for h in heads], axis=0)
        dos = _stack_heads(do, kh, grp)
        dp = _dot_nt(dos, vv[:, _hs(kh)])
        delta = jnp.sum(p * dp, axis=-1, keepdims=True)
        ds = (p * (dp - delta)).astype(BF)
        dsink = ps * delta
        for g in range(grp):
            part = -jnp.sum(dsink[g * WINDOW:(g + 1) * WINDOW], axis=0, keepdims=True)
            dsk = dsk + jnp.where(lane == kh * grp + g, part, 0.0)
        dq, dg = _rms_bwd(_dot(ds, kk[:, _hs(kh)]) * QK_SCALE, xh, r, gq)
        dqs += [dq[g * WINDOW:(g + 1) * WINDOW] for g in range(grp)]
        dgq = dgq + dg
        dks.append(_dot_tn(ds, qn))
        dvs.append(_dot_tn(p_bf, dos))
    cat = lambda xs: jnp.concatenate(xs, axis=-1)
    return cat(dqs), cat(dks), cat(dvs), dgq, dsk


def _mixer_pool_fwd(proj, mkv, pbd, scale, gq, gk, *, name, tm):
    s_len, d = proj.shape
    main = d - KVW
    gd = main // len(POOL_WINDOWS)
    mlen = mkv.shape[0]
    hb = tm // HALO

    def body(u_ref, halo_ref, mq_ref, mkv_ref, pbd_ref, scale_ref, gq_ref, gk_ref, o_ref, mp_ref, mk_s, mv_s):
        i = pl.program_id(0)

        @pl.when(i == 0)
        def _():
            mk, mv = _mem_kv(mkv_ref[...], gk_ref[...])
            mk_s[...] = mk
            mv_s[...] = mv

        halo = jnp.where(i > 0, halo_ref[...], 0.0)
        dif = _pool_diff(u_ref[...], halo, i * tm, gd)
        mixed = _dot(dif.astype(BF), pbd_ref[...]) * scale_ref[...]
        mem, mp_ref[...] = _mem_fwd(mq_ref[...], mk_s[...], mv_s[...], gq_ref[...])
        o_ref[...] = jnp.concatenate([mixed, mem], axis=-1).astype(BF)

    full = lambda shape: pl.BlockSpec(shape, lambda i: (0,) * len(shape))
    return _call(
        body, name, (s_len // tm,),
        [pl.BlockSpec((tm, main), lambda i: (i, 0)),
         pl.BlockSpec((HALO, main), lambda i: (jnp.maximum(i * hb - 1, 0), 0)),
         pl.BlockSpec((tm, KVW), lambda i: (i, main // KVW)),
         full((mlen, 2 * KVW)), full((main, main)), full((1, main)), full((1, HEAD)), full((1, HEAD))],
        [pl.BlockSpec((tm, d), lambda i: (i, 0)), pl.BlockSpec((tm, KV_HEADS * mlen), lambda i: (i, 0))],
        [SDS((s_len, d), BF), SDS((s_len, KV_HEADS * mlen), BF)],
        scratch=[pltpu.VMEM((mlen, KVW), BF), pltpu.VMEM((mlen, KVW), BF)],
        semantics=("arbitrary",))(proj, proj, proj, mkv, pbd, scale, gq, gk)


def _mixer_pool_bwd(proj, dcat, mkv, pbd, scale, gq, gk, mem_probs, *, name, tm):
    s_len, d = proj.shape
    main = d - KVW
    gd = main // len(POOL_WINDOWS)
    mlen = mkv.shape[0]
    hb = tm // HALO
    nt = s_len // tm
    last_halo = s_len // HALO - 1

    def body(u_ref, halo_ref, mq_ref, do_ref, donext_ref, dom_ref, mkv_ref, pbd_ref, scale_ref, gq_ref, gk_ref, mp_ref,
             dproj_ref, dpbd_ref, dscale_ref, dmkv_ref, dgq_ref, dgk_ref, mk_s, mv_s, dmk_s, dmv_s):
        i = pl.program_id(0)

        @pl.when(i == 0)
        def _():
            mk, mv = _mem_kv(mkv_ref[...], gk_ref[...])
            mk_s[...] = mk
            mv_s[...] = mv
            dmk_s[...] = jnp.zeros_like(dmk_s)
            dmv_s[...] = jnp.zeros_like(dmv_s)
            dpbd_ref[...] = jnp.zeros_like(dpbd_ref)
            dscale_ref[...] = jnp.zeros_like(dscale_ref)
            dgq_ref[...] = jnp.zeros_like(dgq_ref)

        pbd = pbd_ref[...]
        scale = scale_ref[...]
        halo = jnp.where(i > 0, halo_ref[...], 0.0)
        dif = _pool_diff(u_ref[...], halo, i * tm, gd).astype(BF)
        do = do_ref[...]
        dscale_ref[...] += jnp.sum(do * _dot(dif, pbd), axis=0, keepdims=True)
        dmixed = (do * scale).astype(BF)
        dpbd_ref[...] += _dot_tn(dif, dmixed)
        dd = _dot_nt(dmixed, pbd)
        donext = jnp.where(i < nt - 1, donext_ref[...], 0.0)
        dd_halo = _dot_nt((donext * scale).astype(BF), pbd)
        du = _pool_diff_bwd(dd, dd_halo, i * tm, gd)

        dmq, dmk, dmv, dgq = _mem_bwd(mq_ref[...], dom_ref[...], mk_s[...], mv_s[...], gq_ref[...], mp_ref[...])
        dmk_s[...] += dmk
        dmv_s[...] += dmv
        dgq_ref[...] += dgq
        dproj_ref[...] = jnp.concatenate([du, dmq], axis=-1).astype(BF)

        @pl.when(i == nt - 1)
        def _():
            dmkv, dgk = _mem_kv_bwd(mkv_ref[...], dmk_s[...], dmv_s[...], gk_ref[...])
            dmkv_ref[...] = dmkv
            dgk_ref[...] = dgk

    full = lambda shape: pl.BlockSpec(shape, lambda i: (0,) * len(shape))
    return _call(
        body, name, (nt,),
        [pl.BlockSpec((tm, main), lambda i: (i, 0)),
         pl.BlockSpec((HALO, main), lambda i: (jnp.maximum(i * hb - 1, 0), 0)),
         pl.BlockSpec((tm, KVW), lambda i: (i, main // KVW)),
         pl.BlockSpec((tm, main), lambda i: (i, 0)),
         pl.BlockSpec((HALO, main), lambda i: (jnp.minimum((i + 1) * hb, last_halo), 0)),
         pl.BlockSpec((tm, KVW), lambda i: (i, main // KVW)),
         full((mlen, 2 * KVW)), full((main, main)), full((1, main)), full((1, HEAD)), full((1, HEAD)),
         pl.BlockSpec((tm, KV_HEADS * mlen), lambda i: (i, 0))],
        [pl.BlockSpec((tm, d), lambda i: (i, 0)), full((main, main)), full((1, main)), full((mlen, 2 * KVW)),
         full((1, HEAD)), full((1, HEAD))],
        [SDS((s_len, d), BF), SDS((main, main), F32), SDS((1, main), F32), SDS((mlen, 2 * KVW), F32),
         SDS((1, HEAD), F32), SDS((1, HEAD), F32)],
        scratch=[pltpu.VMEM((mlen, KVW), BF), pltpu.VMEM((mlen, KVW), BF), pltpu.VMEM((mlen, KVW), F32),
                 pltpu.VMEM((mlen, KVW), F32)],
        semantics=("arbitrary",))(proj, proj, proj, dcat, dcat, dcat, mkv, pbd, scale, gq, gk, mem_probs)


def _mem_attn_fwd(proj, mkv, gq, gk, cat, *, name, tm):
    s_len, d = proj.shape
    main = d - KVW
    mlen = mkv.shape[0]

    def body(mq_ref, mkv_ref, gq_ref, gk_ref, _, o_ref, mp_ref, mk_s, mv_s):
        @pl.when(pl.program_id(0) == 0)
        def _():
            mk, mv = _mem_kv(mkv_ref[...], gk_ref[...])
            mk_s[...] = mk
            mv_s[...] = mv

        mem, mp_ref[...] = _mem_fwd(mq_ref[...], mk_s[...], mv_s[...], gq_ref[...])
        o_ref[...] = mem.astype(BF)

    full = lambda shape: pl.BlockSpec(shape, lambda i: (0,) * len(shape))
    memcol = lambda i: (i, main // KVW)
    return _call(
        body, name, (s_len // tm,),
        [pl.BlockSpec((tm, KVW), memcol), full((mlen, 2 * KVW)), full((1, HEAD)), full((1, HEAD)), ANY],
        [pl.BlockSpec((tm, KVW), memcol), pl.BlockSpec((tm, KV_HEADS * mlen), lambda i: (i, 0))],
        [SDS((s_len, d), BF), SDS((s_len, KV_HEADS * mlen), BF)],
        scratch=[pltpu.VMEM((mlen, KVW), BF), pltpu.VMEM((mlen, KVW), BF)],
        semantics=("arbitrary",), aliases={4: 0})(proj, mkv, gq, gk, cat)


def _mem_attn_bwd(proj, dcat, mkv, gq, gk, mem_probs, dproj, *, name, tm):
    s_len, d = proj.shape
    main = d - KVW
    mlen = mkv.shape[0]
    nt = s_len // tm

    def body(mq_ref, dom_ref, mkv_ref, gq_ref, gk_ref, mp_ref, _, dproj_ref, dmkv_ref, dgq_ref, dgk_ref,
             mk_s, mv_s, dmk_s, dmv_s):
        i = pl.program_id(0)

        @pl.when(i == 0)
        def _():
            mk, mv = _mem_kv(mkv_ref[...], gk_ref[...])
            mk_s[...] = mk
            mv_s[...] = mv
            dmk_s[...] = jnp.zeros_like(dmk_s)
            dmv_s[...] = jnp.zeros_like(dmv_s)
            dgq_ref[...] = jnp.zeros_like(dgq_ref)

        dmq, dmk, dmv, dgq = _mem_bwd(mq_ref[...], dom_ref[...], mk_s[...], mv_s[...], gq_ref[...], mp_ref[...])
        dmk_s[...] += dmk
        dmv_s[...] += dmv
        dgq_ref[...] += dgq
        dproj_ref[...] = dmq.astype(BF)

        @pl.when(i == nt - 1)
        def _():
            dmkv, dgk = _mem_kv_bwd(mkv_ref[...], dmk_s[...], dmv_s[...], gk_ref[...])
            dmkv_ref[...] = dmkv
            dgk_ref[...] = dgk

    full = lambda shape: pl.BlockSpec(shape, lambda i: (0,) * len(shape))
    memcol = lambda i: (i, main // KVW)
    return _call(
        body, name, (nt,),
        [pl.BlockSpec((tm, KVW), memcol), pl.BlockSpec((tm, KVW), memcol), full((mlen, 2 * KVW)), full((1, HEAD)),
         full((1, HEAD)), pl.BlockSpec((tm, KV_HEADS * mlen), lambda i: (i, 0)), ANY],
        [pl.BlockSpec((tm, KVW), memcol), full((mlen, 2 * KVW)), full((1, HEAD)), full((1, HEAD))],
        [SDS((s_len, d), BF), SDS((mlen, 2 * KVW), F32), SDS((1, HEAD), F32), SDS((1, HEAD), F32)],
        scratch=[pltpu.VMEM((mlen, KVW), BF), pltpu.VMEM((mlen, KVW), BF), pltpu.VMEM((mlen, KVW), F32),
                 pltpu.VMEM((mlen, KVW), F32)],
        semantics=("arbitrary",), aliases={6: 0})(proj, dcat, mkv, gq, gk, mem_probs, dproj)


def _mixer_swa_fwd(proj, kn, v, gqs, sinks, *, name):
    s_len, d = proj.shape
    main = d - KVW
    qh = main // HEAD
    tm = WINDOW
    prow = qh // KV_HEADS * tm

    def body(q_ref, kp_ref, kc_ref, vp_ref, vc_ref, gqs_ref, sinks_ref, o_ref, p_ref, ps_ref):
        n = pl.program_id(0)
        kk = jnp.concatenate([kp_ref[...], kc_ref[...]], axis=0)
        vv = jnp.concatenate([vp_ref[...], vc_ref[...]], axis=0)
        att, probs, sink_probs = _swa_fwd(q_ref[...], kk, vv, gqs_ref[...], sinks_ref[...], n, qh)
        for kh in range(KV_HEADS):
            p_ref[0, kh] = probs[kh]
        ps_ref[...] = sink_probs
        o_ref[...] = att.astype(BF)

    full = lambda shape: pl.BlockSpec(shape, lambda i: (0,) * len(shape))
    prev = lambda i: (jnp.maximum(i - 1, 0), 0)
    cur = lambda i: (i, 0)
    return _call(
        body, name, (s_len // tm,),
        [pl.BlockSpec((tm, main), cur), pl.BlockSpec((tm, KVW), prev), pl.BlockSpec((tm, KVW), cur),
         pl.BlockSpec((tm, KVW), prev), pl.BlockSpec((tm, KVW), cur), full((1, HEAD)), full((1, LANE))],
        [pl.BlockSpec((tm, main), cur), pl.BlockSpec((1, KV_HEADS, prow, 2 * tm), lambda i: (i, 0, 0, 0)),
         pl.BlockSpec((tm, LANE), cur)],
        [SDS((s_len, d), BF), SDS((s_len // tm, KV_HEADS, prow, 2 * tm), BF), SDS((s_len, LANE), F32)],
        semantics=("parallel",))(proj, kn, kn, v, v, gqs, sinks)


def _mixer_swa_bwd(proj, dcat, kn, v, gqs, probs, sink_probs, *, name):
    s_len, d = proj.shape
    main = d - KVW
    qh = main // HEAD
    tm = WINDOW
    nt = s_len // tm
    prow = qh // KV_HEADS * tm

    def body(q_ref, do_ref, kp_ref, kc_ref, vp_ref, vc_ref, gqs_ref, p_ref, ps_ref,
             dproj_ref, dk_ref, dv_ref, dgqs_ref, dsinks_ref):
        n = pl.program_id(0)

        @pl.when(n == 0)
        def _():
            dk_ref[...] = jnp.zeros_like(dk_ref)
            dv_ref[...] = jnp.zeros_like(dv_ref)
            dgqs_ref[...] = jnp.zeros_like(dgqs_ref)
            dsinks_ref[...] = jnp.zeros_like(dsinks_ref)

        kk = jnp.concatenate([kp_ref[...], kc_ref[...]], axis=0)
        vv = jnp.concatenate([vp_ref[...], vc_ref[...]], axis=0)
        dq, dkk, dvv, dgqs, dsk = _swa_bwd(q_ref[...], do_ref[...], kk, vv, gqs_ref[...],
                                           [p_ref[0, kh] for kh in range(KV_HEADS)], ps_ref[...], qh)
        prev = pl.ds(pl.multiple_of(jnp.maximum(n - 1, 0) * tm, tm), tm)
        own = pl.ds(pl.multiple_of(n * tm, tm), tm)
        dk_ref[prev, :] += dkk[:tm]
        dk_ref[own, :] += dkk[tm:]
        dv_ref[prev, :] += dvv[:tm]
        dv_ref[own, :] += dvv[tm:]
        dgqs_ref[...] += dgqs
        dsinks_ref[...] += dsk
        dproj_ref[...] = dq.astype(BF)

    full = lambda shape: pl.BlockSpec(shape, lambda i: (0,) * len(shape))
    prev_b = lambda i: (jnp.maximum(i - 1, 0), 0)
    cur = lambda i: (i, 0)
    return _call(
        body, name, (nt,),
        [pl.BlockSpec((tm, main), cur), pl.BlockSpec((tm, main), cur),
         pl.BlockSpec((tm, KVW), prev_b), pl.BlockSpec((tm, KVW), cur),
         pl.BlockSpec((tm, KVW), prev_b), pl.BlockSpec((tm, KVW), cur),
         full((1, HEAD)), pl.BlockSpec((1, KV_HEADS, prow, 2 * tm), lambda i: (i, 0, 0, 0)),
         pl.BlockSpec((tm, LANE), cur)],
        [pl.BlockSpec((tm, main), cur), full((s_len, KVW)), full((s_len, KVW)), full((1, HEAD)), full((1, LANE))],
        [SDS((s_len, d), BF), SDS((s_len, KVW), F32), SDS((s_len, KVW), F32), SDS((1, HEAD), F32),
         SDS((1, LANE), F32)],
        semantics=("arbitrary",))(proj, dcat, kn, kn, v, v, gqs, probs, sink_probs)


def _kv_prep(kv, gk, *, tm):
    s_len = kv.shape[0]

    def body(kv_ref, gk_ref, k_ref, v_ref):
        k, v = _mem_kv(kv_ref[...], gk_ref[...])
        k_ref[...] = k
        v_ref[...] = v

    row = lambda i: (i, 0)
    return _call(
        body, "kv_prep", (s_len // tm,),
        [pl.BlockSpec((tm, 2 * KVW), row), pl.BlockSpec((1, HEAD), lambda i: (0, 0))],
        [pl.BlockSpec((tm, KVW), row), pl.BlockSpec((tm, KVW), row)],
        [SDS((s_len, KVW), BF), SDS((s_len, KVW), BF)], semantics=("parallel",))(kv, gk)


def _kv_bwd(kv, dks, dvs, gk, *, tm):
    s_len = kv.shape[0]
    nl = len(dks)

    def body(*refs):
        kv_ref, gk_ref = refs[0], refs[1]
        dk_refs, dv_refs = refs[2:2 + nl], refs[2 + nl:2 + 2 * nl]
        dkv_ref, dgk_ref = refs[2 + 2 * nl], refs[3 + 2 * nl]
        dk, dv = dk_refs[0][...], dv_refs[0][...]
        for t in range(1, nl):
            dk = dk + dk_refs[t][...]
            dv = dv + dv_refs[t][...]
        dkv, dgk = _mem_kv_bwd(kv_ref[...], dk, dv, gk_ref[...])
        dkv_ref[...] = dkv.astype(BF)

        @pl.when(pl.program_id(0) == 0)
        def _():
            dgk_ref[...] = jnp.zeros_like(dgk_ref)

        dgk_ref[...] += dgk

    row = lambda i: (i, 0)
    one = pl.BlockSpec((1, HEAD), lambda i: (0, 0))
    return _call(
        body, "kv_bwd", (s_len // tm,),
        [pl.BlockSpec((tm, 2 * KVW), row), one] + [pl.BlockSpec((tm, KVW), row)] * (2 * nl),
        [pl.BlockSpec((tm, 2 * KVW), row), one],
        [SDS((s_len, 2 * KVW), BF), SDS((1, HEAD), F32)], semantics=("arbitrary",))(kv, gk, *dks, *dvs)


def _place():
    x, y, c = lax.axis_index("x"), lax.axis_index("y"), lax.axis_index("c")
    flips = [(1 - x, y), (x, 1 - y), (1 - x, 1 - y)]
    return x, y, c, flips


def _remote(src, dst, send_sem, recv_sem, to):
    return pltpu.make_async_remote_copy(src_ref=src, dst_ref=dst, send_sem=send_sem, recv_sem=recv_sem,
                                        device_id=to, device_id_type=MESH)


def _gather_copies(p_refs, wg_refs, send, recv):
    x, y, c, flips = _place()
    chip = 2 * x + y
    cps = []
    for j, (fx, fy) in enumerate(flips):
        for b in range(2):
            half = p_refs[b].shape[0] // 2
            mine = pl.ds(c * half, half)
            cps.append(_remote(p_refs[b].at[mine, :], wg_refs[b].at[chip, mine, :], send.at[2 * j + b],
                               recv.at[2 * j + b], (fx, fy, c)))
    return cps, cps


def _forward_copies(p_refs, wg_refs, send, recv):
    x, y, c, flips = _place()
    chip = 2 * x + y
    sib = (x, y, 1 - c)
    sends, arrivals = [], []
    for b in range(2):
        half = p_refs[b].shape[0] // 2
        own = _remote(p_refs[b], wg_refs[b].at[chip], send.at[b], recv.at[b], sib)
        sends.append(own)
        arrivals.append(own)
        for j, (fx, fy) in enumerate(flips):
            k = 2 + 3 * b + j
            landed = wg_refs[b].at[2 * fx + fy, pl.ds(c * half, half), :]
            other = wg_refs[b].at[2 * fx + fy, pl.ds((1 - c) * half, half), :]
            sends.append(_remote(landed, landed, send.at[k], recv.at[k], sib))
            arrivals.append(_remote(other, other, send.at[k], recv.at[k], sib))
    return sends, arrivals


def _swap_copies(g_refs, r_refs, send, recv):
    x, y, c, _ = _place()
    cps = []
    for b in range(2):
        half = g_refs[b].shape[1] // 2
        cps.append(_remote(g_refs[b].at[:, pl.ds((1 - c) * half, half), :], r_refs[b], send.at[b], recv.at[b],
                           (x, y, 1 - c)))
    return cps, cps


def _split_start(make_copies, n_sems, bufs, after, fresh, *, name):
    def body(a1, a2, b1, b2, after_ref, send, recv, *outs):
        for cp in make_copies((a1, a2), (b1, b2), send, recv)[0]:
            cp.start()
        outs[4][...] = jnp.zeros_like(outs[4])

    extra_shape = () if fresh is None else (pltpu.HBM(fresh, F32),)
    extra_spec = () if fresh is None else (HBM,)
    return pl.pallas_call(
        body, name=name,
        out_shape=(pltpu.SemaphoreType.DMA((n_sems,)), pltpu.SemaphoreType.DMA((n_sems,)))
        + tuple(pltpu.HBM(b.shape, b.dtype) for b in bufs) + (SDS((8, LANE), F32),) + extra_shape,
        in_specs=(HBM, HBM, HBM, HBM, ANY), out_specs=(SEM, SEM, HBM, HBM, HBM, HBM, VMEM_WHOLE) + extra_spec,
        input_output_aliases={0: 2, 1: 3, 2: 4, 3: 5},
        compiler_params=pltpu.CompilerParams(has_side_effects=SIDE_EFFECT))(*[_in_hbm(b) for b in bufs], after)


def _split_wait(make_copies, started, after, *, name):
    send, recv, bufs = started[0], started[1], started[2:6]

    def body(a1, a2, b1, b2, send_ref, recv_ref, after_ref, *outs):
        sends, arrivals = make_copies((a1, a2), (b1, b2), send_ref, recv_ref)
        for cp in arrivals:
            cp.wait_recv()
        for cp in sends:
            cp.wait_send()

    return pl.pallas_call(
        body, name=name, out_shape=tuple(pltpu.HBM(b.shape, b.dtype) for b in bufs),
        in_specs=(HBM, HBM, HBM, HBM, SEM, SEM, ANY), out_specs=(HBM, HBM, HBM, HBM),
        input_output_aliases={0: 0, 1: 1, 2: 2, 3: 3},
        compiler_params=pltpu.CompilerParams(has_side_effects=SIDE_EFFECT))(*bufs, send, recv, after)


def _gather_small(ps):
    def body(ps_ref, o_ref, send, recv):
        x, y, c, flips = _place()
        chip = 2 * x + y
        o_ref[chip] = ps_ref[...]
        cps = [_remote(ps_ref, o_ref.at[chip], send.at[j], recv.at[j], (fx, fy, c))
               for j, (fx, fy) in enumerate(flips)]
        for cp in cps:
            cp.start()
        for j, (fx, fy) in enumerate(flips):
            _remote(ps_ref, o_ref.at[2 * fx + fy], send.at[j], recv.at[j], (fx, fy, c)).wait_recv()
        for cp in cps:
            cp.wait_send()

    return pl.pallas_call(
        body, name="gather_small", in_specs=[VMEM_WHOLE], out_specs=VMEM_WHOLE,
        out_shape=SDS((N_CHIPS,) + ps.shape, ps.dtype),
        scratch_shapes=[pltpu.SemaphoreType.DMA((3,)), pltpu.SemaphoreType.DMA((3,))])(ps)


def _sum_sibling(g, r, place, *, tm, name):
    n_sh, half, w = r.shape
    nt = half // tm

    def body(place_ref, g_ref, r_ref, pbf_ref, own_ref):
        s = pl.program_id(1)
        p = g_ref[0] + r_ref[0]
        pbf_ref[0] = p.astype(BF)

        @pl.when(s == place_ref[1])
        def _():
            own_ref[...] = p

    return _call(
        body, name, (nt, n_sh),
        [pl.BlockSpec((1, tm, w), lambda i, s, pr: (s, pr[0] * nt + i, 0)),
         pl.BlockSpec((1, tm, w), lambda i, s, pr: (s, i, 0))],
        [pl.BlockSpec((1, tm, w), lambda i, s, pr: (s, i, 0)), pl.BlockSpec((tm, w), lambda i, s, pr: (i, 0))],
        [SDS((n_sh, half, w), BF), SDS((half, w), F32)],
        semantics=("arbitrary", "arbitrary"), prefetch=1)(place, g, r)


def _rs_copies(p_refs, land_refs, send, recv):
    _, _, c, flips = _place()
    cps = []
    for j, (fx, fy) in enumerate(flips):
        for b in range(2):
            cps.append(_remote(p_refs[b].at[2 * fx + fy], land_refs[b].at[j], send.at[2 * j + b], recv.at[2 * j + b],
                               (fx, fy, c)))
    return cps, cps


def _sum_chips(own, r, full, layer, place, *, tm, name):
    half, w = own.shape

    def body(place_ref, own_ref, r_ref, _, o_ref):
        o_ref[0, 0] = ((own_ref[...] + r_ref[0].astype(F32)) + r_ref[1].astype(F32)) + r_ref[2].astype(F32)

    return _call(
        body, name, (half // tm,),
        [pl.BlockSpec((tm, w), lambda i, pr: (i, 0)), pl.BlockSpec((3, tm, w), lambda i, pr: (0, i, 0)), ANY],
        pl.BlockSpec((1, 1, tm, w), lambda i, pr: (layer, pr[0], i, 0)), SDS(full.shape, F32),
        semantics=("parallel",), prefetch=1, aliases={3: 0})(place, own, r, full)


def _share_with_sibling(f1, f2, lo, hi, *, name):
    def body(_, __, o1_ref, o2_ref, send, recv):
        x, y, c, _ = _place()
        lay, mine, other = pl.ds(lo, hi - lo), pl.ds(c, 1), pl.ds(1 - c, 1)
        cps = [_remote(o1_ref.at[lay, mine], o1_ref.at[lay, mine], send.at[0], recv.at[0], (x, y, 1 - c)),
               _remote(o2_ref.at[lay, mine], o2_ref.at[lay, mine], send.at[1], recv.at[1], (x, y, 1 - c))]
        for cp in cps:
            cp.start()
        for cp in cps:
            cp.wait_send()
        _remote(o1_ref.at[lay, other], o1_ref.at[lay, other], send.at[0], recv.at[0], (x, y, 1 - c)).wait_recv()
        _remote(o2_ref.at[lay, other], o2_ref.at[lay, other], send.at[1], recv.at[1], (x, y, 1 - c)).wait_recv()

    return pl.pallas_call(
        body, name=name, in_specs=[ANY, ANY], out_specs=[ANY, ANY],
        out_shape=[SDS(f1.shape, f1.dtype), SDS(f2.shape, f2.dtype)], input_output_aliases={0: 0, 1: 1},
        scratch_shapes=[pltpu.SemaphoreType.DMA((2,)), pltpu.SemaphoreType.DMA((2,))])(f1, f2)


def _allreduce_small(sg):
    rows, w = sg.shape
    half = rows // 2
    assert half % 8 == 0

    def body(sg_ref, o_ref, sib_buf, part, slots, send, recv):
        x, y, c, flips = _place()
        chip = 2 * x + y
        sib = (x, y, 1 - c)
        mine = pl.ds(pl.multiple_of(c * half, 8), half)
        other = pl.ds(pl.multiple_of((1 - c) * half, 8), half)
        to_sib = _remote(sg_ref.at[other, :], sib_buf, send.at[0], recv.at[0], sib)
        to_sib.start()
        to_sib.wait_recv()
        part[...] = sg_ref[mine, :] + sib_buf[...]
        slots[chip] = part[...]
        to_chips = [_remote(part, slots.at[chip], send.at[1 + j], recv.at[1 + j], (fx, fy, c))
                    for j, (fx, fy) in enumerate(flips)]
        for cp in to_chips:
            cp.start()
        for j, (fx, fy) in enumerate(flips):
            _remote(part, slots.at[2 * fx + fy], send.at[1 + j], recv.at[1 + j], (fx, fy, c)).wait_recv()
        o_ref[mine, :] = ((slots[0] + slots[1]) + slots[2]) + slots[3]
        back = _remote(o_ref.at[mine, :], o_ref.at[mine, :], send.at[4], recv.at[4], sib)
        back.start()
        _remote(o_ref.at[other, :], o_ref.at[other, :], send.at[4], recv.at[4], sib).wait_recv()
        for cp in [to_sib, back] + to_chips:
            cp.wait_send()

    return pl.pallas_call(
        body, name="allreduce_small", in_specs=[VMEM_WHOLE], out_specs=VMEM_WHOLE, out_shape=SDS((rows, w), F32),
        scratch_shapes=[pltpu.VMEM((half, w), F32), pltpu.VMEM((half, w), F32), pltpu.VMEM((N_CHIPS, half, w), F32),
                        pltpu.SemaphoreType.DMA((5,)), pltpu.SemaphoreType.DMA((5,))])(sg)


def _adamw(g_arr, layer0, g_off, per_layer, w, m, v, *, name, tm, layers=None, prev=None):
    rows, cols = w.shape
    assert g_off % tm == 0 and per_layer % tm == 0 and rows % per_layer == 0
    npl = per_layer // tm
    lo, hi = layers or (0, rows // per_layer)
    base = lo * npl
    c1 = 1.0 - ADAM_B1 ** ADAM_STEP
    c2 = 1.0 - ADAM_B2 ** ADAM_STEP

    def body(g_ref, w_ref, m_ref, v_ref, *rest):
        go_ref, d_ref, mo_ref, vo_ref = rest[-4:]
        g = g_ref[0]
        mn = ADAM_B1 * m_ref[...] + (1.0 - ADAM_B1) * g
        vn = ADAM_B2 * v_ref[...] + (1.0 - ADAM_B2) * (g * g)
        go_ref[...] = g
        mo_ref[...] = mn
        vo_ref[...] = vn
        d_ref[...] = -ADAM_LR * ((mn / c1) / (jnp.sqrt(vn / c2) + ADAM_EPS) + ADAM_WD * w_ref[...])

    blk = pl.BlockSpec((tm, cols), lambda i: (base + i, 0))
    extra = list(prev or [])
    return _call(
        body, name, ((hi - lo) * npl,),
        [pl.BlockSpec((1, tm, cols), lambda i: (layer0 + lo + i // npl, g_off // tm + i % npl, 0)), blk, blk, blk]
        + [ANY] * len(extra),
        [blk] * 4,
        [SDS((rows, cols), F32)] * 4, semantics=("parallel",),
        aliases={4 + k: k for k in range(len(extra))})(g_arr, w, m, v, *extra)


def _pack_small(parts, width):
    flat = jnp.concatenate([p.reshape(-1).astype(F32) for p in parts])
    rows = -(-flat.shape[0] // (16 * width)) * 16
    return jnp.pad(flat, (0, rows * width - flat.shape[0])).reshape(rows, width)


def _unpack_small(packed, shapes):
    flat = packed.reshape(-1)
    out, off = [], 0
    for shp in shapes:
        size = 1
        for n in shp:
            size *= n
        out.append(flat[off:off + size].reshape(shp))
        off += size
    return out


def _block_diag(pw):
    g, c, _ = pw.shape
    eye = jnp.eye(g, dtype=pw.dtype)
    return (eye[:, None, :, None] * pw[:, :, None, :]).reshape(g * c, g * c)


def _diag_blocks(full, g):
    c = full.shape[0] // g
    return jnp.stack([full[i * c:(i + 1) * c, i * c:(i + 1) * c] for i in range(g)])


def kernel(x, mem, norm_mix, w_in, pool_w, pool_scale, kv_norm, w_kv, k_norm, q_norm, sinks, mem_norm, w_mem_kv, mem_q_norm, mem_k_norm, w_out, norm_mlp, w_up, w_down, loss_target, m_norm_mix, m_w_in, m_pool_w, m_pool_scale, m_kv_norm, m_w_kv, m_k_norm, m_q_norm, m_sinks, m_mem_norm, m_w_mem_kv, m_mem_q_norm, m_mem_k_norm, m_w_out, m_norm_mlp, m_w_up, m_w_down, v_norm_mix, v_w_in, v_pool_w, v_pool_scale, v_kv_norm, v_w_kv, v_k_norm, v_q_norm, v_sinks, v_mem_norm, v_w_mem_kv, v_mem_q_norm, v_mem_k_norm, v_w_out, v_norm_mlp, v_w_up, v_w_down):
    s_len, d = x.shape[1], x.shape[2]
    n_layers, n_pool = norm_mix.shape[0], pool_w.shape[0]
    n_swa = n_layers - n_pool
    main = d - KVW
    qh = main // HEAD
    ff = w_down.shape[1] * N_CHIPS
    dq = d // N_CHIPS
    assert w_up.shape[2] == d and ff == N_CHIPS * d and w_kv.shape[1] == 2 * KVW
    tm = min(512, s_len)
    tm_mem = mem.shape[1]

    cx, cy, cc = lax.axis_index("x"), lax.axis_index("y"), lax.axis_index("c")
    chip = 2 * cx + cy
    place = jnp.stack([cc, chip]).astype(jnp.int32)

    off_down, off_up, off_in, off_out = 0, d, 2 * d, 2 * d + dq
    rows1 = off_out + dq
    off_mkv, off_kv = 0, dq
    rows2 = 2 * dq

    ps = jnp.pad(pool_scale, ((0, 8 - n_pool), (0, 2 * LANE - pool_scale.shape[1])))
    psg = _gather_small(ps)
    pool_scale_full = jnp.concatenate([psg[k, :n_pool, :pool_scale.shape[1]] for k in range(N_CHIPS)], axis=1)

    def packed_weights(l):
        p1 = jnp.concatenate([w_down[l], w_up[l], w_in[l], w_out[l]]).astype(BF)
        p2 = jnp.concatenate([w_mem_kv[l], w_kv] if l == n_pool else [w_mem_kv[l]]).astype(BF)
        return p1, p2

    def gather_start(l, after):
        p1, p2 = packed_weights(l)
        bufs = (p1, p2, lax.empty((N_CHIPS,) + p1.shape, BF), lax.empty((N_CHIPS,) + p2.shape, BF))
        return _split_start(_gather_copies, 6, bufs, after, None, name=f"gather_start_{l}")

    def gather_land(l, started, after):
        bufs = _split_wait(_gather_copies, started, after, name=f"gather_wait_{l}")
        return _split_start(_forward_copies, 8, bufs, place, None, name=f"forward_start_{l}")

    def gather_finish(l, forwarding, after):
        bufs = _split_wait(_forward_copies, forwarding, after, name=f"forward_wait_{l}")
        return bufs[2], bufs[3]

    def w_rows(arr, off, nrows, width):
        assert off % nrows == 0
        return (arr, (N_CHIPS, nrows, width), lambda j: (0, off // nrows, 0))

    row = lambda a: a.reshape(1, -1)
    h = x.reshape(s_len, d)
    memx = mem.reshape(tm_mem, d)
    tgt = loss_target.reshape(s_len, d)
    pbd = [_block_diag(pool_w[l]).astype(BF) for l in range(n_pool)]
    sinks_pad = [jnp.pad(row(sinks[j]), ((0, 0), (0, LANE - qh))) for j in range(n_swa)]

    w_in_l, w_out_l, w_down_l, w_up_all_l, w_mkv_l = [], [], [], [], []
    w_kv_g = None
    forwarding = gather_land(0, gather_start(0, psg), psg)
    travelling = gather_start(1, forwarding[6]) if n_layers > 1 else None
    saved, probs, sink_probs, mem_probs = [], {}, {}, {}
    kv = hn_kv = kn = vsh = None
    for l in range(n_layers):
        wg1, wg2 = gather_finish(l, forwarding, h if l else forwarding[6])
        w_in_l.append(w_rows(wg1, off_in, dq, d))
        w_out_l.append(w_rows(wg1, off_out, dq, d))
        w_down_l.append(w_rows(wg1, off_down, d, d))
        w_up_all_l.append((wg1, (N_CHIPS, d, d), lambda j: (0, off_up // d, 0)))
        w_mkv_l.append(w_rows(wg2, off_mkv, dq, 2 * KVW))
        g_mix = row(norm_mix[l])
        if travelling is not None:
            g_mix = g_mix + travelling[6][0, 0]
        if l == n_pool:
            w_kv_g = w_rows(wg2, off_kv, dq, 2 * KVW)
            kv, hn_kv = _norm_mm(h, row(kv_norm), w_kv_g, 1, 2 * KVW, act=False, name="kv_proj", tm=tm)
            kn, vsh = _kv_prep(kv, row(k_norm), tm=tm)
        h0 = h
        proj, xn = _norm_mm(h0, g_mix, w_in_l[l], 1, d, act=False, name=f"in_proj_{l}", tm=tm)
        mkv, memn = _norm_mm(memx, row(mem_norm[l]), w_mkv_l[l], 1, 2 * KVW, act=False, name=f"mem_kv_{l}", tm=tm_mem)
        if l < n_pool:
            cat, mem_probs[l] = _mixer_pool_fwd(proj, mkv, pbd[l], row(pool_scale_full[l]), row(mem_q_norm[l]),
                                                row(mem_k_norm[l]), name=f"mixer_fwd_{l}", tm=tm)
        else:
            j = l - n_pool
            cat, probs[l], sink_probs[l] = _mixer_swa_fwd(proj, kn, vsh, row(q_norm[j]), sinks_pad[j],
                                                          name=f"mixer_fwd_{l}")
            cat, mem_probs[l] = _mem_attn_fwd(proj, mkv, row(mem_q_norm[l]), row(mem_k_norm[l]), cat,
                                              name=f"mem_attn_fwd_{l}", tm=tm)
        h1 = _mm_res(h0, cat, w_out_l[l], name=f"out_proj_{l}", tm=tm)
        hh, xm = _norm_mm(h1, row(norm_mlp[l]), w_up_all_l[l], N_CHIPS, d, act=True, name=f"mlp_up_{l}", tm=tm)
        after = None
        if travelling is not None:
            forwarding = gather_land(l + 1, travelling, hh)
            travelling = gather_start(l + 2, forwarding[6]) if l + 2 < n_layers else None
            after = forwarding[6]
        h = _mm_res(h1, hh, w_down_l[l], name=f"mlp_down_{l}", tm=tm, after=after)
        saved.append((h0, proj, xn, mkv, memn, cat, h1, hh, xm))

    dh, dh_bf, loss_part = _loss_head(h, tgt, tm=tm)
    loss = lax.psum(loss_part[0, 0], ("x", "y", "c"))

    half1, half2 = rows1 // 2, rows2 // 2
    g1 = lax.empty((N_CHIPS, rows1, d), F32)
    pending = {}
    swapping = None
    tk = min(512, d)

    def reduce_begin(l, swapped, after):
        g1_l, g2_l, r1, r2 = _split_wait(_swap_copies, swapped, after, name=f"swap_wait_{l}")
        pb1, own1 = _sum_sibling(g1_l, r1, place, tm=_tile(half1, 256), name=f"sum_sibling_a_{l}")
        pb2, own2 = _sum_sibling(g2_l, r2, place, tm=_tile(half2, 256), name=f"sum_sibling_b_{l}")
        bufs = (pb1, pb2, lax.empty((3, half1, d), BF), lax.empty((3, half2, 2 * KVW), BF))
        return _split_start(_rs_copies, 6, bufs, place, None, name=f"reduce_start_{l}"), own1, own2
    zeros_mem = jnp.zeros((tm_mem, d), F32)

    def rows_map(off, nrows, tkk):
        per = nrows // tkk
        return lambda i, j: (i // per, off // tkk + i % per, 0)

    def cols_map(off, tkk):
        return lambda i, j: (j, off // tkk + i, 0)

    d_norm_mix, d_norm_mlp, d_mem_norm = [None] * n_layers, [None] * n_layers, [None] * n_layers
    d_mem_q, d_mem_k = [None] * n_layers, [None] * n_layers
    d_pool_w, d_pool_scale = [None] * n_pool, [None] * n_pool
    d_q_norm, d_sinks = [None] * n_swa, [None] * n_swa
    dks, dvs = [], []
    d_kv_norm = d_k_norm = None
    for l in reversed(range(n_layers)):
        h0, proj, xn, mkv, memn, cat, h1, hh, xm = saved[l]
        g2 = jnp.zeros((N_CHIPS, rows2, 2 * KVW), F32)
        g1 = _mm_tn(hh, dh_bf, g1, rows_map(off_down, d, tk), tk, d, name=f"dw_down_{l}")
        du = _mm_nt_relu2(dh_bf, hh, w_down_l[l], N_CHIPS, name=f"d_mlp_act_{l}", tm=tm)
        g1 = _mm_tn(xm, du, g1, cols_map(off_up, tk), tk, d, name=f"dw_up_{l}")
        g_mlp = row(norm_mlp[l])
        if swapping is not None:
            pending[swapping[0]] = reduce_begin(*swapping, after=g1)
            g_mlp = g_mlp + pending[swapping[0]][0][6][0, 0]
        dh1, dh1_bf, d_norm_mlp[l] = _mm_nt_normbwd(du, w_up_all_l[l], N_CHIPS, h1, g_mlp, dh,
                                                    name=f"d_mlp_in_{l}", tm=tm)
        tkq = min(tk, dq)
        g1 = _mm_tn(cat, dh1_bf, g1, rows_map(off_out, dq, tkq), tkq, d, name=f"dw_out_{l}")
        dcat = _mm_nt(dh1_bf, w_out_l[l], d, name=f"d_cat_{l}", tm=tm)
        if l < n_pool:
            dproj, dpbd, dscale, dmkv, d_mem_q[l], d_mem_k[l] = _mixer_pool_bwd(
                proj, dcat, mkv, pbd[l], row(pool_scale_full[l]), row(mem_q_norm[l]), row(mem_k_norm[l]),
                mem_probs[l], name=f"mixer_bwd_{l}", tm=tm)
            d_pool_w[l] = _diag_blocks(dpbd, len(POOL_WINDOWS))
            d_pool_scale[l] = dscale
        else:
            j = l - n_pool
            dproj, dk, dv, d_q_norm[j], dsk = _mixer_swa_bwd(proj, dcat, kn, vsh, row(q_norm[j]), probs[l],
                                                             sink_probs[l], name=f"mixer_bwd_{l}")
            dproj, dmkv, d_mem_q[l], d_mem_k[l] = _mem_attn_bwd(
                proj, dcat, mkv, row(mem_q_norm[l]), row(mem_k_norm[l]), mem_probs[l], dproj,
                name=f"mem_attn_bwd_{l}", tm=tm)
            d_sinks[j] = dsk[0, :qh]
            dks.append(dk)
            dvs.append(dv)
        g1 = _mm_tn(xn, dproj, g1, rows_map(off_in, dq, tkq), tkq, d, name=f"dw_in_{l}")
        dh, dh_bf, d_norm_mix[l] = _mm_nt_normbwd(dproj, w_in_l[l], 1, h0, row(norm_mix[l]), dh1,
                                                  name=f"d_in_{l}", tm=tm)
        g2 = _mm_tn(memn, dmkv, g2, rows_map(off_mkv, dq, tkq), tkq, 2 * KVW, name=f"dw_mem_kv_{l}")
        _, _, d_mem_norm[l] = _mm_nt_normbwd(dmkv, w_mkv_l[l], 1, memx, row(mem_norm[l]), zeros_mem,
                                             name=f"d_mem_norm_{l}", tm=tm_mem)
        if l == n_pool:
            dkv, d_k_norm = _kv_bwd(kv, dks, dvs, row(k_norm), tm=tm)
            g2 = _mm_tn(hn_kv, dkv, g2, rows_map(off_kv, dq, tkq), tkq, 2 * KVW, name="dw_kv")
            dh, dh_bf, d_kv_norm = _mm_nt_normbwd(dkv, w_kv_g, 1, h0, row(kv_norm), dh, name="d_kv_in", tm=tm)
        bufs = (g1, g2, lax.empty((N_CHIPS, half1, d), F32), lax.empty((N_CHIPS, half2, 2 * KVW), F32))
        swapping = (l, _split_start(_swap_copies, 2, bufs, place, (N_CHIPS, rows1, d) if l > 0 else None,
                                    name=f"swap_start_{l}"))
        g1 = swapping[1][7] if l > 0 else None
    grad_x = dh.reshape(x.shape)

    small_names = ["norm_mix", "pool_w", "pool_scale", "kv_norm", "k_norm", "q_norm", "sinks", "mem_norm",
                   "mem_q_norm", "mem_k_norm", "norm_mlp"]
    small_grads = {
        "norm_mix": jnp.concatenate(d_norm_mix), "pool_w": jnp.stack(d_pool_w),
        "pool_scale": jnp.concatenate(d_pool_scale), "kv_norm": d_kv_norm[0], "k_norm": d_k_norm[0],
        "q_norm": jnp.concatenate(d_q_norm), "sinks": jnp.stack(d_sinks), "mem_norm": jnp.concatenate(d_mem_norm),
        "mem_q_norm": jnp.concatenate(d_mem_q), "mem_k_norm": jnp.concatenate(d_mem_k),
        "norm_mlp": jnp.concatenate(d_norm_mlp)}
    width = d
    sg = _pack_small([small_grads[n] for n in small_names], width)
    sg = sg + swapping[1][6][0, 0]
    sg = _allreduce_small(sg)
    pending[0] = reduce_begin(*swapping, after=sg)
    reduced = dict(zip(small_names, _unpack_small(sg, [small_grads[n].shape for n in small_names])))
    psw = pool_scale.shape[1]
    reduced["pool_scale"] = lax.dynamic_slice_in_dim(reduced["pool_scale"], chip * psw, psw, axis=1)
    params = dict(norm_mix=(norm_mix, m_norm_mix, v_norm_mix), pool_w=(pool_w, m_pool_w, v_pool_w),
                  pool_scale=(pool_scale, m_pool_scale, v_pool_scale), kv_norm=(kv_norm, m_kv_norm, v_kv_norm),
                  k_norm=(k_norm, m_k_norm, v_k_norm), q_norm=(q_norm, m_q_norm, v_q_norm),
                  sinks=(sinks, m_sinks, v_sinks), mem_norm=(mem_norm, m_mem_norm, v_mem_norm),
                  mem_q_norm=(mem_q_norm, m_mem_q_norm, v_mem_q_norm),
                  mem_k_norm=(mem_k_norm, m_mem_k_norm, v_mem_k_norm), norm_mlp=(norm_mlp, m_norm_mlp, v_norm_mlp))
    shapes = [params[n][0].shape for n in small_names]
    packs = [_pack_small([reduced[n].reshape(params[n][0].shape) for n in small_names], width)
             + pending[0][0][6][0, 0]]
    packs += [_pack_small([params[n][t] for n in small_names], width) for t in range(3)]
    res = _adamw(packs[0][None], 0, 0, packs[0].shape[0], packs[1], packs[2], packs[3], name="adamw_small", tm=8)
    small = {n: [] for n in small_names}
    for r in res:
        for n, a in zip(small_names, _unpack_small(r, shapes)):
            small[n].append(a)

    full1 = lax.empty((n_layers, 2, half1, d), F32)
    full2 = lax.empty((n_layers, 2, half2, 2 * KVW), F32)
    kinds = (("w_down", 1, off_down, d, w_down, m_w_down, v_w_down), ("w_up", 1, off_up, d, w_up, m_w_up, v_w_up),
             ("w_in", 1, off_in, dq, w_in, m_w_in, v_w_in), ("w_out", 1, off_out, dq, w_out, m_w_out, v_w_out),
             ("w_mem_kv", 2, off_mkv, dq, w_mem_kv, m_w_mem_kv, v_w_mem_kv))
    big = {}
    first = 1 if n_layers > 1 and n_pool > 0 else 0
    after = res[0]
    for lo, hi in ((first, n_layers), (0, first)):
        if lo == hi:
            continue
        for l in reversed(range(lo, hi)):
            exchange, own1, own2 = pending[l]
            _, _, x1, x2 = _split_wait(_rs_copies, exchange, after, name=f"reduce_wait_{l}")
            full1 = _sum_chips(own1, x1, full1, l, place, tm=_tile(half1, 256), name=f"sum_chips_a_{l}")
            full2 = _sum_chips(own2, x2, full2, l, place, tm=_tile(half2, 256), name=f"sum_chips_b_{l}")
        full1, full2 = _share_with_sibling(full1, full2, lo, hi, name=f"share_with_sibling_{lo}")
        views = (None, full1.reshape(n_layers, rows1, d), full2.reshape(n_layers, rows2, 2 * KVW))
        for name, which, off, per, w_, m_, v_ in kinds:
            cols = views[which].shape[2]
            big[name] = _adamw(views[which], 0, off, per, w_.reshape(-1, cols), m_.reshape(-1, cols),
                               v_.reshape(-1, cols), name=f"adamw_{name}_{lo}", tm=min(256, dq), layers=(lo, hi),
                               prev=big.get(name))
        if lo <= n_pool < hi:
            big["w_kv"] = _adamw(views[2], n_pool, off_kv, dq, w_kv, m_w_kv, v_w_kv, name="adamw_w_kv",
                                 tm=min(256, dq))
        after = big["w_mem_kv"][1]
    shapes_big = dict(w_down=w_down.shape, w_up=w_up.shape, w_in=w_in.shape, w_out=w_out.shape,
                      w_mem_kv=w_mem_kv.shape, w_kv=w_kv.shape)
    big = {n: [r.reshape(shapes_big[n]) for r in big[n]] for n in big}

    order = ["norm_mix", "w_in", "pool_w", "pool_scale", "kv_norm", "w_kv", "k_norm", "q_norm", "sinks", "mem_norm",
             "w_mem_kv", "mem_q_norm", "mem_k_norm", "w_out", "norm_mlp", "w_up", "w_down"]
    out = {**big, **small}
    return (loss, grad_x, *[out[n][0] for n in order], *[out[n][1] for n in order],
            *[out[n][2] for n in order], *[out[n][3] for n in order])
```

```python
import functools

import jax
import jax.numpy as jnp
from jax import lax
from jax.experimental import pallas as pl
from jax.experimental.pallas import tpu as pltpu

F32, BF = jnp.float32, jnp.bfloat16
SDS = jax.ShapeDtypeStruct
MESH = pl.DeviceIdType.MESH
ANY = pl.BlockSpec(memory_space=pl.ANY)
HBM = pl.BlockSpec(memory_space=pltpu.HBM)
SEM = pl.BlockSpec(memory_space=pltpu.SEMAPHORE)
VMEM_WHOLE = pl.BlockSpec(memory_space=pltpu.VMEM)
SIDE_EFFECT = pltpu.SideEffectType.DATAFLOW_SIDE_EFFECTING


def _in_hbm(a):
    return pltpu.with_memory_space_constraint(a, pltpu.HBM)


EPS = 1e-6
HEAD = 64
KV_HEADS = 4
KVW = KV_HEADS * HEAD
WINDOW = 128
POOL_WINDOWS = (2, 4, 8, 16)
HALO = 16
QK_SCALE = HEAD ** -0.5
NEG = float(jnp.finfo(jnp.float32).min)
N_CHIPS = 4
LANE = 128

ADAM_LR, ADAM_B1, ADAM_B2, ADAM_EPS, ADAM_WD, ADAM_STEP = 0.001, 0.9, 0.999, 1e-08, 0.01, 10

VMEM_LIMIT_MB = 56


def _call(body, name, grid, in_specs, out_specs, out_shape, *, scratch=(), semantics=None, aliases=None,
          prefetch=0):
    params = pltpu.CompilerParams(dimension_semantics=semantics, vmem_limit_bytes=VMEM_LIMIT_MB << 20)
    if prefetch:
        spec = pltpu.PrefetchScalarGridSpec(num_scalar_prefetch=prefetch, grid=grid, in_specs=in_specs,
                                            out_specs=out_specs, scratch_shapes=list(scratch))
        return pl.pallas_call(body, name=name, grid_spec=spec, out_shape=out_shape,
                              input_output_aliases=aliases or {}, compiler_params=params)
    return pl.pallas_call(body, name=name, grid=grid, in_specs=in_specs, out_specs=out_specs, out_shape=out_shape,
                          scratch_shapes=list(scratch), input_output_aliases=aliases or {}, compiler_params=params)


def _tile(n, pref):
    return max(t for t in range(8, min(n, pref) + 1, 8) if n % t == 0)


def _dot(a, b):
    return jnp.dot(a, b, preferred_element_type=F32)


def _dot_nt(a, b):
    return lax.dot_general(a, b, (((1,), (1,)), ((), ())), preferred_element_type=F32)


def _dot_tn(a, b):
    return lax.dot_general(a, b, (((0,), (0,)), ((), ())), preferred_element_type=F32)


def _rms(x):
    r = lax.rsqrt(jnp.mean(x * x, axis=-1, keepdims=True) + EPS)
    return x * r, r


def _rms_bwd(dy, xh, r, g):
    dg = jnp.sum(dy * xh, axis=0, keepdims=True)
    dyg = dy * g
    dx = r * (dyg - xh * jnp.mean(dyg * xh, axis=-1, keepdims=True))
    return dx, dg


def _norm_mm(h, g, w, nj, tn, *, act, name, tm):
    w_arr, w_block, w_imap = w
    rows, d = h.shape

    def body(h_ref, g_ref, w_ref, y_ref, xn_ref):
        xh, _ = _rms(h_ref[...])
        xn = (xh * g_ref[...]).astype(BF)
        xn_ref[...] = xn
        for j in range(nj):
            u = _dot(xn, w_ref[j] if nj > 1 else w_ref[...].reshape(d, tn))
            if act:
                a = jnp.maximum(u, 0.0)
                y_ref[:, j * tn:(j + 1) * tn] = (a * a).astype(BF)
            else:
                y_ref[:, j * tn:(j + 1) * tn] = u

    assert nj == 1 or w_block[0] == nj
    return _call(
        body, name, (rows // tm,),
        [pl.BlockSpec((tm, d), lambda i: (i, 0)), pl.BlockSpec((1, d), lambda i: (0, 0)),
         pl.BlockSpec(w_block, lambda i: w_imap(0))],
        [pl.BlockSpec((tm, nj * tn), lambda i: (i, 0)), pl.BlockSpec((tm, d), lambda i: (i, 0))],
        [SDS((rows, nj * tn), BF if act else F32), SDS((rows, d), BF)],
        semantics=("parallel",))(h, g, w_arr)


def _mm_res(res, a, w, *, name, tm, after=None):
    w_arr, w_block, w_imap = w
    rows, k = a.shape
    n = res.shape[1]

    def body(res_ref, a_ref, w_ref, *rest):
        rest[-1][...] = res_ref[...] + _dot(a_ref[...], w_ref[...].reshape(k, n))

    extra = [] if after is None else [after]
    return _call(
        body, name, (rows // tm,),
        [pl.BlockSpec((tm, n), lambda i: (i, 0)), pl.BlockSpec((tm, k), lambda i: (i, 0)),
         pl.BlockSpec(w_block, lambda i: w_imap(0))] + [ANY] * len(extra),
        pl.BlockSpec((tm, n), lambda i: (i, 0)), SDS((rows, n), F32), semantics=("parallel",))(res, a, w_arr, *extra)


def _mm_nt(dy, w, k, *, name, tm):
    w_arr, w_block, w_imap = w
    rows, n = dy.shape

    def body(dy_ref, w_ref, o_ref):
        o_ref[...] = _dot_nt(dy_ref[...], w_ref[...].reshape(k, n))

    return _call(
        body, name, (rows // tm,),
        [pl.BlockSpec((tm, n), lambda i: (i, 0)), pl.BlockSpec(w_block, lambda i: w_imap(0))],
        pl.BlockSpec((tm, k), lambda i: (i, 0)), SDS((rows, k), F32), semantics=("parallel",))(dy, w_arr)


def _mm_nt_relu2(dh, hh, w, nj, *, name, tm):
    w_arr, w_block, w_imap = w
    rows, d = dh.shape
    tk = hh.shape[1] // nj

    def body(dh_ref, hh_ref, w_ref, o_ref):
        dh_t = dh_ref[...]
        for j in range(nj):
            cols = slice(j * tk, (j + 1) * tk)
            dhh = _dot_nt(dh_t, w_ref[j])
            o_ref[:, cols] = (dhh * (2.0 * jnp.sqrt(hh_ref[:, cols].astype(F32)))).astype(BF)

    assert w_block[0] == nj
    return _call(
        body, name, (rows // tm,),
        [pl.BlockSpec((tm, d), lambda i: (i, 0)), pl.BlockSpec((tm, nj * tk), lambda i: (i, 0)),
         pl.BlockSpec(w_block, lambda i: w_imap(0))],
        pl.BlockSpec((tm, nj * tk), lambda i: (i, 0)), SDS((rows, nj * tk), BF),
        semantics=("parallel",))(dh, hh, w_arr)


def _mm_nt_normbwd(dy, w, nsplit, h, g, dres, *, name, tm):
    w_arr, w_block, w_imap = w
    rows, n = dy.shape
    d = h.shape[1]
    ns = n // nsplit

    def body(dy_ref, w_ref, h_ref, g_ref, dres_ref, o_ref, obf_ref, dg_ref):
        if nsplit == 1:
            dxn = _dot_nt(dy_ref[...].astype(BF), w_ref[...].reshape(d, n))
        else:
            dxn = _dot_nt(dy_ref[:, 0:ns].astype(BF), w_ref[0])
            for s in range(1, nsplit):
                dxn += _dot_nt(dy_ref[:, s * ns:(s + 1) * ns].astype(BF), w_ref[s])
        xh, r = _rms(h_ref[...])
        dx, dg = _rms_bwd(dxn, xh, r, g_ref[...])
        out = dres_ref[...] + dx
        o_ref[...] = out
        obf_ref[...] = out.astype(BF)

        @pl.when(pl.program_id(0) == 0)
        def _():
            dg_ref[...] = jnp.zeros_like(dg_ref)

        dg_ref[...] += dg

    row = lambda i: (i, 0)
    return _call(
        body, name, (rows // tm,),
        [pl.BlockSpec((tm, n), row), pl.BlockSpec(w_block, lambda i: w_imap(0)), pl.BlockSpec((tm, d), row),
         pl.BlockSpec((1, d), lambda i: (0, 0)), pl.BlockSpec((tm, d), row)],
        [pl.BlockSpec((tm, d), row), pl.BlockSpec((tm, d), row), pl.BlockSpec((1, d), lambda i: (0, 0))],
        [SDS((rows, d), F32), SDS((rows, d), BF), SDS((1, d), F32)],
        semantics=("arbitrary",))(dy, w_arr, h, g, dres)


def _mm_tn(x, dy, packed, out_imap, tk, tn, *, name):
    s_len, k = x.shape
    n = dy.shape[1]

    def body(x_ref, dy_ref, _, o_ref):
        o_ref[0] = _dot_tn(x_ref[...], dy_ref[...].astype(BF))

    return _call(
        body, name, (k // tk, n // tn),
        [pl.BlockSpec((s_len, tk), lambda i, j: (0, i)), pl.BlockSpec((s_len, tn), lambda i, j: (0, j)), ANY],
        pl.BlockSpec((1, tk, tn), out_imap), SDS(packed.shape, packed.dtype),
        semantics=("parallel", "parallel"), aliases={2: 0})(x, dy, packed)


def _loss_head(y, tgt, *, tm):
    rows, d = y.shape

    def body(y_ref, t_ref, dh_ref, dhbf_ref, loss_ref):
        err = y_ref[...] - t_ref[...]
        dh = err * (1.0 / d)
        dh_ref[...] = dh
        dhbf_ref[...] = dh.astype(BF)

        @pl.when(pl.program_id(0) == 0)
        def _():
            loss_ref[...] = jnp.zeros_like(loss_ref)

        loss_ref[...] += 0.5 * jnp.sum(jnp.mean(err * err, axis=-1, keepdims=True), axis=0, keepdims=True)

    row = lambda i: (i, 0)
    return _call(
        body, "loss_head", (rows // tm,), [pl.BlockSpec((tm, d), row), pl.BlockSpec((tm, d), row)],
        [pl.BlockSpec((tm, d), row), pl.BlockSpec((tm, d), row), pl.BlockSpec((1, 1), lambda i: (0, 0))],
        [SDS((rows, d), F32), SDS((rows, d), BF), SDS((1, 1), F32)], semantics=("arbitrary",))(y, tgt)


def _hs(h):
    return slice(HEAD * h, HEAD * (h + 1))


def _softmax_rows(s):
    e = jnp.exp(s - jnp.max(s, axis=-1, keepdims=True))
    return e * (1.0 / jnp.sum(e, axis=-1, keepdims=True))


def _scaled_bf16(qn):
    return (qn * QK_SCALE).astype(BF)


def _mem_fwd(mq, mk, mv, gq):
    outs, probs = [], []
    for h in range(KV_HEADS):
        xh, _ = _rms(mq[:, _hs(h)])
        p = _softmax_rows(_dot_nt(_scaled_bf16(xh * gq), mk[:, _hs(h)])).astype(BF)
        probs.append(p)
        outs.append(_dot(p, mv[:, _hs(h)]))
    return jnp.concatenate(outs, axis=-1), jnp.concatenate(probs, axis=-1)


def _mem_bwd(mq, do, mk, mv, gq, probs):
    dqs, dks, dvs, dgq = [], [], [], 0.0
    mlen = mk.shape[0]
    for h in range(KV_HEADS):
        xh, r = _rms(mq[:, _hs(h)])
        qn = _scaled_bf16(xh * gq)
        p_bf = probs[:, h * mlen:(h + 1) * mlen]
        p = p_bf.astype(F32)
        doh = do[:, _hs(h)].astype(BF)
        dp = _dot_nt(doh, mv[:, _hs(h)])
        ds = (p * (dp - jnp.sum(p * dp, axis=-1, keepdims=True))).astype(BF)
        dq, dg = _rms_bwd(_dot(ds, mk[:, _hs(h)]) * QK_SCALE, xh, r, gq)
        dqs.append(dq)
        dgq = dgq + dg
        dks.append(_dot_tn(ds, qn))
        dvs.append(_dot_tn(p_bf, doh))
    cat = lambda xs: jnp.concatenate(xs, axis=-1)
    return cat(dqs), cat(dks), cat(dvs), dgq


def _mem_kv(mkv, gk):
    ks = []
    for h in range(KV_HEADS):
        xh, _ = _rms(mkv[:, _hs(h)])
        ks.append(xh * gk)
    return jnp.concatenate(ks, axis=-1).astype(BF), mkv[:, KVW:].astype(BF)


def _mem_kv_bwd(mkv, dmk, dmv, gk):
    dxs, dgk = [], 0.0
    for h in range(KV_HEADS):
        xh, r = _rms(mkv[:, _hs(h)])
        dx, dg = _rms_bwd(dmk[:, _hs(h)], xh, r, gk)
        dxs.append(dx)
        dgk = dgk + dg
    return jnp.concatenate(dxs + [dmv], axis=-1), dgk


def _pool_select(col, gd, a2, a4, a8, a16):
    return jnp.where(col < gd, a2, jnp.where(col < 2 * gd, a4, jnp.where(col < 3 * gd, a8, a16)))


def _pool_count(t0, shape, gd):
    col = lax.broadcasted_iota(jnp.int32, shape, 1)
    t = t0 + lax.broadcasted_iota(jnp.int32, shape, 0)
    win = _pool_select(col, gd, *POOL_WINDOWS)
    return jnp.minimum(t + 1, win).astype(F32)


def _pool_diff(u, halo, t0, gd):
    c = jnp.concatenate([halo, u], axis=0)
    s2 = c + pltpu.roll(c, 1, 0)
    s4 = s2 + pltpu.roll(s2, 2, 0)
    s8 = s4 + pltpu.roll(s4, 4, 0)
    s16 = s8 + pltpu.roll(s8, 8, 0)
    col = lax.broadcasted_iota(jnp.int32, c.shape, 1)
    ws = _pool_select(col, gd, s2, s4, s8, s16)[HALO:]
    return ws / _pool_count(t0, u.shape, gd) - u


def _pool_diff_bwd(dd, dd_halo, t0, gd):
    t = dd.shape[0]
    z = jnp.concatenate([dd / _pool_count(t0, dd.shape, gd), dd_halo / _pool_count(t0 + t, dd_halo.shape, gd)], axis=0)
    n = z.shape[0]
    f2 = z + pltpu.roll(z, n - 1, 0)
    f4 = f2 + pltpu.roll(f2, n - 2, 0)
    f8 = f4 + pltpu.roll(f4, n - 4, 0)
    f16 = f8 + pltpu.roll(f8, n - 8, 0)
    col = lax.broadcasted_iota(jnp.int32, z.shape, 1)
    return _pool_select(col, gd, f2, f4, f8, f16)[:t] - dd


def _swa_bias(n):
    qi = lax.broadcasted_iota(jnp.int32, (WINDOW, 2 * WINDOW), 0)
    kj = lax.broadcasted_iota(jnp.int32, (WINDOW, 2 * WINDOW), 1)
    dist = qi + WINDOW - kj
    valid = (dist >= 0) & (dist < WINDOW) & ((kj >= WINDOW) | (n > 0))
    return dist.astype(F32), valid


def _slopes(qh):
    return [2.0 ** (-8.0 * (h + 1) / qh) for h in range(qh)]


def _swa_probs(qn, kk, dist, valid, slope, sink):
    s = _dot_nt(qn, kk) - slope * dist
    s = jnp.where(valid, s, NEG)
    m = jnp.maximum(jnp.max(s, axis=-1, keepdims=True), sink)
    e = jnp.exp(s - m)
    es = jnp.exp(sink - m)
    z = jnp.sum(e, axis=-1, keepdims=True) + es
    inv = 1.0 / z
    return e * inv, es * inv


def _stack_heads(a, kh, grp):
    return jnp.concatenate([a[:, _hs(h)] for h in range(kh * grp, (kh + 1) * grp)], axis=0)


def _swa_group(q, kh, grp, n, qh, sinks):
    heads = range(kh * grp, (kh + 1) * grp)
    dist, valid = _swa_bias(n)
    slopes = _slopes(qh)
    rows = lambda vals: jnp.concatenate([jnp.broadcast_to(v, (WINDOW, 1)) for v in vals], axis=0)
    slope = rows([jnp.full((1, 1), slopes[h], F32) for h in heads])
    sink = rows([sinks[:, h:h + 1] for h in heads])
    return (_stack_heads(q, kh, grp), slope, sink, jnp.concatenate([dist] * grp, axis=0),
            jnp.concatenate([valid] * grp, axis=0))


def _swa_fwd(q, kk, vv, gq, sinks, n, qh):
    grp = qh // KV_HEADS
    lane = lax.broadcasted_iota(jnp.int32, (WINDOW, LANE), 1)
    outs, probs, sink_probs = [], [], jnp.zeros((WINDOW, LANE), F32)
    for kh in range(KV_HEADS):
        qs, slope, sink, dist, valid = _swa_group(q, kh, grp, n, qh, sinks)
        xh, _ = _rms(qs)
        p, ps = _swa_probs(_scaled_bf16(xh * gq), kk[:, _hs(kh)], dist, valid, slope, sink)
        p = p.astype(BF)
        probs.append(p)
        o = _dot(p, vv[:, _hs(kh)])
        for g in range(grp):
            outs.append(o[g * WINDOW:(g + 1) * WINDOW])
            sink_probs = jnp.where(lane == kh * grp + g, ps[g * WINDOW:(g + 1) * WINDOW], sink_probs)
    return jnp.concatenate(outs, axis=-1), probs, sink_probs


def _swa_bwd(q, do, kk, vv, gq, probs, sink_probs, qh):
    grp = qh // KV_HEADS
    lane = lax.broadcasted_iota(jnp.int32, (1, LANE), 1)
    dqs, dks, dvs, dgq, dsk = [], [], [], 0.0, jnp.zeros((1, LANE), F32)
    do = do.astype(BF)
    for kh in range(KV_HEADS):
        heads = range(kh * grp, (kh + 1) * grp)
        xh, r = _rms(_stack_heads(q, kh, grp))
        qn = _scaled_bf16(xh * gq)
        p_bf = probs[kh]
        p = p_bf.astype(F32)
        ps = jnp.concatenate([sink_probs[:, h:h + 1] for h in heads], axis=0)
        dos = _stack_heads(do, kh, grp)
        dp = _dot_nt(dos, vv[:, _hs(kh)])
        delta = jnp.sum(p * dp, axis=-1, keepdims=True)
        ds = (p * (dp - delta)).astype(BF)
        dsink = ps * delta
        for g in range(grp):
            part = -jnp.sum(dsink[g * WINDOW:(g + 1) * WINDOW], axis=0, keepdims=True)
            dsk = dsk + jnp.where(lane == kh * grp + g, part, 0.0)
        dq, dg = _rms_bwd(_dot(ds, kk[:, _hs(kh)]) * QK_SCALE, xh, r, gq)
        dqs += [dq[g * WINDOW:(g + 1) * WINDOW] for g in range(grp)]
        dgq = dgq + dg
        dks.append(_dot_tn(ds, qn))
        dvs.append(_dot_tn(p_bf, dos))
    cat = lambda xs: jnp.concatenate(xs, axis=-1)
    return cat(dqs), cat(dks), cat(dvs), dgq, dsk


def _mixer_pool_fwd(proj, mkv, pbd, scale, gq, gk, *, name, tm):
    s_len, d = proj.shape
    main = d - KVW
    gd = main // len(POOL_WINDOWS)
    mlen = mkv.shape[0]
    hb = tm // HALO

    def body(u_ref, halo_ref, mq_ref, mkv_ref, pbd_ref, scale_ref, gq_ref, gk_ref, o_ref, mp_ref, mk_s, mv_s):
        i = pl.program_id(0)

        @pl.when(i == 0)
        def _():
            mk, mv = _mem_kv(mkv_ref[...], gk_ref[...])
            mk_s[...] = mk
            mv_s[...] = mv

        halo = jnp.where(i > 0, halo_ref[...], 0.0)
        dif = _pool_diff(u_ref[...], halo, i * tm, gd)
        mixed = _dot(dif.astype(BF), pbd_ref[...]) * scale_ref[...]
        mem, mp_ref[...] = _mem_fwd(mq_ref[...], mk_s[...], mv_s[...], gq_ref[...])
        o_ref[...] = jnp.concatenate([mixed, mem], axis=-1).astype(BF)

    full = lambda shape: pl.BlockSpec(shape, lambda i: (0,) * len(shape))
    return _call(
        body, name, (s_len // tm,),
        [pl.BlockSpec((tm, main), lambda i: (i, 0)),
         pl.BlockSpec((HALO, main), lambda i: (jnp.maximum(i * hb - 1, 0), 0)),
         pl.BlockSpec((tm, KVW), lambda i: (i, main // KVW)),
         full((mlen, 2 * KVW)), full((main, main)), full((1, main)), full((1, HEAD)), full((1, HEAD))],
        [pl.BlockSpec((tm, d), lambda i: (i, 0)), pl.BlockSpec((tm, KV_HEADS * mlen), lambda i: (i, 0))],
        [SDS((s_len, d), BF), SDS((s_len, KV_HEADS * mlen), BF)],
        scratch=[pltpu.VMEM((mlen, KVW), BF), pltpu.VMEM((mlen, KVW), BF)],
        semantics=("arbitrary",))(proj, proj, proj, mkv, pbd, scale, gq, gk)


def _mixer_pool_bwd(proj, dcat, mkv, pbd, scale, gq, gk, mem_probs, *, name, tm):
    s_len, d = proj.shape
    main = d - KVW
    gd = main // len(POOL_WINDOWS)
    mlen = mkv.shape[0]
    hb = tm // HALO
    nt = s_len // tm
    last_halo = s_len // HALO - 1

    def body(u_ref, halo_ref, mq_ref, do_ref, donext_ref, dom_ref, mkv_ref, pbd_ref, scale_ref, gq_ref, gk_ref, mp_ref,
             dproj_ref, dpbd_ref, dscale_ref, dmkv_ref, dgq_ref, dgk_ref, mk_s, mv_s, dmk_s, dmv_s):
        i = pl.program_id(0)

        @pl.when(i == 0)
        def _():
            mk, mv = _mem_kv(mkv_ref[...], gk_ref[...])
            mk_s[...] = mk
            mv_s[...] = mv
            dmk_s[...] = jnp.zeros_like(dmk_s)
            dmv_s[...] = jnp.zeros_like(dmv_s)
            dpbd_ref[...] = jnp.zeros_like(dpbd_ref)
            dscale_ref[...] = jnp.zeros_like(dscale_ref)
            dgq_ref[...] = jnp.zeros_like(dgq_ref)

        pbd = pbd_ref[...]
        scale = scale_ref[...]
        halo = jnp.where(i > 0, halo_ref[...], 0.0)
        dif = _pool_diff(u_ref[...], halo, i * tm, gd).astype(BF)
        do = do_ref[...]
        dscale_ref[...] += jnp.sum(do * _dot(dif, pbd), axis=0, keepdims=True)
        dmixed = (do * scale).astype(BF)
        dpbd_ref[...] += _dot_tn(dif, dmixed)
        dd = _dot_nt(dmixed, pbd)
        donext = jnp.where(i < nt - 1, donext_ref[...], 0.0)
        dd_halo = _dot_nt((donext * scale).astype(BF), pbd)
        du = _pool_diff_bwd(dd, dd_halo, i * tm, gd)

        dmq, dmk, dmv, dgq = _mem_bwd(mq_ref[...], dom_ref[...], mk_s[...], mv_s[...], gq_ref[...], mp_ref[...])
        dmk_s[...] += dmk
        dmv_s[...] += dmv
        dgq_ref[...] += dgq
        dproj_ref[...] = jnp.concatenate([du, dmq], axis=-1).astype(BF)

        @pl.when(i == nt - 1)
        def _():
            dmkv, dgk = _mem_kv_bwd(mkv_ref[...], dmk_s[...], dmv_s[...], gk_ref[...])
            dmkv_ref[...] = dmkv
            dgk_ref[...] = dgk

    full = lambda shape: pl.BlockSpec(shape, lambda i: (0,) * len(shape))
    return _call(
        body, name, (nt,),
        [pl.BlockSpec((tm, main), lambda i: (i, 0)),
         pl.BlockSpec((HALO, main), lambda i: (jnp.maximum(i * hb - 1, 0), 0)),
         pl.BlockSpec((tm, KVW), lambda i: (i, main // KVW)),
         pl.BlockSpec((tm, main), lambda i: (i, 0)),
         pl.BlockSpec((HALO, main), lambda i: (jnp.minimum((i + 1) * hb, last_halo), 0)),
         pl.BlockSpec((tm, KVW), lambda i: (i, main // KVW)),
         full((mlen, 2 * KVW)), full((main, main)), full((1, main)), full((1, HEAD)), full((1, HEAD)),
         pl.BlockSpec((tm, KV_HEADS * mlen), lambda i: (i, 0))],
        [pl.BlockSpec((tm, d), lambda i: (i, 0)), full((main, main)), full((1, main)), full((mlen, 2 * KVW)),
         full((1, HEAD)), full((1, HEAD))],
        [SDS((s_len, d), BF), SDS((main, main), F32), SDS((1, main), F32), SDS((mlen, 2 * KVW), F32),
         SDS((1, HEAD), F32), SDS((1, HEAD), F32)],
        scratch=[pltpu.VMEM((mlen, KVW), BF), pltpu.VMEM((mlen, KVW), BF), pltpu.VMEM((mlen, KVW), F32),
                 pltpu.VMEM((mlen, KVW), F32)],
        semantics=("arbitrary",))(proj, proj, proj, dcat, dcat, dcat, mkv, pbd, scale, gq, gk, mem_probs)


def _mem_attn_fwd(proj, mkv, gq, gk, cat, *, name, tm):
    s_len, d = proj.shape
    main = d - KVW
    mlen = mkv.shape[0]

    def body(mq_ref, mkv_ref, gq_ref, gk_ref, _, o_ref, mp_ref, mk_s, mv_s):
        @pl.when(pl.program_id(0) == 0)
        def _():
            mk, mv = _mem_kv(mkv_ref[...], gk_ref[...])
            mk_s[...] = mk
            mv_s[...] = mv

        mem, mp_ref[...] = _mem_fwd(mq_ref[...], mk_s[...], mv_s[...], gq_ref[...])
        o_ref[...] = mem.astype(BF)

    full = lambda shape: pl.BlockSpec(shape, lambda i: (0,) * len(shape))
    memcol = lambda i: (i, main // KVW)
    return _call(
        body, name, (s_len // tm,),
        [pl.BlockSpec((tm, KVW), memcol), full((mlen, 2 * KVW)), full((1, HEAD)), full((1, HEAD)), ANY],
        [pl.BlockSpec((tm, KVW), memcol), pl.BlockSpec((tm, KV_HEADS * mlen), lambda i: (i, 0))],
        [SDS((s_len, d), BF), SDS((s_len, KV_HEADS * mlen), BF)],
        scratch=[pltpu.VMEM((mlen, KVW), BF), pltpu.VMEM((mlen, KVW), BF)],
        semantics=("arbitrary",), aliases={4: 0})(proj, mkv, gq, gk, cat)


def _mem_attn_bwd(proj, dcat, mkv, gq, gk, mem_probs, dproj, *, name, tm):
    s_len, d = proj.shape
    main = d - KVW
    mlen = mkv.shape[0]
    nt = s_len // tm

    def body(mq_ref, dom_ref, mkv_ref, gq_ref, gk_ref, mp_ref, _, dproj_ref, dmkv_ref, dgq_ref, dgk_ref,
             mk_s, mv_s, dmk_s, dmv_s):
        i = pl.program_id(0)

        @pl.when(i == 0)
        def _():
            mk, mv = _mem_kv(mkv_ref[...], gk_ref[...])
            mk_s[...] = mk
            mv_s[...] = mv
            dmk_s[...] = jnp.zeros_like(dmk_s)
            dmv_s[...] = jnp.zeros_like(dmv_s)
            dgq_ref[...] = jnp.zeros_like(dgq_ref)

        dmq, dmk, dmv, dgq = _mem_bwd(mq_ref[...], dom_ref[...], mk_s[...], mv_s[...], gq_ref[...], mp_ref[...])
        dmk_s[...] += dmk
        dmv_s[...] += dmv
        dgq_ref[...] += dgq
        dproj_ref[...] = dmq.astype(BF)

        @pl.when(i == nt - 1)
        def _():
            dmkv, dgk = _mem_kv_bwd(mkv_ref[...], dmk_s[...], dmv_s[...], gk_ref[...])
            dmkv_ref[...] = dmkv
            dgk_ref[...] = dgk

    full = lambda shape: pl.BlockSpec(shape, lambda i: (0,) * len(shape))
    memcol = lambda i: (i, main // KVW)
    return _call(
        body, name, (nt,),
        [pl.BlockSpec((tm, KVW), memcol), pl.BlockSpec((tm, KVW), memcol), full((mlen, 2 * KVW)), full((1, HEAD)),
         full((1, HEAD)), pl.BlockSpec((tm, KV_HEADS * mlen), lambda i: (i, 0)), ANY],
        [pl.BlockSpec((tm, KVW), memcol), full((mlen, 2 * KVW)), full((1, HEAD)), full((1, HEAD))],
        [SDS((s_len, d), BF), SDS((mlen, 2 * KVW), F32), SDS((1, HEAD), F32), SDS((1, HEAD), F32)],
        scratch=[pltpu.VMEM((mlen, KVW), BF), pltpu.VMEM((mlen, KVW), BF), pltpu.VMEM((mlen, KVW), F32),
                 pltpu.VMEM((mlen, KVW), F32)],
        semantics=("arbitrary",), aliases={6: 0})(proj, dcat, mkv, gq, gk, mem_probs, dproj)


def _mixer_swa_fwd(proj, kn, v, gqs, sinks, *, name):
    s_len, d = proj.shape
    main = d - KVW
    qh = main // HEAD
    tm = WINDOW
    prow = qh // KV_HEADS * tm

    def body(q_ref, kp_ref, kc_ref, vp_ref, vc_ref, gqs_ref, sinks_ref, o_ref, p_ref, ps_ref):
        n = pl.program_id(0)
        kk = jnp.concatenate([kp_ref[...], kc_ref[...]], axis=0)
        vv = jnp.concatenate([vp_ref[...], vc_ref[...]], axis=0)
        att, probs, sink_probs = _swa_fwd(q_ref[...], kk, vv, gqs_ref[...], sinks_ref[...], n, qh)
        for kh in range(KV_HEADS):
            p_ref[0, kh] = probs[kh]
        ps_ref[...] = sink_probs
        o_ref[...] = att.astype(BF)

    full = lambda shape: pl.BlockSpec(shape, lambda i: (0,) * len(shape))
    prev = lambda i: (jnp.maximum(i - 1, 0), 0)
    cur = lambda i: (i, 0)
    return _call(
        body, name, (s_len // tm,),
        [pl.BlockSpec((tm, main), cur), pl.BlockSpec((tm, KVW), prev), pl.BlockSpec((tm, KVW), cur),
         pl.BlockSpec((tm, KVW), prev), pl.BlockSpec((tm, KVW), cur), full((1, HEAD)), full((1, LANE))],
        [pl.BlockSpec((tm, main), cur), pl.BlockSpec((1, KV_HEADS, prow, 2 * tm), lambda i: (i, 0, 0, 0)),
         pl.BlockSpec((tm, LANE), cur)],
        [SDS((s_len, d), BF), SDS((s_len // tm, KV_HEADS, prow, 2 * tm), BF), SDS((s_len, LANE), F32)],
        semantics=("parallel",))(proj, kn, kn, v, v, gqs, sinks)


def _mixer_swa_bwd(proj, dcat, kn, v, gqs, probs, sink_probs, *, name):
    s_len, d = proj.shape
    main = d - KVW
    qh = main // HEAD
    tm = WINDOW
    nt = s_len // tm
    prow = qh // KV_HEADS * tm

    def body(q_ref, do_ref, kp_ref, kc_ref, vp_ref, vc_ref, gqs_ref, p_ref, ps_ref,
             dproj_ref, dk_ref, dv_ref, dgqs_ref, dsinks_ref):
        n = pl.program_id(0)

        @pl.when(n == 0)
        def _():
            dk_ref[...] = jnp.zeros_like(dk_ref)
            dv_ref[...] = jnp.zeros_like(dv_ref)
            dgqs_ref[...] = jnp.zeros_like(dgqs_ref)
            dsinks_ref[...] = jnp.zeros_like(dsinks_ref)

        kk = jnp.concatenate([kp_ref[...], kc_ref[...]], axis=0)
        vv = jnp.concatenate([vp_ref[...], vc_ref[...]], axis=0)
        dq, dkk, dvv, dgqs, dsk = _swa_bwd(q_ref[...], do_ref[...], kk, vv, gqs_ref[...],
                                           [p_ref[0, kh] for kh in range(KV_HEADS)], ps_ref[...], qh)
        prev = pl.ds(pl.multiple_of(jnp.maximum(n - 1, 0) * tm, tm), tm)
        own = pl.ds(pl.multiple_of(n * tm, tm), tm)
        dk_ref[prev, :] += dkk[:tm]
        dk_ref[own, :] += dkk[tm:]
        dv_ref[prev, :] += dvv[:tm]
        dv_ref[own, :] += dvv[tm:]
        dgqs_ref[...] += dgqs
        dsinks_ref[...] += dsk
        dproj_ref[...] = dq.astype(BF)

    full = lambda shape: pl.BlockSpec(shape, lambda i: (0,) * len(shape))
    prev_b = lambda i: (jnp.maximum(i - 1, 0), 0)
    cur = lambda i: (i, 0)
    return _call(
        body, name, (nt,),
        [pl.BlockSpec((tm, main), cur), pl.BlockSpec((tm, main), cur),
         pl.BlockSpec((tm, KVW), prev_b), pl.BlockSpec((tm, KVW), cur),
         pl.BlockSpec((tm, KVW), prev_b), pl.BlockSpec((tm, KVW), cur),
         full((1, HEAD)), pl.BlockSpec((1, KV_HEADS, prow, 2 * tm), lambda i: (i, 0, 0, 0)),
         pl.BlockSpec((tm, LANE), cur)],
        [pl.BlockSpec((tm, main), cur), full((s_len, KVW)), full((s_len, KVW)), full((1, HEAD)), full((1, LANE))],
        [SDS((s_len, d), BF), SDS((s_len, KVW), F32), SDS((s_len, KVW), F32), SDS((1, HEAD), F32),
         SDS((1, LANE), F32)],
        semantics=("arbitrary",))(proj, dcat, kn, kn, v, v, gqs, probs, sink_probs)


def _kv_prep(kv, gk, *, tm):
    s_len = kv.shape[0]

    def body(kv_ref, gk_ref, k_ref, v_ref):
        k, v = _mem_kv(kv_ref[...], gk_ref[...])
        k_ref[...] = k
        v_ref[...] = v

    row = lambda i: (i, 0)
    return _call(
        body, "kv_prep", (s_len // tm,),
        [pl.BlockSpec((tm, 2 * KVW), row), pl.BlockSpec((1, HEAD), lambda i: (0, 0))],
        [pl.BlockSpec((tm, KVW), row), pl.BlockSpec((tm, KVW), row)],
        [SDS((s_len, KVW), BF), SDS((s_len, KVW), BF)], semantics=("parallel",))(kv, gk)


def _kv_bwd(kv, dks, dvs, gk, *, tm):
    s_len = kv.shape[0]
    nl = len(dks)

    def body(*refs):
        kv_ref, gk_ref = refs[0], refs[1]
        dk_refs, dv_refs = refs[2:2 + nl], refs[2 + nl:2 + 2 * nl]
        dkv_ref, dgk_ref = refs[2 + 2 * nl], refs[3 + 2 * nl]
        dk, dv = dk_refs[0][...], dv_refs[0][...]
        for t in range(1, nl):
            dk = dk + dk_refs[t][...]
            dv = dv + dv_refs[t][...]
        dkv, dgk = _mem_kv_bwd(kv_ref[...], dk, dv, gk_ref[...])
        dkv_ref[...] = dkv.astype(BF)

        @pl.when(pl.program_id(0) == 0)
        def _():
            dgk_ref[...] = jnp.zeros_like(dgk_ref)

        dgk_ref[...] += dgk

    row = lambda i: (i, 0)
    one = pl.BlockSpec((1, HEAD), lambda i: (0, 0))
    return _call(
        body, "kv_bwd", (s_len // tm,),
        [pl.BlockSpec((tm, 2 * KVW), row), one] + [pl.BlockSpec((tm, KVW), row)] * (2 * nl),
        [pl.BlockSpec((tm, 2 * KVW), row), one],
        [SDS((s_len, 2 * KVW), BF), SDS((1, HEAD), F32)], semantics=("arbitrary",))(kv, gk, *dks, *dvs)


def _place():
    x, y, c = lax.axis_index("x"), lax.axis_index("y"), lax.axis_index("c")
    flips = [(1 - x, y), (x, 1 - y), (1 - x, 1 - y)]
    return x, y, c, flips


def _remote(src, dst, send_sem, recv_sem, to):
    return pltpu.make_async_remote_copy(src_ref=src, dst_ref=dst, send_sem=send_sem, recv_sem=recv_sem,
                                        device_id=to, device_id_type=MESH)


def _gather_copies(p_refs, wg_refs, send, recv):
    x, y, c, flips = _place()
    chip = 2 * x + y
    cps = []
    for j, (fx, fy) in enumerate(flips):
        for b in range(2):
            half = p_refs[b].shape[0] // 2
            mine = pl.ds(c * half, half)
            cps.append(_remote(p_refs[b].at[mine, :], wg_refs[b].at[chip, mine, :], send.at[2 * j + b],
                               recv.at[2 * j + b], (fx, fy, c)))
    return cps, cps


def _forward_copies(p_refs, wg_refs, send, recv):
    x, y, c, flips = _place()
    chip = 2 * x + y
    sib = (x, y, 1 - c)
    sends, arrivals = [], []
    for b in range(2):
        half = p_refs[b].shape[0] // 2
        own = _remote(p_refs[b], wg_refs[b].at[chip], send.at[b], recv.at[b], sib)
        sends.append(own)
        arrivals.append(own)
        for j, (fx, fy) in enumerate(flips):
            k = 2 + 3 * b + j
            landed = wg_refs[b].at[2 * fx + fy, pl.ds(c * half, half), :]
            other = wg_refs[b].at[2 * fx + fy, pl.ds((1 - c) * half, half), :]
            sends.append(_remote(landed, landed, send.at[k], recv.at[k], sib))
            arrivals.append(_remote(other, other, send.at[k], recv.at[k], sib))
    return sends, arrivals


def _swap_copies(g_refs, r_refs, send, recv):
    x, y, c, _ = _place()
    cps = []
    for b in range(2):
        half = g_refs[b].shape[1] // 2
        cps.append(_remote(g_refs[b].at[:, pl.ds((1 - c) * half, half), :], r_refs[b], send.at[b], recv.at[b],
                           (x, y, 1 - c)))
    return cps, cps


def _split_start(make_copies, n_sems, bufs, after, fresh, *, name):
    def body(a1, a2, b1, b2, after_ref, send, recv, *outs):
        for cp in make_copies((a1, a2), (b1, b2), send, recv)[0]:
            cp.start()
        outs[4][...] = jnp.zeros_like(outs[4])

    extra_shape = () if fresh is None else (pltpu.HBM(fresh, F32),)
    extra_spec = () if fresh is None else (HBM,)
    return pl.pallas_call(
        body, name=name,
        out_shape=(pltpu.SemaphoreType.DMA((n_sems,)), pltpu.SemaphoreType.DMA((n_sems,)))
        + tuple(pltpu.HBM(b.shape, b.dtype) for b in bufs) + (SDS((8, LANE), F32),) + extra_shape,
        in_specs=(HBM, HBM, HBM, HBM, ANY), out_specs=(SEM, SEM, HBM, HBM, HBM, HBM, VMEM_WHOLE) + extra_spec,
        input_output_aliases={0: 2, 1: 3, 2: 4, 3: 5},
        compiler_params=pltpu.CompilerParams(has_side_effects=SIDE_EFFECT))(*[_in_hbm(b) for b in bufs], after)


def _split_wait(make_copies, started, after, *, name):
    send, recv, bufs = started[0], started[1], started[2:6]

    def body(a1, a2, b1, b2, send_ref, recv_ref, after_ref, *outs):
        sends, arrivals = make_copies((a1, a2), (b1, b2), send_ref, recv_ref)
        for cp in arrivals:
            cp.wait_recv()
        for cp in sends:
            cp.wait_send()

    return pl.pallas_call(
        body, name=name, out_shape=tuple(pltpu.HBM(b.shape, b.dtype) for b in bufs),
        in_specs=(HBM, HBM, HBM, HBM, SEM, SEM, ANY), out_specs=(HBM, HBM, HBM, HBM),
        input_output_aliases={0: 0, 1: 1, 2: 2, 3: 3},
        compiler_params=pltpu.CompilerParams(has_side_effects=SIDE_EFFECT))(*bufs, send, recv, after)


def _gather_small(ps):
    def body(ps_ref, o_ref, send, recv):
        x, y, c, flips = _place()
        chip = 2 * x + y
        o_ref[chip] = ps_ref[...]
        cps = [_remote(ps_ref, o_ref.at[chip], send.at[j], recv.at[j], (fx, fy, c))
               for j, (fx, fy) in enumerate(flips)]
        for cp in cps:
            cp.start()
        for j, (fx, fy) in enumerate(flips):
            _remote(ps_ref, o_ref.at[2 * fx + fy], send.at[j], recv.at[j], (fx, fy, c)).wait_recv()
        for cp in cps:
            cp.wait_send()

    return pl.pallas_call(
        body, name="gather_small", in_specs=[VMEM_WHOLE], out_specs=VMEM_WHOLE,
        out_shape=SDS((N_CHIPS,) + ps.shape, ps.dtype),
        scratch_shapes=[pltpu.SemaphoreType.DMA((3,)), pltpu.SemaphoreType.DMA((3,))])(ps)


def _sum_sibling(g, r, place, *, tm, name):
    n_sh, half, w = r.shape
    nt = half // tm

    def body(place_ref, g_ref, r_ref, pbf_ref, own_ref):
        s = pl.program_id(1)
        p = g_ref[0] + r_ref[0]
        pbf_ref[0] = p.astype(BF)

        @pl.when(s == place_ref[1])
        def _():
            own_ref[...] = p

    return _call(
        body, name, (nt, n_sh),
        [pl.BlockSpec((1, tm, w), lambda i, s, pr: (s, pr[0] * nt + i, 0)),
         pl.BlockSpec((1, tm, w), lambda i, s, pr: (s, i, 0))],
        [pl.BlockSpec((1, tm, w), lambda i, s, pr: (s, i, 0)), pl.BlockSpec((tm, w), lambda i, s, pr: (i, 0))],
        [SDS((n_sh, half, w), BF), SDS((half, w), F32)],
        semantics=("arbitrary", "arbitrary"), prefetch=1)(place, g, r)


def _rs_copies(p_refs, land_refs, send, recv):
    _, _, c, flips = _place()
    cps = []
    for j, (fx, fy) in enumerate(flips):
        for b in range(2):
            cps.append(_remote(p_refs[b].at[2 * fx + fy], land_refs[b].at[j], send.at[2 * j + b], recv.at[2 * j + b],
                               (fx, fy, c)))
    return cps, cps


def _sum_chips(own, r, full, layer, place, *, tm, name):
    half, w = own.shape

    def body(place_ref, own_ref, r_ref, _, o_ref):
        o_ref[0, 0] = ((own_ref[...] + r_ref[0].astype(F32)) + r_ref[1].astype(F32)) + r_ref[2].astype(F32)

    return _call(
        body, name, (half // tm,),
        [pl.BlockSpec((tm, w), lambda i, pr: (i, 0)), pl.BlockSpec((3, tm, w), lambda i, pr: (0, i, 0)), ANY],
        pl.BlockSpec((1, 1, tm, w), lambda i, pr: (layer, pr[0], i, 0)), SDS(full.shape, F32),
        semantics=("parallel",), prefetch=1, aliases={3: 0})(place, own, r, full)


def _share_with_sibling(f1, f2, lo, hi, *, name):
    def body(_, __, o1_ref, o2_ref, send, recv):
        x, y, c, _ = _place()
        lay, mine, other = pl.ds(lo, hi - lo), pl.ds(c, 1), pl.ds(1 - c, 1)
        cps = [_remote(o1_ref.at[lay, mine], o1_ref.at[lay, mine], send.at[0], recv.at[0], (x, y, 1 - c)),
               _remote(o2_ref.at[lay, mine], o2_ref.at[lay, mine], send.at[1], recv.at[1], (x, y, 1 - c))]
        for cp in cps:
            cp.start()
        for cp in cps:
            cp.wait_send()
        _remote(o1_ref.at[lay, other], o1_ref.at[lay, other], send.at[0], recv.at[0], (x, y, 1 - c)).wait_recv()
        _remote(o2_ref.at[lay, other], o2_ref.at[lay, other], send.at[1], recv.at[1], (x, y, 1 - c)).wait_recv()

    return pl.pallas_call(
        body, name=name, in_specs=[ANY, ANY], out_specs=[ANY, ANY],
        out_shape=[SDS(f1.shape, f1.dtype), SDS(f2.shape, f2.dtype)], input_output_aliases={0: 0, 1: 1},
        scratch_shapes=[pltpu.SemaphoreType.DMA((2,)), pltpu.SemaphoreType.DMA((2,))])(f1, f2)


def _allreduce_small(sg):
    rows, w = sg.shape
    half = rows // 2
    assert half % 8 == 0

    def body(sg_ref, o_ref, sib_buf, part, slots, send, recv):
        x, y, c, flips = _place()
        chip = 2 * x + y
        sib = (x, y, 1 - c)
        mine = pl.ds(pl.multiple_of(c * half, 8), half)
        other = pl.ds(pl.multiple_of((1 - c) * half, 8), half)
        to_sib = _remote(sg_ref.at[other, :], sib_buf, send.at[0], recv.at[0], sib)
        to_sib.start()
        to_sib.wait_recv()
        part[...] = sg_ref[mine, :] + sib_buf[...]
        slots[chip] = part[...]
        to_chips = [_remote(part, slots.at[chip], send.at[1 + j], recv.at[1 + j], (fx, fy, c))
                    for j, (fx, fy) in enumerate(flips)]
        for cp in to_chips:
            cp.start()
        for j, (fx, fy) in enumerate(flips):
            _remote(part, slots.at[2 * fx + fy], send.at[1 + j], recv.at[1 + j], (fx, fy, c)).wait_recv()
        o_ref[mine, :] = ((slots[0] + slots[1]) + slots[2]) + slots[3]
        back = _remote(o_ref.at[mine, :], o_ref.at[mine, :], send.at[4], recv.at[4], sib)
        back.start()
        _remote(o_ref.at[other, :], o_ref.at[other, :], send.at[4], recv.at[4], sib).wait_recv()
        for cp in [to_sib, back] + to_chips:
            cp.wait_send()

    return pl.pallas_call(
        body, name="allreduce_small", in_specs=[VMEM_WHOLE], out_specs=VMEM_WHOLE, out_shape=SDS((rows, w), F32),
        scratch_shapes=[pltpu.VMEM((half, w), F32), pltpu.VMEM((half, w), F32), pltpu.VMEM((N_CHIPS, half, w), F32),
                        pltpu.SemaphoreType.DMA((5,)), pltpu.SemaphoreType.DMA((5,))])(sg)


def _adamw(g_arr, layer0, g_off, per_layer, w, m, v, *, name, tm, layers=None, prev=None):
    rows, cols = w.shape
    assert g_off % tm == 0 and per_layer % tm == 0 and rows % per_layer == 0
    npl = per_layer // tm
    lo, hi = layers or (0, rows // per_layer)
    base = lo * npl
    c1 = 1.0 - ADAM_B1 ** ADAM_STEP
    c2 = 1.0 - ADAM_B2 ** ADAM_STEP

    def body(g_ref, w_ref, m_ref, v_ref, *rest):
        go_ref, d_ref, mo_ref, vo_ref = rest[-4:]
        g = g_ref[0]
        mn = ADAM_B1 * m_ref[...] + (1.0 - ADAM_B1) * g
        vn = ADAM_B2 * v_ref[...] + (1.0 - ADAM_B2) * (g * g)
        go_ref[...] = g
        mo_ref[...] = mn
        vo_ref[...] = vn
        d_ref[...] = -ADAM_LR * ((mn / c1) / (jnp.sqrt(vn / c2) + ADAM_EPS) + ADAM_WD * w_ref[...])

    blk = pl.BlockSpec((tm, cols), lambda i: (base + i, 0))
    extra = list(prev or [])
    return _call(
        body, name, ((hi - lo) * npl,),
        [pl.BlockSpec((1, tm, cols), lambda i: (layer0 + lo + i // npl, g_off // tm + i % npl, 0)), blk, blk, blk]
        + [ANY] * len(extra),
        [blk] * 4,
        [SDS((rows, cols), F32)] * 4, semantics=("parallel",),
        aliases={4 + k: k for k in range(len(extra))})(g_arr, w, m, v, *extra)


def _pack_small(parts, width):
    flat = jnp.concatenate([p.reshape(-1).astype(F32) for p in parts])
    rows = -(-flat.shape[0] // (16 * width)) * 16
    return jnp.pad(flat, (0, rows * width - flat.shape[0])).reshape(rows, width)


def _unpack_small(packed, shapes):
    flat = packed.reshape(-1)
    out, off = [], 0
    for shp in shapes:
        size = 1
        for n in shp:
            size *= n
        out.append(flat[off:off + size].reshape(shp))
        off += size
    return out


def _block_diag(pw):
    g, c, _ = pw.shape
    eye = jnp.eye(g, dtype=pw.dtype)
    return (eye[:, None, :, None] * pw[:, :, None, :]).reshape(g * c, g * c)


def _diag_blocks(full, g):
    c = full.shape[0] // g
    return jnp.stack([full[i * c:(i + 1) * c, i * c:(i + 1) * c] for i in range(g)])


def kernel(x, mem, norm_mix, w_in, pool_w, pool_scale, kv_norm, w_kv, k_norm, q_norm, sinks, mem_norm, w_mem_kv, mem_q_norm, mem_k_norm, w_out, norm_mlp, w_up, w_down, loss_target, m_norm_mix, m_w_in, m_pool_w, m_pool_scale, m_kv_norm, m_w_kv, m_k_norm, m_q_norm, m_sinks, m_mem_norm, m_w_mem_kv, m_mem_q_norm, m_mem_k_norm, m_w_out, m_norm_mlp, m_w_up, m_w_down, v_norm_mix, v_w_in, v_pool_w, v_pool_scale, v_kv_norm, v_w_kv, v_k_norm, v_q_norm, v_sinks, v_mem_norm, v_w_mem_kv, v_mem_q_norm, v_mem_k_norm, v_w_out, v_norm_mlp, v_w_up, v_w_down):
    s_len, d = x.shape[1], x.shape[2]
    n_layers, n_pool = norm_mix.shape[0], pool_w.shape[0]
    n_swa = n_layers - n_pool
    main = d - KVW
    qh = main // HEAD
    ff = w_down.shape[1] * N_CHIPS
    dq = d // N_CHIPS
    assert w_up.shape[2] == d and ff == N_CHIPS * d and w_kv.shape[1] == 2 * KVW
    tm = min(512, s_len)
    tmb = min(1024, s_len)
    tm_mem = mem.shape[1]

    cx, cy, cc = lax.axis_index("x"), lax.axis_index("y"), lax.axis_index("c")
    chip = 2 * cx + cy
    place = jnp.stack([cc, chip]).astype(jnp.int32)

    off_down, off_up, off_in, off_out = 0, d, 2 * d, 2 * d + dq
    rows1 = off_out + dq
    off_mkv, off_kv = 0, dq
    rows2 = 2 * dq

    ps = jnp.pad(pool_scale, ((0, 8 - n_pool), (0, 2 * LANE - pool_scale.shape[1])))
    psg = _gather_small(ps)
    pool_scale_full = jnp.concatenate([psg[k, :n_pool, :pool_scale.shape[1]] for k in range(N_CHIPS)], axis=1)

    def packed_weights(l):
        p1 = jnp.concatenate([w_down[l], w_up[l], w_in[l], w_out[l]]).astype(BF)
        p2 = jnp.concatenate([w_mem_kv[l], w_kv] if l == n_pool else [w_mem_kv[l]]).astype(BF)
        return p1, p2

    def gather_start(l, after):
        p1, p2 = packed_weights(l)
        bufs = (p1, p2, lax.empty((N_CHIPS,) + p1.shape, BF), lax.empty((N_CHIPS,) + p2.shape, BF))
        return _split_start(_gather_copies, 6, bufs, after, None, name=f"gather_start_{l}")

    def gather_land(l, started, after):
        bufs = _split_wait(_gather_copies, started, after, name=f"gather_wait_{l}")
        return _split_start(_forward_copies, 8, bufs, place, None, name=f"forward_start_{l}")

    def gather_finish(l, forwarding, after):
        bufs = _split_wait(_forward_copies, forwarding, after, name=f"forward_wait_{l}")
        return bufs[2], bufs[3]

    def w_rows(arr, off, nrows, width):
        assert off % nrows == 0
        return (arr, (N_CHIPS, nrows, width), lambda j: (0, off // nrows, 0))

    row = lambda a: a.reshape(1, -1)
    h = x.reshape(s_len, d)
    memx = mem.reshape(tm_mem, d)
    tgt = loss_target.reshape(s_len, d)
    pbd = [_block_diag(pool_w[l]).astype(BF) for l in range(n_pool)]
    sinks_pad = [jnp.pad(row(sinks[j]), ((0, 0), (0, LANE - qh))) for j in range(n_swa)]

    w_in_l, w_out_l, w_down_l, w_up_all_l, w_mkv_l = [], [], [], [], []
    w_kv_g = None
    forwarding = gather_land(0, gather_start(0, psg), psg)
    travelling = gather_start(1, forwarding[6]) if n_layers > 1 else None
    saved, probs, sink_probs, mem_probs = [], {}, {}, {}
    kv = hn_kv = kn = vsh = None
    for l in range(n_layers):
        wg1, wg2 = gather_finish(l, forwarding, h if l else forwarding[6])
        w_in_l.append(w_rows(wg1, off_in, dq, d))
        w_out_l.append(w_rows(wg1, off_out, dq, d))
        w_down_l.append(w_rows(wg1, off_down, d, d))
        w_up_all_l.append((wg1, (N_CHIPS, d, d), lambda j: (0, off_up // d, 0)))
        w_mkv_l.append(w_rows(wg2, off_mkv, dq, 2 * KVW))
        g_mix = row(norm_mix[l])
        if travelling is not None:
            g_mix = g_mix + travelling[6][0, 0]
        if l == n_pool:
            w_kv_g = w_rows(wg2, off_kv, dq, 2 * KVW)
            kv, hn_kv = _norm_mm(h, row(kv_norm), w_kv_g, 1, 2 * KVW, act=False, name="kv_proj", tm=tmb)
            kn, vsh = _kv_prep(kv, row(k_norm), tm=tmb)
        h0 = h
        proj, xn = _norm_mm(h0, g_mix, w_in_l[l], 1, d, act=False, name=f"in_proj_{l}", tm=tmb)
        mkv, memn = _norm_mm(memx, row(mem_norm[l]), w_mkv_l[l], 1, 2 * KVW, act=False, name=f"mem_kv_{l}", tm=tm_mem)
        if l < n_pool:
            cat, mem_probs[l] = _mixer_pool_fwd(proj, mkv, pbd[l], row(pool_scale_full[l]), row(mem_q_norm[l]),
                                                row(mem_k_norm[l]), name=f"mixer_fwd_{l}", tm=tm)
        else:
            j = l - n_pool
            cat, probs[l], sink_probs[l] = _mixer_swa_fwd(proj, kn, vsh, row(q_norm[j]), sinks_pad[j],
                                                          name=f"mixer_fwd_{l}")
            cat, mem_probs[l] = _mem_attn_fwd(proj, mkv, row(mem_q_norm[l]), row(mem_k_norm[l]), cat,
                                              name=f"mem_attn_fwd_{l}", tm=tm)
        h1 = _mm_res(h0, cat, w_out_l[l], name=f"out_proj_{l}", tm=tmb)
        hh, xm = _norm_mm(h1, row(norm_mlp[l]), w_up_all_l[l], N_CHIPS, d, act=True, name=f"mlp_up_{l}", tm=tm)
        after = None
        if travelling is not None:
            forwarding = gather_land(l + 1, travelling, hh)
            travelling = gather_start(l + 2, forwarding[6]) if l + 2 < n_layers else None
            after = forwarding[6]
        h = _mm_res(h1, hh, w_down_l[l], name=f"mlp_down_{l}", tm=tm, after=after)
        saved.append((h0, proj, xn, mkv, memn, cat, h1, hh, xm))

    dh, dh_bf, loss_part = _loss_head(h, tgt, tm=tmb)
    loss = lax.psum(loss_part[0, 0], ("x", "y", "c"))

    half1, half2 = rows1 // 2, rows2 // 2
    g1 = lax.empty((N_CHIPS, rows1, d), F32)
    pending = {}
    swapping = None
    tk = min(512, d)

    def reduce_begin(l, swapped, after):
        g1_l, g2_l, r1, r2 = _split_wait(_swap_copies, swapped, after, name=f"swap_wait_{l}")
        pb1, own1 = _sum_sibling(g1_l, r1, place, tm=_tile(half1, 640), name=f"sum_sibling_a_{l}")
        pb2, own2 = _sum_sibling(g2_l, r2, place, tm=_tile(half2, 256), name=f"sum_sibling_b_{l}")
        bufs = (pb1, pb2, lax.empty((3, half1, d), BF), lax.empty((3, half2, 2 * KVW), BF))
        return _split_start(_rs_copies, 6, bufs, place, None, name=f"reduce_start_{l}"), own1, own2
    zeros_mem = jnp.zeros((tm_mem, d), F32)

    def rows_map(off, nrows, tkk):
        per = nrows // tkk
        return lambda i, j: (i // per, off // tkk + i % per, 0)

    def cols_map(off, tkk):
        return lambda i, j: (j, off // tkk + i, 0)

    d_norm_mix, d_norm_mlp, d_mem_norm = [None] * n_layers, [None] * n_layers, [None] * n_layers
    d_mem_q, d_mem_k = [None] * n_layers, [None] * n_layers
    d_pool_w, d_pool_scale = [None] * n_pool, [None] * n_pool
    d_q_norm, d_sinks = [None] * n_swa, [None] * n_swa
    dks, dvs = [], []
    d_kv_norm = d_k_norm = None
    for l in reversed(range(n_layers)):
        h0, proj, xn, mkv, memn, cat, h1, hh, xm = saved[l]
        g2 = jnp.zeros((N_CHIPS, rows2, 2 * KVW), F32)
        g1 = _mm_tn(hh, dh_bf, g1, rows_map(off_down, d, tk), tk, d, name=f"dw_down_{l}")
        du = _mm_nt_relu2(dh_bf, hh, w_down_l[l], N_CHIPS, name=f"d_mlp_act_{l}", tm=tm)
        g1 = _mm_tn(xm, du, g1, cols_map(off_up, tk), tk, d, name=f"dw_up_{l}")
        g_mlp = row(norm_mlp[l])
        if swapping is not None:
            pending[swapping[0]] = reduce_begin(*swapping, after=g1)
            g_mlp = g_mlp + pending[swapping[0]][0][6][0, 0]
        dh1, dh1_bf, d_norm_mlp[l] = _mm_nt_normbwd(du, w_up_all_l[l], N_CHIPS, h1, g_mlp, dh,
                                                    name=f"d_mlp_in_{l}", tm=tm)
        tkq = min(tk, dq)
        g1 = _mm_tn(cat, dh1_bf, g1, rows_map(off_out, dq, tkq), tkq, d, name=f"dw_out_{l}")
        dcat = _mm_nt(dh1_bf, w_out_l[l], d, name=f"d_cat_{l}", tm=tmb)
        if l < n_pool:
            dproj, dpbd, dscale, dmkv, d_mem_q[l], d_mem_k[l] = _mixer_pool_bwd(
                proj, dcat, mkv, pbd[l], row(pool_scale_full[l]), row(mem_q_norm[l]), row(mem_k_norm[l]),
                mem_probs[l], name=f"mixer_bwd_{l}", tm=tm)
            d_pool_w[l] = _diag_blocks(dpbd, len(POOL_WINDOWS))
            d_pool_scale[l] = dscale
        else:
            j = l - n_pool
            dproj, dk, dv, d_q_norm[j], dsk = _mixer_swa_bwd(proj, dcat, kn, vsh, row(q_norm[j]), probs[l],
                                                             sink_probs[l], name=f"mixer_bwd_{l}")
            dproj, dmkv, d_mem_q[l], d_mem_k[l] = _mem_attn_bwd(
                proj, dcat, mkv, row(mem_q_norm[l]), row(mem_k_norm[l]), mem_probs[l], dproj,
                name=f"mem_attn_bwd_{l}", tm=tm)
            d_sinks[j] = dsk[0, :qh]
            dks.append(dk)
            dvs.append(dv)
        g1 = _mm_tn(xn, dproj, g1, rows_map(off_in, dq, tkq), tkq, d, name=f"dw_in_{l}")
        dh, dh_bf, d_norm_mix[l] = _mm_nt_normbwd(dproj, w_in_l[l], 1, h0, row(norm_mix[l]), dh1,
                                                  name=f"d_in_{l}", tm=tmb)
        g2 = _mm_tn(memn, dmkv, g2, rows_map(off_mkv, dq, tkq), tkq, 2 * KVW, name=f"dw_mem_kv_{l}")
        _, _, d_mem_norm[l] = _mm_nt_normbwd(dmkv, w_mkv_l[l], 1, memx, row(mem_norm[l]), zeros_mem,
                                             name=f"d_mem_norm_{l}", tm=tm_mem)
        if l == n_pool:
            dkv, d_k_norm = _kv_bwd(kv, dks, dvs, row(k_norm), tm=tmb)
            g2 = _mm_tn(hn_kv, dkv, g2, rows_map(off_kv, dq, tkq), tkq, 2 * KVW, name="dw_kv")
            dh, dh_bf, d_kv_norm = _mm_nt_normbwd(dkv, w_kv_g, 1, h0, row(kv_norm), dh, name="d_kv_in", tm=tmb)
        bufs = (g1, g2, lax.empty((N_CHIPS, half1, d), F32), lax.empty((N_CHIPS, half2, 2 * KVW), F32))
        swapping = (l, _split_start(_swap_copies, 2, bufs, place, (N_CHIPS, rows1, d) if l > 0 else None,
                                    name=f"swap_start_{l}"))
        g1 = swapping[1][7] if l > 0 else None
    grad_x = dh.reshape(x.shape)

    small_names = ["norm_mix", "pool_w", "pool_scale", "kv_norm", "k_norm", "q_norm", "sinks", "mem_norm",
                   "mem_q_norm", "mem_k_norm", "norm_mlp"]
    small_grads = {
        "norm_mix": jnp.concatenate(d_norm_mix), "pool_w": jnp.stack(d_pool_w),
        "pool_scale": jnp.concatenate(d_pool_scale), "kv_norm": d_kv_norm[0], "k_norm": d_k_norm[0],
        "q_norm": jnp.concatenate(d_q_norm), "sinks": jnp.stack(d_sinks), "mem_norm": jnp.concatenate(d_mem_norm),
        "mem_q_norm": jnp.concatenate(d_mem_q), "mem_k_norm": jnp.concatenate(d_mem_k),
        "norm_mlp": jnp.concatenate(d_norm_mlp)}
    width = d
    sg = _pack_small([small_grads[n] for n in small_names], width)
    sg = sg + swapping[1][6][0, 0]
    sg = _allreduce_small(sg)
    pending[0] = reduce_begin(*swapping, after=sg)
    reduced = dict(zip(small_names, _unpack_small(sg, [small_grads[n].shape for n in small_names])))
    psw = pool_scale.shape[1]
    reduced["pool_scale"] = lax.dynamic_slice_in_dim(reduced["pool_scale"], chip * psw, psw, axis=1)
    params = dict(norm_mix=(norm_mix, m_norm_mix, v_norm_mix), pool_w=(pool_w, m_pool_w, v_pool_w),
                  pool_scale=(pool_scale, m_pool_scale, v_pool_scale), kv_norm=(kv_norm, m_kv_norm, v_kv_norm),
                  k_norm=(k_norm, m_k_norm, v_k_norm), q_norm=(q_norm, m_q_norm, v_q_norm),
                  sinks=(sinks, m_sinks, v_sinks), mem_norm=(mem_norm, m_mem_norm, v_mem_norm),
                  mem_q_norm=(mem_q_norm, m_mem_q_norm, v_mem_q_norm),
                  mem_k_norm=(mem_k_norm, m_mem_k_norm, v_mem_k_norm), norm_mlp=(norm_mlp, m_norm_mlp, v_norm_mlp))
    shapes = [params[n][0].shape for n in small_names]
    packs = [_pack_small([reduced[n].reshape(params[n][0].shape) for n in small_names], width)
             + pending[0][0][6][0, 0]]
    packs += [_pack_small([params[n][t] for n in small_names], width) for t in range(3)]
    res = _adamw(packs[0][None], 0, 0, packs[0].shape[0], packs[1], packs[2], packs[3], name="adamw_small", tm=8)
    small = {n: [] for n in small_names}
    for r in res:
        for n, a in zip(small_names, _unpack_small(r, shapes)):
            small[n].append(a)

    full1 = lax.empty((n_layers, 2, half1, d), F32)
    full2 = lax.empty((n_layers, 2, half2, 2 * KVW), F32)
    kinds = (("w_down", 1, off_down, d, w_down, m_w_down, v_w_down), ("w_up", 1, off_up, d, w_up, m_w_up, v_w_up),
             ("w_in", 1, off_in, dq, w_in, m_w_in, v_w_in), ("w_out", 1, off_out, dq, w_out, m_w_out, v_w_out),
             ("w_mem_kv", 2, off_mkv, dq, w_mem_kv, m_w_mem_kv, v_w_mem_kv))
    big = {}
    first = 1 if n_layers > 1 and n_pool > 0 else 0
    after = res[0]
    for lo, hi in ((first, n_layers), (0, first)):
        if lo == hi:
            continue
        for l in reversed(range(lo, hi)):
            exchange, own1, own2 = pending[l]
            _, _, x1, x2 = _split_wait(_rs_copies, exchange, after, name=f"reduce_wait_{l}")
            full1 = _sum_chips(own1, x1, full1, l, place, tm=_tile(half1, 640), name=f"sum_chips_a_{l}")
            full2 = _sum_chips(own2, x2, full2, l, place, tm=_tile(half2, 256), name=f"sum_chips_b_{l}")
        full1, full2 = _share_with_sibling(full1, full2, lo, hi, name=f"share_with_sibling_{lo}")
        views = (None, full1.reshape(n_layers, rows1, d), full2.reshape(n_layers, rows2, 2 * KVW))
        for name, which, off, per, w_, m_, v_ in kinds:
            cols = views[which].shape[2]
            big[name] = _adamw(views[which], 0, off, per, w_.reshape(-1, cols), m_.reshape(-1, cols),
                               v_.reshape(-1, cols), name=f"adamw_{name}_{lo}", tm=min(512, per), layers=(lo, hi),
                               prev=big.get(name))
        if lo <= n_pool < hi:
            big["w_kv"] = _adamw(views[2], n_pool, off_kv, dq, w_kv, m_w_kv, v_w_kv, name="adamw_w_kv",
                                 tm=min(256, dq))
        after = big["w_mem_kv"][1]
    shapes_big = dict(w_down=w_down.shape, w_up=w_up.shape, w_in=w_in.shape, w_out=w_out.shape,
                      w_mem_kv=w_mem_kv.shape, w_kv=w_kv.shape)
    big = {n: [r.reshape(shapes_big[n]) for r in big[n]] for n in big}

    order = ["norm_mix", "w_in", "pool_w", "pool_scale", "kv_norm", "w_kv", "k_norm", "q_norm", "sinks", "mem_norm",
             "w_mem_kv", "mem_q_norm", "mem_k_norm", "w_out", "norm_mlp", "w_up", "w_down"]
    out = {**big, **small}
    return (loss, grad_x, *[out[n][0] for n in order], *[out[n][1] for n in order],
            *[out[n][2] for n in order], *[out[n][3] for n in order])
```

```python
import functools

import jax
import jax.numpy as jnp
from jax import lax
from jax.experimental import pallas as pl
from jax.experimental.pallas import tpu as pltpu

F32, BF = jnp.float32, jnp.bfloat16
SDS = jax.ShapeDtypeStruct
MESH = pl.DeviceIdType.MESH
ANY = pl.BlockSpec(memory_space=pl.ANY)
HBM = pl.BlockSpec(memory_space=pltpu.HBM)
SEM = pl.BlockSpec(memory_space=pltpu.SEMAPHORE)
VMEM_WHOLE = pl.BlockSpec(memory_space=pltpu.VMEM)
SIDE_EFFECT = pltpu.SideEffectType.DATAFLOW_SIDE_EFFECTING


def _in_hbm(a):
    return pltpu.with_memory_space_constraint(a, pltpu.HBM)


EPS = 1e-6
HEAD = 64
KV_HEADS = 4
KVW = KV_HEADS * HEAD
WINDOW = 128
POOL_WINDOWS = (2, 4, 8, 16)
HALO = 16
QK_SCALE = HEAD ** -0.5
NEG = float(jnp.finfo(jnp.float32).min)
N_CHIPS = 4
LANE = 128

ADAM_LR, ADAM_B1, ADAM_B2, ADAM_EPS, ADAM_WD, ADAM_STEP = 0.001, 0.9, 0.999, 1e-08, 0.01, 10

VMEM_LIMIT_MB = 56


def _call(body, name, grid, in_specs, out_specs, out_shape, *, scratch=(), semantics=None, aliases=None,
          prefetch=0):
    params = pltpu.CompilerParams(dimension_semantics=semantics, vmem_limit_bytes=VMEM_LIMIT_MB << 20)
    if prefetch:
        spec = pltpu.PrefetchScalarGridSpec(num_scalar_prefetch=prefetch, grid=grid, in_specs=in_specs,
                                            out_specs=out_specs, scratch_shapes=list(scratch))
        return pl.pallas_call(body, name=name, grid_spec=spec, out_shape=out_shape,
                              input_output_aliases=aliases or {}, compiler_params=params)
    return pl.pallas_call(body, name=name, grid=grid, in_specs=in_specs, out_specs=out_specs, out_shape=out_shape,
                          scratch_shapes=list(scratch), input_output_aliases=aliases or {}, compiler_params=params)


def _tile(n, pref):
    return max(t for t in range(8, min(n, pref) + 1, 8) if n % t == 0)


def _dot(a, b):
    return jnp.dot(a, b, preferred_element_type=F32)


def _dot_nt(a, b):
    return lax.dot_general(a, b, (((1,), (1,)), ((), ())), preferred_element_type=F32)


def _dot_tn(a, b):
    return lax.dot_general(a, b, (((0,), (0,)), ((), ())), preferred_element_type=F32)


def _rms(x):
    r = lax.rsqrt(jnp.mean(x * x, axis=-1, keepdims=True) + EPS)
    return x * r, r


def _rms_bwd(dy, xh, r, g):
    dg = jnp.sum(dy * xh, axis=0, keepdims=True)
    dyg = dy * g
    dx = r * (dyg - xh * jnp.mean(dyg * xh, axis=-1, keepdims=True))
    return dx, dg


def _norm_mm(h, g, w, nj, tn, *, act, name, tm):
    w_arr, w_block, w_imap = w
    rows, d = h.shape

    def body(h_ref, g_ref, w_ref, y_ref, xn_ref):
        xh, _ = _rms(h_ref[...])
        xn = (xh * g_ref[...]).astype(BF)
        xn_ref[...] = xn
        for j in range(nj):
            u = _dot(xn, w_ref[j] if nj > 1 else w_ref[...].reshape(d, tn))
            if act:
                a = jnp.maximum(u, 0.0)
                y_ref[:, j * tn:(j + 1) * tn] = (a * a).astype(BF)
            else:
                y_ref[:, j * tn:(j + 1) * tn] = u

    assert nj == 1 or w_block[0] == nj
    return _call(
        body, name, (rows // tm,),
        [pl.BlockSpec((tm, d), lambda i: (i, 0)), pl.BlockSpec((1, d), lambda i: (0, 0)),
         pl.BlockSpec(w_block, lambda i: w_imap(0))],
        [pl.BlockSpec((tm, nj * tn), lambda i: (i, 0)), pl.BlockSpec((tm, d), lambda i: (i, 0))],
        [SDS((rows, nj * tn), BF if act else F32), SDS((rows, d), BF)],
        semantics=("parallel",))(h, g, w_arr)


def _mm_res(res, a, w, *, name, tm, after=None):
    w_arr, w_block, w_imap = w
    rows, k = a.shape
    n = res.shape[1]

    def body(res_ref, a_ref, w_ref, *rest):
        rest[-1][...] = res_ref[...] + _dot(a_ref[...], w_ref[...].reshape(k, n))

    extra = [] if after is None else [after]
    return _call(
        body, name, (rows // tm,),
        [pl.BlockSpec((tm, n), lambda i: (i, 0)), pl.BlockSpec((tm, k), lambda i: (i, 0)),
         pl.BlockSpec(w_block, lambda i: w_imap(0))] + [ANY] * len(extra),
        pl.BlockSpec((tm, n), lambda i: (i, 0)), SDS((rows, n), F32), semantics=("parallel",))(res, a, w_arr, *extra)


def _mm_nt(dy, w, k, *, name, tm):
    w_arr, w_block, w_imap = w
    rows, n = dy.shape

    def body(dy_ref, w_ref, o_ref):
        o_ref[...] = _dot_nt(dy_ref[...], w_ref[...].reshape(k, n))

    return _call(
        body, name, (rows // tm,),
        [pl.BlockSpec((tm, n), lambda i: (i, 0)), pl.BlockSpec(w_block, lambda i: w_imap(0))],
        pl.BlockSpec((tm, k), lambda i: (i, 0)), SDS((rows, k), F32), semantics=("parallel",))(dy, w_arr)


def _mm_nt_relu2(dh, hh, w, nj, *, name, tm):
    w_arr, w_block, w_imap = w
    rows, d = dh.shape
    tk = hh.shape[1] // nj

    def body(dh_ref, hh_ref, w_ref, o_ref):
        dh_t = dh_ref[...]
        for j in range(nj):
            cols = slice(j * tk, (j + 1) * tk)
            dhh = _dot_nt(dh_t, w_ref[j])
            o_ref[:, cols] = (dhh * (2.0 * jnp.sqrt(hh_ref[:, cols].astype(F32)))).astype(BF)

    assert w_block[0] == nj
    return _call(
        body, name, (rows // tm,),
        [pl.BlockSpec((tm, d), lambda i: (i, 0)), pl.BlockSpec((tm, nj * tk), lambda i: (i, 0)),
         pl.BlockSpec(w_block, lambda i: w_imap(0))],
        pl.BlockSpec((tm, nj * tk), lambda i: (i, 0)), SDS((rows, nj * tk), BF),
        semantics=("parallel",))(dh, hh, w_arr)


def _mm_nt_normbwd(dy, w, nsplit, h, g, dres, *, name, tm):
    w_arr, w_block, w_imap = w
    rows, n = dy.shape
    d = h.shape[1]
    ns = n // nsplit

    def body(dy_ref, w_ref, h_ref, g_ref, dres_ref, o_ref, obf_ref, dg_ref):
        if nsplit == 1:
            dxn = _dot_nt(dy_ref[...].astype(BF), w_ref[...].reshape(d, n))
        else:
            dxn = _dot_nt(dy_ref[:, 0:ns].astype(BF), w_ref[0])
            for s in range(1, nsplit):
                dxn += _dot_nt(dy_ref[:, s * ns:(s + 1) * ns].astype(BF), w_ref[s])
        xh, r = _rms(h_ref[...])
        dx, dg = _rms_bwd(dxn, xh, r, g_ref[...])
        out = dres_ref[...] + dx
        o_ref[...] = out
        obf_ref[...] = out.astype(BF)

        @pl.when(pl.program_id(0) == 0)
        def _():
            dg_ref[...] = jnp.zeros_like(dg_ref)

        dg_ref[...] += dg

    row = lambda i: (i, 0)
    return _call(
        body, name, (rows // tm,),
        [pl.BlockSpec((tm, n), row), pl.BlockSpec(w_block, lambda i: w_imap(0)), pl.BlockSpec((tm, d), row),
         pl.BlockSpec((1, d), lambda i: (0, 0)), pl.BlockSpec((tm, d), row)],
        [pl.BlockSpec((tm, d), row), pl.BlockSpec((tm, d), row), pl.BlockSpec((1, d), lambda i: (0, 0))],
        [SDS((rows, d), F32), SDS((rows, d), BF), SDS((1, d), F32)],
        semantics=("arbitrary",))(dy, w_arr, h, g, dres)


def _mm_tn(x, dy, packed, out_imap, tk, tn, *, name):
    s_len, k = x.shape
    n = dy.shape[1]

    def body(x_ref, dy_ref, _, o_ref):
        o_ref[0] = _dot_tn(x_ref[...], dy_ref[...].astype(BF))

    return _call(
        body, name, (k // tk, n // tn),
        [pl.BlockSpec((s_len, tk), lambda i, j: (0, i)), pl.BlockSpec((s_len, tn), lambda i, j: (0, j)), ANY],
        pl.BlockSpec((1, tk, tn), out_imap), SDS(packed.shape, packed.dtype),
        semantics=("parallel", "parallel"), aliases={2: 0})(x, dy, packed)


def _loss_head(y, tgt, *, tm):
    rows, d = y.shape

    def body(y_ref, t_ref, dh_ref, dhbf_ref, loss_ref):
        err = y_ref[...] - t_ref[...]
        dh = err * (1.0 / d)
        dh_ref[...] = dh
        dhbf_ref[...] = dh.astype(BF)

        @pl.when(pl.program_id(0) == 0)
        def _():
            loss_ref[...] = jnp.zeros_like(loss_ref)

        loss_ref[...] += 0.5 * jnp.sum(jnp.mean(err * err, axis=-1, keepdims=True), axis=0, keepdims=True)

    row = lambda i: (i, 0)
    return _call(
        body, "loss_head", (rows // tm,), [pl.BlockSpec((tm, d), row), pl.BlockSpec((tm, d), row)],
        [pl.BlockSpec((tm, d), row), pl.BlockSpec((tm, d), row), pl.BlockSpec((1, 1), lambda i: (0, 0))],
        [SDS((rows, d), F32), SDS((rows, d), BF), SDS((1, 1), F32)], semantics=("arbitrary",))(y, tgt)


def _hs(h):
    return slice(HEAD * h, HEAD * (h + 1))


def _softmax_rows(s):
    e = jnp.exp(s - jnp.max(s, axis=-1, keepdims=True))
    return e * (1.0 / jnp.sum(e, axis=-1, keepdims=True))


def _scaled_bf16(qn):
    return (qn * QK_SCALE).astype(BF)


def _mem_fwd(mq, mk, mv, gq):
    outs, probs = [], []
    for h in range(KV_HEADS):
        xh, _ = _rms(mq[:, _hs(h)])
        p = _softmax_rows(_dot_nt(_scaled_bf16(xh * gq), mk[:, _hs(h)])).astype(BF)
        probs.append(p)
        outs.append(_dot(p, mv[:, _hs(h)]))
    return jnp.concatenate(outs, axis=-1), jnp.concatenate(probs, axis=-1)


def _mem_bwd(mq, do, mk, mv, gq, probs):
    dqs, dks, dvs, dgq = [], [], [], 0.0
    mlen = mk.shape[0]
    for h in range(KV_HEADS):
        xh, r = _rms(mq[:, _hs(h)])
        qn = _scaled_bf16(xh * gq)
        p_bf = probs[:, h * mlen:(h + 1) * mlen]
        p = p_bf.astype(F32)
        doh = do[:, _hs(h)].astype(BF)
        dp = _dot_nt(doh, mv[:, _hs(h)])
        ds = (p * (dp - jnp.sum(p * dp, axis=-1, keepdims=True))).astype(BF)
        dq, dg = _rms_bwd(_dot(ds, mk[:, _hs(h)]) * QK_SCALE, xh, r, gq)
        dqs.append(dq)
        dgq = dgq + dg
        dks.append(_dot_tn(ds, qn))
        dvs.append(_dot_tn(p_bf, doh))
    cat = lambda xs: jnp.concatenate(xs, axis=-1)
    return cat(dqs), cat(dks), cat(dvs), dgq


def _mem_kv(mkv, gk):
    ks = []
    for h in range(KV_HEADS):
        xh, _ = _rms(mkv[:, _hs(h)])
        ks.append(xh * gk)
    return jnp.concatenate(ks, axis=-1).astype(BF), mkv[:, KVW:].astype(BF)


def _mem_kv_bwd(mkv, dmk, dmv, gk):
    dxs, dgk = [], 0.0
    for h in range(KV_HEADS):
        xh, r = _rms(mkv[:, _hs(h)])
        dx, dg = _rms_bwd(dmk[:, _hs(h)], xh, r, gk)
        dxs.append(dx)
        dgk = dgk + dg
    return jnp.concatenate(dxs + [dmv], axis=-1), dgk


def _pool_select(col, gd, a2, a4, a8, a16):
    return jnp.where(col < gd, a2, jnp.where(col < 2 * gd, a4, jnp.where(col < 3 * gd, a8, a16)))


def _pool_count(t0, shape, gd):
    col = lax.broadcasted_iota(jnp.int32, shape, 1)
    t = t0 + lax.broadcasted_iota(jnp.int32, shape, 0)
    win = _pool_select(col, gd, *POOL_WINDOWS)
    return jnp.minimum(t + 1, win).astype(F32)


def _pool_diff(u, halo, t0, gd):
    c = jnp.concatenate([halo, u], axis=0)
    s2 = c + pltpu.roll(c, 1, 0)
    s4 = s2 + pltpu.roll(s2, 2, 0)
    s8 = s4 + pltpu.roll(s4, 4, 0)
    s16 = s8 + pltpu.roll(s8, 8, 0)
    col = lax.broadcasted_iota(jnp.int32, c.shape, 1)
    ws = _pool_select(col, gd, s2, s4, s8, s16)[HALO:]
    return ws / _pool_count(t0, u.shape, gd) - u


def _pool_diff_bwd(dd, dd_halo, t0, gd):
    t = dd.shape[0]
    z = jnp.concatenate([dd / _pool_count(t0, dd.shape, gd), dd_halo / _pool_count(t0 + t, dd_halo.shape, gd)], axis=0)
    n = z.shape[0]
    f2 = z + pltpu.roll(z, n - 1, 0)
    f4 = f2 + pltpu.roll(f2, n - 2, 0)
    f8 = f4 + pltpu.roll(f4, n - 4, 0)
    f16 = f8 + pltpu.roll(f8, n - 8, 0)
    col = lax.broadcasted_iota(jnp.int32, z.shape, 1)
    return _pool_select(col, gd, f2, f4, f8, f16)[:t] - dd


def _swa_bias(n):
    qi = lax.broadcasted_iota(jnp.int32, (WINDOW, 2 * WINDOW), 0)
    kj = lax.broadcasted_iota(jnp.int32, (WINDOW, 2 * WINDOW), 1)
    dist = qi + WINDOW - kj
    valid = (dist >= 0) & (dist < WINDOW) & ((kj >= WINDOW) | (n > 0))
    return dist.astype(F32), valid


def _slopes(qh):
    return [2.0 ** (-8.0 * (h + 1) / qh) for h in range(qh)]


def _swa_probs(qn, kk, dist, valid, slope, sink):
    s = _dot_nt(qn, kk) - slope * dist
    s = jnp.where(valid, s, NEG)
    m = jnp.maximum(jnp.max(s, axis=-1, keepdims=True), sink)
    e = jnp.exp(s - m)
    es = jnp.exp(sink - m)
    z = jnp.sum(e, axis=-1, keepdims=True) + es
    inv = 1.0 / z
    return e * inv, es * inv


def _stack_heads(a, kh, grp):
    return jnp.concatenate([a[:, _hs(h)] for h in range(kh * grp, (kh + 1) * grp)], axis=0)


def _swa_group(q, kh, grp, n, qh, sinks):
    heads = range(kh * grp, (kh + 1) * grp)
    dist, valid = _swa_bias(n)
    slopes = _slopes(qh)
    rows = lambda vals: jnp.concatenate([jnp.broadcast_to(v, (WINDOW, 1)) for v in vals], axis=0)
    slope = rows([jnp.full((1, 1), slopes[h], F32) for h in heads])
    sink = rows([sinks[:, h:h + 1] for h in heads])
    return (_stack_heads(q, kh, grp), slope, sink, jnp.concatenate([dist] * grp, axis=0),
            jnp.concatenate([valid] * grp, axis=0))


def _swa_fwd(q, kk, vv, gq, sinks, n, qh):
    grp = qh // KV_HEADS
    lane = lax.broadcasted_iota(jnp.int32, (WINDOW, LANE), 1)
    outs, probs, sink_probs = [], [], jnp.zeros((WINDOW, LANE), F32)
    for kh in range(KV_HEADS):
        qs, slope, sink, dist, valid = _swa_group(q, kh, grp, n, qh, sinks)
        xh, _ = _rms(qs)
        p, ps = _swa_probs(_scaled_bf16(xh * gq), kk[:, _hs(kh)], dist, valid, slope, sink)
        p = p.astype(BF)
        probs.append(p)
        o = _dot(p, vv[:, _hs(kh)])
        for g in range(grp):
            outs.append(o[g * WINDOW:(g + 1) * WINDOW])
            sink_probs = jnp.where(lane == kh * grp + g, ps[g * WINDOW:(g + 1) * WINDOW], sink_probs)
    return jnp.concatenate(outs, axis=-1), probs, sink_probs


def _swa_bwd(q, do, kk, vv, gq, probs, sink_probs, qh):
    grp = qh // KV_HEADS
    lane = lax.broadcasted_iota(jnp.int32, (1, LANE), 1)
    dqs, dks, dvs, dgq, dsk = [], [], [], 0.0, jnp.zeros((1, LANE), F32)
    do = do.astype(BF)
    for kh in range(KV_HEADS):
        heads = range(kh * grp, (kh + 1) * grp)
        xh, r = _rms(_stack_heads(q, kh, grp))
        qn = _scaled_bf16(xh * gq)
        p_bf = probs[kh]
        p = p_bf.astype(F32)
        ps = jnp.concatenate([sink_probs[:, h:h + 1] for h in heads], axis=0)
        dos = _stack_heads(do, kh, grp)
        dp = _dot_nt(dos, vv[:, _hs(kh)])
        delta = jnp.sum(p * dp, axis=-1, keepdims=True)
        ds = (p * (dp - delta)).astype(BF)
        dsink = ps * delta
        for g in range(grp):
            part = -jnp.sum(dsink[g * WINDOW:(g + 1) * WINDOW], axis=0, keepdims=True)
            dsk = dsk + jnp.where(lane == kh * grp + g, part, 0.0)
        dq, dg = _rms_bwd(_dot(ds, kk[:, _hs(kh)]) * QK_SCALE, xh, r, gq)
        dqs += [dq[g * WINDOW:(g + 1) * WINDOW] for g in range(grp)]
        dgq = dgq + dg
        dks.append(_dot_tn(ds, qn))
        dvs.append(_dot_tn(p_bf, dos))
    cat = lambda xs: jnp.concatenate(xs, axis=-1)
    return cat(dqs), cat(dks), cat(dvs), dgq, dsk


def _mixer_pool_fwd(proj, mkv, pbd, scale, gq, gk, *, name, tm):
    s_len, d = proj.shape
    main = d - KVW
    gd = main // len(POOL_WINDOWS)
    mlen = mkv.shape[0]
    hb = tm // HALO

    def body(u_ref, halo_ref, mq_ref, mkv_ref, pbd_ref, scale_ref, gq_ref, gk_ref, o_ref, mp_ref, mk_s, mv_s):
        i = pl.program_id(0)

        @pl.when(i == 0)
        def _():
            mk, mv = _mem_kv(mkv_ref[...], gk_ref[...])
            mk_s[...] = mk
            mv_s[...] = mv

        halo = jnp.where(i > 0, halo_ref[...], 0.0)
        dif = _pool_diff(u_ref[...], halo, i * tm, gd)
        mixed = _dot(dif.astype(BF), pbd_ref[...]) * scale_ref[...]
        mem, mp_ref[...] = _mem_fwd(mq_ref[...], mk_s[...], mv_s[...], gq_ref[...])
        o_ref[...] = jnp.concatenate([mixed, mem], axis=-1).astype(BF)

    full = lambda shape: pl.BlockSpec(shape, lambda i: (0,) * len(shape))
    return _call(
        body, name, (s_len // tm,),
        [pl.BlockSpec((tm, main), lambda i: (i, 0)),
         pl.BlockSpec((HALO, main), lambda i: (jnp.maximum(i * hb - 1, 0), 0)),
         pl.BlockSpec((tm, KVW), lambda i: (i, main // KVW)),
         full((mlen, 2 * KVW)), full((main, main)), full((1, main)), full((1, HEAD)), full((1, HEAD))],
        [pl.BlockSpec((tm, d), lambda i: (i, 0)), pl.BlockSpec((tm, KV_HEADS * mlen), lambda i: (i, 0))],
        [SDS((s_len, d), BF), SDS((s_len, KV_HEADS * mlen), BF)],
        scratch=[pltpu.VMEM((mlen, KVW), BF), pltpu.VMEM((mlen, KVW), BF)],
        semantics=("arbitrary",))(proj, proj, proj, mkv, pbd, scale, gq, gk)


def _mixer_pool_bwd(proj, dcat, mkv, pbd, scale, gq, gk, mem_probs, *, name, tm):
    s_len, d = proj.shape
    main = d - KVW
    gd = main // len(POOL_WINDOWS)
    mlen = mkv.shape[0]
    hb = tm // HALO
    nt = s_len // tm
    last_halo = s_len // HALO - 1

    def body(u_ref, halo_ref, mq_ref, do_ref, donext_ref, dom_ref, mkv_ref, pbd_ref, scale_ref, gq_ref, gk_ref, mp_ref,
             dproj_ref, dpbd_ref, dscale_ref, dmkv_ref, dgq_ref, dgk_ref, mk_s, mv_s, dmk_s, dmv_s):
        i = pl.program_id(0)

        @pl.when(i == 0)
        def _():
            mk, mv = _mem_kv(mkv_ref[...], gk_ref[...])
            mk_s[...] = mk
            mv_s[...] = mv
            dmk_s[...] = jnp.zeros_like(dmk_s)
            dmv_s[...] = jnp.zeros_like(dmv_s)
            dpbd_ref[...] = jnp.zeros_like(dpbd_ref)
            dscale_ref[...] = jnp.zeros_like(dscale_ref)
            dgq_ref[...] = jnp.zeros_like(dgq_ref)

        pbd = pbd_ref[...]
        scale = scale_ref[...]
        halo = jnp.where(i > 0, halo_ref[...], 0.0)
        dif = _pool_diff(u_ref[...], halo, i * tm, gd).astype(BF)
        do = do_ref[...]
        dscale_ref[...] += jnp.sum(do * _dot(dif, pbd), axis=0, keepdims=True)
        dmixed = (do * scale).astype(BF)
        dpbd_ref[...] += _dot_tn(dif, dmixed)
        dd = _dot_nt(dmixed, pbd)
        donext = jnp.where(i < nt - 1, donext_ref[...], 0.0)
        dd_halo = _dot_nt((donext * scale).astype(BF), pbd)
        du = _pool_diff_bwd(dd, dd_halo, i * tm, gd)

        dmq, dmk, dmv, dgq = _mem_bwd(mq_ref[...], dom_ref[...], mk_s[...], mv_s[...], gq_ref[...], mp_ref[...])
        dmk_s[...] += dmk
        dmv_s[...] += dmv
        dgq_ref[...] += dgq
        dproj_ref[...] = jnp.concatenate([du, dmq], axis=-1).astype(BF)

        @pl.when(i == nt - 1)
        def _():
            dmkv, dgk = _mem_kv_bwd(mkv_ref[...], dmk_s[...], dmv_s[...], gk_ref[...])
            dmkv_ref[...] = dmkv
            dgk_ref[...] = dgk

    full = lambda shape: pl.BlockSpec(shape, lambda i: (0,) * len(shape))
    return _call(
        body, name, (nt,),
        [pl.BlockSpec((tm, main), lambda i: (i, 0)),
         pl.BlockSpec((HALO, main), lambda i: (jnp.maximum(i * hb - 1, 0), 0)),
         pl.BlockSpec((tm, KVW), lambda i: (i, main // KVW)),
         pl.BlockSpec((tm, main), lambda i: (i, 0)),
         pl.BlockSpec((HALO, main), lambda i: (jnp.minimum((i + 1) * hb, last_halo), 0)),
         pl.BlockSpec((tm, KVW), lambda i: (i, main // KVW)),
         full((mlen, 2 * KVW)), full((main, main)), full((1, main)), full((1, HEAD)), full((1, HEAD)),
         pl.BlockSpec((tm, KV_HEADS * mlen), lambda i: (i, 0))],
        [pl.BlockSpec((tm, d), lambda i: (i, 0)), full((main, main)), full((1, main)), full((mlen, 2 * KVW)),
         full((1, HEAD)), full((1, HEAD))],
        [SDS((s_len, d), BF), SDS((main, main), F32), SDS((1, main), F32), SDS((mlen, 2 * KVW), F32),
         SDS((1, HEAD), F32), SDS((1, HEAD), F32)],
        scratch=[pltpu.VMEM((mlen, KVW), BF), pltpu.VMEM((mlen, KVW), BF), pltpu.VMEM((mlen, KVW), F32),
                 pltpu.VMEM((mlen, KVW), F32)],
        semantics=("arbitrary",))(proj, proj, proj, dcat, dcat, dcat, mkv, pbd, scale, gq, gk, mem_probs)


def _mem_attn_fwd(proj, mkv, gq, gk, cat, *, name, tm):
    s_len, d = proj.shape
    main = d - KVW
    mlen = mkv.shape[0]

    def body(mq_ref, mkv_ref, gq_ref, gk_ref, _, o_ref, mp_ref, mk_s, mv_s):
        @pl.when(pl.program_id(0) == 0)
        def _():
            mk, mv = _mem_kv(mkv_ref[...], gk_ref[...])
            mk_s[...] = mk
            mv_s[...] = mv

        mem, mp_ref[...] = _mem_fwd(mq_ref[...], mk_s[...], mv_s[...], gq_ref[...])
        o_ref[...] = mem.astype(BF)

    full = lambda shape: pl.BlockSpec(shape, lambda i: (0,) * len(shape))
    memcol = lambda i: (i, main // KVW)
    return _call(
        body, name, (s_len // tm,),
        [pl.BlockSpec((tm, KVW), memcol), full((mlen, 2 * KVW)), full((1, HEAD)), full((1, HEAD)), ANY],
        [pl.BlockSpec((tm, KVW), memcol), pl.BlockSpec((tm, KV_HEADS * mlen), lambda i: (i, 0))],
        [SDS((s_len, d), BF), SDS((s_len, KV_HEADS * mlen), BF)],
        scratch=[pltpu.VMEM((mlen, KVW), BF), pltpu.VMEM((mlen, KVW), BF)],
        semantics=("arbitrary",), aliases={4: 0})(proj, mkv, gq, gk, cat)


def _mem_attn_bwd(proj, dcat, mkv, gq, gk, mem_probs, dproj, *, name, tm):
    s_len, d = proj.shape
    main = d - KVW
    mlen = mkv.shape[0]
    nt = s_len // tm

    def body(mq_ref, dom_ref, mkv_ref, gq_ref, gk_ref, mp_ref, _, dproj_ref, dmkv_ref, dgq_ref, dgk_ref,
             mk_s, mv_s, dmk_s, dmv_s):
        i = pl.program_id(0)

        @pl.when(i == 0)
        def _():
            mk, mv = _mem_kv(mkv_ref[...], gk_ref[...])
            mk_s[...] = mk
            mv_s[...] = mv
            dmk_s[...] = jnp.zeros_like(dmk_s)
            dmv_s[...] = jnp.zeros_like(dmv_s)
            dgq_ref[...] = jnp.zeros_like(dgq_ref)

        dmq, dmk, dmv, dgq = _mem_bwd(mq_ref[...], dom_ref[...], mk_s[...], mv_s[...], gq_ref[...], mp_ref[...])
        dmk_s[...] += dmk
        dmv_s[...] += dmv
        dgq_ref[...] += dgq
        dproj_ref[...] = dmq.astype(BF)

        @pl.when(i == nt - 1)
        def _():
            dmkv, dgk = _mem_kv_bwd(mkv_ref[...], dmk_s[...], dmv_s[...], gk_ref[...])
            dmkv_ref[...] = dmkv
            dgk_ref[...] = dgk

    full = lambda shape: pl.BlockSpec(shape, lambda i: (0,) * len(shape))
    memcol = lambda i: (i, main // KVW)
    return _call(
        body, name, (nt,),
        [pl.BlockSpec((tm, KVW), memcol), pl.BlockSpec((tm, KVW), memcol), full((mlen, 2 * KVW)), full((1, HEAD)),
         full((1, HEAD)), pl.BlockSpec((tm, KV_HEADS * mlen), lambda i: (i, 0)), ANY],
        [pl.BlockSpec((tm, KVW), memcol), full((mlen, 2 * KVW)), full((1, HEAD)), full((1, HEAD))],
        [SDS((s_len, d), BF), SDS((mlen, 2 * KVW), F32), SDS((1, HEAD), F32), SDS((1, HEAD), F32)],
        scratch=[pltpu.VMEM((mlen, KVW), BF), pltpu.VMEM((mlen, KVW), BF), pltpu.VMEM((mlen, KVW), F32),
                 pltpu.VMEM((mlen, KVW), F32)],
        semantics=("arbitrary",), aliases={6: 0})(proj, dcat, mkv, gq, gk, mem_probs, dproj)


def _mixer_swa_fwd(proj, kn, v, gqs, sinks, *, name):
    s_len, d = proj.shape
    main = d - KVW
    qh = main // HEAD
    tm = WINDOW
    prow = qh // KV_HEADS * tm

    def body(q_ref, kp_ref, kc_ref, vp_ref, vc_ref, gqs_ref, sinks_ref, o_ref, p_ref, ps_ref):
        n = pl.program_id(0)
        kk = jnp.concatenate([kp_ref[...], kc_ref[...]], axis=0)
        vv = jnp.concatenate([vp_ref[...], vc_ref[...]], axis=0)
        att, probs, sink_probs = _swa_fwd(q_ref[...], kk, vv, gqs_ref[...], sinks_ref[...], n, qh)
        for kh in range(KV_HEADS):
            p_ref[0, kh] = probs[kh]
        ps_ref[...] = sink_probs
        o_ref[...] = att.astype(BF)

    full = lambda shape: pl.BlockSpec(shape, lambda i: (0,) * len(shape))
    prev = lambda i: (jnp.maximum(i - 1, 0), 0)
    cur = lambda i: (i, 0)
    return _call(
        body, name, (s_len // tm,),
        [pl.BlockSpec((tm, main), cur), pl.BlockSpec((tm, KVW), prev), pl.BlockSpec((tm, KVW), cur),
         pl.BlockSpec((tm, KVW), prev), pl.BlockSpec((tm, KVW), cur), full((1, HEAD)), full((1, LANE))],
        [pl.BlockSpec((tm, main), cur), pl.BlockSpec((1, KV_HEADS, prow, 2 * tm), lambda i: (i, 0, 0, 0)),
         pl.BlockSpec((tm, LANE), cur)],
        [SDS((s_len, d), BF), SDS((s_len // tm, KV_HEADS, prow, 2 * tm), BF), SDS((s_len, LANE), F32)],
        semantics=("parallel",))(proj, kn, kn, v, v, gqs, sinks)


def _mixer_swa_bwd(proj, dcat, kn, v, gqs, probs, sink_probs, *, name):
    s_len, d = proj.shape
    main = d - KVW
    qh = main // HEAD
    tm = WINDOW
    nt = s_len // tm
    prow = qh // KV_HEADS * tm

    def body(q_ref, do_ref, kp_ref, kc_ref, vp_ref, vc_ref, gqs_ref, p_ref, ps_ref,
             dproj_ref, dk_ref, dv_ref, dgqs_ref, dsinks_ref):
        n = pl.program_id(0)

        @pl.when(n == 0)
        def _():
            dk_ref[...] = jnp.zeros_like(dk_ref)
            dv_ref[...] = jnp.zeros_like(dv_ref)
            dgqs_ref[...] = jnp.zeros_like(dgqs_ref)
            dsinks_ref[...] = jnp.zeros_like(dsinks_ref)

        kk = jnp.concatenate([kp_ref[...], kc_ref[...]], axis=0)
        vv = jnp.concatenate([vp_ref[...], vc_ref[...]], axis=0)
        dq, dkk, dvv, dgqs, dsk = _swa_bwd(q_ref[...], do_ref[...], kk, vv, gqs_ref[...],
                                           [p_ref[0, kh] for kh in range(KV_HEADS)], ps_ref[...], qh)
        prev = pl.ds(pl.multiple_of(jnp.maximum(n - 1, 0) * tm, tm), tm)
        own = pl.ds(pl.multiple_of(n * tm, tm), tm)
        dk_ref[prev, :] += dkk[:tm]
        dk_ref[own, :] += dkk[tm:]
        dv_ref[prev, :] += dvv[:tm]
        dv_ref[own, :] += dvv[tm:]
        dgqs_ref[...] += dgqs
        dsinks_ref[...] += dsk
        dproj_ref[...] = dq.astype(BF)

    full = lambda shape: pl.BlockSpec(shape, lambda i: (0,) * len(shape))
    prev_b = lambda i: (jnp.maximum(i - 1, 0), 0)
    cur = lambda i: (i, 0)
    return _call(
        body, name, (nt,),
        [pl.BlockSpec((tm, main), cur), pl.BlockSpec((tm, main), cur),
         pl.BlockSpec((tm, KVW), prev_b), pl.BlockSpec((tm, KVW), cur),
         pl.BlockSpec((tm, KVW), prev_b), pl.BlockSpec((tm, KVW), cur),
         full((1, HEAD)), pl.BlockSpec((1, KV_HEADS, prow, 2 * tm), lambda i: (i, 0, 0, 0)),
         pl.BlockSpec((tm, LANE), cur)],
        [pl.BlockSpec((tm, main), cur), full((s_len, KVW)), full((s_len, KVW)), full((1, HEAD)), full((1, LANE))],
        [SDS((s_len, d), BF), SDS((s_len, KVW), F32), SDS((s_len, KVW), F32), SDS((1, HEAD), F32),
         SDS((1, LANE), F32)],
        semantics=("arbitrary",))(proj, dcat, kn, kn, v, v, gqs, probs, sink_probs)


def _kv_prep(kv, gk, *, tm):
    s_len = kv.shape[0]

    def body(kv_ref, gk_ref, k_ref, v_ref):
        k, v = _mem_kv(kv_ref[...], gk_ref[...])
        k_ref[...] = k
        v_ref[...] = v

    row = lambda i: (i, 0)
    return _call(
        body, "kv_prep", (s_len // tm,),
        [pl.BlockSpec((tm, 2 * KVW), row), pl.BlockSpec((1, HEAD), lambda i: (0, 0))],
        [pl.BlockSpec((tm, KVW), row), pl.BlockSpec((tm, KVW), row)],
        [SDS((s_len, KVW), BF), SDS((s_len, KVW), BF)], semantics=("parallel",))(kv, gk)


def _kv_bwd(kv, dks, dvs, gk, *, tm):
    s_len = kv.shape[0]
    nl = len(dks)

    def body(*refs):
        kv_ref, gk_ref = refs[0], refs[1]
        dk_refs, dv_refs = refs[2:2 + nl], refs[2 + nl:2 + 2 * nl]
        dkv_ref, dgk_ref = refs[2 + 2 * nl], refs[3 + 2 * nl]
        dk, dv = dk_refs[0][...], dv_refs[0][...]
        for t in range(1, nl):
            dk = dk + dk_refs[t][...]
            dv = dv + dv_refs[t][...]
        dkv, dgk = _mem_kv_bwd(kv_ref[...], dk, dv, gk_ref[...])
        dkv_ref[...] = dkv.astype(BF)

        @pl.when(pl.program_id(0) == 0)
        def _():
            dgk_ref[...] = jnp.zeros_like(dgk_ref)

        dgk_ref[...] += dgk

    row = lambda i: (i, 0)
    one = pl.BlockSpec((1, HEAD), lambda i: (0, 0))
    return _call(
        body, "kv_bwd", (s_len // tm,),
        [pl.BlockSpec((tm, 2 * KVW), row), one] + [pl.BlockSpec((tm, KVW), row)] * (2 * nl),
        [pl.BlockSpec((tm, 2 * KVW), row), one],
        [SDS((s_len, 2 * KVW), BF), SDS((1, HEAD), F32)], semantics=("arbitrary",))(kv, gk, *dks, *dvs)


def _place():
    x, y, c = lax.axis_index("x"), lax.axis_index("y"), lax.axis_index("c")
    flips = [(1 - x, y), (x, 1 - y), (1 - x, 1 - y)]
    return x, y, c, flips


def _remote(src, dst, send_sem, recv_sem, to):
    return pltpu.make_async_remote_copy(src_ref=src, dst_ref=dst, send_sem=send_sem, recv_sem=recv_sem,
                                        device_id=to, device_id_type=MESH)


def _gather_copies(p_refs, wg_refs, send, recv):
    x, y, c, flips = _place()
    chip = 2 * x + y
    cps = []
    for j, (fx, fy) in enumerate(flips):
        for b in range(2):
            half = p_refs[b].shape[0] // 2
            mine = pl.ds(c * half, half)
            cps.append(_remote(p_refs[b].at[mine, :], wg_refs[b].at[chip, mine, :], send.at[2 * j + b],
                               recv.at[2 * j + b], (fx, fy, c)))
    return cps, cps


def _forward_copies(p_refs, wg_refs, send, recv):
    x, y, c, flips = _place()
    chip = 2 * x + y
    sib = (x, y, 1 - c)
    sends, arrivals = [], []
    for b in range(2):
        half = p_refs[b].shape[0] // 2
        own = _remote(p_refs[b], wg_refs[b].at[chip], send.at[b], recv.at[b], sib)
        sends.append(own)
        arrivals.append(own)
        for j, (fx, fy) in enumerate(flips):
            k = 2 + 3 * b + j
            landed = wg_refs[b].at[2 * fx + fy, pl.ds(c * half, half), :]
            other = wg_refs[b].at[2 * fx + fy, pl.ds((1 - c) * half, half), :]
            sends.append(_remote(landed, landed, send.at[k], recv.at[k], sib))
            arrivals.append(_remote(other, other, send.at[k], recv.at[k], sib))
    return sends, arrivals


def _swap_copies(g_refs, r_refs, send, recv):
    x, y, c, _ = _place()
    cps = []
    for b in range(2):
        half = g_refs[b].shape[1] // 2
        cps.append(_remote(g_refs[b].at[:, pl.ds((1 - c) * half, half), :], r_refs[b], send.at[b], recv.at[b],
                           (x, y, 1 - c)))
    return cps, cps


def _split_start(make_copies, n_sems, bufs, after, fresh, *, name):
    def body(a1, a2, b1, b2, after_ref, send, recv, *outs):
        for cp in make_copies((a1, a2), (b1, b2), send, recv)[0]:
            cp.start()
        outs[4][...] = jnp.zeros_like(outs[4])

    extra_shape = () if fresh is None else (pltpu.HBM(fresh, F32),)
    extra_spec = () if fresh is None else (HBM,)
    return pl.pallas_call(
        body, name=name,
        out_shape=(pltpu.SemaphoreType.DMA((n_sems,)), pltpu.SemaphoreType.DMA((n_sems,)))
        + tuple(pltpu.HBM(b.shape, b.dtype) for b in bufs) + (SDS((8, LANE), F32),) + extra_shape,
        in_specs=(HBM, HBM, HBM, HBM, ANY), out_specs=(SEM, SEM, HBM, HBM, HBM, HBM, VMEM_WHOLE) + extra_spec,
        input_output_aliases={0: 2, 1: 3, 2: 4, 3: 5},
        compiler_params=pltpu.CompilerParams(has_side_effects=SIDE_EFFECT))(*[_in_hbm(b) for b in bufs], after)


def _split_wait(make_copies, started, after, *, name):
    send, recv, bufs = started[0], started[1], started[2:6]

    def body(a1, a2, b1, b2, send_ref, recv_ref, after_ref, *outs):
        sends, arrivals = make_copies((a1, a2), (b1, b2), send_ref, recv_ref)
        for cp in arrivals:
            cp.wait_recv()
        for cp in sends:
            cp.wait_send()

    return pl.pallas_call(
        body, name=name, out_shape=tuple(pltpu.HBM(b.shape, b.dtype) for b in bufs),
        in_specs=(HBM, HBM, HBM, HBM, SEM, SEM, ANY), out_specs=(HBM, HBM, HBM, HBM),
        input_output_aliases={0: 0, 1: 1, 2: 2, 3: 3},
        compiler_params=pltpu.CompilerParams(has_side_effects=SIDE_EFFECT))(*bufs, send, recv, after)


def _gather_small(ps):
    def body(ps_ref, o_ref, send, recv):
        x, y, c, flips = _place()
        chip = 2 * x + y
        o_ref[chip] = ps_ref[...]
        cps = [_remote(ps_ref, o_ref.at[chip], send.at[j], recv.at[j], (fx, fy, c))
               for j, (fx, fy) in enumerate(flips)]
        for cp in cps:
            cp.start()
        for j, (fx, fy) in enumerate(flips):
            _remote(ps_ref, o_ref.at[2 * fx + fy], send.at[j], recv.at[j], (fx, fy, c)).wait_recv()
        for cp in cps:
            cp.wait_send()

    return pl.pallas_call(
        body, name="gather_small", in_specs=[VMEM_WHOLE], out_specs=VMEM_WHOLE,
        out_shape=SDS((N_CHIPS,) + ps.shape, ps.dtype),
        scratch_shapes=[pltpu.SemaphoreType.DMA((3,)), pltpu.SemaphoreType.DMA((3,))])(ps)


def _sum_sibling(g, r, place, *, tm, name):
    n_sh, half, w = r.shape
    nt = half // tm

    def body(place_ref, g_ref, r_ref, pbf_ref, own_ref):
        s = pl.program_id(1)
        p = g_ref[0] + r_ref[0]
        pbf_ref[0] = p.astype(BF)

        @pl.when(s == place_ref[1])
        def _():
            own_ref[...] = p

    return _call(
        body, name, (nt, n_sh),
        [pl.BlockSpec((1, tm, w), lambda i, s, pr: (s, pr[0] * nt + i, 0)),
         pl.BlockSpec((1, tm, w), lambda i, s, pr: (s, i, 0))],
        [pl.BlockSpec((1, tm, w), lambda i, s, pr: (s, i, 0)), pl.BlockSpec((tm, w), lambda i, s, pr: (i, 0))],
        [SDS((n_sh, half, w), BF), SDS((half, w), F32)],
        semantics=("arbitrary", "arbitrary"), prefetch=1)(place, g, r)


def _rs_copies(p_refs, land_refs, send, recv):
    _, _, c, flips = _place()
    cps = []
    for j, (fx, fy) in enumerate(flips):
        for b in range(2):
            cps.append(_remote(p_refs[b].at[2 * fx + fy], land_refs[b].at[j], send.at[2 * j + b], recv.at[2 * j + b],
                               (fx, fy, c)))
    return cps, cps


def _sum_chips(own, r, full, layer, place, *, tm, name):
    half, w = own.shape

    def body(place_ref, own_ref, r_ref, _, o_ref):
        o_ref[0, 0] = ((own_ref[...] + r_ref[0].astype(F32)) + r_ref[1].astype(F32)) + r_ref[2].astype(F32)

    return _call(
        body, name, (half // tm,),
        [pl.BlockSpec((tm, w), lambda i, pr: (i, 0)), pl.BlockSpec((3, tm, w), lambda i, pr: (0, i, 0)), ANY],
        pl.BlockSpec((1, 1, tm, w), lambda i, pr: (layer, pr[0], i, 0)), SDS(full.shape, F32),
        semantics=("parallel",), prefetch=1, aliases={3: 0})(place, own, r, full)


def _share_with_sibling(f1, f2, lo, hi, *, name):
    def body(_, __, o1_ref, o2_ref, send, recv):
        x, y, c, _ = _place()
        lay, mine, other = pl.ds(lo, hi - lo), pl.ds(c, 1), pl.ds(1 - c, 1)
        cps = [_remote(o1_ref.at[lay, mine], o1_ref.at[lay, mine], send.at[0], recv.at[0], (x, y, 1 - c)),
               _remote(o2_ref.at[lay, mine], o2_ref.at[lay, mine], send.at[1], recv.at[1], (x, y, 1 - c))]
        for cp in cps:
            cp.start()
        for cp in cps:
            cp.wait_send()
        _remote(o1_ref.at[lay, other], o1_ref.at[lay, other], send.at[0], recv.at[0], (x, y, 1 - c)).wait_recv()
        _remote(o2_ref.at[lay, other], o2_ref.at[lay, other], send.at[1], recv.at[1], (x, y, 1 - c)).wait_recv()

    return pl.pallas_call(
        body, name=name, in_specs=[ANY, ANY], out_specs=[ANY, ANY],
        out_shape=[SDS(f1.shape, f1.dtype), SDS(f2.shape, f2.dtype)], input_output_aliases={0: 0, 1: 1},
        scratch_shapes=[pltpu.SemaphoreType.DMA((2,)), pltpu.SemaphoreType.DMA((2,))])(f1, f2)


def _allreduce_small(sg):
    rows, w = sg.shape
    half = rows // 2
    assert half % 8 == 0

    def body(sg_ref, o_ref, sib_buf, part, slots, send, recv):
        x, y, c, flips = _place()
        chip = 2 * x + y
        sib = (x, y, 1 - c)
        mine = pl.ds(pl.multiple_of(c * half, 8), half)
        other = pl.ds(pl.multiple_of((1 - c) * half, 8), half)
        to_sib = _remote(sg_ref.at[other, :], sib_buf, send.at[0], recv.at[0], sib)
        to_sib.start()
        to_sib.wait_recv()
        part[...] = sg_ref[mine, :] + sib_buf[...]
        slots[chip] = part[...]
        to_chips = [_remote(part, slots.at[chip], send.at[1 + j], recv.at[1 + j], (fx, fy, c))
                    for j, (fx, fy) in enumerate(flips)]
        for cp in to_chips:
            cp.start()
        for j, (fx, fy) in enumerate(flips):
            _remote(part, slots.at[2 * fx + fy], send.at[1 + j], recv.at[1 + j], (fx, fy, c)).wait_recv()
        o_ref[mine, :] = ((slots[0] + slots[1]) + slots[2]) + slots[3]
        back = _remote(o_ref.at[mine, :], o_ref.at[mine, :], send.at[4], recv.at[4], sib)
        back.start()
        _remote(o_ref.at[other, :], o_ref.at[other, :], send.at[4], recv.at[4], sib).wait_recv()
        for cp in [to_sib, back] + to_chips:
            cp.wait_send()

    return pl.pallas_call(
        body, name="allreduce_small", in_specs=[VMEM_WHOLE], out_specs=VMEM_WHOLE, out_shape=SDS((rows, w), F32),
        scratch_shapes=[pltpu.VMEM((half, w), F32), pltpu.VMEM((half, w), F32), pltpu.VMEM((N_CHIPS, half, w), F32),
                        pltpu.SemaphoreType.DMA((5,)), pltpu.SemaphoreType.DMA((5,))])(sg)


def _adamw_math(g, w, m, v):
    mn = ADAM_B1 * m + (1.0 - ADAM_B1) * g
    vn = ADAM_B2 * v + (1.0 - ADAM_B2) * (g * g)
    m_hat = mn / (1.0 - ADAM_B1 ** ADAM_STEP)
    v_hat = vn / (1.0 - ADAM_B2 ** ADAM_STEP)
    return -ADAM_LR * (m_hat / (jnp.sqrt(v_hat) + ADAM_EPS) + ADAM_WD * w), mn, vn


def _adamw_small(gs, ws, ms, vs):
    n = len(gs)

    def body(*refs):
        for k in range(n):
            g, w, m, v = (refs[t * n + k][...] for t in range(4))
            refs[4 * n + k][...], refs[5 * n + k][...], refs[6 * n + k][...] = _adamw_math(g, w, m, v)

    out = pl.pallas_call(
        body, name="adamw_small", in_specs=[VMEM_WHOLE] * (4 * n), out_specs=[VMEM_WHOLE] * (3 * n),
        out_shape=[SDS(w.shape, F32) for w in ws] * 3)(*gs, *ws, *ms, *vs)
    return out[:n], out[n:2 * n], out[2 * n:]


def _adamw(g_arr, layer0, g_off, per_layer, w, m, v, *, name, tm, layers=None, prev=None):
    rows, cols = w.shape
    assert g_off % tm == 0 and per_layer % tm == 0 and rows % per_layer == 0
    npl = per_layer // tm
    lo, hi = layers or (0, rows // per_layer)
    base = lo * npl

    def body(g_ref, w_ref, m_ref, v_ref, *rest):
        go_ref, d_ref, mo_ref, vo_ref = rest[-4:]
        g = g_ref[0]
        go_ref[...] = g
        d_ref[...], mo_ref[...], vo_ref[...] = _adamw_math(g, w_ref[...], m_ref[...], v_ref[...])

    blk = pl.BlockSpec((tm, cols), lambda i: (base + i, 0))
    extra = list(prev or [])
    return _call(
        body, name, ((hi - lo) * npl,),
        [pl.BlockSpec((1, tm, cols), lambda i: (layer0 + lo + i // npl, g_off // tm + i % npl, 0)), blk, blk, blk]
        + [ANY] * len(extra),
        [blk] * 4,
        [SDS((rows, cols), F32)] * 4, semantics=("parallel",),
        aliases={4 + k: k for k in range(len(extra))})(g_arr, w, m, v, *extra)


def _pack_small(parts, width):
    flat = jnp.concatenate([p.reshape(-1).astype(F32) for p in parts])
    rows = -(-flat.shape[0] // (16 * width)) * 16
    return jnp.pad(flat, (0, rows * width - flat.shape[0])).reshape(rows, width)


def _unpack_small(packed, shapes):
    flat = packed.reshape(-1)
    out, off = [], 0
    for shp in shapes:
        size = 1
        for n in shp:
            size *= n
        out.append(flat[off:off + size].reshape(shp))
        off += size
    return out


def _block_diag(pw):
    g, c, _ = pw.shape
    eye = jnp.eye(g, dtype=pw.dtype)
    return (eye[:, None, :, None] * pw[:, :, None, :]).reshape(g * c, g * c)


def _diag_blocks(full, g):
    c = full.shape[0] // g
    return jnp.stack([full[i * c:(i + 1) * c, i * c:(i + 1) * c] for i in range(g)])


def kernel(x, mem, norm_mix, w_in, pool_w, pool_scale, kv_norm, w_kv, k_norm, q_norm, sinks, mem_norm, w_mem_kv, mem_q_norm, mem_k_norm, w_out, norm_mlp, w_up, w_down, loss_target, m_norm_mix, m_w_in, m_pool_w, m_pool_scale, m_kv_norm, m_w_kv, m_k_norm, m_q_norm, m_sinks, m_mem_norm, m_w_mem_kv, m_mem_q_norm, m_mem_k_norm, m_w_out, m_norm_mlp, m_w_up, m_w_down, v_norm_mix, v_w_in, v_pool_w, v_pool_scale, v_kv_norm, v_w_kv, v_k_norm, v_q_norm, v_sinks, v_mem_norm, v_w_mem_kv, v_mem_q_norm, v_mem_k_norm, v_w_out, v_norm_mlp, v_w_up, v_w_down):
    s_len, d = x.shape[1], x.shape[2]
    n_layers, n_pool = norm_mix.shape[0], pool_w.shape[0]
    n_swa = n_layers - n_pool
    main = d - KVW
    qh = main // HEAD
    ff = w_down.shape[1] * N_CHIPS
    dq = d // N_CHIPS
    assert w_up.shape[2] == d and ff == N_CHIPS * d and w_kv.shape[1] == 2 * KVW
    tm = min(512, s_len)
    tmb = min(1024, s_len)
    tm_mem = mem.shape[1]

    cx, cy, cc = lax.axis_index("x"), lax.axis_index("y"), lax.axis_index("c")
    chip = 2 * cx + cy
    place = jnp.stack([cc, chip]).astype(jnp.int32)

    off_down, off_up, off_in, off_out = 0, d, 2 * d, 2 * d + dq
    rows1 = off_out + dq
    off_mkv, off_kv = 0, dq
    rows2 = 2 * dq

    ps = jnp.pad(pool_scale, ((0, 8 - n_pool), (0, 2 * LANE - pool_scale.shape[1])))
    psg = _gather_small(ps)
    pool_scale_full = jnp.concatenate([psg[k, :n_pool, :pool_scale.shape[1]] for k in range(N_CHIPS)], axis=1)

    def packed_weights(l):
        p1 = jnp.concatenate([w_down[l], w_up[l], w_in[l], w_out[l]]).astype(BF)
        p2 = jnp.concatenate([w_mem_kv[l], w_kv] if l == n_pool else [w_mem_kv[l]]).astype(BF)
        return p1, p2

    def gather_start(l, after):
        p1, p2 = packed_weights(l)
        bufs = (p1, p2, lax.empty((N_CHIPS,) + p1.shape, BF), lax.empty((N_CHIPS,) + p2.shape, BF))
        return _split_start(_gather_copies, 6, bufs, after, None, name=f"gather_start_{l}")

    def gather_land(l, started, after):
        bufs = _split_wait(_gather_copies, started, after, name=f"gather_wait_{l}")
        return _split_start(_forward_copies, 8, bufs, place, None, name=f"forward_start_{l}")

    def gather_finish(l, forwarding, after):
        bufs = _split_wait(_forward_copies, forwarding, after, name=f"forward_wait_{l}")
        return bufs[2], bufs[3]

    def w_rows(arr, off, nrows, width):
        assert off % nrows == 0
        return (arr, (N_CHIPS, nrows, width), lambda j: (0, off // nrows, 0))

    row = lambda a: a.reshape(1, -1)
    h = x.reshape(s_len, d)
    memx = mem.reshape(tm_mem, d)
    tgt = loss_target.reshape(s_len, d)
    pbd = [_block_diag(pool_w[l]).astype(BF) for l in range(n_pool)]
    sinks_pad = [jnp.pad(row(sinks[j]), ((0, 0), (0, LANE - qh))) for j in range(n_swa)]

    w_in_l, w_out_l, w_down_l, w_up_all_l, w_mkv_l = [], [], [], [], []
    w_kv_g = None
    forwarding = gather_land(0, gather_start(0, psg), psg)
    travelling = gather_start(1, forwarding[6]) if n_layers > 1 else None
    saved, probs, sink_probs, mem_probs = [], {}, {}, {}
    kv = hn_kv = kn = vsh = None
    for l in range(n_layers):
        wg1, wg2 = gather_finish(l, forwarding, h if l else forwarding[6])
        w_in_l.append(w_rows(wg1, off_in, dq, d))
        w_out_l.append(w_rows(wg1, off_out, dq, d))
        w_down_l.append(w_rows(wg1, off_down, d, d))
        w_up_all_l.append((wg1, (N_CHIPS, d, d), lambda j: (0, off_up // d, 0)))
        w_mkv_l.append(w_rows(wg2, off_mkv, dq, 2 * KVW))
        g_mix = row(norm_mix[l])
        if travelling is not None:
            g_mix = g_mix + travelling[6][0, 0]
        if l == n_pool:
            w_kv_g = w_rows(wg2, off_kv, dq, 2 * KVW)
            kv, hn_kv = _norm_mm(h, row(kv_norm), w_kv_g, 1, 2 * KVW, act=False, name="kv_proj", tm=tmb)
            kn, vsh = _kv_prep(kv, row(k_norm), tm=tmb)
        h0 = h
        proj, xn = _norm_mm(h0, g_mix, w_in_l[l], 1, d, act=False, name=f"in_proj_{l}", tm=tmb)
        mkv, memn = _norm_mm(memx, row(mem_norm[l]), w_mkv_l[l], 1, 2 * KVW, act=False, name=f"mem_kv_{l}", tm=tm_mem)
        if l < n_pool:
            cat, mem_probs[l] = _mixer_pool_fwd(proj, mkv, pbd[l], row(pool_scale_full[l]), row(mem_q_norm[l]),
                                                row(mem_k_norm[l]), name=f"mixer_fwd_{l}", tm=tm)
        else:
            j = l - n_pool
            cat, probs[l], sink_probs[l] = _mixer_swa_fwd(proj, kn, vsh, row(q_norm[j]), sinks_pad[j],
                                                          name=f"mixer_fwd_{l}")
            cat, mem_probs[l] = _mem_attn_fwd(proj, mkv, row(mem_q_norm[l]), row(mem_k_norm[l]), cat,
                                              name=f"mem_attn_fwd_{l}", tm=tm)
        h1 = _mm_res(h0, cat, w_out_l[l], name=f"out_proj_{l}", tm=tmb)
        hh, xm = _norm_mm(h1, row(norm_mlp[l]), w_up_all_l[l], N_CHIPS, d, act=True, name=f"mlp_up_{l}", tm=tm)
        after = None
        if travelling is not None:
            forwarding = gather_land(l + 1, travelling, hh)
            travelling = gather_start(l + 2, forwarding[6]) if l + 2 < n_layers else None
            after = forwarding[6]
        h = _mm_res(h1, hh, w_down_l[l], name=f"mlp_down_{l}", tm=tm, after=after)
        saved.append((h0, proj, xn, mkv, memn, cat, h1, hh, xm))

    dh, dh_bf, loss_part = _loss_head(h, tgt, tm=tmb)

    half1, half2 = rows1 // 2, rows2 // 2
    g1 = lax.empty((N_CHIPS, rows1, d), F32)
    pending = {}
    swapping = None
    tk = min(512, d)

    def reduce_begin(l, swapped, after):
        g1_l, g2_l, r1, r2 = _split_wait(_swap_copies, swapped, after, name=f"swap_wait_{l}")
        pb1, own1 = _sum_sibling(g1_l, r1, place, tm=_tile(half1, 640), name=f"sum_sibling_a_{l}")
        pb2, own2 = _sum_sibling(g2_l, r2, place, tm=_tile(half2, 256), name=f"sum_sibling_b_{l}")
        bufs = (pb1, pb2, lax.empty((3, half1, d), BF), lax.empty((3, half2, 2 * KVW), BF))
        return _split_start(_rs_copies, 6, bufs, place, None, name=f"reduce_start_{l}"), own1, own2
    zeros_mem = jnp.zeros((tm_mem, d), F32)

    def rows_map(off, nrows, tkk):
        per = nrows // tkk
        return lambda i, j: (i // per, off // tkk + i % per, 0)

    def cols_map(off, tkk):
        return lambda i, j: (j, off // tkk + i, 0)

    d_norm_mix, d_norm_mlp, d_mem_norm = [None] * n_layers, [None] * n_layers, [None] * n_layers
    d_mem_q, d_mem_k = [None] * n_layers, [None] * n_layers
    d_pool_w, d_pool_scale = [None] * n_pool, [None] * n_pool
    d_q_norm, d_sinks = [None] * n_swa, [None] * n_swa
    dks, dvs = [], []
    d_kv_norm = d_k_norm = None
    for l in reversed(range(n_layers)):
        h0, proj, xn, mkv, memn, cat, h1, hh, xm = saved[l]
        g2 = jnp.zeros((N_CHIPS, rows2, 2 * KVW), F32)
        g1 = _mm_tn(hh, dh_bf, g1, rows_map(off_down, d, tk), tk, d, name=f"dw_down_{l}")
        du = _mm_nt_relu2(dh_bf, hh, w_down_l[l], N_CHIPS, name=f"d_mlp_act_{l}", tm=tm)
        g1 = _mm_tn(xm, du, g1, cols_map(off_up, tk), tk, d, name=f"dw_up_{l}")
        g_mlp = row(norm_mlp[l])
        if swapping is not None:
            pending[swapping[0]] = reduce_begin(*swapping, after=g1)
            g_mlp = g_mlp + pending[swapping[0]][0][6][0, 0]
        dh1, dh1_bf, d_norm_mlp[l] = _mm_nt_normbwd(du, w_up_all_l[l], N_CHIPS, h1, g_mlp, dh,
                                                    name=f"d_mlp_in_{l}", tm=tm)
        tkq = min(tk, dq)
        g1 = _mm_tn(cat, dh1_bf, g1, rows_map(off_out, dq, tkq), tkq, d, name=f"dw_out_{l}")
        dcat = _mm_nt(dh1_bf, w_out_l[l], d, name=f"d_cat_{l}", tm=tmb)
        if l < n_pool:
            dproj, dpbd, dscale, dmkv, d_mem_q[l], d_mem_k[l] = _mixer_pool_bwd(
                proj, dcat, mkv, pbd[l], row(pool_scale_full[l]), row(mem_q_norm[l]), row(mem_k_norm[l]),
                mem_probs[l], name=f"mixer_bwd_{l}", tm=tm)
            d_pool_w[l] = _diag_blocks(dpbd, len(POOL_WINDOWS))
            d_pool_scale[l] = dscale
        else:
            j = l - n_pool
            dproj, dk, dv, d_q_norm[j], dsk = _mixer_swa_bwd(proj, dcat, kn, vsh, row(q_norm[j]), probs[l],
                                                             sink_probs[l], name=f"mixer_bwd_{l}")
            dproj, dmkv, d_mem_q[l], d_mem_k[l] = _mem_attn_bwd(
                proj, dcat, mkv, row(mem_q_norm[l]), row(mem_k_norm[l]), mem_probs[l], dproj,
                name=f"mem_attn_bwd_{l}", tm=tm)
            d_sinks[j] = dsk[0, :qh]
            dks.append(dk)
            dvs.append(dv)
        g1 = _mm_tn(xn, dproj, g1, rows_map(off_in, dq, tkq), tkq, d, name=f"dw_in_{l}")
        dh, dh_bf, d_norm_mix[l] = _mm_nt_normbwd(dproj, w_in_l[l], 1, h0, row(norm_mix[l]), dh1,
                                                  name=f"d_in_{l}", tm=tmb)
        g2 = _mm_tn(memn, dmkv, g2, rows_map(off_mkv, dq, tkq), tkq, 2 * KVW, name=f"dw_mem_kv_{l}")
        _, _, d_mem_norm[l] = _mm_nt_normbwd(dmkv, w_mkv_l[l], 1, memx, row(mem_norm[l]), zeros_mem,
                                             name=f"d_mem_norm_{l}", tm=tm_mem)
        if l == n_pool:
            dkv, d_k_norm = _kv_bwd(kv, dks, dvs, row(k_norm), tm=tmb)
            g2 = _mm_tn(hn_kv, dkv, g2, rows_map(off_kv, dq, tkq), tkq, 2 * KVW, name="dw_kv")
            dh, dh_bf, d_kv_norm = _mm_nt_normbwd(dkv, w_kv_g, 1, h0, row(kv_norm), dh, name="d_kv_in", tm=tmb)
        bufs = (g1, g2, lax.empty((N_CHIPS, half1, d), F32), lax.empty((N_CHIPS, half2, 2 * KVW), F32))
        swapping = (l, _split_start(_swap_copies, 2, bufs, place, (N_CHIPS, rows1, d) if l > 0 else None,
                                    name=f"swap_start_{l}"))
        g1 = swapping[1][7] if l > 0 else None
    grad_x = dh.reshape(x.shape)

    small_names = ["norm_mix", "pool_w", "pool_scale", "kv_norm", "k_norm", "q_norm", "sinks", "mem_norm",
                   "mem_q_norm", "mem_k_norm", "norm_mlp"]
    small_grads = {
        "norm_mix": jnp.concatenate(d_norm_mix), "pool_w": jnp.stack(d_pool_w),
        "pool_scale": jnp.concatenate(d_pool_scale), "kv_norm": d_kv_norm[0], "k_norm": d_k_norm[0],
        "q_norm": jnp.concatenate(d_q_norm), "sinks": jnp.stack(d_sinks), "mem_norm": jnp.concatenate(d_mem_norm),
        "mem_q_norm": jnp.concatenate(d_mem_q), "mem_k_norm": jnp.concatenate(d_mem_k),
        "norm_mlp": jnp.concatenate(d_norm_mlp)}
    width = d
    sg = _pack_small([small_grads[n] for n in small_names] + [loss_part], width)
    sg = sg + swapping[1][6][0, 0]
    sg = _allreduce_small(sg)
    pending[0] = reduce_begin(*swapping, after=sg)
    *unpacked, loss_sum = _unpack_small(sg, [small_grads[n].shape for n in small_names] + [(1, 1)])
    loss = loss_sum[0, 0]
    reduced = dict(zip(small_names, unpacked))
    psw = pool_scale.shape[1]
    reduced["pool_scale"] = lax.dynamic_slice_in_dim(reduced["pool_scale"], chip * psw, psw, axis=1)
    params = dict(norm_mix=(norm_mix, m_norm_mix, v_norm_mix), pool_w=(pool_w, m_pool_w, v_pool_w),
                  pool_scale=(pool_scale, m_pool_scale, v_pool_scale), kv_norm=(kv_norm, m_kv_norm, v_kv_norm),
                  k_norm=(k_norm, m_k_norm, v_k_norm), q_norm=(q_norm, m_q_norm, v_q_norm),
                  sinks=(sinks, m_sinks, v_sinks), mem_norm=(mem_norm, m_mem_norm, v_mem_norm),
                  mem_q_norm=(mem_q_norm, m_mem_q_norm, v_mem_q_norm),
                  mem_k_norm=(mem_k_norm, m_mem_k_norm, v_mem_k_norm), norm_mlp=(norm_mlp, m_norm_mlp, v_norm_mlp))
    flat2 = lambda a: a.reshape(-1, a.shape[-1])
    grads = [flat2(reduced[n].reshape(params[n][0].shape)) for n in small_names]
    grads[0] = grads[0] + pending[0][0][6][0, 0]
    res = _adamw_small(grads, *[[flat2(params[n][t]) for n in small_names] for t in range(3)])
    small = {n: [grads[k].reshape(params[n][0].shape)] + [r[k].reshape(params[n][0].shape) for r in res]
             for k, n in enumerate(small_names)}

    full1 = lax.empty((n_layers, 2, half1, d), F32)
    full2 = lax.empty((n_layers, 2, half2, 2 * KVW), F32)
    kinds = (("w_down", 1, off_down, d, w_down, m_w_down, v_w_down), ("w_up", 1, off_up, d, w_up, m_w_up, v_w_up),
             ("w_in", 1, off_in, dq, w_in, m_w_in, v_w_in), ("w_out", 1, off_out, dq, w_out, m_w_out, v_w_out),
             ("w_mem_kv", 2, off_mkv, dq, w_mem_kv, m_w_mem_kv, v_w_mem_kv))
    big = {}
    first = 1 if n_layers > 1 and n_pool > 0 else 0
    after = res[0][0]
    for lo, hi in ((first, n_layers), (0, first)):
        if lo == hi:
            continue
        for l in reversed(range(lo, hi)):
            exchange, own1, own2 = pending[l]
            _, _, x1, x2 = _split_wait(_rs_copies, exchange, after, name=f"reduce_wait_{l}")
            full1 = _sum_chips(own1, x1, full1, l, place, tm=_tile(half1, 640), name=f"sum_chips_a_{l}")
            full2 = _sum_chips(own2, x2, full2, l, place, tm=_tile(half2, 256), name=f"sum_chips_b_{l}")
        full1, full2 = _share_with_sibling(full1, full2, lo, hi, name=f"share_with_sibling_{lo}")
        views = (None, full1.reshape(n_layers, rows1, d), full2.reshape(n_layers, rows2, 2 * KVW))
        for name, which, off, per, w_, m_, v_ in kinds:
            cols = views[which].shape[2]
            big[name] = _adamw(views[which], 0, off, per, w_.reshape(-1, cols), m_.reshape(-1, cols),
                               v_.reshape(-1, cols), name=f"adamw_{name}_{lo}", tm=min(512, per), layers=(lo, hi),
                               prev=big.get(name))
        if lo <= n_pool < hi:
            big["w_kv"] = _adamw(views[2], n_pool, off_kv, dq, w_kv, m_w_kv, v_w_kv, name="adamw_w_kv",
                                 tm=min(256, dq))
        after = big["w_mem_kv"][1]
    shapes_big = dict(w_down=w_down.shape, w_up=w_up.shape, w_in=w_in.shape, w_out=w_out.shape,
                      w_mem_kv=w_mem_kv.shape, w_kv=w_kv.shape)
    big = {n: [r.reshape(shapes_big[n]) for r in big[n]] for n in big}

    order = ["norm_mix", "w_in", "pool_w", "pool_scale", "kv_norm", "w_kv", "k_norm", "q_norm", "sinks", "mem_norm",
             "w_mem_kv", "mem_q_norm", "mem_k_norm", "w_out", "norm_mlp", "w_up", "w_down"]
    out = {**big, **small}
    return (loss, grad_x, *[out[n][0] for n in order], *[out[n][1] for n in order],
            *[out[n][2] for n in order], *[out[n][3] for n in order])
```

```python
import functools

import jax
import jax.numpy as jnp
from jax import lax
from jax.experimental import pallas as pl
from jax.experimental.pallas import tpu as pltpu

F32, BF = jnp.float32, jnp.bfloat16
SDS = jax.ShapeDtypeStruct
MESH = pl.DeviceIdType.MESH
ANY = pl.BlockSpec(memory_space=pl.ANY)
HBM = pl.BlockSpec(memory_space=pltpu.HBM)
SEM = pl.BlockSpec(memory_space=pltpu.SEMAPHORE)
VMEM_WHOLE = pl.BlockSpec(memory_space=pltpu.VMEM)
SIDE_EFFECT = pltpu.SideEffectType.DATAFLOW_SIDE_EFFECTING


def _in_hbm(a):
    return pltpu.with_memory_space_constraint(a, pltpu.HBM)


EPS = 1e-6
HEAD = 64
KV_HEADS = 4
KVW = KV_HEADS * HEAD
WINDOW = 128
POOL_WINDOWS = (2, 4, 8, 16)
HALO = 16
QK_SCALE = HEAD ** -0.5
NEG = float(jnp.finfo(jnp.float32).min)
N_CHIPS = 4
LANE = 128

ADAM_LR, ADAM_B1, ADAM_B2, ADAM_EPS, ADAM_WD, ADAM_STEP = 0.001, 0.9, 0.999, 1e-08, 0.01, 10

VMEM_LIMIT_MB = 56


def _call(body, name, grid, in_specs, out_specs, out_shape, *, scratch=(), semantics=None, aliases=None,
          prefetch=0):
    params = pltpu.CompilerParams(dimension_semantics=semantics, vmem_limit_bytes=VMEM_LIMIT_MB << 20)
    if prefetch:
        spec = pltpu.PrefetchScalarGridSpec(num_scalar_prefetch=prefetch, grid=grid, in_specs=in_specs,
                                            out_specs=out_specs, scratch_shapes=list(scratch))
        return pl.pallas_call(body, name=name, grid_spec=spec, out_shape=out_shape,
                              input_output_aliases=aliases or {}, compiler_params=params)
    return pl.pallas_call(body, name=name, grid=grid, in_specs=in_specs, out_specs=out_specs, out_shape=out_shape,
                          scratch_shapes=list(scratch), input_output_aliases=aliases or {}, compiler_params=params)


def _tile(n, pref):
    return max(t for t in range(8, min(n, pref) + 1, 8) if n % t == 0)


def _dot(a, b):
    return jnp.dot(a, b, preferred_element_type=F32)


def _dot_nt(a, b):
    return lax.dot_general(a, b, (((1,), (1,)), ((), ())), preferred_element_type=F32)


def _dot_tn(a, b):
    return lax.dot_general(a, b, (((0,), (0,)), ((), ())), preferred_element_type=F32)


def _rms(x):
    r = lax.rsqrt(jnp.mean(x * x, axis=-1, keepdims=True) + EPS)
    return x * r, r


def _rms_bwd(dy, xh, r, g):
    dg = jnp.sum(dy * xh, axis=0, keepdims=True)
    dyg = dy * g
    dx = r * (dyg - xh * jnp.mean(dyg * xh, axis=-1, keepdims=True))
    return dx, dg


def _norm_mm(h, g, w, nj, tn, *, act, name, tm):
    w_arr, w_block, w_imap = w
    rows, d = h.shape

    def body(h_ref, g_ref, w_ref, y_ref, xn_ref):
        xh, _ = _rms(h_ref[...])
        xn = (xh * g_ref[...]).astype(BF)
        xn_ref[...] = xn
        for j in range(nj):
            u = _dot(xn, w_ref[j] if nj > 1 else w_ref[...].reshape(d, tn))
            if act:
                a = jnp.maximum(u, 0.0)
                y_ref[:, j * tn:(j + 1) * tn] = (a * a).astype(BF)
            else:
                y_ref[:, j * tn:(j + 1) * tn] = u

    assert nj == 1 or w_block[0] == nj
    return _call(
        body, name, (rows // tm,),
        [pl.BlockSpec((tm, d), lambda i: (i, 0)), pl.BlockSpec((1, d), lambda i: (0, 0)),
         pl.BlockSpec(w_block, lambda i: w_imap(0))],
        [pl.BlockSpec((tm, nj * tn), lambda i: (i, 0)), pl.BlockSpec((tm, d), lambda i: (i, 0))],
        [SDS((rows, nj * tn), BF if act else F32), SDS((rows, d), BF)],
        semantics=("parallel",))(h, g, w_arr)


def _mm_res(res, a, w, *, name, tm, after=None):
    w_arr, w_block, w_imap = w
    rows, k = a.shape
    n = res.shape[1]

    def body(res_ref, a_ref, w_ref, *rest):
        rest[-1][...] = res_ref[...] + _dot(a_ref[...], w_ref[...].reshape(k, n))

    extra = [] if after is None else [after]
    return _call(
        body, name, (rows // tm,),
        [pl.BlockSpec((tm, n), lambda i: (i, 0)), pl.BlockSpec((tm, k), lambda i: (i, 0)),
         pl.BlockSpec(w_block, lambda i: w_imap(0))] + [ANY] * len(extra),
        pl.BlockSpec((tm, n), lambda i: (i, 0)), SDS((rows, n), F32), semantics=("parallel",))(res, a, w_arr, *extra)


def _mm_nt(dy, w, k, *, name, tm):
    w_arr, w_block, w_imap = w
    rows, n = dy.shape

    def body(dy_ref, w_ref, o_ref):
        o_ref[...] = _dot_nt(dy_ref[...], w_ref[...].reshape(k, n))

    return _call(
        body, name, (rows // tm,),
        [pl.BlockSpec((tm, n), lambda i: (i, 0)), pl.BlockSpec(w_block, lambda i: w_imap(0))],
        pl.BlockSpec((tm, k), lambda i: (i, 0)), SDS((rows, k), F32), semantics=("parallel",))(dy, w_arr)


def _mm_nt_relu2(dh, hh, w, nj, *, name, tm):
    w_arr, w_block, w_imap = w
    rows, d = dh.shape
    tk = hh.shape[1] // nj

    def body(dh_ref, hh_ref, w_ref, o_ref):
        dh_t = dh_ref[...]
        for j in range(nj):
            cols = slice(j * tk, (j + 1) * tk)
            dhh = _dot_nt(dh_t, w_ref[j])
            o_ref[:, cols] = (dhh * (2.0 * jnp.sqrt(hh_ref[:, cols].astype(F32)))).astype(BF)

    assert w_block[0] == nj
    return _call(
        body, name, (rows // tm,),
        [pl.BlockSpec((tm, d), lambda i: (i, 0)), pl.BlockSpec((tm, nj * tk), lambda i: (i, 0)),
         pl.BlockSpec(w_block, lambda i: w_imap(0))],
        pl.BlockSpec((tm, nj * tk), lambda i: (i, 0)), SDS((rows, nj * tk), BF),
        semantics=("parallel",))(dh, hh, w_arr)


def _mm_nt_normbwd(dy, w, nsplit, h, g, dres, *, name, tm):
    w_arr, w_block, w_imap = w
    rows, n = dy.shape
    d = h.shape[1]
    ns = n // nsplit

    def body(dy_ref, w_ref, h_ref, g_ref, dres_ref, o_ref, obf_ref, dg_ref):
        if nsplit == 1:
            dxn = _dot_nt(dy_ref[...].astype(BF), w_ref[...].reshape(d, n))
        else:
            dxn = _dot_nt(dy_ref[:, 0:ns].astype(BF), w_ref[0])
            for s in range(1, nsplit):
                dxn += _dot_nt(dy_ref[:, s * ns:(s + 1) * ns].astype(BF), w_ref[s])
        xh, r = _rms(h_ref[...])
        dx, dg = _rms_bwd(dxn, xh, r, g_ref[...])
        out = dres_ref[...] + dx
        o_ref[...] = out
        obf_ref[...] = out.astype(BF)

        @pl.when(pl.program_id(0) == 0)
        def _():
            dg_ref[...] = jnp.zeros_like(dg_ref)

        dg_ref[...] += dg

    row = lambda i: (i, 0)
    return _call(
        body, name, (rows // tm,),
        [pl.BlockSpec((tm, n), row), pl.BlockSpec(w_block, lambda i: w_imap(0)), pl.BlockSpec((tm, d), row),
         pl.BlockSpec((1, d), lambda i: (0, 0)), pl.BlockSpec((tm, d), row)],
        [pl.BlockSpec((tm, d), row), pl.BlockSpec((tm, d), row), pl.BlockSpec((1, d), lambda i: (0, 0))],
        [SDS((rows, d), F32), SDS((rows, d), BF), SDS((1, d), F32)],
        semantics=("arbitrary",))(dy, w_arr, h, g, dres)


def _mm_tn(x, dy, packed, out_imap, tk, tn, *, name):
    s_len, k = x.shape
    n = dy.shape[1]

    def body(x_ref, dy_ref, _, o_ref):
        o_ref[0] = _dot_tn(x_ref[...], dy_ref[...].astype(BF))

    return _call(
        body, name, (k // tk, n // tn),
        [pl.BlockSpec((s_len, tk), lambda i, j: (0, i)), pl.BlockSpec((s_len, tn), lambda i, j: (0, j)), ANY],
        pl.BlockSpec((1, tk, tn), out_imap), SDS(packed.shape, packed.dtype),
        semantics=("parallel", "parallel"), aliases={2: 0})(x, dy, packed)


def _loss_head(y, tgt, *, tm):
    rows, d = y.shape

    def body(y_ref, t_ref, dh_ref, dhbf_ref, loss_ref):
        err = y_ref[...] - t_ref[...]
        dh = err * (1.0 / d)
        dh_ref[...] = dh
        dhbf_ref[...] = dh.astype(BF)

        @pl.when(pl.program_id(0) == 0)
        def _():
            loss_ref[...] = jnp.zeros_like(loss_ref)

        loss_ref[...] += 0.5 * jnp.sum(jnp.mean(err * err, axis=-1, keepdims=True), axis=0, keepdims=True)

    row = lambda i: (i, 0)
    return _call(
        body, "loss_head", (rows // tm,), [pl.BlockSpec((tm, d), row), pl.BlockSpec((tm, d), row)],
        [pl.BlockSpec((tm, d), row), pl.BlockSpec((tm, d), row), pl.BlockSpec((1, 1), lambda i: (0, 0))],
        [SDS((rows, d), F32), SDS((rows, d), BF), SDS((1, 1), F32)], semantics=("arbitrary",))(y, tgt)


def _hs(h):
    return slice(HEAD * h, HEAD * (h + 1))


def _softmax_rows(s):
    e = jnp.exp(s - jnp.max(s, axis=-1, keepdims=True))
    return e * (1.0 / jnp.sum(e, axis=-1, keepdims=True))


def _scaled_bf16(qn):
    return (qn * QK_SCALE).astype(BF)


def _mem_fwd(mq, mk, mv, gq):
    outs, probs = [], []
    for h in range(KV_HEADS):
        xh, _ = _rms(mq[:, _hs(h)])
        p = _softmax_rows(_dot_nt(_scaled_bf16(xh * gq), mk[:, _hs(h)])).astype(BF)
        probs.append(p)
        outs.append(_dot(p, mv[:, _hs(h)]))
    return jnp.concatenate(outs, axis=-1), jnp.concatenate(probs, axis=-1)


def _mem_bwd(mq, do, mk, mv, gq, probs):
    dqs, dks, dvs, dgq = [], [], [], 0.0
    mlen = mk.shape[0]
    for h in range(KV_HEADS):
        xh, r = _rms(mq[:, _hs(h)])
        qn = _scaled_bf16(xh * gq)
        p_bf = probs[:, h * mlen:(h + 1) * mlen]
        p = p_bf.astype(F32)
        doh = do[:, _hs(h)].astype(BF)
        dp = _dot_nt(doh, mv[:, _hs(h)])
        ds = (p * (dp - jnp.sum(p * dp, axis=-1, keepdims=True))).astype(BF)
        dq, dg = _rms_bwd(_dot(ds, mk[:, _hs(h)]) * QK_SCALE, xh, r, gq)
        dqs.append(dq)
        dgq = dgq + dg
        dks.append(_dot_tn(ds, qn))
        dvs.append(_dot_tn(p_bf, doh))
    cat = lambda xs: jnp.concatenate(xs, axis=-1)
    return cat(dqs), cat(dks), cat(dvs), dgq


def _mem_kv(mkv, gk):
    ks = []
    for h in range(KV_HEADS):
        xh, _ = _rms(mkv[:, _hs(h)])
        ks.append(xh * gk)
    return jnp.concatenate(ks, axis=-1).astype(BF), mkv[:, KVW:].astype(BF)


def _mem_kv_bwd(mkv, dmk, dmv, gk):
    dxs, dgk = [], 0.0
    for h in range(KV_HEADS):
        xh, r = _rms(mkv[:, _hs(h)])
        dx, dg = _rms_bwd(dmk[:, _hs(h)], xh, r, gk)
        dxs.append(dx)
        dgk = dgk + dg
    return jnp.concatenate(dxs + [dmv], axis=-1), dgk


def _pool_select(col, gd, a2, a4, a8, a16):
    return jnp.where(col < gd, a2, jnp.where(col < 2 * gd, a4, jnp.where(col < 3 * gd, a8, a16)))


def _pool_count(t0, shape, gd):
    col = lax.broadcasted_iota(jnp.int32, shape, 1)
    t = t0 + lax.broadcasted_iota(jnp.int32, shape, 0)
    win = _pool_select(col, gd, *POOL_WINDOWS)
    return jnp.minimum(t + 1, win).astype(F32)


def _pool_diff(u, halo, t0, gd):
    c = jnp.concatenate([halo, u], axis=0)
    s2 = c + pltpu.roll(c, 1, 0)
    s4 = s2 + pltpu.roll(s2, 2, 0)
    s8 = s4 + pltpu.roll(s4, 4, 0)
    s16 = s8 + pltpu.roll(s8, 8, 0)
    col = lax.broadcasted_iota(jnp.int32, c.shape, 1)
    ws = _pool_select(col, gd, s2, s4, s8, s16)[HALO:]
    return ws / _pool_count(t0, u.shape, gd) - u


def _pool_diff_bwd(dd, dd_halo, t0, gd):
    t = dd.shape[0]
    z = jnp.concatenate([dd / _pool_count(t0, dd.shape, gd), dd_halo / _pool_count(t0 + t, dd_halo.shape, gd)], axis=0)
    n = z.shape[0]
    f2 = z + pltpu.roll(z, n - 1, 0)
    f4 = f2 + pltpu.roll(f2, n - 2, 0)
    f8 = f4 + pltpu.roll(f4, n - 4, 0)
    f16 = f8 + pltpu.roll(f8, n - 8, 0)
    col = lax.broadcasted_iota(jnp.int32, z.shape, 1)
    return _pool_select(col, gd, f2, f4, f8, f16)[:t] - dd


def _swa_bias(n):
    qi = lax.broadcasted_iota(jnp.int32, (WINDOW, 2 * WINDOW), 0)
    kj = lax.broadcasted_iota(jnp.int32, (WINDOW, 2 * WINDOW), 1)
    dist = qi + WINDOW - kj
    valid = (dist >= 0) & (dist < WINDOW) & ((kj >= WINDOW) | (n > 0))
    return dist.astype(F32), valid


def _slopes(qh):
    return [2.0 ** (-8.0 * (h + 1) / qh) for h in range(qh)]


def _swa_probs(qn, kk, dist, valid, slope, sink):
    s = _dot_nt(qn, kk) - slope * dist
    s = jnp.where(valid, s, NEG)
    m = jnp.maximum(jnp.max(s, axis=-1, keepdims=True), sink)
    e = jnp.exp(s - m)
    es = jnp.exp(sink - m)
    z = jnp.sum(e, axis=-1, keepdims=True) + es
    inv = 1.0 / z
    return e * inv, es * inv


def _stack_heads(a, kh, grp):
    return jnp.concatenate([a[:, _hs(h)] for h in range(kh * grp, (kh + 1) * grp)], axis=0)


def _swa_group(q, kh, grp, n, qh, sinks):
    heads = range(kh * grp, (kh + 1) * grp)
    dist, valid = _swa_bias(n)
    slopes = _slopes(qh)
    rows = lambda vals: jnp.concatenate([jnp.broadcast_to(v, (WINDOW, 1)) for v in vals], axis=0)
    slope = rows([jnp.full((1, 1), slopes[h], F32) for h in heads])
    sink = rows([sinks[:, h:h + 1] for h in heads])
    return (_stack_heads(q, kh, grp), slope, sink, jnp.concatenate([dist] * grp, axis=0),
            jnp.concatenate([valid] * grp, axis=0))


def _swa_fwd(q, kk, vv, gq, sinks, n, qh):
    grp = qh // KV_HEADS
    lane = lax.broadcasted_iota(jnp.int32, (WINDOW, LANE), 1)
    outs, probs, sink_probs = [], [], jnp.zeros((WINDOW, LANE), F32)
    for kh in range(KV_HEADS):
        qs, slope, sink, dist, valid = _swa_group(q, kh, grp, n, qh, sinks)
        xh, _ = _rms(qs)
        p, ps = _swa_probs(_scaled_bf16(xh * gq), kk[:, _hs(kh)], dist, valid, slope, sink)
        p = p.astype(BF)
        probs.append(p)
        o = _dot(p, vv[:, _hs(kh)])
        for g in range(grp):
            outs.append(o[g * WINDOW:(g + 1) * WINDOW])
            sink_probs = jnp.where(lane == kh * grp + g, ps[g * WINDOW:(g + 1) * WINDOW], sink_probs)
    return jnp.concatenate(outs, axis=-1), probs, sink_probs


def _swa_bwd(q, do, kk, vv, gq, probs, sink_probs, qh):
    grp = qh // KV_HEADS
    lane = lax.broadcasted_iota(jnp.int32, (1, LANE), 1)
    dqs, dks, dvs, dgq, dsk = [], [], [], 0.0, jnp.zeros((1, LANE), F32)
    do = do.astype(BF)
    for kh in range(KV_HEADS):
        heads = range(kh * grp, (kh + 1) * grp)
        xh, r = _rms(_stack_heads(q, kh, grp))
        qn = _scaled_bf16(xh * gq)
        p_bf = probs[kh]
        p = p_bf.astype(F32)
        ps = jnp.concatenate([sink_probs[:, h:h + 1] for h in heads], axis=0)
        dos = _stack_heads(do, kh, grp)
        dp = _dot_nt(dos, vv[:, _hs(kh)])
        delta = jnp.sum(p * dp, axis=-1, keepdims=True)
        ds = (p * (dp - delta)).astype(BF)
        dsink = ps * delta
        for g in range(grp):
            part = -jnp.sum(dsink[g * WINDOW:(g + 1) * WINDOW], axis=0, keepdims=True)
            dsk = dsk + jnp.where(lane == kh * grp + g, part, 0.0)
        dq, dg = _rms_bwd(_dot(ds, kk[:, _hs(kh)]) * QK_SCALE, xh, r, gq)
        dqs += [dq[g * WINDOW:(g + 1) * WINDOW] for g in range(grp)]
        dgq = dgq + dg
        dks.append(_dot_tn(ds, qn))
        dvs.append(_dot_tn(p_bf, dos))
    cat = lambda xs: jnp.concatenate(xs, axis=-1)
    return cat(dqs), cat(dks), cat(dvs), dgq, dsk


def _mixer_pool_fwd(proj, mkv, pbd, scale, gq, gk, *, name, tm):
    s_len, d = proj.shape
    main = d - KVW
    gd = main // len(POOL_WINDOWS)
    mlen = mkv.shape[0]
    hb = tm // HALO

    def body(u_ref, halo_ref, mq_ref, mkv_ref, pbd_ref, scale_ref, gq_ref, gk_ref, o_ref, mp_ref, mk_s, mv_s):
        i = pl.program_id(0)

        @pl.when(i == 0)
        def _():
            mk, mv = _mem_kv(mkv_ref[...], gk_ref[...])
            mk_s[...] = mk
            mv_s[...] = mv

        halo = jnp.where(i > 0, halo_ref[...], 0.0)
        dif = _pool_diff(u_ref[...], halo, i * tm, gd)
        mixed = _dot(dif.astype(BF), pbd_ref[...]) * scale_ref[...]
        mem, mp_ref[...] = _mem_fwd(mq_ref[...], mk_s[...], mv_s[...], gq_ref[...])
        o_ref[...] = jnp.concatenate([mixed, mem], axis=-1).astype(BF)

    full = lambda shape: pl.BlockSpec(shape, lambda i: (0,) * len(shape))
    return _call(
        body, name, (s_len // tm,),
        [pl.BlockSpec((tm, main), lambda i: (i, 0)),
         pl.BlockSpec((HALO, main), lambda i: (jnp.maximum(i * hb - 1, 0), 0)),
         pl.BlockSpec((tm, KVW), lambda i: (i, main // KVW)),
         full((mlen, 2 * KVW)), full((main, main)), full((1, main)), full((1, HEAD)), full((1, HEAD))],
        [pl.BlockSpec((tm, d), lambda i: (i, 0)), pl.BlockSpec((tm, KV_HEADS * mlen), lambda i: (i, 0))],
        [SDS((s_len, d), BF), SDS((s_len, KV_HEADS * mlen), BF)],
        scratch=[pltpu.VMEM((mlen, KVW), BF), pltpu.VMEM((mlen, KVW), BF)],
        semantics=("arbitrary",))(proj, proj, proj, mkv, pbd, scale, gq, gk)


def _mixer_pool_bwd(proj, dcat, mkv, pbd, scale, gq, gk, mem_probs, *, name, tm):
    s_len, d = proj.shape
    main = d - KVW
    gd = main // len(POOL_WINDOWS)
    mlen = mkv.shape[0]
    hb = tm // HALO
    nt = s_len // tm
    last_halo = s_len // HALO - 1

    def body(u_ref, halo_ref, mq_ref, do_ref, donext_ref, dom_ref, mkv_ref, pbd_ref, scale_ref, gq_ref, gk_ref, mp_ref,
             dproj_ref, dpbd_ref, dscale_ref, dmkv_ref, dgq_ref, dgk_ref, mk_s, mv_s, dmk_s, dmv_s):
        i = pl.program_id(0)

        @pl.when(i == 0)
        def _():
            mk, mv = _mem_kv(mkv_ref[...], gk_ref[...])
            mk_s[...] = mk
            mv_s[...] = mv
            dmk_s[...] = jnp.zeros_like(dmk_s)
            dmv_s[...] = jnp.zeros_like(dmv_s)
            dpbd_ref[...] = jnp.zeros_like(dpbd_ref)
            dscale_ref[...] = jnp.zeros_like(dscale_ref)
            dgq_ref[...] = jnp.zeros_like(dgq_ref)

        pbd = pbd_ref[...]
        scale = scale_ref[...]
        halo = jnp.where(i > 0, halo_ref[...], 0.0)
        dif = _pool_diff(u_ref[...], halo, i * tm, gd).astype(BF)
        do = do_ref[...]
        dscale_ref[...] += jnp.sum(do * _dot(dif, pbd), axis=0, keepdims=True)
        dmixed = (do * scale).astype(BF)
        dpbd_ref[...] += _dot_tn(dif, dmixed)
        dd = _dot_nt(dmixed, pbd)
        donext = jnp.where(i < nt - 1, donext_ref[...], 0.0)
        dd_halo = _dot_nt((donext * scale).astype(BF), pbd)
        du = _pool_diff_bwd(dd, dd_halo, i * tm, gd)

        dmq, dmk, dmv, dgq = _mem_bwd(mq_ref[...], dom_ref[...], mk_s[...], mv_s[...], gq_ref[...], mp_ref[...])
        dmk_s[...] += dmk
        dmv_s[...] += dmv
        dgq_ref[...] += dgq
        dproj_ref[...] = jnp.concatenate([du, dmq], axis=-1).astype(BF)

        @pl.when(i == nt - 1)
        def _():
            dmkv, dgk = _mem_kv_bwd(mkv_ref[...], dmk_s[...], dmv_s[...], gk_ref[...])
            dmkv_ref[...] = dmkv
            dgk_ref[...] = dgk

    full = lambda shape: pl.BlockSpec(shape, lambda i: (0,) * len(shape))
    return _call(
        body, name, (nt,),
        [pl.BlockSpec((tm, main), lambda i: (i, 0)),
         pl.BlockSpec((HALO, main), lambda i: (jnp.maximum(i * hb - 1, 0), 0)),
         pl.BlockSpec((tm, KVW), lambda i: (i, main // KVW)),
         pl.BlockSpec((tm, main), lambda i: (i, 0)),
         pl.BlockSpec((HALO, main), lambda i: (jnp.minimum((i + 1) * hb, last_halo), 0)),
         pl.BlockSpec((tm, KVW), lambda i: (i, main // KVW)),
         full((mlen, 2 * KVW)), full((main, main)), full((1, main)), full((1, HEAD)), full((1, HEAD)),
         pl.BlockSpec((tm, KV_HEADS * mlen), lambda i: (i, 0))],
        [pl.BlockSpec((tm, d), lambda i: (i, 0)), full((main, main)), full((1, main)), full((mlen, 2 * KVW)),
         full((1, HEAD)), full((1, HEAD))],
        [SDS((s_len, d), BF), SDS((main, main), F32), SDS((1, main), F32), SDS((mlen, 2 * KVW), F32),
         SDS((1, HEAD), F32), SDS((1, HEAD), F32)],
        scratch=[pltpu.VMEM((mlen, KVW), BF), pltpu.VMEM((mlen, KVW), BF), pltpu.VMEM((mlen, KVW), F32),
                 pltpu.VMEM((mlen, KVW), F32)],
        semantics=("arbitrary",))(proj, proj, proj, dcat, dcat, dcat, mkv, pbd, scale, gq, gk, mem_probs)


def _mem_attn_fwd(proj, mkv, gq, gk, cat, *, name, tm):
    s_len, d = proj.shape
    main = d - KVW
    mlen = mkv.shape[0]

    def body(mq_ref, mkv_ref, gq_ref, gk_ref, _, o_ref, mp_ref, mk_s, mv_s):
        @pl.when(pl.program_id(0) == 0)
        def _():
            mk, mv = _mem_kv(mkv_ref[...], gk_ref[...])
            mk_s[...] = mk
            mv_s[...] = mv

        mem, mp_ref[...] = _mem_fwd(mq_ref[...], mk_s[...], mv_s[...], gq_ref[...])
        o_ref[...] = mem.astype(BF)

    full = lambda shape: pl.BlockSpec(shape, lambda i: (0,) * len(shape))
    memcol = lambda i: (i, main // KVW)
    return _call(
        body, name, (s_len // tm,),
        [pl.BlockSpec((tm, KVW), memcol), full((mlen, 2 * KVW)), full((1, HEAD)), full((1, HEAD)), ANY],
        [pl.BlockSpec((tm, KVW), memcol), pl.BlockSpec((tm, KV_HEADS * mlen), lambda i: (i, 0))],
        [SDS((s_len, d), BF), SDS((s_len, KV_HEADS * mlen), BF)],
        scratch=[pltpu.VMEM((mlen, KVW), BF), pltpu.VMEM((mlen, KVW), BF)],
        semantics=("arbitrary",), aliases={4: 0})(proj, mkv, gq, gk, cat)


def _mem_attn_bwd(proj, dcat, mkv, gq, gk, mem_probs, dproj, *, name, tm):
    s_len, d = proj.shape
    main = d - KVW
    mlen = mkv.shape[0]
    nt = s_len // tm

    def body(mq_ref, dom_ref, mkv_ref, gq_ref, gk_ref, mp_ref, _, dproj_ref, dmkv_ref, dgq_ref, dgk_ref,
             mk_s, mv_s, dmk_s, dmv_s):
        i = pl.program_id(0)

        @pl.when(i == 0)
        def _():
            mk, mv = _mem_kv(mkv_ref[...], gk_ref[...])
            mk_s[...] = mk
            mv_s[...] = mv
            dmk_s[...] = jnp.zeros_like(dmk_s)
            dmv_s[...] = jnp.zeros_like(dmv_s)
            dgq_ref[...] = jnp.zeros_like(dgq_ref)

        dmq, dmk, dmv, dgq = _mem_bwd(mq_ref[...], dom_ref[...], mk_s[...], mv_s[...], gq_ref[...], mp_ref[...])
        dmk_s[...] += dmk
        dmv_s[...] += dmv
        dgq_ref[...] += dgq
        dproj_ref[...] = dmq.astype(BF)

        @pl.when(i == nt - 1)
        def _():
            dmkv, dgk = _mem_kv_bwd(mkv_ref[...], dmk_s[...], dmv_s[...], gk_ref[...])
            dmkv_ref[...] = dmkv
            dgk_ref[...] = dgk

    full = lambda shape: pl.BlockSpec(shape, lambda i: (0,) * len(shape))
    memcol = lambda i: (i, main // KVW)
    return _call(
        body, name, (nt,),
        [pl.BlockSpec((tm, KVW), memcol), pl.BlockSpec((tm, KVW), memcol), full((mlen, 2 * KVW)), full((1, HEAD)),
         full((1, HEAD)), pl.BlockSpec((tm, KV_HEADS * mlen), lambda i: (i, 0)), ANY],
        [pl.BlockSpec((tm, KVW), memcol), full((mlen, 2 * KVW)), full((1, HEAD)), full((1, HEAD))],
        [SDS((s_len, d), BF), SDS((mlen, 2 * KVW), F32), SDS((1, HEAD), F32), SDS((1, HEAD), F32)],
        scratch=[pltpu.VMEM((mlen, KVW), BF), pltpu.VMEM((mlen, KVW), BF), pltpu.VMEM((mlen, KVW), F32),
                 pltpu.VMEM((mlen, KVW), F32)],
        semantics=("arbitrary",), aliases={6: 0})(proj, dcat, mkv, gq, gk, mem_probs, dproj)


def _mixer_swa_fwd(proj, kn, v, gqs, sinks, *, name):
    s_len, d = proj.shape
    main = d - KVW
    qh = main // HEAD
    tm = WINDOW
    prow = qh // KV_HEADS * tm

    def body(q_ref, kp_ref, kc_ref, vp_ref, vc_ref, gqs_ref, sinks_ref, o_ref, p_ref, ps_ref):
        n = pl.program_id(0)
        kk = jnp.concatenate([kp_ref[...], kc_ref[...]], axis=0)
        vv = jnp.concatenate([vp_ref[...], vc_ref[...]], axis=0)
        att, probs, sink_probs = _swa_fwd(q_ref[...], kk, vv, gqs_ref[...], sinks_ref[...], n, qh)
        for kh in range(KV_HEADS):
            p_ref[0, kh] = probs[kh]
        ps_ref[...] = sink_probs
        o_ref[...] = att.astype(BF)

    full = lambda shape: pl.BlockSpec(shape, lambda i: (0,) * len(shape))
    prev = lambda i: (jnp.maximum(i - 1, 0), 0)
    cur = lambda i: (i, 0)
    return _call(
        body, name, (s_len // tm,),
        [pl.BlockSpec((tm, main), cur), pl.BlockSpec((tm, KVW), prev), pl.BlockSpec((tm, KVW), cur),
         pl.BlockSpec((tm, KVW), prev), pl.BlockSpec((tm, KVW), cur), full((1, HEAD)), full((1, LANE))],
        [pl.BlockSpec((tm, main), cur), pl.BlockSpec((1, KV_HEADS, prow, 2 * tm), lambda i: (i, 0, 0, 0)),
         pl.BlockSpec((tm, LANE), cur)],
        [SDS((s_len, d), BF), SDS((s_len // tm, KV_HEADS, prow, 2 * tm), BF), SDS((s_len, LANE), F32)],
        semantics=("parallel",))(proj, kn, kn, v, v, gqs, sinks)


def _mixer_swa_bwd(proj, dcat, kn, v, gqs, probs, sink_probs, *, name):
    s_len, d = proj.shape
    main = d - KVW
    qh = main // HEAD
    tm = WINDOW
    nt = s_len // tm
    prow = qh // KV_HEADS * tm

    def body(q_ref, do_ref, kp_ref, kc_ref, vp_ref, vc_ref, gqs_ref, p_ref, ps_ref,
             dproj_ref, dk_ref, dv_ref, dgqs_ref, dsinks_ref):
        n = pl.program_id(0)

        @pl.when(n == 0)
        def _():
            dk_ref[...] = jnp.zeros_like(dk_ref)
            dv_ref[...] = jnp.zeros_like(dv_ref)
            dgqs_ref[...] = jnp.zeros_like(dgqs_ref)
            dsinks_ref[...] = jnp.zeros_like(dsinks_ref)

        kk = jnp.concatenate([kp_ref[...], kc_ref[...]], axis=0)
        vv = jnp.concatenate([vp_ref[...], vc_ref[...]], axis=0)
        dq, dkk, dvv, dgqs, dsk = _swa_bwd(q_ref[...], do_ref[...], kk, vv, gqs_ref[...],
                                           [p_ref[0, kh] for kh in range(KV_HEADS)], ps_ref[...], qh)
        prev = pl.ds(pl.multiple_of(jnp.maximum(n - 1, 0) * tm, tm), tm)
        own = pl.ds(pl.multiple_of(n * tm, tm), tm)
        dk_ref[prev, :] += dkk[:tm]
        dk_ref[own, :] += dkk[tm:]
        dv_ref[prev, :] += dvv[:tm]
        dv_ref[own, :] += dvv[tm:]
        dgqs_ref[...] += dgqs
        dsinks_ref[...] += dsk
        dproj_ref[...] = dq.astype(BF)

    full = lambda shape: pl.BlockSpec(shape, lambda i: (0,) * len(shape))
    prev_b = lambda i: (jnp.maximum(i - 1, 0), 0)
    cur = lambda i: (i, 0)
    return _call(
        body, name, (nt,),
        [pl.BlockSpec((tm, main), cur), pl.BlockSpec((tm, main), cur),
         pl.BlockSpec((tm, KVW), prev_b), pl.BlockSpec((tm, KVW), cur),
         pl.BlockSpec((tm, KVW), prev_b), pl.BlockSpec((tm, KVW), cur),
         full((1, HEAD)), pl.BlockSpec((1, KV_HEADS, prow, 2 * tm), lambda i: (i, 0, 0, 0)),
         pl.BlockSpec((tm, LANE), cur)],
        [pl.BlockSpec((tm, main), cur), full((s_len, KVW)), full((s_len, KVW)), full((1, HEAD)), full((1, LANE))],
        [SDS((s_len, d), BF), SDS((s_len, KVW), F32), SDS((s_len, KVW), F32), SDS((1, HEAD), F32),
         SDS((1, LANE), F32)],
        semantics=("arbitrary",))(proj, dcat, kn, kn, v, v, gqs, probs, sink_probs)


def _kv_prep(kv, gk, *, tm):
    s_len = kv.shape[0]

    def body(kv_ref, gk_ref, k_ref, v_ref):
        k, v = _mem_kv(kv_ref[...], gk_ref[...])
        k_ref[...] = k
        v_ref[...] = v

    row = lambda i: (i, 0)
    return _call(
        body, "kv_prep", (s_len // tm,),
        [pl.BlockSpec((tm, 2 * KVW), row), pl.BlockSpec((1, HEAD), lambda i: (0, 0))],
        [pl.BlockSpec((tm, KVW), row), pl.BlockSpec((tm, KVW), row)],
        [SDS((s_len, KVW), BF), SDS((s_len, KVW), BF)], semantics=("parallel",))(kv, gk)


def _kv_bwd(kv, dks, dvs, gk, *, tm):
    s_len = kv.shape[0]
    nl = len(dks)

    def body(*refs):
        kv_ref, gk_ref = refs[0], refs[1]
        dk_refs, dv_refs = refs[2:2 + nl], refs[2 + nl:2 + 2 * nl]
        dkv_ref, dgk_ref = refs[2 + 2 * nl], refs[3 + 2 * nl]
        dk, dv = dk_refs[0][...], dv_refs[0][...]
        for t in range(1, nl):
            dk = dk + dk_refs[t][...]
            dv = dv + dv_refs[t][...]
        dkv, dgk = _mem_kv_bwd(kv_ref[...], dk, dv, gk_ref[...])
        dkv_ref[...] = dkv.astype(BF)

        @pl.when(pl.program_id(0) == 0)
        def _():
            dgk_ref[...] = jnp.zeros_like(dgk_ref)

        dgk_ref[...] += dgk

    row = lambda i: (i, 0)
    one = pl.BlockSpec((1, HEAD), lambda i: (0, 0))
    return _call(
        body, "kv_bwd", (s_len // tm,),
        [pl.BlockSpec((tm, 2 * KVW), row), one] + [pl.BlockSpec((tm, KVW), row)] * (2 * nl),
        [pl.BlockSpec((tm, 2 * KVW), row), one],
        [SDS((s_len, 2 * KVW), BF), SDS((1, HEAD), F32)], semantics=("arbitrary",))(kv, gk, *dks, *dvs)


def _place():
    x, y, c = lax.axis_index("x"), lax.axis_index("y"), lax.axis_index("c")
    flips = [(1 - x, y), (x, 1 - y), (1 - x, 1 - y)]
    return x, y, c, flips


def _remote(src, dst, send_sem, recv_sem, to):
    return pltpu.make_async_remote_copy(src_ref=src, dst_ref=dst, send_sem=send_sem, recv_sem=recv_sem,
                                        device_id=to, device_id_type=MESH)


def _gather_copies(p_refs, wg_refs, send, recv):
    x, y, c, flips = _place()
    chip = 2 * x + y
    cps = []
    for j, (fx, fy) in enumerate(flips):
        for b in range(2):
            half = p_refs[b].shape[0] // 2
            mine = pl.ds(c * half, half)
            cps.append(_remote(p_refs[b].at[mine, :], wg_refs[b].at[chip, mine, :], send.at[2 * j + b],
                               recv.at[2 * j + b], (fx, fy, c)))
    return cps, cps


def _forward_copies(p_refs, wg_refs, send, recv):
    x, y, c, flips = _place()
    chip = 2 * x + y
    sib = (x, y, 1 - c)
    sends, arrivals = [], []
    for b in range(2):
        half = p_refs[b].shape[0] // 2
        own = _remote(p_refs[b], wg_refs[b].at[chip], send.at[b], recv.at[b], sib)
        sends.append(own)
        arrivals.append(own)
        for j, (fx, fy) in enumerate(flips):
            k = 2 + 3 * b + j
            landed = wg_refs[b].at[2 * fx + fy, pl.ds(c * half, half), :]
            other = wg_refs[b].at[2 * fx + fy, pl.ds((1 - c) * half, half), :]
            sends.append(_remote(landed, landed, send.at[k], recv.at[k], sib))
            arrivals.append(_remote(other, other, send.at[k], recv.at[k], sib))
    return sends, arrivals


def _swap_copies(g_refs, r_refs, send, recv):
    x, y, c, _ = _place()
    cps = []
    for b in range(2):
        half = g_refs[b].shape[1] // 2
        cps.append(_remote(g_refs[b].at[:, pl.ds((1 - c) * half, half), :], r_refs[b], send.at[b], recv.at[b],
                           (x, y, 1 - c)))
    return cps, cps


def _split_start(make_copies, n_sems, bufs, after, fresh, *, name):
    def body(a1, a2, b1, b2, after_ref, send, recv, *outs):
        for cp in make_copies((a1, a2), (b1, b2), send, recv)[0]:
            cp.start()
        outs[4][...] = jnp.zeros_like(outs[4])

    extra_shape = () if fresh is None else (pltpu.HBM(fresh, F32),)
    extra_spec = () if fresh is None else (HBM,)
    return pl.pallas_call(
        body, name=name,
        out_shape=(pltpu.SemaphoreType.DMA((n_sems,)), pltpu.SemaphoreType.DMA((n_sems,)))
        + tuple(pltpu.HBM(b.shape, b.dtype) for b in bufs) + (SDS((8, LANE), F32),) + extra_shape,
        in_specs=(HBM, HBM, HBM, HBM, ANY), out_specs=(SEM, SEM, HBM, HBM, HBM, HBM, VMEM_WHOLE) + extra_spec,
        input_output_aliases={0: 2, 1: 3, 2: 4, 3: 5},
        compiler_params=pltpu.CompilerParams(has_side_effects=SIDE_EFFECT))(*[_in_hbm(b) for b in bufs], after)


def _split_wait(make_copies, started, after, *, name):
    send, recv, bufs = started[0], started[1], started[2:6]

    def body(a1, a2, b1, b2, send_ref, recv_ref, after_ref, *outs):
        sends, arrivals = make_copies((a1, a2), (b1, b2), send_ref, recv_ref)
        for cp in arrivals:
            cp.wait_recv()
        for cp in sends:
            cp.wait_send()

    return pl.pallas_call(
        body, name=name, out_shape=tuple(pltpu.HBM(b.shape, b.dtype) for b in bufs),
        in_specs=(HBM, HBM, HBM, HBM, SEM, SEM, ANY), out_specs=(HBM, HBM, HBM, HBM),
        input_output_aliases={0: 0, 1: 1, 2: 2, 3: 3},
        compiler_params=pltpu.CompilerParams(has_side_effects=SIDE_EFFECT))(*bufs, send, recv, after)


def _gather_small(ps):
    def body(ps_ref, o_ref, send, recv):
        x, y, c, flips = _place()
        chip = 2 * x + y
        o_ref[chip] = ps_ref[...]
        cps = [_remote(ps_ref, o_ref.at[chip], send.at[j], recv.at[j], (fx, fy, c))
               for j, (fx, fy) in enumerate(flips)]
        for cp in cps:
            cp.start()
        for j, (fx, fy) in enumerate(flips):
            _remote(ps_ref, o_ref.at[2 * fx + fy], send.at[j], recv.at[j], (fx, fy, c)).wait_recv()
        for cp in cps:
            cp.wait_send()

    return pl.pallas_call(
        body, name="gather_small", in_specs=[VMEM_WHOLE], out_specs=VMEM_WHOLE,
        out_shape=SDS((N_CHIPS,) + ps.shape, ps.dtype),
        scratch_shapes=[pltpu.SemaphoreType.DMA((3,)), pltpu.SemaphoreType.DMA((3,))])(ps)


def _sum_sibling(g, r, place, *, tm, name):
    n_sh, half, w = r.shape
    nt = half // tm

    def body(place_ref, g_ref, r_ref, pbf_ref, own_ref):
        s = pl.program_id(1)
        p = g_ref[0] + r_ref[0]
        pbf_ref[0] = p.astype(BF)

        @pl.when(s == place_ref[1])
        def _():
            own_ref[...] = p

    return _call(
        body, name, (nt, n_sh),
        [pl.BlockSpec((1, tm, w), lambda i, s, pr: (s, pr[0] * nt + i, 0)),
         pl.BlockSpec((1, tm, w), lambda i, s, pr: (s, i, 0))],
        [pl.BlockSpec((1, tm, w), lambda i, s, pr: (s, i, 0)), pl.BlockSpec((tm, w), lambda i, s, pr: (i, 0))],
        [SDS((n_sh, half, w), BF), SDS((half, w), F32)],
        semantics=("arbitrary", "arbitrary"), prefetch=1)(place, g, r)


def _rs_copies(p_refs, land_refs, send, recv):
    _, _, c, flips = _place()
    cps = []
    for j, (fx, fy) in enumerate(flips):
        for b in range(2):
            cps.append(_remote(p_refs[b].at[2 * fx + fy], land_refs[b].at[j], send.at[2 * j + b], recv.at[2 * j + b],
                               (fx, fy, c)))
    return cps, cps


def _sum_chips(own, r, full, layer, place, *, tm, name):
    half, w = own.shape

    def body(place_ref, own_ref, r_ref, _, o_ref):
        o_ref[0, 0] = ((own_ref[...] + r_ref[0].astype(F32)) + r_ref[1].astype(F32)) + r_ref[2].astype(F32)

    return _call(
        body, name, (half // tm,),
        [pl.BlockSpec((tm, w), lambda i, pr: (i, 0)), pl.BlockSpec((3, tm, w), lambda i, pr: (0, i, 0)), ANY],
        pl.BlockSpec((1, 1, tm, w), lambda i, pr: (layer, pr[0], i, 0)), SDS(full.shape, F32),
        semantics=("parallel",), prefetch=1, aliases={3: 0})(place, own, r, full)


def _share_with_sibling(f1, f2, lo, hi, *, name):
    def body(_, __, o1_ref, o2_ref, send, recv):
        x, y, c, _ = _place()
        lay, mine, other = pl.ds(lo, hi - lo), pl.ds(c, 1), pl.ds(1 - c, 1)
        cps = [_remote(o1_ref.at[lay, mine], o1_ref.at[lay, mine], send.at[0], recv.at[0], (x, y, 1 - c)),
               _remote(o2_ref.at[lay, mine], o2_ref.at[lay, mine], send.at[1], recv.at[1], (x, y, 1 - c))]
        for cp in cps:
            cp.start()
        for cp in cps:
            cp.wait_send()
        _remote(o1_ref.at[lay, other], o1_ref.at[lay, other], send.at[0], recv.at[0], (x, y, 1 - c)).wait_recv()
        _remote(o2_ref.at[lay, other], o2_ref.at[lay, other], send.at[1], recv.at[1], (x, y, 1 - c)).wait_recv()

    return pl.pallas_call(
        body, name=name, in_specs=[ANY, ANY], out_specs=[ANY, ANY],
        out_shape=[SDS(f1.shape, f1.dtype), SDS(f2.shape, f2.dtype)], input_output_aliases={0: 0, 1: 1},
        scratch_shapes=[pltpu.SemaphoreType.DMA((2,)), pltpu.SemaphoreType.DMA((2,))])(f1, f2)


def _allreduce_small(sg):
    rows, w = sg.shape
    half = rows // 2
    assert half % 8 == 0

    def body(sg_ref, o_ref, sib_buf, part, slots, send, recv):
        x, y, c, flips = _place()
        chip = 2 * x + y
        sib = (x, y, 1 - c)
        mine = pl.ds(pl.multiple_of(c * half, 8), half)
        other = pl.ds(pl.multiple_of((1 - c) * half, 8), half)
        to_sib = _remote(sg_ref.at[other, :], sib_buf, send.at[0], recv.at[0], sib)
        to_sib.start()
        to_sib.wait_recv()
        part[...] = sg_ref[mine, :] + sib_buf[...]
        slots[chip] = part[...]
        to_chips = [_remote(part, slots.at[chip], send.at[1 + j], recv.at[1 + j], (fx, fy, c))
                    for j, (fx, fy) in enumerate(flips)]
        for cp in to_chips:
            cp.start()
        for j, (fx, fy) in enumerate(flips):
            _remote(part, slots.at[2 * fx + fy], send.at[1 + j], recv.at[1 + j], (fx, fy, c)).wait_recv()
        o_ref[mine, :] = ((slots[0] + slots[1]) + slots[2]) + slots[3]
        back = _remote(o_ref.at[mine, :], o_ref.at[mine, :], send.at[4], recv.at[4], sib)
        back.start()
        _remote(o_ref.at[other, :], o_ref.at[other, :], send.at[4], recv.at[4], sib).wait_recv()
        for cp in [to_sib, back] + to_chips:
            cp.wait_send()

    return pl.pallas_call(
        body, name="allreduce_small", in_specs=[VMEM_WHOLE], out_specs=VMEM_WHOLE, out_shape=SDS((rows, w), F32),
        scratch_shapes=[pltpu.VMEM((half, w), F32), pltpu.VMEM((half, w), F32), pltpu.VMEM((N_CHIPS, half, w), F32),
                        pltpu.SemaphoreType.DMA((5,)), pltpu.SemaphoreType.DMA((5,))])(sg)


def _adamw_math(g, w, m, v):
    mn = ADAM_B1 * m + (1.0 - ADAM_B1) * g
    vn = ADAM_B2 * v + (1.0 - ADAM_B2) * (g * g)
    m_hat = mn / (1.0 - ADAM_B1 ** ADAM_STEP)
    v_hat = vn / (1.0 - ADAM_B2 ** ADAM_STEP)
    return -ADAM_LR * (m_hat / (jnp.sqrt(v_hat) + ADAM_EPS) + ADAM_WD * w), mn, vn


def _adamw_small(gs, ws, ms, vs):
    n = len(gs)

    def body(*refs):
        for k in range(n):
            g, w, m, v = (refs[t * n + k][...] for t in range(4))
            refs[4 * n + k][...], refs[5 * n + k][...], refs[6 * n + k][...] = _adamw_math(g, w, m, v)

    out = pl.pallas_call(
        body, name="adamw_small", in_specs=[VMEM_WHOLE] * (4 * n), out_specs=[VMEM_WHOLE] * (3 * n),
        out_shape=[SDS(w.shape, F32) for w in ws] * 3)(*gs, *ws, *ms, *vs)
    return out[:n], out[n:2 * n], out[2 * n:]


def _adamw(g_arr, layer0, g_off, per_layer, w, m, v, *, name, tm, layers=None, prev=None):
    rows, cols = w.shape
    assert g_off % tm == 0 and per_layer % tm == 0 and rows % per_layer == 0
    npl = per_layer // tm
    lo, hi = layers or (0, rows // per_layer)
    base = lo * npl

    def body(g_ref, w_ref, m_ref, v_ref, *rest):
        go_ref, d_ref, mo_ref, vo_ref = rest[-4:]
        g = g_ref[0]
        go_ref[...] = g
        d_ref[...], mo_ref[...], vo_ref[...] = _adamw_math(g, w_ref[...], m_ref[...], v_ref[...])

    blk = pl.BlockSpec((tm, cols), lambda i: (base + i, 0))
    extra = list(prev or [])
    return _call(
        body, name, ((hi - lo) * npl,),
        [pl.BlockSpec((1, tm, cols), lambda i: (layer0 + lo + i // npl, g_off // tm + i % npl, 0)), blk, blk, blk]
        + [ANY] * len(extra),
        [blk] * 4,
        [SDS((rows, cols), F32)] * 4, semantics=("parallel",),
        aliases={4 + k: k for k in range(len(extra))})(g_arr, w, m, v, *extra)


def _pack_small(parts, width):
    flat = jnp.concatenate([p.reshape(-1).astype(F32) for p in parts])
    rows = -(-flat.shape[0] // (16 * width)) * 16
    return jnp.pad(flat, (0, rows * width - flat.shape[0])).reshape(rows, width)


def _unpack_small(packed, shapes):
    flat = packed.reshape(-1)
    out, off = [], 0
    for shp in shapes:
        size = 1
        for n in shp:
            size *= n
        out.append(flat[off:off + size].reshape(shp))
        off += size
    return out


def _block_diag(pw):
    g, c, _ = pw.shape
    eye = jnp.eye(g, dtype=pw.dtype)
    return (eye[:, None, :, None] * pw[:, :, None, :]).reshape(g * c, g * c)


def _diag_blocks(full, g):
    c = full.shape[0] // g
    return jnp.stack([full[i * c:(i + 1) * c, i * c:(i + 1) * c] for i in range(g)])


def kernel(x, mem, norm_mix, w_in, pool_w, pool_scale, kv_norm, w_kv, k_norm, q_norm, sinks, mem_norm, w_mem_kv, mem_q_norm, mem_k_norm, w_out, norm_mlp, w_up, w_down, loss_target, m_norm_mix, m_w_in, m_pool_w, m_pool_scale, m_kv_norm, m_w_kv, m_k_norm, m_q_norm, m_sinks, m_mem_norm, m_w_mem_kv, m_mem_q_norm, m_mem_k_norm, m_w_out, m_norm_mlp, m_w_up, m_w_down, v_norm_mix, v_w_in, v_pool_w, v_pool_scale, v_kv_norm, v_w_kv, v_k_norm, v_q_norm, v_sinks, v_mem_norm, v_w_mem_kv, v_mem_q_norm, v_mem_k_norm, v_w_out, v_norm_mlp, v_w_up, v_w_down):
    s_len, d = x.shape[1], x.shape[2]
    n_layers, n_pool = norm_mix.shape[0], pool_w.shape[0]
    n_swa = n_layers - n_pool
    main = d - KVW
    qh = main // HEAD
    ff = w_down.shape[1] * N_CHIPS
    dq = d // N_CHIPS
    assert w_up.shape[2] == d and ff == N_CHIPS * d and w_kv.shape[1] == 2 * KVW
    tm = min(512, s_len)
    tmb = min(1024, s_len)
    tm_mem = mem.shape[1]

    cx, cy, cc = lax.axis_index("x"), lax.axis_index("y"), lax.axis_index("c")
    chip = 2 * cx + cy
    place = jnp.stack([cc, chip]).astype(jnp.int32)

    off_down, off_up, off_in, off_out = 0, d, 2 * d, 2 * d + dq
    rows1 = off_out + dq
    off_mkv, off_kv = 0, dq
    rows2 = 2 * dq

    ps = jnp.pad(pool_scale, ((0, 8 - n_pool), (0, 2 * LANE - pool_scale.shape[1])))

    def packed_weights(l):
        p1 = jnp.concatenate([w_down[l], w_up[l], w_in[l], w_out[l]]).astype(BF)
        p2 = jnp.concatenate([w_mem_kv[l], w_kv] if l == n_pool else [w_mem_kv[l]]).astype(BF)
        return p1, p2

    def gather_start(l, after):
        p1, p2 = packed_weights(l)
        bufs = (p1, p2, lax.empty((N_CHIPS,) + p1.shape, BF), lax.empty((N_CHIPS,) + p2.shape, BF))
        return _split_start(_gather_copies, 6, bufs, after, None, name=f"gather_start_{l}")

    def gather_land(l, started, after):
        bufs = _split_wait(_gather_copies, started, after, name=f"gather_wait_{l}")
        return _split_start(_forward_copies, 8, bufs, place, None, name=f"forward_start_{l}")

    def gather_finish(l, forwarding, after):
        bufs = _split_wait(_forward_copies, forwarding, after, name=f"forward_wait_{l}")
        return bufs[2], bufs[3]

    def w_rows(arr, off, nrows, width):
        assert off % nrows == 0
        return (arr, (N_CHIPS, nrows, width), lambda j: (0, off // nrows, 0))

    row = lambda a: a.reshape(1, -1)
    h = x.reshape(s_len, d)
    memx = mem.reshape(tm_mem, d)
    tgt = loss_target.reshape(s_len, d)
    pbd = [_block_diag(pool_w[l]).astype(BF) for l in range(n_pool)]
    sinks_pad = [jnp.pad(row(sinks[j]), ((0, 0), (0, LANE - qh))) for j in range(n_swa)]

    w_in_l, w_out_l, w_down_l, w_up_all_l, w_mkv_l = [], [], [], [], []
    w_kv_g = None
    first = gather_start(0, place)
    psg = _gather_small(ps + first[6][0, 0])
    pool_scale_full = jnp.concatenate([psg[k, :n_pool, :pool_scale.shape[1]] for k in range(N_CHIPS)], axis=1)
    forwarding = gather_land(0, first, psg)
    travelling = gather_start(1, forwarding[6]) if n_layers > 1 else None
    saved, probs, sink_probs, mem_probs = [], {}, {}, {}
    kv = hn_kv = kn = vsh = None
    for l in range(n_layers):
        wg1, wg2 = gather_finish(l, forwarding, h if l else forwarding[6])
        w_in_l.append(w_rows(wg1, off_in, dq, d))
        w_out_l.append(w_rows(wg1, off_out, dq, d))
        w_down_l.append(w_rows(wg1, off_down, d, d))
        w_up_all_l.append((wg1, (N_CHIPS, d, d), lambda j: (0, off_up // d, 0)))
        w_mkv_l.append(w_rows(wg2, off_mkv, dq, 2 * KVW))
        g_mix = row(norm_mix[l])
        if travelling is not None:
            g_mix = g_mix + travelling[6][0, 0]
        if l == n_pool:
            w_kv_g = w_rows(wg2, off_kv, dq, 2 * KVW)
            kv, hn_kv = _norm_mm(h, row(kv_norm), w_kv_g, 1, 2 * KVW, act=False, name="kv_proj", tm=tmb)
            kn, vsh = _kv_prep(kv, row(k_norm), tm=tmb)
        h0 = h
        proj, xn = _norm_mm(h0, g_mix, w_in_l[l], 1, d, act=False, name=f"in_proj_{l}", tm=tmb)
        mkv, memn = _norm_mm(memx, row(mem_norm[l]), w_mkv_l[l], 1, 2 * KVW, act=False, name=f"mem_kv_{l}", tm=tm_mem)
        if l < n_pool:
            cat, mem_probs[l] = _mixer_pool_fwd(proj, mkv, pbd[l], row(pool_scale_full[l]), row(mem_q_norm[l]),
                                                row(mem_k_norm[l]), name=f"mixer_fwd_{l}", tm=tm)
        else:
            j = l - n_pool
            cat, probs[l], sink_probs[l] = _mixer_swa_fwd(proj, kn, vsh, row(q_norm[j]), sinks_pad[j],
                                                          name=f"mixer_fwd_{l}")
            cat, mem_probs[l] = _mem_attn_fwd(proj, mkv, row(mem_q_norm[l]), row(mem_k_norm[l]), cat,
                                              name=f"mem_attn_fwd_{l}", tm=tmb)
        h1 = _mm_res(h0, cat, w_out_l[l], name=f"out_proj_{l}", tm=tmb)
        hh, xm = _norm_mm(h1, row(norm_mlp[l]), w_up_all_l[l], N_CHIPS, d, act=True, name=f"mlp_up_{l}", tm=tm)
        after = None
        if travelling is not None:
            forwarding = gather_land(l + 1, travelling, hh)
            travelling = gather_start(l + 2, forwarding[6]) if l + 2 < n_layers else None
            after = forwarding[6]
        h = _mm_res(h1, hh, w_down_l[l], name=f"mlp_down_{l}", tm=tm, after=after)
        saved.append((h0, proj, xn, mkv, memn, cat, h1, hh, xm))

    dh, dh_bf, loss_part = _loss_head(h, tgt, tm=tmb)

    half1, half2 = rows1 // 2, rows2 // 2
    g1 = lax.empty((N_CHIPS, rows1, d), F32)
    pending = {}
    swapping = None
    tk = min(512, d)

    def reduce_begin(l, swapped, after):
        g1_l, g2_l, r1, r2 = _split_wait(_swap_copies, swapped, after, name=f"swap_wait_{l}")
        pb1, own1 = _sum_sibling(g1_l, r1, place, tm=_tile(half1, 640), name=f"sum_sibling_a_{l}")
        pb2, own2 = _sum_sibling(g2_l, r2, place, tm=_tile(half2, 256), name=f"sum_sibling_b_{l}")
        bufs = (pb1, pb2, lax.empty((3, half1, d), BF), lax.empty((3, half2, 2 * KVW), BF))
        return _split_start(_rs_copies, 6, bufs, place, None, name=f"reduce_start_{l}"), own1, own2
    zeros_mem = jnp.zeros((tm_mem, d), F32)

    def rows_map(off, nrows, tkk):
        per = nrows // tkk
        return lambda i, j: (i // per, off // tkk + i % per, 0)

    def cols_map(off, tkk):
        return lambda i, j: (j, off // tkk + i, 0)

    d_norm_mix, d_norm_mlp, d_mem_norm = [None] * n_layers, [None] * n_layers, [None] * n_layers
    d_mem_q, d_mem_k = [None] * n_layers, [None] * n_layers
    d_pool_w, d_pool_scale = [None] * n_pool, [None] * n_pool
    d_q_norm, d_sinks = [None] * n_swa, [None] * n_swa
    dks, dvs = [], []
    d_kv_norm = d_k_norm = None
    for l in reversed(range(n_layers)):
        h0, proj, xn, mkv, memn, cat, h1, hh, xm = saved[l]
        g2 = jnp.zeros((N_CHIPS, rows2, 2 * KVW), F32)
        g1 = _mm_tn(hh, dh_bf, g1, rows_map(off_down, d, tk), tk, d, name=f"dw_down_{l}")
        du = _mm_nt_relu2(dh_bf, hh, w_down_l[l], N_CHIPS, name=f"d_mlp_act_{l}", tm=tm)
        g1 = _mm_tn(xm, du, g1, cols_map(off_up, tk), tk, d, name=f"dw_up_{l}")
        g_mlp = row(norm_mlp[l])
        if swapping is not None:
            pending[swapping[0]] = reduce_begin(*swapping, after=g1)
            g_mlp = g_mlp + pending[swapping[0]][0][6][0, 0]
        dh1, dh1_bf, d_norm_mlp[l] = _mm_nt_normbwd(du, w_up_all_l[l], N_CHIPS, h1, g_mlp, dh,
                                                    name=f"d_mlp_in_{l}", tm=tm)
        tkq = min(tk, dq)
        g1 = _mm_tn(cat, dh1_bf, g1, rows_map(off_out, dq, tkq), tkq, d, name=f"dw_out_{l}")
        dcat = _mm_nt(dh1_bf, w_out_l[l], d, name=f"d_cat_{l}", tm=tmb)
        if l < n_pool:
            dproj, dpbd, dscale, dmkv, d_mem_q[l], d_mem_k[l] = _mixer_pool_bwd(
                proj, dcat, mkv, pbd[l], row(pool_scale_full[l]), row(mem_q_norm[l]), row(mem_k_norm[l]),
                mem_probs[l], name=f"mixer_bwd_{l}", tm=tm)
            d_pool_w[l] = _diag_blocks(dpbd, len(POOL_WINDOWS))
            d_pool_scale[l] = dscale
        else:
            j = l - n_pool
            dproj, dk, dv, d_q_norm[j], dsk = _mixer_swa_bwd(proj, dcat, kn, vsh, row(q_norm[j]), probs[l],
                                                             sink_probs[l], name=f"mixer_bwd_{l}")
            dproj, dmkv, d_mem_q[l], d_mem_k[l] = _mem_attn_bwd(
                proj, dcat, mkv, row(mem_q_norm[l]), row(mem_k_norm[l]), mem_probs[l], dproj,
                name=f"mem_attn_bwd_{l}", tm=tmb)
            d_sinks[j] = dsk[0, :qh]
            dks.append(dk)
            dvs.append(dv)
        g1 = _mm_tn(xn, dproj, g1, rows_map(off_in, dq, tkq), tkq, d, name=f"dw_in_{l}")
        dh, dh_bf, d_norm_mix[l] = _mm_nt_normbwd(dproj, w_in_l[l], 1, h0, row(norm_mix[l]), dh1,
                                                  name=f"d_in_{l}", tm=tmb)
        g2 = _mm_tn(memn, dmkv, g2, rows_map(off_mkv, dq, tkq), tkq, 2 * KVW, name=f"dw_mem_kv_{l}")
        _, _, d_mem_norm[l] = _mm_nt_normbwd(dmkv, w_mkv_l[l], 1, memx, row(mem_norm[l]), zeros_mem,
                                             name=f"d_mem_norm_{l}", tm=tm_mem)
        if l == n_pool:
            dkv, d_k_norm = _kv_bwd(kv, dks, dvs, row(k_norm), tm=tmb)
            g2 = _mm_tn(hn_kv, dkv, g2, rows_map(off_kv, dq, tkq), tkq, 2 * KVW, name="dw_kv")
            dh, dh_bf, d_kv_norm = _mm_nt_normbwd(dkv, w_kv_g, 1, h0, row(kv_norm), dh, name="d_kv_in", tm=tmb)
        bufs = (g1, g2, lax.empty((N_CHIPS, half1, d), F32), lax.empty((N_CHIPS, half2, 2 * KVW), F32))
        swapping = (l, _split_start(_swap_copies, 2, bufs, place, (N_CHIPS, rows1, d) if l > 0 else None,
                                    name=f"swap_start_{l}"))
        g1 = swapping[1][7] if l > 0 else None
    grad_x = dh.reshape(x.shape)

    small_names = ["norm_mix", "pool_w", "pool_scale", "kv_norm", "k_norm", "q_norm", "sinks", "mem_norm",
                   "mem_q_norm", "mem_k_norm", "norm_mlp"]
    small_grads = {
        "norm_mix": jnp.concatenate(d_norm_mix), "pool_w": jnp.stack(d_pool_w),
        "pool_scale": jnp.concatenate(d_pool_scale), "kv_norm": d_kv_norm[0], "k_norm": d_k_norm[0],
        "q_norm": jnp.concatenate(d_q_norm), "sinks": jnp.stack(d_sinks), "mem_norm": jnp.concatenate(d_mem_norm),
        "mem_q_norm": jnp.concatenate(d_mem_q), "mem_k_norm": jnp.concatenate(d_mem_k),
        "norm_mlp": jnp.concatenate(d_norm_mlp)}
    width = d
    sg = _pack_small([small_grads[n] for n in small_names] + [loss_part], width)
    sg = sg + swapping[1][6][0, 0]
    sg = _allreduce_small(sg)
    pending[0] = reduce_begin(*swapping, after=sg)
    *unpacked, loss_sum = _unpack_small(sg, [small_grads[n].shape for n in small_names] + [(1, 1)])
    loss = loss_sum[0, 0]
    reduced = dict(zip(small_names, unpacked))
    psw = pool_scale.shape[1]
    reduced["pool_scale"] = lax.dynamic_slice_in_dim(reduced["pool_scale"], chip * psw, psw, axis=1)
    params = dict(norm_mix=(norm_mix, m_norm_mix, v_norm_mix), pool_w=(pool_w, m_pool_w, v_pool_w),
                  pool_scale=(pool_scale, m_pool_scale, v_pool_scale), kv_norm=(kv_norm, m_kv_norm, v_kv_norm),
                  k_norm=(k_norm, m_k_norm, v_k_norm), q_norm=(q_norm, m_q_norm, v_q_norm),
                  sinks=(sinks, m_sinks, v_sinks), mem_norm=(mem_norm, m_mem_norm, v_mem_norm),
                  mem_q_norm=(mem_q_norm, m_mem_q_norm, v_mem_q_norm),
                  mem_k_norm=(mem_k_norm, m_mem_k_norm, v_mem_k_norm), norm_mlp=(norm_mlp, m_norm_mlp, v_norm_mlp))
    flat2 = lambda a: a.reshape(-1, a.shape[-1])
    grads = [flat2(reduced[n].reshape(params[n][0].shape)) for n in small_names]
    grads[0] = grads[0] + pending[0][0][6][0, 0]
    res = _adamw_small(grads, *[[flat2(params[n][t]) for n in small_names] for t in range(3)])
    small = {n: [grads[k].reshape(params[n][0].shape)] + [r[k].reshape(params[n][0].shape) for r in res]
             for k, n in enumerate(small_names)}

    full1 = lax.empty((n_layers, 2, half1, d), F32)
    full2 = lax.empty((n_layers, 2, half2, 2 * KVW), F32)
    kinds = (("w_down", 1, off_down, d, w_down, m_w_down, v_w_down), ("w_up", 1, off_up, d, w_up, m_w_up, v_w_up),
             ("w_in", 1, off_in, dq, w_in, m_w_in, v_w_in), ("w_out", 1, off_out, dq, w_out, m_w_out, v_w_out),
             ("w_mem_kv", 2, off_mkv, dq, w_mem_kv, m_w_mem_kv, v_w_mem_kv))
    big = {}
    first = 1 if n_layers > 1 and n_pool > 0 else 0
    after = res[0][0]
    for lo, hi in ((first, n_layers), (0, first)):
        if lo == hi:
            continue
        for l in reversed(range(lo, hi)):
            exchange, own1, own2 = pending[l]
            _, _, x1, x2 = _split_wait(_rs_copies, exchange, after, name=f"reduce_wait_{l}")
            full1 = _sum_chips(own1, x1, full1, l, place, tm=_tile(half1, 640), name=f"sum_chips_a_{l}")
            full2 = _sum_chips(own2, x2, full2, l, place, tm=_tile(half2, 256), name=f"sum_chips_b_{l}")
        full1, full2 = _share_with_sibling(full1, full2, lo, hi, name=f"share_with_sibling_{lo}")
        views = (None, full1.reshape(n_layers, rows1, d), full2.reshape(n_layers, rows2, 2 * KVW))
        for name, which, off, per, w_, m_, v_ in kinds:
            cols = views[which].shape[2]
            big[name] = _adamw(views[which], 0, off, per, w_.reshape(-1, cols), m_.reshape(-1, cols),
                               v_.reshape(-1, cols), name=f"adamw_{name}_{lo}", tm=min(512, per), layers=(lo, hi),
                               prev=big.get(name))
        if lo <= n_pool < hi:
            big["w_kv"] = _adamw(views[2], n_pool, off_kv, dq, w_kv, m_w_kv, v_w_kv, name="adamw_w_kv",
                                 tm=min(256, dq))
        after = big["w_mem_kv"][1]
    shapes_big = dict(w_down=w_down.shape, w_up=w_up.shape, w_in=w_in.shape, w_out=w_out.shape,
                      w_mem_kv=w_mem_kv.shape, w_kv=w_kv.shape)
    big = {n: [r.reshape(shapes_big[n]) for r in big[n]] for n in big}

    order = ["norm_mix", "w_in", "pool_w", "pool_scale", "kv_norm", "w_kv", "k_norm", "q_norm", "sinks", "mem_norm",
             "w_mem_kv", "mem_q_norm", "mem_k_norm", "w_out", "norm_mlp", "w_up", "w_down"]
    out = {**big, **small}
    return (loss, grad_x, *[out[n][0] for n in order], *[out[n][1] for n in order],
            *[out[n][2] for n in order], *[out[n][3] for n in order])
```

```python
import functools

import jax
import jax.numpy as jnp
from jax import lax
from jax.experimental import pallas as pl
from jax.experimental.pallas import tpu as pltpu

F32, BF = jnp.float32, jnp.bfloat16
SDS = jax.ShapeDtypeStruct
MESH = pl.DeviceIdType.MESH
ANY = pl.BlockSpec(memory_space=pl.ANY)
HBM = pl.BlockSpec(memory_space=pltpu.HBM)
SEM = pl.BlockSpec(memory_space=pltpu.SEMAPHORE)
VMEM_WHOLE = pl.BlockSpec(memory_space=pltpu.VMEM)
SIDE_EFFECT = pltpu.SideEffectType.DATAFLOW_SIDE_EFFECTING


def _in_hbm(a):
    return pltpu.with_memory_space_constraint(a, pltpu.HBM)


EPS = 1e-6
HEAD = 64
KV_HEADS = 4
KVW = KV_HEADS * HEAD
WINDOW = 128
POOL_WINDOWS = (2, 4, 8, 16)
HALO = 16
QK_SCALE = HEAD ** -0.5
NEG = float(jnp.finfo(jnp.float32).min)
N_CHIPS = 4
LANE = 128

ADAM_LR, ADAM_B1, ADAM_B2, ADAM_EPS, ADAM_WD, ADAM_STEP = 0.001, 0.9, 0.999, 1e-08, 0.01, 10

VMEM_LIMIT_MB = 56


def _call(body, name, grid, in_specs, out_specs, out_shape, *, scratch=(), semantics=None, aliases=None,
          prefetch=0):
    params = pltpu.CompilerParams(dimension_semantics=semantics, vmem_limit_bytes=VMEM_LIMIT_MB << 20)
    if prefetch:
        spec = pltpu.PrefetchScalarGridSpec(num_scalar_prefetch=prefetch, grid=grid, in_specs=in_specs,
                                            out_specs=out_specs, scratch_shapes=list(scratch))
        return pl.pallas_call(body, name=name, grid_spec=spec, out_shape=out_shape,
                              input_output_aliases=aliases or {}, compiler_params=params)
    return pl.pallas_call(body, name=name, grid=grid, in_specs=in_specs, out_specs=out_specs, out_shape=out_shape,
                          scratch_shapes=list(scratch), input_output_aliases=aliases or {}, compiler_params=params)


def _tile(n, pref):
    return max(t for t in range(8, min(n, pref) + 1, 8) if n % t == 0)


def _dot(a, b):
    return jnp.dot(a, b, preferred_element_type=F32)


def _dot_nt(a, b):
    return lax.dot_general(a, b, (((1,), (1,)), ((), ())), preferred_element_type=F32)


def _dot_tn(a, b):
    return lax.dot_general(a, b, (((0,), (0,)), ((), ())), preferred_element_type=F32)


def _rms(x):
    r = lax.rsqrt(jnp.mean(x * x, axis=-1, keepdims=True) + EPS)
    return x * r, r


def _rms_bwd(dy, xh, r, g):
    dg = jnp.sum(dy * xh, axis=0, keepdims=True)
    dyg = dy * g
    dx = r * (dyg - xh * jnp.mean(dyg * xh, axis=-1, keepdims=True))
    return dx, dg


def _norm_mm(h, g, w, nj, tn, *, act, name, tm):
    w_arr, w_block, w_imap = w
    rows, d = h.shape

    def body(h_ref, g_ref, w_ref, y_ref, xn_ref):
        xh, _ = _rms(h_ref[...])
        xn = (xh * g_ref[...]).astype(BF)
        xn_ref[...] = xn
        for j in range(nj):
            u = _dot(xn, w_ref[j] if nj > 1 else w_ref[...].reshape(d, tn))
            if act:
                a = jnp.maximum(u, 0.0)
                y_ref[:, j * tn:(j + 1) * tn] = (a * a).astype(BF)
            else:
                y_ref[:, j * tn:(j + 1) * tn] = u

    assert nj == 1 or w_block[0] == nj
    return _call(
        body, name, (rows // tm,),
        [pl.BlockSpec((tm, d), lambda i: (i, 0)), pl.BlockSpec((1, d), lambda i: (0, 0)),
         pl.BlockSpec(w_block, lambda i: w_imap(0))],
        [pl.BlockSpec((tm, nj * tn), lambda i: (i, 0)), pl.BlockSpec((tm, d), lambda i: (i, 0))],
        [SDS((rows, nj * tn), BF if act else F32), SDS((rows, d), BF)],
        semantics=("parallel",))(h, g, w_arr)


def _mm_res(res, a, w, *, name, tm, after=None):
    w_arr, w_block, w_imap = w
    rows, k = a.shape
    n = res.shape[1]

    def body(res_ref, a_ref, w_ref, *rest):
        rest[-1][...] = res_ref[...] + _dot(a_ref[...], w_ref[...].reshape(k, n))

    extra = [] if after is None else [after]
    return _call(
        body, name, (rows // tm,),
        [pl.BlockSpec((tm, n), lambda i: (i, 0)), pl.BlockSpec((tm, k), lambda i: (i, 0)),
         pl.BlockSpec(w_block, lambda i: w_imap(0))] + [ANY] * len(extra),
        pl.BlockSpec((tm, n), lambda i: (i, 0)), SDS((rows, n), F32), semantics=("parallel",))(res, a, w_arr, *extra)


def _mm_nt(dy, w, k, *, name, tm):
    w_arr, w_block, w_imap = w
    rows, n = dy.shape

    def body(dy_ref, w_ref, o_ref):
        o_ref[...] = _dot_nt(dy_ref[...], w_ref[...].reshape(k, n))

    return _call(
        body, name, (rows // tm,),
        [pl.BlockSpec((tm, n), lambda i: (i, 0)), pl.BlockSpec(w_block, lambda i: w_imap(0))],
        pl.BlockSpec((tm, k), lambda i: (i, 0)), SDS((rows, k), F32), semantics=("parallel",))(dy, w_arr)


def _mm_nt_relu2(dh, hh, w, nj, *, name, tm):
    w_arr, w_block, w_imap = w
    rows, d = dh.shape
    tk = hh.shape[1] // nj

    def body(dh_ref, hh_ref, w_ref, o_ref):
        dh_t = dh_ref[...]
        for j in range(nj):
            cols = slice(j * tk, (j + 1) * tk)
            dhh = _dot_nt(dh_t, w_ref[j])
            o_ref[:, cols] = (dhh * (2.0 * jnp.sqrt(hh_ref[:, cols].astype(F32)))).astype(BF)

    assert w_block[0] == nj
    return _call(
        body, name, (rows // tm,),
        [pl.BlockSpec((tm, d), lambda i: (i, 0)), pl.BlockSpec((tm, nj * tk), lambda i: (i, 0)),
         pl.BlockSpec(w_block, lambda i: w_imap(0))],
        pl.BlockSpec((tm, nj * tk), lambda i: (i, 0)), SDS((rows, nj * tk), BF),
        semantics=("parallel",))(dh, hh, w_arr)


def _mm_nt_normbwd(dy, w, nsplit, h, g, dres, *, name, tm):
    w_arr, w_block, w_imap = w
    rows, n = dy.shape
    d = h.shape[1]
    ns = n // nsplit

    def body(dy_ref, w_ref, h_ref, g_ref, dres_ref, o_ref, obf_ref, dg_ref):
        if nsplit == 1:
            dxn = _dot_nt(dy_ref[...].astype(BF), w_ref[...].reshape(d, n))
        else:
            dxn = _dot_nt(dy_ref[:, 0:ns].astype(BF), w_ref[0])
            for s in range(1, nsplit):
                dxn += _dot_nt(dy_ref[:, s * ns:(s + 1) * ns].astype(BF), w_ref[s])
        xh, r = _rms(h_ref[...])
        dx, dg = _rms_bwd(dxn, xh, r, g_ref[...])
        out = dres_ref[...] + dx
        o_ref[...] = out
        obf_ref[...] = out.astype(BF)

        @pl.when(pl.program_id(0) == 0)
        def _():
            dg_ref[...] = jnp.zeros_like(dg_ref)

        dg_ref[...] += dg

    row = lambda i: (i, 0)
    return _call(
        body, name, (rows // tm,),
        [pl.BlockSpec((tm, n), row), pl.BlockSpec(w_block, lambda i: w_imap(0)), pl.BlockSpec((tm, d), row),
         pl.BlockSpec((1, d), lambda i: (0, 0)), pl.BlockSpec((tm, d), row)],
        [pl.BlockSpec((tm, d), row), pl.BlockSpec((tm, d), row), pl.BlockSpec((1, d), lambda i: (0, 0))],
        [SDS((rows, d), F32), SDS((rows, d), BF), SDS((1, d), F32)],
        semantics=("arbitrary",))(dy, w_arr, h, g, dres)


def _mm_tn(x, dy, packed, out_imap, tk, tn, *, name):
    s_len, k = x.shape
    n = dy.shape[1]

    def body(x_ref, dy_ref, _, o_ref):
        o_ref[0] = _dot_tn(x_ref[...], dy_ref[...].astype(BF))

    return _call(
        body, name, (k // tk, n // tn),
        [pl.BlockSpec((s_len, tk), lambda i, j: (0, i)), pl.BlockSpec((s_len, tn), lambda i, j: (0, j)), ANY],
        pl.BlockSpec((1, tk, tn), out_imap), SDS(packed.shape, packed.dtype),
        semantics=("parallel", "parallel"), aliases={2: 0})(x, dy, packed)


def _loss_head(y, tgt, *, tm):
    rows, d = y.shape

    def body(y_ref, t_ref, dh_ref, dhbf_ref, loss_ref):
        err = y_ref[...] - t_ref[...]
        dh = err * (1.0 / d)
        dh_ref[...] = dh
        dhbf_ref[...] = dh.astype(BF)

        @pl.when(pl.program_id(0) == 0)
        def _():
            loss_ref[...] = jnp.zeros_like(loss_ref)

        loss_ref[...] += 0.5 * jnp.sum(jnp.mean(err * err, axis=-1, keepdims=True), axis=0, keepdims=True)

    row = lambda i: (i, 0)
    return _call(
        body, "loss_head", (rows // tm,), [pl.BlockSpec((tm, d), row), pl.BlockSpec((tm, d), row)],
        [pl.BlockSpec((tm, d), row), pl.BlockSpec((tm, d), row), pl.BlockSpec((1, 1), lambda i: (0, 0))],
        [SDS((rows, d), F32), SDS((rows, d), BF), SDS((1, 1), F32)], semantics=("arbitrary",))(y, tgt)


def _hs(h):
    return slice(HEAD * h, HEAD * (h + 1))


def _softmax_rows(s):
    e = jnp.exp(s - jnp.max(s, axis=-1, keepdims=True))
    return e * (1.0 / jnp.sum(e, axis=-1, keepdims=True))


def _scaled_bf16(qn):
    return (qn * QK_SCALE).astype(BF)


def _mem_fwd(mq, mk, mv, gq):
    outs, probs = [], []
    for h in range(KV_HEADS):
        xh, _ = _rms(mq[:, _hs(h)])
        p = _softmax_rows(_dot_nt(_scaled_bf16(xh * gq), mk[:, _hs(h)])).astype(BF)
        probs.append(p)
        outs.append(_dot(p, mv[:, _hs(h)]))
    return jnp.concatenate(outs, axis=-1), jnp.concatenate(probs, axis=-1)


def _mem_bwd(mq, do, mk, mv, gq, probs):
    dqs, dks, dvs, dgq = [], [], [], 0.0
    mlen = mk.shape[0]
    for h in range(KV_HEADS):
        xh, r = _rms(mq[:, _hs(h)])
        qn = _scaled_bf16(xh * gq)
        p_bf = probs[:, h * mlen:(h + 1) * mlen]
        p = p_bf.astype(F32)
        doh = do[:, _hs(h)].astype(BF)
        dp = _dot_nt(doh, mv[:, _hs(h)])
        ds = (p * (dp - jnp.sum(p * dp, axis=-1, keepdims=True))).astype(BF)
        dq, dg = _rms_bwd(_dot(ds, mk[:, _hs(h)]) * QK_SCALE, xh, r, gq)
        dqs.append(dq)
        dgq = dgq + dg
        dks.append(_dot_tn(ds, qn))
        dvs.append(_dot_tn(p_bf, doh))
    cat = lambda xs: jnp.concatenate(xs, axis=-1)
    return cat(dqs), cat(dks), cat(dvs), dgq


def _mem_kv(mkv, gk):
    ks = []
    for h in range(KV_HEADS):
        xh, _ = _rms(mkv[:, _hs(h)])
        ks.append(xh * gk)
    return jnp.concatenate(ks, axis=-1).astype(BF), mkv[:, KVW:].astype(BF)


def _mem_kv_bwd(mkv, dmk, dmv, gk):
    dxs, dgk = [], 0.0
    for h in range(KV_HEADS):
        xh, r = _rms(mkv[:, _hs(h)])
        dx, dg = _rms_bwd(dmk[:, _hs(h)], xh, r, gk)
        dxs.append(dx)
        dgk = dgk + dg
    return jnp.concatenate(dxs + [dmv], axis=-1), dgk


def _pool_select(col, gd, a2, a4, a8, a16):
    return jnp.where(col < gd, a2, jnp.where(col < 2 * gd, a4, jnp.where(col < 3 * gd, a8, a16)))


def _pool_count(t0, shape, gd):
    col = lax.broadcasted_iota(jnp.int32, shape, 1)
    t = t0 + lax.broadcasted_iota(jnp.int32, shape, 0)
    win = _pool_select(col, gd, *POOL_WINDOWS)
    return jnp.minimum(t + 1, win).astype(F32)


def _pool_diff(u, halo, t0, gd):
    c = jnp.concatenate([halo, u], axis=0)
    s2 = c + pltpu.roll(c, 1, 0)
    s4 = s2 + pltpu.roll(s2, 2, 0)
    s8 = s4 + pltpu.roll(s4, 4, 0)
    s16 = s8 + pltpu.roll(s8, 8, 0)
    col = lax.broadcasted_iota(jnp.int32, c.shape, 1)
    ws = _pool_select(col, gd, s2, s4, s8, s16)[HALO:]
    return ws / _pool_count(t0, u.shape, gd) - u


def _pool_diff_bwd(dd, dd_halo, t0, gd):
    t = dd.shape[0]
    z = jnp.concatenate([dd / _pool_count(t0, dd.shape, gd), dd_halo / _pool_count(t0 + t, dd_halo.shape, gd)], axis=0)
    n = z.shape[0]
    f2 = z + pltpu.roll(z, n - 1, 0)
    f4 = f2 + pltpu.roll(f2, n - 2, 0)
    f8 = f4 + pltpu.roll(f4, n - 4, 0)
    f16 = f8 + pltpu.roll(f8, n - 8, 0)
    col = lax.broadcasted_iota(jnp.int32, z.shape, 1)
    return _pool_select(col, gd, f2, f4, f8, f16)[:t] - dd


def _swa_bias(n):
    qi = lax.broadcasted_iota(jnp.int32, (WINDOW, 2 * WINDOW), 0)
    kj = lax.broadcasted_iota(jnp.int32, (WINDOW, 2 * WINDOW), 1)
    dist = qi + WINDOW - kj
    valid = (dist >= 0) & (dist < WINDOW) & ((kj >= WINDOW) | (n > 0))
    return dist.astype(F32), valid


def _slopes(qh):
    return [2.0 ** (-8.0 * (h + 1) / qh) for h in range(qh)]


def _swa_probs(qn, kk, dist, valid, slope, sink):
    s = _dot_nt(qn, kk) - slope * dist
    s = jnp.where(valid, s, NEG)
    m = jnp.maximum(jnp.max(s, axis=-1, keepdims=True), sink)
    e = jnp.exp(s - m)
    es = jnp.exp(sink - m)
    z = jnp.sum(e, axis=-1, keepdims=True) + es
    inv = 1.0 / z
    return e * inv, es * inv


def _stack_heads(a, kh, grp):
    return jnp.concatenate([a[:, _hs(h)] for h in range(kh * grp, (kh + 1) * grp)], axis=0)


def _swa_group(q, kh, grp, n, qh, sinks):
    heads = range(kh * grp, (kh + 1) * grp)
    dist, valid = _swa_bias(n)
    slopes = _slopes(qh)
    rows = lambda vals: jnp.concatenate([jnp.broadcast_to(v, (WINDOW, 1)) for v in vals], axis=0)
    slope = rows([jnp.full((1, 1), slopes[h], F32) for h in heads])
    sink = rows([sinks[:, h:h + 1] for h in heads])
    return (_stack_heads(q, kh, grp), slope, sink, jnp.concatenate([dist] * grp, axis=0),
            jnp.concatenate([valid] * grp, axis=0))


def _swa_fwd(q, kk, vv, gq, sinks, n, qh):
    grp = qh // KV_HEADS
    lane = lax.broadcasted_iota(jnp.int32, (WINDOW, LANE), 1)
    outs, probs, sink_probs = [], [], jnp.zeros((WINDOW, LANE), F32)
    for kh in range(KV_HEADS):
        qs, slope, sink, dist, valid = _swa_group(q, kh, grp, n, qh, sinks)
        xh, _ = _rms(qs)
        p, ps = _swa_probs(_scaled_bf16(xh * gq), kk[:, _hs(kh)], dist, valid, slope, sink)
        p = p.astype(BF)
        probs.append(p)
        o = _dot(p, vv[:, _hs(kh)])
        for g in range(grp):
            outs.append(o[g * WINDOW:(g + 1) * WINDOW])
            sink_probs = jnp.where(lane == kh * grp + g, ps[g * WINDOW:(g + 1) * WINDOW], sink_probs)
    return jnp.concatenate(outs, axis=-1), probs, sink_probs


def _swa_bwd(q, do, kk, vv, gq, probs, sink_probs, qh):
    grp = qh // KV_HEADS
    lane = lax.broadcasted_iota(jnp.int32, (1, LANE), 1)
    dqs, dks, dvs, dgq, dsk = [], [], [], 0.0, jnp.zeros((1, LANE), F32)
    do = do.astype(BF)
    for kh in range(KV_HEADS):
        heads = range(kh * grp, (kh + 1) * grp)
        xh, r = _rms(_stack_heads(q, kh, grp))
        qn = _scaled_bf16(xh * gq)
        p_bf = probs[kh]
        p = p_bf.astype(F32)
        ps = jnp.concatenate([sink_probs[:, h:h + 1] for h in heads], axis=0)
        dos = _stack_heads(do, kh, grp)
        dp = _dot_nt(dos, vv[:, _hs(kh)])
        delta = jnp.sum(p * dp, axis=-1, keepdims=True)
        ds = (p * (dp - delta)).astype(BF)
        dsink = ps * delta
        for g in range(grp):
            part = -jnp.sum(dsink[g * WINDOW:(g + 1) * WINDOW], axis=0, keepdims=True)
            dsk = dsk + jnp.where(lane == kh * grp + g, part, 0.0)
        dq, dg = _rms_bwd(_dot(ds, kk[:, _hs(kh)]) * QK_SCALE, xh, r, gq)
        dqs += [dq[g * WINDOW:(g + 1) * WINDOW] for g in range(grp)]
        dgq = dgq + dg
        dks.append(_dot_tn(ds, qn))
        dvs.append(_dot_tn(p_bf, dos))
    cat = lambda xs: jnp.concatenate(xs, axis=-1)
    return cat(dqs), cat(dks), cat(dvs), dgq, dsk


def _mixer_pool_fwd(proj, mkv, pbd, scale, gq, gk, *, name, tm):
    s_len, d = proj.shape
    main = d - KVW
    gd = main // len(POOL_WINDOWS)
    mlen = mkv.shape[0]
    hb = tm // HALO

    def body(u_ref, halo_ref, mq_ref, mkv_ref, pbd_ref, scale_ref, gq_ref, gk_ref, o_ref, mp_ref, mk_s, mv_s):
        i = pl.program_id(0)

        @pl.when(i == 0)
        def _():
            mk, mv = _mem_kv(mkv_ref[...], gk_ref[...])
            mk_s[...] = mk
            mv_s[...] = mv

        halo = jnp.where(i > 0, halo_ref[...], 0.0)
        dif = _pool_diff(u_ref[...], halo, i * tm, gd)
        mixed = _dot(dif.astype(BF), pbd_ref[...]) * scale_ref[...]
        mem, mp_ref[...] = _mem_fwd(mq_ref[...], mk_s[...], mv_s[...], gq_ref[...])
        o_ref[...] = jnp.concatenate([mixed, mem], axis=-1).astype(BF)

    full = lambda shape: pl.BlockSpec(shape, lambda i: (0,) * len(shape))
    return _call(
        body, name, (s_len // tm,),
        [pl.BlockSpec((tm, main), lambda i: (i, 0)),
         pl.BlockSpec((HALO, main), lambda i: (jnp.maximum(i * hb - 1, 0), 0)),
         pl.BlockSpec((tm, KVW), lambda i: (i, main // KVW)),
         full((mlen, 2 * KVW)), full((main, main)), full((1, main)), full((1, HEAD)), full((1, HEAD))],
        [pl.BlockSpec((tm, d), lambda i: (i, 0)), pl.BlockSpec((tm, KV_HEADS * mlen), lambda i: (i, 0))],
        [SDS((s_len, d), BF), SDS((s_len, KV_HEADS * mlen), BF)],
        scratch=[pltpu.VMEM((mlen, KVW), BF), pltpu.VMEM((mlen, KVW), BF)],
        semantics=("arbitrary",))(proj, proj, proj, mkv, pbd, scale, gq, gk)


def _mixer_pool_bwd(proj, dcat, mkv, pbd, scale, gq, gk, mem_probs, *, name, tm):
    s_len, d = proj.shape
    main = d - KVW
    gd = main // len(POOL_WINDOWS)
    mlen = mkv.shape[0]
    hb = tm // HALO
    nt = s_len // tm
    last_halo = s_len // HALO - 1

    def body(u_ref, halo_ref, mq_ref, do_ref, donext_ref, dom_ref, mkv_ref, pbd_ref, scale_ref, gq_ref, gk_ref, mp_ref,
             dproj_ref, dpbd_ref, dscale_ref, dmkv_ref, dgq_ref, dgk_ref, mk_s, mv_s, dmk_s, dmv_s):
        i = pl.program_id(0)

        @pl.when(i == 0)
        def _():
            mk, mv = _mem_kv(mkv_ref[...], gk_ref[...])
            mk_s[...] = mk
            mv_s[...] = mv
            dmk_s[...] = jnp.zeros_like(dmk_s)
            dmv_s[...] = jnp.zeros_like(dmv_s)
            dpbd_ref[...] = jnp.zeros_like(dpbd_ref)
            dscale_ref[...] = jnp.zeros_like(dscale_ref)
            dgq_ref[...] = jnp.zeros_like(dgq_ref)

        pbd = pbd_ref[...]
        scale = scale_ref[...]
        halo = jnp.where(i > 0, halo_ref[...], 0.0)
        dif = _pool_diff(u_ref[...], halo, i * tm, gd).astype(BF)
        do = do_ref[...]
        dscale_ref[...] += jnp.sum(do * _dot(dif, pbd), axis=0, keepdims=True)
        dmixed = (do * scale).astype(BF)
        dpbd_ref[...] += _dot_tn(dif, dmixed)
        dd = _dot_nt(dmixed, pbd)
        donext = jnp.where(i < nt - 1, donext_ref[...], 0.0)
        dd_halo = _dot_nt((donext * scale).astype(BF), pbd)
        du = _pool_diff_bwd(dd, dd_halo, i * tm, gd)

        dmq, dmk, dmv, dgq = _mem_bwd(mq_ref[...], dom_ref[...], mk_s[...], mv_s[...], gq_ref[...], mp_ref[...])
        dmk_s[...] += dmk
        dmv_s[...] += dmv
        dgq_ref[...] += dgq
        dproj_ref[...] = jnp.concatenate([du, dmq], axis=-1).astype(BF)

        @pl.when(i == nt - 1)
        def _():
            dmkv, dgk = _mem_kv_bwd(mkv_ref[...], dmk_s[...], dmv_s[...], gk_ref[...])
            dmkv_ref[...] = dmkv
            dgk_ref[...] = dgk

    full = lambda shape: pl.BlockSpec(shape, lambda i: (0,) * len(shape))
    return _call(
        body, name, (nt,),
        [pl.BlockSpec((tm, main), lambda i: (i, 0)),
         pl.BlockSpec((HALO, main), lambda i: (jnp.maximum(i * hb - 1, 0), 0)),
         pl.BlockSpec((tm, KVW), lambda i: (i, main // KVW)),
         pl.BlockSpec((tm, main), lambda i: (i, 0)),
         pl.BlockSpec((HALO, main), lambda i: (jnp.minimum((i + 1) * hb, last_halo), 0)),
         pl.BlockSpec((tm, KVW), lambda i: (i, main // KVW)),
         full((mlen, 2 * KVW)), full((main, main)), full((1, main)), full((1, HEAD)), full((1, HEAD)),
         pl.BlockSpec((tm, KV_HEADS * mlen), lambda i: (i, 0))],
        [pl.BlockSpec((tm, d), lambda i: (i, 0)), full((main, main)), full((1, main)), full((mlen, 2 * KVW)),
         full((1, HEAD)), full((1, HEAD))],
        [SDS((s_len, d), BF), SDS((main, main), F32), SDS((1, main), F32), SDS((mlen, 2 * KVW), F32),
         SDS((1, HEAD), F32), SDS((1, HEAD), F32)],
        scratch=[pltpu.VMEM((mlen, KVW), BF), pltpu.VMEM((mlen, KVW), BF), pltpu.VMEM((mlen, KVW), F32),
                 pltpu.VMEM((mlen, KVW), F32)],
        semantics=("arbitrary",))(proj, proj, proj, dcat, dcat, dcat, mkv, pbd, scale, gq, gk, mem_probs)


def _mem_attn_fwd(proj, mkv, gq, gk, cat, *, name, tm):
    s_len, d = proj.shape
    main = d - KVW
    mlen = mkv.shape[0]

    def body(mq_ref, mkv_ref, gq_ref, gk_ref, _, o_ref, mp_ref, mk_s, mv_s):
        @pl.when(pl.program_id(0) == 0)
        def _():
            mk, mv = _mem_kv(mkv_ref[...], gk_ref[...])
            mk_s[...] = mk
            mv_s[...] = mv

        mem, mp_ref[...] = _mem_fwd(mq_ref[...], mk_s[...], mv_s[...], gq_ref[...])
        o_ref[...] = mem.astype(BF)

    full = lambda shape: pl.BlockSpec(shape, lambda i: (0,) * len(shape))
    memcol = lambda i: (i, main // KVW)
    return _call(
        body, name, (s_len // tm,),
        [pl.BlockSpec((tm, KVW), memcol), full((mlen, 2 * KVW)), full((1, HEAD)), full((1, HEAD)), ANY],
        [pl.BlockSpec((tm, KVW), memcol), pl.BlockSpec((tm, KV_HEADS * mlen), lambda i: (i, 0))],
        [SDS((s_len, d), BF), SDS((s_len, KV_HEADS * mlen), BF)],
        scratch=[pltpu.VMEM((mlen, KVW), BF), pltpu.VMEM((mlen, KVW), BF)],
        semantics=("arbitrary",), aliases={4: 0})(proj, mkv, gq, gk, cat)


def _mem_attn_bwd(proj, dcat, mkv, gq, gk, mem_probs, dproj, *, name, tm):
    s_len, d = proj.shape
    main = d - KVW
    mlen = mkv.shape[0]
    nt = s_len // tm

    def body(mq_ref, dom_ref, mkv_ref, gq_ref, gk_ref, mp_ref, _, dproj_ref, dmkv_ref, dgq_ref, dgk_ref,
             mk_s, mv_s, dmk_s, dmv_s):
        i = pl.program_id(0)

        @pl.when(i == 0)
        def _():
            mk, mv = _mem_kv(mkv_ref[...], gk_ref[...])
            mk_s[...] = mk
            mv_s[...] = mv
            dmk_s[...] = jnp.zeros_like(dmk_s)
            dmv_s[...] = jnp.zeros_like(dmv_s)
            dgq_ref[...] = jnp.zeros_like(dgq_ref)

        dmq, dmk, dmv, dgq = _mem_bwd(mq_ref[...], dom_ref[...], mk_s[...], mv_s[...], gq_ref[...], mp_ref[...])
        dmk_s[...] += dmk
        dmv_s[...] += dmv
        dgq_ref[...] += dgq
        dproj_ref[...] = dmq.astype(BF)

        @pl.when(i == nt - 1)
        def _():
            dmkv, dgk = _mem_kv_bwd(mkv_ref[...], dmk_s[...], dmv_s[...], gk_ref[...])
            dmkv_ref[...] = dmkv
            dgk_ref[...] = dgk

    full = lambda shape: pl.BlockSpec(shape, lambda i: (0,) * len(shape))
    memcol = lambda i: (i, main // KVW)
    return _call(
        body, name, (nt,),
        [pl.BlockSpec((tm, KVW), memcol), pl.BlockSpec((tm, KVW), memcol), full((mlen, 2 * KVW)), full((1, HEAD)),
         full((1, HEAD)), pl.BlockSpec((tm, KV_HEADS * mlen), lambda i: (i, 0)), ANY],
        [pl.BlockSpec((tm, KVW), memcol), full((mlen, 2 * KVW)), full((1, HEAD)), full((1, HEAD))],
        [SDS((s_len, d), BF), SDS((mlen, 2 * KVW), F32), SDS((1, HEAD), F32), SDS((1, HEAD), F32)],
        scratch=[pltpu.VMEM((mlen, KVW), BF), pltpu.VMEM((mlen, KVW), BF), pltpu.VMEM((mlen, KVW), F32),
                 pltpu.VMEM((mlen, KVW), F32)],
        semantics=("arbitrary",), aliases={6: 0})(proj, dcat, mkv, gq, gk, mem_probs, dproj)


def _mixer_swa_fwd(proj, kn, v, gqs, sinks, *, name):
    s_len, d = proj.shape
    main = d - KVW
    qh = main // HEAD
    tm = WINDOW
    prow = qh // KV_HEADS * tm

    def body(q_ref, kp_ref, kc_ref, vp_ref, vc_ref, gqs_ref, sinks_ref, o_ref, p_ref, ps_ref):
        n = pl.program_id(0)
        kk = jnp.concatenate([kp_ref[...], kc_ref[...]], axis=0)
        vv = jnp.concatenate([vp_ref[...], vc_ref[...]], axis=0)
        att, probs, sink_probs = _swa_fwd(q_ref[...], kk, vv, gqs_ref[...], sinks_ref[...], n, qh)
        for kh in range(KV_HEADS):
            p_ref[0, kh] = probs[kh]
        ps_ref[...] = sink_probs
        o_ref[...] = att.astype(BF)

    full = lambda shape: pl.BlockSpec(shape, lambda i: (0,) * len(shape))
    prev = lambda i: (jnp.maximum(i - 1, 0), 0)
    cur = lambda i: (i, 0)
    return _call(
        body, name, (s_len // tm,),
        [pl.BlockSpec((tm, main), cur), pl.BlockSpec((tm, KVW), prev), pl.BlockSpec((tm, KVW), cur),
         pl.BlockSpec((tm, KVW), prev), pl.BlockSpec((tm, KVW), cur), full((1, HEAD)), full((1, LANE))],
        [pl.BlockSpec((tm, main), cur), pl.BlockSpec((1, KV_HEADS, prow, 2 * tm), lambda i: (i, 0, 0, 0)),
         pl.BlockSpec((tm, LANE), cur)],
        [SDS((s_len, d), BF), SDS((s_len // tm, KV_HEADS, prow, 2 * tm), BF), SDS((s_len, LANE), F32)],
        semantics=("parallel",))(proj, kn, kn, v, v, gqs, sinks)


def _mixer_swa_bwd(proj, dcat, kn, v, gqs, probs, sink_probs, *, name):
    s_len, d = proj.shape
    main = d - KVW
    qh = main // HEAD
    tm = WINDOW
    nt = s_len // tm
    prow = qh // KV_HEADS * tm

    def body(q_ref, do_ref, kp_ref, kc_ref, vp_ref, vc_ref, gqs_ref, p_ref, ps_ref,
             dproj_ref, dk_ref, dv_ref, dgqs_ref, dsinks_ref):
        n = pl.program_id(0)

        @pl.when(n == 0)
        def _():
            dk_ref[...] = jnp.zeros_like(dk_ref)
            dv_ref[...] = jnp.zeros_like(dv_ref)
            dgqs_ref[...] = jnp.zeros_like(dgqs_ref)
            dsinks_ref[...] = jnp.zeros_like(dsinks_ref)

        kk = jnp.concatenate([kp_ref[...], kc_ref[...]], axis=0)
        vv = jnp.concatenate([vp_ref[...], vc_ref[...]], axis=0)
        dq, dkk, dvv, dgqs, dsk = _swa_bwd(q_ref[...], do_ref[...], kk, vv, gqs_ref[...],
                                           [p_ref[0, kh] for kh in range(KV_HEADS)], ps_ref[...], qh)
        prev = pl.ds(pl.multiple_of(jnp.maximum(n - 1, 0) * tm, tm), tm)
        own = pl.ds(pl.multiple_of(n * tm, tm), tm)
        dk_ref[prev, :] += dkk[:tm]
        dk_ref[own, :] += dkk[tm:]
        dv_ref[prev, :] += dvv[:tm]
        dv_ref[own, :] += dvv[tm:]
        dgqs_ref[...] += dgqs
        dsinks_ref[...] += dsk
        dproj_ref[...] = dq.astype(BF)

    full = lambda shape: pl.BlockSpec(shape, lambda i: (0,) * len(shape))
    prev_b = lambda i: (jnp.maximum(i - 1, 0), 0)
    cur = lambda i: (i, 0)
    return _call(
        body, name, (nt,),
        [pl.BlockSpec((tm, main), cur), pl.BlockSpec((tm, main), cur),
         pl.BlockSpec((tm, KVW), prev_b), pl.BlockSpec((tm, KVW), cur),
         pl.BlockSpec((tm, KVW), prev_b), pl.BlockSpec((tm, KVW), cur),
         full((1, HEAD)), pl.BlockSpec((1, KV_HEADS, prow, 2 * tm), lambda i: (i, 0, 0, 0)),
         pl.BlockSpec((tm, LANE), cur)],
        [pl.BlockSpec((tm, main), cur), full((s_len, KVW)), full((s_len, KVW)), full((1, HEAD)), full((1, LANE))],
        [SDS((s_len, d), BF), SDS((s_len, KVW), F32), SDS((s_len, KVW), F32), SDS((1, HEAD), F32),
         SDS((1, LANE), F32)],
        semantics=("arbitrary",))(proj, dcat, kn, kn, v, v, gqs, probs, sink_probs)


def _kv_prep(kv, gk, *, tm):
    s_len = kv.shape[0]

    def body(kv_ref, gk_ref, k_ref, v_ref):
        k, v = _mem_kv(kv_ref[...], gk_ref[...])
        k_ref[...] = k
        v_ref[...] = v

    row = lambda i: (i, 0)
    return _call(
        body, "kv_prep", (s_len // tm,),
        [pl.BlockSpec((tm, 2 * KVW), row), pl.BlockSpec((1, HEAD), lambda i: (0, 0))],
        [pl.BlockSpec((tm, KVW), row), pl.BlockSpec((tm, KVW), row)],
        [SDS((s_len, KVW), BF), SDS((s_len, KVW), BF)], semantics=("parallel",))(kv, gk)


def _kv_bwd(kv, dks, dvs, gk, *, tm):
    s_len = kv.shape[0]
    nl = len(dks)

    def body(*refs):
        kv_ref, gk_ref = refs[0], refs[1]
        dk_refs, dv_refs = refs[2:2 + nl], refs[2 + nl:2 + 2 * nl]
        dkv_ref, dgk_ref = refs[2 + 2 * nl], refs[3 + 2 * nl]
        dk, dv = dk_refs[0][...], dv_refs[0][...]
        for t in range(1, nl):
            dk = dk + dk_refs[t][...]
            dv = dv + dv_refs[t][...]
        dkv, dgk = _mem_kv_bwd(kv_ref[...], dk, dv, gk_ref[...])
        dkv_ref[...] = dkv.astype(BF)

        @pl.when(pl.program_id(0) == 0)
        def _():
            dgk_ref[...] = jnp.zeros_like(dgk_ref)

        dgk_ref[...] += dgk

    row = lambda i: (i, 0)
    one = pl.BlockSpec((1, HEAD), lambda i: (0, 0))
    return _call(
        body, "kv_bwd", (s_len // tm,),
        [pl.BlockSpec((tm, 2 * KVW), row), one] + [pl.BlockSpec((tm, KVW), row)] * (2 * nl),
        [pl.BlockSpec((tm, 2 * KVW), row), one],
        [SDS((s_len, 2 * KVW), BF), SDS((1, HEAD), F32)], semantics=("arbitrary",))(kv, gk, *dks, *dvs)


def _place():
    x, y, c = lax.axis_index("x"), lax.axis_index("y"), lax.axis_index("c")
    flips = [(1 - x, y), (x, 1 - y), (1 - x, 1 - y)]
    return x, y, c, flips


def _remote(src, dst, send_sem, recv_sem, to):
    return pltpu.make_async_remote_copy(src_ref=src, dst_ref=dst, send_sem=send_sem, recv_sem=recv_sem,
                                        device_id=to, device_id_type=MESH)


def _gather_copies(p_refs, wg_refs, send, recv):
    x, y, c, flips = _place()
    chip = 2 * x + y
    cps = []
    for j, (fx, fy) in enumerate(flips):
        for b in range(2):
            half = p_refs[b].shape[0] // 2
            mine = pl.ds(c * half, half)
            cps.append(_remote(p_refs[b].at[mine, :], wg_refs[b].at[chip, mine, :], send.at[2 * j + b],
                               recv.at[2 * j + b], (fx, fy, c)))
    return cps, cps


def _forward_copies(p_refs, wg_refs, send, recv):
    x, y, c, flips = _place()
    chip = 2 * x + y
    sib = (x, y, 1 - c)
    sends, arrivals = [], []
    for b in range(2):
        half = p_refs[b].shape[0] // 2
        own = _remote(p_refs[b], wg_refs[b].at[chip], send.at[b], recv.at[b], sib)
        sends.append(own)
        arrivals.append(own)
        for j, (fx, fy) in enumerate(flips):
            k = 2 + 3 * b + j
            landed = wg_refs[b].at[2 * fx + fy, pl.ds(c * half, half), :]
            other = wg_refs[b].at[2 * fx + fy, pl.ds((1 - c) * half, half), :]
            sends.append(_remote(landed, landed, send.at[k], recv.at[k], sib))
            arrivals.append(_remote(other, other, send.at[k], recv.at[k], sib))
    return sends, arrivals


def _swap_copies(g_refs, r_refs, send, recv):
    x, y, c, _ = _place()
    cps = []
    for b in range(2):
        half = g_refs[b].shape[1] // 2
        cps.append(_remote(g_refs[b].at[:, pl.ds((1 - c) * half, half), :], r_refs[b], send.at[b], recv.at[b],
                           (x, y, 1 - c)))
    return cps, cps


def _split_start(make_copies, n_sems, bufs, after, fresh, *, name):
    def body(a1, a2, b1, b2, after_ref, send, recv, *outs):
        for cp in make_copies((a1, a2), (b1, b2), send, recv)[0]:
            cp.start()
        outs[4][...] = jnp.zeros_like(outs[4])

    extra_shape = () if fresh is None else (pltpu.HBM(fresh, F32),)
    extra_spec = () if fresh is None else (HBM,)
    return pl.pallas_call(
        body, name=name,
        out_shape=(pltpu.SemaphoreType.DMA((n_sems,)), pltpu.SemaphoreType.DMA((n_sems,)))
        + tuple(pltpu.HBM(b.shape, b.dtype) for b in bufs) + (SDS((8, LANE), F32),) + extra_shape,
        in_specs=(HBM, HBM, HBM, HBM, ANY), out_specs=(SEM, SEM, HBM, HBM, HBM, HBM, VMEM_WHOLE) + extra_spec,
        input_output_aliases={0: 2, 1: 3, 2: 4, 3: 5},
        compiler_params=pltpu.CompilerParams(has_side_effects=SIDE_EFFECT))(*[_in_hbm(b) for b in bufs], after)


def _split_wait(make_copies, started, after, *, name):
    send, recv, bufs = started[0], started[1], started[2:6]

    def body(a1, a2, b1, b2, send_ref, recv_ref, after_ref, *outs):
        sends, arrivals = make_copies((a1, a2), (b1, b2), send_ref, recv_ref)
        for cp in arrivals:
            cp.wait_recv()
        for cp in sends:
            cp.wait_send()

    return pl.pallas_call(
        body, name=name, out_shape=tuple(pltpu.HBM(b.shape, b.dtype) for b in bufs),
        in_specs=(HBM, HBM, HBM, HBM, SEM, SEM, ANY), out_specs=(HBM, HBM, HBM, HBM),
        input_output_aliases={0: 0, 1: 1, 2: 2, 3: 3},
        compiler_params=pltpu.CompilerParams(has_side_effects=SIDE_EFFECT))(*bufs, send, recv, after)


def _gather_small(ps):
    def body(ps_ref, o_ref, send, recv):
        x, y, c, flips = _place()
        chip = 2 * x + y
        o_ref[chip] = ps_ref[...]
        cps = [_remote(ps_ref, o_ref.at[chip], send.at[j], recv.at[j], (fx, fy, c))
               for j, (fx, fy) in enumerate(flips)]
        for cp in cps:
            cp.start()
        for j, (fx, fy) in enumerate(flips):
            _remote(ps_ref, o_ref.at[2 * fx + fy], send.at[j], recv.at[j], (fx, fy, c)).wait_recv()
        for cp in cps:
            cp.wait_send()

    return pl.pallas_call(
        body, name="gather_small", in_specs=[VMEM_WHOLE], out_specs=VMEM_WHOLE,
        out_shape=SDS((N_CHIPS,) + ps.shape, ps.dtype),
        scratch_shapes=[pltpu.SemaphoreType.DMA((3,)), pltpu.SemaphoreType.DMA((3,))])(ps)


def _sum_sibling(g, r, place, *, tm, name):
    n_sh, half, w = r.shape
    nt = half // tm

    def body(place_ref, g_ref, r_ref, pbf_ref, own_ref):
        s = pl.program_id(1)
        p = g_ref[0] + r_ref[0]
        pbf_ref[0] = p.astype(BF)

        @pl.when(s == place_ref[1])
        def _():
            own_ref[...] = p

    return _call(
        body, name, (nt, n_sh),
        [pl.BlockSpec((1, tm, w), lambda i, s, pr: (s, pr[0] * nt + i, 0)),
         pl.BlockSpec((1, tm, w), lambda i, s, pr: (s, i, 0))],
        [pl.BlockSpec((1, tm, w), lambda i, s, pr: (s, i, 0)), pl.BlockSpec((tm, w), lambda i, s, pr: (i, 0))],
        [SDS((n_sh, half, w), BF), SDS((half, w), F32)],
        semantics=("arbitrary", "arbitrary"), prefetch=1)(place, g, r)


def _rs_copies(p_refs, land_refs, send, recv):
    _, _, c, flips = _place()
    cps = []
    for j, (fx, fy) in enumerate(flips):
        for b in range(2):
            cps.append(_remote(p_refs[b].at[2 * fx + fy], land_refs[b].at[j], send.at[2 * j + b], recv.at[2 * j + b],
                               (fx, fy, c)))
    return cps, cps


def _sum_chips(own, r, full, layer, place, *, tm, name):
    half, w = own.shape

    def body(place_ref, own_ref, r_ref, _, o_ref):
        o_ref[0, 0] = ((own_ref[...] + r_ref[0].astype(F32)) + r_ref[1].astype(F32)) + r_ref[2].astype(F32)

    return _call(
        body, name, (half // tm,),
        [pl.BlockSpec((tm, w), lambda i, pr: (i, 0)), pl.BlockSpec((3, tm, w), lambda i, pr: (0, i, 0)), ANY],
        pl.BlockSpec((1, 1, tm, w), lambda i, pr: (layer, pr[0], i, 0)), SDS(full.shape, F32),
        semantics=("parallel",), prefetch=1, aliases={3: 0})(place, own, r, full)


def _share_with_sibling(f1, f2, lo, hi, *, name):
    def body(_, __, o1_ref, o2_ref, send, recv):
        x, y, c, _ = _place()
        lay, mine, other = pl.ds(lo, hi - lo), pl.ds(c, 1), pl.ds(1 - c, 1)
        cps = [_remote(o1_ref.at[lay, mine], o1_ref.at[lay, mine], send.at[0], recv.at[0], (x, y, 1 - c)),
               _remote(o2_ref.at[lay, mine], o2_ref.at[lay, mine], send.at[1], recv.at[1], (x, y, 1 - c))]
        for cp in cps:
            cp.start()
        for cp in cps:
            cp.wait_send()
        _remote(o1_ref.at[lay, other], o1_ref.at[lay, other], send.at[0], recv.at[0], (x, y, 1 - c)).wait_recv()
        _remote(o2_ref.at[lay, other], o2_ref.at[lay, other], send.at[1], recv.at[1], (x, y, 1 - c)).wait_recv()

    return pl.pallas_call(
        body, name=name, in_specs=[ANY, ANY], out_specs=[ANY, ANY],
        out_shape=[SDS(f1.shape, f1.dtype), SDS(f2.shape, f2.dtype)], input_output_aliases={0: 0, 1: 1},
        scratch_shapes=[pltpu.SemaphoreType.DMA((2,)), pltpu.SemaphoreType.DMA((2,))])(f1, f2)


def _allreduce_small(sg):
    rows, w = sg.shape
    half = rows // 2
    assert half % 8 == 0

    def body(sg_ref, o_ref, sib_buf, part, slots, send, recv):
        x, y, c, flips = _place()
        chip = 2 * x + y
        sib = (x, y, 1 - c)
        mine = pl.ds(pl.multiple_of(c * half, 8), half)
        other = pl.ds(pl.multiple_of((1 - c) * half, 8), half)
        to_sib = _remote(sg_ref.at[other, :], sib_buf, send.at[0], recv.at[0], sib)
        to_sib.start()
        to_sib.wait_recv()
        part[...] = sg_ref[mine, :] + sib_buf[...]
        slots[chip] = part[...]
        to_chips = [_remote(part, slots.at[chip], send.at[1 + j], recv.at[1 + j], (fx, fy, c))
                    for j, (fx, fy) in enumerate(flips)]
        for cp in to_chips:
            cp.start()
        for j, (fx, fy) in enumerate(flips):
            _remote(part, slots.at[2 * fx + fy], send.at[1 + j], recv.at[1 + j], (fx, fy, c)).wait_recv()
        o_ref[mine, :] = ((slots[0] + slots[1]) + slots[2]) + slots[3]
        back = _remote(o_ref.at[mine, :], o_ref.at[mine, :], send.at[4], recv.at[4], sib)
        back.start()
        _remote(o_ref.at[other, :], o_ref.at[other, :], send.at[4], recv.at[4], sib).wait_recv()
        for cp in [to_sib, back] + to_chips:
            cp.wait_send()

    return pl.pallas_call(
        body, name="allreduce_small", in_specs=[VMEM_WHOLE], out_specs=VMEM_WHOLE, out_shape=SDS((rows, w), F32),
        scratch_shapes=[pltpu.VMEM((half, w), F32), pltpu.VMEM((half, w), F32), pltpu.VMEM((N_CHIPS, half, w), F32),
                        pltpu.SemaphoreType.DMA((5,)), pltpu.SemaphoreType.DMA((5,))])(sg)


def _adamw_math(g, w, m, v):
    mn = ADAM_B1 * m + (1.0 - ADAM_B1) * g
    vn = ADAM_B2 * v + (1.0 - ADAM_B2) * (g * g)
    m_hat = mn / (1.0 - ADAM_B1 ** ADAM_STEP)
    v_hat = vn / (1.0 - ADAM_B2 ** ADAM_STEP)
    return -ADAM_LR * (m_hat / (jnp.sqrt(v_hat) + ADAM_EPS) + ADAM_WD * w), mn, vn


def _adamw_small(gs, ws, ms, vs):
    n = len(gs)

    def body(*refs):
        for k in range(n):
            g, w, m, v = (refs[t * n + k][...] for t in range(4))
            refs[4 * n + k][...], refs[5 * n + k][...], refs[6 * n + k][...] = _adamw_math(g, w, m, v)

    out = pl.pallas_call(
        body, name="adamw_small", in_specs=[VMEM_WHOLE] * (4 * n), out_specs=[VMEM_WHOLE] * (3 * n),
        out_shape=[SDS(w.shape, F32) for w in ws] * 3)(*gs, *ws, *ms, *vs)
    return out[:n], out[n:2 * n], out[2 * n:]


def _adamw(g_arr, layer0, g_off, per_layer, w, m, v, *, name, tm, layers=None, prev=None):
    rows, cols = w.shape
    assert g_off % tm == 0 and per_layer % tm == 0 and rows % per_layer == 0
    npl = per_layer // tm
    lo, hi = layers or (0, rows // per_layer)
    base = lo * npl

    def body(g_ref, w_ref, m_ref, v_ref, *rest):
        go_ref, d_ref, mo_ref, vo_ref = rest[-4:]
        g = g_ref[0]
        go_ref[...] = g
        d_ref[...], mo_ref[...], vo_ref[...] = _adamw_math(g, w_ref[...], m_ref[...], v_ref[...])

    blk = pl.BlockSpec((tm, cols), lambda i: (base + i, 0))
    extra = list(prev or [])
    return _call(
        body, name, ((hi - lo) * npl,),
        [pl.BlockSpec((1, tm, cols), lambda i: (layer0 + lo + i // npl, g_off // tm + i % npl, 0)), blk, blk, blk]
        + [ANY] * len(extra),
        [blk] * 4,
        [SDS((rows, cols), F32)] * 4, semantics=("parallel",),
        aliases={4 + k: k for k in range(len(extra))})(g_arr, w, m, v, *extra)


def _pack_small(parts, width):
    flat = jnp.concatenate([p.reshape(-1).astype(F32) for p in parts])
    rows = -(-flat.shape[0] // (16 * width)) * 16
    return jnp.pad(flat, (0, rows * width - flat.shape[0])).reshape(rows, width)


def _unpack_small(packed, shapes):
    flat = packed.reshape(-1)
    out, off = [], 0
    for shp in shapes:
        size = 1
        for n in shp:
            size *= n
        out.append(flat[off:off + size].reshape(shp))
        off += size
    return out


def _block_diag(pw):
    g, c, _ = pw.shape
    eye = jnp.eye(g, dtype=pw.dtype)
    return (eye[:, None, :, None] * pw[:, :, None, :]).reshape(g * c, g * c)


def _diag_blocks(full, g):
    c = full.shape[0] // g
    return jnp.stack([full[i * c:(i + 1) * c, i * c:(i + 1) * c] for i in range(g)])


def kernel(x, mem, norm_mix, w_in, pool_w, pool_scale, kv_norm, w_kv, k_norm, q_norm, sinks, mem_norm, w_mem_kv, mem_q_norm, mem_k_norm, w_out, norm_mlp, w_up, w_down, loss_target, m_norm_mix, m_w_in, m_pool_w, m_pool_scale, m_kv_norm, m_w_kv, m_k_norm, m_q_norm, m_sinks, m_mem_norm, m_w_mem_kv, m_mem_q_norm, m_mem_k_norm, m_w_out, m_norm_mlp, m_w_up, m_w_down, v_norm_mix, v_w_in, v_pool_w, v_pool_scale, v_kv_norm, v_w_kv, v_k_norm, v_q_norm, v_sinks, v_mem_norm, v_w_mem_kv, v_mem_q_norm, v_mem_k_norm, v_w_out, v_norm_mlp, v_w_up, v_w_down):
    s_len, d = x.shape[1], x.shape[2]
    n_layers, n_pool = norm_mix.shape[0], pool_w.shape[0]
    n_swa = n_layers - n_pool
    main = d - KVW
    qh = main // HEAD
    ff = w_down.shape[1] * N_CHIPS
    dq = d // N_CHIPS
    assert w_up.shape[2] == d and ff == N_CHIPS * d and w_kv.shape[1] == 2 * KVW
    tm = min(512, s_len)
    tmb = min(1024, s_len)
    tm_mem = mem.shape[1]

    cx, cy, cc = lax.axis_index("x"), lax.axis_index("y"), lax.axis_index("c")
    chip = 2 * cx + cy
    place = jnp.stack([cc, chip]).astype(jnp.int32)

    off_down, off_up, off_in, off_out = 0, d, 2 * d, 2 * d + dq
    rows1 = off_out + dq
    off_mkv, off_kv = 0, dq
    rows2 = 2 * dq

    ps = jnp.pad(pool_scale, ((0, 8 - n_pool), (0, 2 * LANE - pool_scale.shape[1])))

    def packed_weights(l):
        p1 = jnp.concatenate([w_down[l], w_up[l], w_in[l], w_out[l]]).astype(BF)
        p2 = jnp.concatenate([w_mem_kv[l], w_kv] if l == n_pool else [w_mem_kv[l]]).astype(BF)
        return p1, p2

    def gather_start(l, after):
        p1, p2 = packed_weights(l)
        bufs = (p1, p2, lax.empty((N_CHIPS,) + p1.shape, BF), lax.empty((N_CHIPS,) + p2.shape, BF))
        return _split_start(_gather_copies, 6, bufs, after, None, name=f"gather_start_{l}")

    def gather_land(l, started, after):
        bufs = _split_wait(_gather_copies, started, after, name=f"gather_wait_{l}")
        return _split_start(_forward_copies, 8, bufs, place, None, name=f"forward_start_{l}")

    def gather_finish(l, forwarding, after):
        bufs = _split_wait(_forward_copies, forwarding, after, name=f"forward_wait_{l}")
        return bufs[2], bufs[3]

    def w_rows(arr, off, nrows, width):
        assert off % nrows == 0
        return (arr, (N_CHIPS, nrows, width), lambda j: (0, off // nrows, 0))

    row = lambda a: a.reshape(1, -1)
    h = x.reshape(s_len, d)
    memx = mem.reshape(tm_mem, d)
    tgt = loss_target.reshape(s_len, d)
    pbd = [_block_diag(pool_w[l]).astype(BF) for l in range(n_pool)]
    sinks_pad = [jnp.pad(row(sinks[j]), ((0, 0), (0, LANE - qh))) for j in range(n_swa)]

    w_in_l, w_out_l, w_down_l, w_up_all_l, w_mkv_l = [], [], [], [], []
    w_kv_g = None
    first = gather_start(0, place)
    psg = _gather_small(ps + first[6][0, 0])
    pool_scale_full = jnp.concatenate([psg[k, :n_pool, :pool_scale.shape[1]] for k in range(N_CHIPS)], axis=1)
    forwarding = gather_land(0, first, psg)
    travelling = gather_start(1, forwarding[6]) if n_layers > 1 else None
    saved, probs, sink_probs, mem_probs = [], {}, {}, {}
    kv = hn_kv = kn = vsh = None
    for l in range(n_layers):
        wg1, wg2 = gather_finish(l, forwarding, h if l else forwarding[6])
        w_in_l.append(w_rows(wg1, off_in, dq, d))
        w_out_l.append(w_rows(wg1, off_out, dq, d))
        w_down_l.append(w_rows(wg1, off_down, d, d))
        w_up_all_l.append((wg1, (N_CHIPS, d, d), lambda j: (0, off_up // d, 0)))
        w_mkv_l.append(w_rows(wg2, off_mkv, dq, 2 * KVW))
        g_mix = row(norm_mix[l])
        if travelling is not None:
            g_mix = g_mix + travelling[6][0, 0]
        if l == n_pool:
            w_kv_g = w_rows(wg2, off_kv, dq, 2 * KVW)
            kv, hn_kv = _norm_mm(h, row(kv_norm), w_kv_g, 1, 2 * KVW, act=False, name="kv_proj", tm=tmb)
            kn, vsh = _kv_prep(kv, row(k_norm), tm=tmb)
        h0 = h
        proj, xn = _norm_mm(h0, g_mix, w_in_l[l], 1, d, act=False, name=f"in_proj_{l}", tm=tmb)
        mkv, memn = _norm_mm(memx, row(mem_norm[l]), w_mkv_l[l], 1, 2 * KVW, act=False, name=f"mem_kv_{l}", tm=tm_mem)
        if l < n_pool:
            cat, mem_probs[l] = _mixer_pool_fwd(proj, mkv, pbd[l], row(pool_scale_full[l]), row(mem_q_norm[l]),
                                                row(mem_k_norm[l]), name=f"mixer_fwd_{l}", tm=tmb)
        else:
            j = l - n_pool
            cat, probs[l], sink_probs[l] = _mixer_swa_fwd(proj, kn, vsh, row(q_norm[j]), sinks_pad[j],
                                                          name=f"mixer_fwd_{l}")
            cat, mem_probs[l] = _mem_attn_fwd(proj, mkv, row(mem_q_norm[l]), row(mem_k_norm[l]), cat,
                                              name=f"mem_attn_fwd_{l}", tm=tmb)
        h1 = _mm_res(h0, cat, w_out_l[l], name=f"out_proj_{l}", tm=tmb)
        hh, xm = _norm_mm(h1, row(norm_mlp[l]), w_up_all_l[l], N_CHIPS, d, act=True, name=f"mlp_up_{l}", tm=tm)
        after = None
        if travelling is not None:
            forwarding = gather_land(l + 1, travelling, hh)
            travelling = gather_start(l + 2, forwarding[6]) if l + 2 < n_layers else None
            after = forwarding[6]
        h = _mm_res(h1, hh, w_down_l[l], name=f"mlp_down_{l}", tm=tm, after=after)
        saved.append((h0, proj, xn, mkv, memn, cat, h1, hh, xm))

    dh, dh_bf, loss_part = _loss_head(h, tgt, tm=tmb)

    half1, half2 = rows1 // 2, rows2 // 2
    g1 = lax.empty((N_CHIPS, rows1, d), F32)
    pending = {}
    swapping = None
    tk = min(512, d)

    def reduce_begin(l, swapped, after):
        g1_l, g2_l, r1, r2 = _split_wait(_swap_copies, swapped, after, name=f"swap_wait_{l}")
        pb1, own1 = _sum_sibling(g1_l, r1, place, tm=_tile(half1, 640), name=f"sum_sibling_a_{l}")
        pb2, own2 = _sum_sibling(g2_l, r2, place, tm=_tile(half2, 256), name=f"sum_sibling_b_{l}")
        bufs = (pb1, pb2, lax.empty((3, half1, d), BF), lax.empty((3, half2, 2 * KVW), BF))
        return _split_start(_rs_copies, 6, bufs, place, None, name=f"reduce_start_{l}"), own1, own2
    zeros_mem = jnp.zeros((tm_mem, d), F32)

    def rows_map(off, nrows, tkk):
        per = nrows // tkk
        return lambda i, j: (i // per, off // tkk + i % per, 0)

    def cols_map(off, tkk):
        return lambda i, j: (j, off // tkk + i, 0)

    d_norm_mix, d_norm_mlp, d_mem_norm = [None] * n_layers, [None] * n_layers, [None] * n_layers
    d_mem_q, d_mem_k = [None] * n_layers, [None] * n_layers
    d_pool_w, d_pool_scale = [None] * n_pool, [None] * n_pool
    d_q_norm, d_sinks = [None] * n_swa, [None] * n_swa
    dks, dvs = [], []
    d_kv_norm = d_k_norm = None
    for l in reversed(range(n_layers)):
        h0, proj, xn, mkv, memn, cat, h1, hh, xm = saved[l]
        g2 = jnp.zeros((N_CHIPS, rows2, 2 * KVW), F32)
        g1 = _mm_tn(hh, dh_bf, g1, rows_map(off_down, d, tk), tk, d, name=f"dw_down_{l}")
        du = _mm_nt_relu2(dh_bf, hh, w_down_l[l], N_CHIPS, name=f"d_mlp_act_{l}", tm=tm)
        g1 = _mm_tn(xm, du, g1, cols_map(off_up, tk), tk, d, name=f"dw_up_{l}")
        g_mlp = row(norm_mlp[l])
        if swapping is not None:
            pending[swapping[0]] = reduce_begin(*swapping, after=g1)
            g_mlp = g_mlp + pending[swapping[0]][0][6][0, 0]
        dh1, dh1_bf, d_norm_mlp[l] = _mm_nt_normbwd(du, w_up_all_l[l], N_CHIPS, h1, g_mlp, dh,
                                                    name=f"d_mlp_in_{l}", tm=tm)
        tkq = min(tk, dq)
        g1 = _mm_tn(cat, dh1_bf, g1, rows_map(off_out, dq, tkq), tkq, d, name=f"dw_out_{l}")
        dcat = _mm_nt(dh1_bf, w_out_l[l], d, name=f"d_cat_{l}", tm=tmb)
        if l < n_pool:
            dproj, dpbd, dscale, dmkv, d_mem_q[l], d_mem_k[l] = _mixer_pool_bwd(
                proj, dcat, mkv, pbd[l], row(pool_scale_full[l]), row(mem_q_norm[l]), row(mem_k_norm[l]),
                mem_probs[l], name=f"mixer_bwd_{l}", tm=tmb)
            d_pool_w[l] = _diag_blocks(dpbd, len(POOL_WINDOWS))
            d_pool_scale[l] = dscale
        else:
            j = l - n_pool
            dproj, dk, dv, d_q_norm[j], dsk = _mixer_swa_bwd(proj, dcat, kn, vsh, row(q_norm[j]), probs[l],
                                                             sink_probs[l], name=f"mixer_bwd_{l}")
            dproj, dmkv, d_mem_q[l], d_mem_k[l] = _mem_attn_bwd(
                proj, dcat, mkv, row(mem_q_norm[l]), row(mem_k_norm[l]), mem_probs[l], dproj,
                name=f"mem_attn_bwd_{l}", tm=tmb)
            d_sinks[j] = dsk[0, :qh]
            dks.append(dk)
            dvs.append(dv)
        g1 = _mm_tn(xn, dproj, g1, rows_map(off_in, dq, tkq), tkq, d, name=f"dw_in_{l}")
        dh, dh_bf, d_norm_mix[l] = _mm_nt_normbwd(dproj, w_in_l[l], 1, h0, row(norm_mix[l]), dh1,
                                                  name=f"d_in_{l}", tm=tmb)
        g2 = _mm_tn(memn, dmkv, g2, rows_map(off_mkv, dq, tkq), tkq, 2 * KVW, name=f"dw_mem_kv_{l}")
        _, _, d_mem_norm[l] = _mm_nt_normbwd(dmkv, w_mkv_l[l], 1, memx, row(mem_norm[l]), zeros_mem,
                                             name=f"d_mem_norm_{l}", tm=tm_mem)
        if l == n_pool:
            dkv, d_k_norm = _kv_bwd(kv, dks, dvs, row(k_norm), tm=tmb)
            g2 = _mm_tn(hn_kv, dkv, g2, rows_map(off_kv, dq, tkq), tkq, 2 * KVW, name="dw_kv")
            dh, dh_bf, d_kv_norm = _mm_nt_normbwd(dkv, w_kv_g, 1, h0, row(kv_norm), dh, name="d_kv_in", tm=tmb)
        bufs = (g1, g2, lax.empty((N_CHIPS, half1, d), F32), lax.empty((N_CHIPS, half2, 2 * KVW), F32))
        swapping = (l, _split_start(_swap_copies, 2, bufs, place, (N_CHIPS, rows1, d) if l > 0 else None,
                                    name=f"swap_start_{l}"))
        g1 = swapping[1][7] if l > 0 else None
    grad_x = dh.reshape(x.shape)

    small_names = ["norm_mix", "pool_w", "pool_scale", "kv_norm", "k_norm", "q_norm", "sinks", "mem_norm",
                   "mem_q_norm", "mem_k_norm", "norm_mlp"]
    small_grads = {
        "norm_mix": jnp.concatenate(d_norm_mix), "pool_w": jnp.stack(d_pool_w),
        "pool_scale": jnp.concatenate(d_pool_scale), "kv_norm": d_kv_norm[0], "k_norm": d_k_norm[0],
        "q_norm": jnp.concatenate(d_q_norm), "sinks": jnp.stack(d_sinks), "mem_norm": jnp.concatenate(d_mem_norm),
        "mem_q_norm": jnp.concatenate(d_mem_q), "mem_k_norm": jnp.concatenate(d_mem_k),
        "norm_mlp": jnp.concatenate(d_norm_mlp)}
    width = d
    sg = _pack_small([small_grads[n] for n in small_names] + [loss_part], width)
    sg = sg + swapping[1][6][0, 0]
    sg = _allreduce_small(sg)
    pending[0] = reduce_begin(*swapping, after=sg)
    *unpacked, loss_sum = _unpack_small(sg, [small_grads[n].shape for n in small_names] + [(1, 1)])
    loss = loss_sum[0, 0]
    reduced = dict(zip(small_names, unpacked))
    psw = pool_scale.shape[1]
    reduced["pool_scale"] = lax.dynamic_slice_in_dim(reduced["pool_scale"], chip * psw, psw, axis=1)
    params = dict(norm_mix=(norm_mix, m_norm_mix, v_norm_mix), pool_w=(pool_w, m_pool_w, v_pool_w),
                  pool_scale=(pool_scale, m_pool_scale, v_pool_scale), kv_norm=(kv_norm, m_kv_norm, v_kv_norm),
                  k_norm=(k_norm, m_k_norm, v_k_norm), q_norm=(q_norm, m_q_norm, v_q_norm),
                  sinks=(sinks, m_sinks, v_sinks), mem_norm=(mem_norm, m_mem_norm, v_mem_norm),
                  mem_q_norm=(mem_q_norm, m_mem_q_norm, v_mem_q_norm),
                  mem_k_norm=(mem_k_norm, m_mem_k_norm, v_mem_k_norm), norm_mlp=(norm_mlp, m_norm_mlp, v_norm_mlp))
    flat2 = lambda a: a.reshape(-1, a.shape[-1])
    grads = [flat2(reduced[n].reshape(params[n][0].shape)) for n in small_names]
    grads[0] = grads[0] + pending[0][0][6][0, 0]
    res = _adamw_small(grads, *[[flat2(params[n][t]) for n in small_names] for t in range(3)])
    small = {n: [grads[k].reshape(params[n][0].shape)] + [r[k].reshape(params[n][0].shape) for r in res]
             for k, n in enumerate(small_names)}

    full1 = lax.empty((n_layers, 2, half1, d), F32)
    full2 = lax.empty((n_layers, 2, half2, 2 * KVW), F32)
    kinds = (("w_down", 1, off_down, d, w_down, m_w_down, v_w_down), ("w_up", 1, off_up, d, w_up, m_w_up, v_w_up),
             ("w_in", 1, off_in, dq, w_in, m_w_in, v_w_in), ("w_out", 1, off_out, dq, w_out, m_w_out, v_w_out),
             ("w_mem_kv", 2, off_mkv, dq, w_mem_kv, m_w_mem_kv, v_w_mem_kv))
    big = {}
    first = 1 if n_layers > 1 and n_pool > 0 else 0
    after = res[0][0]
    for lo, hi in ((first, n_layers), (0, first)):
        if lo == hi:
            continue
        for l in reversed(range(lo, hi)):
            exchange, own1, own2 = pending[l]
            _, _, x1, x2 = _split_wait(_rs_copies, exchange, after, name=f"reduce_wait_{l}")
            full1 = _sum_chips(own1, x1, full1, l, place, tm=_tile(half1, 640), name=f"sum_chips_a_{l}")
            full2 = _sum_chips(own2, x2, full2, l, place, tm=_tile(half2, 256), name=f"sum_chips_b_{l}")
        full1, full2 = _share_with_sibling(full1, full2, lo, hi, name=f"share_with_sibling_{lo}")
        views = (None, full1.reshape(n_layers, rows1, d), full2.reshape(n_layers, rows2, 2 * KVW))
        for name, which, off, per, w_, m_, v_ in kinds:
            cols = views[which].shape[2]
            big[name] = _adamw(views[which], 0, off, per, w_.reshape(-1, cols), m_.reshape(-1, cols),
                               v_.reshape(-1, cols), name=f"adamw_{name}_{lo}", tm=min(512, per), layers=(lo, hi),
                               prev=big.get(name))
        if lo <= n_pool < hi:
            big["w_kv"] = _adamw(views[2], n_pool, off_kv, dq, w_kv, m_w_kv, v_w_kv, name="adamw_w_kv",
                                 tm=min(256, dq))
        after = big["w_mem_kv"][1]
    shapes_big = dict(w_down=w_down.shape, w_up=w_up.shape, w_in=w_in.shape, w_out=w_out.shape,
                      w_mem_kv=w_mem_kv.shape, w_kv=w_kv.shape)
    big = {n: [r.reshape(shapes_big[n]) for r in big[n]] for n in big}

    order = ["norm_mix", "w_in", "pool_w", "pool_scale", "kv_norm", "w_kv", "k_norm", "q_norm", "sinks", "mem_norm",
             "w_mem_kv", "mem_q_norm", "mem_k_norm", "w_out", "norm_mlp", "w_up", "w_down"]
    out = {**big, **small}
    return (loss, grad_x, *[out[n][0] for n in order], *[out[n][1] for n in order],
            *[out[n][2] for n in order], *[out[n][3] for n in order])
```

```python
import functools

import jax
import jax.numpy as jnp
from jax import lax
from jax.experimental import pallas as pl
from jax.experimental.pallas import tpu as pltpu

F32, BF = jnp.float32, jnp.bfloat16
SDS = jax.ShapeDtypeStruct
MESH = pl.DeviceIdType.MESH
ANY = pl.BlockSpec(memory_space=pl.ANY)
HBM = pl.BlockSpec(memory_space=pltpu.HBM)
SEM = pl.BlockSpec(memory_space=pltpu.SEMAPHORE)
VMEM_WHOLE = pl.BlockSpec(memory_space=pltpu.VMEM)
SIDE_EFFECT = pltpu.SideEffectType.DATAFLOW_SIDE_EFFECTING


def _in_hbm(a):
    return pltpu.with_memory_space_constraint(a, pltpu.HBM)


EPS = 1e-6
HEAD = 64
KV_HEADS = 4
KVW = KV_HEADS * HEAD
WINDOW = 128
POOL_WINDOWS = (2, 4, 8, 16)
HALO = 16
QK_SCALE = HEAD ** -0.5
NEG = float(jnp.finfo(jnp.float32).min)
N_CHIPS = 4
LANE = 128

ADAM_LR, ADAM_B1, ADAM_B2, ADAM_EPS, ADAM_WD, ADAM_STEP = 0.001, 0.9, 0.999, 1e-08, 0.01, 10

VMEM_LIMIT_MB = 56


def _call(body, name, grid, in_specs, out_specs, out_shape, *, scratch=(), semantics=None, aliases=None,
          prefetch=0):
    params = pltpu.CompilerParams(dimension_semantics=semantics, vmem_limit_bytes=VMEM_LIMIT_MB << 20)
    if prefetch:
        spec = pltpu.PrefetchScalarGridSpec(num_scalar_prefetch=prefetch, grid=grid, in_specs=in_specs,
                                            out_specs=out_specs, scratch_shapes=list(scratch))
        return pl.pallas_call(body, name=name, grid_spec=spec, out_shape=out_shape,
                              input_output_aliases=aliases or {}, compiler_params=params)
    return pl.pallas_call(body, name=name, grid=grid, in_specs=in_specs, out_specs=out_specs, out_shape=out_shape,
                          scratch_shapes=list(scratch), input_output_aliases=aliases or {}, compiler_params=params)


def _tile(n, pref):
    return max(t for t in range(8, min(n, pref) + 1, 8) if n % t == 0)


def _dot(a, b):
    return jnp.dot(a, b, preferred_element_type=F32)


def _dot_nt(a, b):
    return lax.dot_general(a, b, (((1,), (1,)), ((), ())), preferred_element_type=F32)


def _dot_tn(a, b):
    return lax.dot_general(a, b, (((0,), (0,)), ((), ())), preferred_element_type=F32)


def _rms(x):
    r = lax.rsqrt(jnp.mean(x * x, axis=-1, keepdims=True) + EPS)
    return x * r, r


def _rms_bwd(dy, xh, r, g):
    dg = jnp.sum(dy * xh, axis=0, keepdims=True)
    dyg = dy * g
    dx = r * (dyg - xh * jnp.mean(dyg * xh, axis=-1, keepdims=True))
    return dx, dg


def _norm_mm(h, g, w, nj, tn, *, act, name, tm):
    w_arr, w_block, w_imap = w
    rows, d = h.shape

    def body(h_ref, g_ref, w_ref, y_ref, xn_ref):
        xh, _ = _rms(h_ref[...])
        xn = (xh * g_ref[...]).astype(BF)
        xn_ref[...] = xn
        for j in range(nj):
            u = _dot(xn, w_ref[j] if nj > 1 else w_ref[...].reshape(d, tn))
            if act:
                a = jnp.maximum(u, 0.0)
                y_ref[:, j * tn:(j + 1) * tn] = (a * a).astype(BF)
            else:
                y_ref[:, j * tn:(j + 1) * tn] = u

    assert nj == 1 or w_block[0] == nj
    return _call(
        body, name, (rows // tm,),
        [pl.BlockSpec((tm, d), lambda i: (i, 0)), pl.BlockSpec((1, d), lambda i: (0, 0)),
         pl.BlockSpec(w_block, lambda i: w_imap(0))],
        [pl.BlockSpec((tm, nj * tn), lambda i: (i, 0)), pl.BlockSpec((tm, d), lambda i: (i, 0))],
        [SDS((rows, nj * tn), BF if act else F32), SDS((rows, d), BF)],
        semantics=("parallel",))(h, g, w_arr)


def _mm_res(res, a, w, *, name, tm, after=None):
    w_arr, w_block, w_imap = w
    rows, k = a.shape
    n = res.shape[1]

    def body(res_ref, a_ref, w_ref, *rest):
        rest[-1][...] = res_ref[...] + _dot(a_ref[...], w_ref[...].reshape(k, n))

    extra = [] if after is None else [after]
    return _call(
        body, name, (rows // tm,),
        [pl.BlockSpec((tm, n), lambda i: (i, 0)), pl.BlockSpec((tm, k), lambda i: (i, 0)),
         pl.BlockSpec(w_block, lambda i: w_imap(0))] + [ANY] * len(extra),
        pl.BlockSpec((tm, n), lambda i: (i, 0)), SDS((rows, n), F32), semantics=("parallel",))(res, a, w_arr, *extra)


def _mm_nt(dy, w, k, *, name, tm):
    w_arr, w_block, w_imap = w
    rows, n = dy.shape

    def body(dy_ref, w_ref, o_ref):
        o_ref[...] = _dot_nt(dy_ref[...], w_ref[...].reshape(k, n))

    return _call(
        body, name, (rows // tm,),
        [pl.BlockSpec((tm, n), lambda i: (i, 0)), pl.BlockSpec(w_block, lambda i: w_imap(0))],
        pl.BlockSpec((tm, k), lambda i: (i, 0)), SDS((rows, k), F32), semantics=("parallel",))(dy, w_arr)


def _mm_nt_relu2(dh, hh, w, nj, *, name, tm):
    w_arr, w_block, w_imap = w
    rows, d = dh.shape
    tk = hh.shape[1] // nj

    def body(dh_ref, hh_ref, w_ref, o_ref):
        dh_t = dh_ref[...]
        for j in range(nj):
            cols = slice(j * tk, (j + 1) * tk)
            dhh = _dot_nt(dh_t, w_ref[j])
            o_ref[:, cols] = (dhh * (2.0 * jnp.sqrt(hh_ref[:, cols].astype(F32)))).astype(BF)

    assert w_block[0] == nj
    return _call(
        body, name, (rows // tm,),
        [pl.BlockSpec((tm, d), lambda i: (i, 0)), pl.BlockSpec((tm, nj * tk), lambda i: (i, 0)),
         pl.BlockSpec(w_block, lambda i: w_imap(0))],
        pl.BlockSpec((tm, nj * tk), lambda i: (i, 0)), SDS((rows, nj * tk), BF),
        semantics=("parallel",))(dh, hh, w_arr)


def _mm_nt_normbwd(dy, w, nsplit, h, g, dres, *, name, tm):
    w_arr, w_block, w_imap = w
    rows, n = dy.shape
    d = h.shape[1]
    ns = n // nsplit

    def body(dy_ref, w_ref, h_ref, g_ref, dres_ref, o_ref, obf_ref, dg_ref):
        if nsplit == 1:
            dxn = _dot_nt(dy_ref[...].astype(BF), w_ref[...].reshape(d, n))
        else:
            dxn = _dot_nt(dy_ref[:, 0:ns].astype(BF), w_ref[0])
            for s in range(1, nsplit):
                dxn += _dot_nt(dy_ref[:, s * ns:(s + 1) * ns].astype(BF), w_ref[s])
        xh, r = _rms(h_ref[...])
        dx, dg = _rms_bwd(dxn, xh, r, g_ref[...])
        out = dres_ref[...] + dx
        o_ref[...] = out
        obf_ref[...] = out.astype(BF)

        @pl.when(pl.program_id(0) == 0)
        def _():
            dg_ref[...] = jnp.zeros_like(dg_ref)

        dg_ref[...] += dg

    row = lambda i: (i, 0)
    return _call(
        body, name, (rows // tm,),
        [pl.BlockSpec((tm, n), row), pl.BlockSpec(w_block, lambda i: w_imap(0)), pl.BlockSpec((tm, d), row),
         pl.BlockSpec((1, d), lambda i: (0, 0)), pl.BlockSpec((tm, d), row)],
        [pl.BlockSpec((tm, d), row), pl.BlockSpec((tm, d), row), pl.BlockSpec((1, d), lambda i: (0, 0))],
        [SDS((rows, d), F32), SDS((rows, d), BF), SDS((1, d), F32)],
        semantics=("arbitrary",))(dy, w_arr, h, g, dres)


def _mm_tn(x, dy, packed, out_imap, tk, tn, *, name):
    s_len, k = x.shape
    n = dy.shape[1]

    def body(x_ref, dy_ref, _, o_ref):
        o_ref[0] = _dot_tn(x_ref[...], dy_ref[...].astype(BF))

    return _call(
        body, name, (k // tk, n // tn),
        [pl.BlockSpec((s_len, tk), lambda i, j: (0, i)), pl.BlockSpec((s_len, tn), lambda i, j: (0, j)), ANY],
        pl.BlockSpec((1, tk, tn), out_imap), SDS(packed.shape, packed.dtype),
        semantics=("parallel", "parallel"), aliases={2: 0})(x, dy, packed)


def _loss_head(y, tgt, *, tm):
    rows, d = y.shape

    def body(y_ref, t_ref, dh_ref, dhbf_ref, loss_ref):
        err = y_ref[...] - t_ref[...]
        dh = err * (1.0 / d)
        dh_ref[...] = dh
        dhbf_ref[...] = dh.astype(BF)

        @pl.when(pl.program_id(0) == 0)
        def _():
            loss_ref[...] = jnp.zeros_like(loss_ref)

        loss_ref[...] += 0.5 * jnp.sum(jnp.mean(err * err, axis=-1, keepdims=True), axis=0, keepdims=True)

    row = lambda i: (i, 0)
    return _call(
        body, "loss_head", (rows // tm,), [pl.BlockSpec((tm, d), row), pl.BlockSpec((tm, d), row)],
        [pl.BlockSpec((tm, d), row), pl.BlockSpec((tm, d), row), pl.BlockSpec((1, 1), lambda i: (0, 0))],
        [SDS((rows, d), F32), SDS((rows, d), BF), SDS((1, 1), F32)], semantics=("arbitrary",))(y, tgt)


def _hs(h):
    return slice(HEAD * h, HEAD * (h + 1))


def _softmax_rows(s):
    e = jnp.exp(s - jnp.max(s, axis=-1, keepdims=True))
    return e * (1.0 / jnp.sum(e, axis=-1, keepdims=True))


def _scaled_bf16(qn):
    return (qn * QK_SCALE).astype(BF)


def _mem_fwd(mq, mk, mv, gq):
    outs, probs = [], []
    for h in range(KV_HEADS):
        xh, _ = _rms(mq[:, _hs(h)])
        p = _softmax_rows(_dot_nt(_scaled_bf16(xh * gq), mk[:, _hs(h)])).astype(BF)
        probs.append(p)
        outs.append(_dot(p, mv[:, _hs(h)]))
    return jnp.concatenate(outs, axis=-1), jnp.concatenate(probs, axis=-1)


def _mem_bwd(mq, do, mk, mv, gq, probs):
    dqs, dks, dvs, dgq = [], [], [], 0.0
    mlen = mk.shape[0]
    for h in range(KV_HEADS):
        xh, r = _rms(mq[:, _hs(h)])
        qn = _scaled_bf16(xh * gq)
        p_bf = probs[:, h * mlen:(h + 1) * mlen]
        p = p_bf.astype(F32)
        doh = do[:, _hs(h)].astype(BF)
        dp = _dot_nt(doh, mv[:, _hs(h)])
        ds = (p * (dp - jnp.sum(p * dp, axis=-1, keepdims=True))).astype(BF)
        dq, dg = _rms_bwd(_dot(ds, mk[:, _hs(h)]) * QK_SCALE, xh, r, gq)
        dqs.append(dq)
        dgq = dgq + dg
        dks.append(_dot_tn(ds, qn))
        dvs.append(_dot_tn(p_bf, doh))
    cat = lambda xs: jnp.concatenate(xs, axis=-1)
    return cat(dqs), cat(dks), cat(dvs), dgq


def _mem_kv(mkv, gk):
    ks = []
    for h in range(KV_HEADS):
        xh, _ = _rms(mkv[:, _hs(h)])
        ks.append(xh * gk)
    return jnp.concatenate(ks, axis=-1).astype(BF), mkv[:, KVW:].astype(BF)


def _mem_kv_bwd(mkv, dmk, dmv, gk):
    dxs, dgk = [], 0.0
    for h in range(KV_HEADS):
        xh, r = _rms(mkv[:, _hs(h)])
        dx, dg = _rms_bwd(dmk[:, _hs(h)], xh, r, gk)
        dxs.append(dx)
        dgk = dgk + dg
    return jnp.concatenate(dxs + [dmv], axis=-1), dgk


def _pool_select(col, gd, a2, a4, a8, a16):
    return jnp.where(col < gd, a2, jnp.where(col < 2 * gd, a4, jnp.where(col < 3 * gd, a8, a16)))


def _pool_count(t0, shape, gd):
    col = lax.broadcasted_iota(jnp.int32, shape, 1)
    t = t0 + lax.broadcasted_iota(jnp.int32, shape, 0)
    win = _pool_select(col, gd, *POOL_WINDOWS)
    return jnp.minimum(t + 1, win).astype(F32)


def _pool_diff(u, halo, t0, gd):
    c = jnp.concatenate([halo, u], axis=0)
    s2 = c + pltpu.roll(c, 1, 0)
    s4 = s2 + pltpu.roll(s2, 2, 0)
    s8 = s4 + pltpu.roll(s4, 4, 0)
    s16 = s8 + pltpu.roll(s8, 8, 0)
    col = lax.broadcasted_iota(jnp.int32, c.shape, 1)
    ws = _pool_select(col, gd, s2, s4, s8, s16)[HALO:]
    return ws / _pool_count(t0, u.shape, gd) - u


def _pool_diff_bwd(dd, dd_halo, t0, gd):
    t = dd.shape[0]
    z = jnp.concatenate([dd / _pool_count(t0, dd.shape, gd), dd_halo / _pool_count(t0 + t, dd_halo.shape, gd)], axis=0)
    n = z.shape[0]
    f2 = z + pltpu.roll(z, n - 1, 0)
    f4 = f2 + pltpu.roll(f2, n - 2, 0)
    f8 = f4 + pltpu.roll(f4, n - 4, 0)
    f16 = f8 + pltpu.roll(f8, n - 8, 0)
    col = lax.broadcasted_iota(jnp.int32, z.shape, 1)
    return _pool_select(col, gd, f2, f4, f8, f16)[:t] - dd


def _swa_bias(n):
    qi = lax.broadcasted_iota(jnp.int32, (WINDOW, 2 * WINDOW), 0)
    kj = lax.broadcasted_iota(jnp.int32, (WINDOW, 2 * WINDOW), 1)
    dist = qi + WINDOW - kj
    valid = (dist >= 0) & (dist < WINDOW) & ((kj >= WINDOW) | (n > 0))
    return dist.astype(F32), valid


def _slopes(qh):
    return [2.0 ** (-8.0 * (h + 1) / qh) for h in range(qh)]


def _swa_probs(qn, kk, dist, valid, slope, sink):
    s = _dot_nt(qn, kk) - slope * dist
    s = jnp.where(valid, s, NEG)
    m = jnp.maximum(jnp.max(s, axis=-1, keepdims=True), sink)
    e = jnp.exp(s - m)
    es = jnp.exp(sink - m)
    z = jnp.sum(e, axis=-1, keepdims=True) + es
    inv = 1.0 / z
    return e * inv, es * inv


def _stack_heads(a, kh, grp):
    return jnp.concatenate([a[:, _hs(h)] for h in range(kh * grp, (kh + 1) * grp)], axis=0)


def _swa_group(q, kh, grp, n, qh, sinks):
    heads = range(kh * grp, (kh + 1) * grp)
    dist, valid = _swa_bias(n)
    slopes = _slopes(qh)
    rows = lambda vals: jnp.concatenate([jnp.broadcast_to(v, (WINDOW, 1)) for v in vals], axis=0)
    slope = rows([jnp.full((1, 1), slopes[h], F32) for h in heads])
    sink = rows([sinks[:, h:h + 1] for h in heads])
    return (_stack_heads(q, kh, grp), slope, sink, jnp.concatenate([dist] * grp, axis=0),
            jnp.concatenate([valid] * grp, axis=0))


def _swa_fwd(q, kk, vv, gq, sinks, n, qh):
    grp = qh // KV_HEADS
    lane = lax.broadcasted_iota(jnp.int32, (WINDOW, LANE), 1)
    outs, probs, sink_probs = [], [], jnp.zeros((WINDOW, LANE), F32)
    for kh in range(KV_HEADS):
        qs, slope, sink, dist, valid = _swa_group(q, kh, grp, n, qh, sinks)
        xh, _ = _rms(qs)
        p, ps = _swa_probs(_scaled_bf16(xh * gq), kk[:, _hs(kh)], dist, valid, slope, sink)
        p = p.astype(BF)
        probs.append(p)
        o = _dot(p, vv[:, _hs(kh)])
        for g in range(grp):
            outs.append(o[g * WINDOW:(g + 1) * WINDOW])
            sink_probs = jnp.where(lane == kh * grp + g, ps[g * WINDOW:(g + 1) * WINDOW], sink_probs)
    return jnp.concatenate(outs, axis=-1), probs, sink_probs


def _swa_bwd(q, do, kk, vv, gq, probs, sink_probs, qh):
    grp = qh // KV_HEADS
    lane = lax.broadcasted_iota(jnp.int32, (1, LANE), 1)
    dqs, dks, dvs, dgq, dsk = [], [], [], 0.0, jnp.zeros((1, LANE), F32)
    do = do.astype(BF)
    for kh in range(KV_HEADS):
        heads = range(kh * grp, (kh + 1) * grp)
        xh, r = _rms(_stack_heads(q, kh, grp))
        qn = _scaled_bf16(xh * gq)
        p_bf = probs[kh]
        p = p_bf.astype(F32)
        ps = jnp.concatenate([sink_probs[:, h:h + 1] for h in heads], axis=0)
        dos = _stack_heads(do, kh, grp)
        dp = _dot_nt(dos, vv[:, _hs(kh)])
        delta = jnp.sum(p * dp, axis=-1, keepdims=True)
        ds = (p * (dp - delta)).astype(BF)
        dsink = ps * delta
        for g in range(grp):
            part = -jnp.sum(dsink[g * WINDOW:(g + 1) * WINDOW], axis=0, keepdims=True)
            dsk = dsk + jnp.where(lane == kh * grp + g, part, 0.0)
        dq, dg = _rms_bwd(_dot(ds, kk[:, _hs(kh)]) * QK_SCALE, xh, r, gq)
        dqs += [dq[g * WINDOW:(g + 1) * WINDOW] for g in range(grp)]
        dgq = dgq + dg
        dks.append(_dot_tn(ds, qn))
        dvs.append(_dot_tn(p_bf, dos))
    cat = lambda xs: jnp.concatenate(xs, axis=-1)
    return cat(dqs), cat(dks), cat(dvs), dgq, dsk


def _mixer_pool_fwd(proj, mkv, pbd, scale, gq, gk, *, name, tm):
    s_len, d = proj.shape
    main = d - KVW
    gd = main // len(POOL_WINDOWS)
    mlen = mkv.shape[0]
    hb = tm // HALO

    def body(u_ref, halo_ref, mq_ref, mkv_ref, pbd_ref, scale_ref, gq_ref, gk_ref, o_ref, mp_ref, mk_s, mv_s):
        i = pl.program_id(0)

        @pl.when(i == 0)
        def _():
            mk, mv = _mem_kv(mkv_ref[...], gk_ref[...])
            mk_s[...] = mk
            mv_s[...] = mv

        halo = jnp.where(i > 0, halo_ref[...], 0.0)
        dif = _pool_diff(u_ref[...], halo, i * tm, gd)
        mixed = _dot(dif.astype(BF), pbd_ref[...]) * scale_ref[...]
        mem, mp_ref[...] = _mem_fwd(mq_ref[...], mk_s[...], mv_s[...], gq_ref[...])
        o_ref[...] = jnp.concatenate([mixed, mem], axis=-1).astype(BF)

    full = lambda shape: pl.BlockSpec(shape, lambda i: (0,) * len(shape))
    return _call(
        body, name, (s_len // tm,),
        [pl.BlockSpec((tm, main), lambda i: (i, 0)),
         pl.BlockSpec((HALO, main), lambda i: (jnp.maximum(i * hb - 1, 0), 0)),
         pl.BlockSpec((tm, KVW), lambda i: (i, main // KVW)),
         full((mlen, 2 * KVW)), full((main, main)), full((1, main)), full((1, HEAD)), full((1, HEAD))],
        [pl.BlockSpec((tm, d), lambda i: (i, 0)), pl.BlockSpec((tm, KV_HEADS * mlen), lambda i: (i, 0))],
        [SDS((s_len, d), BF), SDS((s_len, KV_HEADS * mlen), BF)],
        scratch=[pltpu.VMEM((mlen, KVW), BF), pltpu.VMEM((mlen, KVW), BF)],
        semantics=("arbitrary",))(proj, proj, proj, mkv, pbd, scale, gq, gk)


def _mixer_pool_bwd(proj, dcat, mkv, pbd, scale, gq, gk, mem_probs, *, name, tm):
    s_len, d = proj.shape
    main = d - KVW
    gd = main // len(POOL_WINDOWS)
    mlen = mkv.shape[0]
    hb = tm // HALO
    nt = s_len // tm
    last_halo = s_len // HALO - 1

    def body(u_ref, halo_ref, mq_ref, do_ref, donext_ref, dom_ref, mkv_ref, pbd_ref, scale_ref, gq_ref, gk_ref, mp_ref,
             dproj_ref, dpbd_ref, dscale_ref, dmkv_ref, dgq_ref, dgk_ref, mk_s, mv_s, dmk_s, dmv_s):
        i = pl.program_id(0)

        @pl.when(i == 0)
        def _():
            mk, mv = _mem_kv(mkv_ref[...], gk_ref[...])
            mk_s[...] = mk
            mv_s[...] = mv
            dmk_s[...] = jnp.zeros_like(dmk_s)
            dmv_s[...] = jnp.zeros_like(dmv_s)
            dpbd_ref[...] = jnp.zeros_like(dpbd_ref)
            dscale_ref[...] = jnp.zeros_like(dscale_ref)
            dgq_ref[...] = jnp.zeros_like(dgq_ref)

        pbd = pbd_ref[...]
        scale = scale_ref[...]
        halo = jnp.where(i > 0, halo_ref[...], 0.0)
        dif = _pool_diff(u_ref[...], halo, i * tm, gd).astype(BF)
        do = do_ref[...]
        dscale_ref[...] += jnp.sum(do * _dot(dif, pbd), axis=0, keepdims=True)
        dmixed = (do * scale).astype(BF)
        dpbd_ref[...] += _dot_tn(dif, dmixed)
        dd = _dot_nt(dmixed, pbd)
        donext = jnp.where(i < nt - 1, donext_ref[...], 0.0)
        dd_halo = _dot_nt((donext * scale).astype(BF), pbd)
        du = _pool_diff_bwd(dd, dd_halo, i * tm, gd)

        dmq, dmk, dmv, dgq = _mem_bwd(mq_ref[...], dom_ref[...], mk_s[...], mv_s[...], gq_ref[...], mp_ref[...])
        dmk_s[...] += dmk
        dmv_s[...] += dmv
        dgq_ref[...] += dgq
        dproj_ref[...] = jnp.concatenate([du, dmq], axis=-1).astype(BF)

        @pl.when(i == nt - 1)
        def _():
            dmkv, dgk = _mem_kv_bwd(mkv_ref[...], dmk_s[...], dmv_s[...], gk_ref[...])
            dmkv_ref[...] = dmkv
            dgk_ref[...] = dgk

    full = lambda shape: pl.BlockSpec(shape, lambda i: (0,) * len(shape))
    return _call(
        body, name, (nt,),
        [pl.BlockSpec((tm, main), lambda i: (i, 0)),
         pl.BlockSpec((HALO, main), lambda i: (jnp.maximum(i * hb - 1, 0), 0)),
         pl.BlockSpec((tm, KVW), lambda i: (i, main // KVW)),
         pl.BlockSpec((tm, main), lambda i: (i, 0)),
         pl.BlockSpec((HALO, main), lambda i: (jnp.minimum((i + 1) * hb, last_halo), 0)),
         pl.BlockSpec((tm, KVW), lambda i: (i, main // KVW)),
         full((mlen, 2 * KVW)), full((main, main)), full((1, main)), full((1, HEAD)), full((1, HEAD)),
         pl.BlockSpec((tm, KV_HEADS * mlen), lambda i: (i, 0))],
        [pl.BlockSpec((tm, d), lambda i: (i, 0)), full((main, main)), full((1, main)), full((mlen, 2 * KVW)),
         full((1, HEAD)), full((1, HEAD))],
        [SDS((s_len, d), BF), SDS((main, main), F32), SDS((1, main), F32), SDS((mlen, 2 * KVW), F32),
         SDS((1, HEAD), F32), SDS((1, HEAD), F32)],
        scratch=[pltpu.VMEM((mlen, KVW), BF), pltpu.VMEM((mlen, KVW), BF), pltpu.VMEM((mlen, KVW), F32),
                 pltpu.VMEM((mlen, KVW), F32)],
        semantics=("arbitrary",))(proj, proj, proj, dcat, dcat, dcat, mkv, pbd, scale, gq, gk, mem_probs)


def _mem_attn_fwd(proj, mkv, gq, gk, cat, *, name, tm):
    s_len, d = proj.shape
    main = d - KVW
    mlen = mkv.shape[0]

    def body(mq_ref, mkv_ref, gq_ref, gk_ref, _, o_ref, mp_ref, mk_s, mv_s):
        @pl.when(pl.program_id(0) == 0)
        def _():
            mk, mv = _mem_kv(mkv_ref[...], gk_ref[...])
            mk_s[...] = mk
            mv_s[...] = mv

        mem, mp_ref[...] = _mem_fwd(mq_ref[...], mk_s[...], mv_s[...], gq_ref[...])
        o_ref[...] = mem.astype(BF)

    full = lambda shape: pl.BlockSpec(shape, lambda i: (0,) * len(shape))
    memcol = lambda i: (i, main // KVW)
    return _call(
        body, name, (s_len // tm,),
        [pl.BlockSpec((tm, KVW), memcol), full((mlen, 2 * KVW)), full((1, HEAD)), full((1, HEAD)), ANY],
        [pl.BlockSpec((tm, KVW), memcol), pl.BlockSpec((tm, KV_HEADS * mlen), lambda i: (i, 0))],
        [SDS((s_len, d), BF), SDS((s_len, KV_HEADS * mlen), BF)],
        scratch=[pltpu.VMEM((mlen, KVW), BF), pltpu.VMEM((mlen, KVW), BF)],
        semantics=("arbitrary",), aliases={4: 0})(proj, mkv, gq, gk, cat)


def _mem_attn_bwd(proj, dcat, mkv, gq, gk, mem_probs, dproj, *, name, tm):
    s_len, d = proj.shape
    main = d - KVW
    mlen = mkv.shape[0]
    nt = s_len // tm

    def body(mq_ref, dom_ref, mkv_ref, gq_ref, gk_ref, mp_ref, _, dproj_ref, dmkv_ref, dgq_ref, dgk_ref,
             mk_s, mv_s, dmk_s, dmv_s):
        i = pl.program_id(0)

        @pl.when(i == 0)
        def _():
            mk, mv = _mem_kv(mkv_ref[...], gk_ref[...])
            mk_s[...] = mk
            mv_s[...] = mv
            dmk_s[...] = jnp.zeros_like(dmk_s)
            dmv_s[...] = jnp.zeros_like(dmv_s)
            dgq_ref[...] = jnp.zeros_like(dgq_ref)

        dmq, dmk, dmv, dgq = _mem_bwd(mq_ref[...], dom_ref[...], mk_s[...], mv_s[...], gq_ref[...], mp_ref[...])
        dmk_s[...] += dmk
        dmv_s[...] += dmv
        dgq_ref[...] += dgq
        dproj_ref[...] = dmq.astype(BF)

        @pl.when(i == nt - 1)
        def _():
            dmkv, dgk = _mem_kv_bwd(mkv_ref[...], dmk_s[...], dmv_s[...], gk_ref[...])
            dmkv_ref[...] = dmkv
            dgk_ref[...] = dgk

    full = lambda shape: pl.BlockSpec(shape, lambda i: (0,) * len(shape))
    memcol = lambda i: (i, main // KVW)
    return _call(
        body, name, (nt,),
        [pl.BlockSpec((tm, KVW), memcol), pl.BlockSpec((tm, KVW), memcol), full((mlen, 2 * KVW)), full((1, HEAD)),
         full((1, HEAD)), pl.BlockSpec((tm, KV_HEADS * mlen), lambda i: (i, 0)), ANY],
        [pl.BlockSpec((tm, KVW), memcol), full((mlen, 2 * KVW)), full((1, HEAD)), full((1, HEAD))],
        [SDS((s_len, d), BF), SDS((mlen, 2 * KVW), F32), SDS((1, HEAD), F32), SDS((1, HEAD), F32)],
        scratch=[pltpu.VMEM((mlen, KVW), BF), pltpu.VMEM((mlen, KVW), BF), pltpu.VMEM((mlen, KVW), F32),
                 pltpu.VMEM((mlen, KVW), F32)],
        semantics=("arbitrary",), aliases={6: 0})(proj, dcat, mkv, gq, gk, mem_probs, dproj)


def _mixer_swa_fwd(proj, kn, v, gqs, sinks, *, name):
    s_len, d = proj.shape
    main = d - KVW
    qh = main // HEAD
    tm = WINDOW
    prow = qh // KV_HEADS * tm

    def body(q_ref, kp_ref, kc_ref, vp_ref, vc_ref, gqs_ref, sinks_ref, o_ref, p_ref, ps_ref):
        n = pl.program_id(0)
        kk = jnp.concatenate([kp_ref[...], kc_ref[...]], axis=0)
        vv = jnp.concatenate([vp_ref[...], vc_ref[...]], axis=0)
        att, probs, sink_probs = _swa_fwd(q_ref[...], kk, vv, gqs_ref[...], sinks_ref[...], n, qh)
        for kh in range(KV_HEADS):
            p_ref[0, kh] = probs[kh]
        ps_ref[...] = sink_probs
        o_ref[...] = att.astype(BF)

    full = lambda shape: pl.BlockSpec(shape, lambda i: (0,) * len(shape))
    prev = lambda i: (jnp.maximum(i - 1, 0), 0)
    cur = lambda i: (i, 0)
    return _call(
        body, name, (s_len // tm,),
        [pl.BlockSpec((tm, main), cur), pl.BlockSpec((tm, KVW), prev), pl.BlockSpec((tm, KVW), cur),
         pl.BlockSpec((tm, KVW), prev), pl.BlockSpec((tm, KVW), cur), full((1, HEAD)), full((1, LANE))],
        [pl.BlockSpec((tm, main), cur), pl.BlockSpec((1, KV_HEADS, prow, 2 * tm), lambda i: (i, 0, 0, 0)),
         pl.BlockSpec((tm, LANE), cur)],
        [SDS((s_len, d), BF), SDS((s_len // tm, KV_HEADS, prow, 2 * tm), BF), SDS((s_len, LANE), F32)],
        semantics=("parallel",))(proj, kn, kn, v, v, gqs, sinks)


def _mixer_swa_bwd(proj, dcat, kn, v, gqs, probs, sink_probs, *, name):
    s_len, d = proj.shape
    main = d - KVW
    qh = main // HEAD
    tm = WINDOW
    nt = s_len // tm
    prow = qh // KV_HEADS * tm

    def body(q_ref, do_ref, kp_ref, kc_ref, vp_ref, vc_ref, gqs_ref, p_ref, ps_ref,
             dproj_ref, dk_ref, dv_ref, dgqs_ref, dsinks_ref):
        n = pl.program_id(0)

        @pl.when(n == 0)
        def _():
            dk_ref[...] = jnp.zeros_like(dk_ref)
            dv_ref[...] = jnp.zeros_like(dv_ref)
            dgqs_ref[...] = jnp.zeros_like(dgqs_ref)
            dsinks_ref[...] = jnp.zeros_like(dsinks_ref)

        kk = jnp.concatenate([kp_ref[...], kc_ref[...]], axis=0)
        vv = jnp.concatenate([vp_ref[...], vc_ref[...]], axis=0)
        dq, dkk, dvv, dgqs, dsk = _swa_bwd(q_ref[...], do_ref[...], kk, vv, gqs_ref[...],
                                           [p_ref[0, kh] for kh in range(KV_HEADS)], ps_ref[...], qh)
        prev = pl.ds(pl.multiple_of(jnp.maximum(n - 1, 0) * tm, tm), tm)
        own = pl.ds(pl.multiple_of(n * tm, tm), tm)
        dk_ref[prev, :] += dkk[:tm]
        dk_ref[own, :] += dkk[tm:]
        dv_ref[prev, :] += dvv[:tm]
        dv_ref[own, :] += dvv[tm:]
        dgqs_ref[...] += dgqs
        dsinks_ref[...] += dsk
        dproj_ref[...] = dq.astype(BF)

    full = lambda shape: pl.BlockSpec(shape, lambda i: (0,) * len(shape))
    prev_b = lambda i: (jnp.maximum(i - 1, 0), 0)
    cur = lambda i: (i, 0)
    return _call(
        body, name, (nt,),
        [pl.BlockSpec((tm, main), cur), pl.BlockSpec((tm, main), cur),
         pl.BlockSpec((tm, KVW), prev_b), pl.BlockSpec((tm, KVW), cur),
         pl.BlockSpec((tm, KVW), prev_b), pl.BlockSpec((tm, KVW), cur),
         full((1, HEAD)), pl.BlockSpec((1, KV_HEADS, prow, 2 * tm), lambda i: (i, 0, 0, 0)),
         pl.BlockSpec((tm, LANE), cur)],
        [pl.BlockSpec((tm, main), cur), full((s_len, KVW)), full((s_len, KVW)), full((1, HEAD)), full((1, LANE))],
        [SDS((s_len, d), BF), SDS((s_len, KVW), F32), SDS((s_len, KVW), F32), SDS((1, HEAD), F32),
         SDS((1, LANE), F32)],
        semantics=("arbitrary",))(proj, dcat, kn, kn, v, v, gqs, probs, sink_probs)


def _kv_prep(kv, gk, *, tm):
    s_len = kv.shape[0]

    def body(kv_ref, gk_ref, k_ref, v_ref):
        k, v = _mem_kv(kv_ref[...], gk_ref[...])
        k_ref[...] = k
        v_ref[...] = v

    row = lambda i: (i, 0)
    return _call(
        body, "kv_prep", (s_len // tm,),
        [pl.BlockSpec((tm, 2 * KVW), row), pl.BlockSpec((1, HEAD), lambda i: (0, 0))],
        [pl.BlockSpec((tm, KVW), row), pl.BlockSpec((tm, KVW), row)],
        [SDS((s_len, KVW), BF), SDS((s_len, KVW), BF)], semantics=("parallel",))(kv, gk)


def _kv_bwd(kv, dks, dvs, gk, *, tm):
    s_len = kv.shape[0]
    nl = len(dks)

    def body(*refs):
        kv_ref, gk_ref = refs[0], refs[1]
        dk_refs, dv_refs = refs[2:2 + nl], refs[2 + nl:2 + 2 * nl]
        dkv_ref, dgk_ref = refs[2 + 2 * nl], refs[3 + 2 * nl]
        dk, dv = dk_refs[0][...], dv_refs[0][...]
        for t in range(1, nl):
            dk = dk + dk_refs[t][...]
            dv = dv + dv_refs[t][...]
        dkv, dgk = _mem_kv_bwd(kv_ref[...], dk, dv, gk_ref[...])
        dkv_ref[...] = dkv.astype(BF)

        @pl.when(pl.program_id(0) == 0)
        def _():
            dgk_ref[...] = jnp.zeros_like(dgk_ref)

        dgk_ref[...] += dgk

    row = lambda i: (i, 0)
    one = pl.BlockSpec((1, HEAD), lambda i: (0, 0))
    return _call(
        body, "kv_bwd", (s_len // tm,),
        [pl.BlockSpec((tm, 2 * KVW), row), one] + [pl.BlockSpec((tm, KVW), row)] * (2 * nl),
        [pl.BlockSpec((tm, 2 * KVW), row), one],
        [SDS((s_len, 2 * KVW), BF), SDS((1, HEAD), F32)], semantics=("arbitrary",))(kv, gk, *dks, *dvs)


def _place():
    x, y, c = lax.axis_index("x"), lax.axis_index("y"), lax.axis_index("c")
    flips = [(1 - x, y), (x, 1 - y), (1 - x, 1 - y)]
    return x, y, c, flips


def _remote(src, dst, send_sem, recv_sem, to):
    return pltpu.make_async_remote_copy(src_ref=src, dst_ref=dst, send_sem=send_sem, recv_sem=recv_sem,
                                        device_id=to, device_id_type=MESH)


def _gather_copies(p_refs, wg_refs, send, recv):
    x, y, c, flips = _place()
    chip = 2 * x + y
    cps = []
    for j, (fx, fy) in enumerate(flips):
        for b in range(2):
            half = p_refs[b].shape[0] // 2
            mine = pl.ds(c * half, half)
            cps.append(_remote(p_refs[b].at[mine, :], wg_refs[b].at[chip, mine, :], send.at[2 * j + b],
                               recv.at[2 * j + b], (fx, fy, c)))
    return cps, cps


def _forward_copies(p_refs, wg_refs, send, recv):
    x, y, c, flips = _place()
    chip = 2 * x + y
    sib = (x, y, 1 - c)
    sends, arrivals = [], []
    for b in range(2):
        half = p_refs[b].shape[0] // 2
        own = _remote(p_refs[b], wg_refs[b].at[chip], send.at[b], recv.at[b], sib)
        sends.append(own)
        arrivals.append(own)
        for j, (fx, fy) in enumerate(flips):
            k = 2 + 3 * b + j
            landed = wg_refs[b].at[2 * fx + fy, pl.ds(c * half, half), :]
            other = wg_refs[b].at[2 * fx + fy, pl.ds((1 - c) * half, half), :]
            sends.append(_remote(landed, landed, send.at[k], recv.at[k], sib))
            arrivals.append(_remote(other, other, send.at[k], recv.at[k], sib))
    return sends, arrivals


def _swap_copies(g_refs, r_refs, send, recv):
    x, y, c, _ = _place()
    cps = []
    for b in range(2):
        half = g_refs[b].shape[1] // 2
        cps.append(_remote(g_refs[b].at[:, pl.ds((1 - c) * half, half), :], r_refs[b], send.at[b], recv.at[b],
                           (x, y, 1 - c)))
    return cps, cps


def _split_start(make_copies, n_sems, bufs, after, fresh, *, name):
    def body(a1, a2, b1, b2, after_ref, send, recv, *outs):
        for cp in make_copies((a1, a2), (b1, b2), send, recv)[0]:
            cp.start()
        outs[4][...] = jnp.zeros_like(outs[4])

    extra_shape = () if fresh is None else (pltpu.HBM(fresh, F32),)
    extra_spec = () if fresh is None else (HBM,)
    return pl.pallas_call(
        body, name=name,
        out_shape=(pltpu.SemaphoreType.DMA((n_sems,)), pltpu.SemaphoreType.DMA((n_sems,)))
        + tuple(pltpu.HBM(b.shape, b.dtype) for b in bufs) + (SDS((8, LANE), F32),) + extra_shape,
        in_specs=(HBM, HBM, HBM, HBM, ANY), out_specs=(SEM, SEM, HBM, HBM, HBM, HBM, VMEM_WHOLE) + extra_spec,
        input_output_aliases={0: 2, 1: 3, 2: 4, 3: 5},
        compiler_params=pltpu.CompilerParams(has_side_effects=SIDE_EFFECT))(*[_in_hbm(b) for b in bufs], after)


def _split_wait(make_copies, started, after, *, name):
    send, recv, bufs = started[0], started[1], started[2:6]

    def body(a1, a2, b1, b2, send_ref, recv_ref, after_ref, *outs):
        sends, arrivals = make_copies((a1, a2), (b1, b2), send_ref, recv_ref)
        for cp in arrivals:
            cp.wait_recv()
        for cp in sends:
            cp.wait_send()

    return pl.pallas_call(
        body, name=name, out_shape=tuple(pltpu.HBM(b.shape, b.dtype) for b in bufs),
        in_specs=(HBM, HBM, HBM, HBM, SEM, SEM, ANY), out_specs=(HBM, HBM, HBM, HBM),
        input_output_aliases={0: 0, 1: 1, 2: 2, 3: 3},
        compiler_params=pltpu.CompilerParams(has_side_effects=SIDE_EFFECT))(*bufs, send, recv, after)


def _gather_small(ps):
    def body(ps_ref, o_ref, send, recv):
        x, y, c, flips = _place()
        chip = 2 * x + y
        o_ref[chip] = ps_ref[...]
        cps = [_remote(ps_ref, o_ref.at[chip], send.at[j], recv.at[j], (fx, fy, c))
               for j, (fx, fy) in enumerate(flips)]
        for cp in cps:
            cp.start()
        for j, (fx, fy) in enumerate(flips):
            _remote(ps_ref, o_ref.at[2 * fx + fy], send.at[j], recv.at[j], (fx, fy, c)).wait_recv()
        for cp in cps:
            cp.wait_send()

    return pl.pallas_call(
        body, name="gather_small", in_specs=[VMEM_WHOLE], out_specs=VMEM_WHOLE,
        out_shape=SDS((N_CHIPS,) + ps.shape, ps.dtype),
        scratch_shapes=[pltpu.SemaphoreType.DMA((3,)), pltpu.SemaphoreType.DMA((3,))])(ps)


def _sum_sibling(g, r, place, *, tm, name):
    n_sh, half, w = r.shape
    nt = half // tm

    def body(place_ref, g_ref, r_ref, pbf_ref, own_ref):
        s = pl.program_id(1)
        p = g_ref[0] + r_ref[0]
        pbf_ref[0] = p.astype(BF)

        @pl.when(s == place_ref[1])
        def _():
            own_ref[...] = p

    return _call(
        body, name, (nt, n_sh),
        [pl.BlockSpec((1, tm, w), lambda i, s, pr: (s, pr[0] * nt + i, 0)),
         pl.BlockSpec((1, tm, w), lambda i, s, pr: (s, i, 0))],
        [pl.BlockSpec((1, tm, w), lambda i, s, pr: (s, i, 0)), pl.BlockSpec((tm, w), lambda i, s, pr: (i, 0))],
        [SDS((n_sh, half, w), BF), SDS((half, w), F32)],
        semantics=("arbitrary", "arbitrary"), prefetch=1)(place, g, r)


def _rs_copies(p_refs, land_refs, send, recv):
    _, _, c, flips = _place()
    cps = []
    for j, (fx, fy) in enumerate(flips):
        for b in range(2):
            cps.append(_remote(p_refs[b].at[2 * fx + fy], land_refs[b].at[j], send.at[2 * j + b], recv.at[2 * j + b],
                               (fx, fy, c)))
    return cps, cps


def _sum_chips(own, r, full, layer, place, *, tm, name):
    half, w = own.shape

    def body(place_ref, own_ref, r_ref, _, o_ref):
        o_ref[0, 0] = ((own_ref[...] + r_ref[0].astype(F32)) + r_ref[1].astype(F32)) + r_ref[2].astype(F32)

    return _call(
        body, name, (half // tm,),
        [pl.BlockSpec((tm, w), lambda i, pr: (i, 0)), pl.BlockSpec((3, tm, w), lambda i, pr: (0, i, 0)), ANY],
        pl.BlockSpec((1, 1, tm, w), lambda i, pr: (layer, pr[0], i, 0)), SDS(full.shape, F32),
        semantics=("parallel",), prefetch=1, aliases={3: 0})(place, own, r, full)


def _share_with_sibling(f1, f2, lo, hi, *, name):
    def body(_, __, o1_ref, o2_ref, send, recv):
        x, y, c, _ = _place()
        lay, mine, other = pl.ds(lo, hi - lo), pl.ds(c, 1), pl.ds(1 - c, 1)
        cps = [_remote(o1_ref.at[lay, mine], o1_ref.at[lay, mine], send.at[0], recv.at[0], (x, y, 1 - c)),
               _remote(o2_ref.at[lay, mine], o2_ref.at[lay, mine], send.at[1], recv.at[1], (x, y, 1 - c))]
        for cp in cps:
            cp.start()
        for cp in cps:
            cp.wait_send()
        _remote(o1_ref.at[lay, other], o1_ref.at[lay, other], send.at[0], recv.at[0], (x, y, 1 - c)).wait_recv()
        _remote(o2_ref.at[lay, other], o2_ref.at[lay, other], send.at[1], recv.at[1], (x, y, 1 - c)).wait_recv()

    return pl.pallas_call(
        body, name=name, in_specs=[ANY, ANY], out_specs=[ANY, ANY],
        out_shape=[SDS(f1.shape, f1.dtype), SDS(f2.shape, f2.dtype)], input_output_aliases={0: 0, 1: 1},
        scratch_shapes=[pltpu.SemaphoreType.DMA((2,)), pltpu.SemaphoreType.DMA((2,))])(f1, f2)


def _allreduce_small(sg):
    rows, w = sg.shape
    half = rows // 2
    assert half % 8 == 0

    def body(sg_ref, o_ref, sib_buf, part, slots, send, recv):
        x, y, c, flips = _place()
        chip = 2 * x + y
        sib = (x, y, 1 - c)
        mine = pl.ds(pl.multiple_of(c * half, 8), half)
        other = pl.ds(pl.multiple_of((1 - c) * half, 8), half)
        to_sib = _remote(sg_ref.at[other, :], sib_buf, send.at[0], recv.at[0], sib)
        to_sib.start()
        to_sib.wait_recv()
        part[...] = sg_ref[mine, :] + sib_buf[...]
        slots[chip] = part[...]
        to_chips = [_remote(part, slots.at[chip], send.at[1 + j], recv.at[1 + j], (fx, fy, c))
                    for j, (fx, fy) in enumerate(flips)]
        for cp in to_chips:
            cp.start()
        for j, (fx, fy) in enumerate(flips):
            _remote(part, slots.at[2 * fx + fy], send.at[1 + j], recv.at[1 + j], (fx, fy, c)).wait_recv()
        o_ref[mine, :] = ((slots[0] + slots[1]) + slots[2]) + slots[3]
        back = _remote(o_ref.at[mine, :], o_ref.at[mine, :], send.at[4], recv.at[4], sib)
        back.start()
        _remote(o_ref.at[other, :], o_ref.at[other, :], send.at[4], recv.at[4], sib).wait_recv()
        for cp in [to_sib, back] + to_chips:
            cp.wait_send()

    return pl.pallas_call(
        body, name="allreduce_small", in_specs=[VMEM_WHOLE], out_specs=VMEM_WHOLE, out_shape=SDS((rows, w), F32),
        scratch_shapes=[pltpu.VMEM((half, w), F32), pltpu.VMEM((half, w), F32), pltpu.VMEM((N_CHIPS, half, w), F32),
                        pltpu.SemaphoreType.DMA((5,)), pltpu.SemaphoreType.DMA((5,))])(sg)


def _adamw_math(g, w, m, v):
    mn = ADAM_B1 * m + (1.0 - ADAM_B1) * g
    vn = ADAM_B2 * v + (1.0 - ADAM_B2) * (g * g)
    m_hat = mn / (1.0 - ADAM_B1 ** ADAM_STEP)
    v_hat = vn / (1.0 - ADAM_B2 ** ADAM_STEP)
    return -ADAM_LR * (m_hat / (jnp.sqrt(v_hat) + ADAM_EPS) + ADAM_WD * w), mn, vn


def _adamw_small(gs, ws, ms, vs):
    n = len(gs)

    def body(*refs):
        for k in range(n):
            g, w, m, v = (refs[t * n + k][...] for t in range(4))
            refs[4 * n + k][...], refs[5 * n + k][...], refs[6 * n + k][...] = _adamw_math(g, w, m, v)

    out = pl.pallas_call(
        body, name="adamw_small", in_specs=[VMEM_WHOLE] * (4 * n), out_specs=[VMEM_WHOLE] * (3 * n),
        out_shape=[SDS(w.shape, F32) for w in ws] * 3)(*gs, *ws, *ms, *vs)
    return out[:n], out[n:2 * n], out[2 * n:]


def _adamw(g_arr, layer0, g_off, per_layer, w, m, v, *, name, tm, layers=None, prev=None):
    rows, cols = w.shape
    assert g_off % tm == 0 and per_layer % tm == 0 and rows % per_layer == 0
    npl = per_layer // tm
    lo, hi = layers or (0, rows // per_layer)
    base = lo * npl

    def body(g_ref, w_ref, m_ref, v_ref, *rest):
        go_ref, d_ref, mo_ref, vo_ref = rest[-4:]
        g = g_ref[0]
        go_ref[...] = g
        d_ref[...], mo_ref[...], vo_ref[...] = _adamw_math(g, w_ref[...], m_ref[...], v_ref[...])

    blk = pl.BlockSpec((tm, cols), lambda i: (base + i, 0))
    extra = list(prev or [])
    return _call(
        body, name, ((hi - lo) * npl,),
        [pl.BlockSpec((1, tm, cols), lambda i: (layer0 + lo + i // npl, g_off // tm + i % npl, 0)), blk, blk, blk]
        + [ANY] * len(extra),
        [blk] * 4,
        [SDS((rows, cols), F32)] * 4, semantics=("parallel",),
        aliases={4 + k: k for k in range(len(extra))})(g_arr, w, m, v, *extra)


def _pack_small(parts, width):
    flat = jnp.concatenate([p.reshape(-1).astype(F32) for p in parts])
    rows = -(-flat.shape[0] // (16 * width)) * 16
    return jnp.pad(flat, (0, rows * width - flat.shape[0])).reshape(rows, width)


def _unpack_small(packed, shapes):
    flat = packed.reshape(-1)
    out, off = [], 0
    for shp in shapes:
        size = 1
        for n in shp:
            size *= n
        out.append(flat[off:off + size].reshape(shp))
        off += size
    return out


def _block_diag(pw):
    g, c, _ = pw.shape
    eye = jnp.eye(g, dtype=pw.dtype)
    return (eye[:, None, :, None] * pw[:, :, None, :]).reshape(g * c, g * c)


def _diag_blocks(full, g):
    c = full.shape[0] // g
    return jnp.stack([full[i * c:(i + 1) * c, i * c:(i + 1) * c] for i in range(g)])


def kernel(x, mem, norm_mix, w_in, pool_w, pool_scale, kv_norm, w_kv, k_norm, q_norm, sinks, mem_norm, w_mem_kv, mem_q_norm, mem_k_norm, w_out, norm_mlp, w_up, w_down, loss_target, m_norm_mix, m_w_in, m_pool_w, m_pool_scale, m_kv_norm, m_w_kv, m_k_norm, m_q_norm, m_sinks, m_mem_norm, m_w_mem_kv, m_mem_q_norm, m_mem_k_norm, m_w_out, m_norm_mlp, m_w_up, m_w_down, v_norm_mix, v_w_in, v_pool_w, v_pool_scale, v_kv_norm, v_w_kv, v_k_norm, v_q_norm, v_sinks, v_mem_norm, v_w_mem_kv, v_mem_q_norm, v_mem_k_norm, v_w_out, v_norm_mlp, v_w_up, v_w_down):
    s_len, d = x.shape[1], x.shape[2]
    n_layers, n_pool = norm_mix.shape[0], pool_w.shape[0]
    n_swa = n_layers - n_pool
    main = d - KVW
    qh = main // HEAD
    ff = w_down.shape[1] * N_CHIPS
    dq = d // N_CHIPS
    assert w_up.shape[2] == d and ff == N_CHIPS * d and w_kv.shape[1] == 2 * KVW
    tm = min(512, s_len)
    tmb = min(1024, s_len)
    tm_mem = mem.shape[1]

    cx, cy, cc = lax.axis_index("x"), lax.axis_index("y"), lax.axis_index("c")
    chip = 2 * cx + cy
    place = jnp.stack([cc, chip]).astype(jnp.int32)

    off_down, off_up, off_in, off_out = 0, d, 2 * d, 2 * d + dq
    rows1 = off_out + dq
    off_mkv, off_kv = 0, dq
    rows2 = 2 * dq

    ps = jnp.pad(pool_scale, ((0, 8 - n_pool), (0, 2 * LANE - pool_scale.shape[1])))

    def packed_weights(l):
        p1 = jnp.concatenate([w_down[l], w_up[l], w_in[l], w_out[l]]).astype(BF)
        p2 = jnp.concatenate([w_mem_kv[l], w_kv] if l == n_pool else [w_mem_kv[l]]).astype(BF)
        return p1, p2

    def gather_start(l, after):
        p1, p2 = packed_weights(l)
        bufs = (p1, p2, lax.empty((N_CHIPS,) + p1.shape, BF), lax.empty((N_CHIPS,) + p2.shape, BF))
        return _split_start(_gather_copies, 6, bufs, after, None, name=f"gather_start_{l}")

    def gather_land(l, started, after):
        bufs = _split_wait(_gather_copies, started, after, name=f"gather_wait_{l}")
        return _split_start(_forward_copies, 8, bufs, place, None, name=f"forward_start_{l}")

    def gather_finish(l, forwarding, after):
        bufs = _split_wait(_forward_copies, forwarding, after, name=f"forward_wait_{l}")
        return bufs[2], bufs[3]

    def w_rows(arr, off, nrows, width):
        assert off % nrows == 0
        return (arr, (N_CHIPS, nrows, width), lambda j: (0, off // nrows, 0))

    row = lambda a: a.reshape(1, -1)
    h = x.reshape(s_len, d)
    memx = mem.reshape(tm_mem, d)
    tgt = loss_target.reshape(s_len, d)
    pbd = [_block_diag(pool_w[l]).astype(BF) for l in range(n_pool)]
    sinks_pad = [jnp.pad(row(sinks[j]), ((0, 0), (0, LANE - qh))) for j in range(n_swa)]

    w_in_l, w_out_l, w_down_l, w_up_all_l, w_mkv_l = [], [], [], [], []
    w_kv_g = None
    first = gather_start(0, place)
    psg = _gather_small(ps + first[6][0, 0])
    pool_scale_full = jnp.concatenate([psg[k, :n_pool, :pool_scale.shape[1]] for k in range(N_CHIPS)], axis=1)
    forwarding = gather_land(0, first, psg)
    travelling = gather_start(1, forwarding[6]) if n_layers > 1 else None
    saved, probs, sink_probs, mem_probs = [], {}, {}, {}
    kv = hn_kv = kn = vsh = None
    for l in range(n_layers):
        wg1, wg2 = gather_finish(l, forwarding, h if l else forwarding[6])
        w_in_l.append(w_rows(wg1, off_in, dq, d))
        w_out_l.append(w_rows(wg1, off_out, dq, d))
        w_down_l.append(w_rows(wg1, off_down, d, d))
        w_up_all_l.append((wg1, (N_CHIPS, d, d), lambda j: (0, off_up // d, 0)))
        w_mkv_l.append(w_rows(wg2, off_mkv, dq, 2 * KVW))
        g_mix = row(norm_mix[l])
        if travelling is not None:
            g_mix = g_mix + travelling[6][0, 0]
        if l == n_pool:
            w_kv_g = w_rows(wg2, off_kv, dq, 2 * KVW)
            kv, hn_kv = _norm_mm(h, row(kv_norm), w_kv_g, 1, 2 * KVW, act=False, name="kv_proj", tm=tmb)
            kn, vsh = _kv_prep(kv, row(k_norm), tm=tmb)
        h0 = h
        proj, xn = _norm_mm(h0, g_mix, w_in_l[l], 1, d, act=False, name=f"in_proj_{l}", tm=tmb)
        mkv, memn = _norm_mm(memx, row(mem_norm[l]), w_mkv_l[l], 1, 2 * KVW, act=False, name=f"mem_kv_{l}", tm=tm_mem)
        if l < n_pool:
            cat, mem_probs[l] = _mixer_pool_fwd(proj, mkv, pbd[l], row(pool_scale_full[l]), row(mem_q_norm[l]),
                                                row(mem_k_norm[l]), name=f"mixer_fwd_{l}", tm=tmb)
        else:
            j = l - n_pool
            cat, probs[l], sink_probs[l] = _mixer_swa_fwd(proj, kn, vsh, row(q_norm[j]), sinks_pad[j],
                                                          name=f"mixer_fwd_{l}")
            cat, mem_probs[l] = _mem_attn_fwd(proj, mkv, row(mem_q_norm[l]), row(mem_k_norm[l]), cat,
                                              name=f"mem_attn_fwd_{l}", tm=tmb)
        h1 = _mm_res(h0, cat, w_out_l[l], name=f"out_proj_{l}", tm=tmb)
        hh, xm = _norm_mm(h1, row(norm_mlp[l]), w_up_all_l[l], N_CHIPS, d, act=True, name=f"mlp_up_{l}", tm=tm)
        after = None
        if travelling is not None:
            forwarding = gather_land(l + 1, travelling, hh)
            travelling = gather_start(l + 2, forwarding[6]) if l + 2 < n_layers else None
            after = forwarding[6]
        h = _mm_res(h1, hh, w_down_l[l], name=f"mlp_down_{l}", tm=tm, after=after)
        saved.append((h0, proj, xn, mkv, memn, cat, h1, hh, xm))

    dh, dh_bf, loss_part = _loss_head(h, tgt, tm=tmb)

    half1, half2 = rows1 // 2, rows2 // 2
    g1 = lax.empty((N_CHIPS, rows1, d), F32)
    pending = {}
    swapping = None
    tk = min(1024, d)

    def reduce_begin(l, swapped, after):
        g1_l, g2_l, r1, r2 = _split_wait(_swap_copies, swapped, after, name=f"swap_wait_{l}")
        pb1, own1 = _sum_sibling(g1_l, r1, place, tm=_tile(half1, 640), name=f"sum_sibling_a_{l}")
        pb2, own2 = _sum_sibling(g2_l, r2, place, tm=_tile(half2, 256), name=f"sum_sibling_b_{l}")
        bufs = (pb1, pb2, lax.empty((3, half1, d), BF), lax.empty((3, half2, 2 * KVW), BF))
        return _split_start(_rs_copies, 6, bufs, place, None, name=f"reduce_start_{l}"), own1, own2
    zeros_mem = jnp.zeros((tm_mem, d), F32)

    def rows_map(off, nrows, tkk):
        per = nrows // tkk
        return lambda i, j: (i // per, off // tkk + i % per, 0)

    def cols_map(off, tkk):
        return lambda i, j: (j, off // tkk + i, 0)

    d_norm_mix, d_norm_mlp, d_mem_norm = [None] * n_layers, [None] * n_layers, [None] * n_layers
    d_mem_q, d_mem_k = [None] * n_layers, [None] * n_layers
    d_pool_w, d_pool_scale = [None] * n_pool, [None] * n_pool
    d_q_norm, d_sinks = [None] * n_swa, [None] * n_swa
    dks, dvs = [], []
    d_kv_norm = d_k_norm = None
    for l in reversed(range(n_layers)):
        h0, proj, xn, mkv, memn, cat, h1, hh, xm = saved[l]
        g2 = jnp.zeros((N_CHIPS, rows2, 2 * KVW), F32)
        g1 = _mm_tn(hh, dh_bf, g1, rows_map(off_down, d, tk), tk, d, name=f"dw_down_{l}")
        du = _mm_nt_relu2(dh_bf, hh, w_down_l[l], N_CHIPS, name=f"d_mlp_act_{l}", tm=tm)
        g1 = _mm_tn(xm, du, g1, cols_map(off_up, tk), tk, d, name=f"dw_up_{l}")
        g_mlp = row(norm_mlp[l])
        if swapping is not None:
            pending[swapping[0]] = reduce_begin(*swapping, after=g1)
            g_mlp = g_mlp + pending[swapping[0]][0][6][0, 0]
        dh1, dh1_bf, d_norm_mlp[l] = _mm_nt_normbwd(du, w_up_all_l[l], N_CHIPS, h1, g_mlp, dh,
                                                    name=f"d_mlp_in_{l}", tm=tm)
        tkq = min(tk, dq)
        g1 = _mm_tn(cat, dh1_bf, g1, rows_map(off_out, dq, tkq), tkq, d, name=f"dw_out_{l}")
        dcat = _mm_nt(dh1_bf, w_out_l[l], d, name=f"d_cat_{l}", tm=tmb)
        if l < n_pool:
            dproj, dpbd, dscale, dmkv, d_mem_q[l], d_mem_k[l] = _mixer_pool_bwd(
                proj, dcat, mkv, pbd[l], row(pool_scale_full[l]), row(mem_q_norm[l]), row(mem_k_norm[l]),
                mem_probs[l], name=f"mixer_bwd_{l}", tm=tmb)
            d_pool_w[l] = _diag_blocks(dpbd, len(POOL_WINDOWS))
            d_pool_scale[l] = dscale
        else:
            j = l - n_pool
            dproj, dk, dv, d_q_norm[j], dsk = _mixer_swa_bwd(proj, dcat, kn, vsh, row(q_norm[j]), probs[l],
                                                             sink_probs[l], name=f"mixer_bwd_{l}")
            dproj, dmkv, d_mem_q[l], d_mem_k[l] = _mem_attn_bwd(
                proj, dcat, mkv, row(mem_q_norm[l]), row(mem_k_norm[l]), mem_probs[l], dproj,
                name=f"mem_attn_bwd_{l}", tm=tmb)
            d_sinks[j] = dsk[0, :qh]
            dks.append(dk)
            dvs.append(dv)
        g1 = _mm_tn(xn, dproj, g1, rows_map(off_in, dq, tkq), tkq, d, name=f"dw_in_{l}")
        dh, dh_bf, d_norm_mix[l] = _mm_nt_normbwd(dproj, w_in_l[l], 1, h0, row(norm_mix[l]), dh1,
                                                  name=f"d_in_{l}", tm=tmb)
        g2 = _mm_tn(memn, dmkv, g2, rows_map(off_mkv, dq, tkq), tkq, 2 * KVW, name=f"dw_mem_kv_{l}")
        _, _, d_mem_norm[l] = _mm_nt_normbwd(dmkv, w_mkv_l[l], 1, memx, row(mem_norm[l]), zeros_mem,
                                             name=f"d_mem_norm_{l}", tm=tm_mem)
        if l == n_pool:
            dkv, d_k_norm = _kv_bwd(kv, dks, dvs, row(k_norm), tm=tmb)
            g2 = _mm_tn(hn_kv, dkv, g2, rows_map(off_kv, dq, tkq), tkq, 2 * KVW, name="dw_kv")
            dh, dh_bf, d_kv_norm = _mm_nt_normbwd(dkv, w_kv_g, 1, h0, row(kv_norm), dh, name="d_kv_in", tm=tmb)
        bufs = (g1, g2, lax.empty((N_CHIPS, half1, d), F32), lax.empty((N_CHIPS, half2, 2 * KVW), F32))
        swapping = (l, _split_start(_swap_copies, 2, bufs, place, (N_CHIPS, rows1, d) if l > 0 else None,
                                    name=f"swap_start_{l}"))
        g1 = swapping[1][7] if l > 0 else None
    grad_x = dh.reshape(x.shape)

    small_names = ["norm_mix", "pool_w", "pool_scale", "kv_norm", "k_norm", "q_norm", "sinks", "mem_norm",
                   "mem_q_norm", "mem_k_norm", "norm_mlp"]
    small_grads = {
        "norm_mix": jnp.concatenate(d_norm_mix), "pool_w": jnp.stack(d_pool_w),
        "pool_scale": jnp.concatenate(d_pool_scale), "kv_norm": d_kv_norm[0], "k_norm": d_k_norm[0],
        "q_norm": jnp.concatenate(d_q_norm), "sinks": jnp.stack(d_sinks), "mem_norm": jnp.concatenate(d_mem_norm),
        "mem_q_norm": jnp.concatenate(d_mem_q), "mem_k_norm": jnp.concatenate(d_mem_k),
        "norm_mlp": jnp.concatenate(d_norm_mlp)}
    width = d
    sg = _pack_small([small_grads[n] for n in small_names] + [loss_part], width)
    sg = sg + swapping[1][6][0, 0]
    sg = _allreduce_small(sg)
    pending[0] = reduce_begin(*swapping, after=sg)
    *unpacked, loss_sum = _unpack_small(sg, [small_grads[n].shape for n in small_names] + [(1, 1)])
    loss = loss_sum[0, 0]
    reduced = dict(zip(small_names, unpacked))
    psw = pool_scale.shape[1]
    reduced["pool_scale"] = lax.dynamic_slice_in_dim(reduced["pool_scale"], chip * psw, psw, axis=1)
    params = dict(norm_mix=(norm_mix, m_norm_mix, v_norm_mix), pool_w=(pool_w, m_pool_w, v_pool_w),
                  pool_scale=(pool_scale, m_pool_scale, v_pool_scale), kv_norm=(kv_norm, m_kv_norm, v_kv_norm),
                  k_norm=(k_norm, m_k_norm, v_k_norm), q_norm=(q_norm, m_q_norm, v_q_norm),
                  sinks=(sinks, m_sinks, v_sinks), mem_norm=(mem_norm, m_mem_norm, v_mem_norm),
                  mem_q_norm=(mem_q_norm, m_mem_q_norm, v_mem_q_norm),
                  mem_k_norm=(mem_k_norm, m_mem_k_norm, v_mem_k_norm), norm_mlp=(norm_mlp, m_norm_mlp, v_norm_mlp))
    flat2 = lambda a: a.reshape(-1, a.shape[-1])
    grads = [flat2(reduced[n].reshape(params[n][0].shape)) for n in small_names]
    grads[0] = grads[0] + pending[0][0][6][0, 0]
    res = _adamw_small(grads, *[[flat2(params[n][t]) for n in small_names] for t in range(3)])
    small = {n: [grads[k].reshape(params[n][0].shape)] + [r[k].reshape(params[n][0].shape) for r in res]
             for k, n in enumerate(small_names)}

    full1 = lax.empty((n_layers, 2, half1, d), F32)
    full2 = lax.empty((n_layers, 2, half2, 2 * KVW), F32)
    kinds = (("w_down", 1, off_down, d, w_down, m_w_down, v_w_down), ("w_up", 1, off_up, d, w_up, m_w_up, v_w_up),
             ("w_in", 1, off_in, dq, w_in, m_w_in, v_w_in), ("w_out", 1, off_out, dq, w_out, m_w_out, v_w_out),
             ("w_mem_kv", 2, off_mkv, dq, w_mem_kv, m_w_mem_kv, v_w_mem_kv))
    big = {}
    first = 1 if n_layers > 1 and n_pool > 0 else 0
    after = res[0][0]
    for lo, hi in ((first, n_layers), (0, first)):
        if lo == hi:
            continue
        for l in reversed(range(lo, hi)):
            exchange, own1, own2 = pending[l]
            _, _, x1, x2 = _split_wait(_rs_copies, exchange, after, name=f"reduce_wait_{l}")
            full1 = _sum_chips(own1, x1, full1, l, place, tm=_tile(half1, 640), name=f"sum_chips_a_{l}")
            full2 = _sum_chips(own2, x2, full2, l, place, tm=_tile(half2, 256), name=f"sum_chips_b_{l}")
        full1, full2 = _share_with_sibling(full1, full2, lo, hi, name=f"share_with_sibling_{lo}")
        views = (None, full1.reshape(n_layers, rows1, d), full2.reshape(n_layers, rows2, 2 * KVW))
        for name, which, off, per, w_, m_, v_ in kinds:
            cols = views[which].shape[2]
            big[name] = _adamw(views[which], 0, off, per, w_.reshape(-1, cols), m_.reshape(-1, cols),
                               v_.reshape(-1, cols), name=f"adamw_{name}_{lo}", tm=min(512, per), layers=(lo, hi),
                               prev=big.get(name))
        if lo <= n_pool < hi:
            big["w_kv"] = _adamw(views[2], n_pool, off_kv, dq, w_kv, m_w_kv, v_w_kv, name="adamw_w_kv",
                                 tm=min(256, dq))
        after = big["w_mem_kv"][1]
    shapes_big = dict(w_down=w_down.shape, w_up=w_up.shape, w_in=w_in.shape, w_out=w_out.shape,
                      w_mem_kv=w_mem_kv.shape, w_kv=w_kv.shape)
    big = {n: [r.reshape(shapes_big[n]) for r in big[n]] for n in big}

    order = ["norm_mix", "w_in", "pool_w", "pool_scale", "kv_norm", "w_kv", "k_norm", "q_norm", "sinks", "mem_norm",
             "w_mem_kv", "mem_q_norm", "mem_k_norm", "w_out", "norm_mlp", "w_up", "w_down"]
    out = {**big, **small}
    return (loss, grad_x, *[out[n][0] for n in order], *[out[n][1] for n in order],
            *[out[n][2] for n in order], *[out[n][3] for n in order])
```

```python
import functools

import jax
import jax.numpy as jnp
from jax import lax
from jax.experimental import pallas as pl
from jax.experimental.pallas import tpu as pltpu

F32, BF = jnp.float32, jnp.bfloat16
SDS = jax.ShapeDtypeStruct
MESH = pl.DeviceIdType.MESH
ANY = pl.BlockSpec(memory_space=pl.ANY)
HBM = pl.BlockSpec(memory_space=pltpu.HBM)
SEM = pl.BlockSpec(memory_space=pltpu.SEMAPHORE)
VMEM_WHOLE = pl.BlockSpec(memory_space=pltpu.VMEM)
SIDE_EFFECT = pltpu.SideEffectType.DATAFLOW_SIDE_EFFECTING


def _in_hbm(a):
    return pltpu.with_memory_space_constraint(a, pltpu.HBM)


EPS = 1e-6
HEAD = 64
KV_HEADS = 4
KVW = KV_HEADS * HEAD
WINDOW = 128
POOL_WINDOWS = (2, 4, 8, 16)
HALO = 16
QK_SCALE = HEAD ** -0.5
NEG = float(jnp.finfo(jnp.float32).min)
N_CHIPS = 4
LANE = 128

ADAM_LR, ADAM_B1, ADAM_B2, ADAM_EPS, ADAM_WD, ADAM_STEP = 0.001, 0.9, 0.999, 1e-08, 0.01, 10

VMEM_LIMIT_MB = 56


def _call(body, name, grid, in_specs, out_specs, out_shape, *, scratch=(), semantics=None, aliases=None,
          prefetch=0):
    params = pltpu.CompilerParams(dimension_semantics=semantics, vmem_limit_bytes=VMEM_LIMIT_MB << 20)
    if prefetch:
        spec = pltpu.PrefetchScalarGridSpec(num_scalar_prefetch=prefetch, grid=grid, in_specs=in_specs,
                                            out_specs=out_specs, scratch_shapes=list(scratch))
        return pl.pallas_call(body, name=name, grid_spec=spec, out_shape=out_shape,
                              input_output_aliases=aliases or {}, compiler_params=params)
    return pl.pallas_call(body, name=name, grid=grid, in_specs=in_specs, out_specs=out_specs, out_shape=out_shape,
                          scratch_shapes=list(scratch), input_output_aliases=aliases or {}, compiler_params=params)


def _tile(n, pref):
    return max(t for t in range(8, min(n, pref) + 1, 8) if n % t == 0)


def _dot(a, b):
    return jnp.dot(a, b, preferred_element_type=F32)


def _dot_nt(a, b):
    return lax.dot_general(a, b, (((1,), (1,)), ((), ())), preferred_element_type=F32)


def _dot_tn(a, b):
    return lax.dot_general(a, b, (((0,), (0,)), ((), ())), preferred_element_type=F32)


def _rms(x):
    r = lax.rsqrt(jnp.mean(x * x, axis=-1, keepdims=True) + EPS)
    return x * r, r


def _rms_bwd(dy, xh, r, g):
    dg = jnp.sum(dy * xh, axis=0, keepdims=True)
    dyg = dy * g
    dx = r * (dyg - xh * jnp.mean(dyg * xh, axis=-1, keepdims=True))
    return dx, dg


def _norm_mm(h, g, w, nj, tn, *, act, name, tm):
    w_arr, w_block, w_imap = w
    rows, d = h.shape

    def body(h_ref, g_ref, w_ref, y_ref, xn_ref):
        xh, _ = _rms(h_ref[...])
        xn = (xh * g_ref[...]).astype(BF)
        xn_ref[...] = xn
        for j in range(nj):
            u = _dot(xn, w_ref[j] if nj > 1 else w_ref[...].reshape(d, tn))
            if act:
                a = jnp.maximum(u, 0.0)
                y_ref[:, j * tn:(j + 1) * tn] = (a * a).astype(BF)
            else:
                y_ref[:, j * tn:(j + 1) * tn] = u

    assert nj == 1 or w_block[0] == nj
    return _call(
        body, name, (rows // tm,),
        [pl.BlockSpec((tm, d), lambda i: (i, 0)), pl.BlockSpec((1, d), lambda i: (0, 0)),
         pl.BlockSpec(w_block, lambda i: w_imap(0))],
        [pl.BlockSpec((tm, nj * tn), lambda i: (i, 0)), pl.BlockSpec((tm, d), lambda i: (i, 0))],
        [SDS((rows, nj * tn), BF if act else F32), SDS((rows, d), BF)],
        semantics=("parallel",))(h, g, w_arr)


def _mm_res(res, a, w, *, name, tm, after=None):
    w_arr, w_block, w_imap = w
    rows, k = a.shape
    n = res.shape[1]

    def body(res_ref, a_ref, w_ref, *rest):
        rest[-1][...] = res_ref[...] + _dot(a_ref[...], w_ref[...].reshape(k, n))

    extra = [] if after is None else [after]
    return _call(
        body, name, (rows // tm,),
        [pl.BlockSpec((tm, n), lambda i: (i, 0)), pl.BlockSpec((tm, k), lambda i: (i, 0)),
         pl.BlockSpec(w_block, lambda i: w_imap(0))] + [ANY] * len(extra),
        pl.BlockSpec((tm, n), lambda i: (i, 0)), SDS((rows, n), F32), semantics=("parallel",))(res, a, w_arr, *extra)


def _mm_nt(dy, w, k, *, name, tm):
    w_arr, w_block, w_imap = w
    rows, n = dy.shape

    def body(dy_ref, w_ref, o_ref):
        o_ref[...] = _dot_nt(dy_ref[...], w_ref[...].reshape(k, n))

    return _call(
        body, name, (rows // tm,),
        [pl.BlockSpec((tm, n), lambda i: (i, 0)), pl.BlockSpec(w_block, lambda i: w_imap(0))],
        pl.BlockSpec((tm, k), lambda i: (i, 0)), SDS((rows, k), F32), semantics=("parallel",))(dy, w_arr)


def _mm_nt_relu2(dh, hh, w, nj, *, name, tm):
    w_arr, w_block, w_imap = w
    rows, d = dh.shape
    tk = hh.shape[1] // nj

    def body(dh_ref, hh_ref, w_ref, o_ref):
        dh_t = dh_ref[...]
        for j in range(nj):
            cols = slice(j * tk, (j + 1) * tk)
            dhh = _dot_nt(dh_t, w_ref[j])
            o_ref[:, cols] = (dhh * (2.0 * jnp.sqrt(hh_ref[:, cols].astype(F32)))).astype(BF)

    assert w_block[0] == nj
    return _call(
        body, name, (rows // tm,),
        [pl.BlockSpec((tm, d), lambda i: (i, 0)), pl.BlockSpec((tm, nj * tk), lambda i: (i, 0)),
         pl.BlockSpec(w_block, lambda i: w_imap(0))],
        pl.BlockSpec((tm, nj * tk), lambda i: (i, 0)), SDS((rows, nj * tk), BF),
        semantics=("parallel",))(dh, hh, w_arr)


def _mm_nt_normbwd(dy, w, nsplit, h, g, dres, *, name, tm):
    w_arr, w_block, w_imap = w
    rows, n = dy.shape
    d = h.shape[1]
    ns = n // nsplit

    def body(dy_ref, w_ref, h_ref, g_ref, dres_ref, o_ref, obf_ref, dg_ref):
        if nsplit == 1:
            dxn = _dot_nt(dy_ref[...].astype(BF), w_ref[...].reshape(d, n))
        else:
            dxn = _dot_nt(dy_ref[:, 0:ns].astype(BF), w_ref[0])
            for s in range(1, nsplit):
                dxn += _dot_nt(dy_ref[:, s * ns:(s + 1) * ns].astype(BF), w_ref[s])
        xh, r = _rms(h_ref[...])
        dx, dg = _rms_bwd(dxn, xh, r, g_ref[...])
        out = dres_ref[...] + dx
        o_ref[...] = out
        obf_ref[...] = out.astype(BF)

        @pl.when(pl.program_id(0) == 0)
        def _():
            dg_ref[...] = jnp.zeros_like(dg_ref)

        dg_ref[...] += dg

    row = lambda i: (i, 0)
    return _call(
        body, name, (rows // tm,),
        [pl.BlockSpec((tm, n), row), pl.BlockSpec(w_block, lambda i: w_imap(0)), pl.BlockSpec((tm, d), row),
         pl.BlockSpec((1, d), lambda i: (0, 0)), pl.BlockSpec((tm, d), row)],
        [pl.BlockSpec((tm, d), row), pl.BlockSpec((tm, d), row), pl.BlockSpec((1, d), lambda i: (0, 0))],
        [SDS((rows, d), F32), SDS((rows, d), BF), SDS((1, d), F32)],
        semantics=("arbitrary",))(dy, w_arr, h, g, dres)


def _mm_tn(x, dy, packed, out_imap, tk, tn, *, name):
    s_len, k = x.shape
    n = dy.shape[1]

    def body(x_ref, dy_ref, _, o_ref):
        o_ref[0] = _dot_tn(x_ref[...], dy_ref[...].astype(BF))

    return _call(
        body, name, (k // tk, n // tn),
        [pl.BlockSpec((s_len, tk), lambda i, j: (0, i)), pl.BlockSpec((s_len, tn), lambda i, j: (0, j)), ANY],
        pl.BlockSpec((1, tk, tn), out_imap), SDS(packed.shape, packed.dtype),
        semantics=("parallel", "parallel"), aliases={2: 0})(x, dy, packed)


def _loss_head(y, tgt, *, tm):
    rows, d = y.shape

    def body(y_ref, t_ref, dh_ref, dhbf_ref, loss_ref):
        err = y_ref[...] - t_ref[...]
        dh = err * (1.0 / d)
        dh_ref[...] = dh
        dhbf_ref[...] = dh.astype(BF)

        @pl.when(pl.program_id(0) == 0)
        def _():
            loss_ref[...] = jnp.zeros_like(loss_ref)

        loss_ref[...] += 0.5 * jnp.sum(jnp.mean(err * err, axis=-1, keepdims=True), axis=0, keepdims=True)

    row = lambda i: (i, 0)
    return _call(
        body, "loss_head", (rows // tm,), [pl.BlockSpec((tm, d), row), pl.BlockSpec((tm, d), row)],
        [pl.BlockSpec((tm, d), row), pl.BlockSpec((tm, d), row), pl.BlockSpec((1, 1), lambda i: (0, 0))],
        [SDS((rows, d), F32), SDS((rows, d), BF), SDS((1, 1), F32)], semantics=("arbitrary",))(y, tgt)


def _hs(h):
    return slice(HEAD * h, HEAD * (h + 1))


def _softmax_rows(s):
    e = jnp.exp(s - jnp.max(s, axis=-1, keepdims=True))
    return e * (1.0 / jnp.sum(e, axis=-1, keepdims=True))


def _scaled_bf16(qn):
    return (qn * QK_SCALE).astype(BF)


def _mem_fwd(mq, mk, mv, gq):
    outs, probs = [], []
    for h in range(KV_HEADS):
        xh, _ = _rms(mq[:, _hs(h)])
        p = _softmax_rows(_dot_nt(_scaled_bf16(xh * gq), mk[:, _hs(h)])).astype(BF)
        probs.append(p)
        outs.append(_dot(p, mv[:, _hs(h)]))
    return jnp.concatenate(outs, axis=-1), jnp.concatenate(probs, axis=-1)


def _mem_bwd(mq, do, mk, mv, gq, probs):
    dqs, dks, dvs, dgq = [], [], [], 0.0
    mlen = mk.shape[0]
    for h in range(KV_HEADS):
        xh, r = _rms(mq[:, _hs(h)])
        qn = _scaled_bf16(xh * gq)
        p_bf = probs[:, h * mlen:(h + 1) * mlen]
        p = p_bf.astype(F32)
        doh = do[:, _hs(h)].astype(BF)
        dp = _dot_nt(doh, mv[:, _hs(h)])
        ds = (p * (dp - jnp.sum(p * dp, axis=-1, keepdims=True))).astype(BF)
        dq, dg = _rms_bwd(_dot(ds, mk[:, _hs(h)]) * QK_SCALE, xh, r, gq)
        dqs.append(dq)
        dgq = dgq + dg
        dks.append(_dot_tn(ds, qn))
        dvs.append(_dot_tn(p_bf, doh))
    cat = lambda xs: jnp.concatenate(xs, axis=-1)
    return cat(dqs), cat(dks), cat(dvs), dgq


def _mem_kv(mkv, gk):
    ks = []
    for h in range(KV_HEADS):
        xh, _ = _rms(mkv[:, _hs(h)])
        ks.append(xh * gk)
    return jnp.concatenate(ks, axis=-1).astype(BF), mkv[:, KVW:].astype(BF)


def _mem_kv_bwd(mkv, dmk, dmv, gk):
    dxs, dgk = [], 0.0
    for h in range(KV_HEADS):
        xh, r = _rms(mkv[:, _hs(h)])
        dx, dg = _rms_bwd(dmk[:, _hs(h)], xh, r, gk)
        dxs.append(dx)
        dgk = dgk + dg
    return jnp.concatenate(dxs + [dmv], axis=-1), dgk


def _pool_select(col, gd, a2, a4, a8, a16):
    return jnp.where(col < gd, a2, jnp.where(col < 2 * gd, a4, jnp.where(col < 3 * gd, a8, a16)))


def _pool_count(t0, shape, gd):
    col = lax.broadcasted_iota(jnp.int32, shape, 1)
    t = t0 + lax.broadcasted_iota(jnp.int32, shape, 0)
    win = _pool_select(col, gd, *POOL_WINDOWS)
    return jnp.minimum(t + 1, win).astype(F32)


def _pool_diff(u, halo, t0, gd):
    c = jnp.concatenate([halo, u], axis=0)
    s2 = c + pltpu.roll(c, 1, 0)
    s4 = s2 + pltpu.roll(s2, 2, 0)
    s8 = s4 + pltpu.roll(s4, 4, 0)
    s16 = s8 + pltpu.roll(s8, 8, 0)
    col = lax.broadcasted_iota(jnp.int32, c.shape, 1)
    ws = _pool_select(col, gd, s2, s4, s8, s16)[HALO:]
    return ws / _pool_count(t0, u.shape, gd) - u


def _pool_diff_bwd(dd, dd_halo, t0, gd):
    t = dd.shape[0]
    z = jnp.concatenate([dd / _pool_count(t0, dd.shape, gd), dd_halo / _pool_count(t0 + t, dd_halo.shape, gd)], axis=0)
    n = z.shape[0]
    f2 = z + pltpu.roll(z, n - 1, 0)
    f4 = f2 + pltpu.roll(f2, n - 2, 0)
    f8 = f4 + pltpu.roll(f4, n - 4, 0)
    f16 = f8 + pltpu.roll(f8, n - 8, 0)
    col = lax.broadcasted_iota(jnp.int32, z.shape, 1)
    return _pool_select(col, gd, f2, f4, f8, f16)[:t] - dd


def _swa_bias(n):
    qi = lax.broadcasted_iota(jnp.int32, (WINDOW, 2 * WINDOW), 0)
    kj = lax.broadcasted_iota(jnp.int32, (WINDOW, 2 * WINDOW), 1)
    dist = qi + WINDOW - kj
    valid = (dist >= 0) & (dist < WINDOW) & ((kj >= WINDOW) | (n > 0))
    return dist.astype(F32), valid


def _slopes(qh):
    return [2.0 ** (-8.0 * (h + 1) / qh) for h in range(qh)]


def _swa_probs(qn, kk, dist, valid, slope, sink):
    s = _dot_nt(qn, kk) - slope * dist
    s = jnp.where(valid, s, NEG)
    m = jnp.maximum(jnp.max(s, axis=-1, keepdims=True), sink)
    e = jnp.exp(s - m)
    es = jnp.exp(sink - m)
    z = jnp.sum(e, axis=-1, keepdims=True) + es
    inv = 1.0 / z
    return e * inv, es * inv


def _stack_heads(a, kh, grp):
    return jnp.concatenate([a[:, _hs(h)] for h in range(kh * grp, (kh + 1) * grp)], axis=0)


def _swa_group(q, kh, grp, n, qh, sinks):
    heads = range(kh * grp, (kh + 1) * grp)
    dist, valid = _swa_bias(n)
    slopes = _slopes(qh)
    rows = lambda vals: jnp.concatenate([jnp.broadcast_to(v, (WINDOW, 1)) for v in vals], axis=0)
    slope = rows([jnp.full((1, 1), slopes[h], F32) for h in heads])
    sink = rows([sinks[:, h:h + 1] for h in heads])
    return (_stack_heads(q, kh, grp), slope, sink, jnp.concatenate([dist] * grp, axis=0),
            jnp.concatenate([valid] * grp, axis=0))


def _swa_fwd(q, kk, vv, gq, sinks, n, qh):
    grp = qh // KV_HEADS
    lane = lax.broadcasted_iota(jnp.int32, (WINDOW, LANE), 1)
    outs, probs, sink_probs = [], [], jnp.zeros((WINDOW, LANE), F32)
    for kh in range(KV_HEADS):
        qs, slope, sink, dist, valid = _swa_group(q, kh, grp, n, qh, sinks)
        xh, _ = _rms(qs)
        p, ps = _swa_probs(_scaled_bf16(xh * gq), kk[:, _hs(kh)], dist, valid, slope, sink)
        p = p.astype(BF)
        probs.append(p)
        o = _dot(p, vv[:, _hs(kh)])
        for g in range(grp):
            outs.append(o[g * WINDOW:(g + 1) * WINDOW])
            sink_probs = jnp.where(lane == kh * grp + g, ps[g * WINDOW:(g + 1) * WINDOW], sink_probs)
    return jnp.concatenate(outs, axis=-1), probs, sink_probs


def _swa_bwd(q, do, kk, vv, gq, probs, sink_probs, qh):
    grp = qh // KV_HEADS
    lane = lax.broadcasted_iota(jnp.int32, (1, LANE), 1)
    dqs, dks, dvs, dgq, dsk = [], [], [], 0.0, jnp.zeros((1, LANE), F32)
    do = do.astype(BF)
    for kh in range(KV_HEADS):
        heads = range(kh * grp, (kh + 1) * grp)
        xh, r = _rms(_stack_heads(q, kh, grp))
        qn = _scaled_bf16(xh * gq)
        p_bf = probs[kh]
        p = p_bf.astype(F32)
        ps = jnp.concatenate([sink_probs[:, h:h + 1] for h in heads], axis=0)
        dos = _stack_heads(do, kh, grp)
        dp = _dot_nt(dos, vv[:, _hs(kh)])
        delta = jnp.sum(p * dp, axis=-1, keepdims=True)
        ds = (p * (dp - delta)).astype(BF)
        dsink = ps * delta
        for g in range(grp):
            part = -jnp.sum(dsink[g * WINDOW:(g + 1) * WINDOW], axis=0, keepdims=True)
            dsk = dsk + jnp.where(lane == kh * grp + g, part, 0.0)
        dq, dg = _rms_bwd(_dot(ds, kk[:, _hs(kh)]) * QK_SCALE, xh, r, gq)
        dqs += [dq[g * WINDOW:(g + 1) * WINDOW] for g in range(grp)]
        dgq = dgq + dg
        dks.append(_dot_tn(ds, qn))
        dvs.append(_dot_tn(p_bf, dos))
    cat = lambda xs: jnp.concatenate(xs, axis=-1)
    return cat(dqs), cat(dks), cat(dvs), dgq, dsk


def _mixer_pool_fwd(proj, mkv, pbd, scale, gq, gk, *, name, tm):
    s_len, d = proj.shape
    main = d - KVW
    gd = main // len(POOL_WINDOWS)
    mlen = mkv.shape[0]
    hb = tm // HALO

    def body(u_ref, halo_ref, mq_ref, mkv_ref, pbd_ref, scale_ref, gq_ref, gk_ref, o_ref, mp_ref, mk_s, mv_s):
        i = pl.program_id(0)

        @pl.when(i == 0)
        def _():
            mk, mv = _mem_kv(mkv_ref[...], gk_ref[...])
            mk_s[...] = mk
            mv_s[...] = mv

        halo = jnp.where(i > 0, halo_ref[...], 0.0)
        dif = _pool_diff(u_ref[...], halo, i * tm, gd)
        mixed = _dot(dif.astype(BF), pbd_ref[...]) * scale_ref[...]
        mem, mp_ref[...] = _mem_fwd(mq_ref[...], mk_s[...], mv_s[...], gq_ref[...])
        o_ref[...] = jnp.concatenate([mixed, mem], axis=-1).astype(BF)

    full = lambda shape: pl.BlockSpec(shape, lambda i: (0,) * len(shape))
    return _call(
        body, name, (s_len // tm,),
        [pl.BlockSpec((tm, main), lambda i: (i, 0)),
         pl.BlockSpec((HALO, main), lambda i: (jnp.maximum(i * hb - 1, 0), 0)),
         pl.BlockSpec((tm, KVW), lambda i: (i, main // KVW)),
         full((mlen, 2 * KVW)), full((main, main)), full((1, main)), full((1, HEAD)), full((1, HEAD))],
        [pl.BlockSpec((tm, d), lambda i: (i, 0)), pl.BlockSpec((tm, KV_HEADS * mlen), lambda i: (i, 0))],
        [SDS((s_len, d), BF), SDS((s_len, KV_HEADS * mlen), BF)],
        scratch=[pltpu.VMEM((mlen, KVW), BF), pltpu.VMEM((mlen, KVW), BF)],
        semantics=("arbitrary",))(proj, proj, proj, mkv, pbd, scale, gq, gk)


def _mixer_pool_bwd(proj, dcat, mkv, pbd, scale, gq, gk, mem_probs, *, name, tm):
    s_len, d = proj.shape
    main = d - KVW
    gd = main // len(POOL_WINDOWS)
    mlen = mkv.shape[0]
    hb = tm // HALO
    nt = s_len // tm
    last_halo = s_len // HALO - 1

    def body(u_ref, halo_ref, mq_ref, do_ref, donext_ref, dom_ref, mkv_ref, pbd_ref, scale_ref, gq_ref, gk_ref, mp_ref,
             dproj_ref, dpbd_ref, dscale_ref, dmkv_ref, dgq_ref, dgk_ref, mk_s, mv_s, dmk_s, dmv_s):
        i = pl.program_id(0)

        @pl.when(i == 0)
        def _():
            mk, mv = _mem_kv(mkv_ref[...], gk_ref[...])
            mk_s[...] = mk
            mv_s[...] = mv
            dmk_s[...] = jnp.zeros_like(dmk_s)
            dmv_s[...] = jnp.zeros_like(dmv_s)
            dpbd_ref[...] = jnp.zeros_like(dpbd_ref)
            dscale_ref[...] = jnp.zeros_like(dscale_ref)
            dgq_ref[...] = jnp.zeros_like(dgq_ref)

        pbd = pbd_ref[...]
        scale = scale_ref[...]
        halo = jnp.where(i > 0, halo_ref[...], 0.0)
        dif = _pool_diff(u_ref[...], halo, i * tm, gd).astype(BF)
        do = do_ref[...]
        dscale_ref[...] += jnp.sum(do * _dot(dif, pbd), axis=0, keepdims=True)
        dmixed = (do * scale).astype(BF)
        dpbd_ref[...] += _dot_tn(dif, dmixed)
        dd = _dot_nt(dmixed, pbd)
        donext = jnp.where(i < nt - 1, donext_ref[...], 0.0)
        dd_halo = _dot_nt((donext * scale).astype(BF), pbd)
        du = _pool_diff_bwd(dd, dd_halo, i * tm, gd)

        dmq, dmk, dmv, dgq = _mem_bwd(mq_ref[...], dom_ref[...], mk_s[...], mv_s[...], gq_ref[...], mp_ref[...])
        dmk_s[...] += dmk
        dmv_s[...] += dmv
        dgq_ref[...] += dgq
        dproj_ref[...] = jnp.concatenate([du, dmq], axis=-1).astype(BF)

        @pl.when(i == nt - 1)
        def _():
            dmkv, dgk = _mem_kv_bwd(mkv_ref[...], dmk_s[...], dmv_s[...], gk_ref[...])
            dmkv_ref[...] = dmkv
            dgk_ref[...] = dgk

    full = lambda shape: pl.BlockSpec(shape, lambda i: (0,) * len(shape))
    return _call(
        body, name, (nt,),
        [pl.BlockSpec((tm, main), lambda i: (i, 0)),
         pl.BlockSpec((HALO, main), lambda i: (jnp.maximum(i * hb - 1, 0), 0)),
         pl.BlockSpec((tm, KVW), lambda i: (i, main // KVW)),
         pl.BlockSpec((tm, main), lambda i: (i, 0)),
         pl.BlockSpec((HALO, main), lambda i: (jnp.minimum((i + 1) * hb, last_halo), 0)),
         pl.BlockSpec((tm, KVW), lambda i: (i, main // KVW)),
         full((mlen, 2 * KVW)), full((main, main)), full((1, main)), full((1, HEAD)), full((1, HEAD)),
         pl.BlockSpec((tm, KV_HEADS * mlen), lambda i: (i, 0))],
        [pl.BlockSpec((tm, d), lambda i: (i, 0)), full((main, main)), full((1, main)), full((mlen, 2 * KVW)),
         full((1, HEAD)), full((1, HEAD))],
        [SDS((s_len, d), BF), SDS((main, main), F32), SDS((1, main), F32), SDS((mlen, 2 * KVW), F32),
         SDS((1, HEAD), F32), SDS((1, HEAD), F32)],
        scratch=[pltpu.VMEM((mlen, KVW), BF), pltpu.VMEM((mlen, KVW), BF), pltpu.VMEM((mlen, KVW), F32),
                 pltpu.VMEM((mlen, KVW), F32)],
        semantics=("arbitrary",))(proj, proj, proj, dcat, dcat, dcat, mkv, pbd, scale, gq, gk, mem_probs)


def _mem_attn_fwd(proj, mkv, gq, gk, cat, *, name, tm):
    s_len, d = proj.shape
    main = d - KVW
    mlen = mkv.shape[0]

    def body(mq_ref, mkv_ref, gq_ref, gk_ref, _, o_ref, mp_ref, mk_s, mv_s):
        @pl.when(pl.program_id(0) == 0)
        def _():
            mk, mv = _mem_kv(mkv_ref[...], gk_ref[...])
            mk_s[...] = mk
            mv_s[...] = mv

        mem, mp_ref[...] = _mem_fwd(mq_ref[...], mk_s[...], mv_s[...], gq_ref[...])
        o_ref[...] = mem.astype(BF)

    full = lambda shape: pl.BlockSpec(shape, lambda i: (0,) * len(shape))
    memcol = lambda i: (i, main // KVW)
    return _call(
        body, name, (s_len // tm,),
        [pl.BlockSpec((tm, KVW), memcol), full((mlen, 2 * KVW)), full((1, HEAD)), full((1, HEAD)), ANY],
        [pl.BlockSpec((tm, KVW), memcol), pl.BlockSpec((tm, KV_HEADS * mlen), lambda i: (i, 0))],
        [SDS((s_len, d), BF), SDS((s_len, KV_HEADS * mlen), BF)],
        scratch=[pltpu.VMEM((mlen, KVW), BF), pltpu.VMEM((mlen, KVW), BF)],
        semantics=("arbitrary",), aliases={4: 0})(proj, mkv, gq, gk, cat)


def _mem_attn_bwd(proj, dcat, mkv, gq, gk, mem_probs, dproj, *, name, tm):
    s_len, d = proj.shape
    main = d - KVW
    mlen = mkv.shape[0]
    nt = s_len // tm

    def body(mq_ref, dom_ref, mkv_ref, gq_ref, gk_ref, mp_ref, _, dproj_ref, dmkv_ref, dgq_ref, dgk_ref,
             mk_s, mv_s, dmk_s, dmv_s):
        i = pl.program_id(0)

        @pl.when(i == 0)
        def _():
            mk, mv = _mem_kv(mkv_ref[...], gk_ref[...])
            mk_s[...] = mk
            mv_s[...] = mv
            dmk_s[...] = jnp.zeros_like(dmk_s)
            dmv_s[...] = jnp.zeros_like(dmv_s)
            dgq_ref[...] = jnp.zeros_like(dgq_ref)

        dmq, dmk, dmv, dgq = _mem_bwd(mq_ref[...], dom_ref[...], mk_s[...], mv_s[...], gq_ref[...], mp_ref[...])
        dmk_s[...] += dmk
        dmv_s[...] += dmv
        dgq_ref[...] += dgq
        dproj_ref[...] = dmq.astype(BF)

        @pl.when(i == nt - 1)
        def _():
            dmkv, dgk = _mem_kv_bwd(mkv_ref[...], dmk_s[...], dmv_s[...], gk_ref[...])
            dmkv_ref[...] = dmkv
            dgk_ref[...] = dgk

    full = lambda shape: pl.BlockSpec(shape, lambda i: (0,) * len(shape))
    memcol = lambda i: (i, main // KVW)
    return _call(
        body, name, (nt,),
        [pl.BlockSpec((tm, KVW), memcol), pl.BlockSpec((tm, KVW), memcol), full((mlen, 2 * KVW)), full((1, HEAD)),
         full((1, HEAD)), pl.BlockSpec((tm, KV_HEADS * mlen), lambda i: (i, 0)), ANY],
        [pl.BlockSpec((tm, KVW), memcol), full((mlen, 2 * KVW)), full((1, HEAD)), full((1, HEAD))],
        [SDS((s_len, d), BF), SDS((mlen, 2 * KVW), F32), SDS((1, HEAD), F32), SDS((1, HEAD), F32)],
        scratch=[pltpu.VMEM((mlen, KVW), BF), pltpu.VMEM((mlen, KVW), BF), pltpu.VMEM((mlen, KVW), F32),
                 pltpu.VMEM((mlen, KVW), F32)],
        semantics=("arbitrary",), aliases={6: 0})(proj, dcat, mkv, gq, gk, mem_probs, dproj)


def _mixer_swa_fwd(proj, kn, v, gqs, sinks, *, name):
    s_len, d = proj.shape
    main = d - KVW
    qh = main // HEAD
    tm = WINDOW
    prow = qh // KV_HEADS * tm

    def body(q_ref, kp_ref, kc_ref, vp_ref, vc_ref, gqs_ref, sinks_ref, o_ref, p_ref, ps_ref):
        n = pl.program_id(0)
        kk = jnp.concatenate([kp_ref[...], kc_ref[...]], axis=0)
        vv = jnp.concatenate([vp_ref[...], vc_ref[...]], axis=0)
        att, probs, sink_probs = _swa_fwd(q_ref[...], kk, vv, gqs_ref[...], sinks_ref[...], n, qh)
        for kh in range(KV_HEADS):
            p_ref[0, kh] = probs[kh]
        ps_ref[...] = sink_probs
        o_ref[...] = att.astype(BF)

    full = lambda shape: pl.BlockSpec(shape, lambda i: (0,) * len(shape))
    prev = lambda i: (jnp.maximum(i - 1, 0), 0)
    cur = lambda i: (i, 0)
    return _call(
        body, name, (s_len // tm,),
        [pl.BlockSpec((tm, main), cur), pl.BlockSpec((tm, KVW), prev), pl.BlockSpec((tm, KVW), cur),
         pl.BlockSpec((tm, KVW), prev), pl.BlockSpec((tm, KVW), cur), full((1, HEAD)), full((1, LANE))],
        [pl.BlockSpec((tm, main), cur), pl.BlockSpec((1, KV_HEADS, prow, 2 * tm), lambda i: (i, 0, 0, 0)),
         pl.BlockSpec((tm, LANE), cur)],
        [SDS((s_len, d), BF), SDS((s_len // tm, KV_HEADS, prow, 2 * tm), BF), SDS((s_len, LANE), F32)],
        semantics=("parallel",))(proj, kn, kn, v, v, gqs, sinks)


def _mixer_swa_bwd(proj, dcat, kn, v, gqs, probs, sink_probs, *, name):
    s_len, d = proj.shape
    main = d - KVW
    qh = main // HEAD
    tm = WINDOW
    nt = s_len // tm
    prow = qh // KV_HEADS * tm

    def body(q_ref, do_ref, kp_ref, kc_ref, vp_ref, vc_ref, gqs_ref, p_ref, ps_ref,
             dproj_ref, dk_ref, dv_ref, dgqs_ref, dsinks_ref):
        n = pl.program_id(0)

        @pl.when(n == 0)
        def _():
            dk_ref[...] = jnp.zeros_like(dk_ref)
            dv_ref[...] = jnp.zeros_like(dv_ref)
            dgqs_ref[...] = jnp.zeros_like(dgqs_ref)
            dsinks_ref[...] = jnp.zeros_like(dsinks_ref)

        kk = jnp.concatenate([kp_ref[...], kc_ref[...]], axis=0)
        vv = jnp.concatenate([vp_ref[...], vc_ref[...]], axis=0)
        dq, dkk, dvv, dgqs, dsk = _swa_bwd(q_ref[...], do_ref[...], kk, vv, gqs_ref[...],
                                           [p_ref[0, kh] for kh in range(KV_HEADS)], ps_ref[...], qh)
        prev = pl.ds(pl.multiple_of(jnp.maximum(n - 1, 0) * tm, tm), tm)
        own = pl.ds(pl.multiple_of(n * tm, tm), tm)
        dk_ref[prev, :] += dkk[:tm]
        dk_ref[own, :] += dkk[tm:]
        dv_ref[prev, :] += dvv[:tm]
        dv_ref[own, :] += dvv[tm:]
        dgqs_ref[...] += dgqs
        dsinks_ref[...] += dsk
        dproj_ref[...] = dq.astype(BF)

    full = lambda shape: pl.BlockSpec(shape, lambda i: (0,) * len(shape))
    prev_b = lambda i: (jnp.maximum(i - 1, 0), 0)
    cur = lambda i: (i, 0)
    return _call(
        body, name, (nt,),
        [pl.BlockSpec((tm, main), cur), pl.BlockSpec((tm, main), cur),
         pl.BlockSpec((tm, KVW), prev_b), pl.BlockSpec((tm, KVW), cur),
         pl.BlockSpec((tm, KVW), prev_b), pl.BlockSpec((tm, KVW), cur),
         full((1, HEAD)), pl.BlockSpec((1, KV_HEADS, prow, 2 * tm), lambda i: (i, 0, 0, 0)),
         pl.BlockSpec((tm, LANE), cur)],
        [pl.BlockSpec((tm, main), cur), full((s_len, KVW)), full((s_len, KVW)), full((1, HEAD)), full((1, LANE))],
        [SDS((s_len, d), BF), SDS((s_len, KVW), F32), SDS((s_len, KVW), F32), SDS((1, HEAD), F32),
         SDS((1, LANE), F32)],
        semantics=("arbitrary",))(proj, dcat, kn, kn, v, v, gqs, probs, sink_probs)


def _kv_prep(kv, gk, *, tm):
    s_len = kv.shape[0]

    def body(kv_ref, gk_ref, k_ref, v_ref):
        k, v = _mem_kv(kv_ref[...], gk_ref[...])
        k_ref[...] = k
        v_ref[...] = v

    row = lambda i: (i, 0)
    return _call(
        body, "kv_prep", (s_len // tm,),
        [pl.BlockSpec((tm, 2 * KVW), row), pl.BlockSpec((1, HEAD), lambda i: (0, 0))],
        [pl.BlockSpec((tm, KVW), row), pl.BlockSpec((tm, KVW), row)],
        [SDS((s_len, KVW), BF), SDS((s_len, KVW), BF)], semantics=("parallel",))(kv, gk)


def _kv_bwd(kv, dks, dvs, gk, *, tm):
    s_len = kv.shape[0]
    nl = len(dks)

    def body(*refs):
        kv_ref, gk_ref = refs[0], refs[1]
        dk_refs, dv_refs = refs[2:2 + nl], refs[2 + nl:2 + 2 * nl]
        dkv_ref, dgk_ref = refs[2 + 2 * nl], refs[3 + 2 * nl]
        dk, dv = dk_refs[0][...], dv_refs[0][...]
        for t in range(1, nl):
            dk = dk + dk_refs[t][...]
            dv = dv + dv_refs[t][...]
        dkv, dgk = _mem_kv_bwd(kv_ref[...], dk, dv, gk_ref[...])
        dkv_ref[...] = dkv.astype(BF)

        @pl.when(pl.program_id(0) == 0)
        def _():
            dgk_ref[...] = jnp.zeros_like(dgk_ref)

        dgk_ref[...] += dgk

    row = lambda i: (i, 0)
    one = pl.BlockSpec((1, HEAD), lambda i: (0, 0))
    return _call(
        body, "kv_bwd", (s_len // tm,),
        [pl.BlockSpec((tm, 2 * KVW), row), one] + [pl.BlockSpec((tm, KVW), row)] * (2 * nl),
        [pl.BlockSpec((tm, 2 * KVW), row), one],
        [SDS((s_len, 2 * KVW), BF), SDS((1, HEAD), F32)], semantics=("arbitrary",))(kv, gk, *dks, *dvs)


def _place():
    x, y, c = lax.axis_index("x"), lax.axis_index("y"), lax.axis_index("c")
    flips = [(1 - x, y), (x, 1 - y), (1 - x, 1 - y)]
    return x, y, c, flips


def _remote(src, dst, send_sem, recv_sem, to):
    return pltpu.make_async_remote_copy(src_ref=src, dst_ref=dst, send_sem=send_sem, recv_sem=recv_sem,
                                        device_id=to, device_id_type=MESH)


def _gather_copies(p_refs, wg_refs, send, recv):
    x, y, c, flips = _place()
    chip = 2 * x + y
    cps = []
    for j, (fx, fy) in enumerate(flips):
        for b in range(2):
            half = p_refs[b].shape[0] // 2
            mine = pl.ds(c * half, half)
            cps.append(_remote(p_refs[b].at[mine, :], wg_refs[b].at[chip, mine, :], send.at[2 * j + b],
                               recv.at[2 * j + b], (fx, fy, c)))
    return cps, cps


def _forward_copies(p_refs, wg_refs, send, recv):
    x, y, c, flips = _place()
    chip = 2 * x + y
    sib = (x, y, 1 - c)
    sends, arrivals = [], []
    for b in range(2):
        half = p_refs[b].shape[0] // 2
        own = _remote(p_refs[b], wg_refs[b].at[chip], send.at[b], recv.at[b], sib)
        sends.append(own)
        arrivals.append(own)
        for j, (fx, fy) in enumerate(flips):
            k = 2 + 3 * b + j
            landed = wg_refs[b].at[2 * fx + fy, pl.ds(c * half, half), :]
            other = wg_refs[b].at[2 * fx + fy, pl.ds((1 - c) * half, half), :]
            sends.append(_remote(landed, landed, send.at[k], recv.at[k], sib))
            arrivals.append(_remote(other, other, send.at[k], recv.at[k], sib))
    return sends, arrivals


def _swap_copies(g_refs, r_refs, send, recv):
    x, y, c, _ = _place()
    cps = []
    for b in range(2):
        half = g_refs[b].shape[1] // 2
        cps.append(_remote(g_refs[b].at[:, pl.ds((1 - c) * half, half), :], r_refs[b], send.at[b], recv.at[b],
                           (x, y, 1 - c)))
    return cps, cps


def _split_start(make_copies, n_sems, bufs, after, fresh, *, name):
    def body(a1, a2, b1, b2, after_ref, send, recv, *outs):
        for cp in make_copies((a1, a2), (b1, b2), send, recv)[0]:
            cp.start()
        outs[4][...] = jnp.zeros_like(outs[4])

    extra_shape = () if fresh is None else (pltpu.HBM(fresh, F32),)
    extra_spec = () if fresh is None else (HBM,)
    return pl.pallas_call(
        body, name=name,
        out_shape=(pltpu.SemaphoreType.DMA((n_sems,)), pltpu.SemaphoreType.DMA((n_sems,)))
        + tuple(pltpu.HBM(b.shape, b.dtype) for b in bufs) + (SDS((8, LANE), F32),) + extra_shape,
        in_specs=(HBM, HBM, HBM, HBM, ANY), out_specs=(SEM, SEM, HBM, HBM, HBM, HBM, VMEM_WHOLE) + extra_spec,
        input_output_aliases={0: 2, 1: 3, 2: 4, 3: 5},
        compiler_params=pltpu.CompilerParams(has_side_effects=SIDE_EFFECT))(*[_in_hbm(b) for b in bufs], after)


def _split_wait(make_copies, started, after, *, name):
    send, recv, bufs = started[0], started[1], started[2:6]

    def body(a1, a2, b1, b2, send_ref, recv_ref, after_ref, *outs):
        sends, arrivals = make_copies((a1, a2), (b1, b2), send_ref, recv_ref)
        for cp in arrivals:
            cp.wait_recv()
        for cp in sends:
            cp.wait_send()

    return pl.pallas_call(
        body, name=name, out_shape=tuple(pltpu.HBM(b.shape, b.dtype) for b in bufs),
        in_specs=(HBM, HBM, HBM, HBM, SEM, SEM, ANY), out_specs=(HBM, HBM, HBM, HBM),
        input_output_aliases={0: 0, 1: 1, 2: 2, 3: 3},
        compiler_params=pltpu.CompilerParams(has_side_effects=SIDE_EFFECT))(*bufs, send, recv, after)


def _gather_small(ps):
    def body(ps_ref, o_ref, send, recv):
        x, y, c, flips = _place()
        chip = 2 * x + y
        o_ref[chip] = ps_ref[...]
        cps = [_remote(ps_ref, o_ref.at[chip], send.at[j], recv.at[j], (fx, fy, c))
               for j, (fx, fy) in enumerate(flips)]
        for cp in cps:
            cp.start()
        for j, (fx, fy) in enumerate(flips):
            _remote(ps_ref, o_ref.at[2 * fx + fy], send.at[j], recv.at[j], (fx, fy, c)).wait_recv()
        for cp in cps:
            cp.wait_send()

    return pl.pallas_call(
        body, name="gather_small", in_specs=[VMEM_WHOLE], out_specs=VMEM_WHOLE,
        out_shape=SDS((N_CHIPS,) + ps.shape, ps.dtype),
        scratch_shapes=[pltpu.SemaphoreType.DMA((3,)), pltpu.SemaphoreType.DMA((3,))])(ps)


def _sum_sibling(g, r, place, *, tm, name):
    n_sh, half, w = r.shape
    nt = half // tm

    def body(place_ref, g_ref, r_ref, pbf_ref, own_ref):
        s = pl.program_id(1)
        p = g_ref[0] + r_ref[0]
        pbf_ref[0] = p.astype(BF)

        @pl.when(s == place_ref[1])
        def _():
            own_ref[...] = p

    return _call(
        body, name, (nt, n_sh),
        [pl.BlockSpec((1, tm, w), lambda i, s, pr: (s, pr[0] * nt + i, 0)),
         pl.BlockSpec((1, tm, w), lambda i, s, pr: (s, i, 0))],
        [pl.BlockSpec((1, tm, w), lambda i, s, pr: (s, i, 0)), pl.BlockSpec((tm, w), lambda i, s, pr: (i, 0))],
        [SDS((n_sh, half, w), BF), SDS((half, w), F32)],
        semantics=("arbitrary", "arbitrary"), prefetch=1)(place, g, r)


def _rs_copies(p_refs, land_refs, send, recv):
    _, _, c, flips = _place()
    cps = []
    for j, (fx, fy) in enumerate(flips):
        for b in range(2):
            cps.append(_remote(p_refs[b].at[2 * fx + fy], land_refs[b].at[j], send.at[2 * j + b], recv.at[2 * j + b],
                               (fx, fy, c)))
    return cps, cps


def _sum_chips(own, r, full, layer, place, *, tm, name):
    half, w = own.shape

    def body(place_ref, own_ref, r_ref, _, o_ref):
        o_ref[0, 0] = ((own_ref[...] + r_ref[0].astype(F32)) + r_ref[1].astype(F32)) + r_ref[2].astype(F32)

    return _call(
        body, name, (half // tm,),
        [pl.BlockSpec((tm, w), lambda i, pr: (i, 0)), pl.BlockSpec((3, tm, w), lambda i, pr: (0, i, 0)), ANY],
        pl.BlockSpec((1, 1, tm, w), lambda i, pr: (layer, pr[0], i, 0)), SDS(full.shape, F32),
        semantics=("parallel",), prefetch=1, aliases={3: 0})(place, own, r, full)


def _share_with_sibling(f1, f2, lo, hi, *, name):
    def body(_, __, o1_ref, o2_ref, send, recv):
        x, y, c, _ = _place()
        lay, mine, other = pl.ds(lo, hi - lo), pl.ds(c, 1), pl.ds(1 - c, 1)
        cps = [_remote(o1_ref.at[lay, mine], o1_ref.at[lay, mine], send.at[0], recv.at[0], (x, y, 1 - c)),
               _remote(o2_ref.at[lay, mine], o2_ref.at[lay, mine], send.at[1], recv.at[1], (x, y, 1 - c))]
        for cp in cps:
            cp.start()
        for cp in cps:
            cp.wait_send()
        _remote(o1_ref.at[lay, other], o1_ref.at[lay, other], send.at[0], recv.at[0], (x, y, 1 - c)).wait_recv()
        _remote(o2_ref.at[lay, other], o2_ref.at[lay, other], send.at[1], recv.at[1], (x, y, 1 - c)).wait_recv()

    return pl.pallas_call(
        body, name=name, in_specs=[ANY, ANY], out_specs=[ANY, ANY],
        out_shape=[SDS(f1.shape, f1.dtype), SDS(f2.shape, f2.dtype)], input_output_aliases={0: 0, 1: 1},
        scratch_shapes=[pltpu.SemaphoreType.DMA((2,)), pltpu.SemaphoreType.DMA((2,))])(f1, f2)


def _allreduce_small(sg):
    rows, w = sg.shape
    half = rows // 2
    assert half % 8 == 0

    def body(sg_ref, o_ref, sib_buf, part, slots, send, recv):
        x, y, c, flips = _place()
        chip = 2 * x + y
        sib = (x, y, 1 - c)
        mine = pl.ds(pl.multiple_of(c * half, 8), half)
        other = pl.ds(pl.multiple_of((1 - c) * half, 8), half)
        to_sib = _remote(sg_ref.at[other, :], sib_buf, send.at[0], recv.at[0], sib)
        to_sib.start()
        to_sib.wait_recv()
        part[...] = sg_ref[mine, :] + sib_buf[...]
        slots[chip] = part[...]
        to_chips = [_remote(part, slots.at[chip], send.at[1 + j], recv.at[1 + j], (fx, fy, c))
                    for j, (fx, fy) in enumerate(flips)]
        for cp in to_chips:
            cp.start()
        for j, (fx, fy) in enumerate(flips):
            _remote(part, slots.at[2 * fx + fy], send.at[1 + j], recv.at[1 + j], (fx, fy, c)).wait_recv()
        o_ref[mine, :] = ((slots[0] + slots[1]) + slots[2]) + slots[3]
        back = _remote(o_ref.at[mine, :], o_ref.at[mine, :], send.at[4], recv.at[4], sib)
        back.start()
        _remote(o_ref.at[other, :], o_ref.at[other, :], send.at[4], recv.at[4], sib).wait_recv()
        for cp in [to_sib, back] + to_chips:
            cp.wait_send()

    return pl.pallas_call(
        body, name="allreduce_small", in_specs=[VMEM_WHOLE], out_specs=VMEM_WHOLE, out_shape=SDS((rows, w), F32),
        scratch_shapes=[pltpu.VMEM((half, w), F32), pltpu.VMEM((half, w), F32), pltpu.VMEM((N_CHIPS, half, w), F32),
                        pltpu.SemaphoreType.DMA((5,)), pltpu.SemaphoreType.DMA((5,))])(sg)


def _adamw_math(g, w, m, v):
    mn = ADAM_B1 * m + (1.0 - ADAM_B1) * g
    vn = ADAM_B2 * v + (1.0 - ADAM_B2) * (g * g)
    m_hat = mn / (1.0 - ADAM_B1 ** ADAM_STEP)
    v_hat = vn / (1.0 - ADAM_B2 ** ADAM_STEP)
    return -ADAM_LR * (m_hat / (jnp.sqrt(v_hat) + ADAM_EPS) + ADAM_WD * w), mn, vn


def _adamw_small(gs, ws, ms, vs):
    n = len(gs)

    def body(*refs):
        for k in range(n):
            g, w, m, v = (refs[t * n + k][...] for t in range(4))
            refs[4 * n + k][...], refs[5 * n + k][...], refs[6 * n + k][...] = _adamw_math(g, w, m, v)

    out = pl.pallas_call(
        body, name="adamw_small", in_specs=[VMEM_WHOLE] * (4 * n), out_specs=[VMEM_WHOLE] * (3 * n),
        out_shape=[SDS(w.shape, F32) for w in ws] * 3)(*gs, *ws, *ms, *vs)
    return out[:n], out[n:2 * n], out[2 * n:]


def _adamw(g_arr, layer0, g_off, per_layer, w, m, v, *, name, tm, layers=None, prev=None):
    rows, cols = w.shape
    assert g_off % tm == 0 and per_layer % tm == 0 and rows % per_layer == 0
    npl = per_layer // tm
    lo, hi = layers or (0, rows // per_layer)
    base = lo * npl

    def body(g_ref, w_ref, m_ref, v_ref, *rest):
        go_ref, d_ref, mo_ref, vo_ref = rest[-4:]
        g = g_ref[0]
        go_ref[...] = g
        d_ref[...], mo_ref[...], vo_ref[...] = _adamw_math(g, w_ref[...], m_ref[...], v_ref[...])

    blk = pl.BlockSpec((tm, cols), lambda i: (base + i, 0))
    extra = list(prev or [])
    return _call(
        body, name, ((hi - lo) * npl,),
        [pl.BlockSpec((1, tm, cols), lambda i: (layer0 + lo + i // npl, g_off // tm + i % npl, 0)), blk, blk, blk]
        + [ANY] * len(extra),
        [blk] * 4,
        [SDS((rows, cols), F32)] * 4, semantics=("parallel",),
        aliases={4 + k: k for k in range(len(extra))})(g_arr, w, m, v, *extra)


def _pack_small(parts, width):
    flat = jnp.concatenate([p.reshape(-1).astype(F32) for p in parts])
    rows = -(-flat.shape[0] // (16 * width)) * 16
    return jnp.pad(flat, (0, rows * width - flat.shape[0])).reshape(rows, width)


def _unpack_small(packed, shapes):
    flat = packed.reshape(-1)
    out, off = [], 0
    for shp in shapes:
        size = 1
        for n in shp:
            size *= n
        out.append(flat[off:off + size].reshape(shp))
        off += size
    return out


def _block_diag(pw):
    g, c, _ = pw.shape
    eye = jnp.eye(g, dtype=pw.dtype)
    return (eye[:, None, :, None] * pw[:, :, None, :]).reshape(g * c, g * c)


def _diag_blocks(full, g):
    c = full.shape[0] // g
    return jnp.stack([full[i * c:(i + 1) * c, i * c:(i + 1) * c] for i in range(g)])


def kernel(x, mem, norm_mix, w_in, pool_w, pool_scale, kv_norm, w_kv, k_norm, q_norm, sinks, mem_norm, w_mem_kv, mem_q_norm, mem_k_norm, w_out, norm_mlp, w_up, w_down, loss_target, m_norm_mix, m_w_in, m_pool_w, m_pool_scale, m_kv_norm, m_w_kv, m_k_norm, m_q_norm, m_sinks, m_mem_norm, m_w_mem_kv, m_mem_q_norm, m_mem_k_norm, m_w_out, m_norm_mlp, m_w_up, m_w_down, v_norm_mix, v_w_in, v_pool_w, v_pool_scale, v_kv_norm, v_w_kv, v_k_norm, v_q_norm, v_sinks, v_mem_norm, v_w_mem_kv, v_mem_q_norm, v_mem_k_norm, v_w_out, v_norm_mlp, v_w_up, v_w_down):
    s_len, d = x.shape[1], x.shape[2]
    n_layers, n_pool = norm_mix.shape[0], pool_w.shape[0]
    n_swa = n_layers - n_pool
    main = d - KVW
    qh = main // HEAD
    ff = w_down.shape[1] * N_CHIPS
    dq = d // N_CHIPS
    assert w_up.shape[2] == d and ff == N_CHIPS * d and w_kv.shape[1] == 2 * KVW
    tm = min(512, s_len)
    tmb = min(1024, s_len)
    tm_mem = mem.shape[1]

    cx, cy, cc = lax.axis_index("x"), lax.axis_index("y"), lax.axis_index("c")
    chip = 2 * cx + cy
    place = jnp.stack([cc, chip]).astype(jnp.int32)

    off_down, off_up, off_in, off_out = 0, d, 2 * d, 2 * d + dq
    rows1 = off_out + dq
    off_mkv, off_kv = 0, dq
    rows2 = 2 * dq

    ps = jnp.pad(pool_scale, ((0, 8 - n_pool), (0, 2 * LANE - pool_scale.shape[1])))

    def packed_weights(l):
        p1 = jnp.concatenate([w_down[l], w_up[l], w_in[l], w_out[l]]).astype(BF)
        p2 = jnp.concatenate([w_mem_kv[l], w_kv] if l == n_pool else [w_mem_kv[l]]).astype(BF)
        return p1, p2

    def gather_start(l, after):
        p1, p2 = packed_weights(l)
        bufs = (p1, p2, lax.empty((N_CHIPS,) + p1.shape, BF), lax.empty((N_CHIPS,) + p2.shape, BF))
        return _split_start(_gather_copies, 6, bufs, after, None, name=f"gather_start_{l}")

    def gather_land(l, started, after):
        bufs = _split_wait(_gather_copies, started, after, name=f"gather_wait_{l}")
        return _split_start(_forward_copies, 8, bufs, place, None, name=f"forward_start_{l}")

    def gather_finish(l, forwarding, after):
        bufs = _split_wait(_forward_copies, forwarding, after, name=f"forward_wait_{l}")
        return bufs[2], bufs[3]

    def w_rows(arr, off, nrows, width):
        assert off % nrows == 0
        return (arr, (N_CHIPS, nrows, width), lambda j: (0, off // nrows, 0))

    row = lambda a: a.reshape(1, -1)
    h = x.reshape(s_len, d)
    memx = mem.reshape(tm_mem, d)
    tgt = loss_target.reshape(s_len, d)
    pbd = [_block_diag(pool_w[l]).astype(BF) for l in range(n_pool)]
    sinks_pad = [jnp.pad(row(sinks[j]), ((0, 0), (0, LANE - qh))) for j in range(n_swa)]

    w_in_l, w_out_l, w_down_l, w_up_all_l, w_mkv_l = [], [], [], [], []
    w_kv_g = None
    first = gather_start(0, place)
    psg = _gather_small(ps + first[6][0, 0])
    pool_scale_full = jnp.concatenate([psg[k, :n_pool, :pool_scale.shape[1]] for k in range(N_CHIPS)], axis=1)
    forwarding = gather_land(0, first, psg)
    travelling = gather_start(1, forwarding[6]) if n_layers > 1 else None
    saved, probs, sink_probs, mem_probs = [], {}, {}, {}
    kv = hn_kv = kn = vsh = None
    for l in range(n_layers):
        wg1, wg2 = gather_finish(l, forwarding, h if l else forwarding[6])
        w_in_l.append(w_rows(wg1, off_in, dq, d))
        w_out_l.append(w_rows(wg1, off_out, dq, d))
        w_down_l.append(w_rows(wg1, off_down, d, d))
        w_up_all_l.append((wg1, (N_CHIPS, d, d), lambda j: (0, off_up // d, 0)))
        w_mkv_l.append(w_rows(wg2, off_mkv, dq, 2 * KVW))
        g_mix = row(norm_mix[l])
        if travelling is not None:
            g_mix = g_mix + travelling[6][0, 0]
        if l == n_pool:
            w_kv_g = w_rows(wg2, off_kv, dq, 2 * KVW)
            kv, hn_kv = _norm_mm(h, row(kv_norm), w_kv_g, 1, 2 * KVW, act=False, name="kv_proj", tm=tmb)
            kn, vsh = _kv_prep(kv, row(k_norm), tm=tmb)
        h0 = h
        proj, xn = _norm_mm(h0, g_mix, w_in_l[l], 1, d, act=False, name=f"in_proj_{l}", tm=tmb)
        mkv, memn = _norm_mm(memx, row(mem_norm[l]), w_mkv_l[l], 1, 2 * KVW, act=False, name=f"mem_kv_{l}", tm=tm_mem)
        if l < n_pool:
            cat, mem_probs[l] = _mixer_pool_fwd(proj, mkv, pbd[l], row(pool_scale_full[l]), row(mem_q_norm[l]),
                                                row(mem_k_norm[l]), name=f"mixer_fwd_{l}", tm=tmb)
        else:
            j = l - n_pool
            cat, probs[l], sink_probs[l] = _mixer_swa_fwd(proj, kn, vsh, row(q_norm[j]), sinks_pad[j],
                                                          name=f"mixer_fwd_{l}")
            cat, mem_probs[l] = _mem_attn_fwd(proj, mkv, row(mem_q_norm[l]), row(mem_k_norm[l]), cat,
                                              name=f"mem_attn_fwd_{l}", tm=tmb)
        h1 = _mm_res(h0, cat, w_out_l[l], name=f"out_proj_{l}", tm=tmb)
        hh, xm = _norm_mm(h1, row(norm_mlp[l]), w_up_all_l[l], N_CHIPS, d, act=True, name=f"mlp_up_{l}", tm=tmb)
        after = None
        if travelling is not None:
            forwarding = gather_land(l + 1, travelling, hh)
            travelling = gather_start(l + 2, forwarding[6]) if l + 2 < n_layers else None
            after = forwarding[6]
        h = _mm_res(h1, hh, w_down_l[l], name=f"mlp_down_{l}", tm=tm, after=after)
        saved.append((h0, proj, xn, mkv, memn, cat, h1, hh, xm))

    dh, dh_bf, loss_part = _loss_head(h, tgt, tm=tmb)

    half1, half2 = rows1 // 2, rows2 // 2
    g1 = lax.empty((N_CHIPS, rows1, d), F32)
    pending = {}
    swapping = None
    tk = min(512, d)

    def reduce_begin(l, swapped, after):
        g1_l, g2_l, r1, r2 = _split_wait(_swap_copies, swapped, after, name=f"swap_wait_{l}")
        pb1, own1 = _sum_sibling(g1_l, r1, place, tm=_tile(half1, 640), name=f"sum_sibling_a_{l}")
        pb2, own2 = _sum_sibling(g2_l, r2, place, tm=_tile(half2, 256), name=f"sum_sibling_b_{l}")
        bufs = (pb1, pb2, lax.empty((3, half1, d), BF), lax.empty((3, half2, 2 * KVW), BF))
        return _split_start(_rs_copies, 6, bufs, place, None, name=f"reduce_start_{l}"), own1, own2
    zeros_mem = jnp.zeros((tm_mem, d), F32)

    def rows_map(off, nrows, tkk):
        per = nrows // tkk
        return lambda i, j: (i // per, off // tkk + i % per, 0)

    def cols_map(off, tkk):
        return lambda i, j: (j, off // tkk + i, 0)

    d_norm_mix, d_norm_mlp, d_mem_norm = [None] * n_layers, [None] * n_layers, [None] * n_layers
    d_mem_q, d_mem_k = [None] * n_layers, [None] * n_layers
    d_pool_w, d_pool_scale = [None] * n_pool, [None] * n_pool
    d_q_norm, d_sinks = [None] * n_swa, [None] * n_swa
    dks, dvs = [], []
    d_kv_norm = d_k_norm = None
    for l in reversed(range(n_layers)):
        h0, proj, xn, mkv, memn, cat, h1, hh, xm = saved[l]
        g2 = jnp.zeros((N_CHIPS, rows2, 2 * KVW), F32)
        g1 = _mm_tn(hh, dh_bf, g1, rows_map(off_down, d, tk), tk, d, name=f"dw_down_{l}")
        du = _mm_nt_relu2(dh_bf, hh, w_down_l[l], N_CHIPS, name=f"d_mlp_act_{l}", tm=tmb)
        g1 = _mm_tn(xm, du, g1, cols_map(off_up, tk), tk, d, name=f"dw_up_{l}")
        g_mlp = row(norm_mlp[l])
        if swapping is not None:
            pending[swapping[0]] = reduce_begin(*swapping, after=g1)
            g_mlp = g_mlp + pending[swapping[0]][0][6][0, 0]
        dh1, dh1_bf, d_norm_mlp[l] = _mm_nt_normbwd(du, w_up_all_l[l], N_CHIPS, h1, g_mlp, dh,
                                                    name=f"d_mlp_in_{l}", tm=tm)
        tkq = min(tk, dq)
        g1 = _mm_tn(cat, dh1_bf, g1, rows_map(off_out, dq, tkq), tkq, d, name=f"dw_out_{l}")
        dcat = _mm_nt(dh1_bf, w_out_l[l], d, name=f"d_cat_{l}", tm=tmb)
        if l < n_pool:
            dproj, dpbd, dscale, dmkv, d_mem_q[l], d_mem_k[l] = _mixer_pool_bwd(
                proj, dcat, mkv, pbd[l], row(pool_scale_full[l]), row(mem_q_norm[l]), row(mem_k_norm[l]),
                mem_probs[l], name=f"mixer_bwd_{l}", tm=tmb)
            d_pool_w[l] = _diag_blocks(dpbd, len(POOL_WINDOWS))
            d_pool_scale[l] = dscale
        else:
            j = l - n_pool
            dproj, dk, dv, d_q_norm[j], dsk = _mixer_swa_bwd(proj, dcat, kn, vsh, row(q_norm[j]), probs[l],
                                                             sink_probs[l], name=f"mixer_bwd_{l}")
            dproj, dmkv, d_mem_q[l], d_mem_k[l] = _mem_attn_bwd(
                proj, dcat, mkv, row(mem_q_norm[l]), row(mem_k_norm[l]), mem_probs[l], dproj,
                name=f"mem_attn_bwd_{l}", tm=tmb)
            d_sinks[j] = dsk[0, :qh]
            dks.append(dk)
            dvs.append(dv)
        g1 = _mm_tn(xn, dproj, g1, rows_map(off_in, dq, tkq), tkq, d, name=f"dw_in_{l}")
        dh, dh_bf, d_norm_mix[l] = _mm_nt_normbwd(dproj, w_in_l[l], 1, h0, row(norm_mix[l]), dh1,
                                                  name=f"d_in_{l}", tm=tmb)
        g2 = _mm_tn(memn, dmkv, g2, rows_map(off_mkv, dq, tkq), tkq, 2 * KVW, name=f"dw_mem_kv_{l}")
        _, _, d_mem_norm[l] = _mm_nt_normbwd(dmkv, w_mkv_l[l], 1, memx, row(mem_norm[l]), zeros_mem,
                                             name=f"d_mem_norm_{l}", tm=tm_mem)
        if l == n_pool:
            dkv, d_k_norm = _kv_bwd(kv, dks, dvs, row(k_norm), tm=tmb)
            g2 = _mm_tn(hn_kv, dkv, g2, rows_map(off_kv, dq, tkq), tkq, 2 * KVW, name="dw_kv")
            dh, dh_bf, d_kv_norm = _mm_nt_normbwd(dkv, w_kv_g, 1, h0, row(kv_norm), dh, name="d_kv_in", tm=tmb)
        bufs = (g1, g2, lax.empty((N_CHIPS, half1, d), F32), lax.empty((N_CHIPS, half2, 2 * KVW), F32))
        swapping = (l, _split_start(_swap_copies, 2, bufs, place, (N_CHIPS, rows1, d) if l > 0 else None,
                                    name=f"swap_start_{l}"))
        g1 = swapping[1][7] if l > 0 else None
    grad_x = dh.reshape(x.shape)

    small_names = ["norm_mix", "pool_w", "pool_scale", "kv_norm", "k_norm", "q_norm", "sinks", "mem_norm",
                   "mem_q_norm", "mem_k_norm", "norm_mlp"]
    small_grads = {
        "norm_mix": jnp.concatenate(d_norm_mix), "pool_w": jnp.stack(d_pool_w),
        "pool_scale": jnp.concatenate(d_pool_scale), "kv_norm": d_kv_norm[0], "k_norm": d_k_norm[0],
        "q_norm": jnp.concatenate(d_q_norm), "sinks": jnp.stack(d_sinks), "mem_norm": jnp.concatenate(d_mem_norm),
        "mem_q_norm": jnp.concatenate(d_mem_q), "mem_k_norm": jnp.concatenate(d_mem_k),
        "norm_mlp": jnp.concatenate(d_norm_mlp)}
    width = d
    sg = _pack_small([small_grads[n] for n in small_names] + [loss_part], width)
    sg = sg + swapping[1][6][0, 0]
    sg = _allreduce_small(sg)
    pending[0] = reduce_begin(*swapping, after=sg)
    *unpacked, loss_sum = _unpack_small(sg, [small_grads[n].shape for n in small_names] + [(1, 1)])
    loss = loss_sum[0, 0]
    reduced = dict(zip(small_names, unpacked))
    psw = pool_scale.shape[1]
    reduced["pool_scale"] = lax.dynamic_slice_in_dim(reduced["pool_scale"], chip * psw, psw, axis=1)
    params = dict(norm_mix=(norm_mix, m_norm_mix, v_norm_mix), pool_w=(pool_w, m_pool_w, v_pool_w),
                  pool_scale=(pool_scale, m_pool_scale, v_pool_scale), kv_norm=(kv_norm, m_kv_norm, v_kv_norm),
                  k_norm=(k_norm, m_k_norm, v_k_norm), q_norm=(q_norm, m_q_norm, v_q_norm),
                  sinks=(sinks, m_sinks, v_sinks), mem_norm=(mem_norm, m_mem_norm, v_mem_norm),
                  mem_q_norm=(mem_q_norm, m_mem_q_norm, v_mem_q_norm),
                  mem_k_norm=(mem_k_norm, m_mem_k_norm, v_mem_k_norm), norm_mlp=(norm_mlp, m_norm_mlp, v_norm_mlp))
    flat2 = lambda a: a.reshape(-1, a.shape[-1])
    grads = [flat2(reduced[n].reshape(params[n][0].shape)) for n in small_names]
    grads[0] = grads[0] + pending[0][0][6][0, 0]
    res = _adamw_small(grads, *[[flat2(params[n][t]) for n in small_names] for t in range(3)])
    small = {n: [grads[k].reshape(params[n][0].shape)] + [r[k].reshape(params[n][0].shape) for r in res]
             for k, n in enumerate(small_names)}

    full1 = lax.empty((n_layers, 2, half1, d), F32)
    full2 = lax.empty((n_layers, 2, half2, 2 * KVW), F32)
    kinds = (("w_down", 1, off_down, d, w_down, m_w_down, v_w_down), ("w_up", 1, off_up, d, w_up, m_w_up, v_w_up),
             ("w_in", 1, off_in, dq, w_in, m_w_in, v_w_in), ("w_out", 1, off_out, dq, w_out, m_w_out, v_w_out),
             ("w_mem_kv", 2, off_mkv, dq, w_mem_kv, m_w_mem_kv, v_w_mem_kv))
    big = {}
    first = 1 if n_layers > 1 and n_pool > 0 else 0
    after = res[0][0]
    for lo, hi in ((first, n_layers), (0, first)):
        if lo == hi:
            continue
        for l in reversed(range(lo, hi)):
            exchange, own1, own2 = pending[l]
            _, _, x1, x2 = _split_wait(_rs_copies, exchange, after, name=f"reduce_wait_{l}")
            full1 = _sum_chips(own1, x1, full1, l, place, tm=_tile(half1, 640), name=f"sum_chips_a_{l}")
            full2 = _sum_chips(own2, x2, full2, l, place, tm=_tile(half2, 256), name=f"sum_chips_b_{l}")
        full1, full2 = _share_with_sibling(full1, full2, lo, hi, name=f"share_with_sibling_{lo}")
        views = (None, full1.reshape(n_layers, rows1, d), full2.reshape(n_layers, rows2, 2 * KVW))
        for name, which, off, per, w_, m_, v_ in kinds:
            cols = views[which].shape[2]
            big[name] = _adamw(views[which], 0, off, per, w_.reshape(-1, cols), m_.reshape(-1, cols),
                               v_.reshape(-1, cols), name=f"adamw_{name}_{lo}", tm=min(512, per), layers=(lo, hi),
                               prev=big.get(name))
        if lo <= n_pool < hi:
            big["w_kv"] = _adamw(views[2], n_pool, off_kv, dq, w_kv, m_w_kv, v_w_kv, name="adamw_w_kv",
                                 tm=min(256, dq))
        after = big["w_mem_kv"][1]
    shapes_big = dict(w_down=w_down.shape, w_up=w_up.shape, w_in=w_in.shape, w_out=w_out.shape,
                      w_mem_kv=w_mem_kv.shape, w_kv=w_kv.shape)
    big = {n: [r.reshape(shapes_big[n]) for r in big[n]] for n in big}

    order = ["norm_mix", "w_in", "pool_w", "pool_scale", "kv_norm", "w_kv", "k_norm", "q_norm", "sinks", "mem_norm",
             "w_mem_kv", "mem_q_norm", "mem_k_norm", "w_out", "norm_mlp", "w_up", "w_down"]
    out = {**big, **small}
    return (loss, grad_x, *[out[n][0] for n in order], *[out[n][1] for n in order],
            *[out[n][2] for n in order], *[out[n][3] for n in order])
```
